```python
import jax
import jax.numpy as jnp
from jax import lax
import numpy as np

D_MODEL = 2048
BATCH = 8
SEQ = 4096
DEPTH = 2

GRID_W = 64
CTX_LEN = 256
HEAD_DIM = 128
BLOCK = 128
NA_HEADS = 4
NA_WIN_R = 8
NA_WIN_C = 16
SWA_HEADS = 4
SWA_KV_HEADS = 2
SWA_WINDOW = 128
MLA_HEADS = 4
MLA_Q_LORA = 384
MLA_KV_LORA = 128
MLA_NOPE = 128
MLA_ROPE = 64
MLA_V = 128
GQA_HEADS = 4
GQA_KV_HEADS = 2
MIX_WIDTH = (NA_HEADS + SWA_HEADS + GQA_HEADS) * HEAD_DIM + MLA_HEADS * MLA_V
D_FF = 5632
CONV_W = 3
ROPE_THETA = 10000.0
EPS = 1e-6
NEG = -1e30
DEEPNORM_ALPHA = (2 * DEPTH) ** 0.25
DEEPNORM_BETA = (8 * DEPTH) ** -0.25
IN_SIZES = (
    NA_HEADS * HEAD_DIM, NA_HEADS * HEAD_DIM, NA_HEADS * HEAD_DIM,
    SWA_HEADS * HEAD_DIM, SWA_KV_HEADS * HEAD_DIM, SWA_KV_HEADS * HEAD_DIM,
    MLA_Q_LORA, MLA_KV_LORA, MLA_ROPE,
    GQA_HEADS * HEAD_DIM, GQA_KV_HEADS * HEAD_DIM, GQA_KV_HEADS * HEAD_DIM,
)
IN_COLS = sum(IN_SIZES)

kernel_name = 'hybrid_dit_parallel_heads_deepnorm'


def layer_norm(x, g=None, b=None):
    xf = x.astype(jnp.float32)
    mu = jnp.mean(xf, axis=-1, keepdims=True)
    var = jnp.mean(jnp.square(xf - mu), axis=-1, keepdims=True)
    y = (xf - mu) * lax.rsqrt(var + EPS)
    if g is not None:
        y = y * g + b
    return y.astype(x.dtype)


def rms_norm(x, g):
    xf = x.astype(jnp.float32)
    y = xf * lax.rsqrt(jnp.mean(xf * xf, axis=-1, keepdims=True) + EPS) * g
    return y.astype(x.dtype)


def modulate(x, shift, scale):
    return layer_norm(x) * (1 + scale) + shift


def rope_1d(x, pos):
    half = x.shape[-1] // 2
    inv_freq = ROPE_THETA ** (-jnp.arange(half, dtype=jnp.float32) / half)
    ang = pos.astype(jnp.float32)[:, None] * inv_freq[None, :]
    cos, sin = jnp.cos(ang), jnp.sin(ang)
    xf = x.astype(jnp.float32)
    x1, x2 = xf[..., :half], xf[..., half:]
    return jnp.concatenate([x1 * cos - x2 * sin, x2 * cos + x1 * sin], axis=-1).astype(x.dtype)


def rope_2d(x, row, col):
    h = x.shape[-1] // 2
    return jnp.concatenate([rope_1d(x[..., :h], row), rope_1d(x[..., h:], col)], axis=-1)


def split_cols(p):
    out, o = [], 0
    for n in IN_SIZES:
        out.append(p[..., o:o + n])
        o += n
    return out


def to_heads(t, n):
    B, S, _ = t.shape
    return t.reshape(B, S, n, -1).transpose(0, 2, 1, 3)


def from_heads(o):
    B, n, S, d = o.shape
    return o.transpose(0, 2, 1, 3).reshape(B, S, n * d)


def full_attention(q, k, v, scale, sink=None):
    s = jnp.einsum('bhgqd,bhkd->bhgqk', q, k, preferred_element_type=jnp.float32) * scale
    if sink is not None:
        s_sink = jnp.broadcast_to(sink.astype(jnp.float32)[None, :, :, None, None], s.shape[:-1] + (1,))
        s = jnp.concatenate([s, s_sink], axis=-1)
    p = jax.nn.softmax(s, axis=-1)[..., :k.shape[2]]
    return jnp.einsum('bhgqk,bhkd->bhgqd', p.astype(v.dtype), v)


def mixer_neighbourhood(px, pc, rpb, need_ctx):
    q, k, v = (to_heads(t, NA_HEADS) for t in px)
    kc, vc = to_heads(pc[1], NA_HEADS), to_heads(pc[2], NA_HEADS)
    scale = HEAD_DIM ** -0.5
    B, H, S, d = q.shape
    rows = S // GRID_W
    wr = min(NA_WIN_R, rows)
    qg = q.reshape(B, H, rows, GRID_W, d)
    kg = k.reshape(B, H, rows, GRID_W, d)
    vg = v.reshape(B, H, rows, GRID_W, d)
    r = jnp.arange(rows)
    r0 = jnp.clip(r - wr // 2, 0, rows - wr)
    krow = r0[:, None] + jnp.arange(wr)[None, :]
    k_band = kg[:, :, krow]
    v_band = vg[:, :, krow]
    cq = jnp.arange(GRID_W)
    c0 = jnp.clip(cq - NA_WIN_C // 2, 0, GRID_W - NA_WIN_C)
    col_in = (cq[None, :] >= c0[:, None]) & (cq[None, :] < c0[:, None] + NA_WIN_C)
    drow_idx = krow - r[:, None] + NA_WIN_R - 1
    dcol_idx = jnp.clip(cq[None, :] - cq[:, None] + NA_WIN_C - 1, 0, 2 * NA_WIN_C - 2)
    bias = rpb[:, drow_idx[:, None, :, None], dcol_idx[None, :, None, :]]
    s = jnp.einsum('bhrqd,bhrwkd->bhrqwk', qg, k_band, preferred_element_type=jnp.float32) * scale
    s = jnp.where(col_in[:, None, :], s + bias[None], NEG)
    nwin = wr * GRID_W
    s = s.reshape(B, H, rows, GRID_W, nwin)
    s_ctx = jnp.einsum('bhrqd,bhcd->bhrqc', qg, kc, preferred_element_type=jnp.float32) * scale
    p = jax.nn.softmax(jnp.concatenate([s, s_ctx], axis=-1), axis=-1)
    o = (jnp.einsum('bhrqn,bhrnd->bhrqd', p[..., :nwin].astype(v.dtype), v_band.reshape(B, H, rows, nwin, d))
         + jnp.einsum('bhrqc,bhcd->bhrqd', p[..., nwin:].astype(v.dtype), vc))
    out_x = from_heads(o.reshape(B, H, S, d))
    out_c = None
    if need_ctx:
        qc = to_heads(pc[0], NA_HEADS)
        out_c = from_heads(full_attention(qc[:, :, None], kc, vc, scale)[:, :, 0])
    return out_x, out_c


def mixer_sliding(px, pc, row, col, sink, need_ctx):
    G = SWA_HEADS // SWA_KV_HEADS
    q = rope_2d(to_heads(px[0], SWA_HEADS), row, col)
    k = rope_2d(to_heads(px[1], SWA_KV_HEADS), row, col)
    v = to_heads(px[2], SWA_KV_HEADS)
    kc, vc = to_heads(pc[1], SWA_KV_HEADS), to_heads(pc[2], SWA_KV_HEADS)
    sink = sink.reshape(SWA_KV_HEADS, G)
    scale = HEAD_DIM ** -0.5
    B, _, S, d = q.shape
    nb = S // BLOCK
    qb = q.reshape(B, SWA_KV_HEADS, G, nb, BLOCK, d)
    pad = ((0, 0), (0, 0), (BLOCK, BLOCK), (0, 0))
    idx = jnp.arange(nb)[:, None] * BLOCK + jnp.arange(3 * BLOCK)[None, :]
    kb = jnp.pad(k, pad)[:, :, idx]
    vb = jnp.pad(v, pad)[:, :, idx]
    kpos = idx - BLOCK
    qpos = jnp.arange(S).reshape(nb, BLOCK)
    valid = ((jnp.abs(kpos[:, None, :] - qpos[:, :, None]) <= SWA_WINDOW)
             & (kpos >= 0)[:, None, :] & (kpos < S)[:, None, :])
    s = jnp.einsum('bhgnqd,bhnkd->bhgnqk', qb, kb, preferred_element_type=jnp.float32) * scale
    s = jnp.where(valid, s, NEG)
    s_ctx = jnp.einsum('bhgnqd,bhcd->bhgnqc', qb, kc, preferred_element_type=jnp.float32) * scale
    s_sink = jnp.broadcast_to(sink.astype(jnp.float32)[None, :, :, None, None, None], s.shape[:-1] + (1,))
    p = jax.nn.softmax(jnp.concatenate([s, s_ctx, s_sink], axis=-1), axis=-1)
    nk, C = 3 * BLOCK, kc.shape[2]
    o = (jnp.einsum('bhgnqk,bhnkd->bhgnqd', p[..., :nk].astype(v.dtype), vb)
         + jnp.einsum('bhgnqc,bhcd->bhgnqd', p[..., nk:nk + C].astype(v.dtype), vc))
    out_x = from_heads(o.reshape(B, SWA_HEADS, S, d))
    out_c = None
    if need_ctx:
        qc = to_heads(pc[0], SWA_HEADS)
        qc = qc.reshape(B, SWA_KV_HEADS, G, qc.shape[2], d)
        oc = full_attention(qc, kc, vc, scale, sink)
        out_c = from_heads(oc.reshape(B, SWA_HEADS, qc.shape[3], d))
    return out_x, out_c


def mixer_mla(px, pc, row, col, q_norm, kv_norm, w_uq, w_ukv, need_ctx):
    def proj_q(cq):
        qh = to_heads(rms_norm(cq, q_norm) @ w_uq, MLA_HEADS)
        return qh[..., :MLA_NOPE], qh[..., MLA_NOPE:]

    def proj_kv(ckv):
        kvh = to_heads(rms_norm(ckv, kv_norm) @ w_ukv, MLA_HEADS)
        return kvh[..., :MLA_NOPE], kvh[..., MLA_NOPE:]

    scale = (MLA_NOPE + MLA_ROPE) ** -0.5

    def attend(qn, qpe, kn, kpe, v):
        s = (jnp.einsum('bhqd,bhkd->bhqk', qn, kn, preferred_element_type=jnp.float32)
             + jnp.einsum('bhqr,bkr->bhqk', qpe, kpe, preferred_element_type=jnp.float32)) * scale
        p = jax.nn.softmax(s, axis=-1)
        return jnp.einsum('bhqk,bhkd->bhqd', p.astype(v.dtype), v)

    qn, qpe = proj_q(px[0])
    qpe = rope_2d(qpe, row, col)
    kn, v = proj_kv(px[1])
    kpe = rope_2d(px[2], row, col)
    kn_c, v_c = proj_kv(pc[1])
    kpe_c = pc[2]
    kn_all = jnp.concatenate([kn, kn_c], axis=2)
    kpe_all = jnp.concatenate([kpe, kpe_c], axis=1)
    v_all = jnp.concatenate([v, v_c], axis=2)
    B, H, S, _ = qn.shape
    nb = S // BLOCK
    blocks = lambda t: jnp.moveaxis(t.reshape(B, H, nb, BLOCK, t.shape[-1]), 2, 0)
    o = lax.map(lambda qs: attend(qs[0], qs[1], kn_all, kpe_all, v_all), (blocks(qn), blocks(qpe)))
    out_x = from_heads(jnp.moveaxis(o, 0, 2).reshape(B, H, S, MLA_V))
    out_c = None
    if need_ctx:
        qn_c, qpe_c = proj_q(pc[0])
        out_c = from_heads(attend(qn_c, qpe_c, kn_c, kpe_c, v_c))
    return out_x, out_c


def mixer_gqa(px, pc, row, col, q_norm, k_norm, need_ctx):
    G = GQA_HEADS // GQA_KV_HEADS
    q = rope_2d(rms_norm(to_heads(px[0], GQA_HEADS), q_norm), row, col)
    k = rope_2d(rms_norm(to_heads(px[1], GQA_KV_HEADS), k_norm), row, col)
    v = to_heads(px[2], GQA_KV_HEADS)
    kc = rms_norm(to_heads(pc[1], GQA_KV_HEADS), k_norm)
    vc = to_heads(pc[2], GQA_KV_HEADS)
    k_all = jnp.concatenate([k, kc], axis=2)
    v_all = jnp.concatenate([v, vc], axis=2)
    scale = HEAD_DIM ** -0.5
    B, _, S, d = q.shape
    nb = S // BLOCK
    qb = jnp.moveaxis(q.reshape(B, GQA_KV_HEADS, G, nb, BLOCK, d), 3, 0)
    o = lax.map(lambda qi: full_attention(qi, k_all, v_all, scale), qb)
    out_x = from_heads(jnp.moveaxis(o, 0, 3).reshape(B, GQA_HEADS, S, d))
    out_c = None
    if need_ctx:
        qc = rms_norm(to_heads(pc[0], GQA_HEADS), q_norm)
        C = qc.shape[2]
        oc = full_attention(qc.reshape(B, GQA_KV_HEADS, G, C, d), kc, vc, scale)
        out_c = from_heads(oc.reshape(B, GQA_HEADS, C, d))
    return out_x, out_c


def depthwise_conv(h, w, b):
    S = h.shape[1]
    hp = jnp.pad(h, ((0, 0), (CONV_W // 2, CONV_W // 2), (0, 0)))
    return hp[:, 0:S] * w[0] + hp[:, 1:S + 1] * w[1] + hp[:, 2:S + 2] * w[2] + b


def conv_ffn(h, w_gate, w_up, conv_w, conv_b, w_down):
    a = depthwise_conv(h @ w_gate, conv_w, conv_b)
    return (jax.nn.silu(a) * (h @ w_up)) @ w_down


def _fwd_setup_inputs(seed: int = 0) -> dict:
    key = jax.random.key(seed)
    ks = jax.random.split(key, 25)
    L, D = DEPTH, D_MODEL

    def nrm(k, shape, s):
        return jax.random.normal(k, shape, jnp.float32) * s

    return {
        'x': nrm(ks[0], (BATCH, SEQ, D), 1.0),
        'c': nrm(ks[1], (BATCH, D), 1.0),
        'ctx': nrm(ks[2], (BATCH, CTX_LEN, D), 1.0),
        'c_ctx': nrm(ks[3], (D,), 1.0),
        'w_ada': nrm(ks[4], (L, D, 6 * D), D ** -0.5),
        'b_ada': nrm(ks[5], (L, 6 * D), 0.02),
        'w_in': nrm(ks[6], (L, D, IN_COLS), D ** -0.5),
        'na_rpb': nrm(ks[7], (L, NA_HEADS, 2 * NA_WIN_R - 1, 2 * NA_WIN_C - 1), 0.5),
        'swa_sink': nrm(ks[8], (L, SWA_HEADS), 0.5),
        'mla_q_norm': 1.0 + nrm(ks[9], (L, MLA_Q_LORA), 0.05),
        'mla_kv_norm': 1.0 + nrm(ks[10], (L, MLA_KV_LORA), 0.05),
        'mla_w_uq': nrm(ks[11], (L, MLA_Q_LORA, MLA_HEADS * (MLA_NOPE + MLA_ROPE)), MLA_Q_LORA ** -0.5),
        'mla_w_ukv': nrm(ks[12], (L, MLA_KV_LORA, MLA_HEADS * (MLA_NOPE + MLA_V)), MLA_KV_LORA ** -0.5),
        'gqa_q_norm': 1.0 + nrm(ks[13], (L, HEAD_DIM), 0.05),
        'gqa_k_norm': 1.0 + nrm(ks[14], (L, HEAD_DIM), 0.05),
        'w_out': nrm(ks[15], (L, MIX_WIDTH, D), DEEPNORM_BETA * MIX_WIDTH ** -0.5),
        'ln1_g': 1.0 + nrm(ks[16], (L, D), 0.05),
        'ln1_b': nrm(ks[17], (L, D), 0.02),
        'ffn_w_gate': nrm(ks[18], (L, D, D_FF), D ** -0.5),
        'ffn_w_up': nrm(ks[19], (L, D, D_FF), D ** -0.5),
        'ffn_conv_w': nrm(ks[20], (L, CONV_W, D_FF), CONV_W ** -0.5),
        'ffn_conv_b': nrm(ks[21], (L, D_FF), 0.02),
        'ffn_w_down': nrm(ks[22], (L, D_FF, D), DEEPNORM_BETA * D_FF ** -0.5),
        'ln2_g': 1.0 + nrm(ks[23], (L, D), 0.05),
        'ln2_b': nrm(ks[24], (L, D), 0.02),
    }


def _fwd_reference(x, c, ctx, c_ctx, w_ada, b_ada, w_in, na_rpb, swa_sink, mla_q_norm, mla_kv_norm,
              mla_w_uq, mla_w_ukv, gqa_q_norm, gqa_k_norm, w_out, ln1_g, ln1_b,
              ffn_w_gate, ffn_w_up, ffn_conv_w, ffn_conv_b, ffn_w_down, ln2_g, ln2_b):
    t = jnp.arange(x.shape[1])
    row, col = t // GRID_W, t % GRID_W
    for i in range(DEPTH):
        need_ctx = i < DEPTH - 1
        mod_x = jnp.split((jax.nn.silu(c) @ w_ada[i] + b_ada[i])[:, None, :], 6, axis=-1)
        mod_c = jnp.split(jax.nn.silu(c_ctx) @ w_ada[i] + b_ada[i], 6, axis=-1)

        px = split_cols(modulate(x, mod_x[0], mod_x[1]) @ w_in[i])
        pc = split_cols(modulate(ctx, mod_c[0], mod_c[1]) @ w_in[i])
        oa_x, oa_c = mixer_neighbourhood(px[0:3], pc[0:3], na_rpb[i], need_ctx)
        ob_x, ob_c = mixer_sliding(px[3:6], pc[3:6], row, col, swa_sink[i], need_ctx)
        oc_x, oc_c = mixer_mla(px[6:9], pc[6:9], row, col, mla_q_norm[i], mla_kv_norm[i],
                               mla_w_uq[i], mla_w_ukv[i], need_ctx)
        od_x, od_c = mixer_gqa(px[9:12], pc[9:12], row, col, gqa_q_norm[i], gqa_k_norm[i], need_ctx)
        mix_x = jnp.concatenate([oa_x, ob_x, oc_x, od_x], axis=-1)
        x = layer_norm(DEEPNORM_ALPHA * x + mod_x[2] * (mix_x @ w_out[i]), ln1_g[i], ln1_b[i])
        if need_ctx:
            mix_c = jnp.concatenate([oa_c, ob_c, oc_c, od_c], axis=-1)
            ctx = layer_norm(DEEPNORM_ALPHA * ctx + mod_c[2] * (mix_c @ w_out[i]), ln1_g[i], ln1_b[i])

        ffn_args = (ffn_w_gate[i], ffn_w_up[i], ffn_conv_w[i], ffn_conv_b[i], ffn_w_down[i])
        x = layer_norm(DEEPNORM_ALPHA * x + mod_x[5] * conv_ffn(modulate(x, mod_x[3], mod_x[4]), *ffn_args),
                       ln2_g[i], ln2_b[i])
        if need_ctx:
            ctx = layer_norm(DEEPNORM_ALPHA * ctx + mod_c[5] * conv_ffn(modulate(ctx, mod_c[3], mod_c[4]), *ffn_args),
                             ln2_g[i], ln2_b[i])
    return x


import jax as _jax
import jax.numpy as _jnp

TWIN_FORMAT = 'train_step'
FWD_PARAMS = ['x', 'c', 'ctx', 'c_ctx', 'w_ada', 'b_ada', 'w_in', 'na_rpb', 'swa_sink', 'mla_q_norm', 'mla_kv_norm', 'mla_w_uq', 'mla_w_ukv', 'gqa_q_norm', 'gqa_k_norm', 'w_out', 'ln1_g', 'ln1_b', 'ffn_w_gate', 'ffn_w_up', 'ffn_conv_w', 'ffn_conv_b', 'ffn_w_down', 'ln2_g', 'ln2_b']
TWIN_WEIGHTS = ['c_ctx', 'w_ada', 'b_ada', 'w_in', 'na_rpb', 'swa_sink', 'mla_q_norm', 'mla_kv_norm', 'mla_w_uq', 'mla_w_ukv', 'gqa_q_norm', 'gqa_k_norm', 'w_out', 'ln1_g', 'ln1_b', 'ffn_w_gate', 'ffn_w_up', 'ffn_conv_w', 'ffn_conv_b', 'ffn_w_down', 'ln2_g', 'ln2_b']
TWIN_DIFF_INPUT = 'x'
TWIN_INPUTS = ['x', 'c', 'ctx', 'c_ctx', 'w_ada', 'b_ada', 'w_in', 'na_rpb', 'swa_sink', 'mla_q_norm', 'mla_kv_norm', 'mla_w_uq', 'mla_w_ukv', 'gqa_q_norm', 'gqa_k_norm', 'w_out', 'ln1_g', 'ln1_b', 'ffn_w_gate', 'ffn_w_up', 'ffn_conv_w', 'ffn_conv_b', 'ffn_w_down', 'ln2_g', 'ln2_b', 'loss_target', 'm_c_ctx', 'm_w_ada', 'm_b_ada', 'm_w_in', 'm_na_rpb', 'm_swa_sink', 'm_mla_q_norm', 'm_mla_kv_norm', 'm_mla_w_uq', 'm_mla_w_ukv', 'm_gqa_q_norm', 'm_gqa_k_norm', 'm_w_out', 'm_ln1_g', 'm_ln1_b', 'm_ffn_w_gate', 'm_ffn_w_up', 'm_ffn_conv_w', 'm_ffn_conv_b', 'm_ffn_w_down', 'm_ln2_g', 'm_ln2_b', 'v_c_ctx', 'v_w_ada', 'v_b_ada', 'v_w_in', 'v_na_rpb', 'v_swa_sink', 'v_mla_q_norm', 'v_mla_kv_norm', 'v_mla_w_uq', 'v_mla_w_ukv', 'v_gqa_q_norm', 'v_gqa_k_norm', 'v_w_out', 'v_ln1_g', 'v_ln1_b', 'v_ffn_w_gate', 'v_ffn_w_up', 'v_ffn_conv_w', 'v_ffn_conv_b', 'v_ffn_w_down', 'v_ln2_g', 'v_ln2_b']
TWIN_OUTPUTS = ['loss', 'grad_x', 'grad_c_ctx', 'grad_w_ada', 'grad_b_ada', 'grad_w_in', 'grad_na_rpb', 'grad_swa_sink', 'grad_mla_q_norm', 'grad_mla_kv_norm', 'grad_mla_w_uq', 'grad_mla_w_ukv', 'grad_gqa_q_norm', 'grad_gqa_k_norm', 'grad_w_out', 'grad_ln1_g', 'grad_ln1_b', 'grad_ffn_w_gate', 'grad_ffn_w_up', 'grad_ffn_conv_w', 'grad_ffn_conv_b', 'grad_ffn_w_down', 'grad_ln2_g', 'grad_ln2_b', 'delta_c_ctx', 'delta_w_ada', 'delta_b_ada', 'delta_w_in', 'delta_na_rpb', 'delta_swa_sink', 'delta_mla_q_norm', 'delta_mla_kv_norm', 'delta_mla_w_uq', 'delta_mla_w_ukv', 'delta_gqa_q_norm', 'delta_gqa_k_norm', 'delta_w_out', 'delta_ln1_g', 'delta_ln1_b', 'delta_ffn_w_gate', 'delta_ffn_w_up', 'delta_ffn_conv_w', 'delta_ffn_conv_b', 'delta_ffn_w_down', 'delta_ln2_g', 'delta_ln2_b', 'new_m_c_ctx', 'new_m_w_ada', 'new_m_b_ada', 'new_m_w_in', 'new_m_na_rpb', 'new_m_swa_sink', 'new_m_mla_q_norm', 'new_m_mla_kv_norm', 'new_m_mla_w_uq', 'new_m_mla_w_ukv', 'new_m_gqa_q_norm', 'new_m_gqa_k_norm', 'new_m_w_out', 'new_m_ln1_g', 'new_m_ln1_b', 'new_m_ffn_w_gate', 'new_m_ffn_w_up', 'new_m_ffn_conv_w', 'new_m_ffn_conv_b', 'new_m_ffn_w_down', 'new_m_ln2_g', 'new_m_ln2_b', 'new_v_c_ctx', 'new_v_w_ada', 'new_v_b_ada', 'new_v_w_in', 'new_v_na_rpb', 'new_v_swa_sink', 'new_v_mla_q_norm', 'new_v_mla_kv_norm', 'new_v_mla_w_uq', 'new_v_mla_w_ukv', 'new_v_gqa_q_norm', 'new_v_gqa_k_norm', 'new_v_w_out', 'new_v_ln1_g', 'new_v_ln1_b', 'new_v_ffn_w_gate', 'new_v_ffn_w_up', 'new_v_ffn_conv_w', 'new_v_ffn_conv_b', 'new_v_ffn_w_down', 'new_v_ln2_g', 'new_v_ln2_b']
TWIN_LEAF_KINDS = {'loss': 'loss', 'grad_x': 'grad_x', 'grad_c_ctx': 'grad_w', 'grad_w_ada': 'grad_w', 'grad_b_ada': 'grad_w', 'grad_w_in': 'grad_w', 'grad_na_rpb': 'grad_w', 'grad_swa_sink': 'grad_w', 'grad_mla_q_norm': 'grad_w', 'grad_mla_kv_norm': 'grad_w', 'grad_mla_w_uq': 'grad_w', 'grad_mla_w_ukv': 'grad_w', 'grad_gqa_q_norm': 'grad_w', 'grad_gqa_k_norm': 'grad_w', 'grad_w_out': 'grad_w', 'grad_ln1_g': 'grad_w', 'grad_ln1_b': 'grad_w', 'grad_ffn_w_gate': 'grad_w', 'grad_ffn_w_up': 'grad_w', 'grad_ffn_conv_w': 'grad_w', 'grad_ffn_conv_b': 'grad_w', 'grad_ffn_w_down': 'grad_w', 'grad_ln2_g': 'grad_w', 'grad_ln2_b': 'grad_w', 'delta_c_ctx': 'delta_w', 'delta_w_ada': 'delta_w', 'delta_b_ada': 'delta_w', 'delta_w_in': 'delta_w', 'delta_na_rpb': 'delta_w', 'delta_swa_sink': 'delta_w', 'delta_mla_q_norm': 'delta_w', 'delta_mla_kv_norm': 'delta_w', 'delta_mla_w_uq': 'delta_w', 'delta_mla_w_ukv': 'delta_w', 'delta_gqa_q_norm': 'delta_w', 'delta_gqa_k_norm': 'delta_w', 'delta_w_out': 'delta_w', 'delta_ln1_g': 'delta_w', 'delta_ln1_b': 'delta_w', 'delta_ffn_w_gate': 'delta_w', 'delta_ffn_w_up': 'delta_w', 'delta_ffn_conv_w': 'delta_w', 'delta_ffn_conv_b': 'delta_w', 'delta_ffn_w_down': 'delta_w', 'delta_ln2_g': 'delta_w', 'delta_ln2_b': 'delta_w', 'new_m_c_ctx': 'new_m', 'new_m_w_ada': 'new_m', 'new_m_b_ada': 'new_m', 'new_m_w_in': 'new_m', 'new_m_na_rpb': 'new_m', 'new_m_swa_sink': 'new_m', 'new_m_mla_q_norm': 'new_m', 'new_m_mla_kv_norm': 'new_m', 'new_m_mla_w_uq': 'new_m', 'new_m_mla_w_ukv': 'new_m', 'new_m_gqa_q_norm': 'new_m', 'new_m_gqa_k_norm': 'new_m', 'new_m_w_out': 'new_m', 'new_m_ln1_g': 'new_m', 'new_m_ln1_b': 'new_m', 'new_m_ffn_w_gate': 'new_m', 'new_m_ffn_w_up': 'new_m', 'new_m_ffn_conv_w': 'new_m', 'new_m_ffn_conv_b': 'new_m', 'new_m_ffn_w_down': 'new_m', 'new_m_ln2_g': 'new_m', 'new_m_ln2_b': 'new_m', 'new_v_c_ctx': 'new_v', 'new_v_w_ada': 'new_v', 'new_v_b_ada': 'new_v', 'new_v_w_in': 'new_v', 'new_v_na_rpb': 'new_v', 'new_v_swa_sink': 'new_v', 'new_v_mla_q_norm': 'new_v', 'new_v_mla_kv_norm': 'new_v', 'new_v_mla_w_uq': 'new_v', 'new_v_mla_w_ukv': 'new_v', 'new_v_gqa_q_norm': 'new_v', 'new_v_gqa_k_norm': 'new_v', 'new_v_w_out': 'new_v', 'new_v_ln1_g': 'new_v', 'new_v_ln1_b': 'new_v', 'new_v_ffn_w_gate': 'new_v', 'new_v_ffn_w_up': 'new_v', 'new_v_ffn_conv_w': 'new_v', 'new_v_ffn_conv_b': 'new_v', 'new_v_ffn_w_down': 'new_v', 'new_v_ln2_g': 'new_v', 'new_v_ln2_b': 'new_v'}


def _forward(args):
    return _fwd_reference(*[args[k] for k in FWD_PARAMS])


def _output_shape():
    def fwd():
        inp = _fwd_setup_inputs(0)
        return _fwd_reference(*[inp[k] for k in FWD_PARAMS])
    out = _jax.eval_shape(fwd)
    return out.shape, out.dtype

N_MICROBATCH = 1
ADAM_LR = 0.001
ADAM_B1 = 0.9
ADAM_B2 = 0.999
ADAM_EPS = 1e-08
ADAM_WD = 0.01
ADAM_STEP = 10
PER_EXAMPLE_BATCH_AXIS = {'x': 0, 'c': 0, 'ctx': 0, 'loss_target': 0}
SHARED_INPUTS = []
_WEIGHT_DTYPES = {'c_ctx': _jnp.float32, 'w_ada': _jnp.float32, 'b_ada': _jnp.float32, 'w_in': _jnp.float32, 'na_rpb': _jnp.float32, 'swa_sink': _jnp.float32, 'mla_q_norm': _jnp.float32, 'mla_kv_norm': _jnp.float32, 'mla_w_uq': _jnp.float32, 'mla_w_ukv': _jnp.float32, 'gqa_q_norm': _jnp.float32, 'gqa_k_norm': _jnp.float32, 'w_out': _jnp.float32, 'ln1_g': _jnp.float32, 'ln1_b': _jnp.float32, 'ffn_w_gate': _jnp.float32, 'ffn_w_up': _jnp.float32, 'ffn_conv_w': _jnp.float32, 'ffn_conv_b': _jnp.float32, 'ffn_w_down': _jnp.float32, 'ln2_g': _jnp.float32, 'ln2_b': _jnp.float32}
MOMENT_SCALE = {'c_ctx': 1.239730e-02, 'w_ada': 1.718046e-02, 'b_ada': 3.340454e-02, 'w_in': 1.144900e-02, 'na_rpb': 1.589175e-03, 'swa_sink': 2.108414e-04, 'mla_q_norm': 2.899719e-03, 'mla_kv_norm': 2.394304e-02, 'mla_w_uq': 2.088544e-03, 'mla_w_ukv': 9.298962e-03, 'gqa_q_norm': 4.644424e-03, 'gqa_k_norm': 4.791755e-03, 'w_out': 3.109973e-02, 'ln1_g': 1.506237e+00, 'ln1_b': 2.821974e-01, 'ffn_w_gate': 1.278197e-02, 'ffn_w_up': 1.273809e-02, 'ffn_conv_w': 1.296154e-02, 'ffn_conv_b': 9.851307e-03, 'ffn_w_down': 4.260105e-02, 'ln2_g': 1.150288e+01, 'ln2_b': 7.991459e-01}


def _to_microbatches(a, axis):
    t = _jnp.moveaxis(a, axis, 0)
    t = t.reshape((N_MICROBATCH, t.shape[0] // N_MICROBATCH) + t.shape[1:])
    return _jnp.moveaxis(t, 1, axis + 1)


def setup_inputs(seed: int = 0) -> dict:
    inp = _fwd_setup_inputs(seed)
    key = _jax.random.fold_in(_jax.random.key(seed), 7919)
    shape, _ = _output_shape()
    out = dict(inp)
    out["loss_target"] = _jax.random.normal(_jax.random.fold_in(key, 0), shape, _jnp.float32)
    for i, name in enumerate(TWIN_WEIGHTS):
        w = inp[name].astype(_jnp.float32)
        if MOMENT_SCALE is None:
            s = _jnp.sqrt(_jnp.mean(_jnp.square(w)) + 1e-30)
        else:
            s = MOMENT_SCALE[name]
        km, kv = _jax.random.split(_jax.random.fold_in(key, i + 1))
        out[name] = w
        out["m_" + name] = s * _jax.random.normal(km, w.shape, _jnp.float32)
        out["v_" + name] = (s * s) * _jax.random.uniform(kv, w.shape, _jnp.float32, 0.5, 1.5)
    if N_MICROBATCH > 1:
        for name, axis in PER_EXAMPLE_BATCH_AXIS.items():
            out[name] = _to_microbatches(out[name], axis)
    return {'x': out['x'], 'c': out['c'], 'ctx': out['ctx'], 'c_ctx': out['c_ctx'], 'w_ada': out['w_ada'], 'b_ada': out['b_ada'], 'w_in': out['w_in'], 'na_rpb': out['na_rpb'], 'swa_sink': out['swa_sink'], 'mla_q_norm': out['mla_q_norm'], 'mla_kv_norm': out['mla_kv_norm'], 'mla_w_uq': out['mla_w_uq'], 'mla_w_ukv': out['mla_w_ukv'], 'gqa_q_norm': out['gqa_q_norm'], 'gqa_k_norm': out['gqa_k_norm'], 'w_out': out['w_out'], 'ln1_g': out['ln1_g'], 'ln1_b': out['ln1_b'], 'ffn_w_gate': out['ffn_w_gate'], 'ffn_w_up': out['ffn_w_up'], 'ffn_conv_w': out['ffn_conv_w'], 'ffn_conv_b': out['ffn_conv_b'], 'ffn_w_down': out['ffn_w_down'], 'ln2_g': out['ln2_g'], 'ln2_b': out['ln2_b'], 'loss_target': out['loss_target'], 'm_c_ctx': out['m_c_ctx'], 'm_w_ada': out['m_w_ada'], 'm_b_ada': out['m_b_ada'], 'm_w_in': out['m_w_in'], 'm_na_rpb': out['m_na_rpb'], 'm_swa_sink': out['m_swa_sink'], 'm_mla_q_norm': out['m_mla_q_norm'], 'm_mla_kv_norm': out['m_mla_kv_norm'], 'm_mla_w_uq': out['m_mla_w_uq'], 'm_mla_w_ukv': out['m_mla_w_ukv'], 'm_gqa_q_norm': out['m_gqa_q_norm'], 'm_gqa_k_norm': out['m_gqa_k_norm'], 'm_w_out': out['m_w_out'], 'm_ln1_g': out['m_ln1_g'], 'm_ln1_b': out['m_ln1_b'], 'm_ffn_w_gate': out['m_ffn_w_gate'], 'm_ffn_w_up': out['m_ffn_w_up'], 'm_ffn_conv_w': out['m_ffn_conv_w'], 'm_ffn_conv_b': out['m_ffn_conv_b'], 'm_ffn_w_down': out['m_ffn_w_down'], 'm_ln2_g': out['m_ln2_g'], 'm_ln2_b': out['m_ln2_b'], 'v_c_ctx': out['v_c_ctx'], 'v_w_ada': out['v_w_ada'], 'v_b_ada': out['v_b_ada'], 'v_w_in': out['v_w_in'], 'v_na_rpb': out['v_na_rpb'], 'v_swa_sink': out['v_swa_sink'], 'v_mla_q_norm': out['v_mla_q_norm'], 'v_mla_kv_norm': out['v_mla_kv_norm'], 'v_mla_w_uq': out['v_mla_w_uq'], 'v_mla_w_ukv': out['v_mla_w_ukv'], 'v_gqa_q_norm': out['v_gqa_q_norm'], 'v_gqa_k_norm': out['v_gqa_k_norm'], 'v_w_out': out['v_w_out'], 'v_ln1_g': out['v_ln1_g'], 'v_ln1_b': out['v_ln1_b'], 'v_ffn_w_gate': out['v_ffn_w_gate'], 'v_ffn_w_up': out['v_ffn_w_up'], 'v_ffn_conv_w': out['v_ffn_conv_w'], 'v_ffn_conv_b': out['v_ffn_conv_b'], 'v_ffn_w_down': out['v_ffn_w_down'], 'v_ln2_g': out['v_ln2_g'], 'v_ln2_b': out['v_ln2_b']}


def _loss(weights, diff, rest, loss_target):
    with _jax.named_scope("forward"):
        args = {**rest, TWIN_DIFF_INPUT: diff, **{k: w.astype(_WEIGHT_DTYPES[k]) for k, w in weights.items()}}
        y = _forward(args)
    with _jax.named_scope("loss_head"):
        err = _jnp.square(y.astype(_jnp.float32) - loss_target)
        return 0.5 * _jnp.sum(_jnp.mean(err, axis=-1)) if err.ndim else 0.5 * err


def _adamw(w, g, m, v):
    m = ADAM_B1 * m + (1.0 - ADAM_B1) * g
    v = ADAM_B2 * v + (1.0 - ADAM_B2) * _jnp.square(g)
    m_hat = m / (1.0 - ADAM_B1 ** ADAM_STEP)
    v_hat = v / (1.0 - ADAM_B2 ** ADAM_STEP)
    delta = -ADAM_LR * (m_hat / (_jnp.sqrt(v_hat) + ADAM_EPS) + ADAM_WD * w)
    return delta, m, v


def reference(x, c, ctx, c_ctx, w_ada, b_ada, w_in, na_rpb, swa_sink, mla_q_norm, mla_kv_norm, mla_w_uq, mla_w_ukv, gqa_q_norm, gqa_k_norm, w_out, ln1_g, ln1_b, ffn_w_gate, ffn_w_up, ffn_conv_w, ffn_conv_b, ffn_w_down, ln2_g, ln2_b, loss_target, m_c_ctx, m_w_ada, m_b_ada, m_w_in, m_na_rpb, m_swa_sink, m_mla_q_norm, m_mla_kv_norm, m_mla_w_uq, m_mla_w_ukv, m_gqa_q_norm, m_gqa_k_norm, m_w_out, m_ln1_g, m_ln1_b, m_ffn_w_gate, m_ffn_w_up, m_ffn_conv_w, m_ffn_conv_b, m_ffn_w_down, m_ln2_g, m_ln2_b, v_c_ctx, v_w_ada, v_b_ada, v_w_in, v_na_rpb, v_swa_sink, v_mla_q_norm, v_mla_kv_norm, v_mla_w_uq, v_mla_w_ukv, v_gqa_q_norm, v_gqa_k_norm, v_w_out, v_ln1_g, v_ln1_b, v_ffn_w_gate, v_ffn_w_up, v_ffn_conv_w, v_ffn_conv_b, v_ffn_w_down, v_ln2_g, v_ln2_b):
    given = dict(x=x, c=c, ctx=ctx, c_ctx=c_ctx, w_ada=w_ada, b_ada=b_ada, w_in=w_in, na_rpb=na_rpb, swa_sink=swa_sink, mla_q_norm=mla_q_norm, mla_kv_norm=mla_kv_norm, mla_w_uq=mla_w_uq, mla_w_ukv=mla_w_ukv, gqa_q_norm=gqa_q_norm, gqa_k_norm=gqa_k_norm, w_out=w_out, ln1_g=ln1_g, ln1_b=ln1_b, ffn_w_gate=ffn_w_gate, ffn_w_up=ffn_w_up, ffn_conv_w=ffn_conv_w, ffn_conv_b=ffn_conv_b, ffn_w_down=ffn_w_down, ln2_g=ln2_g, ln2_b=ln2_b, loss_target=loss_target, m_c_ctx=m_c_ctx, m_w_ada=m_w_ada, m_b_ada=m_b_ada, m_w_in=m_w_in, m_na_rpb=m_na_rpb, m_swa_sink=m_swa_sink, m_mla_q_norm=m_mla_q_norm, m_mla_kv_norm=m_mla_kv_norm, m_mla_w_uq=m_mla_w_uq, m_mla_w_ukv=m_mla_w_ukv, m_gqa_q_norm=m_gqa_q_norm, m_gqa_k_norm=m_gqa_k_norm, m_w_out=m_w_out, m_ln1_g=m_ln1_g, m_ln1_b=m_ln1_b, m_ffn_w_gate=m_ffn_w_gate, m_ffn_w_up=m_ffn_w_up, m_ffn_conv_w=m_ffn_conv_w, m_ffn_conv_b=m_ffn_conv_b, m_ffn_w_down=m_ffn_w_down, m_ln2_g=m_ln2_g, m_ln2_b=m_ln2_b, v_c_ctx=v_c_ctx, v_w_ada=v_w_ada, v_b_ada=v_b_ada, v_w_in=v_w_in, v_na_rpb=v_na_rpb, v_swa_sink=v_swa_sink, v_mla_q_norm=v_mla_q_norm, v_mla_kv_norm=v_mla_kv_norm, v_mla_w_uq=v_mla_w_uq, v_mla_w_ukv=v_mla_w_ukv, v_gqa_q_norm=v_gqa_q_norm, v_gqa_k_norm=v_gqa_k_norm, v_w_out=v_w_out, v_ln1_g=v_ln1_g, v_ln1_b=v_ln1_b, v_ffn_w_gate=v_ffn_w_gate, v_ffn_w_up=v_ffn_w_up, v_ffn_conv_w=v_ffn_conv_w, v_ffn_conv_b=v_ffn_conv_b, v_ffn_w_down=v_ffn_w_down, v_ln2_g=v_ln2_g, v_ln2_b=v_ln2_b)
    weights = {n: given[n] for n in TWIN_WEIGHTS}
    shared = {n: given[n] for n in SHARED_INPUTS}
    per_example = {n: given[n] for n in ['x', 'c', 'ctx']}
    grad_fn = _jax.value_and_grad(_loss, argnums=(0, 1))

    def one_microbatch(ex, loss_target):
        ex = dict(ex)
        diff = ex.pop(TWIN_DIFF_INPUT)
        return grad_fn(weights, diff, {**shared, **ex}, loss_target)

    if N_MICROBATCH == 1:
        loss, (grad_w, grad_x) = one_microbatch(per_example, given["loss_target"])
    else:
        def body(carry, xs):
            loss_sum, grad_sum = carry
            l_k, (gw_k, gx_k) = one_microbatch(xs[0], xs[1])
            with _jax.named_scope("update"):
                return (loss_sum + l_k, _jax.tree.map(_jnp.add, grad_sum, gw_k)), gx_k

        init = (_jnp.zeros((), _jnp.float32), _jax.tree.map(_jnp.zeros_like, weights))
        (loss, grad_w), grad_x = _jax.lax.scan(body, init, (per_example, given["loss_target"]))
    with _jax.named_scope("update"):
        delta_w, new_m, new_v = {}, {}, {}
        for n in TWIN_WEIGHTS:
            delta_w[n], new_m[n], new_v[n] = _adamw(weights[n], grad_w[n], given["m_" + n], given["v_" + n])
    return (loss, grad_x, *[grad_w[n] for n in TWIN_WEIGHTS], *[delta_w[n] for n in TWIN_WEIGHTS],
            *[new_m[n] for n in TWIN_WEIGHTS], *[new_v[n] for n in TWIN_WEIGHTS])
```

```python
import functools
import math

import numpy as np
import jax
import jax.numpy as jnp
from jax import lax
from jax.experimental import pallas as pl
from jax.experimental.pallas import tpu as pltpu

F32 = jnp.float32
BF16 = jnp.bfloat16
MESH = pl.DeviceIdType.MESH

GRID_W = 64
HEAD_DIM = 128
NA_HEADS, NA_WIN_R, NA_WIN_C = 4, 8, 16
SWA_HEADS, SWA_KV_HEADS, SWA_WINDOW = 4, 2, 128
MLA_HEADS, MLA_Q_LORA, MLA_KV_LORA, MLA_NOPE, MLA_ROPE, MLA_V = 4, 384, 128, 128, 64, 128
GQA_HEADS, GQA_KV_HEADS = 4, 2
ROPE_THETA = 10000.0
EPS = 1e-6
NEG = -1e30
DEPTH = 2
DEEPNORM_ALPHA = (2 * DEPTH) ** 0.25
ADAM_LR, ADAM_B1, ADAM_B2, ADAM_EPS, ADAM_WD, ADAM_STEP = 0.001, 0.9, 0.999, 1e-08, 0.01, 10

LANE = 128
V7X_VMEM_BYTES = 64 * 1024 * 1024
VMEM_LIMIT = 56 * 1024 * 1024
BQ = 128

CB_NA_Q, CB_NA_K, CB_NA_V = 0, 4, 8
CB_SWA_Q, CB_SWA_K, CB_SWA_V = 12, 16, 18
CB_CQ, CB_CKV = 20, 23
CB_GQA_Q, CB_GQA_K, CB_GQA_V = 24, 28, 30
CB_KPE = 32
PCOLS = 33 * LANE
IN_COLS = 4160


def _cparams(sem=None, **kw):
    return pltpu.CompilerParams(dimension_semantics=sem, vmem_limit_bytes=VMEM_LIMIT, **kw)


def _pick(n, target, mult=LANE):
    best = None
    for d in range(mult, min(n, target) + 1, mult):
        if n % d == 0:
            best = d
    return n if best is None else best


def _mm(a, b, *, mode="nn", out_dtype=F32, a_off=0, a_k=None, tm=1088, tn=1408, tk=512, exact=False, add=None,
        stack=None, split4=False, name):
    if mode == "tn":
        K, M = a.shape
        K2, N = b.shape
    elif mode == "nn":
        M, K = a.shape
        K2, N = b.shape
    else:
        M, K = a.shape
        N, K2 = b.shape
    if a_k is not None:
        K = a_k
    assert K == K2, (a.shape, b.shape, mode)
    bm = _pick(M, tm, LANE if mode == "tn" else 16)
    bn = _pick(N // 4 if split4 else N, tn)
    bk = _pick(K, tk)
    assert a_off % bk == 0
    koff = a_off // bk
    nk = K // bk
    if mode == "tn":
        a_spec = pl.BlockSpec((bk, bm), lambda i, j, k: (k, i))
        b_spec = pl.BlockSpec((bk, bn), lambda i, j, k: (k, j))
        dims = (((0,), (0,)), ((), ()))
    elif mode == "nn":
        a_spec = pl.BlockSpec((bm, bk), lambda i, j, k: (i, k + koff))
        b_spec = pl.BlockSpec((bk, bn), lambda i, j, k: (k, j))
        dims = (((1,), (0,)), ((), ()))
    else:
        a_spec = pl.BlockSpec((bm, bk), lambda i, j, k: (i, k + koff))
        b_spec = pl.BlockSpec((bn, bk), lambda i, j, k: (j, k))
        dims = (((1,), (1,)), ((), ()))

    operands = [a, b]
    in_specs = [a_spec, b_spec]
    if add is not None:
        operands.append(add)
        in_specs.append(pl.BlockSpec((bm, bn), lambda i, j, k: (i, j)))
    aliases = {}
    if stack is None:
        out_spec = pl.BlockSpec((bm, bn), lambda i, j, k: (i, j))
        out_shape = jax.ShapeDtypeStruct((M, N), out_dtype)
    else:
        n_layers, layer, buf = stack
        if split4:
            nb = N // 4 // bn
            assert N % (4 * bn) == 0
            out_spec = pl.BlockSpec((None, None, bm, bn), lambda i, j, k: (layer, j // nb, i, j % nb))
            out_shape = jax.ShapeDtypeStruct((n_layers, 4, M, N // 4), out_dtype)
        else:
            out_spec = pl.BlockSpec((None, bm, bn), lambda i, j, k: (layer, i, j))
            out_shape = jax.ShapeDtypeStruct((n_layers, M, N), out_dtype)
        if buf is not None:
            aliases = {len(operands): 0}
            operands.append(buf)
            in_specs.append(pl.BlockSpec(memory_space=pl.ANY))
    has_add, has_buf = add is not None, bool(aliases)

    def body(*refs):
        a_ref, b_ref = refs[:2]
        add_ref = refs[2] if has_add else None
        o_ref, acc_ref = refs[2 + has_add + has_buf:]
        k = pl.program_id(2)

        @pl.when(k == 0)
        def _():
            acc_ref[...] = jnp.zeros_like(acc_ref)

        if exact:
            acc_ref[...] += lax.dot_general(a_ref[...].astype(F32), b_ref[...].astype(F32), dims,
                                            precision=lax.Precision.HIGHEST, preferred_element_type=F32)
        else:
            acc_ref[...] += lax.dot_general(a_ref[...].astype(BF16), b_ref[...].astype(BF16), dims,
                                            preferred_element_type=F32)

        @pl.when(k == nk - 1)
        def _():
            res = acc_ref[...]
            if has_add:
                res = res + add_ref[...].astype(F32)
            o_ref[...] = res.astype(o_ref.dtype)

    return pl.pallas_call(
        body, name=name,
        grid=(M // bm, N // bn, nk),
        in_specs=in_specs, out_specs=out_spec, out_shape=out_shape,
        scratch_shapes=[pltpu.VMEM((bm, bn), F32)],
        input_output_aliases=aliases,
        compiler_params=_cparams(("parallel", "parallel", "arbitrary")),
    )(*operands)


def _row_block(T, S):
    return _pick(math.gcd(T, S), 256, 16)


def _ln_stats(x):
    mu = jnp.mean(x, axis=-1, keepdims=True)
    xc = x - mu
    var = jnp.mean(xc * xc, axis=-1, keepdims=True)
    rstd = lax.rsqrt(var + EPS)
    return xc * rstd, rstd


def _ln_bwd(dxhat, xhat, rstd):
    m1 = jnp.mean(dxhat, axis=-1, keepdims=True)
    m2 = jnp.mean(dxhat * xhat, axis=-1, keepdims=True)
    return rstd * (dxhat - m1 - xhat * m2)


def _sel(ref, is_ctx):
    return jnp.where(is_ctx, ref[1:2, :], ref[0:1, :])


def _mod_fwd(x, shift, scale, S, *, name):
    T, D = x.shape
    bt = _row_block(T, S)

    def body(x_ref, sh_ref, sc_ref, o_ref):
        is_ctx = pl.program_id(0) * bt >= S
        xhat, _ = _ln_stats(x_ref[...])
        o_ref[...] = (xhat * (1.0 + _sel(sc_ref, is_ctx)) + _sel(sh_ref, is_ctx)).astype(o_ref.dtype)

    return pl.pallas_call(
        body, name=name, grid=(T // bt,),
        in_specs=[pl.BlockSpec((bt, D), lambda i: (i, 0)), pl.BlockSpec((2, D), lambda i: (0, 0)),
                  pl.BlockSpec((2, D), lambda i: (0, 0))],
        out_specs=pl.BlockSpec((bt, D), lambda i: (i, 0)),
        out_shape=jax.ShapeDtypeStruct((T, D), BF16),
        compiler_params=_cparams(("parallel",)),
    )(x, shift, scale)


def _acc_groups(acc_ref, row, val, is_ctx):
    f = jnp.where(is_ctx, 1.0, 0.0).astype(F32)
    acc_ref[row:row + 1, :] += val * (1.0 - f)
    acc_ref[row + 1:row + 2, :] += val * f


def _mod_bwd(x, dh, scale, dx_in, S, *, name):
    T, D = x.shape
    bt = _row_block(T, S)

    def body(x_ref, dh_ref, sc_ref, dxi_ref, dx_ref, acc_ref):
        i = pl.program_id(0)
        is_ctx = i * bt >= S

        @pl.when(i == 0)
        def _():
            acc_ref[...] = jnp.zeros_like(acc_ref)

        xhat, rstd = _ln_stats(x_ref[...])
        dh = dh_ref[...].astype(F32)
        dxhat = dh * (1.0 + _sel(sc_ref, is_ctx))
        dx_ref[...] = dxi_ref[...] + _ln_bwd(dxhat, xhat, rstd)
        _acc_groups(acc_ref, 0, jnp.sum(dh, axis=0, keepdims=True), is_ctx)
        _acc_groups(acc_ref, 2, jnp.sum(dh * xhat, axis=0, keepdims=True), is_ctx)

    return pl.pallas_call(
        body, name=name, grid=(T // bt,),
        in_specs=[pl.BlockSpec((bt, D), lambda i: (i, 0)), pl.BlockSpec((bt, D), lambda i: (i, 0)),
                  pl.BlockSpec((2, D), lambda i: (0, 0)), pl.BlockSpec((bt, D), lambda i: (i, 0))],
        out_specs=[pl.BlockSpec((bt, D), lambda i: (i, 0)), pl.BlockSpec((8, D), lambda i: (0, 0))],
        out_shape=[jax.ShapeDtypeStruct((T, D), F32), jax.ShapeDtypeStruct((8, D), F32)],
        compiler_params=_cparams(("arbitrary",)),
    )(x, dh, scale, dx_in)


def _res_fwd(x, z, gate, lg, lb, S, *, name):
    T, D = x.shape
    bt = _row_block(T, S)

    def body(x_ref, z_ref, g_ref, lg_ref, lb_ref, o_ref):
        is_ctx = pl.program_id(0) * bt >= S
        u = DEEPNORM_ALPHA * x_ref[...] + _sel(g_ref, is_ctx) * z_ref[...]
        uhat, _ = _ln_stats(u)
        o_ref[...] = uhat * lg_ref[...] + lb_ref[...]

    row = pl.BlockSpec((bt, D), lambda i: (i, 0))
    return pl.pallas_call(
        body, name=name, grid=(T // bt,),
        in_specs=[row, row, pl.BlockSpec((2, D), lambda i: (0, 0)), pl.BlockSpec((1, D), lambda i: (0, 0)),
                  pl.BlockSpec((1, D), lambda i: (0, 0))],
        out_specs=row,
        out_shape=jax.ShapeDtypeStruct((T, D), F32),
        compiler_params=_cparams(("parallel",)),
    )(x, z, gate, lg, lb)


def _res_bwd(x, z, gate, lg, dy, S, *, name):
    T, D = x.shape
    bt = _row_block(T, S)

    def body(x_ref, z_ref, g_ref, lg_ref, dy_ref, dx_ref, dz_ref, acc_ref):
        i = pl.program_id(0)
        is_ctx = i * bt >= S

        @pl.when(i == 0)
        def _():
            acc_ref[...] = jnp.zeros_like(acc_ref)

        gate_v = _sel(g_ref, is_ctx)
        zv = z_ref[...]
        u = DEEPNORM_ALPHA * x_ref[...] + gate_v * zv
        uhat, rstd = _ln_stats(u)
        dyv = dy_ref[...]
        du = _ln_bwd(dyv * lg_ref[...], uhat, rstd)
        dx_ref[...] = DEEPNORM_ALPHA * du
        dz_ref[...] = (gate_v * du).astype(dz_ref.dtype)
        _acc_groups(acc_ref, 0, jnp.sum(du * zv, axis=0, keepdims=True), is_ctx)
        acc_ref[2:3, :] += jnp.sum(dyv * uhat, axis=0, keepdims=True)
        acc_ref[3:4, :] += jnp.sum(dyv, axis=0, keepdims=True)

    row = pl.BlockSpec((bt, D), lambda i: (i, 0))
    return pl.pallas_call(
        body, name=name, grid=(T // bt,),
        in_specs=[row, row, pl.BlockSpec((2, D), lambda i: (0, 0)), pl.BlockSpec((1, D), lambda i: (0, 0)), row],
        out_specs=[row, row, pl.BlockSpec((8, D), lambda i: (0, 0))],
        out_shape=[jax.ShapeDtypeStruct((T, D), F32), jax.ShapeDtypeStruct((T, D), BF16),
                   jax.ShapeDtypeStruct((8, D), F32)],
        compiler_params=_cparams(("arbitrary",)),
    )(x, z, gate, lg, dy)


def _loss_fwd_bwd(y, target, S, *, name):
    T, D = y.shape
    bt = _row_block(T, S)
    n_lat = S // bt

    def body(y_ref, t_ref, dy_ref, l_ref):
        i = pl.program_id(0)

        @pl.when(i == 0)
        def _():
            l_ref[...] = jnp.zeros_like(l_ref)

        keep = jnp.where(i * bt >= S, 0.0, 1.0).astype(F32)
        err = (y_ref[...] - t_ref[...]) * keep
        dy_ref[...] = err * (1.0 / D)
        l_ref[...] += jnp.sum(err * err) * (0.5 / D)

    return pl.pallas_call(
        body, name=name, grid=(T // bt,),
        in_specs=[pl.BlockSpec((bt, D), lambda i: (i, 0)),
                  pl.BlockSpec((bt, D), lambda i: (jnp.minimum(i, n_lat - 1), 0))],
        out_specs=[pl.BlockSpec((bt, D), lambda i: (i, 0)), pl.BlockSpec((8, LANE), lambda i: (0, 0))],
        out_shape=[jax.ShapeDtypeStruct((T, D), F32), jax.ShapeDtypeStruct((8, LANE), F32)],
        compiler_params=_cparams(("arbitrary",)),
    )(y, target)


def _rope_tables(S, C, dim):
    half = dim // 4
    t = jnp.arange(S)
    row = (t // GRID_W).astype(F32)
    col = (t % GRID_W).astype(F32)
    inv = ROPE_THETA ** (-jnp.arange(half, dtype=F32) / half)
    ar, ac = row[:, None] * inv[None, :], col[:, None] * inv[None, :]
    cos = jnp.concatenate([jnp.cos(ar), jnp.cos(ar), jnp.cos(ac), jnp.cos(ac)], axis=1)
    ss = jnp.concatenate([-jnp.sin(ar), jnp.sin(ar), -jnp.sin(ac), jnp.sin(ac)], axis=1)
    cos = jnp.pad(cos, ((0, C), (0, LANE - dim)), constant_values=1.0)
    ss = jnp.pad(ss, ((0, C), (0, LANE - dim)))
    return cos, ss


def _rope(x, cos, ss, half):
    lane = lax.broadcasted_iota(jnp.int32, x.shape, 1)
    first = (lane % (2 * half)) < half
    partner = jnp.where(first, pltpu.roll(x, LANE - half, 1), pltpu.roll(x, half, 1))
    return x * cos + partner * ss


def _rms(x):
    r = lax.rsqrt(jnp.mean(x * x, axis=-1, keepdims=True) + EPS)
    return x * r, r


_CAST_BLOCKS = tuple(range(0, 12)) + (18, 19, 30, 31)
_ROPE_BLOCKS = tuple(range(12, 18))
_GQA_Q_BLOCKS = tuple(range(24, 28))
_GQA_K_BLOCKS = (28, 29)


def _prep_fwd(p, tabs, gq, gk, mq, mkv, S, *, name):
    T = p.shape[0]
    bt = _row_block(T, S)
    cA, sA, cP, sP = tabs

    def body(p_ref, cA_ref, sA_ref, cP_ref, sP_ref, gq_ref, gk_ref, mq_ref, mkv_ref, o_ref):
        def blk(b):
            return p_ref[:, b * LANE:(b + 1) * LANE]

        def put(b, val):
            o_ref[:, b * LANE:(b + 1) * LANE] = val.astype(o_ref.dtype)

        cA_v, sA_v = cA_ref[...], sA_ref[...]
        for b in _CAST_BLOCKS:
            put(b, blk(b))
        for b in _ROPE_BLOCKS:
            put(b, _rope(blk(b), cA_v, sA_v, 32))
        for b in _GQA_Q_BLOCKS:
            put(b, _rope(_rms(blk(b))[0] * gq_ref[...], cA_v, sA_v, 32))
        for b in _GQA_K_BLOCKS:
            put(b, _rope(_rms(blk(b))[0] * gk_ref[...], cA_v, sA_v, 32))
        put(CB_KPE, _rope(blk(CB_KPE), cP_ref[...], sP_ref[...], 16))
        cq = p_ref[:, CB_CQ * LANE:CB_CKV * LANE]
        o_ref[:, CB_CQ * LANE:CB_CKV * LANE] = (_rms(cq)[0] * mq_ref[...]).astype(o_ref.dtype)
        put(CB_CKV, _rms(blk(CB_CKV))[0] * mkv_ref[...])

    row128 = pl.BlockSpec((bt, LANE), lambda i: (i, 0))
    vec = lambda n: pl.BlockSpec((1, n), lambda i: (0, 0))
    return pl.pallas_call(
        body, name=name, grid=(T // bt,),
        in_specs=[pl.BlockSpec((bt, PCOLS), lambda i: (i, 0)), row128, row128, row128, row128,
                  vec(LANE), vec(LANE), vec(MLA_Q_LORA), vec(LANE)],
        out_specs=pl.BlockSpec((bt, PCOLS), lambda i: (i, 0)),
        out_shape=jax.ShapeDtypeStruct((T, PCOLS), BF16),
        compiler_params=_cparams(("parallel",)),
    )(p, cA, sA, cP, sP, gq, gk, mq, mkv)


def _prep_bwd(p, grads, tabs, gq, gk, mq, mkv, S, *, name):
    T = p.shape[0]
    bt = _row_block(T, S)
    cA, sA, cP, sP = tabs
    arrays = []
    where = {}
    for key, (arr, cb) in grads.items():
        idx = next((n for n, a in enumerate(arrays) if a is arr), None)
        if idx is None:
            arrays.append(arr)
            idx = len(arrays) - 1
        where[key] = (idx, cb)
    ng = len(arrays)

    def body(*refs):
        p_ref, cA_ref, sA_ref, cP_ref, sP_ref, gq_ref, gk_ref, mq_ref, mkv_ref = refs[:9]
        g_refs = refs[9:9 + ng]
        o_ref, acc_ref = refs[9 + ng:]
        i = pl.program_id(0)

        @pl.when(i == 0)
        def _():
            acc_ref[...] = jnp.zeros_like(acc_ref)

        def blk(b):
            return p_ref[:, b * LANE:(b + 1) * LANE]

        def grad(b, width=LANE):
            idx, cb = where[b]
            return g_refs[idx][:, cb * LANE:cb * LANE + width].astype(F32)

        def put(b, val):
            o_ref[:, b * LANE:(b + 1) * LANE] = val.astype(o_ref.dtype)

        def rms_bwd(x, dy, g, row, width):
            n, r = _rms(x)
            acc_ref[row:row + 1, 0:width] += jnp.sum(dy * n, axis=0, keepdims=True)
            dn = dy * g
            return r * (dn - n * jnp.mean(dn * n, axis=-1, keepdims=True))

        cA_v, sA_v = cA_ref[...], sA_ref[...]
        for b in _CAST_BLOCKS:
            put(b, grad(b))
        for b in _ROPE_BLOCKS:
            put(b, _rope(grad(b), cA_v, -sA_v, 32))
        for b in _GQA_Q_BLOCKS:
            put(b, rms_bwd(blk(b), _rope(grad(b), cA_v, -sA_v, 32), gq_ref[...], 0, LANE))
        for b in _GQA_K_BLOCKS:
            put(b, rms_bwd(blk(b), _rope(grad(b), cA_v, -sA_v, 32), gk_ref[...], 1, LANE))
        put(CB_KPE, _rope(grad(CB_KPE), cP_ref[...], -sP_ref[...], 16))
        dcq = rms_bwd(p_ref[:, CB_CQ * LANE:CB_CKV * LANE], grad(CB_CQ, MLA_Q_LORA), mq_ref[...], 2, MLA_Q_LORA)
        o_ref[:, CB_CQ * LANE:CB_CKV * LANE] = dcq.astype(o_ref.dtype)
        put(CB_CKV, rms_bwd(blk(CB_CKV), grad(CB_CKV), mkv_ref[...], 3, LANE))

    row128 = pl.BlockSpec((bt, LANE), lambda i: (i, 0))
    vec = lambda n: pl.BlockSpec((1, n), lambda i: (0, 0))
    g_specs = [pl.BlockSpec((bt, a.shape[1]), lambda i: (i, 0)) for a in arrays]
    return pl.pallas_call(
        body, name=name, grid=(T // bt,),
        in_specs=[pl.BlockSpec((bt, PCOLS), lambda i: (i, 0)), row128, row128, row128, row128,
                  vec(LANE), vec(LANE), vec(MLA_Q_LORA), vec(LANE)] + g_specs,
        out_specs=[pl.BlockSpec((bt, PCOLS), lambda i: (i, 0)), pl.BlockSpec((8, MLA_Q_LORA), lambda i: (0, 0))],
        out_shape=[jax.ShapeDtypeStruct((T, PCOLS), BF16), jax.ShapeDtypeStruct((8, MLA_Q_LORA), F32)],
        compiler_params=_cparams(("arbitrary",)),
    )(p, cA, sA, cP, sP, gq, gk, mq, mkv, *arrays)


def _pe_rope(qm, tabs, S, sign, out_dtype, *, name):
    T, N = qm.shape
    bt = _row_block(T, S)
    cP, sP = tabs[2], tabs[3]

    def body(x_ref, c_ref, s_ref, o_ref):
        for b in range(MLA_HEADS):
            o_ref[:, b * LANE:(b + 1) * LANE] = x_ref[:, b * LANE:(b + 1) * LANE].astype(o_ref.dtype)
        for b in range(MLA_HEADS, 2 * MLA_HEADS):
            x = x_ref[:, b * LANE:(b + 1) * LANE].astype(F32)
            o_ref[:, b * LANE:(b + 1) * LANE] = _rope(x, c_ref[...], sign * s_ref[...], 16).astype(o_ref.dtype)

    row128 = pl.BlockSpec((bt, LANE), lambda i: (i, 0))
    return pl.pallas_call(
        body, name=name, grid=(T // bt,),
        in_specs=[pl.BlockSpec((bt, N), lambda i: (i, 0)), row128, row128],
        out_specs=pl.BlockSpec((bt, N), lambda i: (i, 0)),
        out_shape=jax.ShapeDtypeStruct((T, N), out_dtype),
        compiler_params=_cparams(("parallel",)),
    )(qm, cP, sP)


def _dot_nt(a, b):
    return lax.dot_general(a, b, (((1,), (1,)), ((), ())), preferred_element_type=F32)


def _dot_tn(a, b):
    return lax.dot_general(a, b, (((0,), (0,)), ((), ())), preferred_element_type=F32)


def _dot(a, b):
    return jnp.dot(a, b, preferred_element_type=F32)


def _window_fns(band, S, n_var):
    n_lat = S // BQ
    if band is None:
        return None
    reach, span = band

    def fns(j):
        start = jnp.clip(j - reach, 0, n_lat - span)
        return start, jnp.clip(j - start, 0, n_var - 1)

    return fns


class _AttnCfg:
    def __init__(self, *, Hkv, G, S, C, band, scale, n_var=0, bias_per_head=False, has_sink=False, two=False):
        self.Hkv, self.G, self.S, self.C, self.band, self.scale = Hkv, G, S, C, band, scale
        self.n_var, self.bias_per_head, self.has_sink, self.two = n_var, bias_per_head, has_sink, two
        self.W = S if band is None else band[1] * BQ
        self.T = S + C


def _attn_probs(cfg, j, q_ref, k_ref, q2_ref, k2_ref, bias_ref, sink_ref):
    G, S, C, W = cfg.G, cfg.S, cfg.C, cfg.W
    is_ctx = j * BQ >= S
    if cfg.band is None:
        off, var = 0, 0
    else:
        start, var = _window_fns(cfg.band, S, cfg.n_var)(j)
        off = pl.multiple_of(start * BQ, BQ)
    qt = q_ref[...]
    qs = jnp.concatenate([qt[:, g * LANE:(g + 1) * LANE] for g in range(G)], axis=0) if G > 1 else qt
    kw = k_ref[pl.ds(off, W), :]
    kc = k_ref[pl.ds(S, C), :]
    s_w = _dot_nt(qs, kw)
    s_c = _dot_nt(qs, kc)
    q2s = k2w = k2c = None
    if cfg.two:
        q2s = q2_ref[...]
        k2w = k2_ref[pl.ds(off, W), :]
        k2c = k2_ref[pl.ds(S, C), :]
        s_w = s_w + _dot_nt(q2s, k2w)
        s_c = s_c + _dot_nt(q2s, k2c)
    s_w = s_w * cfg.scale
    s_c = s_c * cfg.scale
    if cfg.n_var:
        b = bias_ref[0, pl.ds(var, 1)][0]
        s_w = s_w + (jnp.concatenate([b] * G, axis=0) if G > 1 else b)
    s_w = jnp.where(is_ctx, NEG, s_w)
    m = jnp.maximum(jnp.max(s_w, axis=-1, keepdims=True), jnp.max(s_c, axis=-1, keepdims=True))
    if cfg.has_sink:
        sink = sink_ref[0][:, 0:1]
        m = jnp.maximum(m, sink)
    e_w = jnp.exp(s_w - m)
    e_c = jnp.exp(s_c - m)
    l = jnp.sum(e_w, axis=-1, keepdims=True) + jnp.sum(e_c, axis=-1, keepdims=True)
    p_s = None
    if cfg.has_sink:
        e_s = jnp.exp(sink - m)
        l = l + e_s
    inv = 1.0 / l
    if cfg.has_sink:
        p_s = e_s * inv
    return e_w * inv, e_c * inv, p_s, (off, var, qs, kw, kc, q2s, k2w, k2c)


def _attn_specs(cfg, q_cb, k_cb, v_cb, q2_cb, k2_cb):
    G, T = cfg.G, cfg.T
    specs = [pl.BlockSpec((BQ, G * LANE), lambda h, j: (j, q_cb // G + h)),
             pl.BlockSpec((T, LANE), lambda h, j: (0, k_cb + h)),
             pl.BlockSpec((T, LANE), lambda h, j: (0, v_cb + h))]
    if cfg.two:
        specs += [pl.BlockSpec((BQ, LANE), lambda h, j: (j, q2_cb + h)),
                  pl.BlockSpec((T, LANE), lambda h, j: (0, k2_cb))]
    if cfg.n_var:
        if cfg.bias_per_head:
            specs.append(pl.BlockSpec((1, cfg.n_var, BQ, cfg.W), lambda h, j: (h, 0, 0, 0)))
        else:
            specs.append(pl.BlockSpec((1, cfg.n_var, BQ, cfg.W), lambda h, j: (0, 0, 0, 0)))
    if cfg.has_sink:
        specs.append(pl.BlockSpec((1, G * BQ, LANE), lambda h, j: (h, 0, 0)))
    return specs


def _attn_unpack(cfg, refs):
    refs = list(refs)
    q_ref, k_ref, v_ref = refs[:3]
    n = 3
    q2_ref = k2_ref = bias_ref = sink_ref = None
    if cfg.two:
        q2_ref, k2_ref = refs[n:n + 2]
        n += 2
    if cfg.n_var:
        bias_ref = refs[n]
        n += 1
    if cfg.has_sink:
        sink_ref = refs[n]
        n += 1
    return (q_ref, k_ref, v_ref, q2_ref, k2_ref, bias_ref, sink_ref), refs[n:]


def _attn_fwd(cfg, q, q_cb, k, k_cb, v, v_cb, *, q2=None, q2_cb=0, k2=None, k2_cb=0, bias=None, sink=None, name):
    G, T, S, C, W = cfg.G, cfg.T, cfg.S, cfg.C, cfg.W
    assert q_cb % G == 0
    operands = [q, k, v] + ([q2, k2] if cfg.two else []) + ([bias] if cfg.n_var else []) + ([sink] if cfg.has_sink else [])

    def body(*refs):
        (q_ref, k_ref, v_ref, q2_ref, k2_ref, bias_ref, sink_ref), (o_ref,) = _attn_unpack(cfg, refs)
        j = pl.program_id(1)
        p_w, p_c, _, (off, _, _, _, _, _, _, _) = _attn_probs(cfg, j, q_ref, k_ref, q2_ref, k2_ref, bias_ref, sink_ref)
        o = _dot(p_w.astype(BF16), v_ref[pl.ds(off, W), :]) + _dot(p_c.astype(BF16), v_ref[pl.ds(S, C), :])
        for g in range(G):
            o_ref[:, g * LANE:(g + 1) * LANE] = o[g * BQ:(g + 1) * BQ].astype(o_ref.dtype)

    return pl.pallas_call(
        body, name=name, grid=(cfg.Hkv, T // BQ),
        in_specs=_attn_specs(cfg, q_cb, k_cb, v_cb, q2_cb, k2_cb),
        out_specs=pl.BlockSpec((BQ, G * LANE), lambda h, j: (j, h)),
        out_shape=jax.ShapeDtypeStruct((T, cfg.Hkv * G * LANE), BF16),
        compiler_params=_cparams(("parallel", "parallel")),
    )(*operands)


def _attn_bwd(cfg, q, q_cb, k, k_cb, v, v_cb, do, do_cb, *, q2=None, q2_cb=0, k2=None, k2_cb=0, bias=None, sink=None,
              want_dbias=False, dq_dtype=F32, name):
    G, T, S, C, W, Hkv = cfg.G, cfg.T, cfg.S, cfg.C, cfg.W, cfg.Hkv
    assert q_cb % G == 0 and do_cb % G == 0 and not (want_dbias and G > 1)
    operands = [q, k, v] + ([q2, k2] if cfg.two else []) + ([bias] if cfg.n_var else []) + ([sink] if cfg.has_sink else [])
    operands.append(do)
    in_specs = _attn_specs(cfg, q_cb, k_cb, v_cb, q2_cb, k2_cb)
    in_specs.append(pl.BlockSpec((BQ, G * LANE), lambda h, j: (j, do_cb // G + h)))

    out_specs = [pl.BlockSpec((BQ, G * LANE), lambda h, j: (j, h)),
                 pl.BlockSpec((T, LANE), lambda h, j: (0, h)),
                 pl.BlockSpec((T, LANE), lambda h, j: (0, h))]
    out_shape = [jax.ShapeDtypeStruct((T, Hkv * G * LANE), dq_dtype),
                 jax.ShapeDtypeStruct((T, Hkv * LANE), F32),
                 jax.ShapeDtypeStruct((T, Hkv * LANE), F32)]
    if cfg.two:
        out_specs += [pl.BlockSpec((BQ, LANE), lambda h, j: (j, h)), pl.BlockSpec((T, LANE), lambda h, j: (0, 0))]
        out_shape += [jax.ShapeDtypeStruct((T, Hkv * LANE), dq_dtype), jax.ShapeDtypeStruct((T, LANE), F32)]
    if want_dbias:
        out_specs.append(pl.BlockSpec((1, cfg.n_var, BQ, W), lambda h, j: (h, 0, 0, 0)))
        out_shape.append(jax.ShapeDtypeStruct((Hkv, cfg.n_var, BQ, W), F32))
    if cfg.has_sink:
        out_specs.append(pl.BlockSpec((1, G * BQ, LANE), lambda h, j: (h, 0, 0)))
        out_shape.append(jax.ShapeDtypeStruct((Hkv, G * BQ, LANE), F32))

    def body(*refs):
        (q_ref, k_ref, v_ref, q2_ref, k2_ref, bias_ref, sink_ref), rest = _attn_unpack(cfg, refs)
        do_ref, dq_ref, dk_ref, dv_ref = rest[:4]
        rest = rest[4:]
        dq2_ref = dk2_ref = dbias_ref = dsink_ref = None
        if cfg.two:
            dq2_ref, dk2_ref = rest[:2]
            rest = rest[2:]
        if want_dbias:
            dbias_ref = rest[0]
            rest = rest[1:]
        if cfg.has_sink:
            dsink_ref = rest[0]
        h = pl.program_id(0)
        j = pl.program_id(1)

        @pl.when(j == 0)
        def _():
            dk_ref[...] = jnp.zeros_like(dk_ref)
            dv_ref[...] = jnp.zeros_like(dv_ref)
            if want_dbias:
                dbias_ref[...] = jnp.zeros_like(dbias_ref)
            if cfg.has_sink:
                dsink_ref[...] = jnp.zeros_like(dsink_ref)

        if cfg.two:
            @pl.when((j == 0) & (h == 0))
            def _():
                dk2_ref[...] = jnp.zeros_like(dk2_ref)

        p_w, p_c, p_s, (off, var, qs, kw, kc, q2s, k2w, k2c) = _attn_probs(
            cfg, j, q_ref, k_ref, q2_ref, k2_ref, bias_ref, sink_ref)
        dot_ = do_ref[...]
        dos = jnp.concatenate([dot_[:, g * LANE:(g + 1) * LANE] for g in range(G)], axis=0) if G > 1 else dot_
        dos = dos.astype(BF16)
        vw = v_ref[pl.ds(off, W), :]
        vc = v_ref[pl.ds(S, C), :]
        dp_w = _dot_nt(dos, vw)
        dp_c = _dot_nt(dos, vc)
        delta = jnp.sum(p_w * dp_w, axis=-1, keepdims=True) + jnp.sum(p_c * dp_c, axis=-1, keepdims=True)
        ds_w = p_w * (dp_w - delta)
        ds_c = p_c * (dp_c - delta)
        if want_dbias:
            dbias_ref[0, pl.ds(var, 1)] += ds_w[None]
        if cfg.has_sink:
            dsink_ref[0] += jnp.broadcast_to(-(p_s * delta), (G * BQ, LANE))
        dsw = (ds_w * cfg.scale).astype(BF16)
        dsc = (ds_c * cfg.scale).astype(BF16)
        dq = _dot(dsw, kw) + _dot(dsc, kc)
        for g in range(G):
            dq_ref[:, g * LANE:(g + 1) * LANE] = dq[g * BQ:(g + 1) * BQ].astype(dq_ref.dtype)
        dk_ref[pl.ds(off, W), :] += _dot_tn(dsw, qs)
        dk_ref[pl.ds(S, C), :] += _dot_tn(dsc, qs)
        dv_ref[pl.ds(off, W), :] += _dot_tn(p_w.astype(BF16), dos)
        dv_ref[pl.ds(S, C), :] += _dot_tn(p_c.astype(BF16), dos)
        if cfg.two:
            dq2_ref[...] = (_dot(dsw, k2w) + _dot(dsc, k2c)).astype(dq2_ref.dtype)
            dk2_ref[pl.ds(off, W), :] += _dot_tn(dsw, q2s)
            dk2_ref[pl.ds(S, C), :] += _dot_tn(dsc, q2s)

    return pl.pallas_call(
        body, name=name, grid=(Hkv, T // BQ),
        in_specs=in_specs, out_specs=out_specs, out_shape=out_shape,
        compiler_params=_cparams(("arbitrary", "arbitrary")),
    )(*operands)


def _na_bias(rpb, S):
    H = rpb.shape[0]
    rows = S // GRID_W
    pad_l = GRID_W - 1 - (NA_WIN_C - 1)
    ext = jnp.concatenate([jnp.broadcast_to(rpb[:, :, :1], (H, 2 * NA_WIN_R - 1, pad_l)), rpb,
                           jnp.broadcast_to(rpb[:, :, -1:], (H, 2 * NA_WIN_R - 1, pad_l))], axis=2)
    by_col = jnp.stack([ext[:, :, GRID_W - 1 - qc:2 * GRID_W - 1 - qc] for qc in range(GRID_W)], axis=2)
    cq = np.arange(GRID_W)
    c0 = np.clip(cq - NA_WIN_C // 2, 0, GRID_W - NA_WIN_C)
    col_in = (cq[None, :] >= c0[:, None]) & (cq[None, :] < c0[:, None] + NA_WIN_C)
    n_lat = S // BQ
    neg_tile = jnp.full((H, GRID_W, GRID_W), NEG, F32)
    variants = []
    for v in range(5):
        j = {0: 0, 1: 1, 2: 2, 3: n_lat - 2, 4: n_lat - 1}[v]
        start = int(np.clip(j - 2, 0, n_lat - 5))
        assert j - start == v
        q_rows = []
        for qr in range(2):
            r = 2 * j + qr
            r0 = int(np.clip(r - NA_WIN_R // 2, 0, rows - NA_WIN_R))
            k_tiles = []
            for kr in range(10):
                krow = 2 * start + kr
                if r0 <= krow < r0 + NA_WIN_R:
                    k_tiles.append(jnp.where(col_in[None], by_col[:, krow - r + NA_WIN_R - 1], NEG))
                else:
                    k_tiles.append(neg_tile)
            q_rows.append(jnp.concatenate(k_tiles, axis=2))
        variants.append(jnp.concatenate(q_rows, axis=1))
    return jnp.stack(variants, axis=1)


def _swa_mask(S):
    qq = np.arange(BQ)[:, None]
    kk = np.arange(3 * BQ)[None, :]
    tiles = [np.where(np.abs(kk - v * BQ - qq) <= SWA_WINDOW, 0.0, NEG) for v in range(3)]
    return jnp.asarray(np.stack(tiles)[None], F32)


def _ffn_tiles(T, S, F):
    return _row_block(T, S), _pick(F, 1408)


def _halo_specs(T, bt, bf, col_off):
    n8 = bt // 8
    return [pl.BlockSpec((bt, bf), lambda f, i: (i, f + col_off)),
            pl.BlockSpec((8, bf), lambda f, i: (jnp.maximum(i * n8 - 1, 0), f + col_off)),
            pl.BlockSpec((8, bf), lambda f, i: (jnp.minimum((i + 1) * n8, T // 8 - 1), f + col_off))]


def _neighbours(x, prev8, next8, i, bt, S, T):
    r = lax.broadcasted_iota(jnp.int32, x.shape, 0)
    g0 = i * bt
    first_open = jnp.logical_or(g0 == 0, g0 == S)
    last_open = jnp.logical_or(g0 + bt == S, g0 + bt == T)
    before = jnp.where(r == 0, jnp.where(first_open, 0.0, prev8[7:8, :]), pltpu.roll(x, 1, 0))
    after = jnp.where(r == bt - 1, jnp.where(last_open, 0.0, next8[0:1, :]), pltpu.roll(x, bt - 1, 0))
    return before, after


def _sigmoid(a):
    return 1.0 / (1.0 + jnp.exp(-a))


def _ffn_fwd(gu, cw, cb, S, *, name):
    T, F2 = gu.shape
    F = F2 // 2
    bt, bf = _ffn_tiles(T, S, F)

    def body(g_ref, gp_ref, gn_ref, u_ref, w_ref, b_ref, o_ref):
        i = pl.program_id(1)
        g = g_ref[...]
        before, after = _neighbours(g, gp_ref[...], gn_ref[...], i, bt, S, T)
        a = before * w_ref[0:1, :] + g * w_ref[1:2, :] + after * w_ref[2:3, :] + b_ref[...]
        o_ref[...] = (a * _sigmoid(a) * u_ref[...]).astype(o_ref.dtype)

    return pl.pallas_call(
        body, name=name, grid=(F // bf, T // bt),
        in_specs=_halo_specs(T, bt, bf, 0) + [pl.BlockSpec((bt, bf), lambda f, i: (i, f + F // bf)),
                                              pl.BlockSpec((3, bf), lambda f, i: (0, f)),
                                              pl.BlockSpec((1, bf), lambda f, i: (0, f))],
        out_specs=pl.BlockSpec((bt, bf), lambda f, i: (i, f)),
        out_shape=jax.ShapeDtypeStruct((T, F), BF16),
        compiler_params=_cparams(("parallel", "parallel")),
    )(gu, gu, gu, gu, cw, cb)


def _ffn_bwd_act(gu, da_out, cw, cb, S, *, name):
    T, F2 = gu.shape
    F = F2 // 2
    bt, bf = _ffn_tiles(T, S, F)

    def body(g_ref, gp_ref, gn_ref, u_ref, d_ref, w_ref, b_ref, da_ref, du_ref, acc_ref):
        i = pl.program_id(1)

        @pl.when(i == 0)
        def _():
            acc_ref[...] = jnp.zeros_like(acc_ref)

        g = g_ref[...]
        before, after = _neighbours(g, gp_ref[...], gn_ref[...], i, bt, S, T)
        a = before * w_ref[0:1, :] + g * w_ref[1:2, :] + after * w_ref[2:3, :] + b_ref[...]
        sig = _sigmoid(a)
        d = d_ref[...]
        du_ref[...] = (d * (a * sig)).astype(du_ref.dtype)
        da = d * u_ref[...] * (sig * (1.0 + a * (1.0 - sig)))
        da_ref[...] = da
        acc_ref[0:1, :] += jnp.sum(da * before, axis=0, keepdims=True)
        acc_ref[1:2, :] += jnp.sum(da * g, axis=0, keepdims=True)
        acc_ref[2:3, :] += jnp.sum(da * after, axis=0, keepdims=True)
        acc_ref[3:4, :] += jnp.sum(da, axis=0, keepdims=True)

    blk = pl.BlockSpec((bt, bf), lambda f, i: (i, f))
    return pl.pallas_call(
        body, name=name, grid=(F // bf, T // bt),
        in_specs=_halo_specs(T, bt, bf, 0) + [pl.BlockSpec((bt, bf), lambda f, i: (i, f + F // bf)), blk,
                                              pl.BlockSpec((3, bf), lambda f, i: (0, f)),
                                              pl.BlockSpec((1, bf), lambda f, i: (0, f))],
        out_specs=[blk, blk, pl.BlockSpec((8, bf), lambda f, i: (0, f))],
        out_shape=[jax.ShapeDtypeStruct((T, F), F32), jax.ShapeDtypeStruct((T, F), BF16),
                   jax.ShapeDtypeStruct((8, F), F32)],
        compiler_params=_cparams(("parallel", "arbitrary")),
    )(gu, gu, gu, gu, da_out, cw, cb)


def _ffn_bwd_conv(da, cw, S, *, name):
    T, F = da.shape
    bt, bf = _ffn_tiles(T, S, F)

    def body(d_ref, dp_ref, dn_ref, w_ref, o_ref):
        i = pl.program_id(1)
        d = d_ref[...]
        before, after = _neighbours(d, dp_ref[...], dn_ref[...], i, bt, S, T)
        o_ref[...] = (after * w_ref[0:1, :] + d * w_ref[1:2, :] + before * w_ref[2:3, :]).astype(o_ref.dtype)

    return pl.pallas_call(
        body, name=name, grid=(F // bf, T // bt),
        in_specs=_halo_specs(T, bt, bf, 0) + [pl.BlockSpec((3, bf), lambda f, i: (0, f))],
        out_specs=pl.BlockSpec((bt, bf), lambda f, i: (i, f)),
        out_shape=jax.ShapeDtypeStruct((T, F), BF16),
        compiler_params=_cparams(("parallel", "parallel")),
    )(da, da, da, cw)


def _ew_rows(R, N, n_arrays):
    return _pick(R, max(16, (1 << 18) // N), 16)


def _adam(w, g, m, v, *, name):
    R, N = w.shape
    br = _ew_rows(R, N, 7)
    bc1 = 1.0 - ADAM_B1 ** ADAM_STEP
    bc2 = 1.0 - ADAM_B2 ** ADAM_STEP

    def body(w_ref, g_ref, m_ref, v_ref, d_ref, mo_ref, vo_ref):
        gv = g_ref[...]
        mn = ADAM_B1 * m_ref[...] + (1.0 - ADAM_B1) * gv
        vn = ADAM_B2 * v_ref[...] + (1.0 - ADAM_B2) * (gv * gv)
        mo_ref[...] = mn
        vo_ref[...] = vn
        d_ref[...] = -ADAM_LR * ((mn / bc1) / (jnp.sqrt(vn / bc2) + ADAM_EPS) + ADAM_WD * w_ref[...])

    blk = pl.BlockSpec((br, N), lambda i: (i, 0))
    shp = jax.ShapeDtypeStruct((R, N), F32)
    return pl.pallas_call(
        body, name=name, grid=(R // br,),
        in_specs=[blk, blk, blk, blk], out_specs=[blk, blk, blk], out_shape=[shp, shp, shp],
        compiler_params=_cparams(("parallel",)),
    )(w, g, m, v)


def _sum_lead(x, out_dtype, *, name):
    n, R, N = x.shape
    br = _ew_rows(R, N, n + 1)

    def body(x_ref, o_ref):
        acc = x_ref[0].astype(F32)
        for k in range(1, n):
            acc = acc + x_ref[k].astype(F32)
        o_ref[...] = acc.astype(o_ref.dtype)

    return pl.pallas_call(
        body, name=name, grid=(R // br,),
        in_specs=[pl.BlockSpec((n, br, N), lambda i: (0, i, 0))],
        out_specs=pl.BlockSpec((br, N), lambda i: (i, 0)),
        out_shape=jax.ShapeDtypeStruct((R, N), out_dtype),
        compiler_params=_cparams(("parallel",)),
    )(x)


def _sum4_layers(x, *, name):
    L, n, R, N = x.shape
    br = _ew_rows(R, N, n + 1)

    def body(x_ref, o_ref):
        acc = x_ref[0].astype(F32)
        for k in range(1, n):
            acc = acc + x_ref[k].astype(F32)
        o_ref[...] = acc

    return pl.pallas_call(
        body, name=name, grid=(L, R // br),
        in_specs=[pl.BlockSpec((None, n, br, N), lambda l, i: (l, 0, i, 0))],
        out_specs=pl.BlockSpec((None, br, N), lambda l, i: (l, i, 0)),
        out_shape=jax.ShapeDtypeStruct((L, R, N), F32),
        compiler_params=_cparams(("parallel", "parallel")),
    )(x)


def _add_half(g, r, core, *, name):
    L, Q, _, R, N = g.shape
    br = _ew_rows(R, N, 3)

    def body(c_ref, g_ref, r_ref, o_ref):
        o_ref[...] = (g_ref[...] + r_ref[...]).astype(o_ref.dtype)

    return pl.pallas_call(
        body, name=name,
        grid_spec=pltpu.PrefetchScalarGridSpec(
            num_scalar_prefetch=1, grid=(L, Q, R // br),
            in_specs=[pl.BlockSpec((None, None, None, br, N), lambda l, q, i, c_ref: (l, q, c_ref[0], i, 0)),
                      pl.BlockSpec((None, None, br, N), lambda l, q, i, c_ref: (l, q, i, 0))],
            out_specs=pl.BlockSpec((None, None, br, N), lambda l, q, i, c_ref: (l, q, i, 0))),
        out_shape=jax.ShapeDtypeStruct((L, Q, R, N), BF16),
        compiler_params=_cparams(("parallel", "parallel", "parallel")),
    )(core.reshape(1).astype(jnp.int32), g, r)


_ANY = pl.BlockSpec(memory_space=pl.ANY)


def _place():
    return lax.axis_index("x"), lax.axis_index("y"), lax.axis_index("c")


def _allgather8(blocks, *, name):
    n = len(blocks)

    def body(*refs):
        xs, outs = refs[:n], refs[n:2 * n]
        send_sems, recv_sems, local_sems = refs[2 * n:]
        x, y, c = _place()
        me, sibling = (x, y, c), (x, y, 1 - c)
        chips = [(1 - x, y), (x, 1 - y), (1 - x, 1 - y)]

        def slot(a, px, py, pc):
            return outs[a].at[4 * px + 2 * py + pc]

        def copy(a, k, block, to, src=None):
            return pltpu.make_async_remote_copy(
                src_ref=slot(a, *block) if src is None else src, dst_ref=slot(a, *block),
                send_sem=send_sems.at[a, k], recv_sem=recv_sems.at[a, k], device_id=to, device_id_type=MESH)

        mine = [pltpu.make_async_copy(xs[a], slot(a, *me), local_sems.at[a]) for a in range(n)]
        for cp in mine:
            cp.start()
        first = []
        for a in range(n):
            first.append(copy(a, 0, me, sibling, src=xs[a]))
            first += [copy(a, 1 + j, me, (*chip, c), src=xs[a]) for j, chip in enumerate(chips)]
        for cp in first:
            cp.start()
        passed = []
        for j, chip in enumerate(chips):
            for a in range(n):
                copy(a, 1 + j, (*chip, c), me).wait_recv()
                fwd = copy(a, 4 + j, (*chip, c), sibling)
                fwd.start()
                passed.append(fwd)
        for a in range(n):
            copy(a, 0, sibling, me).wait_recv()
            for j, chip in enumerate(chips):
                copy(a, 4 + j, (*chip, 1 - c), me).wait_recv()
        for cp in first + passed:
            cp.wait_send()
        for cp in mine:
            cp.wait()

    return pl.pallas_call(
        body, name=name,
        in_specs=[_ANY] * n, out_specs=[_ANY] * n,
        out_shape=[jax.ShapeDtypeStruct((8,) + b.shape, b.dtype) for b in blocks],
        scratch_shapes=[pltpu.SemaphoreType.DMA((n, 7)), pltpu.SemaphoreType.DMA((n, 7)), pltpu.SemaphoreType.DMA((n,))],
    )(*blocks)


def _send_other_half(gs, *, name):
    n = len(gs)

    def body(*refs):
        xs, outs = refs[:n], refs[n:2 * n]
        send_sems, recv_sems = refs[2 * n:]
        x, y, c = _place()
        cps = [pltpu.make_async_remote_copy(
            src_ref=xs[a].at[:, :, 1 - c], dst_ref=outs[a], send_sem=send_sems.at[a], recv_sem=recv_sems.at[a],
            device_id=(x, y, 1 - c), device_id_type=MESH) for a in range(n)]
        for cp in cps:
            cp.start()
        for cp in cps:
            cp.wait()

    return pl.pallas_call(
        body, name=name,
        in_specs=[_ANY] * n, out_specs=[_ANY] * n,
        out_shape=[jax.ShapeDtypeStruct(g.shape[:2] + g.shape[3:], g.dtype) for g in gs],
        scratch_shapes=[pltpu.SemaphoreType.DMA((n,)), pltpu.SemaphoreType.DMA((n,))],
    )(*gs)


def _chip_scatter(parts, *, name):
    n = len(parts)

    def body(*refs):
        xs, outs = refs[:n], refs[n:2 * n]
        send_sems, recv_sems, local_sems = refs[2 * n:]
        x, y, c = _place()
        q_me = 2 * x + y
        chips = [(1 - x, y), (x, 1 - y), (1 - x, 1 - y)]
        own = [pltpu.make_async_copy(xs[a].at[:, q_me], outs[a].at[:, q_me], local_sems.at[a]) for a in range(n)]
        for cp in own:
            cp.start()
        sends, recvs = [], []
        for a in range(n):
            for k, (px, py) in enumerate(chips):
                q_to = 2 * px + py
                sends.append(pltpu.make_async_remote_copy(
                    src_ref=xs[a].at[:, q_to], dst_ref=outs[a].at[:, q_me], send_sem=send_sems.at[a, k],
                    recv_sem=recv_sems.at[a, k], device_id=(px, py, c), device_id_type=MESH))
                recvs.append(pltpu.make_async_remote_copy(
                    src_ref=xs[a].at[:, q_me], dst_ref=outs[a].at[:, q_to], send_sem=send_sems.at[a, k],
                    recv_sem=recv_sems.at[a, k], device_id=(px, py, c), device_id_type=MESH))
        for cp in sends:
            cp.start()
        for cp in recvs:
            cp.wait_recv()
        for cp in sends:
            cp.wait_send()
        for cp in own:
            cp.wait()

    return pl.pallas_call(
        body, name=name,
        in_specs=[_ANY] * n, out_specs=[_ANY] * n,
        out_shape=[jax.ShapeDtypeStruct(p.shape, p.dtype) for p in parts],
        scratch_shapes=[pltpu.SemaphoreType.DMA((n, 3)), pltpu.SemaphoreType.DMA((n, 3)), pltpu.SemaphoreType.DMA((n,))],
    )(*parts)


def _join_halves(halves, *, name):
    n = len(halves)

    def body(*refs):
        xs, outs = refs[:n], refs[n:2 * n]
        send_sems, recv_sems, local_sems = refs[2 * n:]
        x, y, c = _place()
        own = [pltpu.make_async_copy(xs[a], outs[a].at[:, c], local_sems.at[a]) for a in range(n)]
        for cp in own:
            cp.start()
        sends = [pltpu.make_async_remote_copy(
            src_ref=xs[a], dst_ref=outs[a].at[:, c], send_sem=send_sems.at[a], recv_sem=recv_sems.at[a],
            device_id=(x, y, 1 - c), device_id_type=MESH) for a in range(n)]
        recvs = [pltpu.make_async_remote_copy(
            src_ref=xs[a], dst_ref=outs[a].at[:, 1 - c], send_sem=send_sems.at[a], recv_sem=recv_sems.at[a],
            device_id=(x, y, 1 - c), device_id_type=MESH) for a in range(n)]
        for cp in sends:
            cp.start()
        for cp in recvs:
            cp.wait_recv()
        for cp in sends:
            cp.wait_send()
        for cp in own:
            cp.wait()

    return pl.pallas_call(
        body, name=name,
        in_specs=[_ANY] * n, out_specs=[_ANY] * n,
        out_shape=[jax.ShapeDtypeStruct((h.shape[0], 2) + h.shape[1:], h.dtype) for h in halves],
        scratch_shapes=[pltpu.SemaphoreType.DMA((n,)), pltpu.SemaphoreType.DMA((n,)), pltpu.SemaphoreType.DMA((n,))],
    )(*halves)


def _perm_w_in(w):
    pad = jnp.zeros((w.shape[0], PCOLS - IN_COLS), w.dtype)
    return jnp.concatenate([w[:, :3072], w[:, 3136:IN_COLS], w[:, 3072:3136], pad], axis=1)


def _unperm_w_in(g):
    return jnp.concatenate([g[:, :3072], g[:, 4096:IN_COLS], g[:, 3072:4096]], axis=1)


def _perm_w_uq(w):
    w4 = w.reshape(MLA_Q_LORA, MLA_HEADS, MLA_NOPE + MLA_ROPE)
    nope = w4[:, :, :MLA_NOPE].reshape(MLA_Q_LORA, MLA_HEADS * LANE)
    pe = jnp.pad(w4[:, :, MLA_NOPE:], ((0, 0), (0, 0), (0, LANE - MLA_ROPE))).reshape(MLA_Q_LORA, MLA_HEADS * LANE)
    return jnp.concatenate([nope, pe], axis=1)


def _unperm_w_uq(g):
    nope = g[:, :MLA_HEADS * LANE].reshape(MLA_Q_LORA, MLA_HEADS, LANE)
    pe = g[:, MLA_HEADS * LANE:].reshape(MLA_Q_LORA, MLA_HEADS, LANE)[:, :, :MLA_ROPE]
    return jnp.concatenate([nope, pe], axis=2).reshape(MLA_Q_LORA, MLA_HEADS * (MLA_NOPE + MLA_ROPE))


def _perm_w_ukv(w):
    w4 = w.reshape(MLA_KV_LORA, MLA_HEADS, MLA_NOPE + MLA_V)
    return jnp.concatenate([w4[:, :, :MLA_NOPE].reshape(MLA_KV_LORA, -1), w4[:, :, MLA_NOPE:].reshape(MLA_KV_LORA, -1)], axis=1)


def _unperm_w_ukv(g):
    kn = g[:, :MLA_HEADS * LANE].reshape(MLA_KV_LORA, MLA_HEADS, LANE)
    vv = g[:, MLA_HEADS * LANE:].reshape(MLA_KV_LORA, MLA_HEADS, LANE)
    return jnp.concatenate([kn, vv], axis=2).reshape(MLA_KV_LORA, -1)


def _silu(v):
    return v * jax.nn.sigmoid(v)


def _silu_grad(v):
    s = jax.nn.sigmoid(v)
    return s * (1.0 + v * (1.0 - s))


_WEIGHTS = ("c_ctx", "w_ada", "b_ada", "w_in", "na_rpb", "swa_sink", "mla_q_norm", "mla_kv_norm", "mla_w_uq", "mla_w_ukv",
            "gqa_q_norm", "gqa_k_norm", "w_out", "ln1_g", "ln1_b", "ffn_w_gate", "ffn_w_up", "ffn_conv_w", "ffn_conv_b",
            "ffn_w_down", "ln2_g", "ln2_b")
_COL_SHARDED = ("w_in", "mla_w_uq", "mla_w_ukv", "ffn_w_gate", "ffn_w_up")
_ROW_SHARDED = ("w_out", "ffn_w_down")
_BIG = _COL_SHARDED + _ROW_SHARDED
_SMALL = ("c_ctx", "b_ada", "na_rpb", "swa_sink", "mla_q_norm", "mla_kv_norm", "gqa_q_norm", "gqa_k_norm", "ln1_g", "ln1_b",
          "ffn_conv_w", "ffn_conv_b", "ln2_g", "ln2_b")


def _pack(arrays):
    flat = jnp.concatenate([a.reshape(-1) for a in arrays])
    n = flat.shape[0]
    rows = -(-n // (8 * LANE)) * 8
    return jnp.pad(flat, (0, rows * LANE - n)).reshape(rows, LANE)


def _unpack(packed, like):
    flat = packed.reshape(-1)
    out, o = [], 0
    for a in like:
        out.append(flat[o:o + a.size].reshape(a.shape))
        o += a.size
    return out


def _train_step(x, c, ctx, loss_target, w, m_in, v_in):
    L = DEPTH
    S, D = x.shape[1], x.shape[2]
    C = ctx.shape[1]
    T = S + C
    F = w["ffn_conv_b"].shape[1]
    ax, ay, ac = _place()
    chip = 2 * ax + ay
    dev = 2 * chip + ac
    n_ada = w["w_ada"].shape[2]

    def my_half(a):
        r = a.shape[1] // 2
        return lax.dynamic_slice_in_dim(a, ac * r, r, axis=1).astype(BF16)

    gathered = _allgather8([my_half(w[n]) for n in _BIG] + [w["ffn_conv_w"]], name="gather_weights")
    full = {}
    for n, g in zip(_BIG, gathered[:-1]):
        _, _, r, cols = g.shape
        g = g.reshape(4, 2, L, r, cols)
        if n in _COL_SHARDED:
            full[n] = g.transpose(2, 1, 3, 0, 4).reshape(L, 2 * r, 4 * cols)
        else:
            full[n] = g.transpose(2, 0, 1, 3, 4).reshape(L, 8 * r, cols)
    conv_w = gathered[-1][::2].transpose(1, 2, 0, 3).reshape(L, 3, F)
    w_in_p = [_perm_w_in(full["w_in"][l]) for l in range(L)]
    w_uq_p = [_perm_w_uq(full["mla_w_uq"][l]) for l in range(L)]
    w_ukv_p = [_perm_w_ukv(full["mla_w_ukv"][l]) for l in range(L)]
    w_gu = [jnp.concatenate([full["ffn_w_gate"][l], full["ffn_w_up"][l]], axis=1) for l in range(L)]

    (c_all,) = _allgather8([c], name="gather_c")
    c16 = jnp.concatenate([c_all.reshape(8, D), jnp.broadcast_to(w["c_ctx"][None], (8, D))], axis=0)
    row_keep = (jnp.arange(16) <= 8).astype(F32)[:, None]
    sc = _silu(c16) * row_keep
    b_loc = lax.dynamic_slice_in_dim(w["b_ada"], chip * n_ada, n_ada, axis=1)
    mod_loc = jnp.stack([_mm(sc, w["w_ada"][l], name="mod_mm") + b_loc[l][None] for l in range(L)])
    (mod_g,) = _allgather8([mod_loc], name="gather_mod")
    mod_all = mod_g[::2].transpose(1, 2, 0, 3).reshape(L, 16, 4 * n_ada)
    mod_x = lax.dynamic_index_in_dim(mod_all, dev, axis=1, keepdims=False)
    mod_c = mod_all[:, 8]
    mods = [jnp.stack([mod_x[l].reshape(6, D), mod_c[l].reshape(6, D)], axis=1) for l in range(L)]

    tabs = _rope_tables(S, C, HEAD_DIM) + _rope_tables(S, C, MLA_ROPE)
    swa_mask = _swa_mask(S)
    scale = HEAD_DIM ** -0.5
    cfg_na = _AttnCfg(Hkv=NA_HEADS, G=1, S=S, C=C, band=(2, 5), scale=scale, n_var=5, bias_per_head=True)
    cfg_swa = _AttnCfg(Hkv=SWA_KV_HEADS, G=SWA_HEADS // SWA_KV_HEADS, S=S, C=C, band=(1, 3), scale=scale, n_var=3, has_sink=True)
    cfg_gqa = _AttnCfg(Hkv=GQA_KV_HEADS, G=GQA_HEADS // GQA_KV_HEADS, S=S, C=C, band=None, scale=scale)
    cfg_mla = _AttnCfg(Hkv=MLA_HEADS, G=1, S=S, C=C, band=None, scale=(MLA_NOPE + MLA_ROPE) ** -0.5, two=True)
    row = lambda a: a[None, :]

    xt = jnp.concatenate([x[0], ctx[0]], axis=0)
    saved = []
    for l in range(L):
        md = mods[l]
        gq, gk, mq, mkv = row(w["gqa_q_norm"][l]), row(w["gqa_k_norm"][l]), row(w["mla_q_norm"][l]), row(w["mla_kv_norm"][l])
        h1 = _mod_fwd(xt, md[0], md[1], S, name="mod_fwd")
        p = _mm(h1, w_in_p[l], name="in_proj")
        qkv = _prep_fwd(p, tabs, gq, gk, mq, mkv, S, name="prep_fwd")
        qm = _mm(qkv, w_uq_p[l], a_off=CB_CQ * LANE, a_k=MLA_Q_LORA, tk=LANE, name="mla_uq")
        qmb = _pe_rope(qm, tabs, S, 1.0, BF16, name="mla_q_rope")
        kvm = _mm(qkv, w_ukv_p[l], a_off=CB_CKV * LANE, a_k=MLA_KV_LORA, tk=LANE, out_dtype=BF16, name="mla_ukv")
        bias_na = _na_bias(w["na_rpb"][l], S)
        sink = jnp.broadcast_to(jnp.repeat(w["swa_sink"][l].reshape(SWA_KV_HEADS, -1), BQ, axis=1)[:, :, None],
                                (SWA_KV_HEADS, SWA_HEADS // SWA_KV_HEADS * BQ, LANE))
        oa = _attn_fwd(cfg_na, qkv, CB_NA_Q, qkv, CB_NA_K, qkv, CB_NA_V, bias=bias_na, name="na_fwd")
        ob = _attn_fwd(cfg_swa, qkv, CB_SWA_Q, qkv, CB_SWA_K, qkv, CB_SWA_V, bias=swa_mask, sink=sink, name="swa_fwd")
        oc = _attn_fwd(cfg_mla, qmb, 0, kvm, 0, kvm, MLA_HEADS, q2=qmb, q2_cb=MLA_HEADS, k2=qkv, k2_cb=CB_KPE, name="mla_fwd")
        od = _attn_fwd(cfg_gqa, qkv, CB_GQA_Q, qkv, CB_GQA_K, qkv, CB_GQA_V, name="gqa_fwd")
        mix = jnp.concatenate([oa, ob, oc, od], axis=1)
        z1 = _mm(mix, full["w_out"][l], name="out_proj")
        x1 = _res_fwd(xt, z1, md[2], row(w["ln1_g"][l]), row(w["ln1_b"][l]), S, name="res_fwd")
        h2 = _mod_fwd(x1, md[3], md[4], S, name="mod_fwd")
        gu = _mm(h2, w_gu[l], name="ffn_in")
        act = _ffn_fwd(gu, conv_w[l], row(w["ffn_conv_b"][l]), S, name="ffn_mid")
        z2 = _mm(act, full["ffn_w_down"][l], name="ffn_out")
        x2 = _res_fwd(x1, z2, md[5], row(w["ln2_g"][l]), row(w["ln2_b"][l]), S, name="res_fwd")
        saved.append(dict(x=xt, h1=h1, p=p, qkv=qkv, qmb=qmb, kvm=kvm, bias_na=bias_na, sink=sink, mix=mix, z1=z1, x1=x1,
                          h2=h2, gu=gu, act=act, z2=z2))
        xt = x2

    dx, loss_part = _loss_fwd_bwd(xt, loss_target[0], S, name="loss")
    loss = lax.psum(loss_part[0, 0], ("x", "y", "c"))

    gbuf = {n: None for n in _BIG}
    small = {n: [None] * L for n in ("na_rpb", "swa_sink", "mla_q_norm", "mla_kv_norm", "gqa_q_norm", "gqa_k_norm",
                                     "ln1_g", "ln1_b", "ffn_conv_w", "ffn_conv_b", "ln2_g", "ln2_b")}
    dmod = [None] * L
    for l in reversed(range(L)):
        sv, md = saved[l], mods[l]
        gq, gk, mq, mkv = row(w["gqa_q_norm"][l]), row(w["gqa_k_norm"][l]), row(w["mla_q_norm"][l]), row(w["mla_kv_norm"][l])
        cb_row = row(w["ffn_conv_b"][l])
        dx1, dz2, acc_r2 = _res_bwd(sv["x1"], sv["z2"], md[5], row(w["ln2_g"][l]), dx, S, name="res_bwd")
        dact = _mm(dz2, full["ffn_w_down"][l], mode="nt", name="ffn_out_dx")
        gbuf["ffn_w_down"] = _mm(sv["act"], dz2, mode="tn", stack=(L, l, gbuf["ffn_w_down"]), name="ffn_out_dw")
        da, du, acc_f = _ffn_bwd_act(sv["gu"], dact, conv_w[l], cb_row, S, name="ffn_mid_bwd")
        dg = _ffn_bwd_conv(da, conv_w[l], S, name="ffn_conv_bwd")
        dh2 = _mm(dg, full["ffn_w_gate"][l], mode="nt", name="ffn_in_dx")
        dh2 = _mm(du, full["ffn_w_up"][l], mode="nt", add=dh2, name="ffn_in_dx_add")
        gbuf["ffn_w_gate"] = _mm(sv["h2"], dg, mode="tn", stack=(L, l, gbuf["ffn_w_gate"]), split4=True, name="ffn_in_dw")
        gbuf["ffn_w_up"] = _mm(sv["h2"], du, mode="tn", stack=(L, l, gbuf["ffn_w_up"]), split4=True, name="ffn_in_dw")
        dx1, acc_m2 = _mod_bwd(sv["x1"], dh2, md[4], dx1, S, name="mod_bwd")
        dxa, dz1, acc_r1 = _res_bwd(sv["x"], sv["z1"], md[2], row(w["ln1_g"][l]), dx1, S, name="res_bwd")
        dmix = _mm(dz1, full["w_out"][l], mode="nt", out_dtype=BF16, name="out_proj_dx")
        gbuf["w_out"] = _mm(sv["mix"], dz1, mode="tn", stack=(L, l, gbuf["w_out"]), name="out_proj_dw")

        qkv, qmb, kvm = sv["qkv"], sv["qmb"], sv["kvm"]
        dq_a, dk_a, dv_a, dbias = _attn_bwd(cfg_na, qkv, CB_NA_Q, qkv, CB_NA_K, qkv, CB_NA_V, dmix, 0, bias=sv["bias_na"],
                                            want_dbias=True, name="na_bwd")
        dq_b, dk_b, dv_b, dsink = _attn_bwd(cfg_swa, qkv, CB_SWA_Q, qkv, CB_SWA_K, qkv, CB_SWA_V, dmix, NA_HEADS,
                                            bias=swa_mask, sink=sv["sink"], name="swa_bwd")
        dq_c, dk_c, dv_c, dq2_c, dk2_c = _attn_bwd(cfg_mla, qmb, 0, kvm, 0, kvm, MLA_HEADS, dmix, NA_HEADS + SWA_HEADS,
                                                   q2=qmb, q2_cb=MLA_HEADS, k2=qkv, k2_cb=CB_KPE, name="mla_bwd")
        dq_d, dk_d, dv_d = _attn_bwd(cfg_gqa, qkv, CB_GQA_Q, qkv, CB_GQA_K, qkv, CB_GQA_V, dmix,
                                     NA_HEADS + SWA_HEADS + MLA_HEADS, name="gqa_bwd")
        dqm = _pe_rope(jnp.concatenate([dq_c, dq2_c], axis=1), tabs, S, -1.0, BF16, name="mla_q_rope_bwd")
        dkvm = jnp.concatenate([dk_c, dv_c], axis=1).astype(BF16)
        dcq = _mm(dqm, w_uq_p[l], mode="nt", name="mla_uq_dx")
        dckv = _mm(dkvm, w_ukv_p[l], mode="nt", name="mla_ukv_dx")
        cqn = qkv[:, CB_CQ * LANE:CB_CKV * LANE]
        ckvn = qkv[:, CB_CKV * LANE:(CB_CKV + 1) * LANE]
        d_uq = _unperm_w_uq(_mm(cqn, dqm, mode="tn", name="mla_uq_dw"))
        d_ukv = _unperm_w_ukv(_mm(ckvn, dkvm, mode="tn", name="mla_ukv_dw"))
        grads = {}
        for h in range(NA_HEADS):
            grads[CB_NA_Q + h], grads[CB_NA_K + h], grads[CB_NA_V + h] = (dq_a, h), (dk_a, h), (dv_a, h)
        for h in range(SWA_HEADS):
            grads[CB_SWA_Q + h] = (dq_b, h)
        for h in range(SWA_KV_HEADS):
            grads[CB_SWA_K + h], grads[CB_SWA_V + h] = (dk_b, h), (dv_b, h)
        for h in range(GQA_HEADS):
            grads[CB_GQA_Q + h] = (dq_d, h)
        for h in range(GQA_KV_HEADS):
            grads[CB_GQA_K + h], grads[CB_GQA_V + h] = (dk_d, h), (dv_d, h)
        grads[CB_KPE], grads[CB_CQ], grads[CB_CKV] = (dk2_c, 0), (dcq, 0), (dckv, 0)
        dp, acc_p = _prep_bwd(sv["p"], grads, tabs, gq, gk, mq, mkv, S, name="prep_bwd")
        dh1 = _mm(dp, w_in_p[l], mode="nt", name="in_proj_dx")
        d_in = _unperm_w_in(_mm(sv["h1"], dp, mode="tn", name="in_proj_dw"))
        dx, acc_m1 = _mod_bwd(sv["x"], dh1, md[1], dxa, S, name="mod_bwd")

        to4 = lambda g: g.reshape(g.shape[0], 4, g.shape[1] // 4).transpose(1, 0, 2)
        for n, g in (("w_in", d_in), ("mla_w_uq", d_uq), ("mla_w_ukv", d_ukv)):
            g4 = to4(g)[None]
            gbuf[n] = g4 if gbuf[n] is None else jnp.concatenate([g4, gbuf[n]], axis=0)
        dmod[l] = jnp.stack([acc_m1[0:2], acc_m1[2:4], acc_r1[0:2], acc_m2[0:2], acc_m2[2:4], acc_r2[0:2]])
        rpb_vjp = jax.vjp(lambda r: _na_bias(r, S), w["na_rpb"][l])[1]
        small["na_rpb"][l] = rpb_vjp(dbias)[0]
        small["swa_sink"][l] = dsink[:, :, 0].reshape(SWA_KV_HEADS, -1, BQ).sum(axis=-1).reshape(-1)
        small["gqa_q_norm"][l], small["gqa_k_norm"][l] = acc_p[0, :LANE], acc_p[1, :LANE]
        small["mla_q_norm"][l], small["mla_kv_norm"][l] = acc_p[2], acc_p[3, :LANE]
        small["ln1_g"][l], small["ln1_b"][l] = acc_r1[2], acc_r1[3]
        small["ln2_g"][l], small["ln2_b"][l] = acc_r2[2], acc_r2[3]
        small["ffn_conv_w"][l], small["ffn_conv_b"][l] = acc_f[0:3], acc_f[3]
    grad_x = dx[:S][None]

    dmod_x = jnp.stack([dmod[l][:, 0].reshape(-1) for l in range(L)])
    dmod_c = jnp.stack([dmod[l][:, 1].reshape(-1) for l in range(L)])
    small_names = tuple(small)
    bucket = [dmod_x, dmod_c] + [jnp.stack(small[n]) for n in small_names]
    (b8,) = _allgather8([_pack(bucket)], name="gather_small")
    tot = _unpack(_sum_lead(b8, F32, name="sum_small"), bucket)
    dmod_x_all = b8.reshape(8, -1)[:, :dmod_x.size].reshape(8, L, 6 * D)
    dmod_c_tot = tot[1]
    g_small = dict(zip(small_names, tot[2:]))
    g_small["b_ada"] = tot[0] + dmod_c_tot
    g_small["ffn_conv_w"] = lax.dynamic_slice_in_dim(g_small["ffn_conv_w"], chip * (F // 4), F // 4, axis=2)

    dmod16 = jnp.concatenate([dmod_x_all, jnp.broadcast_to(dmod_c_tot[None], (8, L, 6 * D))], axis=0) * row_keep[:, :, None]
    dmod16 = lax.dynamic_slice_in_dim(dmod16, chip * n_ada, n_ada, axis=2)
    g_ada, dsc = None, None
    for l in range(L):
        g_ada = _mm(sc, dmod16[:, l], mode="tn", exact=True, stack=(L, l, g_ada), name="ada_dw")
        dsc = _mm(dmod16[:, l], w["w_ada"][l], mode="nt", add=dsc, name="ada_dx" if dsc is None else "ada_dx_add")
    (dsc8,) = _allgather8([dsc[8:16]], name="gather_dsc")
    dsc4 = dsc8[::2, 0]
    g_small["c_ctx"] = (((dsc4[0] + dsc4[1]) + dsc4[2]) + dsc4[3]) * _silu_grad(w["c_ctx"])

    def halves_view(g):
        return g.reshape(g.shape[0], 4, 2, g.shape[2] // 2, g.shape[3])

    views = []
    for n in _BIG:
        g = gbuf[n]
        if n in _ROW_SHARDED:
            g = g.reshape(L, 4, g.shape[1] // 4, g.shape[2])
        views.append(halves_view(g))
    from_sibling = _send_other_half(views, name="rs_core_exchange")
    parts = [_add_half(g, r, ac, name="rs_core_add") for g, r in zip(views, from_sibling)]
    scattered = _chip_scatter(parts, name="rs_chip_scatter")
    halves = [_sum4_layers(s, name="rs_chip_sum") for s in scattered]
    joined = _join_halves(halves, name="rs_join")
    g_big = {n: j.reshape(L, 2 * j.shape[2], j.shape[3]) for n, j in zip(_BIG, joined)}
    g_big["w_ada"] = g_ada

    grad, delta, new_m, new_v = {}, {}, {}, {}
    for n in _BIG + ("w_ada",):
        shp = w[n].shape
        flat = lambda a: a.reshape(shp[0] * shp[1], shp[2])
        d_, m_, v_ = _adam(flat(w[n]), flat(g_big[n]), flat(m_in[n]), flat(v_in[n]), name="adam")
        grad[n], delta[n], new_m[n], new_v[n] = g_big[n], d_.reshape(shp), m_.reshape(shp), v_.reshape(shp)
    like = [w[n] for n in _SMALL]
    packed = [_pack([src[n].reshape(w[n].shape) for n in _SMALL]) for src in (w, g_small, m_in, v_in)]
    d_s, m_s, v_s = _adam(*packed, name="adam_small")
    for n, g_, d_, m_, v_ in zip(_SMALL, _unpack(packed[1], like), _unpack(d_s, like), _unpack(m_s, like), _unpack(v_s, like)):
        grad[n], delta[n], new_m[n], new_v[n] = g_, d_, m_, v_

    return (loss, grad_x, *[grad[n] for n in _WEIGHTS], *[delta[n] for n in _WEIGHTS],
            *[new_m[n] for n in _WEIGHTS], *[new_v[n] for n in _WEIGHTS])


def kernel(x, c, ctx, c_ctx, w_ada, b_ada, w_in, na_rpb, swa_sink, mla_q_norm, mla_kv_norm, mla_w_uq, mla_w_ukv, gqa_q_norm, gqa_k_norm, w_out, ln1_g, ln1_b, ffn_w_gate, ffn_w_up, ffn_conv_w, ffn_conv_b, ffn_w_down, ln2_g, ln2_b, loss_target, m_c_ctx, m_w_ada, m_b_ada, m_w_in, m_na_rpb, m_swa_sink, m_mla_q_norm, m_mla_kv_norm, m_mla_w_uq, m_mla_w_ukv, m_gqa_q_norm, m_gqa_k_norm, m_w_out, m_ln1_g, m_ln1_b, m_ffn_w_gate, m_ffn_w_up, m_ffn_conv_w, m_ffn_conv_b, m_ffn_w_down, m_ln2_g, m_ln2_b, v_c_ctx, v_w_ada, v_b_ada, v_w_in, v_na_rpb, v_swa_sink, v_mla_q_norm, v_mla_kv_norm, v_mla_w_uq, v_mla_w_ukv, v_gqa_q_norm, v_gqa_k_norm, v_w_out, v_ln1_g, v_ln1_b, v_ffn_w_gate, v_ffn_w_up, v_ffn_conv_w, v_ffn_conv_b, v_ffn_w_down, v_ln2_g, v_ln2_b):
    args = locals()
    w = {n: args[n] for n in _WEIGHTS}
    m_in = {n: args["m_" + n] for n in _WEIGHTS}
    v_in = {n: args["v_" + n] for n in _WEIGHTS}
    return _train_step(x, c, ctx, loss_target, w, m_in, v_in)
```

```python
import functools
import math

import numpy as np
import jax
import jax.numpy as jnp
from jax import lax
from jax.experimental import pallas as pl
from jax.experimental.pallas import tpu as pltpu

F32 = jnp.float32
BF16 = jnp.bfloat16
MESH = pl.DeviceIdType.MESH

GRID_W = 64
HEAD_DIM = 128
NA_HEADS, NA_WIN_R, NA_WIN_C = 4, 8, 16
SWA_HEADS, SWA_KV_HEADS, SWA_WINDOW = 4, 2, 128
MLA_HEADS, MLA_Q_LORA, MLA_KV_LORA, MLA_NOPE, MLA_ROPE, MLA_V = 4, 384, 128, 128, 64, 128
GQA_HEADS, GQA_KV_HEADS = 4, 2
ROPE_THETA = 10000.0
EPS = 1e-6
NEG = -1e30
DEPTH = 2
DEEPNORM_ALPHA = (2 * DEPTH) ** 0.25
ADAM_LR, ADAM_B1, ADAM_B2, ADAM_EPS, ADAM_WD, ADAM_STEP = 0.001, 0.9, 0.999, 1e-08, 0.01, 10

LANE = 128
V7X_VMEM_BYTES = 64 * 1024 * 1024
VMEM_LIMIT = 56 * 1024 * 1024
MM_VMEM_BUDGET = 40 * 1024 * 1024
BQ = 128

CB_NA_Q, CB_NA_K, CB_NA_V = 0, 4, 8
CB_SWA_Q, CB_SWA_K, CB_SWA_V = 12, 16, 18
CB_CQ, CB_CKV = 20, 23
CB_GQA_Q, CB_GQA_K, CB_GQA_V = 24, 28, 30
CB_KPE = 32
PCOLS = 33 * LANE
IN_COLS = 4160


def _cparams(sem=None, **kw):
    return pltpu.CompilerParams(dimension_semantics=sem, vmem_limit_bytes=VMEM_LIMIT, **kw)


def _pick(n, target, mult=LANE):
    best = None
    for d in range(mult, min(n, target) + 1, mult):
        if n % d == 0:
            best = d
    return n if best is None else best


def _mm(a, b, *, mode="nn", out_dtype=F32, a_off=0, a_k=None, tm=1408, tn=1408, tk=2816, exact=False, add=None,
        stack=None, split4=False, name):
    if mode == "tn":
        K, M = a.shape
        K2, N = b.shape
    elif mode == "nn":
        M, K = a.shape
        K2, N = b.shape
    else:
        M, K = a.shape
        N, K2 = b.shape
    if a_k is not None:
        K = a_k
    assert K == K2, (a.shape, b.shape, mode)
    m_mult = LANE if mode == "tn" else 16
    n_cols = N // 4 if split4 else N
    bm, bn, bk = _pick(M, tm, m_mult), _pick(n_cols, tn), _pick(K, tk)
    sa, sb, so = a.dtype.itemsize, b.dtype.itemsize, jnp.dtype(out_dtype).itemsize

    def vmem_estimate():
        acc = bm * bn * 4 if K // bk > 1 else 0
        return 2 * (bm * bk * sa + bk * bn * sb) + acc + 2 * bm * bn * so + (2 * bm * bn * 4 if add is not None else 0)

    while vmem_estimate() > MM_VMEM_BUDGET:
        if bm >= bn and _pick(M, bm - 1, m_mult) < bm:
            bm = _pick(M, bm - 1, m_mult)
        elif _pick(n_cols, bn - 1) < bn:
            bn = _pick(n_cols, bn - 1)
        else:
            assert _pick(K, bk - 1) < bk, "no tiling fits VMEM"
            bk = _pick(K, bk - 1)
    assert a_off % bk == 0
    koff = a_off // bk
    nk = K // bk
    if mode == "tn":
        a_spec = pl.BlockSpec((bk, bm), lambda i, j, k: (k, i))
        b_spec = pl.BlockSpec((bk, bn), lambda i, j, k: (k, j))
        dims = (((0,), (0,)), ((), ()))
    elif mode == "nn":
        a_spec = pl.BlockSpec((bm, bk), lambda i, j, k: (i, k + koff))
        b_spec = pl.BlockSpec((bk, bn), lambda i, j, k: (k, j))
        dims = (((1,), (0,)), ((), ()))
    else:
        a_spec = pl.BlockSpec((bm, bk), lambda i, j, k: (i, k + koff))
        b_spec = pl.BlockSpec((bn, bk), lambda i, j, k: (j, k))
        dims = (((1,), (1,)), ((), ()))

    operands = [a, b]
    in_specs = [a_spec, b_spec]
    if add is not None:
        operands.append(add)
        in_specs.append(pl.BlockSpec((bm, bn), lambda i, j, k: (i, j)))
    aliases = {}
    if stack is None:
        out_spec = pl.BlockSpec((bm, bn), lambda i, j, k: (i, j))
        out_shape = jax.ShapeDtypeStruct((M, N), out_dtype)
    else:
        n_layers, layer, buf = stack
        if split4:
            nb = N // 4 // bn
            assert N % (4 * bn) == 0
            out_spec = pl.BlockSpec((None, None, bm, bn), lambda i, j, k: (layer, j // nb, i, j % nb))
            out_shape = jax.ShapeDtypeStruct((n_layers, 4, M, N // 4), out_dtype)
        else:
            out_spec = pl.BlockSpec((None, bm, bn), lambda i, j, k: (layer, i, j))
            out_shape = jax.ShapeDtypeStruct((n_layers, M, N), out_dtype)
        if buf is not None:
            aliases = {len(operands): 0}
            operands.append(buf)
            in_specs.append(pl.BlockSpec(memory_space=pl.ANY))
    has_add, has_buf = add is not None, bool(aliases)

    def body(*refs):
        a_ref, b_ref = refs[:2]
        add_ref = refs[2] if has_add else None
        o_ref = refs[2 + has_add + has_buf]
        if exact:
            prod = lax.dot_general(a_ref[...].astype(F32), b_ref[...].astype(F32), dims,
                                   precision=lax.Precision.HIGHEST, preferred_element_type=F32)
        else:
            prod = lax.dot_general(a_ref[...].astype(BF16), b_ref[...].astype(BF16), dims, preferred_element_type=F32)

        def finish(res):
            if has_add:
                res = res + add_ref[...].astype(F32)
            o_ref[...] = res.astype(o_ref.dtype)

        if nk == 1:
            finish(prod)
            return
        acc_ref = refs[-1]
        k = pl.program_id(2)

        @pl.when(k == 0)
        def _():
            acc_ref[...] = prod

        @pl.when((k > 0) & (k < nk - 1))
        def _():
            acc_ref[...] += prod

        @pl.when(k == nk - 1)
        def _():
            finish(acc_ref[...] + prod)

    return pl.pallas_call(
        body, name=name,
        grid=(M // bm, N // bn, nk),
        in_specs=in_specs, out_specs=out_spec, out_shape=out_shape,
        scratch_shapes=[pltpu.VMEM((bm, bn), F32)] if nk > 1 else [],
        input_output_aliases=aliases,
        compiler_params=_cparams(("parallel", "parallel", "arbitrary")),
    )(*operands)


def _row_block(T, S):
    return _pick(math.gcd(T, S), 256, 16)


def _ln_stats(x):
    mu = jnp.mean(x, axis=-1, keepdims=True)
    xc = x - mu
    var = jnp.mean(xc * xc, axis=-1, keepdims=True)
    rstd = lax.rsqrt(var + EPS)
    return xc * rstd, rstd


def _ln_bwd(dxhat, xhat, rstd):
    m1 = jnp.mean(dxhat, axis=-1, keepdims=True)
    m2 = jnp.mean(dxhat * xhat, axis=-1, keepdims=True)
    return rstd * (dxhat - m1 - xhat * m2)


def _sel(ref, is_ctx):
    return jnp.where(is_ctx, ref[1:2, :], ref[0:1, :])


def _mod_fwd(x, shift, scale, S, *, name):
    T, D = x.shape
    bt = _row_block(T, S)

    def body(x_ref, sh_ref, sc_ref, o_ref):
        is_ctx = pl.program_id(0) * bt >= S
        xhat, _ = _ln_stats(x_ref[...])
        o_ref[...] = (xhat * (1.0 + _sel(sc_ref, is_ctx)) + _sel(sh_ref, is_ctx)).astype(o_ref.dtype)

    return pl.pallas_call(
        body, name=name, grid=(T // bt,),
        in_specs=[pl.BlockSpec((bt, D), lambda i: (i, 0)), pl.BlockSpec((2, D), lambda i: (0, 0)),
                  pl.BlockSpec((2, D), lambda i: (0, 0))],
        out_specs=pl.BlockSpec((bt, D), lambda i: (i, 0)),
        out_shape=jax.ShapeDtypeStruct((T, D), BF16),
        compiler_params=_cparams(("parallel",)),
    )(x, shift, scale)


def _acc_groups(acc_ref, row, val, is_ctx):
    f = jnp.where(is_ctx, 1.0, 0.0).astype(F32)
    acc_ref[row:row + 1, :] += val * (1.0 - f)
    acc_ref[row + 1:row + 2, :] += val * f


def _mod_bwd(x, dh, scale, dx_in, S, *, name):
    T, D = x.shape
    bt = _row_block(T, S)

    def body(x_ref, dh_ref, sc_ref, dxi_ref, dx_ref, acc_ref):
        i = pl.program_id(0)
        is_ctx = i * bt >= S

        @pl.when(i == 0)
        def _():
            acc_ref[...] = jnp.zeros_like(acc_ref)

        xhat, rstd = _ln_stats(x_ref[...])
        dh = dh_ref[...].astype(F32)
        dxhat = dh * (1.0 + _sel(sc_ref, is_ctx))
        dx_ref[...] = dxi_ref[...] + _ln_bwd(dxhat, xhat, rstd)
        _acc_groups(acc_ref, 0, jnp.sum(dh, axis=0, keepdims=True), is_ctx)
        _acc_groups(acc_ref, 2, jnp.sum(dh * xhat, axis=0, keepdims=True), is_ctx)

    return pl.pallas_call(
        body, name=name, grid=(T // bt,),
        in_specs=[pl.BlockSpec((bt, D), lambda i: (i, 0)), pl.BlockSpec((bt, D), lambda i: (i, 0)),
                  pl.BlockSpec((2, D), lambda i: (0, 0)), pl.BlockSpec((bt, D), lambda i: (i, 0))],
        out_specs=[pl.BlockSpec((bt, D), lambda i: (i, 0)), pl.BlockSpec((8, D), lambda i: (0, 0))],
        out_shape=[jax.ShapeDtypeStruct((T, D), F32), jax.ShapeDtypeStruct((8, D), F32)],
        compiler_params=_cparams(("arbitrary",)),
    )(x, dh, scale, dx_in)


def _res_fwd(x, z, gate, lg, lb, S, *, name):
    T, D = x.shape
    bt = _row_block(T, S)

    def body(x_ref, z_ref, g_ref, lg_ref, lb_ref, o_ref):
        is_ctx = pl.program_id(0) * bt >= S
        u = DEEPNORM_ALPHA * x_ref[...] + _sel(g_ref, is_ctx) * z_ref[...]
        uhat, _ = _ln_stats(u)
        o_ref[...] = uhat * lg_ref[...] + lb_ref[...]

    row = pl.BlockSpec((bt, D), lambda i: (i, 0))
    return pl.pallas_call(
        body, name=name, grid=(T // bt,),
        in_specs=[row, row, pl.BlockSpec((2, D), lambda i: (0, 0)), pl.BlockSpec((1, D), lambda i: (0, 0)),
                  pl.BlockSpec((1, D), lambda i: (0, 0))],
        out_specs=row,
        out_shape=jax.ShapeDtypeStruct((T, D), F32),
        compiler_params=_cparams(("parallel",)),
    )(x, z, gate, lg, lb)


def _res_bwd(x, z, gate, lg, dy, S, *, name):
    T, D = x.shape
    bt = _row_block(T, S)

    def body(x_ref, z_ref, g_ref, lg_ref, dy_ref, dx_ref, dz_ref, acc_ref):
        i = pl.program_id(0)
        is_ctx = i * bt >= S

        @pl.when(i == 0)
        def _():
            acc_ref[...] = jnp.zeros_like(acc_ref)

        gate_v = _sel(g_ref, is_ctx)
        zv = z_ref[...]
        u = DEEPNORM_ALPHA * x_ref[...] + gate_v * zv
        uhat, rstd = _ln_stats(u)
        dyv = dy_ref[...]
        du = _ln_bwd(dyv * lg_ref[...], uhat, rstd)
        dx_ref[...] = DEEPNORM_ALPHA * du
        dz_ref[...] = (gate_v * du).astype(dz_ref.dtype)
        _acc_groups(acc_ref, 0, jnp.sum(du * zv, axis=0, keepdims=True), is_ctx)
        acc_ref[2:3, :] += jnp.sum(dyv * uhat, axis=0, keepdims=True)
        acc_ref[3:4, :] += jnp.sum(dyv, axis=0, keepdims=True)

    row = pl.BlockSpec((bt, D), lambda i: (i, 0))
    return pl.pallas_call(
        body, name=name, grid=(T // bt,),
        in_specs=[row, row, pl.BlockSpec((2, D), lambda i: (0, 0)), pl.BlockSpec((1, D), lambda i: (0, 0)), row],
        out_specs=[row, row, pl.BlockSpec((8, D), lambda i: (0, 0))],
        out_shape=[jax.ShapeDtypeStruct((T, D), F32), jax.ShapeDtypeStruct((T, D), BF16),
                   jax.ShapeDtypeStruct((8, D), F32)],
        compiler_params=_cparams(("arbitrary",)),
    )(x, z, gate, lg, dy)


def _loss_fwd_bwd(y, target, S, *, name):
    T, D = y.shape
    bt = _row_block(T, S)
    n_lat = S // bt

    def body(y_ref, t_ref, dy_ref, l_ref):
        i = pl.program_id(0)

        @pl.when(i == 0)
        def _():
            l_ref[...] = jnp.zeros_like(l_ref)

        keep = jnp.where(i * bt >= S, 0.0, 1.0).astype(F32)
        err = (y_ref[...] - t_ref[...]) * keep
        dy_ref[...] = err * (1.0 / D)
        l_ref[...] += jnp.sum(err * err) * (0.5 / D)

    return pl.pallas_call(
        body, name=name, grid=(T // bt,),
        in_specs=[pl.BlockSpec((bt, D), lambda i: (i, 0)),
                  pl.BlockSpec((bt, D), lambda i: (jnp.minimum(i, n_lat - 1), 0))],
        out_specs=[pl.BlockSpec((bt, D), lambda i: (i, 0)), pl.BlockSpec((8, LANE), lambda i: (0, 0))],
        out_shape=[jax.ShapeDtypeStruct((T, D), F32), jax.ShapeDtypeStruct((8, LANE), F32)],
        compiler_params=_cparams(("arbitrary",)),
    )(y, target)


def _rope_tables(S, C, dim):
    half = dim // 4
    t = jnp.arange(S)
    row = (t // GRID_W).astype(F32)
    col = (t % GRID_W).astype(F32)
    inv = ROPE_THETA ** (-jnp.arange(half, dtype=F32) / half)
    ar, ac = row[:, None] * inv[None, :], col[:, None] * inv[None, :]
    cos = jnp.concatenate([jnp.cos(ar), jnp.cos(ar), jnp.cos(ac), jnp.cos(ac)], axis=1)
    ss = jnp.concatenate([-jnp.sin(ar), jnp.sin(ar), -jnp.sin(ac), jnp.sin(ac)], axis=1)
    cos = jnp.pad(cos, ((0, C), (0, LANE - dim)), constant_values=1.0)
    ss = jnp.pad(ss, ((0, C), (0, LANE - dim)))
    return cos, ss


def _rope(x, cos, ss, half):
    lane = lax.broadcasted_iota(jnp.int32, x.shape, 1)
    first = (lane % (2 * half)) < half
    partner = jnp.where(first, pltpu.roll(x, LANE - half, 1), pltpu.roll(x, half, 1))
    return x * cos + partner * ss


def _rms(x):
    r = lax.rsqrt(jnp.mean(x * x, axis=-1, keepdims=True) + EPS)
    return x * r, r


_CAST_BLOCKS = tuple(range(0, 12)) + (18, 19, 30, 31)
_ROPE_BLOCKS = tuple(range(12, 18))
_GQA_Q_BLOCKS = tuple(range(24, 28))
_GQA_K_BLOCKS = (28, 29)


def _prep_fwd(p, tabs, gq, gk, mq, mkv, S, *, name):
    T = p.shape[0]
    bt = _row_block(T, S)
    cA, sA, cP, sP = tabs

    def body(p_ref, cA_ref, sA_ref, cP_ref, sP_ref, gq_ref, gk_ref, mq_ref, mkv_ref, o_ref):
        def blk(b):
            return p_ref[:, b * LANE:(b + 1) * LANE]

        def put(b, val):
            o_ref[:, b * LANE:(b + 1) * LANE] = val.astype(o_ref.dtype)

        cA_v, sA_v = cA_ref[...], sA_ref[...]
        for b in _CAST_BLOCKS:
            put(b, blk(b))
        for b in _ROPE_BLOCKS:
            put(b, _rope(blk(b), cA_v, sA_v, 32))
        for b in _GQA_Q_BLOCKS:
            put(b, _rope(_rms(blk(b))[0] * gq_ref[...], cA_v, sA_v, 32))
        for b in _GQA_K_BLOCKS:
            put(b, _rope(_rms(blk(b))[0] * gk_ref[...], cA_v, sA_v, 32))
        put(CB_KPE, _rope(blk(CB_KPE), cP_ref[...], sP_ref[...], 16))
        cq = p_ref[:, CB_CQ * LANE:CB_CKV * LANE]
        o_ref[:, CB_CQ * LANE:CB_CKV * LANE] = (_rms(cq)[0] * mq_ref[...]).astype(o_ref.dtype)
        put(CB_CKV, _rms(blk(CB_CKV))[0] * mkv_ref[...])

    row128 = pl.BlockSpec((bt, LANE), lambda i: (i, 0))
    vec = lambda n: pl.BlockSpec((1, n), lambda i: (0, 0))
    return pl.pallas_call(
        body, name=name, grid=(T // bt,),
        in_specs=[pl.BlockSpec((bt, PCOLS), lambda i: (i, 0)), row128, row128, row128, row128,
                  vec(LANE), vec(LANE), vec(MLA_Q_LORA), vec(LANE)],
        out_specs=pl.BlockSpec((bt, PCOLS), lambda i: (i, 0)),
        out_shape=jax.ShapeDtypeStruct((T, PCOLS), BF16),
        compiler_params=_cparams(("parallel",)),
    )(p, cA, sA, cP, sP, gq, gk, mq, mkv)


def _prep_bwd(p, grads, tabs, gq, gk, mq, mkv, S, *, name):
    T = p.shape[0]
    bt = _row_block(T, S)
    cA, sA, cP, sP = tabs
    arrays = []
    where = {}
    for key, (arr, cb) in grads.items():
        idx = next((n for n, a in enumerate(arrays) if a is arr), None)
        if idx is None:
            arrays.append(arr)
            idx = len(arrays) - 1
        where[key] = (idx, cb)
    ng = len(arrays)

    def body(*refs):
        p_ref, cA_ref, sA_ref, cP_ref, sP_ref, gq_ref, gk_ref, mq_ref, mkv_ref = refs[:9]
        g_refs = refs[9:9 + ng]
        o_ref, acc_ref = refs[9 + ng:]
        i = pl.program_id(0)

        @pl.when(i == 0)
        def _():
            acc_ref[...] = jnp.zeros_like(acc_ref)

        def blk(b):
            return p_ref[:, b * LANE:(b + 1) * LANE]

        def grad(b, width=LANE):
            idx, cb = where[b]
            return g_refs[idx][:, cb * LANE:cb * LANE + width].astype(F32)

        def put(b, val):
            o_ref[:, b * LANE:(b + 1) * LANE] = val.astype(o_ref.dtype)

        def rms_bwd(x, dy, g, row, width):
            n, r = _rms(x)
            acc_ref[row:row + 1, 0:width] += jnp.sum(dy * n, axis=0, keepdims=True)
            dn = dy * g
            return r * (dn - n * jnp.mean(dn * n, axis=-1, keepdims=True))

        cA_v, sA_v = cA_ref[...], sA_ref[...]
        for b in _CAST_BLOCKS:
            put(b, grad(b))
        for b in _ROPE_BLOCKS:
            put(b, _rope(grad(b), cA_v, -sA_v, 32))
        for b in _GQA_Q_BLOCKS:
            put(b, rms_bwd(blk(b), _rope(grad(b), cA_v, -sA_v, 32), gq_ref[...], 0, LANE))
        for b in _GQA_K_BLOCKS:
            put(b, rms_bwd(blk(b), _rope(grad(b), cA_v, -sA_v, 32), gk_ref[...], 1, LANE))
        put(CB_KPE, _rope(grad(CB_KPE), cP_ref[...], -sP_ref[...], 16))
        dcq = rms_bwd(p_ref[:, CB_CQ * LANE:CB_CKV * LANE], grad(CB_CQ, MLA_Q_LORA), mq_ref[...], 2, MLA_Q_LORA)
        o_ref[:, CB_CQ * LANE:CB_CKV * LANE] = dcq.astype(o_ref.dtype)
        put(CB_CKV, rms_bwd(blk(CB_CKV), grad(CB_CKV), mkv_ref[...], 3, LANE))

    row128 = pl.BlockSpec((bt, LANE), lambda i: (i, 0))
    vec = lambda n: pl.BlockSpec((1, n), lambda i: (0, 0))
    g_specs = [pl.BlockSpec((bt, a.shape[1]), lambda i: (i, 0)) for a in arrays]
    return pl.pallas_call(
        body, name=name, grid=(T // bt,),
        in_specs=[pl.BlockSpec((bt, PCOLS), lambda i: (i, 0)), row128, row128, row128, row128,
                  vec(LANE), vec(LANE), vec(MLA_Q_LORA), vec(LANE)] + g_specs,
        out_specs=[pl.BlockSpec((bt, PCOLS), lambda i: (i, 0)), pl.BlockSpec((8, MLA_Q_LORA), lambda i: (0, 0))],
        out_shape=[jax.ShapeDtypeStruct((T, PCOLS), BF16), jax.ShapeDtypeStruct((8, MLA_Q_LORA), F32)],
        compiler_params=_cparams(("arbitrary",)),
    )(p, cA, sA, cP, sP, gq, gk, mq, mkv, *arrays)


def _pe_rope(qm, tabs, S, sign, out_dtype, *, name):
    T, N = qm.shape
    bt = _row_block(T, S)
    cP, sP = tabs[2], tabs[3]

    def body(x_ref, c_ref, s_ref, o_ref):
        for b in range(MLA_HEADS):
            o_ref[:, b * LANE:(b + 1) * LANE] = x_ref[:, b * LANE:(b + 1) * LANE].astype(o_ref.dtype)
        for b in range(MLA_HEADS, 2 * MLA_HEADS):
            x = x_ref[:, b * LANE:(b + 1) * LANE].astype(F32)
            o_ref[:, b * LANE:(b + 1) * LANE] = _rope(x, c_ref[...], sign * s_ref[...], 16).astype(o_ref.dtype)

    row128 = pl.BlockSpec((bt, LANE), lambda i: (i, 0))
    return pl.pallas_call(
        body, name=name, grid=(T // bt,),
        in_specs=[pl.BlockSpec((bt, N), lambda i: (i, 0)), row128, row128],
        out_specs=pl.BlockSpec((bt, N), lambda i: (i, 0)),
        out_shape=jax.ShapeDtypeStruct((T, N), out_dtype),
        compiler_params=_cparams(("parallel",)),
    )(qm, cP, sP)


def _dot_nt(a, b):
    return lax.dot_general(a, b, (((1,), (1,)), ((), ())), preferred_element_type=F32)


def _dot_tn(a, b):
    return lax.dot_general(a, b, (((0,), (0,)), ((), ())), preferred_element_type=F32)


def _dot(a, b):
    return jnp.dot(a, b, preferred_element_type=F32)


def _window_fns(band, S, n_var):
    n_lat = S // BQ
    if band is None:
        return None
    reach, span = band

    def fns(j):
        start = jnp.clip(j - reach, 0, n_lat - span)
        return start, jnp.clip(j - start, 0, n_var - 1)

    return fns


class _AttnCfg:
    def __init__(self, *, Hkv, G, S, C, band, scale, n_var=0, bias_per_head=False, has_sink=False, two=False):
        self.Hkv, self.G, self.S, self.C, self.band, self.scale = Hkv, G, S, C, band, scale
        self.n_var, self.bias_per_head, self.has_sink, self.two = n_var, bias_per_head, has_sink, two
        self.W = S if band is None else band[1] * BQ
        self.T = S + C


def _attn_probs(cfg, j, q_ref, k_ref, q2_ref, k2_ref, bias_ref, sink_ref):
    G, S, C, W = cfg.G, cfg.S, cfg.C, cfg.W
    is_ctx = j * BQ >= S
    if cfg.band is None:
        off, var = 0, 0
    else:
        start, var = _window_fns(cfg.band, S, cfg.n_var)(j)
        off = pl.multiple_of(start * BQ, BQ)
    qt = q_ref[...]
    qs = jnp.concatenate([qt[:, g * LANE:(g + 1) * LANE] for g in range(G)], axis=0) if G > 1 else qt
    kw = k_ref[pl.ds(off, W), :]
    kc = k_ref[pl.ds(S, C), :]
    s_w = _dot_nt(qs, kw)
    s_c = _dot_nt(qs, kc)
    q2s = k2w = k2c = None
    if cfg.two:
        q2s = q2_ref[...]
        k2w = k2_ref[pl.ds(off, W), :]
        k2c = k2_ref[pl.ds(S, C), :]
        s_w = s_w + _dot_nt(q2s, k2w)
        s_c = s_c + _dot_nt(q2s, k2c)
    s_w = s_w * cfg.scale
    s_c = s_c * cfg.scale
    if cfg.n_var:
        b = bias_ref[0, pl.ds(var, 1)][0]
        s_w = s_w + (jnp.concatenate([b] * G, axis=0) if G > 1 else b)
    s_w = jnp.where(is_ctx, NEG, s_w)
    m = jnp.maximum(jnp.max(s_w, axis=-1, keepdims=True), jnp.max(s_c, axis=-1, keepdims=True))
    if cfg.has_sink:
        sink = sink_ref[0][:, 0:1]
        m = jnp.maximum(m, sink)
    e_w = jnp.exp(s_w - m)
    e_c = jnp.exp(s_c - m)
    l = jnp.sum(e_w, axis=-1, keepdims=True) + jnp.sum(e_c, axis=-1, keepdims=True)
    p_s = None
    if cfg.has_sink:
        e_s = jnp.exp(sink - m)
        l = l + e_s
    inv = 1.0 / l
    if cfg.has_sink:
        p_s = e_s * inv
    return e_w * inv, e_c * inv, p_s, (off, var, qs, kw, kc, q2s, k2w, k2c)


def _attn_specs(cfg, q_cb, k_cb, v_cb, q2_cb, k2_cb):
    G, T = cfg.G, cfg.T
    specs = [pl.BlockSpec((BQ, G * LANE), lambda h, j: (j, q_cb // G + h)),
             pl.BlockSpec((T, LANE), lambda h, j: (0, k_cb + h)),
             pl.BlockSpec((T, LANE), lambda h, j: (0, v_cb + h))]
    if cfg.two:
        specs += [pl.BlockSpec((BQ, LANE), lambda h, j: (j, q2_cb + h)),
                  pl.BlockSpec((T, LANE), lambda h, j: (0, k2_cb))]
    if cfg.n_var:
        if cfg.bias_per_head:
            specs.append(pl.BlockSpec((1, cfg.n_var, BQ, cfg.W), lambda h, j: (h, 0, 0, 0)))
        else:
            specs.append(pl.BlockSpec((1, cfg.n_var, BQ, cfg.W), lambda h, j: (0, 0, 0, 0)))
    if cfg.has_sink:
        specs.append(pl.BlockSpec((1, G * BQ, LANE), lambda h, j: (h, 0, 0)))
    return specs


def _attn_unpack(cfg, refs):
    refs = list(refs)
    q_ref, k_ref, v_ref = refs[:3]
    n = 3
    q2_ref = k2_ref = bias_ref = sink_ref = None
    if cfg.two:
        q2_ref, k2_ref = refs[n:n + 2]
        n += 2
    if cfg.n_var:
        bias_ref = refs[n]
        n += 1
    if cfg.has_sink:
        sink_ref = refs[n]
        n += 1
    return (q_ref, k_ref, v_ref, q2_ref, k2_ref, bias_ref, sink_ref), refs[n:]


def _attn_fwd(cfg, q, q_cb, k, k_cb, v, v_cb, *, q2=None, q2_cb=0, k2=None, k2_cb=0, bias=None, sink=None, name):
    G, T, S, C, W = cfg.G, cfg.T, cfg.S, cfg.C, cfg.W
    assert q_cb % G == 0
    operands = [q, k, v] + ([q2, k2] if cfg.two else []) + ([bias] if cfg.n_var else []) + ([sink] if cfg.has_sink else [])

    def body(*refs):
        (q_ref, k_ref, v_ref, q2_ref, k2_ref, bias_ref, sink_ref), (o_ref,) = _attn_unpack(cfg, refs)
        j = pl.program_id(1)
        p_w, p_c, _, (off, _, _, _, _, _, _, _) = _attn_probs(cfg, j, q_ref, k_ref, q2_ref, k2_ref, bias_ref, sink_ref)
        o = _dot(p_w.astype(BF16), v_ref[pl.ds(off, W), :]) + _dot(p_c.astype(BF16), v_ref[pl.ds(S, C), :])
        for g in range(G):
            o_ref[:, g * LANE:(g + 1) * LANE] = o[g * BQ:(g + 1) * BQ].astype(o_ref.dtype)

    return pl.pallas_call(
        body, name=name, grid=(cfg.Hkv, T // BQ),
        in_specs=_attn_specs(cfg, q_cb, k_cb, v_cb, q2_cb, k2_cb),
        out_specs=pl.BlockSpec((BQ, G * LANE), lambda h, j: (j, h)),
        out_shape=jax.ShapeDtypeStruct((T, cfg.Hkv * G * LANE), BF16),
        compiler_params=_cparams(("parallel", "parallel")),
    )(*operands)


def _attn_bwd(cfg, q, q_cb, k, k_cb, v, v_cb, do, do_cb, *, q2=None, q2_cb=0, k2=None, k2_cb=0, bias=None, sink=None,
              want_dbias=False, dq_dtype=F32, name):
    G, T, S, C, W, Hkv = cfg.G, cfg.T, cfg.S, cfg.C, cfg.W, cfg.Hkv
    assert q_cb % G == 0 and do_cb % G == 0 and not (want_dbias and G > 1)
    operands = [q, k, v] + ([q2, k2] if cfg.two else []) + ([bias] if cfg.n_var else []) + ([sink] if cfg.has_sink else [])
    operands.append(do)
    in_specs = _attn_specs(cfg, q_cb, k_cb, v_cb, q2_cb, k2_cb)
    in_specs.append(pl.BlockSpec((BQ, G * LANE), lambda h, j: (j, do_cb // G + h)))

    out_specs = [pl.BlockSpec((BQ, G * LANE), lambda h, j: (j, h)),
                 pl.BlockSpec((T, LANE), lambda h, j: (0, h)),
                 pl.BlockSpec((T, LANE), lambda h, j: (0, h))]
    out_shape = [jax.ShapeDtypeStruct((T, Hkv * G * LANE), dq_dtype),
                 jax.ShapeDtypeStruct((T, Hkv * LANE), F32),
                 jax.ShapeDtypeStruct((T, Hkv * LANE), F32)]
    if cfg.two:
        out_specs += [pl.BlockSpec((BQ, LANE), lambda h, j: (j, h)), pl.BlockSpec((T, LANE), lambda h, j: (0, 0))]
        out_shape += [jax.ShapeDtypeStruct((T, Hkv * LANE), dq_dtype), jax.ShapeDtypeStruct((T, LANE), F32)]
    if want_dbias:
        out_specs.append(pl.BlockSpec((1, cfg.n_var, BQ, W), lambda h, j: (h, 0, 0, 0)))
        out_shape.append(jax.ShapeDtypeStruct((Hkv, cfg.n_var, BQ, W), F32))
    if cfg.has_sink:
        out_specs.append(pl.BlockSpec((1, G * BQ, LANE), lambda h, j: (h, 0, 0)))
        out_shape.append(jax.ShapeDtypeStruct((Hkv, G * BQ, LANE), F32))

    def body(*refs):
        (q_ref, k_ref, v_ref, q2_ref, k2_ref, bias_ref, sink_ref), rest = _attn_unpack(cfg, refs)
        do_ref, dq_ref, dk_ref, dv_ref = rest[:4]
        rest = rest[4:]
        dq2_ref = dk2_ref = dbias_ref = dsink_ref = None
        if cfg.two:
            dq2_ref, dk2_ref = rest[:2]
            rest = rest[2:]
        if want_dbias:
            dbias_ref = rest[0]
            rest = rest[1:]
        if cfg.has_sink:
            dsink_ref = rest[0]
        h = pl.program_id(0)
        j = pl.program_id(1)

        @pl.when(j == 0)
        def _():
            dk_ref[...] = jnp.zeros_like(dk_ref)
            dv_ref[...] = jnp.zeros_like(dv_ref)
            if want_dbias:
                dbias_ref[...] = jnp.zeros_like(dbias_ref)
            if cfg.has_sink:
                dsink_ref[...] = jnp.zeros_like(dsink_ref)

        if cfg.two:
            @pl.when((j == 0) & (h == 0))
            def _():
                dk2_ref[...] = jnp.zeros_like(dk2_ref)

        p_w, p_c, p_s, (off, var, qs, kw, kc, q2s, k2w, k2c) = _attn_probs(
            cfg, j, q_ref, k_ref, q2_ref, k2_ref, bias_ref, sink_ref)
        dot_ = do_ref[...]
        dos = jnp.concatenate([dot_[:, g * LANE:(g + 1) * LANE] for g in range(G)], axis=0) if G > 1 else dot_
        dos = dos.astype(BF16)
        vw = v_ref[pl.ds(off, W), :]
        vc = v_ref[pl.ds(S, C), :]
        dp_w = _dot_nt(dos, vw)
        dp_c = _dot_nt(dos, vc)
        delta = jnp.sum(p_w * dp_w, axis=-1, keepdims=True) + jnp.sum(p_c * dp_c, axis=-1, keepdims=True)
        ds_w = p_w * (dp_w - delta)
        ds_c = p_c * (dp_c - delta)
        if want_dbias:
            dbias_ref[0, pl.ds(var, 1)] += ds_w[None]
        if cfg.has_sink:
            dsink_ref[0] += jnp.broadcast_to(-(p_s * delta), (G * BQ, LANE))
        dsw = (ds_w * cfg.scale).astype(BF16)
        dsc = (ds_c * cfg.scale).astype(BF16)
        dq = _dot(dsw, kw) + _dot(dsc, kc)
        for g in range(G):
            dq_ref[:, g * LANE:(g + 1) * LANE] = dq[g * BQ:(g + 1) * BQ].astype(dq_ref.dtype)
        dk_ref[pl.ds(off, W), :] += _dot_tn(dsw, qs)
        dk_ref[pl.ds(S, C), :] += _dot_tn(dsc, qs)
        dv_ref[pl.ds(off, W), :] += _dot_tn(p_w.astype(BF16), dos)
        dv_ref[pl.ds(S, C), :] += _dot_tn(p_c.astype(BF16), dos)
        if cfg.two:
            dq2_ref[...] = (_dot(dsw, k2w) + _dot(dsc, k2c)).astype(dq2_ref.dtype)
            dk2_ref[pl.ds(off, W), :] += _dot_tn(dsw, q2s)
            dk2_ref[pl.ds(S, C), :] += _dot_tn(dsc, q2s)

    return pl.pallas_call(
        body, name=name, grid=(Hkv, T // BQ),
        in_specs=in_specs, out_specs=out_specs, out_shape=out_shape,
        compiler_params=_cparams(("arbitrary", "arbitrary")),
    )(*operands)


def _na_bias(rpb, S):
    H = rpb.shape[0]
    rows = S // GRID_W
    pad_l = GRID_W - 1 - (NA_WIN_C - 1)
    ext = jnp.concatenate([jnp.broadcast_to(rpb[:, :, :1], (H, 2 * NA_WIN_R - 1, pad_l)), rpb,
                           jnp.broadcast_to(rpb[:, :, -1:], (H, 2 * NA_WIN_R - 1, pad_l))], axis=2)
    by_col = jnp.stack([ext[:, :, GRID_W - 1 - qc:2 * GRID_W - 1 - qc] for qc in range(GRID_W)], axis=2)
    cq = np.arange(GRID_W)
    c0 = np.clip(cq - NA_WIN_C // 2, 0, GRID_W - NA_WIN_C)
    col_in = (cq[None, :] >= c0[:, None]) & (cq[None, :] < c0[:, None] + NA_WIN_C)
    n_lat = S // BQ
    neg_tile = jnp.full((H, GRID_W, GRID_W), NEG, F32)
    variants = []
    for v in range(5):
        j = {0: 0, 1: 1, 2: 2, 3: n_lat - 2, 4: n_lat - 1}[v]
        start = int(np.clip(j - 2, 0, n_lat - 5))
        assert j - start == v
        q_rows = []
        for qr in range(2):
            r = 2 * j + qr
            r0 = int(np.clip(r - NA_WIN_R // 2, 0, rows - NA_WIN_R))
            k_tiles = []
            for kr in range(10):
                krow = 2 * start + kr
                if r0 <= krow < r0 + NA_WIN_R:
                    k_tiles.append(jnp.where(col_in[None], by_col[:, krow - r + NA_WIN_R - 1], NEG))
                else:
                    k_tiles.append(neg_tile)
            q_rows.append(jnp.concatenate(k_tiles, axis=2))
        variants.append(jnp.concatenate(q_rows, axis=1))
    return jnp.stack(variants, axis=1)


def _swa_mask(S):
    qq = np.arange(BQ)[:, None]
    kk = np.arange(3 * BQ)[None, :]
    tiles = [np.where(np.abs(kk - v * BQ - qq) <= SWA_WINDOW, 0.0, NEG) for v in range(3)]
    return jnp.asarray(np.stack(tiles)[None], F32)


def _ffn_tiles(T, S, F):
    return _row_block(T, S), _pick(F, 1408)


def _halo_specs(T, bt, bf, col_off):
    n8 = bt // 8
    return [pl.BlockSpec((bt, bf), lambda f, i: (i, f + col_off)),
            pl.BlockSpec((8, bf), lambda f, i: (jnp.maximum(i * n8 - 1, 0), f + col_off)),
            pl.BlockSpec((8, bf), lambda f, i: (jnp.minimum((i + 1) * n8, T // 8 - 1), f + col_off))]


def _neighbours(x, prev8, next8, i, bt, S, T):
    r = lax.broadcasted_iota(jnp.int32, x.shape, 0)
    g0 = i * bt
    first_open = jnp.logical_or(g0 == 0, g0 == S)
    last_open = jnp.logical_or(g0 + bt == S, g0 + bt == T)
    before = jnp.where(r == 0, jnp.where(first_open, 0.0, prev8[7:8, :]), pltpu.roll(x, 1, 0))
    after = jnp.where(r == bt - 1, jnp.where(last_open, 0.0, next8[0:1, :]), pltpu.roll(x, bt - 1, 0))
    return before, after


def _sigmoid(a):
    return 1.0 / (1.0 + jnp.exp(-a))


def _ffn_fwd(gp, u, cw, cb, S, *, name):
    T, F = gp.shape
    bt, bf = _ffn_tiles(T, S, F)

    def body(g_ref, gp_ref, gn_ref, u_ref, w_ref, b_ref, o_ref):
        i = pl.program_id(1)
        g = g_ref[...]
        before, after = _neighbours(g, gp_ref[...], gn_ref[...], i, bt, S, T)
        a = before * w_ref[0:1, :] + g * w_ref[1:2, :] + after * w_ref[2:3, :] + b_ref[...]
        o_ref[...] = (a * _sigmoid(a) * u_ref[...]).astype(o_ref.dtype)

    return pl.pallas_call(
        body, name=name, grid=(F // bf, T // bt),
        in_specs=_halo_specs(T, bt, bf, 0) + [pl.BlockSpec((bt, bf), lambda f, i: (i, f)),
                                              pl.BlockSpec((3, bf), lambda f, i: (0, f)),
                                              pl.BlockSpec((1, bf), lambda f, i: (0, f))],
        out_specs=pl.BlockSpec((bt, bf), lambda f, i: (i, f)),
        out_shape=jax.ShapeDtypeStruct((T, F), BF16),
        compiler_params=_cparams(("parallel", "parallel")),
    )(gp, gp, gp, u, cw, cb)


def _ffn_bwd_act(gp, u, da_out, cw, cb, S, *, name):
    T, F = gp.shape
    bt, bf = _ffn_tiles(T, S, F)

    def body(g_ref, gp_ref, gn_ref, u_ref, d_ref, w_ref, b_ref, da_ref, du_ref, acc_ref):
        i = pl.program_id(1)

        @pl.when(i == 0)
        def _():
            acc_ref[...] = jnp.zeros_like(acc_ref)

        g = g_ref[...]
        before, after = _neighbours(g, gp_ref[...], gn_ref[...], i, bt, S, T)
        a = before * w_ref[0:1, :] + g * w_ref[1:2, :] + after * w_ref[2:3, :] + b_ref[...]
        sig = _sigmoid(a)
        d = d_ref[...]
        du_ref[...] = (d * (a * sig)).astype(du_ref.dtype)
        da = d * u_ref[...] * (sig * (1.0 + a * (1.0 - sig)))
        da_ref[...] = da
        acc_ref[0:1, :] += jnp.sum(da * before, axis=0, keepdims=True)
        acc_ref[1:2, :] += jnp.sum(da * g, axis=0, keepdims=True)
        acc_ref[2:3, :] += jnp.sum(da * after, axis=0, keepdims=True)
        acc_ref[3:4, :] += jnp.sum(da, axis=0, keepdims=True)

    blk = pl.BlockSpec((bt, bf), lambda f, i: (i, f))
    return pl.pallas_call(
        body, name=name, grid=(F // bf, T // bt),
        in_specs=_halo_specs(T, bt, bf, 0) + [blk, blk,
                                              pl.BlockSpec((3, bf), lambda f, i: (0, f)),
                                              pl.BlockSpec((1, bf), lambda f, i: (0, f))],
        out_specs=[blk, blk, pl.BlockSpec((8, bf), lambda f, i: (0, f))],
        out_shape=[jax.ShapeDtypeStruct((T, F), F32), jax.ShapeDtypeStruct((T, F), BF16),
                   jax.ShapeDtypeStruct((8, F), F32)],
        compiler_params=_cparams(("parallel", "arbitrary")),
    )(gp, gp, gp, u, da_out, cw, cb)


def _ffn_bwd_conv(da, cw, S, *, name):
    T, F = da.shape
    bt, bf = _ffn_tiles(T, S, F)

    def body(d_ref, dp_ref, dn_ref, w_ref, o_ref):
        i = pl.program_id(1)
        d = d_ref[...]
        before, after = _neighbours(d, dp_ref[...], dn_ref[...], i, bt, S, T)
        o_ref[...] = (after * w_ref[0:1, :] + d * w_ref[1:2, :] + before * w_ref[2:3, :]).astype(o_ref.dtype)

    return pl.pallas_call(
        body, name=name, grid=(F // bf, T // bt),
        in_specs=_halo_specs(T, bt, bf, 0) + [pl.BlockSpec((3, bf), lambda f, i: (0, f))],
        out_specs=pl.BlockSpec((bt, bf), lambda f, i: (i, f)),
        out_shape=jax.ShapeDtypeStruct((T, F), BF16),
        compiler_params=_cparams(("parallel", "parallel")),
    )(da, da, da, cw)


def _ew_rows(R, N, n_arrays):
    return _pick(R, max(16, (1 << 18) // N), 16)


def _adam(w, g, m, v, *, name):
    R, N = w.shape
    br = _ew_rows(R, N, 7)
    bc1 = 1.0 - ADAM_B1 ** ADAM_STEP
    bc2 = 1.0 - ADAM_B2 ** ADAM_STEP

    def body(w_ref, g_ref, m_ref, v_ref, d_ref, mo_ref, vo_ref):
        gv = g_ref[...]
        mn = ADAM_B1 * m_ref[...] + (1.0 - ADAM_B1) * gv
        vn = ADAM_B2 * v_ref[...] + (1.0 - ADAM_B2) * (gv * gv)
        mo_ref[...] = mn
        vo_ref[...] = vn
        d_ref[...] = -ADAM_LR * ((mn / bc1) / (jnp.sqrt(vn / bc2) + ADAM_EPS) + ADAM_WD * w_ref[...])

    blk = pl.BlockSpec((br, N), lambda i: (i, 0))
    shp = jax.ShapeDtypeStruct((R, N), F32)
    return pl.pallas_call(
        body, name=name, grid=(R // br,),
        in_specs=[blk, blk, blk, blk], out_specs=[blk, blk, blk], out_shape=[shp, shp, shp],
        compiler_params=_cparams(("parallel",)),
    )(w, g, m, v)


def _sum_lead(x, out_dtype, *, name):
    n, R, N = x.shape
    br = _ew_rows(R, N, n + 1)

    def body(x_ref, o_ref):
        acc = x_ref[0].astype(F32)
        for k in range(1, n):
            acc = acc + x_ref[k].astype(F32)
        o_ref[...] = acc.astype(o_ref.dtype)

    return pl.pallas_call(
        body, name=name, grid=(R // br,),
        in_specs=[pl.BlockSpec((n, br, N), lambda i: (0, i, 0))],
        out_specs=pl.BlockSpec((br, N), lambda i: (i, 0)),
        out_shape=jax.ShapeDtypeStruct((R, N), out_dtype),
        compiler_params=_cparams(("parallel",)),
    )(x)


def _sum4_layers(x, core, *, name):
    L, n, R, N = x.shape
    br = _ew_rows(R, N, n + 1)

    def body(c_ref, x_ref, o_ref):
        acc = x_ref[0].astype(F32)
        for k in range(1, n):
            acc = acc + x_ref[k].astype(F32)
        o_ref[...] = acc

    return pl.pallas_call(
        body, name=name,
        grid_spec=pltpu.PrefetchScalarGridSpec(
            num_scalar_prefetch=1, grid=(L, R // br),
            in_specs=[pl.BlockSpec((None, n, br, N), lambda l, i, c_ref: (l, 0, i, 0))],
            out_specs=pl.BlockSpec((None, None, br, N), lambda l, i, c_ref: (l, c_ref[0], i, 0))),
        out_shape=jax.ShapeDtypeStruct((L, 2, R, N), F32),
        compiler_params=_cparams(("parallel", "parallel")),
    )(core.reshape(1).astype(jnp.int32), x)


def _add_half(g, r, core, *, name):
    L, Q, _, R, N = g.shape
    br = _ew_rows(R, N, 3)

    def body(c_ref, g_ref, r_ref, o_ref):
        o_ref[...] = (g_ref[...] + r_ref[...]).astype(o_ref.dtype)

    return pl.pallas_call(
        body, name=name,
        grid_spec=pltpu.PrefetchScalarGridSpec(
            num_scalar_prefetch=1, grid=(L, Q, R // br),
            in_specs=[pl.BlockSpec((None, None, None, br, N), lambda l, q, i, c_ref: (l, q, c_ref[0], i, 0)),
                      pl.BlockSpec((None, None, br, N), lambda l, q, i, c_ref: (l, q, i, 0))],
            out_specs=pl.BlockSpec((None, None, br, N), lambda l, q, i, c_ref: (l, q, i, 0))),
        out_shape=jax.ShapeDtypeStruct((L, Q, R, N), BF16),
        compiler_params=_cparams(("parallel", "parallel", "parallel")),
    )(core.reshape(1).astype(jnp.int32), g, r)


_ANY = pl.BlockSpec(memory_space=pl.ANY)


def _place():
    return lax.axis_index("x"), lax.axis_index("y"), lax.axis_index("c")


def _allgather8(blocks, *, name):
    n = len(blocks)

    def body(*refs):
        xs, outs = refs[:n], refs[n:2 * n]
        send_sems, recv_sems, local_sems = refs[2 * n:]
        x, y, c = _place()
        me, sibling = (x, y, c), (x, y, 1 - c)
        chips = [(1 - x, y), (x, 1 - y), (1 - x, 1 - y)]

        def slot(a, px, py, pc):
            return outs[a].at[4 * px + 2 * py + pc]

        def copy(a, k, block, to, src=None):
            return pltpu.make_async_remote_copy(
                src_ref=slot(a, *block) if src is None else src, dst_ref=slot(a, *block),
                send_sem=send_sems.at[a, k], recv_sem=recv_sems.at[a, k], device_id=to, device_id_type=MESH)

        mine = [pltpu.make_async_copy(xs[a], slot(a, *me), local_sems.at[a]) for a in range(n)]
        for cp in mine:
            cp.start()
        first = []
        for a in range(n):
            first.append(copy(a, 0, me, sibling, src=xs[a]))
            first += [copy(a, 1 + j, me, (*chip, c), src=xs[a]) for j, chip in enumerate(chips)]
        for cp in first:
            cp.start()
        passed = []
        for j, chip in enumerate(chips):
            for a in range(n):
                copy(a, 1 + j, (*chip, c), me).wait_recv()
                fwd = copy(a, 4 + j, (*chip, c), sibling)
                fwd.start()
                passed.append(fwd)
        for a in range(n):
            copy(a, 0, sibling, me).wait_recv()
            for j, chip in enumerate(chips):
                copy(a, 4 + j, (*chip, 1 - c), me).wait_recv()
        for cp in first + passed:
            cp.wait_send()
        for cp in mine:
            cp.wait()

    return pl.pallas_call(
        body, name=name,
        in_specs=[_ANY] * n, out_specs=[_ANY] * n,
        out_shape=[jax.ShapeDtypeStruct((8,) + b.shape, b.dtype) for b in blocks],
        scratch_shapes=[pltpu.SemaphoreType.DMA((n, 7)), pltpu.SemaphoreType.DMA((n, 7)), pltpu.SemaphoreType.DMA((n,))],
    )(*blocks)


def _send_other_half(gs, *, name):
    n = len(gs)

    def body(*refs):
        xs, outs = refs[:n], refs[n:2 * n]
        send_sems, recv_sems = refs[2 * n:]
        x, y, c = _place()
        cps = [pltpu.make_async_remote_copy(
            src_ref=xs[a].at[:, :, 1 - c], dst_ref=outs[a], send_sem=send_sems.at[a], recv_sem=recv_sems.at[a],
            device_id=(x, y, 1 - c), device_id_type=MESH) for a in range(n)]
        for cp in cps:
            cp.start()
        for cp in cps:
            cp.wait()

    return pl.pallas_call(
        body, name=name,
        in_specs=[_ANY] * n, out_specs=[_ANY] * n,
        out_shape=[jax.ShapeDtypeStruct(g.shape[:2] + g.shape[3:], g.dtype) for g in gs],
        scratch_shapes=[pltpu.SemaphoreType.DMA((n,)), pltpu.SemaphoreType.DMA((n,))],
    )(*gs)


def _chip_scatter(parts, *, name):
    n = len(parts)

    def body(*refs):
        xs, outs = refs[:n], refs[n:2 * n]
        send_sems, recv_sems, local_sems = refs[2 * n:]
        x, y, c = _place()
        q_me = 2 * x + y
        chips = [(1 - x, y), (x, 1 - y), (1 - x, 1 - y)]
        own = [pltpu.make_async_copy(xs[a].at[:, q_me], outs[a].at[:, q_me], local_sems.at[a]) for a in range(n)]
        for cp in own:
            cp.start()
        sends, recvs = [], []
        for a in range(n):
            for k, (px, py) in enumerate(chips):
                q_to = 2 * px + py
                sends.append(pltpu.make_async_remote_copy(
                    src_ref=xs[a].at[:, q_to], dst_ref=outs[a].at[:, q_me], send_sem=send_sems.at[a, k],
                    recv_sem=recv_sems.at[a, k], device_id=(px, py, c), device_id_type=MESH))
                recvs.append(pltpu.make_async_remote_copy(
                    src_ref=xs[a].at[:, q_me], dst_ref=outs[a].at[:, q_to], send_sem=send_sems.at[a, k],
                    recv_sem=recv_sems.at[a, k], device_id=(px, py, c), device_id_type=MESH))
        for cp in sends:
            cp.start()
        for cp in recvs:
            cp.wait_recv()
        for cp in sends:
            cp.wait_send()
        for cp in own:
            cp.wait()

    return pl.pallas_call(
        body, name=name,
        in_specs=[_ANY] * n, out_specs=[_ANY] * n,
        out_shape=[jax.ShapeDtypeStruct(p.shape, p.dtype) for p in parts],
        scratch_shapes=[pltpu.SemaphoreType.DMA((n, 3)), pltpu.SemaphoreType.DMA((n, 3)), pltpu.SemaphoreType.DMA((n,))],
    )(*parts)


def _join_halves(bufs, *, name):
    n = len(bufs)

    def body(*refs):
        xs, outs = refs[:n], refs[n:2 * n]
        send_sems, recv_sems = refs[2 * n:]
        x, y, c = _place()
        sends = [pltpu.make_async_remote_copy(
            src_ref=xs[a].at[:, c], dst_ref=outs[a].at[:, c], send_sem=send_sems.at[a], recv_sem=recv_sems.at[a],
            device_id=(x, y, 1 - c), device_id_type=MESH) for a in range(n)]
        recvs = [pltpu.make_async_remote_copy(
            src_ref=xs[a].at[:, c], dst_ref=outs[a].at[:, 1 - c], send_sem=send_sems.at[a], recv_sem=recv_sems.at[a],
            device_id=(x, y, 1 - c), device_id_type=MESH) for a in range(n)]
        for cp in sends:
            cp.start()
        for cp in recvs:
            cp.wait_recv()
        for cp in sends:
            cp.wait_send()

    return pl.pallas_call(
        body, name=name,
        in_specs=[_ANY] * n, out_specs=[_ANY] * n,
        out_shape=[jax.ShapeDtypeStruct(b.shape, b.dtype) for b in bufs],
        input_output_aliases={a: a for a in range(n)},
        scratch_shapes=[pltpu.SemaphoreType.DMA((n,)), pltpu.SemaphoreType.DMA((n,))],
    )(*bufs)


def _perm_w_in(w):
    pad = jnp.zeros((w.shape[0], PCOLS - IN_COLS), w.dtype)
    return jnp.concatenate([w[:, :3072], w[:, 3136:IN_COLS], w[:, 3072:3136], pad], axis=1)


def _unperm_w_in(g):
    return jnp.concatenate([g[:, :3072], g[:, 4096:IN_COLS], g[:, 3072:4096]], axis=1)


def _perm_w_uq(w):
    w4 = w.reshape(MLA_Q_LORA, MLA_HEADS, MLA_NOPE + MLA_ROPE)
    nope = w4[:, :, :MLA_NOPE].reshape(MLA_Q_LORA, MLA_HEADS * LANE)
    pe = jnp.pad(w4[:, :, MLA_NOPE:], ((0, 0), (0, 0), (0, LANE - MLA_ROPE))).reshape(MLA_Q_LORA, MLA_HEADS * LANE)
    return jnp.concatenate([nope, pe], axis=1)


def _unperm_w_uq(g):
    nope = g[:, :MLA_HEADS * LANE].reshape(MLA_Q_LORA, MLA_HEADS, LANE)
    pe = g[:, MLA_HEADS * LANE:].reshape(MLA_Q_LORA, MLA_HEADS, LANE)[:, :, :MLA_ROPE]
    return jnp.concatenate([nope, pe], axis=2).reshape(MLA_Q_LORA, MLA_HEADS * (MLA_NOPE + MLA_ROPE))


def _perm_w_ukv(w):
    w4 = w.reshape(MLA_KV_LORA, MLA_HEADS, MLA_NOPE + MLA_V)
    return jnp.concatenate([w4[:, :, :MLA_NOPE].reshape(MLA_KV_LORA, -1), w4[:, :, MLA_NOPE:].reshape(MLA_KV_LORA, -1)], axis=1)


def _unperm_w_ukv(g):
    kn = g[:, :MLA_HEADS * LANE].reshape(MLA_KV_LORA, MLA_HEADS, LANE)
    vv = g[:, MLA_HEADS * LANE:].reshape(MLA_KV_LORA, MLA_HEADS, LANE)
    return jnp.concatenate([kn, vv], axis=2).reshape(MLA_KV_LORA, -1)


def _silu(v):
    return v * jax.nn.sigmoid(v)


def _silu_grad(v):
    s = jax.nn.sigmoid(v)
    return s * (1.0 + v * (1.0 - s))


_WEIGHTS = ("c_ctx", "w_ada", "b_ada", "w_in", "na_rpb", "swa_sink", "mla_q_norm", "mla_kv_norm", "mla_w_uq", "mla_w_ukv",
            "gqa_q_norm", "gqa_k_norm", "w_out", "ln1_g", "ln1_b", "ffn_w_gate", "ffn_w_up", "ffn_conv_w", "ffn_conv_b",
            "ffn_w_down", "ln2_g", "ln2_b")
_COL_SHARDED = ("w_in", "mla_w_uq", "mla_w_ukv", "ffn_w_gate", "ffn_w_up")
_ROW_SHARDED = ("w_out", "ffn_w_down")
_BIG = _COL_SHARDED + _ROW_SHARDED
_SMALL = ("c_ctx", "b_ada", "na_rpb", "swa_sink", "mla_q_norm", "mla_kv_norm", "gqa_q_norm", "gqa_k_norm", "ln1_g", "ln1_b",
          "ffn_conv_w", "ffn_conv_b", "ln2_g", "ln2_b")


def _pack(arrays):
    flat = jnp.concatenate([a.reshape(-1) for a in arrays])
    n = flat.shape[0]
    rows = -(-n // (8 * LANE)) * 8
    return jnp.pad(flat, (0, rows * LANE - n)).reshape(rows, LANE)


def _unpack(packed, like):
    flat = packed.reshape(-1)
    out, o = [], 0
    for a in like:
        out.append(flat[o:o + a.size].reshape(a.shape))
        o += a.size
    return out


def _train_step(x, c, ctx, loss_target, w, m_in, v_in):
    L = DEPTH
    S, D = x.shape[1], x.shape[2]
    C = ctx.shape[1]
    T = S + C
    F = w["ffn_conv_b"].shape[1]
    ax, ay, ac = _place()
    chip = 2 * ax + ay
    dev = 2 * chip + ac
    n_ada = w["w_ada"].shape[2]

    def my_half(a):
        r = a.shape[1] // 2
        return lax.dynamic_slice_in_dim(a, ac * r, r, axis=1).astype(BF16)

    gathered = _allgather8([my_half(w[n]) for n in _BIG] + [w["ffn_conv_w"]], name="gather_weights")
    full = {}
    for n, g in zip(_BIG, gathered[:-1]):
        _, _, r, cols = g.shape
        g = g.reshape(4, 2, L, r, cols)
        if n in _COL_SHARDED:
            full[n] = g.transpose(2, 1, 3, 0, 4).reshape(L, 2 * r, 4 * cols)
        else:
            full[n] = g.transpose(2, 0, 1, 3, 4).reshape(L, 8 * r, cols)
    conv_w = gathered[-1][::2].transpose(1, 2, 0, 3).reshape(L, 3, F)
    w_in_p = [_perm_w_in(full["w_in"][l]) for l in range(L)]
    w_uq_p = [_perm_w_uq(full["mla_w_uq"][l]) for l in range(L)]
    w_ukv_p = [_perm_w_ukv(full["mla_w_ukv"][l]) for l in range(L)]

    (c_all,) = _allgather8([c], name="gather_c")
    c16 = jnp.concatenate([c_all.reshape(8, D), jnp.broadcast_to(w["c_ctx"][None], (8, D))], axis=0)
    row_keep = (jnp.arange(16) <= 8).astype(F32)[:, None]
    sc = _silu(c16) * row_keep
    b_loc = lax.dynamic_slice_in_dim(w["b_ada"], chip * n_ada, n_ada, axis=1)
    mod_loc = jnp.stack([_mm(sc, w["w_ada"][l], name="mod_mm") + b_loc[l][None] for l in range(L)])
    (mod_g,) = _allgather8([mod_loc], name="gather_mod")
    mod_all = mod_g[::2].transpose(1, 2, 0, 3).reshape(L, 16, 4 * n_ada)
    mod_x = lax.dynamic_index_in_dim(mod_all, dev, axis=1, keepdims=False)
    mod_c = mod_all[:, 8]
    mods = [jnp.stack([mod_x[l].reshape(6, D), mod_c[l].reshape(6, D)], axis=1) for l in range(L)]

    tabs = _rope_tables(S, C, HEAD_DIM) + _rope_tables(S, C, MLA_ROPE)
    swa_mask = _swa_mask(S)
    scale = HEAD_DIM ** -0.5
    cfg_na = _AttnCfg(Hkv=NA_HEADS, G=1, S=S, C=C, band=(2, 5), scale=scale, n_var=5, bias_per_head=True)
    cfg_swa = _AttnCfg(Hkv=SWA_KV_HEADS, G=SWA_HEADS // SWA_KV_HEADS, S=S, C=C, band=(1, 3), scale=scale, n_var=3, has_sink=True)
    cfg_gqa = _AttnCfg(Hkv=GQA_KV_HEADS, G=GQA_HEADS // GQA_KV_HEADS, S=S, C=C, band=None, scale=scale)
    cfg_mla = _AttnCfg(Hkv=MLA_HEADS, G=1, S=S, C=C, band=None, scale=(MLA_NOPE + MLA_ROPE) ** -0.5, two=True)
    row = lambda a: a[None, :]

    xt = jnp.concatenate([x[0], ctx[0]], axis=0)
    saved = []
    for l in range(L):
        md = mods[l]
        gq, gk, mq, mkv = row(w["gqa_q_norm"][l]), row(w["gqa_k_norm"][l]), row(w["mla_q_norm"][l]), row(w["mla_kv_norm"][l])
        h1 = _mod_fwd(xt, md[0], md[1], S, name="mod_fwd")
        p = _mm(h1, w_in_p[l], name="in_proj")
        qkv = _prep_fwd(p, tabs, gq, gk, mq, mkv, S, name="prep_fwd")
        qm = _mm(qkv, w_uq_p[l], a_off=CB_CQ * LANE, a_k=MLA_Q_LORA, tk=LANE, name="mla_uq")
        qmb = _pe_rope(qm, tabs, S, 1.0, BF16, name="mla_q_rope")
        kvm = _mm(qkv, w_ukv_p[l], a_off=CB_CKV * LANE, a_k=MLA_KV_LORA, tk=LANE, out_dtype=BF16, name="mla_ukv")
        bias_na = _na_bias(w["na_rpb"][l], S)
        sink = jnp.broadcast_to(jnp.repeat(w["swa_sink"][l].reshape(SWA_KV_HEADS, -1), BQ, axis=1)[:, :, None],
                                (SWA_KV_HEADS, SWA_HEADS // SWA_KV_HEADS * BQ, LANE))
        oa = _attn_fwd(cfg_na, qkv, CB_NA_Q, qkv, CB_NA_K, qkv, CB_NA_V, bias=bias_na, name="na_fwd")
        ob = _attn_fwd(cfg_swa, qkv, CB_SWA_Q, qkv, CB_SWA_K, qkv, CB_SWA_V, bias=swa_mask, sink=sink, name="swa_fwd")
        oc = _attn_fwd(cfg_mla, qmb, 0, kvm, 0, kvm, MLA_HEADS, q2=qmb, q2_cb=MLA_HEADS, k2=qkv, k2_cb=CB_KPE, name="mla_fwd")
        od = _attn_fwd(cfg_gqa, qkv, CB_GQA_Q, qkv, CB_GQA_K, qkv, CB_GQA_V, name="gqa_fwd")
        mix = jnp.concatenate([oa, ob, oc, od], axis=1)
        z1 = _mm(mix, full["w_out"][l], name="out_proj")
        x1 = _res_fwd(xt, z1, md[2], row(w["ln1_g"][l]), row(w["ln1_b"][l]), S, name="res_fwd")
        h2 = _mod_fwd(x1, md[3], md[4], S, name="mod_fwd")
        gp = _mm(h2, full["ffn_w_gate"][l], name="ffn_in")
        up = _mm(h2, full["ffn_w_up"][l], name="ffn_in")
        act = _ffn_fwd(gp, up, conv_w[l], row(w["ffn_conv_b"][l]), S, name="ffn_mid")
        z2 = _mm(act, full["ffn_w_down"][l], name="ffn_out")
        x2 = _res_fwd(x1, z2, md[5], row(w["ln2_g"][l]), row(w["ln2_b"][l]), S, name="res_fwd")
        saved.append(dict(x=xt, h1=h1, p=p, qkv=qkv, qmb=qmb, kvm=kvm, bias_na=bias_na, sink=sink, mix=mix, z1=z1, x1=x1,
                          h2=h2, gp=gp, up=up, act=act, z2=z2))
        xt = x2

    dx, loss_part = _loss_fwd_bwd(xt, loss_target[0], S, name="loss")
    loss = lax.psum(loss_part[0, 0], ("x", "y", "c"))

    gbuf = {n: None for n in _BIG}
    small = {n: [None] * L for n in ("na_rpb", "swa_sink", "mla_q_norm", "mla_kv_norm", "gqa_q_norm", "gqa_k_norm",
                                     "ln1_g", "ln1_b", "ffn_conv_w", "ffn_conv_b", "ln2_g", "ln2_b")}
    dmod = [None] * L
    for l in reversed(range(L)):
        sv, md = saved[l], mods[l]
        gq, gk, mq, mkv = row(w["gqa_q_norm"][l]), row(w["gqa_k_norm"][l]), row(w["mla_q_norm"][l]), row(w["mla_kv_norm"][l])
        cb_row = row(w["ffn_conv_b"][l])
        dx1, dz2, acc_r2 = _res_bwd(sv["x1"], sv["z2"], md[5], row(w["ln2_g"][l]), dx, S, name="res_bwd")
        dact = _mm(dz2, full["ffn_w_down"][l], mode="nt", name="ffn_out_dx")
        gbuf["ffn_w_down"] = _mm(sv["act"], dz2, mode="tn", stack=(L, l, gbuf["ffn_w_down"]), name="ffn_out_dw")
        da, du, acc_f = _ffn_bwd_act(sv["gp"], sv["up"], dact, conv_w[l], cb_row, S, name="ffn_mid_bwd")
        dg = _ffn_bwd_conv(da, conv_w[l], S, name="ffn_conv_bwd")
        dh2 = _mm(dg, full["ffn_w_gate"][l], mode="nt", name="ffn_in_dx")
        dh2 = _mm(du, full["ffn_w_up"][l], mode="nt", add=dh2, name="ffn_in_dx_add")
        gbuf["ffn_w_gate"] = _mm(sv["h2"], dg, mode="tn", stack=(L, l, gbuf["ffn_w_gate"]), split4=True, name="ffn_in_dw")
        gbuf["ffn_w_up"] = _mm(sv["h2"], du, mode="tn", stack=(L, l, gbuf["ffn_w_up"]), split4=True, name="ffn_in_dw")
        dx1, acc_m2 = _mod_bwd(sv["x1"], dh2, md[4], dx1, S, name="mod_bwd")
        dxa, dz1, acc_r1 = _res_bwd(sv["x"], sv["z1"], md[2], row(w["ln1_g"][l]), dx1, S, name="res_bwd")
        dmix = _mm(dz1, full["w_out"][l], mode="nt", out_dtype=BF16, name="out_proj_dx")
        gbuf["w_out"] = _mm(sv["mix"], dz1, mode="tn", stack=(L, l, gbuf["w_out"]), name="out_proj_dw")

        qkv, qmb, kvm = sv["qkv"], sv["qmb"], sv["kvm"]
        dq_a, dk_a, dv_a, dbias = _attn_bwd(cfg_na, qkv, CB_NA_Q, qkv, CB_NA_K, qkv, CB_NA_V, dmix, 0, bias=sv["bias_na"],
                                            want_dbias=True, name="na_bwd")
        dq_b, dk_b, dv_b, dsink = _attn_bwd(cfg_swa, qkv, CB_SWA_Q, qkv, CB_SWA_K, qkv, CB_SWA_V, dmix, NA_HEADS,
                                            bias=swa_mask, sink=sv["sink"], name="swa_bwd")
        dq_c, dk_c, dv_c, dq2_c, dk2_c = _attn_bwd(cfg_mla, qmb, 0, kvm, 0, kvm, MLA_HEADS, dmix, NA_HEADS + SWA_HEADS,
                                                   q2=qmb, q2_cb=MLA_HEADS, k2=qkv, k2_cb=CB_KPE, name="mla_bwd")
        dq_d, dk_d, dv_d = _attn_bwd(cfg_gqa, qkv, CB_GQA_Q, qkv, CB_GQA_K, qkv, CB_GQA_V, dmix,
                                     NA_HEADS + SWA_HEADS + MLA_HEADS, name="gqa_bwd")
        dqm = _pe_rope(jnp.concatenate([dq_c, dq2_c], axis=1), tabs, S, -1.0, BF16, name="mla_q_rope_bwd")
        dkvm = jnp.concatenate([dk_c, dv_c], axis=1).astype(BF16)
        dcq = _mm(dqm, w_uq_p[l], mode="nt", name="mla_uq_dx")
        dckv = _mm(dkvm, w_ukv_p[l], mode="nt", name="mla_ukv_dx")
        cqn = qkv[:, CB_CQ * LANE:CB_CKV * LANE]
        ckvn = qkv[:, CB_CKV * LANE:(CB_CKV + 1) * LANE]
        d_uq = _unperm_w_uq(_mm(cqn, dqm, mode="tn", name="mla_uq_dw"))
        d_ukv = _unperm_w_ukv(_mm(ckvn, dkvm, mode="tn", name="mla_ukv_dw"))
        grads = {}
        for h in range(NA_HEADS):
            grads[CB_NA_Q + h], grads[CB_NA_K + h], grads[CB_NA_V + h] = (dq_a, h), (dk_a, h), (dv_a, h)
        for h in range(SWA_HEADS):
            grads[CB_SWA_Q + h] = (dq_b, h)
        for h in range(SWA_KV_HEADS):
            grads[CB_SWA_K + h], grads[CB_SWA_V + h] = (dk_b, h), (dv_b, h)
        for h in range(GQA_HEADS):
            grads[CB_GQA_Q + h] = (dq_d, h)
        for h in range(GQA_KV_HEADS):
            grads[CB_GQA_K + h], grads[CB_GQA_V + h] = (dk_d, h), (dv_d, h)
        grads[CB_KPE], grads[CB_CQ], grads[CB_CKV] = (dk2_c, 0), (dcq, 0), (dckv, 0)
        dp, acc_p = _prep_bwd(sv["p"], grads, tabs, gq, gk, mq, mkv, S, name="prep_bwd")
        dh1 = _mm(dp, w_in_p[l], mode="nt", name="in_proj_dx")
        d_in = _unperm_w_in(_mm(sv["h1"], dp, mode="tn", name="in_proj_dw"))
        dx, acc_m1 = _mod_bwd(sv["x"], dh1, md[1], dxa, S, name="mod_bwd")

        to4 = lambda g: g.reshape(g.shape[0], 4, g.shape[1] // 4).transpose(1, 0, 2)
        for n, g in (("w_in", d_in), ("mla_w_uq", d_uq), ("mla_w_ukv", d_ukv)):
            g4 = to4(g)[None]
            gbuf[n] = g4 if gbuf[n] is None else jnp.concatenate([g4, gbuf[n]], axis=0)
        dmod[l] = jnp.stack([acc_m1[0:2], acc_m1[2:4], acc_r1[0:2], acc_m2[0:2], acc_m2[2:4], acc_r2[0:2]])
        rpb_vjp = jax.vjp(lambda r: _na_bias(r, S), w["na_rpb"][l])[1]
        small["na_rpb"][l] = rpb_vjp(dbias)[0]
        small["swa_sink"][l] = dsink[:, :, 0].reshape(SWA_KV_HEADS, -1, BQ).sum(axis=-1).reshape(-1)
        small["gqa_q_norm"][l], small["gqa_k_norm"][l] = acc_p[0, :LANE], acc_p[1, :LANE]
        small["mla_q_norm"][l], small["mla_kv_norm"][l] = acc_p[2], acc_p[3, :LANE]
        small["ln1_g"][l], small["ln1_b"][l] = acc_r1[2], acc_r1[3]
        small["ln2_g"][l], small["ln2_b"][l] = acc_r2[2], acc_r2[3]
        small["ffn_conv_w"][l], small["ffn_conv_b"][l] = acc_f[0:3], acc_f[3]
    grad_x = dx[:S][None]

    dmod_x = jnp.stack([dmod[l][:, 0].reshape(-1) for l in range(L)])
    dmod_c = jnp.stack([dmod[l][:, 1].reshape(-1) for l in range(L)])
    small_names = tuple(small)
    bucket = [dmod_x, dmod_c] + [jnp.stack(small[n]) for n in small_names]
    (b8,) = _allgather8([_pack(bucket)], name="gather_small")
    tot = _unpack(_sum_lead(b8, F32, name="sum_small"), bucket)
    dmod_x_all = b8.reshape(8, -1)[:, :dmod_x.size].reshape(8, L, 6 * D)
    dmod_c_tot = tot[1]
    g_small = dict(zip(small_names, tot[2:]))
    g_small["b_ada"] = tot[0] + dmod_c_tot
    g_small["ffn_conv_w"] = lax.dynamic_slice_in_dim(g_small["ffn_conv_w"], chip * (F // 4), F // 4, axis=2)

    dmod16 = jnp.concatenate([dmod_x_all, jnp.broadcast_to(dmod_c_tot[None], (8, L, 6 * D))], axis=0) * row_keep[:, :, None]
    dmod16 = lax.dynamic_slice_in_dim(dmod16, chip * n_ada, n_ada, axis=2)
    g_ada, dsc = None, None
    for l in range(L):
        g_ada = _mm(sc, dmod16[:, l], mode="tn", exact=True, stack=(L, l, g_ada), name="ada_dw")
        dsc = _mm(dmod16[:, l], w["w_ada"][l], mode="nt", add=dsc, name="ada_dx" if dsc is None else "ada_dx_add")
    (dsc8,) = _allgather8([dsc[8:16]], name="gather_dsc")
    dsc4 = dsc8[::2, 0]
    g_small["c_ctx"] = (((dsc4[0] + dsc4[1]) + dsc4[2]) + dsc4[3]) * _silu_grad(w["c_ctx"])

    def halves_view(g):
        return g.reshape(g.shape[0], 4, 2, g.shape[2] // 2, g.shape[3])

    views = []
    for n in _BIG:
        g = gbuf[n]
        if n in _ROW_SHARDED:
            g = g.reshape(L, 4, g.shape[1] // 4, g.shape[2])
        views.append(halves_view(g))
    from_sibling = _send_other_half(views, name="rs_core_exchange")
    parts = [_add_half(g, r, ac, name="rs_core_add") for g, r in zip(views, from_sibling)]
    scattered = _chip_scatter(parts, name="rs_chip_scatter")
    halves = [_sum4_layers(s, ac, name="rs_chip_sum") for s in scattered]
    joined = _join_halves(halves, name="rs_join")
    g_big = {n: j.reshape(L, 2 * j.shape[2], j.shape[3]) for n, j in zip(_BIG, joined)}
    g_big["w_ada"] = g_ada

    grad, delta, new_m, new_v = {}, {}, {}, {}
    for n in _BIG + ("w_ada",):
        shp = w[n].shape
        flat = lambda a: a.reshape(shp[0] * shp[1], shp[2])
        d_, m_, v_ = _adam(flat(w[n]), flat(g_big[n]), flat(m_in[n]), flat(v_in[n]), name="adam")
        grad[n], delta[n], new_m[n], new_v[n] = g_big[n], d_.reshape(shp), m_.reshape(shp), v_.reshape(shp)
    like = [w[n] for n in _SMALL]
    packed = [_pack([src[n].reshape(w[n].shape) for n in _SMALL]) for src in (w, g_small, m_in, v_in)]
    d_s, m_s, v_s = _adam(*packed, name="adam_small")
    for n, g_, d_, m_, v_ in zip(_SMALL, _unpack(packed[1], like), _unpack(d_s, like), _unpack(m_s, like), _unpack(v_s, like)):
        grad[n], delta[n], new_m[n], new_v[n] = g_, d_, m_, v_

    return (loss, grad_x, *[grad[n] for n in _WEIGHTS], *[delta[n] for n in _WEIGHTS],
            *[new_m[n] for n in _WEIGHTS], *[new_v[n] for n in _WEIGHTS])


def kernel(x, c, ctx, c_ctx, w_ada, b_ada, w_in, na_rpb, swa_sink, mla_q_norm, mla_kv_norm, mla_w_uq, mla_w_ukv, gqa_q_norm, gqa_k_norm, w_out, ln1_g, ln1_b, ffn_w_gate, ffn_w_up, ffn_conv_w, ffn_conv_b, ffn_w_down, ln2_g, ln2_b, loss_target, m_c_ctx, m_w_ada, m_b_ada, m_w_in, m_na_rpb, m_swa_sink, m_mla_q_norm, m_mla_kv_norm, m_mla_w_uq, m_mla_w_ukv, m_gqa_q_norm, m_gqa_k_norm, m_w_out, m_ln1_g, m_ln1_b, m_ffn_w_gate, m_ffn_w_up, m_ffn_conv_w, m_ffn_conv_b, m_ffn_w_down, m_ln2_g, m_ln2_b, v_c_ctx, v_w_ada, v_b_ada, v_w_in, v_na_rpb, v_swa_sink, v_mla_q_norm, v_mla_kv_norm, v_mla_w_uq, v_mla_w_ukv, v_gqa_q_norm, v_gqa_k_norm, v_w_out, v_ln1_g, v_ln1_b, v_ffn_w_gate, v_ffn_w_up, v_ffn_conv_w, v_ffn_conv_b, v_ffn_w_down, v_ln2_g, v_ln2_b):
    args = locals()
    w = {n: args[n] for n in _WEIGHTS}
    m_in = {n: args["m_" + n] for n in _WEIGHTS}
    v_in = {n: args["v_" + n] for n in _WEIGHTS}
    return _train_step(x, c, ctx, loss_target, w, m_in, v_in)
```

```python
import functools
import math

import numpy as np
import jax
import jax.numpy as jnp
from jax import lax
from jax.experimental import pallas as pl
from jax.experimental.pallas import tpu as pltpu

F32 = jnp.float32
BF16 = jnp.bfloat16
MESH = pl.DeviceIdType.MESH

GRID_W = 64
HEAD_DIM = 128
NA_HEADS, NA_WIN_R, NA_WIN_C = 4, 8, 16
SWA_HEADS, SWA_KV_HEADS, SWA_WINDOW = 4, 2, 128
MLA_HEADS, MLA_Q_LORA, MLA_KV_LORA, MLA_NOPE, MLA_ROPE, MLA_V = 4, 384, 128, 128, 64, 128
GQA_HEADS, GQA_KV_HEADS = 4, 2
ROPE_THETA = 10000.0
EPS = 1e-6
NEG = -1e30
DEPTH = 2
DEEPNORM_ALPHA = (2 * DEPTH) ** 0.25
ADAM_LR, ADAM_B1, ADAM_B2, ADAM_EPS, ADAM_WD, ADAM_STEP = 0.001, 0.9, 0.999, 1e-08, 0.01, 10

LANE = 128
V7X_VMEM_BYTES = 64 * 1024 * 1024
VMEM_LIMIT = 56 * 1024 * 1024
MM_VMEM_BUDGET = 40 * 1024 * 1024
EW_VMEM_BUDGET = 28 * 1024 * 1024
BQ = 128

CB_NA_Q, CB_NA_K, CB_NA_V = 0, 4, 8
CB_SWA_Q, CB_SWA_K, CB_SWA_V = 12, 16, 18
CB_CQ, CB_CKV = 20, 23
CB_GQA_Q, CB_GQA_K, CB_GQA_V = 24, 28, 30
CB_KPE = 32
PCOLS = 33 * LANE
IN_COLS = 4160


def _cparams(sem=None, **kw):
    return pltpu.CompilerParams(dimension_semantics=sem, vmem_limit_bytes=VMEM_LIMIT, **kw)


def _pick(n, target, mult=LANE):
    best = None
    for d in range(mult, min(n, target) + 1, mult):
        if n % d == 0:
            best = d
    return n if best is None else best


def _mm(a, b, *, mode="nn", out_dtype=F32, a_off=0, a_k=None, tm=1408, tn=1408, tk=2816, exact=False, add=None,
        stack=None, split4=False, rows=None, name):
    if mode == "tn":
        K, M = a.shape
        K2, N = b.shape
    elif mode == "nn":
        M, K = a.shape
        K2, N = b.shape
    else:
        M, K = a.shape
        N, K2 = b.shape
    if a_k is not None:
        K = a_k
    if rows is not None:
        if mode == "tn":
            assert rows <= min(K, K2)
            K = K2 = rows
        else:
            assert rows <= M
            M = rows
    assert K == K2, (a.shape, b.shape, mode)
    m_mult = LANE if mode == "tn" else 16
    n_cols = N // 4 if split4 else N
    bm, bn, bk = _pick(M, tm, m_mult), _pick(n_cols, tn), _pick(K, tk)
    sa, sb, so = a.dtype.itemsize, b.dtype.itemsize, jnp.dtype(out_dtype).itemsize

    def vmem_estimate():
        acc = bm * bn * 4 if K // bk > 1 else 0
        return 2 * (bm * bk * sa + bk * bn * sb) + acc + 2 * bm * bn * so + (2 * bm * bn * 4 if add is not None else 0)

    while vmem_estimate() > MM_VMEM_BUDGET:
        if bm >= bn and _pick(M, bm - 1, m_mult) < bm:
            bm = _pick(M, bm - 1, m_mult)
        elif _pick(n_cols, bn - 1) < bn:
            bn = _pick(n_cols, bn - 1)
        else:
            assert _pick(K, bk - 1) < bk, "no tiling fits VMEM"
            bk = _pick(K, bk - 1)
    assert a_off % bk == 0
    koff = a_off // bk
    nk = K // bk
    if mode == "tn":
        a_spec = pl.BlockSpec((bk, bm), lambda i, j, k: (k, i))
        b_spec = pl.BlockSpec((bk, bn), lambda i, j, k: (k, j))
        dims = (((0,), (0,)), ((), ()))
    elif mode == "nn":
        a_spec = pl.BlockSpec((bm, bk), lambda i, j, k: (i, k + koff))
        b_spec = pl.BlockSpec((bk, bn), lambda i, j, k: (k, j))
        dims = (((1,), (0,)), ((), ()))
    else:
        a_spec = pl.BlockSpec((bm, bk), lambda i, j, k: (i, k + koff))
        b_spec = pl.BlockSpec((bn, bk), lambda i, j, k: (j, k))
        dims = (((1,), (1,)), ((), ()))

    operands = [a, b]
    in_specs = [a_spec, b_spec]
    if add is not None:
        operands.append(add)
        in_specs.append(pl.BlockSpec((bm, bn), lambda i, j, k: (i, j)))
    aliases = {}
    if stack is None:
        out_spec = pl.BlockSpec((bm, bn), lambda i, j, k: (i, j))
        out_shape = jax.ShapeDtypeStruct((M, N), out_dtype)
    else:
        n_layers, layer, buf = stack
        if split4:
            nb = N // 4 // bn
            assert N % (4 * bn) == 0
            out_spec = pl.BlockSpec((None, None, bm, bn), lambda i, j, k: (layer, j // nb, i, j % nb))
            out_shape = jax.ShapeDtypeStruct((n_layers, 4, M, N // 4), out_dtype)
        else:
            out_spec = pl.BlockSpec((None, bm, bn), lambda i, j, k: (layer, i, j))
            out_shape = jax.ShapeDtypeStruct((n_layers, M, N), out_dtype)
        if buf is not None:
            aliases = {len(operands): 0}
            operands.append(buf)
            in_specs.append(pl.BlockSpec(memory_space=pl.ANY))
    has_add, has_buf = add is not None, bool(aliases)

    def body(*refs):
        a_ref, b_ref = refs[:2]
        add_ref = refs[2] if has_add else None
        o_ref = refs[2 + has_add + has_buf]
        if exact:
            prod = lax.dot_general(a_ref[...].astype(F32), b_ref[...].astype(F32), dims,
                                   precision=lax.Precision.HIGHEST, preferred_element_type=F32)
        else:
            prod = lax.dot_general(a_ref[...].astype(BF16), b_ref[...].astype(BF16), dims, preferred_element_type=F32)

        def finish(res):
            if has_add:
                res = res + add_ref[...].astype(F32)
            o_ref[...] = res.astype(o_ref.dtype)

        if nk == 1:
            finish(prod)
            return
        acc_ref = refs[-1]
        k = pl.program_id(2)

        @pl.when(k == 0)
        def _():
            acc_ref[...] = prod

        @pl.when((k > 0) & (k < nk - 1))
        def _():
            acc_ref[...] += prod

        @pl.when(k == nk - 1)
        def _():
            finish(acc_ref[...] + prod)

    return pl.pallas_call(
        body, name=name,
        grid=(M // bm, N // bn, nk),
        in_specs=in_specs, out_specs=out_spec, out_shape=out_shape,
        scratch_shapes=[pltpu.VMEM((bm, bn), F32)] if nk > 1 else [],
        input_output_aliases=aliases,
        compiler_params=_cparams(("parallel", "parallel", "arbitrary")),
    )(*operands)


def _row_block(T, S):
    return _pick(math.gcd(T, S), 256, 16)


def _ln_stats(x):
    mu = jnp.mean(x, axis=-1, keepdims=True)
    xc = x - mu
    var = jnp.mean(xc * xc, axis=-1, keepdims=True)
    rstd = lax.rsqrt(var + EPS)
    return xc * rstd, rstd


def _ln_bwd(dxhat, xhat, rstd):
    m1 = jnp.mean(dxhat, axis=-1, keepdims=True)
    m2 = jnp.mean(dxhat * xhat, axis=-1, keepdims=True)
    return rstd * (dxhat - m1 - xhat * m2)


def _sel(ref, is_ctx):
    return jnp.where(is_ctx, ref[1:2, :], ref[0:1, :])


def _mod_fwd(x, shift, scale, S, *, rows=None, name):
    T, D = (x.shape[0] if rows is None else rows), x.shape[1]
    bt = _row_block(T, S)

    def body(x_ref, sh_ref, sc_ref, o_ref):
        is_ctx = pl.program_id(0) * bt >= S
        xhat, _ = _ln_stats(x_ref[...])
        o_ref[...] = (xhat * (1.0 + _sel(sc_ref, is_ctx)) + _sel(sh_ref, is_ctx)).astype(o_ref.dtype)

    return pl.pallas_call(
        body, name=name, grid=(T // bt,),
        in_specs=[pl.BlockSpec((bt, D), lambda i: (i, 0)), pl.BlockSpec((2, D), lambda i: (0, 0)),
                  pl.BlockSpec((2, D), lambda i: (0, 0))],
        out_specs=pl.BlockSpec((bt, D), lambda i: (i, 0)),
        out_shape=jax.ShapeDtypeStruct((T, D), BF16),
        compiler_params=_cparams(("parallel",)),
    )(x, shift, scale)


def _acc_groups(acc_ref, row, val, is_ctx):
    f = jnp.where(is_ctx, 1.0, 0.0).astype(F32)
    acc_ref[row:row + 1, :] += val * (1.0 - f)
    acc_ref[row + 1:row + 2, :] += val * f


def _mod_bwd(x, dh, scale, dx_in, S, *, name):
    T, D = dh.shape
    bt = _row_block(T, S)
    in_blocks = dx_in.shape[0] // bt

    def body(x_ref, dh_ref, sc_ref, dxi_ref, dx_ref, acc_ref):
        i = pl.program_id(0)
        is_ctx = i * bt >= S

        @pl.when(i == 0)
        def _():
            acc_ref[...] = jnp.zeros_like(acc_ref)

        xhat, rstd = _ln_stats(x_ref[...])
        dh = dh_ref[...].astype(F32)
        dxhat = dh * (1.0 + _sel(sc_ref, is_ctx))
        dxi = dxi_ref[...] if in_blocks * bt == T else jnp.where(i < in_blocks, dxi_ref[...], 0.0)
        dx_ref[...] = dxi + _ln_bwd(dxhat, xhat, rstd)
        _acc_groups(acc_ref, 0, jnp.sum(dh, axis=0, keepdims=True), is_ctx)
        _acc_groups(acc_ref, 2, jnp.sum(dh * xhat, axis=0, keepdims=True), is_ctx)

    return pl.pallas_call(
        body, name=name, grid=(T // bt,),
        in_specs=[pl.BlockSpec((bt, D), lambda i: (i, 0)), pl.BlockSpec((bt, D), lambda i: (i, 0)),
                  pl.BlockSpec((2, D), lambda i: (0, 0)),
                  pl.BlockSpec((bt, D), lambda i: (jnp.minimum(i, in_blocks - 1), 0))],
        out_specs=[pl.BlockSpec((bt, D), lambda i: (i, 0)), pl.BlockSpec((8, D), lambda i: (0, 0))],
        out_shape=[jax.ShapeDtypeStruct((T, D), F32), jax.ShapeDtypeStruct((8, D), F32)],
        compiler_params=_cparams(("arbitrary",)),
    )(x, dh, scale, dx_in)


def _res_fwd(x, z, gate, lg, lb, S, *, name):
    T, D = z.shape
    bt = _row_block(T, S)

    def body(x_ref, z_ref, g_ref, lg_ref, lb_ref, o_ref):
        is_ctx = pl.program_id(0) * bt >= S
        u = DEEPNORM_ALPHA * x_ref[...] + _sel(g_ref, is_ctx) * z_ref[...]
        uhat, _ = _ln_stats(u)
        o_ref[...] = uhat * lg_ref[...] + lb_ref[...]

    row = pl.BlockSpec((bt, D), lambda i: (i, 0))
    return pl.pallas_call(
        body, name=name, grid=(T // bt,),
        in_specs=[row, row, pl.BlockSpec((2, D), lambda i: (0, 0)), pl.BlockSpec((1, D), lambda i: (0, 0)),
                  pl.BlockSpec((1, D), lambda i: (0, 0))],
        out_specs=row,
        out_shape=jax.ShapeDtypeStruct((T, D), F32),
        compiler_params=_cparams(("parallel",)),
    )(x, z, gate, lg, lb)


def _res_bwd(x, z, gate, lg, dy, S, *, name):
    T, D = z.shape
    bt = _row_block(T, S)

    def body(x_ref, z_ref, g_ref, lg_ref, dy_ref, dx_ref, dz_ref, acc_ref):
        i = pl.program_id(0)
        is_ctx = i * bt >= S

        @pl.when(i == 0)
        def _():
            acc_ref[...] = jnp.zeros_like(acc_ref)

        gate_v = _sel(g_ref, is_ctx)
        zv = z_ref[...]
        u = DEEPNORM_ALPHA * x_ref[...] + gate_v * zv
        uhat, rstd = _ln_stats(u)
        dyv = dy_ref[...]
        du = _ln_bwd(dyv * lg_ref[...], uhat, rstd)
        dx_ref[...] = DEEPNORM_ALPHA * du
        dz_ref[...] = (gate_v * du).astype(dz_ref.dtype)
        _acc_groups(acc_ref, 0, jnp.sum(du * zv, axis=0, keepdims=True), is_ctx)
        acc_ref[2:3, :] += jnp.sum(dyv * uhat, axis=0, keepdims=True)
        acc_ref[3:4, :] += jnp.sum(dyv, axis=0, keepdims=True)

    row = pl.BlockSpec((bt, D), lambda i: (i, 0))
    return pl.pallas_call(
        body, name=name, grid=(T // bt,),
        in_specs=[row, row, pl.BlockSpec((2, D), lambda i: (0, 0)), pl.BlockSpec((1, D), lambda i: (0, 0)), row],
        out_specs=[row, row, pl.BlockSpec((8, D), lambda i: (0, 0))],
        out_shape=[jax.ShapeDtypeStruct((T, D), F32), jax.ShapeDtypeStruct((T, D), BF16),
                   jax.ShapeDtypeStruct((8, D), F32)],
        compiler_params=_cparams(("arbitrary",)),
    )(x, z, gate, lg, dy)


def _loss_fwd_bwd(y, target, S, *, name):
    T, D = y.shape
    bt = _row_block(T, S)
    n_lat = S // bt

    def body(y_ref, t_ref, dy_ref, l_ref):
        i = pl.program_id(0)

        @pl.when(i == 0)
        def _():
            l_ref[...] = jnp.zeros_like(l_ref)

        keep = jnp.where(i * bt >= S, 0.0, 1.0).astype(F32)
        err = (y_ref[...] - t_ref[...]) * keep
        dy_ref[...] = err * (1.0 / D)
        l_ref[...] += jnp.sum(err * err) * (0.5 / D)

    return pl.pallas_call(
        body, name=name, grid=(T // bt,),
        in_specs=[pl.BlockSpec((bt, D), lambda i: (i, 0)),
                  pl.BlockSpec((bt, D), lambda i: (jnp.minimum(i, n_lat - 1), 0))],
        out_specs=[pl.BlockSpec((bt, D), lambda i: (i, 0)), pl.BlockSpec((8, LANE), lambda i: (0, 0))],
        out_shape=[jax.ShapeDtypeStruct((T, D), F32), jax.ShapeDtypeStruct((8, LANE), F32)],
        compiler_params=_cparams(("arbitrary",)),
    )(y, target)


def _rope_tables(S, C, dim):
    half = dim // 4
    t = jnp.arange(S)
    row = (t // GRID_W).astype(F32)
    col = (t % GRID_W).astype(F32)
    inv = ROPE_THETA ** (-jnp.arange(half, dtype=F32) / half)
    ar, ac = row[:, None] * inv[None, :], col[:, None] * inv[None, :]
    cos = jnp.concatenate([jnp.cos(ar), jnp.cos(ar), jnp.cos(ac), jnp.cos(ac)], axis=1)
    ss = jnp.concatenate([-jnp.sin(ar), jnp.sin(ar), -jnp.sin(ac), jnp.sin(ac)], axis=1)
    cos = jnp.pad(cos, ((0, C), (0, LANE - dim)), constant_values=1.0)
    ss = jnp.pad(ss, ((0, C), (0, LANE - dim)))
    return cos, ss


def _rope(x, cos, ss, half):
    lane = lax.broadcasted_iota(jnp.int32, x.shape, 1)
    first = (lane % (2 * half)) < half
    partner = jnp.where(first, pltpu.roll(x, LANE - half, 1), pltpu.roll(x, half, 1))
    return x * cos + partner * ss


def _rms(x):
    r = lax.rsqrt(jnp.mean(x * x, axis=-1, keepdims=True) + EPS)
    return x * r, r


_CAST_BLOCKS = tuple(range(0, 12)) + (18, 19, 30, 31)
_ROPE_BLOCKS = tuple(range(12, 18))
_GQA_Q_BLOCKS = tuple(range(24, 28))
_GQA_K_BLOCKS = (28, 29)


def _prep_fwd(p, tabs, gq, gk, mq, mkv, S, *, name):
    T = p.shape[0]
    bt = _row_block(T, S)
    cA, sA, cP, sP = tabs

    def body(p_ref, cA_ref, sA_ref, cP_ref, sP_ref, gq_ref, gk_ref, mq_ref, mkv_ref, o_ref):
        def blk(b):
            return p_ref[:, b * LANE:(b + 1) * LANE]

        def put(b, val):
            o_ref[:, b * LANE:(b + 1) * LANE] = val.astype(o_ref.dtype)

        cA_v, sA_v = cA_ref[...], sA_ref[...]
        for b in _CAST_BLOCKS:
            put(b, blk(b))
        for b in _ROPE_BLOCKS:
            put(b, _rope(blk(b), cA_v, sA_v, 32))
        for b in _GQA_Q_BLOCKS:
            put(b, _rope(_rms(blk(b))[0] * gq_ref[...], cA_v, sA_v, 32))
        for b in _GQA_K_BLOCKS:
            put(b, _rope(_rms(blk(b))[0] * gk_ref[...], cA_v, sA_v, 32))
        put(CB_KPE, _rope(blk(CB_KPE), cP_ref[...], sP_ref[...], 16))
        cq = p_ref[:, CB_CQ * LANE:CB_CKV * LANE]
        o_ref[:, CB_CQ * LANE:CB_CKV * LANE] = (_rms(cq)[0] * mq_ref[...]).astype(o_ref.dtype)
        put(CB_CKV, _rms(blk(CB_CKV))[0] * mkv_ref[...])

    row128 = pl.BlockSpec((bt, LANE), lambda i: (i, 0))
    vec = lambda n: pl.BlockSpec((1, n), lambda i: (0, 0))
    return pl.pallas_call(
        body, name=name, grid=(T // bt,),
        in_specs=[pl.BlockSpec((bt, PCOLS), lambda i: (i, 0)), row128, row128, row128, row128,
                  vec(LANE), vec(LANE), vec(MLA_Q_LORA), vec(LANE)],
        out_specs=pl.BlockSpec((bt, PCOLS), lambda i: (i, 0)),
        out_shape=jax.ShapeDtypeStruct((T, PCOLS), BF16),
        compiler_params=_cparams(("parallel",)),
    )(p, cA, sA, cP, sP, gq, gk, mq, mkv)


def _prep_bwd(p, grads, tabs, gq, gk, mq, mkv, S, *, name):
    T = p.shape[0]
    bt = _row_block(T, S)
    cA, sA, cP, sP = tabs
    arrays = []
    where = {}
    for key, (arr, cb) in grads.items():
        idx = next((n for n, a in enumerate(arrays) if a is arr), None)
        if idx is None:
            arrays.append(arr)
            idx = len(arrays) - 1
        where[key] = (idx, cb)
    ng = len(arrays)

    def body(*refs):
        p_ref, cA_ref, sA_ref, cP_ref, sP_ref, gq_ref, gk_ref, mq_ref, mkv_ref = refs[:9]
        g_refs = refs[9:9 + ng]
        o_ref, acc_ref = refs[9 + ng:]
        i = pl.program_id(0)

        @pl.when(i == 0)
        def _():
            acc_ref[...] = jnp.zeros_like(acc_ref)

        def blk(b):
            return p_ref[:, b * LANE:(b + 1) * LANE]

        def grad(b, width=LANE):
            idx, cb = where[b]
            return g_refs[idx][:, cb * LANE:cb * LANE + width].astype(F32)

        def put(b, val):
            o_ref[:, b * LANE:(b + 1) * LANE] = val.astype(o_ref.dtype)

        def rms_bwd(x, dy, g, row, width):
            n, r = _rms(x)
            acc_ref[row:row + 1, 0:width] += jnp.sum(dy * n, axis=0, keepdims=True)
            dn = dy * g
            return r * (dn - n * jnp.mean(dn * n, axis=-1, keepdims=True))

        cA_v, sA_v = cA_ref[...], sA_ref[...]
        for b in _CAST_BLOCKS:
            put(b, grad(b))
        for b in _ROPE_BLOCKS:
            put(b, _rope(grad(b), cA_v, -sA_v, 32))
        for b in _GQA_Q_BLOCKS:
            put(b, rms_bwd(blk(b), _rope(grad(b), cA_v, -sA_v, 32), gq_ref[...], 0, LANE))
        for b in _GQA_K_BLOCKS:
            put(b, rms_bwd(blk(b), _rope(grad(b), cA_v, -sA_v, 32), gk_ref[...], 1, LANE))
        put(CB_KPE, _rope(grad(CB_KPE), cP_ref[...], -sP_ref[...], 16))
        dcq = rms_bwd(p_ref[:, CB_CQ * LANE:CB_CKV * LANE], grad(CB_CQ, MLA_Q_LORA), mq_ref[...], 2, MLA_Q_LORA)
        o_ref[:, CB_CQ * LANE:CB_CKV * LANE] = dcq.astype(o_ref.dtype)
        put(CB_CKV, rms_bwd(blk(CB_CKV), grad(CB_CKV), mkv_ref[...], 3, LANE))

    row128 = pl.BlockSpec((bt, LANE), lambda i: (i, 0))
    vec = lambda n: pl.BlockSpec((1, n), lambda i: (0, 0))
    g_specs = [pl.BlockSpec((bt, a.shape[1]), lambda i: (i, 0)) for a in arrays]
    return pl.pallas_call(
        body, name=name, grid=(T // bt,),
        in_specs=[pl.BlockSpec((bt, PCOLS), lambda i: (i, 0)), row128, row128, row128, row128,
                  vec(LANE), vec(LANE), vec(MLA_Q_LORA), vec(LANE)] + g_specs,
        out_specs=[pl.BlockSpec((bt, PCOLS), lambda i: (i, 0)), pl.BlockSpec((8, MLA_Q_LORA), lambda i: (0, 0))],
        out_shape=[jax.ShapeDtypeStruct((T, PCOLS), BF16), jax.ShapeDtypeStruct((8, MLA_Q_LORA), F32)],
        compiler_params=_cparams(("arbitrary",)),
    )(p, cA, sA, cP, sP, gq, gk, mq, mkv, *arrays)


def _pe_rope(qm, tabs, S, sign, out_dtype, *, name):
    T, N = qm.shape
    bt = _row_block(T, S)
    cP, sP = tabs[2], tabs[3]

    def body(x_ref, c_ref, s_ref, o_ref):
        for b in range(MLA_HEADS):
            o_ref[:, b * LANE:(b + 1) * LANE] = x_ref[:, b * LANE:(b + 1) * LANE].astype(o_ref.dtype)
        for b in range(MLA_HEADS, 2 * MLA_HEADS):
            x = x_ref[:, b * LANE:(b + 1) * LANE].astype(F32)
            o_ref[:, b * LANE:(b + 1) * LANE] = _rope(x, c_ref[...], sign * s_ref[...], 16).astype(o_ref.dtype)

    row128 = pl.BlockSpec((bt, LANE), lambda i: (i, 0))
    return pl.pallas_call(
        body, name=name, grid=(T // bt,),
        in_specs=[pl.BlockSpec((bt, N), lambda i: (i, 0)), row128, row128],
        out_specs=pl.BlockSpec((bt, N), lambda i: (i, 0)),
        out_shape=jax.ShapeDtypeStruct((T, N), out_dtype),
        compiler_params=_cparams(("parallel",)),
    )(qm, cP, sP)


def _dot_nt(a, b):
    return lax.dot_general(a, b, (((1,), (1,)), ((), ())), preferred_element_type=F32)


def _dot_tn(a, b):
    return lax.dot_general(a, b, (((0,), (0,)), ((), ())), preferred_element_type=F32)


def _dot(a, b):
    return jnp.dot(a, b, preferred_element_type=F32)


def _window_fns(band, S, n_var):
    n_lat = S // BQ
    if band is None:
        return None
    reach, span = band

    def fns(j):
        start = jnp.clip(j - reach, 0, n_lat - span)
        return start, jnp.clip(j - start, 0, n_var - 1)

    return fns


class _AttnCfg:
    def __init__(self, *, Hkv, G, S, C, band, scale, n_var=0, bias_per_head=False, has_sink=False, two=False, bq=BQ,
                 ctx_queries=True):
        self.Hkv, self.G, self.S, self.C, self.band, self.scale = Hkv, G, S, C, band, scale
        self.n_var, self.bias_per_head, self.has_sink, self.two = n_var, bias_per_head, has_sink, two
        self.W = S if band is None else band[1] * BQ
        self.T = S + C
        self.bq, self.ctx_queries = bq, ctx_queries
        assert band is None or bq == BQ
        assert S % bq == 0 and C % bq == 0


def _attn_probs(cfg, j, q_ref, k_ref, q2_ref, k2_ref, bias_ref, sink_ref):
    G, S, C, W = cfg.G, cfg.S, cfg.C, cfg.W
    is_ctx = j * cfg.bq >= S
    if cfg.band is None:
        off, var = 0, 0
    else:
        start, var = _window_fns(cfg.band, S, cfg.n_var)(j)
        off = pl.multiple_of(start * BQ, BQ)
    qt = q_ref[...]
    qs = jnp.concatenate([qt[:, g * LANE:(g + 1) * LANE] for g in range(G)], axis=0) if G > 1 else qt
    kw = k_ref[pl.ds(off, W), :]
    kc = k_ref[pl.ds(S, C), :]
    s_w = _dot_nt(qs, kw)
    s_c = _dot_nt(qs, kc)
    q2s = k2w = k2c = None
    if cfg.two:
        q2s = q2_ref[...]
        k2w = k2_ref[pl.ds(off, W), :]
        k2c = k2_ref[pl.ds(S, C), :]
        s_w = s_w + _dot_nt(q2s, k2w)
        s_c = s_c + _dot_nt(q2s, k2c)
    operands = (off, var, qs, kw, kc, q2s, k2w, k2c)
    if not cfg.n_var and not cfg.has_sink:
        if cfg.ctx_queries:
            s_w = jnp.where(is_ctx, NEG, s_w)
        m = jnp.maximum(jnp.max(s_w, axis=-1, keepdims=True), jnp.max(s_c, axis=-1, keepdims=True))
        c2 = cfg.scale * math.log2(math.e)
        e_w = jnp.exp2((s_w - m) * c2)
        e_c = jnp.exp2((s_c - m) * c2)
        inv = 1.0 / (jnp.sum(e_w, axis=-1, keepdims=True) + jnp.sum(e_c, axis=-1, keepdims=True))
        return e_w * inv, e_c * inv, None, operands
    s_w = s_w * cfg.scale
    s_c = s_c * cfg.scale
    if cfg.n_var:
        b = bias_ref[0, pl.ds(var, 1)][0]
        s_w = s_w + (jnp.concatenate([b] * G, axis=0) if G > 1 else b)
    if cfg.ctx_queries:
        s_w = jnp.where(is_ctx, NEG, s_w)
    m = jnp.maximum(jnp.max(s_w, axis=-1, keepdims=True), jnp.max(s_c, axis=-1, keepdims=True))
    if cfg.has_sink:
        sink = sink_ref[0][:, 0:1]
        m = jnp.maximum(m, sink)
    e_w = jnp.exp(s_w - m)
    e_c = jnp.exp(s_c - m)
    l = jnp.sum(e_w, axis=-1, keepdims=True) + jnp.sum(e_c, axis=-1, keepdims=True)
    p_s = None
    if cfg.has_sink:
        e_s = jnp.exp(sink - m)
        l = l + e_s
    inv = 1.0 / l
    if cfg.has_sink:
        p_s = e_s * inv
    return e_w * inv, e_c * inv, p_s, operands


def _attn_specs(cfg, q_cb, k_cb, v_cb, q2_cb, k2_cb):
    G, T, bq = cfg.G, cfg.T, cfg.bq
    specs = [pl.BlockSpec((bq, G * LANE), lambda h, j: (j, q_cb // G + h)),
             pl.BlockSpec((T, LANE), lambda h, j: (0, k_cb + h)),
             pl.BlockSpec((T, LANE), lambda h, j: (0, v_cb + h))]
    if cfg.two:
        specs += [pl.BlockSpec((bq, LANE), lambda h, j: (j, q2_cb + h)),
                  pl.BlockSpec((T, LANE), lambda h, j: (0, k2_cb))]
    if cfg.n_var:
        if cfg.bias_per_head:
            specs.append(pl.BlockSpec((1, cfg.n_var, BQ, cfg.W), lambda h, j: (h, 0, 0, 0)))
        else:
            specs.append(pl.BlockSpec((1, cfg.n_var, BQ, cfg.W), lambda h, j: (0, 0, 0, 0)))
    if cfg.has_sink:
        specs.append(pl.BlockSpec((1, G * BQ, LANE), lambda h, j: (h, 0, 0)))
    return specs


def _attn_unpack(cfg, refs):
    refs = list(refs)
    q_ref, k_ref, v_ref = refs[:3]
    n = 3
    q2_ref = k2_ref = bias_ref = sink_ref = None
    if cfg.two:
        q2_ref, k2_ref = refs[n:n + 2]
        n += 2
    if cfg.n_var:
        bias_ref = refs[n]
        n += 1
    if cfg.has_sink:
        sink_ref = refs[n]
        n += 1
    return (q_ref, k_ref, v_ref, q2_ref, k2_ref, bias_ref, sink_ref), refs[n:]


def _attn_fwd(cfg, q, q_cb, k, k_cb, v, v_cb, *, q2=None, q2_cb=0, k2=None, k2_cb=0, bias=None, sink=None, name):
    G, T, S, C, W = cfg.G, cfg.T, cfg.S, cfg.C, cfg.W
    assert q_cb % G == 0
    operands = [q, k, v] + ([q2, k2] if cfg.two else []) + ([bias] if cfg.n_var else []) + ([sink] if cfg.has_sink else [])

    bq = cfg.bq

    def body(*refs):
        (q_ref, k_ref, v_ref, q2_ref, k2_ref, bias_ref, sink_ref), (o_ref,) = _attn_unpack(cfg, refs)
        j = pl.program_id(1)

        def block():
            p_w, p_c, _, (off, _, _, _, _, _, _, _) = _attn_probs(cfg, j, q_ref, k_ref, q2_ref, k2_ref, bias_ref, sink_ref)
            o = _dot(p_w.astype(BF16), v_ref[pl.ds(off, W), :]) + _dot(p_c.astype(BF16), v_ref[pl.ds(S, C), :])
            for g in range(G):
                o_ref[:, g * LANE:(g + 1) * LANE] = o[g * bq:(g + 1) * bq].astype(o_ref.dtype)

        if cfg.ctx_queries:
            block()
        else:
            pl.when(j * bq < S)(block)

            @pl.when(j * bq >= S)
            def _():
                o_ref[...] = jnp.zeros_like(o_ref)

    return pl.pallas_call(
        body, name=name, grid=(cfg.Hkv, T // bq),
        in_specs=_attn_specs(cfg, q_cb, k_cb, v_cb, q2_cb, k2_cb),
        out_specs=pl.BlockSpec((bq, G * LANE), lambda h, j: (j, h)),
        out_shape=jax.ShapeDtypeStruct((T, cfg.Hkv * G * LANE), BF16),
        compiler_params=_cparams(("parallel", "parallel")),
    )(*operands)


def _attn_bwd(cfg, q, q_cb, k, k_cb, v, v_cb, do, do_cb, *, q2=None, q2_cb=0, k2=None, k2_cb=0, bias=None, sink=None,
              want_dbias=False, dq_dtype=F32, name):
    G, T, S, C, W, Hkv = cfg.G, cfg.T, cfg.S, cfg.C, cfg.W, cfg.Hkv
    assert q_cb % G == 0 and do_cb % G == 0 and not (want_dbias and G > 1)
    operands = [q, k, v] + ([q2, k2] if cfg.two else []) + ([bias] if cfg.n_var else []) + ([sink] if cfg.has_sink else [])
    operands.append(do)
    in_specs = _attn_specs(cfg, q_cb, k_cb, v_cb, q2_cb, k2_cb)
    bq = cfg.bq
    do_blocks = do.shape[0] // bq
    assert do.shape[0] == T or (do.shape[0] == S and not cfg.ctx_queries)
    in_specs.append(pl.BlockSpec((bq, G * LANE), lambda h, j: (jnp.minimum(j, do_blocks - 1), do_cb // G + h)))

    out_specs = [pl.BlockSpec((bq, G * LANE), lambda h, j: (j, h)),
                 pl.BlockSpec((T, LANE), lambda h, j: (0, h)),
                 pl.BlockSpec((T, LANE), lambda h, j: (0, h))]
    out_shape = [jax.ShapeDtypeStruct((T, Hkv * G * LANE), dq_dtype),
                 jax.ShapeDtypeStruct((T, Hkv * LANE), F32),
                 jax.ShapeDtypeStruct((T, Hkv * LANE), F32)]
    if cfg.two:
        out_specs += [pl.BlockSpec((bq, LANE), lambda h, j: (j, h)), pl.BlockSpec((T, LANE), lambda h, j: (0, 0))]
        out_shape += [jax.ShapeDtypeStruct((T, Hkv * LANE), dq_dtype), jax.ShapeDtypeStruct((T, LANE), F32)]
    if want_dbias:
        out_specs.append(pl.BlockSpec((1, cfg.n_var, BQ, W), lambda h, j: (h, 0, 0, 0)))
        out_shape.append(jax.ShapeDtypeStruct((Hkv, cfg.n_var, BQ, W), F32))
    if cfg.has_sink:
        out_specs.append(pl.BlockSpec((1, G * BQ, LANE), lambda h, j: (h, 0, 0)))
        out_shape.append(jax.ShapeDtypeStruct((Hkv, G * BQ, LANE), F32))

    def body(*refs):
        (q_ref, k_ref, v_ref, q2_ref, k2_ref, bias_ref, sink_ref), rest = _attn_unpack(cfg, refs)
        do_ref, dq_ref, dk_ref, dv_ref = rest[:4]
        rest = rest[4:]
        dq2_ref = dk2_ref = dbias_ref = dsink_ref = None
        if cfg.two:
            dq2_ref, dk2_ref = rest[:2]
            rest = rest[2:]
        if want_dbias:
            dbias_ref = rest[0]
            rest = rest[1:]
        if cfg.has_sink:
            dsink_ref = rest[0]
        h = pl.program_id(0)
        j = pl.program_id(1)

        @pl.when(j == 0)
        def _():
            dk_ref[...] = jnp.zeros_like(dk_ref)
            dv_ref[...] = jnp.zeros_like(dv_ref)
            if want_dbias:
                dbias_ref[...] = jnp.zeros_like(dbias_ref)
            if cfg.has_sink:
                dsink_ref[...] = jnp.zeros_like(dsink_ref)

        if cfg.two:
            @pl.when((j == 0) & (h == 0))
            def _():
                dk2_ref[...] = jnp.zeros_like(dk2_ref)

        def block():
            p_w, p_c, p_s, (off, var, qs, kw, kc, q2s, k2w, k2c) = _attn_probs(
                cfg, j, q_ref, k_ref, q2_ref, k2_ref, bias_ref, sink_ref)
            dot_ = do_ref[...]
            dos = jnp.concatenate([dot_[:, g * LANE:(g + 1) * LANE] for g in range(G)], axis=0) if G > 1 else dot_
            dos = dos.astype(BF16)
            vw = v_ref[pl.ds(off, W), :]
            vc = v_ref[pl.ds(S, C), :]
            dp_w = _dot_nt(dos, vw)
            dp_c = _dot_nt(dos, vc)
            delta = jnp.sum(p_w * dp_w, axis=-1, keepdims=True) + jnp.sum(p_c * dp_c, axis=-1, keepdims=True)
            ds_w = p_w * (dp_w - delta)
            ds_c = p_c * (dp_c - delta)
            if want_dbias:
                dbias_ref[0, pl.ds(var, 1)] += ds_w[None]
            if cfg.has_sink:
                dsink_ref[0] += jnp.broadcast_to(-(p_s * delta), (G * bq, LANE))
            dsw = (ds_w * cfg.scale).astype(BF16)
            dsc = (ds_c * cfg.scale).astype(BF16)
            dq = _dot(dsw, kw) + _dot(dsc, kc)
            for g in range(G):
                dq_ref[:, g * LANE:(g + 1) * LANE] = dq[g * bq:(g + 1) * bq].astype(dq_ref.dtype)
            dk_ref[pl.ds(off, W), :] += _dot_tn(dsw, qs)
            dk_ref[pl.ds(S, C), :] += _dot_tn(dsc, qs)
            dv_ref[pl.ds(off, W), :] += _dot_tn(p_w.astype(BF16), dos)
            dv_ref[pl.ds(S, C), :] += _dot_tn(p_c.astype(BF16), dos)
            if cfg.two:
                dq2_ref[...] = (_dot(dsw, k2w) + _dot(dsc, k2c)).astype(dq2_ref.dtype)
                dk2_ref[pl.ds(off, W), :] += _dot_tn(dsw, q2s)
                dk2_ref[pl.ds(S, C), :] += _dot_tn(dsc, q2s)

        if cfg.ctx_queries:
            block()
        else:
            pl.when(j * bq < S)(block)

            @pl.when(j * bq >= S)
            def _():
                dq_ref[...] = jnp.zeros_like(dq_ref)
                if cfg.two:
                    dq2_ref[...] = jnp.zeros_like(dq2_ref)

    return pl.pallas_call(
        body, name=name, grid=(Hkv, T // bq),
        in_specs=in_specs, out_specs=out_specs, out_shape=out_shape,
        compiler_params=_cparams(("arbitrary", "arbitrary")),
    )(*operands)


def _na_bias(rpb, S):
    H = rpb.shape[0]
    rows = S // GRID_W
    pad_l = GRID_W - 1 - (NA_WIN_C - 1)
    ext = jnp.concatenate([jnp.broadcast_to(rpb[:, :, :1], (H, 2 * NA_WIN_R - 1, pad_l)), rpb,
                           jnp.broadcast_to(rpb[:, :, -1:], (H, 2 * NA_WIN_R - 1, pad_l))], axis=2)
    by_col = jnp.stack([ext[:, :, GRID_W - 1 - qc:2 * GRID_W - 1 - qc] for qc in range(GRID_W)], axis=2)
    cq = np.arange(GRID_W)
    c0 = np.clip(cq - NA_WIN_C // 2, 0, GRID_W - NA_WIN_C)
    col_in = (cq[None, :] >= c0[:, None]) & (cq[None, :] < c0[:, None] + NA_WIN_C)
    n_lat = S // BQ
    neg_tile = jnp.full((H, GRID_W, GRID_W), NEG, F32)
    variants = []
    for v in range(5):
        j = {0: 0, 1: 1, 2: 2, 3: n_lat - 2, 4: n_lat - 1}[v]
        start = int(np.clip(j - 2, 0, n_lat - 5))
        assert j - start == v
        q_rows = []
        for qr in range(2):
            r = 2 * j + qr
            r0 = int(np.clip(r - NA_WIN_R // 2, 0, rows - NA_WIN_R))
            k_tiles = []
            for kr in range(10):
                krow = 2 * start + kr
                if r0 <= krow < r0 + NA_WIN_R:
                    k_tiles.append(jnp.where(col_in[None], by_col[:, krow - r + NA_WIN_R - 1], NEG))
                else:
                    k_tiles.append(neg_tile)
            q_rows.append(jnp.concatenate(k_tiles, axis=2))
        variants.append(jnp.concatenate(q_rows, axis=1))
    return jnp.stack(variants, axis=1)


def _swa_mask(S):
    qq = np.arange(BQ)[:, None]
    kk = np.arange(3 * BQ)[None, :]
    tiles = [np.where(np.abs(kk - v * BQ - qq) <= SWA_WINDOW, 0.0, NEG) for v in range(3)]
    return jnp.asarray(np.stack(tiles)[None], F32)


def _ffn_tiles(T, S, F):
    return _row_block(T, S), _pick(F, 1408)


def _halo_specs(T, bt, bf, col_off):
    n8 = bt // 8
    return [pl.BlockSpec((bt, bf), lambda f, i: (i, f + col_off)),
            pl.BlockSpec((8, bf), lambda f, i: (jnp.maximum(i * n8 - 1, 0), f + col_off)),
            pl.BlockSpec((8, bf), lambda f, i: (jnp.minimum((i + 1) * n8, T // 8 - 1), f + col_off))]


def _neighbours(x, prev8, next8, i, bt, S, T):
    r = lax.broadcasted_iota(jnp.int32, x.shape, 0)
    g0 = i * bt
    first_open = jnp.logical_or(g0 == 0, g0 == S)
    last_open = jnp.logical_or(g0 + bt == S, g0 + bt == T)
    before = jnp.where(r == 0, jnp.where(first_open, 0.0, prev8[7:8, :]), pltpu.roll(x, 1, 0))
    after = jnp.where(r == bt - 1, jnp.where(last_open, 0.0, next8[0:1, :]), pltpu.roll(x, bt - 1, 0))
    return before, after


def _sigmoid(a):
    return 1.0 / (1.0 + jnp.exp(-a))


def _ffn_fwd(gp, u, cw, cb, S, *, name):
    T, F = gp.shape
    bt, bf = _ffn_tiles(T, S, F)

    def body(g_ref, gp_ref, gn_ref, u_ref, w_ref, b_ref, o_ref):
        i = pl.program_id(1)
        g = g_ref[...]
        before, after = _neighbours(g, gp_ref[...], gn_ref[...], i, bt, S, T)
        a = before * w_ref[0:1, :] + g * w_ref[1:2, :] + after * w_ref[2:3, :] + b_ref[...]
        o_ref[...] = (a * _sigmoid(a) * u_ref[...]).astype(o_ref.dtype)

    return pl.pallas_call(
        body, name=name, grid=(F // bf, T // bt),
        in_specs=_halo_specs(T, bt, bf, 0) + [pl.BlockSpec((bt, bf), lambda f, i: (i, f)),
                                              pl.BlockSpec((3, bf), lambda f, i: (0, f)),
                                              pl.BlockSpec((1, bf), lambda f, i: (0, f))],
        out_specs=pl.BlockSpec((bt, bf), lambda f, i: (i, f)),
        out_shape=jax.ShapeDtypeStruct((T, F), BF16),
        compiler_params=_cparams(("parallel", "parallel")),
    )(gp, gp, gp, u, cw, cb)


def _ffn_bwd_act(gp, u, da_out, cw, cb, S, *, name):
    T, F = gp.shape
    bt, bf = _ffn_tiles(T, S, F)

    def body(g_ref, gp_ref, gn_ref, u_ref, d_ref, w_ref, b_ref, da_ref, du_ref, acc_ref):
        i = pl.program_id(1)

        @pl.when(i == 0)
        def _():
            acc_ref[...] = jnp.zeros_like(acc_ref)

        g = g_ref[...]
        before, after = _neighbours(g, gp_ref[...], gn_ref[...], i, bt, S, T)
        a = before * w_ref[0:1, :] + g * w_ref[1:2, :] + after * w_ref[2:3, :] + b_ref[...]
        sig = _sigmoid(a)
        d = d_ref[...]
        du_ref[...] = (d * (a * sig)).astype(du_ref.dtype)
        da = d * u_ref[...] * (sig * (1.0 + a * (1.0 - sig)))
        da_ref[...] = da
        acc_ref[0:1, :] += jnp.sum(da * before, axis=0, keepdims=True)
        acc_ref[1:2, :] += jnp.sum(da * g, axis=0, keepdims=True)
        acc_ref[2:3, :] += jnp.sum(da * after, axis=0, keepdims=True)
        acc_ref[3:4, :] += jnp.sum(da, axis=0, keepdims=True)

    blk = pl.BlockSpec((bt, bf), lambda f, i: (i, f))
    return pl.pallas_call(
        body, name=name, grid=(F // bf, T // bt),
        in_specs=_halo_specs(T, bt, bf, 0) + [blk, blk,
                                              pl.BlockSpec((3, bf), lambda f, i: (0, f)),
                                              pl.BlockSpec((1, bf), lambda f, i: (0, f))],
        out_specs=[blk, blk, pl.BlockSpec((8, bf), lambda f, i: (0, f))],
        out_shape=[jax.ShapeDtypeStruct((T, F), F32), jax.ShapeDtypeStruct((T, F), BF16),
                   jax.ShapeDtypeStruct((8, F), F32)],
        compiler_params=_cparams(("parallel", "arbitrary")),
    )(gp, gp, gp, u, da_out, cw, cb)


def _ffn_bwd_conv(da, cw, S, *, name):
    T, F = da.shape
    bt, bf = _ffn_tiles(T, S, F)

    def body(d_ref, dp_ref, dn_ref, w_ref, o_ref):
        i = pl.program_id(1)
        d = d_ref[...]
        before, after = _neighbours(d, dp_ref[...], dn_ref[...], i, bt, S, T)
        o_ref[...] = (after * w_ref[0:1, :] + d * w_ref[1:2, :] + before * w_ref[2:3, :]).astype(o_ref.dtype)

    return pl.pallas_call(
        body, name=name, grid=(F // bf, T // bt),
        in_specs=_halo_specs(T, bt, bf, 0) + [pl.BlockSpec((3, bf), lambda f, i: (0, f))],
        out_specs=pl.BlockSpec((bt, bf), lambda f, i: (i, f)),
        out_shape=jax.ShapeDtypeStruct((T, F), BF16),
        compiler_params=_cparams(("parallel", "parallel")),
    )(da, da, da, cw)


def _ew_rows(R, N, n_arrays):
    return _pick(R, max(16, EW_VMEM_BUDGET // (8 * n_arrays * N)), 16)


def _adam(w, g, m, v, *, name):
    R, N = w.shape
    br = _ew_rows(R, N, 7)
    bc1 = 1.0 - ADAM_B1 ** ADAM_STEP
    bc2 = 1.0 - ADAM_B2 ** ADAM_STEP

    def body(w_ref, g_ref, m_ref, v_ref, d_ref, mo_ref, vo_ref):
        gv = g_ref[...]
        mn = ADAM_B1 * m_ref[...] + (1.0 - ADAM_B1) * gv
        vn = ADAM_B2 * v_ref[...] + (1.0 - ADAM_B2) * (gv * gv)
        mo_ref[...] = mn
        vo_ref[...] = vn
        d_ref[...] = -ADAM_LR * ((mn / bc1) / (jnp.sqrt(vn / bc2) + ADAM_EPS) + ADAM_WD * w_ref[...])

    blk = pl.BlockSpec((br, N), lambda i: (i, 0))
    shp = jax.ShapeDtypeStruct((R, N), F32)
    return pl.pallas_call(
        body, name=name, grid=(R // br,),
        in_specs=[blk, blk, blk, blk], out_specs=[blk, blk, blk], out_shape=[shp, shp, shp],
        compiler_params=_cparams(("parallel",)),
    )(w, g, m, v)


def _sum_lead(x, out_dtype, *, name):
    n, R, N = x.shape
    br = _ew_rows(R, N, n + 1)

    def body(x_ref, o_ref):
        acc = x_ref[0].astype(F32)
        for k in range(1, n):
            acc = acc + x_ref[k].astype(F32)
        o_ref[...] = acc.astype(o_ref.dtype)

    return pl.pallas_call(
        body, name=name, grid=(R // br,),
        in_specs=[pl.BlockSpec((n, br, N), lambda i: (0, i, 0))],
        out_specs=pl.BlockSpec((br, N), lambda i: (i, 0)),
        out_shape=jax.ShapeDtypeStruct((R, N), out_dtype),
        compiler_params=_cparams(("parallel",)),
    )(x)


def _sum4_layers(x, core, *, name):
    L, n, R, N = x.shape
    br = _ew_rows(R, N, n + 1)

    def body(c_ref, x_ref, o_ref):
        acc = x_ref[0].astype(F32)
        for k in range(1, n):
            acc = acc + x_ref[k].astype(F32)
        o_ref[...] = acc

    return pl.pallas_call(
        body, name=name,
        grid_spec=pltpu.PrefetchScalarGridSpec(
            num_scalar_prefetch=1, grid=(L, R // br),
            in_specs=[pl.BlockSpec((None, n, br, N), lambda l, i, c_ref: (l, 0, i, 0))],
            out_specs=pl.BlockSpec((None, None, br, N), lambda l, i, c_ref: (l, c_ref[0], i, 0))),
        out_shape=jax.ShapeDtypeStruct((L, 2, R, N), F32),
        compiler_params=_cparams(("parallel", "parallel")),
    )(core.reshape(1).astype(jnp.int32), x)


def _add_half(g, r, core, *, name):
    L, Q, _, R, N = g.shape
    br = _ew_rows(R, N, 3)

    def body(c_ref, g_ref, r_ref, o_ref):
        o_ref[...] = (g_ref[...] + r_ref[...]).astype(o_ref.dtype)

    return pl.pallas_call(
        body, name=name,
        grid_spec=pltpu.PrefetchScalarGridSpec(
            num_scalar_prefetch=1, grid=(L, Q, R // br),
            in_specs=[pl.BlockSpec((None, None, None, br, N), lambda l, q, i, c_ref: (l, q, c_ref[0], i, 0)),
                      pl.BlockSpec((None, None, br, N), lambda l, q, i, c_ref: (l, q, i, 0))],
            out_specs=pl.BlockSpec((None, None, br, N), lambda l, q, i, c_ref: (l, q, i, 0))),
        out_shape=jax.ShapeDtypeStruct((L, Q, R, N), BF16),
        compiler_params=_cparams(("parallel", "parallel", "parallel")),
    )(core.reshape(1).astype(jnp.int32), g, r)


_ANY = pl.BlockSpec(memory_space=pl.ANY)


def _place():
    return lax.axis_index("x"), lax.axis_index("y"), lax.axis_index("c")


def _allgather8(blocks, *, name):
    n = len(blocks)

    def body(*refs):
        xs, outs = refs[:n], refs[n:2 * n]
        send_sems, recv_sems, local_sems = refs[2 * n:]
        x, y, c = _place()
        me, sibling = (x, y, c), (x, y, 1 - c)
        chips = [(1 - x, y), (x, 1 - y), (1 - x, 1 - y)]

        def slot(a, px, py, pc):
            return outs[a].at[4 * px + 2 * py + pc]

        def copy(a, k, block, to, src=None):
            return pltpu.make_async_remote_copy(
                src_ref=slot(a, *block) if src is None else src, dst_ref=slot(a, *block),
                send_sem=send_sems.at[a, k], recv_sem=recv_sems.at[a, k], device_id=to, device_id_type=MESH)

        mine = [pltpu.make_async_copy(xs[a], slot(a, *me), local_sems.at[a]) for a in range(n)]
        for cp in mine:
            cp.start()
        first = []
        for a in range(n):
            first.append(copy(a, 0, me, sibling, src=xs[a]))
            first += [copy(a, 1 + j, me, (*chip, c), src=xs[a]) for j, chip in enumerate(chips)]
        for cp in first:
            cp.start()
        passed = []
        for j, chip in enumerate(chips):
            for a in range(n):
                copy(a, 1 + j, (*chip, c), me).wait_recv()
                fwd = copy(a, 4 + j, (*chip, c), sibling)
                fwd.start()
                passed.append(fwd)
        for a in range(n):
            copy(a, 0, sibling, me).wait_recv()
            for j, chip in enumerate(chips):
                copy(a, 4 + j, (*chip, 1 - c), me).wait_recv()
        for cp in first + passed:
            cp.wait_send()
        for cp in mine:
            cp.wait()

    return pl.pallas_call(
        body, name=name,
        in_specs=[_ANY] * n, out_specs=[_ANY] * n,
        out_shape=[jax.ShapeDtypeStruct((8,) + b.shape, b.dtype) for b in blocks],
        scratch_shapes=[pltpu.SemaphoreType.DMA((n, 7)), pltpu.SemaphoreType.DMA((n, 7)), pltpu.SemaphoreType.DMA((n,))],
    )(*blocks)


def _send_other_half(gs, *, name):
    n = len(gs)

    def body(*refs):
        xs, outs = refs[:n], refs[n:2 * n]
        send_sems, recv_sems = refs[2 * n:]
        x, y, c = _place()
        cps = [pltpu.make_async_remote_copy(
            src_ref=xs[a].at[:, :, 1 - c], dst_ref=outs[a], send_sem=send_sems.at[a], recv_sem=recv_sems.at[a],
            device_id=(x, y, 1 - c), device_id_type=MESH) for a in range(n)]
        for cp in cps:
            cp.start()
        for cp in cps:
            cp.wait()

    return pl.pallas_call(
        body, name=name,
        in_specs=[_ANY] * n, out_specs=[_ANY] * n,
        out_shape=[jax.ShapeDtypeStruct(g.shape[:2] + g.shape[3:], g.dtype) for g in gs],
        scratch_shapes=[pltpu.SemaphoreType.DMA((n,)), pltpu.SemaphoreType.DMA((n,))],
    )(*gs)


def _chip_scatter(parts, *, name):
    n = len(parts)

    def body(*refs):
        xs, outs = refs[:n], refs[n:2 * n]
        send_sems, recv_sems, local_sems = refs[2 * n:]
        x, y, c = _place()
        q_me = 2 * x + y
        chips = [(1 - x, y), (x, 1 - y), (1 - x, 1 - y)]
        own = [pltpu.make_async_copy(xs[a].at[:, q_me], outs[a].at[:, q_me], local_sems.at[a]) for a in range(n)]
        for cp in own:
            cp.start()
        sends, recvs = [], []
        for a in range(n):
            for k, (px, py) in enumerate(chips):
                q_to = 2 * px + py
                sends.append(pltpu.make_async_remote_copy(
                    src_ref=xs[a].at[:, q_to], dst_ref=outs[a].at[:, q_me], send_sem=send_sems.at[a, k],
                    recv_sem=recv_sems.at[a, k], device_id=(px, py, c), device_id_type=MESH))
                recvs.append(pltpu.make_async_remote_copy(
                    src_ref=xs[a].at[:, q_me], dst_ref=outs[a].at[:, q_to], send_sem=send_sems.at[a, k],
                    recv_sem=recv_sems.at[a, k], device_id=(px, py, c), device_id_type=MESH))
        for cp in sends:
            cp.start()
        for cp in recvs:
            cp.wait_recv()
        for cp in sends:
            cp.wait_send()
        for cp in own:
            cp.wait()

    return pl.pallas_call(
        body, name=name,
        in_specs=[_ANY] * n, out_specs=[_ANY] * n,
        out_shape=[jax.ShapeDtypeStruct(p.shape, p.dtype) for p in parts],
        scratch_shapes=[pltpu.SemaphoreType.DMA((n, 3)), pltpu.SemaphoreType.DMA((n, 3)), pltpu.SemaphoreType.DMA((n,))],
    )(*parts)


def _join_halves(bufs, *, name):
    n = len(bufs)

    def body(*refs):
        xs, outs = refs[:n], refs[n:2 * n]
        send_sems, recv_sems = refs[2 * n:]
        x, y, c = _place()
        sends = [pltpu.make_async_remote_copy(
            src_ref=xs[a].at[:, c], dst_ref=outs[a].at[:, c], send_sem=send_sems.at[a], recv_sem=recv_sems.at[a],
            device_id=(x, y, 1 - c), device_id_type=MESH) for a in range(n)]
        recvs = [pltpu.make_async_remote_copy(
            src_ref=xs[a].at[:, c], dst_ref=outs[a].at[:, 1 - c], send_sem=send_sems.at[a], recv_sem=recv_sems.at[a],
            device_id=(x, y, 1 - c), device_id_type=MESH) for a in range(n)]
        for cp in sends:
            cp.start()
        for cp in recvs:
            cp.wait_recv()
        for cp in sends:
            cp.wait_send()

    return pl.pallas_call(
        body, name=name,
        in_specs=[_ANY] * n, out_specs=[_ANY] * n,
        out_shape=[jax.ShapeDtypeStruct(b.shape, b.dtype) for b in bufs],
        input_output_aliases={a: a for a in range(n)},
        scratch_shapes=[pltpu.SemaphoreType.DMA((n,)), pltpu.SemaphoreType.DMA((n,))],
    )(*bufs)


def _perm_w_in(w):
    pad = jnp.zeros((w.shape[0], PCOLS - IN_COLS), w.dtype)
    return jnp.concatenate([w[:, :3072], w[:, 3136:IN_COLS], w[:, 3072:3136], pad], axis=1)


def _unperm_w_in(g):
    return jnp.concatenate([g[:, :3072], g[:, 4096:IN_COLS], g[:, 3072:4096]], axis=1)


def _perm_w_uq(w):
    w4 = w.reshape(MLA_Q_LORA, MLA_HEADS, MLA_NOPE + MLA_ROPE)
    nope = w4[:, :, :MLA_NOPE].reshape(MLA_Q_LORA, MLA_HEADS * LANE)
    pe = jnp.pad(w4[:, :, MLA_NOPE:], ((0, 0), (0, 0), (0, LANE - MLA_ROPE))).reshape(MLA_Q_LORA, MLA_HEADS * LANE)
    return jnp.concatenate([nope, pe], axis=1)


def _unperm_w_uq(g):
    nope = g[:, :MLA_HEADS * LANE].reshape(MLA_Q_LORA, MLA_HEADS, LANE)
    pe = g[:, MLA_HEADS * LANE:].reshape(MLA_Q_LORA, MLA_HEADS, LANE)[:, :, :MLA_ROPE]
    return jnp.concatenate([nope, pe], axis=2).reshape(MLA_Q_LORA, MLA_HEADS * (MLA_NOPE + MLA_ROPE))


def _perm_w_ukv(w):
    w4 = w.reshape(MLA_KV_LORA, MLA_HEADS, MLA_NOPE + MLA_V)
    return jnp.concatenate([w4[:, :, :MLA_NOPE].reshape(MLA_KV_LORA, -1), w4[:, :, MLA_NOPE:].reshape(MLA_KV_LORA, -1)], axis=1)


def _unperm_w_ukv(g):
    kn = g[:, :MLA_HEADS * LANE].reshape(MLA_KV_LORA, MLA_HEADS, LANE)
    vv = g[:, MLA_HEADS * LANE:].reshape(MLA_KV_LORA, MLA_HEADS, LANE)
    return jnp.concatenate([kn, vv], axis=2).reshape(MLA_KV_LORA, -1)


def _silu(v):
    return v * jax.nn.sigmoid(v)


def _silu_grad(v):
    s = jax.nn.sigmoid(v)
    return s * (1.0 + v * (1.0 - s))


_WEIGHTS = ("c_ctx", "w_ada", "b_ada", "w_in", "na_rpb", "swa_sink", "mla_q_norm", "mla_kv_norm", "mla_w_uq", "mla_w_ukv",
            "gqa_q_norm", "gqa_k_norm", "w_out", "ln1_g", "ln1_b", "ffn_w_gate", "ffn_w_up", "ffn_conv_w", "ffn_conv_b",
            "ffn_w_down", "ln2_g", "ln2_b")
_COL_SHARDED = ("w_in", "mla_w_uq", "mla_w_ukv", "ffn_w_gate", "ffn_w_up")
_ROW_SHARDED = ("w_out", "ffn_w_down")
_BIG = _COL_SHARDED + _ROW_SHARDED
_SMALL = ("c_ctx", "b_ada", "na_rpb", "swa_sink", "mla_q_norm", "mla_kv_norm", "gqa_q_norm", "gqa_k_norm", "ln1_g", "ln1_b",
          "ffn_conv_w", "ffn_conv_b", "ln2_g", "ln2_b")


def _pack(arrays):
    flat = jnp.concatenate([a.reshape(-1) for a in arrays])
    n = flat.shape[0]
    rows = -(-n // (8 * LANE)) * 8
    return jnp.pad(flat, (0, rows * LANE - n)).reshape(rows, LANE)


def _unpack(packed, like):
    flat = packed.reshape(-1)
    out, o = [], 0
    for a in like:
        out.append(flat[o:o + a.size].reshape(a.shape))
        o += a.size
    return out


def _train_step(x, c, ctx, loss_target, w, m_in, v_in):
    L = DEPTH
    S, D = x.shape[1], x.shape[2]
    C = ctx.shape[1]
    T = S + C
    F = w["ffn_conv_b"].shape[1]
    ax, ay, ac = _place()
    chip = 2 * ax + ay
    dev = 2 * chip + ac
    n_ada = w["w_ada"].shape[2]

    def my_half(a):
        r = a.shape[1] // 2
        return lax.dynamic_slice_in_dim(a, ac * r, r, axis=1).astype(BF16)

    gathered = _allgather8([my_half(w[n]) for n in _BIG] + [w["ffn_conv_w"]], name="gather_weights")
    full = {}
    for n, g in zip(_BIG, gathered[:-1]):
        _, _, r, cols = g.shape
        g = g.reshape(4, 2, L, r, cols)
        if n in _COL_SHARDED:
            full[n] = g.transpose(2, 1, 3, 0, 4).reshape(L, 2 * r, 4 * cols)
        else:
            full[n] = g.transpose(2, 0, 1, 3, 4).reshape(L, 8 * r, cols)
    conv_w = gathered[-1][::2].transpose(1, 2, 0, 3).reshape(L, 3, F)
    w_in_p = [_perm_w_in(full["w_in"][l]) for l in range(L)]
    w_uq_p = [_perm_w_uq(full["mla_w_uq"][l]) for l in range(L)]
    w_ukv_p = [_perm_w_ukv(full["mla_w_ukv"][l]) for l in range(L)]

    (c_all,) = _allgather8([c], name="gather_c")
    c16 = jnp.concatenate([c_all.reshape(8, D), jnp.broadcast_to(w["c_ctx"][None], (8, D))], axis=0)
    row_keep = (jnp.arange(16) <= 8).astype(F32)[:, None]
    sc = _silu(c16) * row_keep
    b_loc = lax.dynamic_slice_in_dim(w["b_ada"], chip * n_ada, n_ada, axis=1)
    mod_loc = jnp.stack([_mm(sc, w["w_ada"][l], name="mod_mm") + b_loc[l][None] for l in range(L)])
    (mod_g,) = _allgather8([mod_loc], name="gather_mod")
    mod_all = mod_g[::2].transpose(1, 2, 0, 3).reshape(L, 16, 4 * n_ada)
    mod_x = lax.dynamic_index_in_dim(mod_all, dev, axis=1, keepdims=False)
    mod_c = mod_all[:, 8]
    mods = [jnp.stack([mod_x[l].reshape(6, D), mod_c[l].reshape(6, D)], axis=1) for l in range(L)]

    tabs = _rope_tables(S, C, HEAD_DIM) + _rope_tables(S, C, MLA_ROPE)
    swa_mask = _swa_mask(S)
    scale = HEAD_DIM ** -0.5
    def attn_cfgs(l):
        cq = l < L - 1
        return (_AttnCfg(Hkv=NA_HEADS, G=1, S=S, C=C, band=(2, 5), scale=scale, n_var=5, bias_per_head=True, ctx_queries=cq),
                _AttnCfg(Hkv=SWA_KV_HEADS, G=SWA_HEADS // SWA_KV_HEADS, S=S, C=C, band=(1, 3), scale=scale, n_var=3,
                         has_sink=True, ctx_queries=cq),
                _AttnCfg(Hkv=MLA_HEADS, G=1, S=S, C=C, band=None, scale=(MLA_NOPE + MLA_ROPE) ** -0.5, two=True,
                         bq=2 * BQ, ctx_queries=cq),
                _AttnCfg(Hkv=GQA_KV_HEADS, G=GQA_HEADS // GQA_KV_HEADS, S=S, C=C, band=None, scale=scale, ctx_queries=cq))

    row = lambda a: a[None, :]

    xt = jnp.concatenate([x[0], ctx[0]], axis=0)
    saved = []
    for l in range(L):
        md = mods[l]
        gq, gk, mq, mkv = row(w["gqa_q_norm"][l]), row(w["gqa_k_norm"][l]), row(w["mla_q_norm"][l]), row(w["mla_kv_norm"][l])
        h1 = _mod_fwd(xt, md[0], md[1], S, name="mod_fwd")
        p = _mm(h1, w_in_p[l], name="in_proj")
        qkv = _prep_fwd(p, tabs, gq, gk, mq, mkv, S, name="prep_fwd")
        qm = _mm(qkv, w_uq_p[l], a_off=CB_CQ * LANE, a_k=MLA_Q_LORA, tk=LANE, name="mla_uq")
        qmb = _pe_rope(qm, tabs, S, 1.0, BF16, name="mla_q_rope")
        kvm = _mm(qkv, w_ukv_p[l], a_off=CB_CKV * LANE, a_k=MLA_KV_LORA, tk=LANE, out_dtype=BF16, name="mla_ukv")
        bias_na = _na_bias(w["na_rpb"][l], S)
        sink = jnp.broadcast_to(jnp.repeat(w["swa_sink"][l].reshape(SWA_KV_HEADS, -1), BQ, axis=1)[:, :, None],
                                (SWA_KV_HEADS, SWA_HEADS // SWA_KV_HEADS * BQ, LANE))
        cfg_na, cfg_swa, cfg_mla, cfg_gqa = attn_cfgs(l)
        rows = T if l < L - 1 else S
        oa = _attn_fwd(cfg_na, qkv, CB_NA_Q, qkv, CB_NA_K, qkv, CB_NA_V, bias=bias_na, name="na_fwd")
        ob = _attn_fwd(cfg_swa, qkv, CB_SWA_Q, qkv, CB_SWA_K, qkv, CB_SWA_V, bias=swa_mask, sink=sink, name="swa_fwd")
        oc = _attn_fwd(cfg_mla, qmb, 0, kvm, 0, kvm, MLA_HEADS, q2=qmb, q2_cb=MLA_HEADS, k2=qkv, k2_cb=CB_KPE, name="mla_fwd")
        od = _attn_fwd(cfg_gqa, qkv, CB_GQA_Q, qkv, CB_GQA_K, qkv, CB_GQA_V, name="gqa_fwd")
        mix = jnp.concatenate([oa, ob, oc, od], axis=1)
        z1 = _mm(mix, full["w_out"][l], rows=rows, name="out_proj")
        x1 = _res_fwd(xt, z1, md[2], row(w["ln1_g"][l]), row(w["ln1_b"][l]), S, name="res_fwd")
        h2 = _mod_fwd(x1, md[3], md[4], S, name="mod_fwd")
        gp = _mm(h2, full["ffn_w_gate"][l], name="ffn_in")
        up = _mm(h2, full["ffn_w_up"][l], name="ffn_in")
        act = _ffn_fwd(gp, up, conv_w[l], row(w["ffn_conv_b"][l]), S, name="ffn_mid")
        z2 = _mm(act, full["ffn_w_down"][l], name="ffn_out")
        x2 = _res_fwd(x1, z2, md[5], row(w["ln2_g"][l]), row(w["ln2_b"][l]), S, name="res_fwd")
        saved.append(dict(x=xt, h1=h1, p=p, qkv=qkv, qmb=qmb, kvm=kvm, bias_na=bias_na, sink=sink, mix=mix, z1=z1, x1=x1,
                          h2=h2, gp=gp, up=up, act=act, z2=z2, cfgs=(cfg_na, cfg_swa, cfg_mla, cfg_gqa)))
        xt = x2

    dx, loss_part = _loss_fwd_bwd(xt, loss_target[0], S, name="loss")
    loss = lax.psum(loss_part[0, 0], ("x", "y", "c"))

    gbuf = {n: None for n in _BIG}
    small = {n: [None] * L for n in ("na_rpb", "swa_sink", "mla_q_norm", "mla_kv_norm", "gqa_q_norm", "gqa_k_norm",
                                     "ln1_g", "ln1_b", "ffn_conv_w", "ffn_conv_b", "ln2_g", "ln2_b")}
    dmod = [None] * L
    for l in reversed(range(L)):
        sv, md = saved[l], mods[l]
        gq, gk, mq, mkv = row(w["gqa_q_norm"][l]), row(w["gqa_k_norm"][l]), row(w["mla_q_norm"][l]), row(w["mla_kv_norm"][l])
        cb_row = row(w["ffn_conv_b"][l])
        dx1, dz2, acc_r2 = _res_bwd(sv["x1"], sv["z2"], md[5], row(w["ln2_g"][l]), dx, S, name="res_bwd")
        dact = _mm(dz2, full["ffn_w_down"][l], mode="nt", name="ffn_out_dx")
        gbuf["ffn_w_down"] = _mm(sv["act"], dz2, mode="tn", stack=(L, l, gbuf["ffn_w_down"]), name="ffn_out_dw")
        da, du, acc_f = _ffn_bwd_act(sv["gp"], sv["up"], dact, conv_w[l], cb_row, S, name="ffn_mid_bwd")
        dg = _ffn_bwd_conv(da, conv_w[l], S, name="ffn_conv_bwd")
        dh2 = _mm(dg, full["ffn_w_gate"][l], mode="nt", name="ffn_in_dx")
        dh2 = _mm(du, full["ffn_w_up"][l], mode="nt", add=dh2, name="ffn_in_dx_add")
        gbuf["ffn_w_gate"] = _mm(sv["h2"], dg, mode="tn", stack=(L, l, gbuf["ffn_w_gate"]), split4=True, name="ffn_in_dw")
        gbuf["ffn_w_up"] = _mm(sv["h2"], du, mode="tn", stack=(L, l, gbuf["ffn_w_up"]), split4=True, name="ffn_in_dw")
        dx1, acc_m2 = _mod_bwd(sv["x1"], dh2, md[4], dx1, S, name="mod_bwd")
        dxa, dz1, acc_r1 = _res_bwd(sv["x"], sv["z1"], md[2], row(w["ln1_g"][l]), dx1, S, name="res_bwd")
        dmix = _mm(dz1, full["w_out"][l], mode="nt", out_dtype=BF16, name="out_proj_dx")
        gbuf["w_out"] = _mm(sv["mix"], dz1, mode="tn", rows=dz1.shape[0], stack=(L, l, gbuf["w_out"]), name="out_proj_dw")

        qkv, qmb, kvm = sv["qkv"], sv["qmb"], sv["kvm"]
        cfg_na, cfg_swa, cfg_mla, cfg_gqa = sv["cfgs"]
        dq_a, dk_a, dv_a, dbias = _attn_bwd(cfg_na, qkv, CB_NA_Q, qkv, CB_NA_K, qkv, CB_NA_V, dmix, 0, bias=sv["bias_na"],
                                            want_dbias=True, name="na_bwd")
        dq_b, dk_b, dv_b, dsink = _attn_bwd(cfg_swa, qkv, CB_SWA_Q, qkv, CB_SWA_K, qkv, CB_SWA_V, dmix, NA_HEADS,
                                            bias=swa_mask, sink=sv["sink"], name="swa_bwd")
        dq_c, dk_c, dv_c, dq2_c, dk2_c = _attn_bwd(cfg_mla, qmb, 0, kvm, 0, kvm, MLA_HEADS, dmix, NA_HEADS + SWA_HEADS,
                                                   q2=qmb, q2_cb=MLA_HEADS, k2=qkv, k2_cb=CB_KPE, name="mla_bwd")
        dq_d, dk_d, dv_d = _attn_bwd(cfg_gqa, qkv, CB_GQA_Q, qkv, CB_GQA_K, qkv, CB_GQA_V, dmix,
                                     NA_HEADS + SWA_HEADS + MLA_HEADS, name="gqa_bwd")
        dqm = _pe_rope(jnp.concatenate([dq_c, dq2_c], axis=1), tabs, S, -1.0, BF16, name="mla_q_rope_bwd")
        dkvm = jnp.concatenate([dk_c, dv_c], axis=1).astype(BF16)
        dcq = _mm(dqm, w_uq_p[l], mode="nt", name="mla_uq_dx")
        dckv = _mm(dkvm, w_ukv_p[l], mode="nt", name="mla_ukv_dx")
        cqn = qkv[:, CB_CQ * LANE:CB_CKV * LANE]
        ckvn = qkv[:, CB_CKV * LANE:(CB_CKV + 1) * LANE]
        d_uq = _unperm_w_uq(_mm(cqn, dqm, mode="tn", name="mla_uq_dw"))
        d_ukv = _unperm_w_ukv(_mm(ckvn, dkvm, mode="tn", name="mla_ukv_dw"))
        grads = {}
        for h in range(NA_HEADS):
            grads[CB_NA_Q + h], grads[CB_NA_K + h], grads[CB_NA_V + h] = (dq_a, h), (dk_a, h), (dv_a, h)
        for h in range(SWA_HEADS):
            grads[CB_SWA_Q + h] = (dq_b, h)
        for h in range(SWA_KV_HEADS):
            grads[CB_SWA_K + h], grads[CB_SWA_V + h] = (dk_b, h), (dv_b, h)
        for h in range(GQA_HEADS):
            grads[CB_GQA_Q + h] = (dq_d, h)
        for h in range(GQA_KV_HEADS):
            grads[CB_GQA_K + h], grads[CB_GQA_V + h] = (dk_d, h), (dv_d, h)
        grads[CB_KPE], grads[CB_CQ], grads[CB_CKV] = (dk2_c, 0), (dcq, 0), (dckv, 0)
        dp, acc_p = _prep_bwd(sv["p"], grads, tabs, gq, gk, mq, mkv, S, name="prep_bwd")
        dh1 = _mm(dp, w_in_p[l], mode="nt", name="in_proj_dx")
        d_in = _unperm_w_in(_mm(sv["h1"], dp, mode="tn", name="in_proj_dw"))
        dx, acc_m1 = _mod_bwd(sv["x"], dh1, md[1], dxa, S, name="mod_bwd")

        to4 = lambda g: g.reshape(g.shape[0], 4, g.shape[1] // 4).transpose(1, 0, 2)
        for n, g in (("w_in", d_in), ("mla_w_uq", d_uq), ("mla_w_ukv", d_ukv)):
            g4 = to4(g)[None]
            gbuf[n] = g4 if gbuf[n] is None else jnp.concatenate([g4, gbuf[n]], axis=0)
        dmod[l] = jnp.stack([acc_m1[0:2], acc_m1[2:4], acc_r1[0:2], acc_m2[0:2], acc_m2[2:4], acc_r2[0:2]])
        rpb_vjp = jax.vjp(lambda r: _na_bias(r, S), w["na_rpb"][l])[1]
        small["na_rpb"][l] = rpb_vjp(dbias)[0]
        small["swa_sink"][l] = dsink[:, :, 0].reshape(SWA_KV_HEADS, -1, BQ).sum(axis=-1).reshape(-1)
        small["gqa_q_norm"][l], small["gqa_k_norm"][l] = acc_p[0, :LANE], acc_p[1, :LANE]
        small["mla_q_norm"][l], small["mla_kv_norm"][l] = acc_p[2], acc_p[3, :LANE]
        small["ln1_g"][l], small["ln1_b"][l] = acc_r1[2], acc_r1[3]
        small["ln2_g"][l], small["ln2_b"][l] = acc_r2[2], acc_r2[3]
        small["ffn_conv_w"][l], small["ffn_conv_b"][l] = acc_f[0:3], acc_f[3]
    grad_x = dx[:S][None]

    dmod_x = jnp.stack([dmod[l][:, 0].reshape(-1) for l in range(L)])
    dmod_c = jnp.stack([dmod[l][:, 1].reshape(-1) for l in range(L)])
    small_names = tuple(small)
    bucket = [dmod_x, dmod_c] + [jnp.stack(small[n]) for n in small_names]
    (b8,) = _allgather8([_pack(bucket)], name="gather_small")
    tot = _unpack(_sum_lead(b8, F32, name="sum_small"), bucket)
    dmod_x_all = b8.reshape(8, -1)[:, :dmod_x.size].reshape(8, L, 6 * D)
    dmod_c_tot = tot[1]
    g_small = dict(zip(small_names, tot[2:]))
    g_small["b_ada"] = tot[0] + dmod_c_tot
    g_small["ffn_conv_w"] = lax.dynamic_slice_in_dim(g_small["ffn_conv_w"], chip * (F // 4), F // 4, axis=2)

    dmod16 = jnp.concatenate([dmod_x_all, jnp.broadcast_to(dmod_c_tot[None], (8, L, 6 * D))], axis=0) * row_keep[:, :, None]
    dmod16 = lax.dynamic_slice_in_dim(dmod16, chip * n_ada, n_ada, axis=2)
    g_ada, dsc = None, None
    for l in range(L):
        g_ada = _mm(sc, dmod16[:, l], mode="tn", exact=True, stack=(L, l, g_ada), name="ada_dw")
        dsc = _mm(dmod16[:, l], w["w_ada"][l], mode="nt", add=dsc, name="ada_dx" if dsc is None else "ada_dx_add")
    (dsc8,) = _allgather8([dsc[8:16]], name="gather_dsc")
    dsc4 = dsc8[::2, 0]
    g_small["c_ctx"] = (((dsc4[0] + dsc4[1]) + dsc4[2]) + dsc4[3]) * _silu_grad(w["c_ctx"])

    def halves_view(g):
        return g.reshape(g.shape[0], 4, 2, g.shape[2] // 2, g.shape[3])

    views = []
    for n in _BIG:
        g = gbuf[n]
        if n in _ROW_SHARDED:
            g = g.reshape(L, 4, g.shape[1] // 4, g.shape[2])
        views.append(halves_view(g))
    from_sibling = _send_other_half(views, name="rs_core_exchange")
    parts = [_add_half(g, r, ac, name="rs_core_add") for g, r in zip(views, from_sibling)]
    scattered = _chip_scatter(parts, name="rs_chip_scatter")
    halves = [_sum4_layers(s, ac, name="rs_chip_sum") for s in scattered]
    joined = _join_halves(halves, name="rs_join")
    g_big = {n: j.reshape(L, 2 * j.shape[2], j.shape[3]) for n, j in zip(_BIG, joined)}
    g_big["w_ada"] = g_ada

    grad, delta, new_m, new_v = {}, {}, {}, {}
    for n in _BIG + ("w_ada",):
        shp = w[n].shape
        flat = lambda a: a.reshape(shp[0] * shp[1], shp[2])
        d_, m_, v_ = _adam(flat(w[n]), flat(g_big[n]), flat(m_in[n]), flat(v_in[n]), name="adam")
        grad[n], delta[n], new_m[n], new_v[n] = g_big[n], d_.reshape(shp), m_.reshape(shp), v_.reshape(shp)
    like = [w[n] for n in _SMALL]
    packed = [_pack([src[n].reshape(w[n].shape) for n in _SMALL]) for src in (w, g_small, m_in, v_in)]
    d_s, m_s, v_s = _adam(*packed, name="adam_small")
    for n, g_, d_, m_, v_ in zip(_SMALL, _unpack(packed[1], like), _unpack(d_s, like), _unpack(m_s, like), _unpack(v_s, like)):
        grad[n], delta[n], new_m[n], new_v[n] = g_, d_, m_, v_

    return (loss, grad_x, *[grad[n] for n in _WEIGHTS], *[delta[n] for n in _WEIGHTS],
            *[new_m[n] for n in _WEIGHTS], *[new_v[n] for n in _WEIGHTS])


def kernel(x, c, ctx, c_ctx, w_ada, b_ada, w_in, na_rpb, swa_sink, mla_q_norm, mla_kv_norm, mla_w_uq, mla_w_ukv, gqa_q_norm, gqa_k_norm, w_out, ln1_g, ln1_b, ffn_w_gate, ffn_w_up, ffn_conv_w, ffn_conv_b, ffn_w_down, ln2_g, ln2_b, loss_target, m_c_ctx, m_w_ada, m_b_ada, m_w_in, m_na_rpb, m_swa_sink, m_mla_q_norm, m_mla_kv_norm, m_mla_w_uq, m_mla_w_ukv, m_gqa_q_norm, m_gqa_k_norm, m_w_out, m_ln1_g, m_ln1_b, m_ffn_w_gate, m_ffn_w_up, m_ffn_conv_w, m_ffn_conv_b, m_ffn_w_down, m_ln2_g, m_ln2_b, v_c_ctx, v_w_ada, v_b_ada, v_w_in, v_na_rpb, v_swa_sink, v_mla_q_norm, v_mla_kv_norm, v_mla_w_uq, v_mla_w_ukv, v_gqa_q_norm, v_gqa_k_norm, v_w_out, v_ln1_g, v_ln1_b, v_ffn_w_gate, v_ffn_w_up, v_ffn_conv_w, v_ffn_conv_b, v_ffn_w_down, v_ln2_g, v_ln2_b):
    args = locals()
    w = {n: args[n] for n in _WEIGHTS}
    m_in = {n: args["m_" + n] for n in _WEIGHTS}
    v_in = {n: args["v_" + n] for n in _WEIGHTS}
    return _train_step(x, c, ctx, loss_target, w, m_in, v_in)
```

```python
import functools
import math

import numpy as np
import jax
import jax.numpy as jnp
from jax import lax
from jax.experimental import pallas as pl
from jax.experimental.pallas import tpu as pltpu

F32 = jnp.float32
BF16 = jnp.bfloat16
MESH = pl.DeviceIdType.MESH

GRID_W = 64
HEAD_DIM = 128
NA_HEADS, NA_WIN_R, NA_WIN_C = 4, 8, 16
SWA_HEADS, SWA_KV_HEADS, SWA_WINDOW = 4, 2, 128
MLA_HEADS, MLA_Q_LORA, MLA_KV_LORA, MLA_NOPE, MLA_ROPE, MLA_V = 4, 384, 128, 128, 64, 128
GQA_HEADS, GQA_KV_HEADS = 4, 2
ROPE_THETA = 10000.0
EPS = 1e-6
NEG = -1e30
DEPTH = 2
DEEPNORM_ALPHA = (2 * DEPTH) ** 0.25
ADAM_LR, ADAM_B1, ADAM_B2, ADAM_EPS, ADAM_WD, ADAM_STEP = 0.001, 0.9, 0.999, 1e-08, 0.01, 10

LANE = 128
V7X_VMEM_BYTES = 64 * 1024 * 1024
VMEM_LIMIT = 56 * 1024 * 1024
MM_VMEM_BUDGET = 40 * 1024 * 1024
EW_VMEM_BUDGET = 28 * 1024 * 1024
BQ = 128

CB_NA_Q, CB_NA_K, CB_NA_V = 0, 4, 8
CB_SWA_Q, CB_SWA_K, CB_SWA_V = 12, 16, 18
CB_CQ, CB_CKV = 20, 23
CB_GQA_Q, CB_GQA_K, CB_GQA_V = 24, 28, 30
CB_KPE = 32
PCOLS = 33 * LANE
IN_COLS = 4160


def _cparams(sem=None, **kw):
    return pltpu.CompilerParams(dimension_semantics=sem, vmem_limit_bytes=VMEM_LIMIT, **kw)


def _pick(n, target, mult=LANE):
    best = None
    for d in range(mult, min(n, target) + 1, mult):
        if n % d == 0:
            best = d
    return n if best is None else best


def _mm(a, b, *, mode="nn", out_dtype=F32, a_off=0, a_k=None, tm=1408, tn=1408, tk=2816, exact=False, add=None,
        stack=None, split4=False, rows=None, name):
    if mode == "tn":
        K, M = a.shape
        K2, N = b.shape
    elif mode == "nn":
        M, K = a.shape
        K2, N = b.shape
    else:
        M, K = a.shape
        N, K2 = b.shape
    if a_k is not None:
        K = a_k
    if rows is not None:
        if mode == "tn":
            assert rows <= min(K, K2)
            K = K2 = rows
        else:
            assert rows <= M
            M = rows
    assert K == K2, (a.shape, b.shape, mode)
    m_mult = LANE if mode == "tn" else 16
    n_cols = N // 4 if split4 else N
    bm, bn, bk = _pick(M, tm, m_mult), _pick(n_cols, tn), _pick(K, tk)
    sa, sb, so = a.dtype.itemsize, b.dtype.itemsize, jnp.dtype(out_dtype).itemsize

    def vmem_estimate():
        acc = bm * bn * 4 if K // bk > 1 else 0
        return 2 * (bm * bk * sa + bk * bn * sb) + acc + 2 * bm * bn * so + (2 * bm * bn * 4 if add is not None else 0)

    while vmem_estimate() > MM_VMEM_BUDGET:
        if bm >= bn and _pick(M, bm - 1, m_mult) < bm:
            bm = _pick(M, bm - 1, m_mult)
        elif _pick(n_cols, bn - 1) < bn:
            bn = _pick(n_cols, bn - 1)
        else:
            assert _pick(K, bk - 1) < bk, "no tiling fits VMEM"
            bk = _pick(K, bk - 1)
    assert a_off % bk == 0
    koff = a_off // bk
    nk = K // bk
    if mode == "tn":
        a_spec = pl.BlockSpec((bk, bm), lambda i, j, k: (k, i))
        b_spec = pl.BlockSpec((bk, bn), lambda i, j, k: (k, j))
        dims = (((0,), (0,)), ((), ()))
    elif mode == "nn":
        a_spec = pl.BlockSpec((bm, bk), lambda i, j, k: (i, k + koff))
        b_spec = pl.BlockSpec((bk, bn), lambda i, j, k: (k, j))
        dims = (((1,), (0,)), ((), ()))
    else:
        a_spec = pl.BlockSpec((bm, bk), lambda i, j, k: (i, k + koff))
        b_spec = pl.BlockSpec((bn, bk), lambda i, j, k: (j, k))
        dims = (((1,), (1,)), ((), ()))

    operands = [a, b]
    in_specs = [a_spec, b_spec]
    if add is not None:
        operands.append(add)
        in_specs.append(pl.BlockSpec((bm, bn), lambda i, j, k: (i, j)))
    aliases = {}
    if stack is None:
        out_spec = pl.BlockSpec((bm, bn), lambda i, j, k: (i, j))
        out_shape = jax.ShapeDtypeStruct((M, N), out_dtype)
    else:
        n_layers, layer, buf = stack
        if split4:
            nb = N // 4 // bn
            assert N % (4 * bn) == 0
            out_spec = pl.BlockSpec((None, None, bm, bn), lambda i, j, k: (layer, j // nb, i, j % nb))
            out_shape = jax.ShapeDtypeStruct((n_layers, 4, M, N // 4), out_dtype)
        else:
            out_spec = pl.BlockSpec((None, bm, bn), lambda i, j, k: (layer, i, j))
            out_shape = jax.ShapeDtypeStruct((n_layers, M, N), out_dtype)
        if buf is not None:
            aliases = {len(operands): 0}
            operands.append(buf)
            in_specs.append(pl.BlockSpec(memory_space=pl.ANY))
    has_add, has_buf = add is not None, bool(aliases)

    def body(*refs):
        a_ref, b_ref = refs[:2]
        add_ref = refs[2] if has_add else None
        o_ref = refs[2 + has_add + has_buf]
        if exact:
            prod = lax.dot_general(a_ref[...].astype(F32), b_ref[...].astype(F32), dims,
                                   precision=lax.Precision.HIGHEST, preferred_element_type=F32)
        else:
            prod = lax.dot_general(a_ref[...].astype(BF16), b_ref[...].astype(BF16), dims, preferred_element_type=F32)

        def finish(res):
            if has_add:
                res = res + add_ref[...].astype(F32)
            o_ref[...] = res.astype(o_ref.dtype)

        if nk == 1:
            finish(prod)
            return
        acc_ref = refs[-1]
        k = pl.program_id(2)

        @pl.when(k == 0)
        def _():
            acc_ref[...] = prod

        @pl.when((k > 0) & (k < nk - 1))
        def _():
            acc_ref[...] += prod

        @pl.when(k == nk - 1)
        def _():
            finish(acc_ref[...] + prod)

    return pl.pallas_call(
        body, name=name,
        grid=(M // bm, N // bn, nk),
        in_specs=in_specs, out_specs=out_spec, out_shape=out_shape,
        scratch_shapes=[pltpu.VMEM((bm, bn), F32)] if nk > 1 else [],
        input_output_aliases=aliases,
        compiler_params=_cparams(("parallel", "parallel", "arbitrary")),
    )(*operands)


def _row_block(T, S):
    return _pick(math.gcd(T, S), 256, 16)


def _ln_stats(x):
    mu = jnp.mean(x, axis=-1, keepdims=True)
    xc = x - mu
    var = jnp.mean(xc * xc, axis=-1, keepdims=True)
    rstd = lax.rsqrt(var + EPS)
    return xc * rstd, rstd


def _ln_bwd(dxhat, xhat, rstd):
    m1 = jnp.mean(dxhat, axis=-1, keepdims=True)
    m2 = jnp.mean(dxhat * xhat, axis=-1, keepdims=True)
    return rstd * (dxhat - m1 - xhat * m2)


def _sel(ref, is_ctx):
    return jnp.where(is_ctx, ref[1:2, :], ref[0:1, :])


def _mod_fwd(x, shift, scale, S, *, rows=None, name):
    T, D = (x.shape[0] if rows is None else rows), x.shape[1]
    bt = _row_block(T, S)

    def body(x_ref, sh_ref, sc_ref, o_ref):
        is_ctx = pl.program_id(0) * bt >= S
        xhat, _ = _ln_stats(x_ref[...])
        o_ref[...] = (xhat * (1.0 + _sel(sc_ref, is_ctx)) + _sel(sh_ref, is_ctx)).astype(o_ref.dtype)

    return pl.pallas_call(
        body, name=name, grid=(T // bt,),
        in_specs=[pl.BlockSpec((bt, D), lambda i: (i, 0)), pl.BlockSpec((2, D), lambda i: (0, 0)),
                  pl.BlockSpec((2, D), lambda i: (0, 0))],
        out_specs=pl.BlockSpec((bt, D), lambda i: (i, 0)),
        out_shape=jax.ShapeDtypeStruct((T, D), BF16),
        compiler_params=_cparams(("parallel",)),
    )(x, shift, scale)


def _acc_groups(acc_ref, row, val, is_ctx):
    f = jnp.where(is_ctx, 1.0, 0.0).astype(F32)
    acc_ref[row:row + 1, :] += val * (1.0 - f)
    acc_ref[row + 1:row + 2, :] += val * f


def _mod_bwd(x, dh, scale, dx_in, S, *, name):
    T, D = dh.shape
    bt = _row_block(T, S)
    in_blocks = dx_in.shape[0] // bt

    def body(x_ref, dh_ref, sc_ref, dxi_ref, dx_ref, acc_ref):
        i = pl.program_id(0)
        is_ctx = i * bt >= S

        @pl.when(i == 0)
        def _():
            acc_ref[...] = jnp.zeros_like(acc_ref)

        xhat, rstd = _ln_stats(x_ref[...])
        dh = dh_ref[...].astype(F32)
        dxhat = dh * (1.0 + _sel(sc_ref, is_ctx))
        dxi = dxi_ref[...] if in_blocks * bt == T else jnp.where(i < in_blocks, dxi_ref[...], 0.0)
        dx_ref[...] = dxi + _ln_bwd(dxhat, xhat, rstd)
        _acc_groups(acc_ref, 0, jnp.sum(dh, axis=0, keepdims=True), is_ctx)
        _acc_groups(acc_ref, 2, jnp.sum(dh * xhat, axis=0, keepdims=True), is_ctx)

    return pl.pallas_call(
        body, name=name, grid=(T // bt,),
        in_specs=[pl.BlockSpec((bt, D), lambda i: (i, 0)), pl.BlockSpec((bt, D), lambda i: (i, 0)),
                  pl.BlockSpec((2, D), lambda i: (0, 0)),
                  pl.BlockSpec((bt, D), lambda i: (jnp.minimum(i, in_blocks - 1), 0))],
        out_specs=[pl.BlockSpec((bt, D), lambda i: (i, 0)), pl.BlockSpec((8, D), lambda i: (0, 0))],
        out_shape=[jax.ShapeDtypeStruct((T, D), F32), jax.ShapeDtypeStruct((8, D), F32)],
        compiler_params=_cparams(("arbitrary",)),
    )(x, dh, scale, dx_in)


def _res_fwd(x, z, gate, lg, lb, S, *, name):
    T, D = z.shape
    bt = _row_block(T, S)

    def body(x_ref, z_ref, g_ref, lg_ref, lb_ref, o_ref):
        is_ctx = pl.program_id(0) * bt >= S
        u = DEEPNORM_ALPHA * x_ref[...] + _sel(g_ref, is_ctx) * z_ref[...]
        uhat, _ = _ln_stats(u)
        o_ref[...] = uhat * lg_ref[...] + lb_ref[...]

    row = pl.BlockSpec((bt, D), lambda i: (i, 0))
    return pl.pallas_call(
        body, name=name, grid=(T // bt,),
        in_specs=[row, row, pl.BlockSpec((2, D), lambda i: (0, 0)), pl.BlockSpec((1, D), lambda i: (0, 0)),
                  pl.BlockSpec((1, D), lambda i: (0, 0))],
        out_specs=row,
        out_shape=jax.ShapeDtypeStruct((T, D), F32),
        compiler_params=_cparams(("parallel",)),
    )(x, z, gate, lg, lb)


def _res_bwd(x, z, gate, lg, dy, S, *, name):
    T, D = z.shape
    bt = _row_block(T, S)

    def body(x_ref, z_ref, g_ref, lg_ref, dy_ref, dx_ref, dz_ref, acc_ref):
        i = pl.program_id(0)
        is_ctx = i * bt >= S

        @pl.when(i == 0)
        def _():
            acc_ref[...] = jnp.zeros_like(acc_ref)

        gate_v = _sel(g_ref, is_ctx)
        zv = z_ref[...]
        u = DEEPNORM_ALPHA * x_ref[...] + gate_v * zv
        uhat, rstd = _ln_stats(u)
        dyv = dy_ref[...]
        du = _ln_bwd(dyv * lg_ref[...], uhat, rstd)
        dx_ref[...] = DEEPNORM_ALPHA * du
        dz_ref[...] = (gate_v * du).astype(dz_ref.dtype)
        _acc_groups(acc_ref, 0, jnp.sum(du * zv, axis=0, keepdims=True), is_ctx)
        acc_ref[2:3, :] += jnp.sum(dyv * uhat, axis=0, keepdims=True)
        acc_ref[3:4, :] += jnp.sum(dyv, axis=0, keepdims=True)

    row = pl.BlockSpec((bt, D), lambda i: (i, 0))
    return pl.pallas_call(
        body, name=name, grid=(T // bt,),
        in_specs=[row, row, pl.BlockSpec((2, D), lambda i: (0, 0)), pl.BlockSpec((1, D), lambda i: (0, 0)), row],
        out_specs=[row, row, pl.BlockSpec((8, D), lambda i: (0, 0))],
        out_shape=[jax.ShapeDtypeStruct((T, D), F32), jax.ShapeDtypeStruct((T, D), BF16),
                   jax.ShapeDtypeStruct((8, D), F32)],
        compiler_params=_cparams(("arbitrary",)),
    )(x, z, gate, lg, dy)


def _loss_fwd_bwd(y, target, S, *, name):
    T, D = y.shape
    bt = _row_block(T, S)
    n_lat = S // bt

    def body(y_ref, t_ref, dy_ref, l_ref):
        i = pl.program_id(0)

        @pl.when(i == 0)
        def _():
            l_ref[...] = jnp.zeros_like(l_ref)

        keep = jnp.where(i * bt >= S, 0.0, 1.0).astype(F32)
        err = (y_ref[...] - t_ref[...]) * keep
        dy_ref[...] = err * (1.0 / D)
        l_ref[...] += jnp.sum(err * err) * (0.5 / D)

    return pl.pallas_call(
        body, name=name, grid=(T // bt,),
        in_specs=[pl.BlockSpec((bt, D), lambda i: (i, 0)),
                  pl.BlockSpec((bt, D), lambda i: (jnp.minimum(i, n_lat - 1), 0))],
        out_specs=[pl.BlockSpec((bt, D), lambda i: (i, 0)), pl.BlockSpec((8, LANE), lambda i: (0, 0))],
        out_shape=[jax.ShapeDtypeStruct((T, D), F32), jax.ShapeDtypeStruct((8, LANE), F32)],
        compiler_params=_cparams(("arbitrary",)),
    )(y, target)


def _rope_tables(S, C, dim):
    half = dim // 4
    t = jnp.arange(S)
    row = (t // GRID_W).astype(F32)
    col = (t % GRID_W).astype(F32)
    inv = ROPE_THETA ** (-jnp.arange(half, dtype=F32) / half)
    ar, ac = row[:, None] * inv[None, :], col[:, None] * inv[None, :]
    cos = jnp.concatenate([jnp.cos(ar), jnp.cos(ar), jnp.cos(ac), jnp.cos(ac)], axis=1)
    ss = jnp.concatenate([-jnp.sin(ar), jnp.sin(ar), -jnp.sin(ac), jnp.sin(ac)], axis=1)
    cos = jnp.pad(cos, ((0, C), (0, LANE - dim)), constant_values=1.0)
    ss = jnp.pad(ss, ((0, C), (0, LANE - dim)))
    return cos, ss


def _rope(x, cos, ss, half):
    lane = lax.broadcasted_iota(jnp.int32, x.shape, 1)
    first = (lane % (2 * half)) < half
    partner = jnp.where(first, pltpu.roll(x, LANE - half, 1), pltpu.roll(x, half, 1))
    return x * cos + partner * ss


def _rms(x):
    r = lax.rsqrt(jnp.mean(x * x, axis=-1, keepdims=True) + EPS)
    return x * r, r


_CAST_BLOCKS = tuple(range(0, 12)) + (18, 19, 30, 31)
_ROPE_BLOCKS = tuple(range(12, 18))
_GQA_Q_BLOCKS = tuple(range(24, 28))
_GQA_K_BLOCKS = (28, 29)


def _prep_fwd(p, tabs, gq, gk, mq, mkv, S, *, name):
    T = p.shape[0]
    bt = _row_block(T, S)
    cA, sA, cP, sP = tabs

    def body(p_ref, cA_ref, sA_ref, cP_ref, sP_ref, gq_ref, gk_ref, mq_ref, mkv_ref, o_ref):
        def blk(b):
            return p_ref[:, b * LANE:(b + 1) * LANE]

        def put(b, val):
            o_ref[:, b * LANE:(b + 1) * LANE] = val.astype(o_ref.dtype)

        cA_v, sA_v = cA_ref[...], sA_ref[...]
        for b in _CAST_BLOCKS:
            put(b, blk(b))
        for b in _ROPE_BLOCKS:
            put(b, _rope(blk(b), cA_v, sA_v, 32))
        for b in _GQA_Q_BLOCKS:
            put(b, _rope(_rms(blk(b))[0] * gq_ref[...], cA_v, sA_v, 32))
        for b in _GQA_K_BLOCKS:
            put(b, _rope(_rms(blk(b))[0] * gk_ref[...], cA_v, sA_v, 32))
        put(CB_KPE, _rope(blk(CB_KPE), cP_ref[...], sP_ref[...], 16))
        cq = p_ref[:, CB_CQ * LANE:CB_CKV * LANE]
        o_ref[:, CB_CQ * LANE:CB_CKV * LANE] = (_rms(cq)[0] * mq_ref[...]).astype(o_ref.dtype)
        put(CB_CKV, _rms(blk(CB_CKV))[0] * mkv_ref[...])

    row128 = pl.BlockSpec((bt, LANE), lambda i: (i, 0))
    vec = lambda n: pl.BlockSpec((1, n), lambda i: (0, 0))
    return pl.pallas_call(
        body, name=name, grid=(T // bt,),
        in_specs=[pl.BlockSpec((bt, PCOLS), lambda i: (i, 0)), row128, row128, row128, row128,
                  vec(LANE), vec(LANE), vec(MLA_Q_LORA), vec(LANE)],
        out_specs=pl.BlockSpec((bt, PCOLS), lambda i: (i, 0)),
        out_shape=jax.ShapeDtypeStruct((T, PCOLS), BF16),
        compiler_params=_cparams(("parallel",)),
    )(p, cA, sA, cP, sP, gq, gk, mq, mkv)


def _prep_bwd(p, grads, tabs, gq, gk, mq, mkv, S, *, name):
    T = p.shape[0]
    bt = _row_block(T, S)
    cA, sA, cP, sP = tabs
    arrays = []
    where = {}
    for key, (arr, cb) in grads.items():
        idx = next((n for n, a in enumerate(arrays) if a is arr), None)
        if idx is None:
            arrays.append(arr)
            idx = len(arrays) - 1
        where[key] = (idx, cb)
    ng = len(arrays)

    def body(*refs):
        p_ref, cA_ref, sA_ref, cP_ref, sP_ref, gq_ref, gk_ref, mq_ref, mkv_ref = refs[:9]
        g_refs = refs[9:9 + ng]
        o_ref, acc_ref = refs[9 + ng:]
        i = pl.program_id(0)

        @pl.when(i == 0)
        def _():
            acc_ref[...] = jnp.zeros_like(acc_ref)

        def blk(b):
            return p_ref[:, b * LANE:(b + 1) * LANE]

        def grad(b, width=LANE):
            idx, cb = where[b]
            return g_refs[idx][:, cb * LANE:cb * LANE + width].astype(F32)

        def put(b, val):
            o_ref[:, b * LANE:(b + 1) * LANE] = val.astype(o_ref.dtype)

        def rms_bwd(x, dy, g, row, width):
            n, r = _rms(x)
            acc_ref[row:row + 1, 0:width] += jnp.sum(dy * n, axis=0, keepdims=True)
            dn = dy * g
            return r * (dn - n * jnp.mean(dn * n, axis=-1, keepdims=True))

        cA_v, sA_v = cA_ref[...], sA_ref[...]
        for b in _CAST_BLOCKS:
            put(b, grad(b))
        for b in _ROPE_BLOCKS:
            put(b, _rope(grad(b), cA_v, -sA_v, 32))
        for b in _GQA_Q_BLOCKS:
            put(b, rms_bwd(blk(b), _rope(grad(b), cA_v, -sA_v, 32), gq_ref[...], 0, LANE))
        for b in _GQA_K_BLOCKS:
            put(b, rms_bwd(blk(b), _rope(grad(b), cA_v, -sA_v, 32), gk_ref[...], 1, LANE))
        put(CB_KPE, _rope(grad(CB_KPE), cP_ref[...], -sP_ref[...], 16))
        dcq = rms_bwd(p_ref[:, CB_CQ * LANE:CB_CKV * LANE], grad(CB_CQ, MLA_Q_LORA), mq_ref[...], 2, MLA_Q_LORA)
        o_ref[:, CB_CQ * LANE:CB_CKV * LANE] = dcq.astype(o_ref.dtype)
        put(CB_CKV, rms_bwd(blk(CB_CKV), grad(CB_CKV), mkv_ref[...], 3, LANE))

    row128 = pl.BlockSpec((bt, LANE), lambda i: (i, 0))
    vec = lambda n: pl.BlockSpec((1, n), lambda i: (0, 0))
    g_specs = [pl.BlockSpec((bt, a.shape[1]), lambda i: (i, 0)) for a in arrays]
    return pl.pallas_call(
        body, name=name, grid=(T // bt,),
        in_specs=[pl.BlockSpec((bt, PCOLS), lambda i: (i, 0)), row128, row128, row128, row128,
                  vec(LANE), vec(LANE), vec(MLA_Q_LORA), vec(LANE)] + g_specs,
        out_specs=[pl.BlockSpec((bt, PCOLS), lambda i: (i, 0)), pl.BlockSpec((8, MLA_Q_LORA), lambda i: (0, 0))],
        out_shape=[jax.ShapeDtypeStruct((T, PCOLS), BF16), jax.ShapeDtypeStruct((8, MLA_Q_LORA), F32)],
        compiler_params=_cparams(("arbitrary",)),
    )(p, cA, sA, cP, sP, gq, gk, mq, mkv, *arrays)


def _pe_rope(qm, tabs, S, sign, out_dtype, *, name):
    T, N = qm.shape
    bt = _row_block(T, S)
    cP, sP = tabs[2], tabs[3]

    def body(x_ref, c_ref, s_ref, o_ref):
        for b in range(MLA_HEADS):
            o_ref[:, b * LANE:(b + 1) * LANE] = x_ref[:, b * LANE:(b + 1) * LANE].astype(o_ref.dtype)
        for b in range(MLA_HEADS, 2 * MLA_HEADS):
            x = x_ref[:, b * LANE:(b + 1) * LANE].astype(F32)
            o_ref[:, b * LANE:(b + 1) * LANE] = _rope(x, c_ref[...], sign * s_ref[...], 16).astype(o_ref.dtype)

    row128 = pl.BlockSpec((bt, LANE), lambda i: (i, 0))
    return pl.pallas_call(
        body, name=name, grid=(T // bt,),
        in_specs=[pl.BlockSpec((bt, N), lambda i: (i, 0)), row128, row128],
        out_specs=pl.BlockSpec((bt, N), lambda i: (i, 0)),
        out_shape=jax.ShapeDtypeStruct((T, N), out_dtype),
        compiler_params=_cparams(("parallel",)),
    )(qm, cP, sP)


def _dot_nt(a, b):
    return lax.dot_general(a, b, (((1,), (1,)), ((), ())), preferred_element_type=F32)


def _dot_tn(a, b):
    return lax.dot_general(a, b, (((0,), (0,)), ((), ())), preferred_element_type=F32)


def _dot(a, b):
    return jnp.dot(a, b, preferred_element_type=F32)


def _window_fns(band, S, n_var):
    n_lat = S // BQ
    if band is None:
        return None
    reach, span = band

    def fns(j):
        start = jnp.clip(j - reach, 0, n_lat - span)
        return start, jnp.clip(j - start, 0, n_var - 1)

    return fns


class _AttnCfg:
    def __init__(self, *, Hkv, G, S, C, band, scale, n_var=0, bias_per_head=False, has_sink=False, two=False, bq=BQ,
                 ctx_queries=True):
        self.Hkv, self.G, self.S, self.C, self.band, self.scale = Hkv, G, S, C, band, scale
        self.n_var, self.bias_per_head, self.has_sink, self.two = n_var, bias_per_head, has_sink, two
        self.W = S if band is None else band[1] * BQ
        self.T = S + C
        self.bq, self.ctx_queries = bq, ctx_queries
        assert band is None or bq == BQ
        assert S % bq == 0 and C % bq == 0


def _attn_probs(cfg, j, q_ref, k_ref, q2_ref, k2_ref, bias_ref, sink_ref):
    G, S, C, W = cfg.G, cfg.S, cfg.C, cfg.W
    is_ctx = j * cfg.bq >= S
    if cfg.band is None:
        off, var = 0, 0
    else:
        start, var = _window_fns(cfg.band, S, cfg.n_var)(j)
        off = pl.multiple_of(start * BQ, BQ)
    qt = q_ref[...]
    qs = jnp.concatenate([qt[:, g * LANE:(g + 1) * LANE] for g in range(G)], axis=0) if G > 1 else qt
    kw = k_ref[pl.ds(off, W), :]
    kc = k_ref[pl.ds(S, C), :]
    s_w = _dot_nt(qs, kw)
    s_c = _dot_nt(qs, kc)
    q2s = k2w = k2c = None
    if cfg.two:
        q2s = q2_ref[...]
        k2w = k2_ref[pl.ds(off, W), :]
        k2c = k2_ref[pl.ds(S, C), :]
        s_w = s_w + _dot_nt(q2s, k2w)
        s_c = s_c + _dot_nt(q2s, k2c)
    operands = (off, var, qs, kw, kc, q2s, k2w, k2c)
    if not cfg.n_var and not cfg.has_sink:
        if cfg.ctx_queries:
            s_w = jnp.where(is_ctx, NEG, s_w)
        m = jnp.maximum(jnp.max(s_w, axis=-1, keepdims=True), jnp.max(s_c, axis=-1, keepdims=True))
        c2 = cfg.scale * math.log2(math.e)
        e_w = jnp.exp2((s_w - m) * c2)
        e_c = jnp.exp2((s_c - m) * c2)
        inv = 1.0 / (jnp.sum(e_w, axis=-1, keepdims=True) + jnp.sum(e_c, axis=-1, keepdims=True))
        return e_w * inv, e_c * inv, None, operands
    s_w = s_w * cfg.scale
    s_c = s_c * cfg.scale
    if cfg.n_var:
        b = bias_ref[0, pl.ds(var, 1)][0]
        s_w = s_w + (jnp.concatenate([b] * G, axis=0) if G > 1 else b)
    if cfg.ctx_queries:
        s_w = jnp.where(is_ctx, NEG, s_w)
    m = jnp.maximum(jnp.max(s_w, axis=-1, keepdims=True), jnp.max(s_c, axis=-1, keepdims=True))
    if cfg.has_sink:
        sink = sink_ref[0][:, 0:1]
        m = jnp.maximum(m, sink)
    e_w = jnp.exp(s_w - m)
    e_c = jnp.exp(s_c - m)
    l = jnp.sum(e_w, axis=-1, keepdims=True) + jnp.sum(e_c, axis=-1, keepdims=True)
    p_s = None
    if cfg.has_sink:
        e_s = jnp.exp(sink - m)
        l = l + e_s
    inv = 1.0 / l
    if cfg.has_sink:
        p_s = e_s * inv
    return e_w * inv, e_c * inv, p_s, operands


def _attn_specs(cfg, q_cb, k_cb, v_cb, q2_cb, k2_cb):
    G, T, bq = cfg.G, cfg.T, cfg.bq
    specs = [pl.BlockSpec((bq, G * LANE), lambda h, j: (j, q_cb // G + h)),
             pl.BlockSpec((T, LANE), lambda h, j: (0, k_cb + h)),
             pl.BlockSpec((T, LANE), lambda h, j: (0, v_cb + h))]
    if cfg.two:
        specs += [pl.BlockSpec((bq, LANE), lambda h, j: (j, q2_cb + h)),
                  pl.BlockSpec((T, LANE), lambda h, j: (0, k2_cb))]
    if cfg.n_var:
        if cfg.bias_per_head:
            specs.append(pl.BlockSpec((1, cfg.n_var, BQ, cfg.W), lambda h, j: (h, 0, 0, 0)))
        else:
            specs.append(pl.BlockSpec((1, cfg.n_var, BQ, cfg.W), lambda h, j: (0, 0, 0, 0)))
    if cfg.has_sink:
        specs.append(pl.BlockSpec((1, G * BQ, LANE), lambda h, j: (h, 0, 0)))
    return specs


def _attn_unpack(cfg, refs):
    refs = list(refs)
    q_ref, k_ref, v_ref = refs[:3]
    n = 3
    q2_ref = k2_ref = bias_ref = sink_ref = None
    if cfg.two:
        q2_ref, k2_ref = refs[n:n + 2]
        n += 2
    if cfg.n_var:
        bias_ref = refs[n]
        n += 1
    if cfg.has_sink:
        sink_ref = refs[n]
        n += 1
    return (q_ref, k_ref, v_ref, q2_ref, k2_ref, bias_ref, sink_ref), refs[n:]


def _attn_fwd(cfg, q, q_cb, k, k_cb, v, v_cb, *, q2=None, q2_cb=0, k2=None, k2_cb=0, bias=None, sink=None, name):
    G, T, S, C, W = cfg.G, cfg.T, cfg.S, cfg.C, cfg.W
    assert q_cb % G == 0
    operands = [q, k, v] + ([q2, k2] if cfg.two else []) + ([bias] if cfg.n_var else []) + ([sink] if cfg.has_sink else [])

    bq = cfg.bq

    def body(*refs):
        (q_ref, k_ref, v_ref, q2_ref, k2_ref, bias_ref, sink_ref), (o_ref,) = _attn_unpack(cfg, refs)
        j = pl.program_id(1)

        def block():
            p_w, p_c, _, (off, _, _, _, _, _, _, _) = _attn_probs(cfg, j, q_ref, k_ref, q2_ref, k2_ref, bias_ref, sink_ref)
            o = _dot(p_w.astype(BF16), v_ref[pl.ds(off, W), :]) + _dot(p_c.astype(BF16), v_ref[pl.ds(S, C), :])
            for g in range(G):
                o_ref[:, g * LANE:(g + 1) * LANE] = o[g * bq:(g + 1) * bq].astype(o_ref.dtype)

        if cfg.ctx_queries:
            block()
        else:
            pl.when(j * bq < S)(block)

            @pl.when(j * bq >= S)
            def _():
                o_ref[...] = jnp.zeros_like(o_ref)

    return pl.pallas_call(
        body, name=name, grid=(cfg.Hkv, T // bq),
        in_specs=_attn_specs(cfg, q_cb, k_cb, v_cb, q2_cb, k2_cb),
        out_specs=pl.BlockSpec((bq, G * LANE), lambda h, j: (j, h)),
        out_shape=jax.ShapeDtypeStruct((T, cfg.Hkv * G * LANE), BF16),
        compiler_params=_cparams(("parallel", "parallel")),
    )(*operands)


def _attn_bwd(cfg, q, q_cb, k, k_cb, v, v_cb, do, do_cb, *, q2=None, q2_cb=0, k2=None, k2_cb=0, bias=None, sink=None,
              want_dbias=False, dq_dtype=F32, rides=(), name):
    G, T, S, C, W, Hkv = cfg.G, cfg.T, cfg.S, cfg.C, cfg.W, cfg.Hkv
    assert q_cb % G == 0 and do_cb % G == 0 and not (want_dbias and G > 1)
    operands = [q, k, v] + ([q2, k2] if cfg.two else []) + ([bias] if cfg.n_var else []) + ([sink] if cfg.has_sink else [])
    operands.append(do)
    in_specs = _attn_specs(cfg, q_cb, k_cb, v_cb, q2_cb, k2_cb)
    bq = cfg.bq
    do_blocks = do.shape[0] // bq
    assert do.shape[0] == T or (do.shape[0] == S and not cfg.ctx_queries)
    in_specs.append(pl.BlockSpec((bq, G * LANE), lambda h, j: (jnp.minimum(j, do_blocks - 1), do_cb // G + h)))

    out_specs = [pl.BlockSpec((bq, G * LANE), lambda h, j: (j, h)),
                 pl.BlockSpec((T, LANE), lambda h, j: (0, h)),
                 pl.BlockSpec((T, LANE), lambda h, j: (0, h))]
    out_shape = [jax.ShapeDtypeStruct((T, Hkv * G * LANE), dq_dtype),
                 jax.ShapeDtypeStruct((T, Hkv * LANE), F32),
                 jax.ShapeDtypeStruct((T, Hkv * LANE), F32)]
    if cfg.two:
        out_specs += [pl.BlockSpec((bq, LANE), lambda h, j: (j, h)), pl.BlockSpec((T, LANE), lambda h, j: (0, 0))]
        out_shape += [jax.ShapeDtypeStruct((T, Hkv * LANE), dq_dtype), jax.ShapeDtypeStruct((T, LANE), F32)]
    if want_dbias:
        out_specs.append(pl.BlockSpec((1, cfg.n_var, BQ, W), lambda h, j: (h, 0, 0, 0)))
        out_shape.append(jax.ShapeDtypeStruct((Hkv, cfg.n_var, BQ, W), F32))
    if cfg.has_sink:
        out_specs.append(pl.BlockSpec((1, G * BQ, LANE), lambda h, j: (h, 0, 0)))
        out_shape.append(jax.ShapeDtypeStruct((Hkv, G * BQ, LANE), F32))

    n_ride = sum(len(r.arrays) for r in rides)
    operands += [a for r in rides for a in r.arrays]
    in_specs += [_ANY] * n_ride
    out_specs += [_ANY] * n_ride
    out_shape += [s for r in rides for s in r.out_shapes()]
    n_q = T // bq

    def body(*refs):
        (q_ref, k_ref, v_ref, q2_ref, k2_ref, bias_ref, sink_ref), rest = _attn_unpack(cfg, refs)
        do_ref, ride_in = rest[0], rest[1:1 + n_ride]
        dq_ref, dk_ref, dv_ref = rest[1 + n_ride:4 + n_ride]
        rest = rest[4 + n_ride:]
        dq2_ref = dk2_ref = dbias_ref = dsink_ref = None
        if cfg.two:
            dq2_ref, dk2_ref = rest[:2]
            rest = rest[2:]
        if want_dbias:
            dbias_ref = rest[0]
            rest = rest[1:]
        if cfg.has_sink:
            dsink_ref = rest[0]
            rest = rest[1:]
        ride_groups = _ride_split(rides, ride_in, rest[:n_ride], rest[n_ride:])
        h = pl.program_id(0)
        j = pl.program_id(1)
        if rides:
            pl.when((h == 0) & (j == 0))(lambda: _ride_start(rides, ride_groups))

        @pl.when(j == 0)
        def _():
            dk_ref[...] = jnp.zeros_like(dk_ref)
            dv_ref[...] = jnp.zeros_like(dv_ref)
            if want_dbias:
                dbias_ref[...] = jnp.zeros_like(dbias_ref)
            if cfg.has_sink:
                dsink_ref[...] = jnp.zeros_like(dsink_ref)

        if cfg.two:
            @pl.when((j == 0) & (h == 0))
            def _():
                dk2_ref[...] = jnp.zeros_like(dk2_ref)

        def block():
            p_w, p_c, p_s, (off, var, qs, kw, kc, q2s, k2w, k2c) = _attn_probs(
                cfg, j, q_ref, k_ref, q2_ref, k2_ref, bias_ref, sink_ref)
            dot_ = do_ref[...]
            dos = jnp.concatenate([dot_[:, g * LANE:(g + 1) * LANE] for g in range(G)], axis=0) if G > 1 else dot_
            dos = dos.astype(BF16)
            vw = v_ref[pl.ds(off, W), :]
            vc = v_ref[pl.ds(S, C), :]
            dp_w = _dot_nt(dos, vw)
            dp_c = _dot_nt(dos, vc)
            delta = jnp.sum(p_w * dp_w, axis=-1, keepdims=True) + jnp.sum(p_c * dp_c, axis=-1, keepdims=True)
            ds_w = p_w * (dp_w - delta)
            ds_c = p_c * (dp_c - delta)
            if want_dbias:
                dbias_ref[0, pl.ds(var, 1)] += ds_w[None]
            if cfg.has_sink:
                dsink_ref[0] += jnp.broadcast_to(-(p_s * delta), (G * bq, LANE))
            dsw = (ds_w * cfg.scale).astype(BF16)
            dsc = (ds_c * cfg.scale).astype(BF16)
            dq = _dot(dsw, kw) + _dot(dsc, kc)
            for g in range(G):
                dq_ref[:, g * LANE:(g + 1) * LANE] = dq[g * bq:(g + 1) * bq].astype(dq_ref.dtype)
            dk_ref[pl.ds(off, W), :] += _dot_tn(dsw, qs)
            dk_ref[pl.ds(S, C), :] += _dot_tn(dsc, qs)
            dv_ref[pl.ds(off, W), :] += _dot_tn(p_w.astype(BF16), dos)
            dv_ref[pl.ds(S, C), :] += _dot_tn(p_c.astype(BF16), dos)
            if cfg.two:
                dq2_ref[...] = (_dot(dsw, k2w) + _dot(dsc, k2c)).astype(dq2_ref.dtype)
                dk2_ref[pl.ds(off, W), :] += _dot_tn(dsw, q2s)
                dk2_ref[pl.ds(S, C), :] += _dot_tn(dsc, q2s)

        if cfg.ctx_queries:
            block()
        else:
            pl.when(j * bq < S)(block)

            @pl.when(j * bq >= S)
            def _():
                dq_ref[...] = jnp.zeros_like(dq_ref)
                if cfg.two:
                    dq2_ref[...] = jnp.zeros_like(dq2_ref)

        if rides:
            pl.when((h == Hkv - 1) & (j == n_q - 1))(lambda: _ride_wait(rides, ride_groups))

    outs = pl.pallas_call(
        body, name=name, grid=(Hkv, n_q),
        in_specs=in_specs, out_specs=out_specs, out_shape=out_shape,
        scratch_shapes=_ride_scratch(rides),
        compiler_params=_cparams(("arbitrary", "arbitrary")),
    )(*operands)
    if not rides:
        return outs
    return list(outs[:len(outs) - n_ride]) + [_ride_outputs(rides, outs[len(outs) - n_ride:])]


def _na_bias(rpb, S):
    H = rpb.shape[0]
    rows = S // GRID_W
    pad_l = GRID_W - 1 - (NA_WIN_C - 1)
    ext = jnp.concatenate([jnp.broadcast_to(rpb[:, :, :1], (H, 2 * NA_WIN_R - 1, pad_l)), rpb,
                           jnp.broadcast_to(rpb[:, :, -1:], (H, 2 * NA_WIN_R - 1, pad_l))], axis=2)
    by_col = jnp.stack([ext[:, :, GRID_W - 1 - qc:2 * GRID_W - 1 - qc] for qc in range(GRID_W)], axis=2)
    cq = np.arange(GRID_W)
    c0 = np.clip(cq - NA_WIN_C // 2, 0, GRID_W - NA_WIN_C)
    col_in = (cq[None, :] >= c0[:, None]) & (cq[None, :] < c0[:, None] + NA_WIN_C)
    n_lat = S // BQ
    neg_tile = jnp.full((H, GRID_W, GRID_W), NEG, F32)
    variants = []
    for v in range(5):
        j = {0: 0, 1: 1, 2: 2, 3: n_lat - 2, 4: n_lat - 1}[v]
        start = int(np.clip(j - 2, 0, n_lat - 5))
        assert j - start == v
        q_rows = []
        for qr in range(2):
            r = 2 * j + qr
            r0 = int(np.clip(r - NA_WIN_R // 2, 0, rows - NA_WIN_R))
            k_tiles = []
            for kr in range(10):
                krow = 2 * start + kr
                if r0 <= krow < r0 + NA_WIN_R:
                    k_tiles.append(jnp.where(col_in[None], by_col[:, krow - r + NA_WIN_R - 1], NEG))
                else:
                    k_tiles.append(neg_tile)
            q_rows.append(jnp.concatenate(k_tiles, axis=2))
        variants.append(jnp.concatenate(q_rows, axis=1))
    return jnp.stack(variants, axis=1)


def _swa_mask(S):
    qq = np.arange(BQ)[:, None]
    kk = np.arange(3 * BQ)[None, :]
    tiles = [np.where(np.abs(kk - v * BQ - qq) <= SWA_WINDOW, 0.0, NEG) for v in range(3)]
    return jnp.asarray(np.stack(tiles)[None], F32)


def _ffn_tiles(T, S, F):
    return _row_block(T, S), _pick(F, 1408)


def _halo_specs(T, bt, bf, col_off):
    n8 = bt // 8
    return [pl.BlockSpec((bt, bf), lambda f, i: (i, f + col_off)),
            pl.BlockSpec((8, bf), lambda f, i: (jnp.maximum(i * n8 - 1, 0), f + col_off)),
            pl.BlockSpec((8, bf), lambda f, i: (jnp.minimum((i + 1) * n8, T // 8 - 1), f + col_off))]


def _neighbours(x, prev8, next8, i, bt, S, T):
    r = lax.broadcasted_iota(jnp.int32, x.shape, 0)
    g0 = i * bt
    first_open = jnp.logical_or(g0 == 0, g0 == S)
    last_open = jnp.logical_or(g0 + bt == S, g0 + bt == T)
    before = jnp.where(r == 0, jnp.where(first_open, 0.0, prev8[7:8, :]), pltpu.roll(x, 1, 0))
    after = jnp.where(r == bt - 1, jnp.where(last_open, 0.0, next8[0:1, :]), pltpu.roll(x, bt - 1, 0))
    return before, after


def _sigmoid(a):
    return 1.0 / (1.0 + jnp.exp(-a))


def _ffn_fwd(gp, u, cw, cb, S, *, name):
    T, F = gp.shape
    bt, bf = _ffn_tiles(T, S, F)

    def body(g_ref, gp_ref, gn_ref, u_ref, w_ref, b_ref, o_ref):
        i = pl.program_id(1)
        g = g_ref[...]
        before, after = _neighbours(g, gp_ref[...], gn_ref[...], i, bt, S, T)
        a = before * w_ref[0:1, :] + g * w_ref[1:2, :] + after * w_ref[2:3, :] + b_ref[...]
        o_ref[...] = (a * _sigmoid(a) * u_ref[...]).astype(o_ref.dtype)

    return pl.pallas_call(
        body, name=name, grid=(F // bf, T // bt),
        in_specs=_halo_specs(T, bt, bf, 0) + [pl.BlockSpec((bt, bf), lambda f, i: (i, f)),
                                              pl.BlockSpec((3, bf), lambda f, i: (0, f)),
                                              pl.BlockSpec((1, bf), lambda f, i: (0, f))],
        out_specs=pl.BlockSpec((bt, bf), lambda f, i: (i, f)),
        out_shape=jax.ShapeDtypeStruct((T, F), BF16),
        compiler_params=_cparams(("parallel", "parallel")),
    )(gp, gp, gp, u, cw, cb)


def _ffn_bwd_act(gp, u, da_out, cw, cb, S, *, name):
    T, F = gp.shape
    bt, bf = _ffn_tiles(T, S, F)

    def body(g_ref, gp_ref, gn_ref, u_ref, d_ref, w_ref, b_ref, da_ref, du_ref, acc_ref):
        i = pl.program_id(1)

        @pl.when(i == 0)
        def _():
            acc_ref[...] = jnp.zeros_like(acc_ref)

        g = g_ref[...]
        before, after = _neighbours(g, gp_ref[...], gn_ref[...], i, bt, S, T)
        a = before * w_ref[0:1, :] + g * w_ref[1:2, :] + after * w_ref[2:3, :] + b_ref[...]
        sig = _sigmoid(a)
        d = d_ref[...]
        du_ref[...] = (d * (a * sig)).astype(du_ref.dtype)
        da = d * u_ref[...] * (sig * (1.0 + a * (1.0 - sig)))
        da_ref[...] = da
        acc_ref[0:1, :] += jnp.sum(da * before, axis=0, keepdims=True)
        acc_ref[1:2, :] += jnp.sum(da * g, axis=0, keepdims=True)
        acc_ref[2:3, :] += jnp.sum(da * after, axis=0, keepdims=True)
        acc_ref[3:4, :] += jnp.sum(da, axis=0, keepdims=True)

    blk = pl.BlockSpec((bt, bf), lambda f, i: (i, f))
    return pl.pallas_call(
        body, name=name, grid=(F // bf, T // bt),
        in_specs=_halo_specs(T, bt, bf, 0) + [blk, blk,
                                              pl.BlockSpec((3, bf), lambda f, i: (0, f)),
                                              pl.BlockSpec((1, bf), lambda f, i: (0, f))],
        out_specs=[blk, blk, pl.BlockSpec((8, bf), lambda f, i: (0, f))],
        out_shape=[jax.ShapeDtypeStruct((T, F), F32), jax.ShapeDtypeStruct((T, F), BF16),
                   jax.ShapeDtypeStruct((8, F), F32)],
        compiler_params=_cparams(("parallel", "arbitrary")),
    )(gp, gp, gp, u, da_out, cw, cb)


def _ffn_bwd_conv(da, cw, S, *, name):
    T, F = da.shape
    bt, bf = _ffn_tiles(T, S, F)

    def body(d_ref, dp_ref, dn_ref, w_ref, o_ref):
        i = pl.program_id(1)
        d = d_ref[...]
        before, after = _neighbours(d, dp_ref[...], dn_ref[...], i, bt, S, T)
        o_ref[...] = (after * w_ref[0:1, :] + d * w_ref[1:2, :] + before * w_ref[2:3, :]).astype(o_ref.dtype)

    return pl.pallas_call(
        body, name=name, grid=(F // bf, T // bt),
        in_specs=_halo_specs(T, bt, bf, 0) + [pl.BlockSpec((3, bf), lambda f, i: (0, f))],
        out_specs=pl.BlockSpec((bt, bf), lambda f, i: (i, f)),
        out_shape=jax.ShapeDtypeStruct((T, F), BF16),
        compiler_params=_cparams(("parallel", "parallel")),
    )(da, da, da, cw)


def _ew_rows(R, N, n_arrays):
    return _pick(R, max(16, EW_VMEM_BUDGET // (8 * n_arrays * N)), 16)


def _adam(w, g, m, v, *, name):
    R, N = w.shape
    br = _ew_rows(R, N, 7)
    bc1 = 1.0 - ADAM_B1 ** ADAM_STEP
    bc2 = 1.0 - ADAM_B2 ** ADAM_STEP

    def body(w_ref, g_ref, m_ref, v_ref, d_ref, mo_ref, vo_ref):
        gv = g_ref[...]
        mn = ADAM_B1 * m_ref[...] + (1.0 - ADAM_B1) * gv
        vn = ADAM_B2 * v_ref[...] + (1.0 - ADAM_B2) * (gv * gv)
        mo_ref[...] = mn
        vo_ref[...] = vn
        d_ref[...] = -ADAM_LR * ((mn / bc1) / (jnp.sqrt(vn / bc2) + ADAM_EPS) + ADAM_WD * w_ref[...])

    blk = pl.BlockSpec((br, N), lambda i: (i, 0))
    shp = jax.ShapeDtypeStruct((R, N), F32)
    return pl.pallas_call(
        body, name=name, grid=(R // br,),
        in_specs=[blk, blk, blk, blk], out_specs=[blk, blk, blk], out_shape=[shp, shp, shp],
        compiler_params=_cparams(("parallel",)),
    )(w, g, m, v)


def _sum_lead(x, out_dtype, *, name):
    n, R, N = x.shape
    br = _ew_rows(R, N, n + 1)

    def body(x_ref, o_ref):
        acc = x_ref[0].astype(F32)
        for k in range(1, n):
            acc = acc + x_ref[k].astype(F32)
        o_ref[...] = acc.astype(o_ref.dtype)

    return pl.pallas_call(
        body, name=name, grid=(R // br,),
        in_specs=[pl.BlockSpec((n, br, N), lambda i: (0, i, 0))],
        out_specs=pl.BlockSpec((br, N), lambda i: (i, 0)),
        out_shape=jax.ShapeDtypeStruct((R, N), out_dtype),
        compiler_params=_cparams(("parallel",)),
    )(x)


def _sum_parts(parts, landed, chip, core, stack, *, name):
    _, R, N = parts.shape
    n_layers, layer, buf = stack
    br = _ew_rows(R, N, 5)

    def body(pos_ref, own_ref, landed_ref, *rest):
        o_ref = rest[-1]
        acc = own_ref[...].astype(F32)
        for k in range(3):
            acc = acc + landed_ref[k].astype(F32)
        o_ref[...] = acc

    operands = [jnp.stack([chip, core]).astype(jnp.int32), parts, landed]
    in_specs = [pl.BlockSpec((None, br, N), lambda i, pos: (pos[0], i, 0)),
                pl.BlockSpec((3, br, N), lambda i, pos: (0, i, 0))]
    aliases = {}
    if buf is not None:
        aliases = {3: 0}
        operands.append(buf)
        in_specs.append(pl.BlockSpec(memory_space=pl.ANY))
    return pl.pallas_call(
        body, name=name,
        grid_spec=pltpu.PrefetchScalarGridSpec(
            num_scalar_prefetch=1, grid=(R // br,), in_specs=in_specs,
            out_specs=pl.BlockSpec((None, None, br, N), lambda i, pos: (layer, pos[1], i, 0))),
        out_shape=jax.ShapeDtypeStruct((n_layers, 2, R, N), F32),
        input_output_aliases=aliases,
        compiler_params=_cparams(("parallel",)),
    )(*operands)


def _add_half(g, r, core, *, name):
    Q, _, R, N = g.shape
    br = _ew_rows(R, N, 3)

    def body(c_ref, g_ref, r_ref, o_ref):
        o_ref[...] = (g_ref[...] + r_ref[...]).astype(o_ref.dtype)

    return pl.pallas_call(
        body, name=name,
        grid_spec=pltpu.PrefetchScalarGridSpec(
            num_scalar_prefetch=1, grid=(Q, R // br),
            in_specs=[pl.BlockSpec((None, None, br, N), lambda q, i, c_ref: (q, c_ref[0], i, 0)),
                      pl.BlockSpec((None, br, N), lambda q, i, c_ref: (q, i, 0))],
            out_specs=pl.BlockSpec((None, br, N), lambda q, i, c_ref: (q, i, 0))),
        out_shape=jax.ShapeDtypeStruct((Q, R, N), BF16),
        compiler_params=_cparams(("parallel", "parallel")),
    )(core.reshape(1).astype(jnp.int32), g, r)


_ANY = pl.BlockSpec(memory_space=pl.ANY)


def _place():
    return lax.axis_index("x"), lax.axis_index("y"), lax.axis_index("c")


def _allgather8(blocks, *, name):
    n = len(blocks)

    def body(*refs):
        xs, outs = refs[:n], refs[n:2 * n]
        send_sems, recv_sems, local_sems = refs[2 * n:]
        x, y, c = _place()
        me, sibling = (x, y, c), (x, y, 1 - c)
        chips = [(1 - x, y), (x, 1 - y), (1 - x, 1 - y)]

        def slot(a, px, py, pc):
            return outs[a].at[4 * px + 2 * py + pc]

        def copy(a, k, block, to, src=None):
            return pltpu.make_async_remote_copy(
                src_ref=slot(a, *block) if src is None else src, dst_ref=slot(a, *block),
                send_sem=send_sems.at[a, k], recv_sem=recv_sems.at[a, k], device_id=to, device_id_type=MESH)

        mine = [pltpu.make_async_copy(xs[a], slot(a, *me), local_sems.at[a]) for a in range(n)]
        for cp in mine:
            cp.start()
        first = []
        for a in range(n):
            first.append(copy(a, 0, me, sibling, src=xs[a]))
            first += [copy(a, 1 + j, me, (*chip, c), src=xs[a]) for j, chip in enumerate(chips)]
        for cp in first:
            cp.start()
        passed = []
        for j, chip in enumerate(chips):
            for a in range(n):
                copy(a, 1 + j, (*chip, c), me).wait_recv()
                fwd = copy(a, 4 + j, (*chip, c), sibling)
                fwd.start()
                passed.append(fwd)
        for a in range(n):
            copy(a, 0, sibling, me).wait_recv()
            for j, chip in enumerate(chips):
                copy(a, 4 + j, (*chip, 1 - c), me).wait_recv()
        for cp in first + passed:
            cp.wait_send()
        for cp in mine:
            cp.wait()

    return pl.pallas_call(
        body, name=name,
        in_specs=[_ANY] * n, out_specs=[_ANY] * n,
        out_shape=[jax.ShapeDtypeStruct((8,) + b.shape, b.dtype) for b in blocks],
        scratch_shapes=[pltpu.SemaphoreType.DMA((n, 7)), pltpu.SemaphoreType.DMA((n, 7)), pltpu.SemaphoreType.DMA((n,))],
    )(*blocks)


class _Exchange:
    n_sems = 1

    def __init__(self, arrays):
        self.arrays = list(arrays)

    def out_shapes(self):
        return [jax.ShapeDtypeStruct(g.shape[:1] + g.shape[2:], g.dtype) for g in self.arrays]

    def copies(self, xs, outs, send_sems, recv_sems):
        x, y, c = _place()
        cps = [pltpu.make_async_remote_copy(
            src_ref=xs[a].at[:, 1 - c], dst_ref=outs[a], send_sem=send_sems.at[a, 0], recv_sem=recv_sems.at[a, 0],
            device_id=(x, y, 1 - c), device_id_type=MESH) for a in range(len(xs))]
        return cps, cps


class _Scatter(_Exchange):
    n_sems = 3

    def out_shapes(self):
        return [jax.ShapeDtypeStruct((3,) + p.shape[1:], p.dtype) for p in self.arrays]

    def copies(self, xs, outs, send_sems, recv_sems):
        x, y, c = _place()
        sends, recvs = [], []
        for a in range(len(xs)):
            for k, (px, py) in enumerate([(1 - x, y), (x, 1 - y), (1 - x, 1 - y)]):
                sends.append(pltpu.make_async_remote_copy(
                    src_ref=xs[a].at[2 * px + py], dst_ref=outs[a].at[k], send_sem=send_sems.at[a, k],
                    recv_sem=recv_sems.at[a, k], device_id=(px, py, c), device_id_type=MESH))
                recvs.append(sends[-1])
        return sends, recvs


def _ride_scratch(rides):
    shapes = []
    for r in rides:
        shapes += [pltpu.SemaphoreType.DMA((len(r.arrays), r.n_sems)), pltpu.SemaphoreType.DMA((len(r.arrays), r.n_sems))]
    return shapes


def _ride_split(rides, in_refs, out_refs, sem_refs):
    groups, i, o = [], 0, 0
    for k, r in enumerate(rides):
        n = len(r.arrays)
        groups.append((in_refs[i:i + n], out_refs[o:o + n], sem_refs[2 * k], sem_refs[2 * k + 1]))
        i, o = i + n, o + n
    return groups


def _ride_start(rides, groups):
    for r, g in zip(rides, groups):
        for cp in r.copies(*g)[0]:
            cp.start()


def _ride_wait(rides, groups):
    for r, g in zip(rides, groups):
        sends, recvs = r.copies(*g)
        for cp in recvs:
            cp.wait_recv()
        for cp in sends:
            cp.wait_send()


def _run_rides(rides, *, name):
    n_in = sum(len(r.arrays) for r in rides)

    def body(*refs):
        groups = _ride_split(rides, refs[:n_in], refs[n_in:2 * n_in], refs[2 * n_in:])
        _ride_start(rides, groups)
        _ride_wait(rides, groups)

    outs = pl.pallas_call(
        body, name=name,
        in_specs=[_ANY] * n_in, out_specs=[_ANY] * n_in,
        out_shape=[s for r in rides for s in r.out_shapes()],
        scratch_shapes=_ride_scratch(rides),
    )(*[a for r in rides for a in r.arrays])
    return _ride_outputs(rides, outs)


def _ride_outputs(rides, outs):
    res, o = [], 0
    for r in rides:
        res.append(list(outs[o:o + len(r.arrays)]))
        o += len(r.arrays)
    return res


def _join_halves(bufs, *, name):
    n = len(bufs)

    def body(*refs):
        xs, outs = refs[:n], refs[n:2 * n]
        send_sems, recv_sems = refs[2 * n:]
        x, y, c = _place()
        sends = [pltpu.make_async_remote_copy(
            src_ref=xs[a].at[:, c], dst_ref=outs[a].at[:, c], send_sem=send_sems.at[a], recv_sem=recv_sems.at[a],
            device_id=(x, y, 1 - c), device_id_type=MESH) for a in range(n)]
        recvs = [pltpu.make_async_remote_copy(
            src_ref=xs[a].at[:, c], dst_ref=outs[a].at[:, 1 - c], send_sem=send_sems.at[a], recv_sem=recv_sems.at[a],
            device_id=(x, y, 1 - c), device_id_type=MESH) for a in range(n)]
        for cp in sends:
            cp.start()
        for cp in recvs:
            cp.wait_recv()
        for cp in sends:
            cp.wait_send()

    return pl.pallas_call(
        body, name=name,
        in_specs=[_ANY] * n, out_specs=[_ANY] * n,
        out_shape=[jax.ShapeDtypeStruct(b.shape, b.dtype) for b in bufs],
        input_output_aliases={a: a for a in range(n)},
        scratch_shapes=[pltpu.SemaphoreType.DMA((n,)), pltpu.SemaphoreType.DMA((n,))],
    )(*bufs)


def _perm_w_in(w):
    pad = jnp.zeros((w.shape[0], PCOLS - IN_COLS), w.dtype)
    return jnp.concatenate([w[:, :3072], w[:, 3136:IN_COLS], w[:, 3072:3136], pad], axis=1)


def _unperm_w_in(g):
    return jnp.concatenate([g[:, :3072], g[:, 4096:IN_COLS], g[:, 3072:4096]], axis=1)


def _perm_w_uq(w):
    w4 = w.reshape(MLA_Q_LORA, MLA_HEADS, MLA_NOPE + MLA_ROPE)
    nope = w4[:, :, :MLA_NOPE].reshape(MLA_Q_LORA, MLA_HEADS * LANE)
    pe = jnp.pad(w4[:, :, MLA_NOPE:], ((0, 0), (0, 0), (0, LANE - MLA_ROPE))).reshape(MLA_Q_LORA, MLA_HEADS * LANE)
    return jnp.concatenate([nope, pe], axis=1)


def _unperm_w_uq(g):
    nope = g[:, :MLA_HEADS * LANE].reshape(MLA_Q_LORA, MLA_HEADS, LANE)
    pe = g[:, MLA_HEADS * LANE:].reshape(MLA_Q_LORA, MLA_HEADS, LANE)[:, :, :MLA_ROPE]
    return jnp.concatenate([nope, pe], axis=2).reshape(MLA_Q_LORA, MLA_HEADS * (MLA_NOPE + MLA_ROPE))


def _perm_w_ukv(w):
    w4 = w.reshape(MLA_KV_LORA, MLA_HEADS, MLA_NOPE + MLA_V)
    return jnp.concatenate([w4[:, :, :MLA_NOPE].reshape(MLA_KV_LORA, -1), w4[:, :, MLA_NOPE:].reshape(MLA_KV_LORA, -1)], axis=1)


def _unperm_w_ukv(g):
    kn = g[:, :MLA_HEADS * LANE].reshape(MLA_KV_LORA, MLA_HEADS, LANE)
    vv = g[:, MLA_HEADS * LANE:].reshape(MLA_KV_LORA, MLA_HEADS, LANE)
    return jnp.concatenate([kn, vv], axis=2).reshape(MLA_KV_LORA, -1)


def _silu(v):
    return v * jax.nn.sigmoid(v)


def _silu_grad(v):
    s = jax.nn.sigmoid(v)
    return s * (1.0 + v * (1.0 - s))


_WEIGHTS = ("c_ctx", "w_ada", "b_ada", "w_in", "na_rpb", "swa_sink", "mla_q_norm", "mla_kv_norm", "mla_w_uq", "mla_w_ukv",
            "gqa_q_norm", "gqa_k_norm", "w_out", "ln1_g", "ln1_b", "ffn_w_gate", "ffn_w_up", "ffn_conv_w", "ffn_conv_b",
            "ffn_w_down", "ln2_g", "ln2_b")
_COL_SHARDED = ("w_in", "mla_w_uq", "mla_w_ukv", "ffn_w_gate", "ffn_w_up")
_ROW_SHARDED = ("w_out", "ffn_w_down")
_BIG = _COL_SHARDED + _ROW_SHARDED
_SMALL = ("c_ctx", "b_ada", "na_rpb", "swa_sink", "mla_q_norm", "mla_kv_norm", "gqa_q_norm", "gqa_k_norm", "ln1_g", "ln1_b",
          "ffn_conv_w", "ffn_conv_b", "ln2_g", "ln2_b")


def _pack(arrays):
    flat = jnp.concatenate([a.reshape(-1) for a in arrays])
    n = flat.shape[0]
    rows = -(-n // (8 * LANE)) * 8
    return jnp.pad(flat, (0, rows * LANE - n)).reshape(rows, LANE)


def _unpack(packed, like):
    flat = packed.reshape(-1)
    out, o = [], 0
    for a in like:
        out.append(flat[o:o + a.size].reshape(a.shape))
        o += a.size
    return out


def _train_step(x, c, ctx, loss_target, w, m_in, v_in):
    L = DEPTH
    S, D = x.shape[1], x.shape[2]
    C = ctx.shape[1]
    T = S + C
    F = w["ffn_conv_b"].shape[1]
    ax, ay, ac = _place()
    chip = 2 * ax + ay
    dev = 2 * chip + ac
    n_ada = w["w_ada"].shape[2]

    def my_half(a):
        r = a.shape[1] // 2
        return lax.dynamic_slice_in_dim(a, ac * r, r, axis=1).astype(BF16)

    gathered = _allgather8([my_half(w[n]) for n in _BIG] + [w["ffn_conv_w"]], name="gather_weights")
    full = {}
    for n, g in zip(_BIG, gathered[:-1]):
        _, _, r, cols = g.shape
        g = g.reshape(4, 2, L, r, cols)
        if n in _COL_SHARDED:
            full[n] = g.transpose(2, 1, 3, 0, 4).reshape(L, 2 * r, 4 * cols)
        else:
            full[n] = g.transpose(2, 0, 1, 3, 4).reshape(L, 8 * r, cols)
    conv_w = gathered[-1][::2].transpose(1, 2, 0, 3).reshape(L, 3, F)
    w_in_p = [_perm_w_in(full["w_in"][l]) for l in range(L)]
    w_uq_p = [_perm_w_uq(full["mla_w_uq"][l]) for l in range(L)]
    w_ukv_p = [_perm_w_ukv(full["mla_w_ukv"][l]) for l in range(L)]

    (c_all,) = _allgather8([c], name="gather_c")
    c16 = jnp.concatenate([c_all.reshape(8, D), jnp.broadcast_to(w["c_ctx"][None], (8, D))], axis=0)
    row_keep = (jnp.arange(16) <= 8).astype(F32)[:, None]
    sc = _silu(c16) * row_keep
    b_loc = lax.dynamic_slice_in_dim(w["b_ada"], chip * n_ada, n_ada, axis=1)
    mod_loc = jnp.stack([_mm(sc, w["w_ada"][l], name="mod_mm") + b_loc[l][None] for l in range(L)])
    (mod_g,) = _allgather8([mod_loc], name="gather_mod")
    mod_all = mod_g[::2].transpose(1, 2, 0, 3).reshape(L, 16, 4 * n_ada)
    mod_x = lax.dynamic_index_in_dim(mod_all, dev, axis=1, keepdims=False)
    mod_c = mod_all[:, 8]
    mods = [jnp.stack([mod_x[l].reshape(6, D), mod_c[l].reshape(6, D)], axis=1) for l in range(L)]

    tabs = _rope_tables(S, C, HEAD_DIM) + _rope_tables(S, C, MLA_ROPE)
    swa_mask = _swa_mask(S)
    scale = HEAD_DIM ** -0.5
    def attn_cfgs(l):
        cq = l < L - 1
        return (_AttnCfg(Hkv=NA_HEADS, G=1, S=S, C=C, band=(2, 5), scale=scale, n_var=5, bias_per_head=True, ctx_queries=cq),
                _AttnCfg(Hkv=SWA_KV_HEADS, G=SWA_HEADS // SWA_KV_HEADS, S=S, C=C, band=(1, 3), scale=scale, n_var=3,
                         has_sink=True, ctx_queries=cq),
                _AttnCfg(Hkv=MLA_HEADS, G=1, S=S, C=C, band=None, scale=(MLA_NOPE + MLA_ROPE) ** -0.5, two=True,
                         bq=2 * BQ, ctx_queries=cq),
                _AttnCfg(Hkv=GQA_KV_HEADS, G=GQA_HEADS // GQA_KV_HEADS, S=S, C=C, band=None, scale=scale, ctx_queries=cq))

    row = lambda a: a[None, :]

    xt = jnp.concatenate([x[0], ctx[0]], axis=0)
    saved = []
    for l in range(L):
        md = mods[l]
        gq, gk, mq, mkv = row(w["gqa_q_norm"][l]), row(w["gqa_k_norm"][l]), row(w["mla_q_norm"][l]), row(w["mla_kv_norm"][l])
        h1 = _mod_fwd(xt, md[0], md[1], S, name="mod_fwd")
        p = _mm(h1, w_in_p[l], name="in_proj")
        qkv = _prep_fwd(p, tabs, gq, gk, mq, mkv, S, name="prep_fwd")
        qm = _mm(qkv, w_uq_p[l], a_off=CB_CQ * LANE, a_k=MLA_Q_LORA, tk=LANE, name="mla_uq")
        qmb = _pe_rope(qm, tabs, S, 1.0, BF16, name="mla_q_rope")
        kvm = _mm(qkv, w_ukv_p[l], a_off=CB_CKV * LANE, a_k=MLA_KV_LORA, tk=LANE, out_dtype=BF16, name="mla_ukv")
        bias_na = _na_bias(w["na_rpb"][l], S)
        sink = jnp.broadcast_to(jnp.repeat(w["swa_sink"][l].reshape(SWA_KV_HEADS, -1), BQ, axis=1)[:, :, None],
                                (SWA_KV_HEADS, SWA_HEADS // SWA_KV_HEADS * BQ, LANE))
        cfg_na, cfg_swa, cfg_mla, cfg_gqa = attn_cfgs(l)
        rows = T if l < L - 1 else S
        oa = _attn_fwd(cfg_na, qkv, CB_NA_Q, qkv, CB_NA_K, qkv, CB_NA_V, bias=bias_na, name="na_fwd")
        ob = _attn_fwd(cfg_swa, qkv, CB_SWA_Q, qkv, CB_SWA_K, qkv, CB_SWA_V, bias=swa_mask, sink=sink, name="swa_fwd")
        oc = _attn_fwd(cfg_mla, qmb, 0, kvm, 0, kvm, MLA_HEADS, q2=qmb, q2_cb=MLA_HEADS, k2=qkv, k2_cb=CB_KPE, name="mla_fwd")
        od = _attn_fwd(cfg_gqa, qkv, CB_GQA_Q, qkv, CB_GQA_K, qkv, CB_GQA_V, name="gqa_fwd")
        mix = jnp.concatenate([oa, ob, oc, od], axis=1)
        z1 = _mm(mix, full["w_out"][l], rows=rows, name="out_proj")
        x1 = _res_fwd(xt, z1, md[2], row(w["ln1_g"][l]), row(w["ln1_b"][l]), S, name="res_fwd")
        h2 = _mod_fwd(x1, md[3], md[4], S, name="mod_fwd")
        gp = _mm(h2, full["ffn_w_gate"][l], name="ffn_in")
        up = _mm(h2, full["ffn_w_up"][l], name="ffn_in")
        act = _ffn_fwd(gp, up, conv_w[l], row(w["ffn_conv_b"][l]), S, name="ffn_mid")
        z2 = _mm(act, full["ffn_w_down"][l], name="ffn_out")
        x2 = _res_fwd(x1, z2, md[5], row(w["ln2_g"][l]), row(w["ln2_b"][l]), S, name="res_fwd")
        saved.append(dict(x=xt, h1=h1, p=p, qkv=qkv, qmb=qmb, kvm=kvm, bias_na=bias_na, sink=sink, mix=mix, z1=z1, x1=x1,
                          h2=h2, gp=gp, up=up, act=act, z2=z2, cfgs=(cfg_na, cfg_swa, cfg_mla, cfg_gqa)))
        xt = x2

    dx, loss_part = _loss_fwd_bwd(xt, loss_target[0], S, name="loss")
    loss = lax.psum(loss_part[0, 0], ("x", "y", "c"))

    groups = {"ffn": ("ffn_w_gate", "ffn_w_up", "ffn_w_down"), "rest": ("w_in", "mla_w_uq", "mla_w_ukv", "w_out")}
    wgrad = [dict() for _ in range(L)]
    parts, landed = {}, {}

    def halves_of(group, l):
        return [wgrad[l][n].reshape(4, 2, wgrad[l][n].shape[1] // 2, wgrad[l][n].shape[2]) for n in groups[group]]

    def add_halves(group, l, received):
        parts[(group, l)] = [_add_half(h, r, ac, name="rs_core_add") for h, r in zip(halves_of(group, l), received)]

    small = {n: [None] * L for n in ("na_rpb", "swa_sink", "mla_q_norm", "mla_kv_norm", "gqa_q_norm", "gqa_k_norm",
                                     "ln1_g", "ln1_b", "ffn_conv_w", "ffn_conv_b", "ln2_g", "ln2_b")}
    dmod = [None] * L
    for l in reversed(range(L)):
        sv, md = saved[l], mods[l]
        gq, gk, mq, mkv = row(w["gqa_q_norm"][l]), row(w["gqa_k_norm"][l]), row(w["mla_q_norm"][l]), row(w["mla_kv_norm"][l])
        cb_row = row(w["ffn_conv_b"][l])
        dx1, dz2, acc_r2 = _res_bwd(sv["x1"], sv["z2"], md[5], row(w["ln2_g"][l]), dx, S, name="res_bwd")
        dact = _mm(dz2, full["ffn_w_down"][l], mode="nt", name="ffn_out_dx")
        wgrad[l]["ffn_w_down"] = _mm(sv["act"], dz2, mode="tn", name="ffn_out_dw").reshape(4, F // 4, D)
        da, du, acc_f = _ffn_bwd_act(sv["gp"], sv["up"], dact, conv_w[l], cb_row, S, name="ffn_mid_bwd")
        dg = _ffn_bwd_conv(da, conv_w[l], S, name="ffn_conv_bwd")
        dh2 = _mm(dg, full["ffn_w_gate"][l], mode="nt", name="ffn_in_dx")
        dh2 = _mm(du, full["ffn_w_up"][l], mode="nt", add=dh2, name="ffn_in_dx_add")
        wgrad[l]["ffn_w_gate"] = _mm(sv["h2"], dg, mode="tn", stack=(1, 0, None), split4=True,
                                     name="ffn_in_dw").reshape(4, D, F // 4)
        wgrad[l]["ffn_w_up"] = _mm(sv["h2"], du, mode="tn", stack=(1, 0, None), split4=True,
                                   name="ffn_in_dw").reshape(4, D, F // 4)
        dx1, acc_m2 = _mod_bwd(sv["x1"], dh2, md[4], dx1, S, name="mod_bwd")
        dxa, dz1, acc_r1 = _res_bwd(sv["x"], sv["z1"], md[2], row(w["ln1_g"][l]), dx1, S, name="res_bwd")
        dmix = _mm(dz1, full["w_out"][l], mode="nt", out_dtype=BF16, name="out_proj_dx")
        wgrad[l]["w_out"] = _mm(sv["mix"], dz1, mode="tn", rows=dz1.shape[0], name="out_proj_dw").reshape(4, -1, D)

        qkv, qmb, kvm = sv["qkv"], sv["qmb"], sv["kvm"]
        cfg_na, cfg_swa, cfg_mla, cfg_gqa = sv["cfgs"]
        rest_above = l + 1 < L
        rides = [_Exchange(halves_of("ffn", l))] + ([_Exchange(halves_of("rest", l + 1))] if rest_above else [])
        dq_a, dk_a, dv_a, dbias, received = _attn_bwd(cfg_na, qkv, CB_NA_Q, qkv, CB_NA_K, qkv, CB_NA_V, dmix, 0,
                                                      bias=sv["bias_na"], want_dbias=True, rides=rides, name="na_bwd")
        add_halves("ffn", l, received[0])
        if rest_above:
            add_halves("rest", l + 1, received[1])
        dq_b, dk_b, dv_b, dsink = _attn_bwd(cfg_swa, qkv, CB_SWA_Q, qkv, CB_SWA_K, qkv, CB_SWA_V, dmix, NA_HEADS,
                                            bias=swa_mask, sink=sv["sink"], name="swa_bwd")
        dq_c, dk_c, dv_c, dq2_c, dk2_c, got = _attn_bwd(
            cfg_mla, qmb, 0, kvm, 0, kvm, MLA_HEADS, dmix, NA_HEADS + SWA_HEADS, q2=qmb, q2_cb=MLA_HEADS, k2=qkv,
            k2_cb=CB_KPE, rides=[_Scatter(parts[("ffn", l)])], name="mla_bwd")
        landed[("ffn", l)] = got[0]
        rides = [_Scatter(parts[("rest", l + 1)])] if rest_above else []
        gqa_out = _attn_bwd(cfg_gqa, qkv, CB_GQA_Q, qkv, CB_GQA_K, qkv, CB_GQA_V, dmix,
                            NA_HEADS + SWA_HEADS + MLA_HEADS, rides=rides, name="gqa_bwd")
        dq_d, dk_d, dv_d = gqa_out[:3]
        if rest_above:
            landed[("rest", l + 1)] = gqa_out[3][0]
        dqm = _pe_rope(jnp.concatenate([dq_c, dq2_c], axis=1), tabs, S, -1.0, BF16, name="mla_q_rope_bwd")
        dkvm = jnp.concatenate([dk_c, dv_c], axis=1).astype(BF16)
        dcq = _mm(dqm, w_uq_p[l], mode="nt", name="mla_uq_dx")
        dckv = _mm(dkvm, w_ukv_p[l], mode="nt", name="mla_ukv_dx")
        cqn = qkv[:, CB_CQ * LANE:CB_CKV * LANE]
        ckvn = qkv[:, CB_CKV * LANE:(CB_CKV + 1) * LANE]
        d_uq = _unperm_w_uq(_mm(cqn, dqm, mode="tn", name="mla_uq_dw"))
        d_ukv = _unperm_w_ukv(_mm(ckvn, dkvm, mode="tn", name="mla_ukv_dw"))
        grads = {}
        for h in range(NA_HEADS):
            grads[CB_NA_Q + h], grads[CB_NA_K + h], grads[CB_NA_V + h] = (dq_a, h), (dk_a, h), (dv_a, h)
        for h in range(SWA_HEADS):
            grads[CB_SWA_Q + h] = (dq_b, h)
        for h in range(SWA_KV_HEADS):
            grads[CB_SWA_K + h], grads[CB_SWA_V + h] = (dk_b, h), (dv_b, h)
        for h in range(GQA_HEADS):
            grads[CB_GQA_Q + h] = (dq_d, h)
        for h in range(GQA_KV_HEADS):
            grads[CB_GQA_K + h], grads[CB_GQA_V + h] = (dk_d, h), (dv_d, h)
        grads[CB_KPE], grads[CB_CQ], grads[CB_CKV] = (dk2_c, 0), (dcq, 0), (dckv, 0)
        dp, acc_p = _prep_bwd(sv["p"], grads, tabs, gq, gk, mq, mkv, S, name="prep_bwd")
        dh1 = _mm(dp, w_in_p[l], mode="nt", name="in_proj_dx")
        d_in = _unperm_w_in(_mm(sv["h1"], dp, mode="tn", name="in_proj_dw"))
        dx, acc_m1 = _mod_bwd(sv["x"], dh1, md[1], dxa, S, name="mod_bwd")

        to4 = lambda g: g.reshape(g.shape[0], 4, g.shape[1] // 4).transpose(1, 0, 2)
        wgrad[l]["w_in"], wgrad[l]["mla_w_uq"], wgrad[l]["mla_w_ukv"] = to4(d_in), to4(d_uq), to4(d_ukv)
        dmod[l] = jnp.stack([acc_m1[0:2], acc_m1[2:4], acc_r1[0:2], acc_m2[0:2], acc_m2[2:4], acc_r2[0:2]])
        rpb_vjp = jax.vjp(lambda r: _na_bias(r, S), w["na_rpb"][l])[1]
        small["na_rpb"][l] = rpb_vjp(dbias)[0]
        small["swa_sink"][l] = dsink[:, :, 0].reshape(SWA_KV_HEADS, -1, BQ).sum(axis=-1).reshape(-1)
        small["gqa_q_norm"][l], small["gqa_k_norm"][l] = acc_p[0, :LANE], acc_p[1, :LANE]
        small["mla_q_norm"][l], small["mla_kv_norm"][l] = acc_p[2], acc_p[3, :LANE]
        small["ln1_g"][l], small["ln1_b"][l] = acc_r1[2], acc_r1[3]
        small["ln2_g"][l], small["ln2_b"][l] = acc_r2[2], acc_r2[3]
        small["ffn_conv_w"][l], small["ffn_conv_b"][l] = acc_f[0:3], acc_f[3]
    grad_x = dx[:S][None]

    dmod_x = jnp.stack([dmod[l][:, 0].reshape(-1) for l in range(L)])
    dmod_c = jnp.stack([dmod[l][:, 1].reshape(-1) for l in range(L)])
    small_names = tuple(small)
    bucket = [dmod_x, dmod_c] + [jnp.stack(small[n]) for n in small_names]
    (b8,) = _allgather8([_pack(bucket)], name="gather_small")
    tot = _unpack(_sum_lead(b8, F32, name="sum_small"), bucket)
    dmod_x_all = b8.reshape(8, -1)[:, :dmod_x.size].reshape(8, L, 6 * D)
    dmod_c_tot = tot[1]
    g_small = dict(zip(small_names, tot[2:]))
    g_small["b_ada"] = tot[0] + dmod_c_tot
    g_small["ffn_conv_w"] = lax.dynamic_slice_in_dim(g_small["ffn_conv_w"], chip * (F // 4), F // 4, axis=2)

    dmod16 = jnp.concatenate([dmod_x_all, jnp.broadcast_to(dmod_c_tot[None], (8, L, 6 * D))], axis=0) * row_keep[:, :, None]
    dmod16 = lax.dynamic_slice_in_dim(dmod16, chip * n_ada, n_ada, axis=2)
    g_ada, dsc = None, None
    for l in range(L):
        g_ada = _mm(sc, dmod16[:, l], mode="tn", exact=True, stack=(L, l, g_ada), name="ada_dw")
        dsc = _mm(dmod16[:, l], w["w_ada"][l], mode="nt", add=dsc, name="ada_dx" if dsc is None else "ada_dx_add")
    (dsc8,) = _allgather8([dsc[8:16]], name="gather_dsc")
    dsc4 = dsc8[::2, 0]
    g_small["c_ctx"] = (((dsc4[0] + dsc4[1]) + dsc4[2]) + dsc4[3]) * _silu_grad(w["c_ctx"])

    add_halves("rest", 0, _run_rides([_Exchange(halves_of("rest", 0))], name="rs_core_exchange")[0])
    landed[("rest", 0)] = _run_rides([_Scatter(parts[("rest", 0)])], name="rs_chip_scatter")[0]
    sums = {}
    for l in range(L):
        for group, names in groups.items():
            for n, p, got in zip(names, parts[(group, l)], landed[(group, l)]):
                sums[n] = _sum_parts(p, got, chip, ac, (L, l, sums.get(n)), name="rs_chip_sum")
    joined = _join_halves([sums[n] for n in _BIG], name="rs_join")
    g_big = {n: j.reshape(L, 2 * j.shape[2], j.shape[3]) for n, j in zip(_BIG, joined)}
    g_big["w_ada"] = g_ada

    grad, delta, new_m, new_v = {}, {}, {}, {}
    for n in _BIG + ("w_ada",):
        shp = w[n].shape
        flat = lambda a: a.reshape(shp[0] * shp[1], shp[2])
        d_, m_, v_ = _adam(flat(w[n]), flat(g_big[n]), flat(m_in[n]), flat(v_in[n]), name="adam")
        grad[n], delta[n], new_m[n], new_v[n] = g_big[n], d_.reshape(shp), m_.reshape(shp), v_.reshape(shp)
    like = [w[n] for n in _SMALL]
    packed = [_pack([src[n].reshape(w[n].shape) for n in _SMALL]) for src in (w, g_small, m_in, v_in)]
    d_s, m_s, v_s = _adam(*packed, name="adam_small")
    for n, g_, d_, m_, v_ in zip(_SMALL, _unpack(packed[1], like), _unpack(d_s, like), _unpack(m_s, like), _unpack(v_s, like)):
        grad[n], delta[n], new_m[n], new_v[n] = g_, d_, m_, v_

    return (loss, grad_x, *[grad[n] for n in _WEIGHTS], *[delta[n] for n in _WEIGHTS],
            *[new_m[n] for n in _WEIGHTS], *[new_v[n] for n in _WEIGHTS])


def kernel(x, c, ctx, c_ctx, w_ada, b_ada, w_in, na_rpb, swa_sink, mla_q_norm, mla_kv_norm, mla_w_uq, mla_w_ukv, gqa_q_norm, gqa_k_norm, w_out, ln1_g, ln1_b, ffn_w_gate, ffn_w_up, ffn_conv_w, ffn_conv_b, ffn_w_down, ln2_g, ln2_b, loss_target, m_c_ctx, m_w_ada, m_b_ada, m_w_in, m_na_rpb, m_swa_sink, m_mla_q_norm, m_mla_kv_norm, m_mla_w_uq, m_mla_w_ukv, m_gqa_q_norm, m_gqa_k_norm, m_w_out, m_ln1_g, m_ln1_b, m_ffn_w_gate, m_ffn_w_up, m_ffn_conv_w, m_ffn_conv_b, m_ffn_w_down, m_ln2_g, m_ln2_b, v_c_ctx, v_w_ada, v_b_ada, v_w_in, v_na_rpb, v_swa_sink, v_mla_q_norm, v_mla_kv_norm, v_mla_w_uq, v_mla_w_ukv, v_gqa_q_norm, v_gqa_k_norm, v_w_out, v_ln1_g, v_ln1_b, v_ffn_w_gate, v_ffn_w_up, v_ffn_conv_w, v_ffn_conv_b, v_ffn_w_down, v_ln2_g, v_ln2_b):
    args = locals()
    w = {n: args[n] for n in _WEIGHTS}
    m_in = {n: args["m_" + n] for n in _WEIGHTS}
    v_in = {n: args["v_" + n] for n in _WEIGHTS}
    return _train_step(x, c, ctx, loss_target, w, m_in, v_in)
```

```python
import functools
import math

import numpy as np
import jax
import jax.numpy as jnp
from jax import lax
from jax.experimental import pallas as pl
from jax.experimental.pallas import tpu as pltpu

F32 = jnp.float32
BF16 = jnp.bfloat16
MESH = pl.DeviceIdType.MESH

GRID_W = 64
HEAD_DIM = 128
NA_HEADS, NA_WIN_R, NA_WIN_C = 4, 8, 16
SWA_HEADS, SWA_KV_HEADS, SWA_WINDOW = 4, 2, 128
MLA_HEADS, MLA_Q_LORA, MLA_KV_LORA, MLA_NOPE, MLA_ROPE, MLA_V = 4, 384, 128, 128, 64, 128
GQA_HEADS, GQA_KV_HEADS = 4, 2
ROPE_THETA = 10000.0
EPS = 1e-6
NEG = -1e30
DEPTH = 2
DEEPNORM_ALPHA = (2 * DEPTH) ** 0.25
ADAM_LR, ADAM_B1, ADAM_B2, ADAM_EPS, ADAM_WD, ADAM_STEP = 0.001, 0.9, 0.999, 1e-08, 0.01, 10

LANE = 128
V7X_VMEM_BYTES = 64 * 1024 * 1024
VMEM_LIMIT = 56 * 1024 * 1024
MM_VMEM_BUDGET = 40 * 1024 * 1024
EW_VMEM_BUDGET = 28 * 1024 * 1024
BQ = 128

CB_NA_Q, CB_NA_K, CB_NA_V = 0, 4, 8
CB_SWA_Q, CB_SWA_K, CB_SWA_V = 12, 16, 18
CB_CQ, CB_CKV = 20, 23
CB_GQA_Q, CB_GQA_K, CB_GQA_V = 24, 28, 30
CB_KPE = 32
PCOLS = 33 * LANE
IN_COLS = 4160


def _cparams(sem=None, **kw):
    return pltpu.CompilerParams(dimension_semantics=sem, vmem_limit_bytes=VMEM_LIMIT, **kw)


def _pick(n, target, mult=LANE):
    best = None
    for d in range(mult, min(n, target) + 1, mult):
        if n % d == 0:
            best = d
    return n if best is None else best


def _mm(a, b, *, mode="nn", out_dtype=F32, a_off=0, a_k=None, tm=1408, tn=1408, tk=2816, exact=False, add=None,
        stack=None, split4=False, rows=None, rides=(), name):
    if mode == "tn":
        K, M = a.shape
        K2, N = b.shape
    elif mode == "nn":
        M, K = a.shape
        K2, N = b.shape
    else:
        M, K = a.shape
        N, K2 = b.shape
    if a_k is not None:
        K = a_k
    if rows is not None:
        if mode == "tn":
            assert rows <= min(K, K2)
            K = K2 = rows
        else:
            assert rows <= M
            M = rows
    assert K == K2, (a.shape, b.shape, mode)
    m_mult = LANE if mode == "tn" else 16
    n_cols = N // 4 if split4 else N
    bm, bn, bk = _pick(M, tm, m_mult), _pick(n_cols, tn), _pick(K, tk)
    sa, sb, so = a.dtype.itemsize, b.dtype.itemsize, jnp.dtype(out_dtype).itemsize

    def vmem_estimate():
        acc = bm * bn * 4 if K // bk > 1 else 0
        return 2 * (bm * bk * sa + bk * bn * sb) + acc + 2 * bm * bn * so + (2 * bm * bn * 4 if add is not None else 0)

    while vmem_estimate() > MM_VMEM_BUDGET:
        if bm >= bn and _pick(M, bm - 1, m_mult) < bm:
            bm = _pick(M, bm - 1, m_mult)
        elif _pick(n_cols, bn - 1) < bn:
            bn = _pick(n_cols, bn - 1)
        else:
            assert _pick(K, bk - 1) < bk, "no tiling fits VMEM"
            bk = _pick(K, bk - 1)
    assert a_off % bk == 0
    koff = a_off // bk
    nk = K // bk
    if mode == "tn":
        a_spec = pl.BlockSpec((bk, bm), lambda i, j, k: (k, i))
        b_spec = pl.BlockSpec((bk, bn), lambda i, j, k: (k, j))
        dims = (((0,), (0,)), ((), ()))
    elif mode == "nn":
        a_spec = pl.BlockSpec((bm, bk), lambda i, j, k: (i, k + koff))
        b_spec = pl.BlockSpec((bk, bn), lambda i, j, k: (k, j))
        dims = (((1,), (0,)), ((), ()))
    else:
        a_spec = pl.BlockSpec((bm, bk), lambda i, j, k: (i, k + koff))
        b_spec = pl.BlockSpec((bn, bk), lambda i, j, k: (j, k))
        dims = (((1,), (1,)), ((), ()))

    operands = [a, b]
    in_specs = [a_spec, b_spec]
    if add is not None:
        operands.append(add)
        in_specs.append(pl.BlockSpec((bm, bn), lambda i, j, k: (i, j)))
    aliases = {}
    if stack is None:
        out_spec = pl.BlockSpec((bm, bn), lambda i, j, k: (i, j))
        out_shape = jax.ShapeDtypeStruct((M, N), out_dtype)
    else:
        n_layers, layer, buf = stack
        if split4:
            nb = N // 4 // bn
            assert N % (4 * bn) == 0
            out_spec = pl.BlockSpec((None, None, bm, bn), lambda i, j, k: (layer, j // nb, i, j % nb))
            out_shape = jax.ShapeDtypeStruct((n_layers, 4, M, N // 4), out_dtype)
        else:
            out_spec = pl.BlockSpec((None, bm, bn), lambda i, j, k: (layer, i, j))
            out_shape = jax.ShapeDtypeStruct((n_layers, M, N), out_dtype)
        if buf is not None:
            aliases = {len(operands): 0}
            operands.append(buf)
            in_specs.append(pl.BlockSpec(memory_space=pl.ANY))
    has_add, has_buf = add is not None, bool(aliases)
    n_ride = sum(len(r.arrays) for r in rides)
    aliases.update(_ride_aliases(rides, len(operands), 1))
    grid = (M // bm, N // bn, nk)

    def body(*refs):
        a_ref, b_ref = refs[:2]
        add_ref = refs[2] if has_add else None
        base = 2 + has_add + has_buf
        o_ref = refs[base + n_ride]
        scratch = refs[base + 2 * n_ride + 1:]
        if rides:
            ride_groups = _ride_split(rides, refs[base:base + n_ride], refs[base + n_ride + 1:base + 2 * n_ride + 1],
                                      scratch[1 if nk > 1 else 0:])
            steps = [pl.program_id(d) for d in range(3)]
            pl.when((steps[0] == 0) & (steps[1] == 0) & (steps[2] == 0))(lambda: _ride_start(rides, ride_groups))
        compute(a_ref, b_ref, add_ref, o_ref, scratch[0] if nk > 1 else None)
        if rides:
            pl.when((steps[0] == grid[0] - 1) & (steps[1] == grid[1] - 1) & (steps[2] == grid[2] - 1))(
                lambda: _ride_wait(rides, ride_groups))

    def compute(a_ref, b_ref, add_ref, o_ref, acc_ref):
        if exact:
            prod = lax.dot_general(a_ref[...].astype(F32), b_ref[...].astype(F32), dims,
                                   precision=lax.Precision.HIGHEST, preferred_element_type=F32)
        else:
            prod = lax.dot_general(a_ref[...].astype(BF16), b_ref[...].astype(BF16), dims, preferred_element_type=F32)

        def finish(res):
            if has_add:
                res = res + add_ref[...].astype(F32)
            o_ref[...] = res.astype(o_ref.dtype)

        if nk == 1:
            finish(prod)
            return
        k = pl.program_id(2)

        @pl.when(k == 0)
        def _():
            acc_ref[...] = prod

        @pl.when((k > 0) & (k < nk - 1))
        def _():
            acc_ref[...] += prod

        @pl.when(k == nk - 1)
        def _():
            finish(acc_ref[...] + prod)

    outs = pl.pallas_call(
        body, name=name, grid=grid,
        in_specs=in_specs + [_ANY] * n_ride, out_specs=[out_spec] + [_ANY] * n_ride,
        out_shape=[out_shape] + [s for r in rides for s in r.out_shapes()],
        scratch_shapes=([pltpu.VMEM((bm, bn), F32)] if nk > 1 else []) + _ride_scratch(rides),
        input_output_aliases=aliases,
        compiler_params=_cparams(("arbitrary",) * 3 if rides else ("parallel", "parallel", "arbitrary")),
    )(*operands, *[a for r in rides for a in r.arrays])
    if not rides:
        return outs[0]
    return outs[0], _ride_outputs(rides, outs[1:])


def _row_block(T, S):
    return _pick(math.gcd(T, S), 256, 16)


def _ln_stats(x):
    mu = jnp.mean(x, axis=-1, keepdims=True)
    xc = x - mu
    var = jnp.mean(xc * xc, axis=-1, keepdims=True)
    rstd = lax.rsqrt(var + EPS)
    return xc * rstd, rstd


def _ln_bwd(dxhat, xhat, rstd):
    m1 = jnp.mean(dxhat, axis=-1, keepdims=True)
    m2 = jnp.mean(dxhat * xhat, axis=-1, keepdims=True)
    return rstd * (dxhat - m1 - xhat * m2)


def _sel(ref, is_ctx):
    return jnp.where(is_ctx, ref[1:2, :], ref[0:1, :])


def _mod_fwd(x, shift, scale, S, *, rows=None, name):
    T, D = (x.shape[0] if rows is None else rows), x.shape[1]
    bt = _row_block(T, S)

    def body(x_ref, sh_ref, sc_ref, o_ref):
        is_ctx = pl.program_id(0) * bt >= S
        xhat, _ = _ln_stats(x_ref[...])
        o_ref[...] = (xhat * (1.0 + _sel(sc_ref, is_ctx)) + _sel(sh_ref, is_ctx)).astype(o_ref.dtype)

    return pl.pallas_call(
        body, name=name, grid=(T // bt,),
        in_specs=[pl.BlockSpec((bt, D), lambda i: (i, 0)), pl.BlockSpec((2, D), lambda i: (0, 0)),
                  pl.BlockSpec((2, D), lambda i: (0, 0))],
        out_specs=pl.BlockSpec((bt, D), lambda i: (i, 0)),
        out_shape=jax.ShapeDtypeStruct((T, D), BF16),
        compiler_params=_cparams(("parallel",)),
    )(x, shift, scale)


def _acc_groups(acc_ref, row, val, is_ctx):
    f = jnp.where(is_ctx, 1.0, 0.0).astype(F32)
    acc_ref[row:row + 1, :] += val * (1.0 - f)
    acc_ref[row + 1:row + 2, :] += val * f


def _mod_bwd(x, dh, scale, dx_in, S, *, name):
    T, D = dh.shape
    bt = _row_block(T, S)
    in_blocks = dx_in.shape[0] // bt

    def body(x_ref, dh_ref, sc_ref, dxi_ref, dx_ref, acc_ref):
        i = pl.program_id(0)
        is_ctx = i * bt >= S

        @pl.when(i == 0)
        def _():
            acc_ref[...] = jnp.zeros_like(acc_ref)

        xhat, rstd = _ln_stats(x_ref[...])
        dh = dh_ref[...].astype(F32)
        dxhat = dh * (1.0 + _sel(sc_ref, is_ctx))
        dxi = dxi_ref[...] if in_blocks * bt == T else jnp.where(i < in_blocks, dxi_ref[...], 0.0)
        dx_ref[...] = dxi + _ln_bwd(dxhat, xhat, rstd)
        _acc_groups(acc_ref, 0, jnp.sum(dh, axis=0, keepdims=True), is_ctx)
        _acc_groups(acc_ref, 2, jnp.sum(dh * xhat, axis=0, keepdims=True), is_ctx)

    return pl.pallas_call(
        body, name=name, grid=(T // bt,),
        in_specs=[pl.BlockSpec((bt, D), lambda i: (i, 0)), pl.BlockSpec((bt, D), lambda i: (i, 0)),
                  pl.BlockSpec((2, D), lambda i: (0, 0)),
                  pl.BlockSpec((bt, D), lambda i: (jnp.minimum(i, in_blocks - 1), 0))],
        out_specs=[pl.BlockSpec((bt, D), lambda i: (i, 0)), pl.BlockSpec((8, D), lambda i: (0, 0))],
        out_shape=[jax.ShapeDtypeStruct((T, D), F32), jax.ShapeDtypeStruct((8, D), F32)],
        compiler_params=_cparams(("arbitrary",)),
    )(x, dh, scale, dx_in)


def _res_fwd(x, z, gate, lg, lb, S, *, name):
    T, D = z.shape
    bt = _row_block(T, S)

    def body(x_ref, z_ref, g_ref, lg_ref, lb_ref, o_ref):
        is_ctx = pl.program_id(0) * bt >= S
        u = DEEPNORM_ALPHA * x_ref[...] + _sel(g_ref, is_ctx) * z_ref[...]
        uhat, _ = _ln_stats(u)
        o_ref[...] = uhat * lg_ref[...] + lb_ref[...]

    row = pl.BlockSpec((bt, D), lambda i: (i, 0))
    return pl.pallas_call(
        body, name=name, grid=(T // bt,),
        in_specs=[row, row, pl.BlockSpec((2, D), lambda i: (0, 0)), pl.BlockSpec((1, D), lambda i: (0, 0)),
                  pl.BlockSpec((1, D), lambda i: (0, 0))],
        out_specs=row,
        out_shape=jax.ShapeDtypeStruct((T, D), F32),
        compiler_params=_cparams(("parallel",)),
    )(x, z, gate, lg, lb)


def _res_bwd(x, z, gate, lg, dy, S, *, name):
    T, D = z.shape
    bt = _row_block(T, S)

    def body(x_ref, z_ref, g_ref, lg_ref, dy_ref, dx_ref, dz_ref, acc_ref):
        i = pl.program_id(0)
        is_ctx = i * bt >= S

        @pl.when(i == 0)
        def _():
            acc_ref[...] = jnp.zeros_like(acc_ref)

        gate_v = _sel(g_ref, is_ctx)
        zv = z_ref[...]
        u = DEEPNORM_ALPHA * x_ref[...] + gate_v * zv
        uhat, rstd = _ln_stats(u)
        dyv = dy_ref[...]
        du = _ln_bwd(dyv * lg_ref[...], uhat, rstd)
        dx_ref[...] = DEEPNORM_ALPHA * du
        dz_ref[...] = (gate_v * du).astype(dz_ref.dtype)
        _acc_groups(acc_ref, 0, jnp.sum(du * zv, axis=0, keepdims=True), is_ctx)
        acc_ref[2:3, :] += jnp.sum(dyv * uhat, axis=0, keepdims=True)
        acc_ref[3:4, :] += jnp.sum(dyv, axis=0, keepdims=True)

    row = pl.BlockSpec((bt, D), lambda i: (i, 0))
    return pl.pallas_call(
        body, name=name, grid=(T // bt,),
        in_specs=[row, row, pl.BlockSpec((2, D), lambda i: (0, 0)), pl.BlockSpec((1, D), lambda i: (0, 0)), row],
        out_specs=[row, row, pl.BlockSpec((8, D), lambda i: (0, 0))],
        out_shape=[jax.ShapeDtypeStruct((T, D), F32), jax.ShapeDtypeStruct((T, D), BF16),
                   jax.ShapeDtypeStruct((8, D), F32)],
        compiler_params=_cparams(("arbitrary",)),
    )(x, z, gate, lg, dy)


def _loss_fwd_bwd(y, target, S, *, name):
    T, D = y.shape
    bt = _row_block(T, S)
    n_lat = S // bt

    def body(y_ref, t_ref, dy_ref, l_ref):
        i = pl.program_id(0)

        @pl.when(i == 0)
        def _():
            l_ref[...] = jnp.zeros_like(l_ref)

        keep = jnp.where(i * bt >= S, 0.0, 1.0).astype(F32)
        err = (y_ref[...] - t_ref[...]) * keep
        dy_ref[...] = err * (1.0 / D)
        l_ref[...] += jnp.sum(err * err) * (0.5 / D)

    return pl.pallas_call(
        body, name=name, grid=(T // bt,),
        in_specs=[pl.BlockSpec((bt, D), lambda i: (i, 0)),
                  pl.BlockSpec((bt, D), lambda i: (jnp.minimum(i, n_lat - 1), 0))],
        out_specs=[pl.BlockSpec((bt, D), lambda i: (i, 0)), pl.BlockSpec((8, LANE), lambda i: (0, 0))],
        out_shape=[jax.ShapeDtypeStruct((T, D), F32), jax.ShapeDtypeStruct((8, LANE), F32)],
        compiler_params=_cparams(("arbitrary",)),
    )(y, target)


def _rope_tables(S, C, dim):
    half = dim // 4
    t = jnp.arange(S)
    row = (t // GRID_W).astype(F32)
    col = (t % GRID_W).astype(F32)
    inv = ROPE_THETA ** (-jnp.arange(half, dtype=F32) / half)
    ar, ac = row[:, None] * inv[None, :], col[:, None] * inv[None, :]
    cos = jnp.concatenate([jnp.cos(ar), jnp.cos(ar), jnp.cos(ac), jnp.cos(ac)], axis=1)
    ss = jnp.concatenate([-jnp.sin(ar), jnp.sin(ar), -jnp.sin(ac), jnp.sin(ac)], axis=1)
    cos = jnp.pad(cos, ((0, C), (0, LANE - dim)), constant_values=1.0)
    ss = jnp.pad(ss, ((0, C), (0, LANE - dim)))
    return cos, ss


def _rope(x, cos, ss, half):
    lane = lax.broadcasted_iota(jnp.int32, x.shape, 1)
    first = (lane % (2 * half)) < half
    partner = jnp.where(first, pltpu.roll(x, LANE - half, 1), pltpu.roll(x, half, 1))
    return x * cos + partner * ss


def _rms(x):
    r = lax.rsqrt(jnp.mean(x * x, axis=-1, keepdims=True) + EPS)
    return x * r, r


_CAST_BLOCKS = tuple(range(0, 12)) + (18, 19, 30, 31)
_ROPE_BLOCKS = tuple(range(12, 18))
_GQA_Q_BLOCKS = tuple(range(24, 28))
_GQA_K_BLOCKS = (28, 29)


def _prep_fwd(p, tabs, gq, gk, mq, mkv, S, *, name):
    T = p.shape[0]
    bt = _row_block(T, S)
    cA, sA, cP, sP = tabs

    def body(p_ref, cA_ref, sA_ref, cP_ref, sP_ref, gq_ref, gk_ref, mq_ref, mkv_ref, o_ref):
        def blk(b):
            return p_ref[:, b * LANE:(b + 1) * LANE]

        def put(b, val):
            o_ref[:, b * LANE:(b + 1) * LANE] = val.astype(o_ref.dtype)

        cA_v, sA_v = cA_ref[...], sA_ref[...]
        for b in _CAST_BLOCKS:
            put(b, blk(b))
        for b in _ROPE_BLOCKS:
            put(b, _rope(blk(b), cA_v, sA_v, 32))
        for b in _GQA_Q_BLOCKS:
            put(b, _rope(_rms(blk(b))[0] * gq_ref[...], cA_v, sA_v, 32))
        for b in _GQA_K_BLOCKS:
            put(b, _rope(_rms(blk(b))[0] * gk_ref[...], cA_v, sA_v, 32))
        put(CB_KPE, _rope(blk(CB_KPE), cP_ref[...], sP_ref[...], 16))
        cq = p_ref[:, CB_CQ * LANE:CB_CKV * LANE]
        o_ref[:, CB_CQ * LANE:CB_CKV * LANE] = (_rms(cq)[0] * mq_ref[...]).astype(o_ref.dtype)
        put(CB_CKV, _rms(blk(CB_CKV))[0] * mkv_ref[...])

    row128 = pl.BlockSpec((bt, LANE), lambda i: (i, 0))
    vec = lambda n: pl.BlockSpec((1, n), lambda i: (0, 0))
    return pl.pallas_call(
        body, name=name, grid=(T // bt,),
        in_specs=[pl.BlockSpec((bt, PCOLS), lambda i: (i, 0)), row128, row128, row128, row128,
                  vec(LANE), vec(LANE), vec(MLA_Q_LORA), vec(LANE)],
        out_specs=pl.BlockSpec((bt, PCOLS), lambda i: (i, 0)),
        out_shape=jax.ShapeDtypeStruct((T, PCOLS), BF16),
        compiler_params=_cparams(("parallel",)),
    )(p, cA, sA, cP, sP, gq, gk, mq, mkv)


def _prep_bwd(p, grads, tabs, gq, gk, mq, mkv, S, *, name):
    T = p.shape[0]
    bt = _row_block(T, S)
    cA, sA, cP, sP = tabs
    arrays = []
    where = {}
    for key, (arr, cb) in grads.items():
        idx = next((n for n, a in enumerate(arrays) if a is arr), None)
        if idx is None:
            arrays.append(arr)
            idx = len(arrays) - 1
        where[key] = (idx, cb)
    ng = len(arrays)

    def body(*refs):
        p_ref, cA_ref, sA_ref, cP_ref, sP_ref, gq_ref, gk_ref, mq_ref, mkv_ref = refs[:9]
        g_refs = refs[9:9 + ng]
        o_ref, acc_ref = refs[9 + ng:]
        i = pl.program_id(0)

        @pl.when(i == 0)
        def _():
            acc_ref[...] = jnp.zeros_like(acc_ref)

        def blk(b):
            return p_ref[:, b * LANE:(b + 1) * LANE]

        def grad(b, width=LANE):
            idx, cb = where[b]
            return g_refs[idx][:, cb * LANE:cb * LANE + width].astype(F32)

        def put(b, val):
            o_ref[:, b * LANE:(b + 1) * LANE] = val.astype(o_ref.dtype)

        def rms_bwd(x, dy, g, row, width):
            n, r = _rms(x)
            acc_ref[row:row + 1, 0:width] += jnp.sum(dy * n, axis=0, keepdims=True)
            dn = dy * g
            return r * (dn - n * jnp.mean(dn * n, axis=-1, keepdims=True))

        cA_v, sA_v = cA_ref[...], sA_ref[...]
        for b in _CAST_BLOCKS:
            put(b, grad(b))
        for b in _ROPE_BLOCKS:
            put(b, _rope(grad(b), cA_v, -sA_v, 32))
        for b in _GQA_Q_BLOCKS:
            put(b, rms_bwd(blk(b), _rope(grad(b), cA_v, -sA_v, 32), gq_ref[...], 0, LANE))
        for b in _GQA_K_BLOCKS:
            put(b, rms_bwd(blk(b), _rope(grad(b), cA_v, -sA_v, 32), gk_ref[...], 1, LANE))
        put(CB_KPE, _rope(grad(CB_KPE), cP_ref[...], -sP_ref[...], 16))
        dcq = rms_bwd(p_ref[:, CB_CQ * LANE:CB_CKV * LANE], grad(CB_CQ, MLA_Q_LORA), mq_ref[...], 2, MLA_Q_LORA)
        o_ref[:, CB_CQ * LANE:CB_CKV * LANE] = dcq.astype(o_ref.dtype)
        put(CB_CKV, rms_bwd(blk(CB_CKV), grad(CB_CKV), mkv_ref[...], 3, LANE))

    row128 = pl.BlockSpec((bt, LANE), lambda i: (i, 0))
    vec = lambda n: pl.BlockSpec((1, n), lambda i: (0, 0))
    g_specs = [pl.BlockSpec((bt, a.shape[1]), lambda i: (i, 0)) for a in arrays]
    return pl.pallas_call(
        body, name=name, grid=(T // bt,),
        in_specs=[pl.BlockSpec((bt, PCOLS), lambda i: (i, 0)), row128, row128, row128, row128,
                  vec(LANE), vec(LANE), vec(MLA_Q_LORA), vec(LANE)] + g_specs,
        out_specs=[pl.BlockSpec((bt, PCOLS), lambda i: (i, 0)), pl.BlockSpec((8, MLA_Q_LORA), lambda i: (0, 0))],
        out_shape=[jax.ShapeDtypeStruct((T, PCOLS), BF16), jax.ShapeDtypeStruct((8, MLA_Q_LORA), F32)],
        compiler_params=_cparams(("arbitrary",)),
    )(p, cA, sA, cP, sP, gq, gk, mq, mkv, *arrays)


def _pe_rope(qm, tabs, S, sign, out_dtype, *, name):
    T, N = qm.shape
    bt = _row_block(T, S)
    cP, sP = tabs[2], tabs[3]

    def body(x_ref, c_ref, s_ref, o_ref):
        for b in range(MLA_HEADS):
            o_ref[:, b * LANE:(b + 1) * LANE] = x_ref[:, b * LANE:(b + 1) * LANE].astype(o_ref.dtype)
        for b in range(MLA_HEADS, 2 * MLA_HEADS):
            x = x_ref[:, b * LANE:(b + 1) * LANE].astype(F32)
            o_ref[:, b * LANE:(b + 1) * LANE] = _rope(x, c_ref[...], sign * s_ref[...], 16).astype(o_ref.dtype)

    row128 = pl.BlockSpec((bt, LANE), lambda i: (i, 0))
    return pl.pallas_call(
        body, name=name, grid=(T // bt,),
        in_specs=[pl.BlockSpec((bt, N), lambda i: (i, 0)), row128, row128],
        out_specs=pl.BlockSpec((bt, N), lambda i: (i, 0)),
        out_shape=jax.ShapeDtypeStruct((T, N), out_dtype),
        compiler_params=_cparams(("parallel",)),
    )(qm, cP, sP)


def _dot_nt(a, b):
    return lax.dot_general(a, b, (((1,), (1,)), ((), ())), preferred_element_type=F32)


def _dot_tn(a, b):
    return lax.dot_general(a, b, (((0,), (0,)), ((), ())), preferred_element_type=F32)


def _dot(a, b):
    return jnp.dot(a, b, preferred_element_type=F32)


def _window_fns(band, S, n_var):
    n_lat = S // BQ
    if band is None:
        return None
    reach, span = band

    def fns(j):
        start = jnp.clip(j - reach, 0, n_lat - span)
        return start, jnp.clip(j - start, 0, n_var - 1)

    return fns


class _AttnCfg:
    def __init__(self, *, Hkv, G, S, C, band, scale, n_var=0, bias_per_head=False, has_sink=False, two=False, bq=BQ,
                 ctx_queries=True):
        self.Hkv, self.G, self.S, self.C, self.band, self.scale = Hkv, G, S, C, band, scale
        self.n_var, self.bias_per_head, self.has_sink, self.two = n_var, bias_per_head, has_sink, two
        self.W = S if band is None else band[1] * BQ
        self.T = S + C
        self.bq, self.ctx_queries = bq, ctx_queries
        assert band is None or bq == BQ
        assert S % bq == 0 and C % bq == 0


def _attn_probs(cfg, j, q_ref, k_ref, q2_ref, k2_ref, bias_ref, sink_ref):
    G, S, C, W = cfg.G, cfg.S, cfg.C, cfg.W
    is_ctx = j * cfg.bq >= S
    if cfg.band is None:
        off, var = 0, 0
    else:
        start, var = _window_fns(cfg.band, S, cfg.n_var)(j)
        off = pl.multiple_of(start * BQ, BQ)
    qt = q_ref[...]
    qs = jnp.concatenate([qt[:, g * LANE:(g + 1) * LANE] for g in range(G)], axis=0) if G > 1 else qt
    kw = k_ref[pl.ds(off, W), :]
    kc = k_ref[pl.ds(S, C), :]
    s_w = _dot_nt(qs, kw)
    s_c = _dot_nt(qs, kc)
    q2s = k2w = k2c = None
    if cfg.two:
        q2s = q2_ref[...]
        k2w = k2_ref[pl.ds(off, W), :]
        k2c = k2_ref[pl.ds(S, C), :]
        s_w = s_w + _dot_nt(q2s, k2w)
        s_c = s_c + _dot_nt(q2s, k2c)
    operands = (off, var, qs, kw, kc, q2s, k2w, k2c)
    if not cfg.n_var and not cfg.has_sink:
        if cfg.ctx_queries:
            s_w = jnp.where(is_ctx, NEG, s_w)
        m = jnp.maximum(jnp.max(s_w, axis=-1, keepdims=True), jnp.max(s_c, axis=-1, keepdims=True))
        c2 = cfg.scale * math.log2(math.e)
        e_w = jnp.exp2((s_w - m) * c2)
        e_c = jnp.exp2((s_c - m) * c2)
        inv = 1.0 / (jnp.sum(e_w, axis=-1, keepdims=True) + jnp.sum(e_c, axis=-1, keepdims=True))
        return e_w * inv, e_c * inv, None, operands
    s_w = s_w * cfg.scale
    s_c = s_c * cfg.scale
    if cfg.n_var:
        b = bias_ref[0, pl.ds(var, 1)][0]
        s_w = s_w + (jnp.concatenate([b] * G, axis=0) if G > 1 else b)
    if cfg.ctx_queries:
        s_w = jnp.where(is_ctx, NEG, s_w)
    m = jnp.maximum(jnp.max(s_w, axis=-1, keepdims=True), jnp.max(s_c, axis=-1, keepdims=True))
    if cfg.has_sink:
        sink = sink_ref[0][:, 0:1]
        m = jnp.maximum(m, sink)
    e_w = jnp.exp(s_w - m)
    e_c = jnp.exp(s_c - m)
    l = jnp.sum(e_w, axis=-1, keepdims=True) + jnp.sum(e_c, axis=-1, keepdims=True)
    p_s = None
    if cfg.has_sink:
        e_s = jnp.exp(sink - m)
        l = l + e_s
    inv = 1.0 / l
    if cfg.has_sink:
        p_s = e_s * inv
    return e_w * inv, e_c * inv, p_s, operands


def _attn_specs(cfg, q_cb, k_cb, v_cb, q2_cb, k2_cb):
    G, T, bq = cfg.G, cfg.T, cfg.bq
    specs = [pl.BlockSpec((bq, G * LANE), lambda h, j: (j, q_cb // G + h)),
             pl.BlockSpec((T, LANE), lambda h, j: (0, k_cb + h)),
             pl.BlockSpec((T, LANE), lambda h, j: (0, v_cb + h))]
    if cfg.two:
        specs += [pl.BlockSpec((bq, LANE), lambda h, j: (j, q2_cb + h)),
                  pl.BlockSpec((T, LANE), lambda h, j: (0, k2_cb))]
    if cfg.n_var:
        if cfg.bias_per_head:
            specs.append(pl.BlockSpec((1, cfg.n_var, BQ, cfg.W), lambda h, j: (h, 0, 0, 0)))
        else:
            specs.append(pl.BlockSpec((1, cfg.n_var, BQ, cfg.W), lambda h, j: (0, 0, 0, 0)))
    if cfg.has_sink:
        specs.append(pl.BlockSpec((1, G * BQ, LANE), lambda h, j: (h, 0, 0)))
    return specs


def _attn_unpack(cfg, refs):
    refs = list(refs)
    q_ref, k_ref, v_ref = refs[:3]
    n = 3
    q2_ref = k2_ref = bias_ref = sink_ref = None
    if cfg.two:
        q2_ref, k2_ref = refs[n:n + 2]
        n += 2
    if cfg.n_var:
        bias_ref = refs[n]
        n += 1
    if cfg.has_sink:
        sink_ref = refs[n]
        n += 1
    return (q_ref, k_ref, v_ref, q2_ref, k2_ref, bias_ref, sink_ref), refs[n:]


def _attn_fwd(cfg, q, q_cb, k, k_cb, v, v_cb, *, q2=None, q2_cb=0, k2=None, k2_cb=0, bias=None, sink=None, rides=(), name):
    G, T, S, C, W = cfg.G, cfg.T, cfg.S, cfg.C, cfg.W
    assert q_cb % G == 0
    operands = [q, k, v] + ([q2, k2] if cfg.two else []) + ([bias] if cfg.n_var else []) + ([sink] if cfg.has_sink else [])

    bq = cfg.bq
    n_ride = sum(len(r.arrays) for r in rides)
    n_q = T // bq

    def body(*refs):
        (q_ref, k_ref, v_ref, q2_ref, k2_ref, bias_ref, sink_ref), rest = _attn_unpack(cfg, refs)
        o_ref = rest[n_ride]
        ride_groups = _ride_split(rides, rest[:n_ride], rest[n_ride + 1:2 * n_ride + 1], rest[2 * n_ride + 1:])
        h = pl.program_id(0)
        j = pl.program_id(1)
        if rides:
            pl.when((h == 0) & (j == 0))(lambda: _ride_start(rides, ride_groups))

        def block():
            p_w, p_c, _, (off, _, _, _, _, _, _, _) = _attn_probs(cfg, j, q_ref, k_ref, q2_ref, k2_ref, bias_ref, sink_ref)
            o = _dot(p_w.astype(BF16), v_ref[pl.ds(off, W), :]) + _dot(p_c.astype(BF16), v_ref[pl.ds(S, C), :])
            for g in range(G):
                o_ref[:, g * LANE:(g + 1) * LANE] = o[g * bq:(g + 1) * bq].astype(o_ref.dtype)

        if cfg.ctx_queries:
            block()
        else:
            pl.when(j * bq < S)(block)

            @pl.when(j * bq >= S)
            def _():
                o_ref[...] = jnp.zeros_like(o_ref)

        if rides:
            pl.when((h == cfg.Hkv - 1) & (j == n_q - 1))(lambda: _ride_wait(rides, ride_groups))

    outs = pl.pallas_call(
        body, name=name, grid=(cfg.Hkv, n_q),
        in_specs=_attn_specs(cfg, q_cb, k_cb, v_cb, q2_cb, k2_cb) + [_ANY] * n_ride,
        out_specs=[pl.BlockSpec((bq, G * LANE), lambda h, j: (j, h))] + [_ANY] * n_ride,
        out_shape=[jax.ShapeDtypeStruct((T, cfg.Hkv * G * LANE), BF16)] + [s for r in rides for s in r.out_shapes()],
        scratch_shapes=_ride_scratch(rides),
        input_output_aliases=_ride_aliases(rides, len(operands), 1),
        compiler_params=_cparams(("arbitrary", "arbitrary") if rides else ("parallel", "parallel")),
    )(*operands, *[a for r in rides for a in r.arrays])
    if not rides:
        return outs[0]
    return outs[0], _ride_outputs(rides, outs[1:])


def _attn_bwd(cfg, q, q_cb, k, k_cb, v, v_cb, do, do_cb, *, q2=None, q2_cb=0, k2=None, k2_cb=0, bias=None, sink=None,
              want_dbias=False, dq_dtype=F32, rides=(), name):
    G, T, S, C, W, Hkv = cfg.G, cfg.T, cfg.S, cfg.C, cfg.W, cfg.Hkv
    assert q_cb % G == 0 and do_cb % G == 0 and not (want_dbias and G > 1)
    operands = [q, k, v] + ([q2, k2] if cfg.two else []) + ([bias] if cfg.n_var else []) + ([sink] if cfg.has_sink else [])
    operands.append(do)
    in_specs = _attn_specs(cfg, q_cb, k_cb, v_cb, q2_cb, k2_cb)
    bq = cfg.bq
    do_blocks = do.shape[0] // bq
    assert do.shape[0] == T or (do.shape[0] == S and not cfg.ctx_queries)
    in_specs.append(pl.BlockSpec((bq, G * LANE), lambda h, j: (jnp.minimum(j, do_blocks - 1), do_cb // G + h)))

    out_specs = [pl.BlockSpec((bq, G * LANE), lambda h, j: (j, h)),
                 pl.BlockSpec((T, LANE), lambda h, j: (0, h)),
                 pl.BlockSpec((T, LANE), lambda h, j: (0, h))]
    out_shape = [jax.ShapeDtypeStruct((T, Hkv * G * LANE), dq_dtype),
                 jax.ShapeDtypeStruct((T, Hkv * LANE), F32),
                 jax.ShapeDtypeStruct((T, Hkv * LANE), F32)]
    if cfg.two:
        out_specs += [pl.BlockSpec((bq, LANE), lambda h, j: (j, h)), pl.BlockSpec((T, LANE), lambda h, j: (0, 0))]
        out_shape += [jax.ShapeDtypeStruct((T, Hkv * LANE), dq_dtype), jax.ShapeDtypeStruct((T, LANE), F32)]
    if want_dbias:
        out_specs.append(pl.BlockSpec((1, cfg.n_var, BQ, W), lambda h, j: (h, 0, 0, 0)))
        out_shape.append(jax.ShapeDtypeStruct((Hkv, cfg.n_var, BQ, W), F32))
    if cfg.has_sink:
        out_specs.append(pl.BlockSpec((1, G * BQ, LANE), lambda h, j: (h, 0, 0)))
        out_shape.append(jax.ShapeDtypeStruct((Hkv, G * BQ, LANE), F32))

    n_ride = sum(len(r.arrays) for r in rides)
    operands += [a for r in rides for a in r.arrays]
    in_specs += [_ANY] * n_ride
    out_specs += [_ANY] * n_ride
    out_shape += [s for r in rides for s in r.out_shapes()]
    n_q = T // bq

    def body(*refs):
        (q_ref, k_ref, v_ref, q2_ref, k2_ref, bias_ref, sink_ref), rest = _attn_unpack(cfg, refs)
        do_ref, ride_in = rest[0], rest[1:1 + n_ride]
        dq_ref, dk_ref, dv_ref = rest[1 + n_ride:4 + n_ride]
        rest = rest[4 + n_ride:]
        dq2_ref = dk2_ref = dbias_ref = dsink_ref = None
        if cfg.two:
            dq2_ref, dk2_ref = rest[:2]
            rest = rest[2:]
        if want_dbias:
            dbias_ref = rest[0]
            rest = rest[1:]
        if cfg.has_sink:
            dsink_ref = rest[0]
            rest = rest[1:]
        ride_groups = _ride_split(rides, ride_in, rest[:n_ride], rest[n_ride:])
        h = pl.program_id(0)
        j = pl.program_id(1)
        if rides:
            pl.when((h == 0) & (j == 0))(lambda: _ride_start(rides, ride_groups))

        @pl.when(j == 0)
        def _():
            dk_ref[...] = jnp.zeros_like(dk_ref)
            dv_ref[...] = jnp.zeros_like(dv_ref)
            if want_dbias:
                dbias_ref[...] = jnp.zeros_like(dbias_ref)
            if cfg.has_sink:
                dsink_ref[...] = jnp.zeros_like(dsink_ref)

        if cfg.two:
            @pl.when((j == 0) & (h == 0))
            def _():
                dk2_ref[...] = jnp.zeros_like(dk2_ref)

        def block():
            p_w, p_c, p_s, (off, var, qs, kw, kc, q2s, k2w, k2c) = _attn_probs(
                cfg, j, q_ref, k_ref, q2_ref, k2_ref, bias_ref, sink_ref)
            dot_ = do_ref[...]
            dos = jnp.concatenate([dot_[:, g * LANE:(g + 1) * LANE] for g in range(G)], axis=0) if G > 1 else dot_
            dos = dos.astype(BF16)
            vw = v_ref[pl.ds(off, W), :]
            vc = v_ref[pl.ds(S, C), :]
            dp_w = _dot_nt(dos, vw)
            dp_c = _dot_nt(dos, vc)
            delta = jnp.sum(p_w * dp_w, axis=-1, keepdims=True) + jnp.sum(p_c * dp_c, axis=-1, keepdims=True)
            ds_w = p_w * (dp_w - delta)
            ds_c = p_c * (dp_c - delta)
            if want_dbias:
                dbias_ref[0, pl.ds(var, 1)] += ds_w[None]
            if cfg.has_sink:
                dsink_ref[0] += jnp.broadcast_to(-(p_s * delta), (G * bq, LANE))
            dsw = (ds_w * cfg.scale).astype(BF16)
            dsc = (ds_c * cfg.scale).astype(BF16)
            dq = _dot(dsw, kw) + _dot(dsc, kc)
            for g in range(G):
                dq_ref[:, g * LANE:(g + 1) * LANE] = dq[g * bq:(g + 1) * bq].astype(dq_ref.dtype)
            dk_ref[pl.ds(off, W), :] += _dot_tn(dsw, qs)
            dk_ref[pl.ds(S, C), :] += _dot_tn(dsc, qs)
            dv_ref[pl.ds(off, W), :] += _dot_tn(p_w.astype(BF16), dos)
            dv_ref[pl.ds(S, C), :] += _dot_tn(p_c.astype(BF16), dos)
            if cfg.two:
                dq2_ref[...] = (_dot(dsw, k2w) + _dot(dsc, k2c)).astype(dq2_ref.dtype)
                dk2_ref[pl.ds(off, W), :] += _dot_tn(dsw, q2s)
                dk2_ref[pl.ds(S, C), :] += _dot_tn(dsc, q2s)

        if cfg.ctx_queries:
            block()
        else:
            pl.when(j * bq < S)(block)

            @pl.when(j * bq >= S)
            def _():
                dq_ref[...] = jnp.zeros_like(dq_ref)
                if cfg.two:
                    dq2_ref[...] = jnp.zeros_like(dq2_ref)

        if rides:
            pl.when((h == Hkv - 1) & (j == n_q - 1))(lambda: _ride_wait(rides, ride_groups))

    outs = pl.pallas_call(
        body, name=name, grid=(Hkv, n_q),
        in_specs=in_specs, out_specs=out_specs, out_shape=out_shape,
        scratch_shapes=_ride_scratch(rides),
        compiler_params=_cparams(("arbitrary", "arbitrary")),
    )(*operands)
    if not rides:
        return outs
    return list(outs[:len(outs) - n_ride]) + [_ride_outputs(rides, outs[len(outs) - n_ride:])]


def _na_bias(rpb, S):
    H = rpb.shape[0]
    rows = S // GRID_W
    pad_l = GRID_W - 1 - (NA_WIN_C - 1)
    ext = jnp.concatenate([jnp.broadcast_to(rpb[:, :, :1], (H, 2 * NA_WIN_R - 1, pad_l)), rpb,
                           jnp.broadcast_to(rpb[:, :, -1:], (H, 2 * NA_WIN_R - 1, pad_l))], axis=2)
    by_col = jnp.stack([ext[:, :, GRID_W - 1 - qc:2 * GRID_W - 1 - qc] for qc in range(GRID_W)], axis=2)
    cq = np.arange(GRID_W)
    c0 = np.clip(cq - NA_WIN_C // 2, 0, GRID_W - NA_WIN_C)
    col_in = (cq[None, :] >= c0[:, None]) & (cq[None, :] < c0[:, None] + NA_WIN_C)
    n_lat = S // BQ
    neg_tile = jnp.full((H, GRID_W, GRID_W), NEG, F32)
    variants = []
    for v in range(5):
        j = {0: 0, 1: 1, 2: 2, 3: n_lat - 2, 4: n_lat - 1}[v]
        start = int(np.clip(j - 2, 0, n_lat - 5))
        assert j - start == v
        q_rows = []
        for qr in range(2):
            r = 2 * j + qr
            r0 = int(np.clip(r - NA_WIN_R // 2, 0, rows - NA_WIN_R))
            k_tiles = []
            for kr in range(10):
                krow = 2 * start + kr
                if r0 <= krow < r0 + NA_WIN_R:
                    k_tiles.append(jnp.where(col_in[None], by_col[:, krow - r + NA_WIN_R - 1], NEG))
                else:
                    k_tiles.append(neg_tile)
            q_rows.append(jnp.concatenate(k_tiles, axis=2))
        variants.append(jnp.concatenate(q_rows, axis=1))
    return jnp.stack(variants, axis=1)


def _swa_mask(S):
    qq = np.arange(BQ)[:, None]
    kk = np.arange(3 * BQ)[None, :]
    tiles = [np.where(np.abs(kk - v * BQ - qq) <= SWA_WINDOW, 0.0, NEG) for v in range(3)]
    return jnp.asarray(np.stack(tiles)[None], F32)


def _ffn_tiles(T, S, F):
    return _row_block(T, S), _pick(F, 1408)


def _halo_specs(T, bt, bf, col_off):
    n8 = bt // 8
    return [pl.BlockSpec((bt, bf), lambda f, i: (i, f + col_off)),
            pl.BlockSpec((8, bf), lambda f, i: (jnp.maximum(i * n8 - 1, 0), f + col_off)),
            pl.BlockSpec((8, bf), lambda f, i: (jnp.minimum((i + 1) * n8, T // 8 - 1), f + col_off))]


def _neighbours(x, prev8, next8, i, bt, S, T):
    r = lax.broadcasted_iota(jnp.int32, x.shape, 0)
    g0 = i * bt
    first_open = jnp.logical_or(g0 == 0, g0 == S)
    last_open = jnp.logical_or(g0 + bt == S, g0 + bt == T)
    before = jnp.where(r == 0, jnp.where(first_open, 0.0, prev8[7:8, :]), pltpu.roll(x, 1, 0))
    after = jnp.where(r == bt - 1, jnp.where(last_open, 0.0, next8[0:1, :]), pltpu.roll(x, bt - 1, 0))
    return before, after


def _sigmoid(a):
    return 1.0 / (1.0 + jnp.exp(-a))


def _ffn_fwd(gp, u, cw, cb, S, *, name):
    T, F = gp.shape
    bt, bf = _ffn_tiles(T, S, F)

    def body(g_ref, gp_ref, gn_ref, u_ref, w_ref, b_ref, o_ref):
        i = pl.program_id(1)
        g = g_ref[...]
        before, after = _neighbours(g, gp_ref[...], gn_ref[...], i, bt, S, T)
        a = before * w_ref[0:1, :] + g * w_ref[1:2, :] + after * w_ref[2:3, :] + b_ref[...]
        o_ref[...] = (a * _sigmoid(a) * u_ref[...]).astype(o_ref.dtype)

    return pl.pallas_call(
        body, name=name, grid=(F // bf, T // bt),
        in_specs=_halo_specs(T, bt, bf, 0) + [pl.BlockSpec((bt, bf), lambda f, i: (i, f)),
                                              pl.BlockSpec((3, bf), lambda f, i: (0, f)),
                                              pl.BlockSpec((1, bf), lambda f, i: (0, f))],
        out_specs=pl.BlockSpec((bt, bf), lambda f, i: (i, f)),
        out_shape=jax.ShapeDtypeStruct((T, F), BF16),
        compiler_params=_cparams(("parallel", "parallel")),
    )(gp, gp, gp, u, cw, cb)


def _ffn_bwd_act(gp, u, da_out, cw, cb, S, *, name):
    T, F = gp.shape
    bt, bf = _ffn_tiles(T, S, F)

    def body(g_ref, gp_ref, gn_ref, u_ref, d_ref, w_ref, b_ref, da_ref, du_ref, acc_ref):
        i = pl.program_id(1)

        @pl.when(i == 0)
        def _():
            acc_ref[...] = jnp.zeros_like(acc_ref)

        g = g_ref[...]
        before, after = _neighbours(g, gp_ref[...], gn_ref[...], i, bt, S, T)
        a = before * w_ref[0:1, :] + g * w_ref[1:2, :] + after * w_ref[2:3, :] + b_ref[...]
        sig = _sigmoid(a)
        d = d_ref[...]
        du_ref[...] = (d * (a * sig)).astype(du_ref.dtype)
        da = d * u_ref[...] * (sig * (1.0 + a * (1.0 - sig)))
        da_ref[...] = da
        acc_ref[0:1, :] += jnp.sum(da * before, axis=0, keepdims=True)
        acc_ref[1:2, :] += jnp.sum(da * g, axis=0, keepdims=True)
        acc_ref[2:3, :] += jnp.sum(da * after, axis=0, keepdims=True)
        acc_ref[3:4, :] += jnp.sum(da, axis=0, keepdims=True)

    blk = pl.BlockSpec((bt, bf), lambda f, i: (i, f))
    return pl.pallas_call(
        body, name=name, grid=(F // bf, T // bt),
        in_specs=_halo_specs(T, bt, bf, 0) + [blk, blk,
                                              pl.BlockSpec((3, bf), lambda f, i: (0, f)),
                                              pl.BlockSpec((1, bf), lambda f, i: (0, f))],
        out_specs=[blk, blk, pl.BlockSpec((8, bf), lambda f, i: (0, f))],
        out_shape=[jax.ShapeDtypeStruct((T, F), F32), jax.ShapeDtypeStruct((T, F), BF16),
                   jax.ShapeDtypeStruct((8, F), F32)],
        compiler_params=_cparams(("parallel", "arbitrary")),
    )(gp, gp, gp, u, da_out, cw, cb)


def _ffn_bwd_conv(da, cw, S, *, name):
    T, F = da.shape
    bt, bf = _ffn_tiles(T, S, F)

    def body(d_ref, dp_ref, dn_ref, w_ref, o_ref):
        i = pl.program_id(1)
        d = d_ref[...]
        before, after = _neighbours(d, dp_ref[...], dn_ref[...], i, bt, S, T)
        o_ref[...] = (after * w_ref[0:1, :] + d * w_ref[1:2, :] + before * w_ref[2:3, :]).astype(o_ref.dtype)

    return pl.pallas_call(
        body, name=name, grid=(F // bf, T // bt),
        in_specs=_halo_specs(T, bt, bf, 0) + [pl.BlockSpec((3, bf), lambda f, i: (0, f))],
        out_specs=pl.BlockSpec((bt, bf), lambda f, i: (i, f)),
        out_shape=jax.ShapeDtypeStruct((T, F), BF16),
        compiler_params=_cparams(("parallel", "parallel")),
    )(da, da, da, cw)


def _ew_rows(R, N, n_arrays):
    return _pick(R, max(16, EW_VMEM_BUDGET // (8 * n_arrays * N)), 16)


def _adam(w, g, m, v, *, name):
    R, N = w.shape
    br = _ew_rows(R, N, 7)
    bc1 = 1.0 - ADAM_B1 ** ADAM_STEP
    bc2 = 1.0 - ADAM_B2 ** ADAM_STEP

    def body(w_ref, g_ref, m_ref, v_ref, d_ref, mo_ref, vo_ref):
        gv = g_ref[...]
        mn = ADAM_B1 * m_ref[...] + (1.0 - ADAM_B1) * gv
        vn = ADAM_B2 * v_ref[...] + (1.0 - ADAM_B2) * (gv * gv)
        mo_ref[...] = mn
        vo_ref[...] = vn
        d_ref[...] = -ADAM_LR * ((mn / bc1) / (jnp.sqrt(vn / bc2) + ADAM_EPS) + ADAM_WD * w_ref[...])

    blk = pl.BlockSpec((br, N), lambda i: (i, 0))
    shp = jax.ShapeDtypeStruct((R, N), F32)
    return pl.pallas_call(
        body, name=name, grid=(R // br,),
        in_specs=[blk, blk, blk, blk], out_specs=[blk, blk, blk], out_shape=[shp, shp, shp],
        compiler_params=_cparams(("parallel",)),
    )(w, g, m, v)


def _sum_lead(x, out_dtype, *, name):
    n, R, N = x.shape
    br = _ew_rows(R, N, n + 1)

    def body(x_ref, o_ref):
        acc = x_ref[0].astype(F32)
        for k in range(1, n):
            acc = acc + x_ref[k].astype(F32)
        o_ref[...] = acc.astype(o_ref.dtype)

    return pl.pallas_call(
        body, name=name, grid=(R // br,),
        in_specs=[pl.BlockSpec((n, br, N), lambda i: (0, i, 0))],
        out_specs=pl.BlockSpec((br, N), lambda i: (i, 0)),
        out_shape=jax.ShapeDtypeStruct((R, N), out_dtype),
        compiler_params=_cparams(("parallel",)),
    )(x)


def _sum_parts(parts, landed, chip, core, stack, *, name):
    _, R, N = parts.shape
    n_layers, layer, buf = stack
    br = _ew_rows(R, N, 5)

    def body(pos_ref, own_ref, landed_ref, *rest):
        o_ref = rest[-1]
        acc = own_ref[...].astype(F32)
        for k in range(3):
            acc = acc + landed_ref[k].astype(F32)
        o_ref[...] = acc

    operands = [jnp.stack([chip, core]).astype(jnp.int32), parts, landed]
    in_specs = [pl.BlockSpec((None, br, N), lambda i, pos: (pos[0], i, 0)),
                pl.BlockSpec((3, br, N), lambda i, pos: (0, i, 0))]
    aliases = {}
    if buf is not None:
        aliases = {3: 0}
        operands.append(buf)
        in_specs.append(pl.BlockSpec(memory_space=pl.ANY))
    return pl.pallas_call(
        body, name=name,
        grid_spec=pltpu.PrefetchScalarGridSpec(
            num_scalar_prefetch=1, grid=(R // br,), in_specs=in_specs,
            out_specs=pl.BlockSpec((None, None, br, N), lambda i, pos: (layer, pos[1], i, 0))),
        out_shape=jax.ShapeDtypeStruct((n_layers, 2, R, N), F32),
        input_output_aliases=aliases,
        compiler_params=_cparams(("parallel",)),
    )(*operands)


def _place_own(shard, core, slot, *, name):
    R, N = shard.shape
    br = _ew_rows(R // 2, N, 2)
    nb = R // 2 // br

    def body(pos_ref, x_ref, o_ref):
        o_ref[...] = x_ref[...].astype(o_ref.dtype)

    return pl.pallas_call(
        body, name=name,
        grid_spec=pltpu.PrefetchScalarGridSpec(
            num_scalar_prefetch=1, grid=(nb,),
            in_specs=[pl.BlockSpec((br, N), lambda i, pos: (pos[0] * nb + i, 0))],
            out_specs=pl.BlockSpec((None, br, N), lambda i, pos: (pos[1], i, 0))),
        out_shape=jax.ShapeDtypeStruct((8, R // 2, N), BF16),
        compiler_params=_cparams(("parallel",)),
    )(jnp.stack([core, slot]).astype(jnp.int32), shard)


def _add_half(g, r, core, *, name):
    Q, _, R, N = g.shape
    br = _ew_rows(R, N, 3)

    def body(c_ref, g_ref, r_ref, o_ref):
        o_ref[...] = (g_ref[...] + r_ref[...]).astype(o_ref.dtype)

    return pl.pallas_call(
        body, name=name,
        grid_spec=pltpu.PrefetchScalarGridSpec(
            num_scalar_prefetch=1, grid=(Q, R // br),
            in_specs=[pl.BlockSpec((None, None, br, N), lambda q, i, c_ref: (q, c_ref[0], i, 0)),
                      pl.BlockSpec((None, br, N), lambda q, i, c_ref: (q, i, 0))],
            out_specs=pl.BlockSpec((None, br, N), lambda q, i, c_ref: (q, i, 0))),
        out_shape=jax.ShapeDtypeStruct((Q, R, N), BF16),
        compiler_params=_cparams(("parallel", "parallel")),
    )(core.reshape(1).astype(jnp.int32), g, r)


_ANY = pl.BlockSpec(memory_space=pl.ANY)


def _place():
    return lax.axis_index("x"), lax.axis_index("y"), lax.axis_index("c")


def _allgather8(blocks, *, name):
    n = len(blocks)

    def body(*refs):
        xs, outs = refs[:n], refs[n:2 * n]
        send_sems, recv_sems, local_sems = refs[2 * n:]
        x, y, c = _place()
        me, sibling = (x, y, c), (x, y, 1 - c)
        chips = [(1 - x, y), (x, 1 - y), (1 - x, 1 - y)]

        def slot(a, px, py, pc):
            return outs[a].at[4 * px + 2 * py + pc]

        def copy(a, k, block, to, src=None):
            return pltpu.make_async_remote_copy(
                src_ref=slot(a, *block) if src is None else src, dst_ref=slot(a, *block),
                send_sem=send_sems.at[a, k], recv_sem=recv_sems.at[a, k], device_id=to, device_id_type=MESH)

        mine = [pltpu.make_async_copy(xs[a], slot(a, *me), local_sems.at[a]) for a in range(n)]
        for cp in mine:
            cp.start()
        first = []
        for a in range(n):
            first.append(copy(a, 0, me, sibling, src=xs[a]))
            first += [copy(a, 1 + j, me, (*chip, c), src=xs[a]) for j, chip in enumerate(chips)]
        for cp in first:
            cp.start()
        passed = []
        for j, chip in enumerate(chips):
            for a in range(n):
                copy(a, 1 + j, (*chip, c), me).wait_recv()
                fwd = copy(a, 4 + j, (*chip, c), sibling)
                fwd.start()
                passed.append(fwd)
        for a in range(n):
            copy(a, 0, sibling, me).wait_recv()
            for j, chip in enumerate(chips):
                copy(a, 4 + j, (*chip, 1 - c), me).wait_recv()
        for cp in first + passed:
            cp.wait_send()
        for cp in mine:
            cp.wait()

    return pl.pallas_call(
        body, name=name,
        in_specs=[_ANY] * n, out_specs=[_ANY] * n,
        out_shape=[jax.ShapeDtypeStruct((8,) + b.shape, b.dtype) for b in blocks],
        scratch_shapes=[pltpu.SemaphoreType.DMA((n, 7)), pltpu.SemaphoreType.DMA((n, 7)), pltpu.SemaphoreType.DMA((n,))],
    )(*blocks)


class _Exchange:
    n_sems = 1

    def __init__(self, arrays):
        self.arrays = list(arrays)

    def out_shapes(self):
        return [jax.ShapeDtypeStruct(g.shape[:1] + g.shape[2:], g.dtype) for g in self.arrays]

    def copy(self, k, src, dst, sems, landing):
        x, y, c = _place()
        return pltpu.make_async_remote_copy(src_ref=src.at[:, 1 - c], dst_ref=dst, send_sem=sems[0], recv_sem=sems[1],
                                            device_id=(x, y, 1 - c), device_id_type=MESH)

    def copies(self, group, landing):
        xs, outs, send_sems, recv_sems = group
        return [self.copy(k, xs[a], outs[a], (send_sems.at[a, k], recv_sems.at[a, k]), landing)
                for a in range(len(xs)) for k in range(self.n_sems)]


class _Scatter(_Exchange):
    n_sems = 3

    def out_shapes(self):
        return [jax.ShapeDtypeStruct((3,) + p.shape[1:], p.dtype) for p in self.arrays]

    def copy(self, k, src, dst, sems, landing):
        x, y, c = _place()
        px, py = [(1 - x, y), (x, 1 - y), (1 - x, 1 - y)][k]
        return pltpu.make_async_remote_copy(src_ref=src.at[2 * px + py], dst_ref=dst.at[k], send_sem=sems[0], recv_sem=sems[1],
                                            device_id=(px, py, c), device_id_type=MESH)


class _GatherChips(_Exchange):
    n_sems = 3
    in_place = True

    def out_shapes(self):
        return [jax.ShapeDtypeStruct(b.shape, b.dtype) for b in self.arrays]

    def copy(self, k, src, dst, sems, landing):
        x, y, c = _place()
        px, py = [(1 - x, y), (x, 1 - y), (1 - x, 1 - y)][k]
        slot = 4 * px + 2 * py + c if landing else 4 * x + 2 * y + c
        return pltpu.make_async_remote_copy(src_ref=src.at[4 * x + 2 * y + c], dst_ref=dst.at[slot], send_sem=sems[0],
                                            recv_sem=sems[1], device_id=(px, py, c), device_id_type=MESH)


class _GatherCores(_GatherChips):
    n_sems = 4

    def copy(self, k, src, dst, sems, landing):
        x, y, c = _place()
        slot = 2 * k + 1 - c if landing else 2 * k + c
        return pltpu.make_async_remote_copy(src_ref=src.at[2 * k + c], dst_ref=dst.at[slot], send_sem=sems[0],
                                            recv_sem=sems[1], device_id=(x, y, 1 - c), device_id_type=MESH)


def _ride_aliases(rides, first_in, first_out):
    aliases, i = {}, 0
    for r in rides:
        for a in range(len(r.arrays)):
            if getattr(r, "in_place", False):
                aliases[first_in + i + a] = first_out + i + a
        i += len(r.arrays)
    return aliases


def _ride_scratch(rides):
    shapes = []
    for r in rides:
        shapes += [pltpu.SemaphoreType.DMA((len(r.arrays), r.n_sems)), pltpu.SemaphoreType.DMA((len(r.arrays), r.n_sems))]
    return shapes


def _ride_split(rides, in_refs, out_refs, sem_refs):
    groups, i, o = [], 0, 0
    for k, r in enumerate(rides):
        n = len(r.arrays)
        groups.append((in_refs[i:i + n], out_refs[o:o + n], sem_refs[2 * k], sem_refs[2 * k + 1]))
        i, o = i + n, o + n
    return groups


def _ride_start(rides, groups):
    for r, g in zip(rides, groups):
        for cp in r.copies(g, False):
            cp.start()


def _ride_wait(rides, groups):
    for r, g in zip(rides, groups):
        for cp in r.copies(g, True):
            cp.wait_recv()
        for cp in r.copies(g, False):
            cp.wait_send()


def _run_rides(rides, *, name):
    n_in = sum(len(r.arrays) for r in rides)

    def body(*refs):
        groups = _ride_split(rides, refs[:n_in], refs[n_in:2 * n_in], refs[2 * n_in:])
        _ride_start(rides, groups)
        _ride_wait(rides, groups)

    outs = pl.pallas_call(
        body, name=name,
        in_specs=[_ANY] * n_in, out_specs=[_ANY] * n_in,
        out_shape=[s for r in rides for s in r.out_shapes()],
        scratch_shapes=_ride_scratch(rides),
    )(*[a for r in rides for a in r.arrays])
    return _ride_outputs(rides, outs)


def _ride_outputs(rides, outs):
    res, o = [], 0
    for r in rides:
        res.append(list(outs[o:o + len(r.arrays)]))
        o += len(r.arrays)
    return res


def _join_halves(bufs, *, name):
    n = len(bufs)

    def body(*refs):
        xs, outs = refs[:n], refs[n:2 * n]
        send_sems, recv_sems = refs[2 * n:]
        x, y, c = _place()
        sends = [pltpu.make_async_remote_copy(
            src_ref=xs[a].at[:, c], dst_ref=outs[a].at[:, c], send_sem=send_sems.at[a], recv_sem=recv_sems.at[a],
            device_id=(x, y, 1 - c), device_id_type=MESH) for a in range(n)]
        recvs = [pltpu.make_async_remote_copy(
            src_ref=xs[a].at[:, c], dst_ref=outs[a].at[:, 1 - c], send_sem=send_sems.at[a], recv_sem=recv_sems.at[a],
            device_id=(x, y, 1 - c), device_id_type=MESH) for a in range(n)]
        for cp in sends:
            cp.start()
        for cp in recvs:
            cp.wait_recv()
        for cp in sends:
            cp.wait_send()

    return pl.pallas_call(
        body, name=name,
        in_specs=[_ANY] * n, out_specs=[_ANY] * n,
        out_shape=[jax.ShapeDtypeStruct(b.shape, b.dtype) for b in bufs],
        input_output_aliases={a: a for a in range(n)},
        scratch_shapes=[pltpu.SemaphoreType.DMA((n,)), pltpu.SemaphoreType.DMA((n,))],
    )(*bufs)


def _perm_w_in(w):
    pad = jnp.zeros((w.shape[0], PCOLS - IN_COLS), w.dtype)
    return jnp.concatenate([w[:, :3072], w[:, 3136:IN_COLS], w[:, 3072:3136], pad], axis=1)


def _unperm_w_in(g):
    return jnp.concatenate([g[:, :3072], g[:, 4096:IN_COLS], g[:, 3072:4096]], axis=1)


def _perm_w_uq(w):
    w4 = w.reshape(MLA_Q_LORA, MLA_HEADS, MLA_NOPE + MLA_ROPE)
    nope = w4[:, :, :MLA_NOPE].reshape(MLA_Q_LORA, MLA_HEADS * LANE)
    pe = jnp.pad(w4[:, :, MLA_NOPE:], ((0, 0), (0, 0), (0, LANE - MLA_ROPE))).reshape(MLA_Q_LORA, MLA_HEADS * LANE)
    return jnp.concatenate([nope, pe], axis=1)


def _unperm_w_uq(g):
    nope = g[:, :MLA_HEADS * LANE].reshape(MLA_Q_LORA, MLA_HEADS, LANE)
    pe = g[:, MLA_HEADS * LANE:].reshape(MLA_Q_LORA, MLA_HEADS, LANE)[:, :, :MLA_ROPE]
    return jnp.concatenate([nope, pe], axis=2).reshape(MLA_Q_LORA, MLA_HEADS * (MLA_NOPE + MLA_ROPE))


def _perm_w_ukv(w):
    w4 = w.reshape(MLA_KV_LORA, MLA_HEADS, MLA_NOPE + MLA_V)
    return jnp.concatenate([w4[:, :, :MLA_NOPE].reshape(MLA_KV_LORA, -1), w4[:, :, MLA_NOPE:].reshape(MLA_KV_LORA, -1)], axis=1)


def _unperm_w_ukv(g):
    kn = g[:, :MLA_HEADS * LANE].reshape(MLA_KV_LORA, MLA_HEADS, LANE)
    vv = g[:, MLA_HEADS * LANE:].reshape(MLA_KV_LORA, MLA_HEADS, LANE)
    return jnp.concatenate([kn, vv], axis=2).reshape(MLA_KV_LORA, -1)


def _silu(v):
    return v * jax.nn.sigmoid(v)


def _silu_grad(v):
    s = jax.nn.sigmoid(v)
    return s * (1.0 + v * (1.0 - s))


_WEIGHTS = ("c_ctx", "w_ada", "b_ada", "w_in", "na_rpb", "swa_sink", "mla_q_norm", "mla_kv_norm", "mla_w_uq", "mla_w_ukv",
            "gqa_q_norm", "gqa_k_norm", "w_out", "ln1_g", "ln1_b", "ffn_w_gate", "ffn_w_up", "ffn_conv_w", "ffn_conv_b",
            "ffn_w_down", "ln2_g", "ln2_b")
_COL_SHARDED = ("w_in", "mla_w_uq", "mla_w_ukv", "ffn_w_gate", "ffn_w_up")
_ROW_SHARDED = ("w_out", "ffn_w_down")
_BIG = _COL_SHARDED + _ROW_SHARDED
_SMALL = ("c_ctx", "b_ada", "na_rpb", "swa_sink", "mla_q_norm", "mla_kv_norm", "gqa_q_norm", "gqa_k_norm", "ln1_g", "ln1_b",
          "ffn_conv_w", "ffn_conv_b", "ln2_g", "ln2_b")


def _pack(arrays):
    flat = jnp.concatenate([a.reshape(-1) for a in arrays])
    n = flat.shape[0]
    rows = -(-n // (8 * LANE)) * 8
    return jnp.pad(flat, (0, rows * LANE - n)).reshape(rows, LANE)


def _unpack(packed, like):
    flat = packed.reshape(-1)
    out, o = [], 0
    for a in like:
        out.append(flat[o:o + a.size].reshape(a.shape))
        o += a.size
    return out


def _train_step(x, c, ctx, loss_target, w, m_in, v_in):
    L = DEPTH
    S, D = x.shape[1], x.shape[2]
    C = ctx.shape[1]
    T = S + C
    F = w["ffn_conv_b"].shape[1]
    ax, ay, ac = _place()
    chip = 2 * ax + ay
    dev = 2 * chip + ac
    n_ada = w["w_ada"].shape[2]

    gather_groups = {"A": ("w_in", "mla_w_uq", "mla_w_ukv"), "B": ("w_out",), "C": ("ffn_w_gate", "ffn_w_up"), "D": ("ffn_w_down",)}
    full = {n: [None] * L for n in _BIG}
    w_in_p, w_uq_p, w_ukv_p = [None] * L, [None] * L, [None] * L
    half_done = {}

    def chips_step(group, l):
        return _GatherChips([_place_own(w[n][l], ac, dev, name="gather_place") for n in gather_groups[group]])

    def cores_step(group, l):
        return _GatherCores(half_done.pop((group, l)))

    def finish_group(group, l, bufs):
        for n, b in zip(gather_groups[group], bufs):
            r, cols = b.shape[1:]
            if n in _COL_SHARDED:
                full[n][l] = b.reshape(4, 2, r, cols).transpose(1, 2, 0, 3).reshape(2 * r, 4 * cols)
            else:
                full[n][l] = b.reshape(8 * r, cols)
        if group == "A":
            w_in_p[l], w_uq_p[l] = _perm_w_in(full["w_in"][l]), _perm_w_uq(full["mla_w_uq"][l])
            w_ukv_p[l] = _perm_w_ukv(full["mla_w_ukv"][l])

    def with_rides(result, rides):
        return result if rides else (result, [])

    def my_half(a):
        r = a.shape[0] // 2
        return lax.dynamic_slice_in_dim(a, ac * r, r, axis=0).astype(BF16)

    gathered = _allgather8([my_half(w[n][0]) for n in gather_groups["A"]] + [w["ffn_conv_w"]], name="gather_weights")
    finish_group("A", 0, gathered[:-1])
    conv_w = gathered[-1][::2].transpose(1, 2, 0, 3).reshape(L, 3, F)

    (c_all,) = _allgather8([c], name="gather_c")
    c16 = jnp.concatenate([c_all.reshape(8, D), jnp.broadcast_to(w["c_ctx"][None], (8, D))], axis=0)
    row_keep = (jnp.arange(16) <= 8).astype(F32)[:, None]
    sc = _silu(c16) * row_keep
    b_loc = lax.dynamic_slice_in_dim(w["b_ada"], chip * n_ada, n_ada, axis=1)
    mod_loc = jnp.stack([_mm(sc, w["w_ada"][l], name="mod_mm") + b_loc[l][None] for l in range(L)])
    (mod_g,) = _allgather8([mod_loc], name="gather_mod")
    mod_all = mod_g[::2].transpose(1, 2, 0, 3).reshape(L, 16, 4 * n_ada)
    mod_x = lax.dynamic_index_in_dim(mod_all, dev, axis=1, keepdims=False)
    mod_c = mod_all[:, 8]
    mods = [jnp.stack([mod_x[l].reshape(6, D), mod_c[l].reshape(6, D)], axis=1) for l in range(L)]

    tabs = _rope_tables(S, C, HEAD_DIM) + _rope_tables(S, C, MLA_ROPE)
    swa_mask = _swa_mask(S)
    scale = HEAD_DIM ** -0.5
    def attn_cfgs(l):
        cq = l < L - 1
        return (_AttnCfg(Hkv=NA_HEADS, G=1, S=S, C=C, band=(2, 5), scale=scale, n_var=5, bias_per_head=True, ctx_queries=cq),
                _AttnCfg(Hkv=SWA_KV_HEADS, G=SWA_HEADS // SWA_KV_HEADS, S=S, C=C, band=(1, 3), scale=scale, n_var=3,
                         has_sink=True, ctx_queries=cq),
                _AttnCfg(Hkv=MLA_HEADS, G=1, S=S, C=C, band=None, scale=(MLA_NOPE + MLA_ROPE) ** -0.5, two=True,
                         bq=2 * BQ, ctx_queries=cq),
                _AttnCfg(Hkv=GQA_KV_HEADS, G=GQA_HEADS // GQA_KV_HEADS, S=S, C=C, band=None, scale=scale, ctx_queries=cq))

    row = lambda a: a[None, :]

    xt = jnp.concatenate([x[0], ctx[0]], axis=0)
    saved = []
    for l in range(L):
        md = mods[l]
        gq, gk, mq, mkv = row(w["gqa_q_norm"][l]), row(w["gqa_k_norm"][l]), row(w["mla_q_norm"][l]), row(w["mla_kv_norm"][l])
        h1 = _mod_fwd(xt, md[0], md[1], S, name="mod_fwd")
        p = _mm(h1, w_in_p[l], name="in_proj")
        qkv = _prep_fwd(p, tabs, gq, gk, mq, mkv, S, name="prep_fwd")
        qm = _mm(qkv, w_uq_p[l], a_off=CB_CQ * LANE, a_k=MLA_Q_LORA, tk=LANE, name="mla_uq")
        qmb = _pe_rope(qm, tabs, S, 1.0, BF16, name="mla_q_rope")
        kvm = _mm(qkv, w_ukv_p[l], a_off=CB_CKV * LANE, a_k=MLA_KV_LORA, tk=LANE, out_dtype=BF16, name="mla_ukv")
        bias_na = _na_bias(w["na_rpb"][l], S)
        sink = jnp.broadcast_to(jnp.repeat(w["swa_sink"][l].reshape(SWA_KV_HEADS, -1), BQ, axis=1)[:, :, None],
                                (SWA_KV_HEADS, SWA_HEADS // SWA_KV_HEADS * BQ, LANE))
        cfg_na, cfg_swa, cfg_mla, cfg_gqa = attn_cfgs(l)
        rows = T if l < L - 1 else S
        first, more = l == 0, l + 1 < L
        rides = [chips_step("B", l)] if first else []
        oa, got = with_rides(_attn_fwd(cfg_na, qkv, CB_NA_Q, qkv, CB_NA_K, qkv, CB_NA_V, bias=bias_na, rides=rides,
                                       name="na_fwd"), rides)
        if first:
            half_done[("B", l)] = got[0]
        rides = [cores_step("B", l)] if first else []
        ob, got = with_rides(_attn_fwd(cfg_swa, qkv, CB_SWA_Q, qkv, CB_SWA_K, qkv, CB_SWA_V, bias=swa_mask, sink=sink,
                                       rides=rides, name="swa_fwd"), rides)
        if first:
            finish_group("B", l, got[0])
        oc, got = _attn_fwd(cfg_mla, qmb, 0, kvm, 0, kvm, MLA_HEADS, q2=qmb, q2_cb=MLA_HEADS, k2=qkv, k2_cb=CB_KPE,
                            rides=[chips_step("C", l)], name="mla_fwd")
        half_done[("C", l)] = got[0]
        od, got = _attn_fwd(cfg_gqa, qkv, CB_GQA_Q, qkv, CB_GQA_K, qkv, CB_GQA_V,
                            rides=[cores_step("C", l), chips_step("D", l)], name="gqa_fwd")
        finish_group("C", l, got[0])
        half_done[("D", l)] = got[1]
        mix = jnp.concatenate([oa, ob, oc, od], axis=1)
        z1 = _mm(mix, full["w_out"][l], rows=rows, name="out_proj")
        x1 = _res_fwd(xt, z1, md[2], row(w["ln1_g"][l]), row(w["ln1_b"][l]), S, name="res_fwd")
        h2 = _mod_fwd(x1, md[3], md[4], S, name="mod_fwd")
        gp, got = _mm(h2, full["ffn_w_gate"][l], rides=[cores_step("D", l)] + ([chips_step("A", l + 1)] if more else []),
                      name="ffn_in")
        finish_group("D", l, got[0])
        if more:
            half_done[("A", l + 1)] = got[1]
        rides = [cores_step("A", l + 1), chips_step("B", l + 1)] if more else []
        up, got = with_rides(_mm(h2, full["ffn_w_up"][l], rides=rides, name="ffn_in"), rides)
        if more:
            finish_group("A", l + 1, got[0])
            half_done[("B", l + 1)] = got[1]
        act = _ffn_fwd(gp, up, conv_w[l], row(w["ffn_conv_b"][l]), S, name="ffn_mid")
        rides = [cores_step("B", l + 1)] if more else []
        z2, got = with_rides(_mm(act, full["ffn_w_down"][l], rides=rides, name="ffn_out"), rides)
        if more:
            finish_group("B", l + 1, got[0])
        x2 = _res_fwd(x1, z2, md[5], row(w["ln2_g"][l]), row(w["ln2_b"][l]), S, name="res_fwd")
        saved.append(dict(x=xt, h1=h1, p=p, qkv=qkv, qmb=qmb, kvm=kvm, bias_na=bias_na, sink=sink, mix=mix, z1=z1, x1=x1,
                          h2=h2, gp=gp, up=up, act=act, z2=z2, cfgs=(cfg_na, cfg_swa, cfg_mla, cfg_gqa)))
        xt = x2

    dx, loss_part = _loss_fwd_bwd(xt, loss_target[0], S, name="loss")
    loss = lax.psum(loss_part[0, 0], ("x", "y", "c"))

    groups = {"ffn": ("ffn_w_gate", "ffn_w_up", "ffn_w_down"), "rest": ("w_in", "mla_w_uq", "mla_w_ukv", "w_out")}
    wgrad = [dict() for _ in range(L)]
    parts, landed = {}, {}

    def halves_of(group, l):
        return [wgrad[l][n].reshape(4, 2, wgrad[l][n].shape[1] // 2, wgrad[l][n].shape[2]) for n in groups[group]]

    def add_halves(group, l, received):
        parts[(group, l)] = [_add_half(h, r, ac, name="rs_core_add") for h, r in zip(halves_of(group, l), received)]

    small = {n: [None] * L for n in ("na_rpb", "swa_sink", "mla_q_norm", "mla_kv_norm", "gqa_q_norm", "gqa_k_norm",
                                     "ln1_g", "ln1_b", "ffn_conv_w", "ffn_conv_b", "ln2_g", "ln2_b")}
    dmod = [None] * L
    for l in reversed(range(L)):
        sv, md = saved[l], mods[l]
        gq, gk, mq, mkv = row(w["gqa_q_norm"][l]), row(w["gqa_k_norm"][l]), row(w["mla_q_norm"][l]), row(w["mla_kv_norm"][l])
        cb_row = row(w["ffn_conv_b"][l])
        dx1, dz2, acc_r2 = _res_bwd(sv["x1"], sv["z2"], md[5], row(w["ln2_g"][l]), dx, S, name="res_bwd")
        dact = _mm(dz2, full["ffn_w_down"][l], mode="nt", name="ffn_out_dx")
        wgrad[l]["ffn_w_down"] = _mm(sv["act"], dz2, mode="tn", name="ffn_out_dw").reshape(4, F // 4, D)
        da, du, acc_f = _ffn_bwd_act(sv["gp"], sv["up"], dact, conv_w[l], cb_row, S, name="ffn_mid_bwd")
        dg = _ffn_bwd_conv(da, conv_w[l], S, name="ffn_conv_bwd")
        dh2 = _mm(dg, full["ffn_w_gate"][l], mode="nt", name="ffn_in_dx")
        dh2 = _mm(du, full["ffn_w_up"][l], mode="nt", add=dh2, name="ffn_in_dx_add")
        wgrad[l]["ffn_w_gate"] = _mm(sv["h2"], dg, mode="tn", stack=(1, 0, None), split4=True,
                                     name="ffn_in_dw").reshape(4, D, F // 4)
        wgrad[l]["ffn_w_up"] = _mm(sv["h2"], du, mode="tn", stack=(1, 0, None), split4=True,
                                   name="ffn_in_dw").reshape(4, D, F // 4)
        dx1, acc_m2 = _mod_bwd(sv["x1"], dh2, md[4], dx1, S, name="mod_bwd")
        dxa, dz1, acc_r1 = _res_bwd(sv["x"], sv["z1"], md[2], row(w["ln1_g"][l]), dx1, S, name="res_bwd")
        dmix = _mm(dz1, full["w_out"][l], mode="nt", out_dtype=BF16, name="out_proj_dx")
        wgrad[l]["w_out"] = _mm(sv["mix"], dz1, mode="tn", rows=dz1.shape[0], name="out_proj_dw").reshape(4, -1, D)

        qkv, qmb, kvm = sv["qkv"], sv["qmb"], sv["kvm"]
        cfg_na, cfg_swa, cfg_mla, cfg_gqa = sv["cfgs"]
        rest_above = l + 1 < L
        rides = [_Exchange(halves_of("ffn", l))] + ([_Exchange(halves_of("rest", l + 1))] if rest_above else [])
        dq_a, dk_a, dv_a, dbias, received = _attn_bwd(cfg_na, qkv, CB_NA_Q, qkv, CB_NA_K, qkv, CB_NA_V, dmix, 0,
                                                      bias=sv["bias_na"], want_dbias=True, rides=rides, name="na_bwd")
        add_halves("ffn", l, received[0])
        if rest_above:
            add_halves("rest", l + 1, received[1])
        dq_b, dk_b, dv_b, dsink = _attn_bwd(cfg_swa, qkv, CB_SWA_Q, qkv, CB_SWA_K, qkv, CB_SWA_V, dmix, NA_HEADS,
                                            bias=swa_mask, sink=sv["sink"], name="swa_bwd")
        dq_c, dk_c, dv_c, dq2_c, dk2_c, got = _attn_bwd(
            cfg_mla, qmb, 0, kvm, 0, kvm, MLA_HEADS, dmix, NA_HEADS + SWA_HEADS, q2=qmb, q2_cb=MLA_HEADS, k2=qkv,
            k2_cb=CB_KPE, rides=[_Scatter(parts[("ffn", l)])], name="mla_bwd")
        landed[("ffn", l)] = got[0]
        rides = [_Scatter(parts[("rest", l + 1)])] if rest_above else []
        gqa_out = _attn_bwd(cfg_gqa, qkv, CB_GQA_Q, qkv, CB_GQA_K, qkv, CB_GQA_V, dmix,
                            NA_HEADS + SWA_HEADS + MLA_HEADS, rides=rides, name="gqa_bwd")
        dq_d, dk_d, dv_d = gqa_out[:3]
        if rest_above:
            landed[("rest", l + 1)] = gqa_out[3][0]
        dqm = _pe_rope(jnp.concatenate([dq_c, dq2_c], axis=1), tabs, S, -1.0, BF16, name="mla_q_rope_bwd")
        dkvm = jnp.concatenate([dk_c, dv_c], axis=1).astype(BF16)
        dcq = _mm(dqm, w_uq_p[l], mode="nt", name="mla_uq_dx")
        dckv = _mm(dkvm, w_ukv_p[l], mode="nt", name="mla_ukv_dx")
        cqn = qkv[:, CB_CQ * LANE:CB_CKV * LANE]
        ckvn = qkv[:, CB_CKV * LANE:(CB_CKV + 1) * LANE]
        d_uq = _unperm_w_uq(_mm(cqn, dqm, mode="tn", name="mla_uq_dw"))
        d_ukv = _unperm_w_ukv(_mm(ckvn, dkvm, mode="tn", name="mla_ukv_dw"))
        grads = {}
        for h in range(NA_HEADS):
            grads[CB_NA_Q + h], grads[CB_NA_K + h], grads[CB_NA_V + h] = (dq_a, h), (dk_a, h), (dv_a, h)
        for h in range(SWA_HEADS):
            grads[CB_SWA_Q + h] = (dq_b, h)
        for h in range(SWA_KV_HEADS):
            grads[CB_SWA_K + h], grads[CB_SWA_V + h] = (dk_b, h), (dv_b, h)
        for h in range(GQA_HEADS):
            grads[CB_GQA_Q + h] = (dq_d, h)
        for h in range(GQA_KV_HEADS):
            grads[CB_GQA_K + h], grads[CB_GQA_V + h] = (dk_d, h), (dv_d, h)
        grads[CB_KPE], grads[CB_CQ], grads[CB_CKV] = (dk2_c, 0), (dcq, 0), (dckv, 0)
        dp, acc_p = _prep_bwd(sv["p"], grads, tabs, gq, gk, mq, mkv, S, name="prep_bwd")
        dh1 = _mm(dp, w_in_p[l], mode="nt", name="in_proj_dx")
        d_in = _unperm_w_in(_mm(sv["h1"], dp, mode="tn", name="in_proj_dw"))
        dx, acc_m1 = _mod_bwd(sv["x"], dh1, md[1], dxa, S, name="mod_bwd")

        to4 = lambda g: g.reshape(g.shape[0], 4, g.shape[1] // 4).transpose(1, 0, 2)
        wgrad[l]["w_in"], wgrad[l]["mla_w_uq"], wgrad[l]["mla_w_ukv"] = to4(d_in), to4(d_uq), to4(d_ukv)
        dmod[l] = jnp.stack([acc_m1[0:2], acc_m1[2:4], acc_r1[0:2], acc_m2[0:2], acc_m2[2:4], acc_r2[0:2]])
        rpb_vjp = jax.vjp(lambda r: _na_bias(r, S), w["na_rpb"][l])[1]
        small["na_rpb"][l] = rpb_vjp(dbias)[0]
        small["swa_sink"][l] = dsink[:, :, 0].reshape(SWA_KV_HEADS, -1, BQ).sum(axis=-1).reshape(-1)
        small["gqa_q_norm"][l], small["gqa_k_norm"][l] = acc_p[0, :LANE], acc_p[1, :LANE]
        small["mla_q_norm"][l], small["mla_kv_norm"][l] = acc_p[2], acc_p[3, :LANE]
        small["ln1_g"][l], small["ln1_b"][l] = acc_r1[2], acc_r1[3]
        small["ln2_g"][l], small["ln2_b"][l] = acc_r2[2], acc_r2[3]
        small["ffn_conv_w"][l], small["ffn_conv_b"][l] = acc_f[0:3], acc_f[3]
    grad_x = dx[:S][None]

    dmod_x = jnp.stack([dmod[l][:, 0].reshape(-1) for l in range(L)])
    dmod_c = jnp.stack([dmod[l][:, 1].reshape(-1) for l in range(L)])
    small_names = tuple(small)
    bucket = [dmod_x, dmod_c] + [jnp.stack(small[n]) for n in small_names]
    (b8,) = _allgather8([_pack(bucket)], name="gather_small")
    tot = _unpack(_sum_lead(b8, F32, name="sum_small"), bucket)
    dmod_x_all = b8.reshape(8, -1)[:, :dmod_x.size].reshape(8, L, 6 * D)
    dmod_c_tot = tot[1]
    g_small = dict(zip(small_names, tot[2:]))
    g_small["b_ada"] = tot[0] + dmod_c_tot
    g_small["ffn_conv_w"] = lax.dynamic_slice_in_dim(g_small["ffn_conv_w"], chip * (F // 4), F // 4, axis=2)

    dmod16 = jnp.concatenate([dmod_x_all, jnp.broadcast_to(dmod_c_tot[None], (8, L, 6 * D))], axis=0) * row_keep[:, :, None]
    dmod16 = lax.dynamic_slice_in_dim(dmod16, chip * n_ada, n_ada, axis=2)
    g_ada, dsc = None, None
    for l in range(L):
        g_ada = _mm(sc, dmod16[:, l], mode="tn", exact=True, stack=(L, l, g_ada), name="ada_dw")
        dsc = _mm(dmod16[:, l], w["w_ada"][l], mode="nt", add=dsc, name="ada_dx" if dsc is None else "ada_dx_add")
    (dsc8,) = _allgather8([dsc[8:16]], name="gather_dsc")
    dsc4 = dsc8[::2, 0]
    g_small["c_ctx"] = (((dsc4[0] + dsc4[1]) + dsc4[2]) + dsc4[3]) * _silu_grad(w["c_ctx"])

    add_halves("rest", 0, _run_rides([_Exchange(halves_of("rest", 0))], name="rs_core_exchange")[0])
    landed[("rest", 0)] = _run_rides([_Scatter(parts[("rest", 0)])], name="rs_chip_scatter")[0]
    sums = {}
    for l in range(L):
        for group, names in groups.items():
            for n, p, got in zip(names, parts[(group, l)], landed[(group, l)]):
                sums[n] = _sum_parts(p, got, chip, ac, (L, l, sums.get(n)), name="rs_chip_sum")
    joined = _join_halves([sums[n] for n in _BIG], name="rs_join")
    g_big = {n: j.reshape(L, 2 * j.shape[2], j.shape[3]) for n, j in zip(_BIG, joined)}
    g_big["w_ada"] = g_ada

    grad, delta, new_m, new_v = {}, {}, {}, {}
    for n in _BIG + ("w_ada",):
        shp = w[n].shape
        flat = lambda a: a.reshape(shp[0] * shp[1], shp[2])
        d_, m_, v_ = _adam(flat(w[n]), flat(g_big[n]), flat(m_in[n]), flat(v_in[n]), name="adam")
        grad[n], delta[n], new_m[n], new_v[n] = g_big[n], d_.reshape(shp), m_.reshape(shp), v_.reshape(shp)
    like = [w[n] for n in _SMALL]
    packed = [_pack([src[n].reshape(w[n].shape) for n in _SMALL]) for src in (w, g_small, m_in, v_in)]
    d_s, m_s, v_s = _adam(*packed, name="adam_small")
    for n, g_, d_, m_, v_ in zip(_SMALL, _unpack(packed[1], like), _unpack(d_s, like), _unpack(m_s, like), _unpack(v_s, like)):
        grad[n], delta[n], new_m[n], new_v[n] = g_, d_, m_, v_

    return (loss, grad_x, *[grad[n] for n in _WEIGHTS], *[delta[n] for n in _WEIGHTS],
            *[new_m[n] for n in _WEIGHTS], *[new_v[n] for n in _WEIGHTS])


def kernel(x, c, ctx, c_ctx, w_ada, b_ada, w_in, na_rpb, swa_sink, mla_q_norm, mla_kv_norm, mla_w_uq, mla_w_ukv, gqa_q_norm, gqa_k_norm, w_out, ln1_g, ln1_b, ffn_w_gate, ffn_w_up, ffn_conv_w, ffn_conv_b, ffn_w_down, ln2_g, ln2_b, loss_target, m_c_ctx, m_w_ada, m_b_ada, m_w_in, m_na_rpb, m_swa_sink, m_mla_q_norm, m_mla_kv_norm, m_mla_w_uq, m_mla_w_ukv, m_gqa_q_norm, m_gqa_k_norm, m_w_out, m_ln1_g, m_ln1_b, m_ffn_w_gate, m_ffn_w_up, m_ffn_conv_w, m_ffn_conv_b, m_ffn_w_down, m_ln2_g, m_ln2_b, v_c_ctx, v_w_ada, v_b_ada, v_w_in, v_na_rpb, v_swa_sink, v_mla_q_norm, v_mla_kv_norm, v_mla_w_uq, v_mla_w_ukv, v_gqa_q_norm, v_gqa_k_norm, v_w_out, v_ln1_g, v_ln1_b, v_ffn_w_gate, v_ffn_w_up, v_ffn_conv_w, v_ffn_conv_b, v_ffn_w_down, v_ln2_g, v_ln2_b):
    args = locals()
    w = {n: args[n] for n in _WEIGHTS}
    m_in = {n: args["m_" + n] for n in _WEIGHTS}
    v_in = {n: args["v_" + n] for n in _WEIGHTS}
    return _train_step(x, c, ctx, loss_target, w, m_in, v_in)
```

```python
import functools
import math

import numpy as np
import jax
import jax.numpy as jnp
from jax import lax
from jax.experimental import pallas as pl
from jax.experimental.pallas import tpu as pltpu

F32 = jnp.float32
BF16 = jnp.bfloat16
MESH = pl.DeviceIdType.MESH

GRID_W = 64
HEAD_DIM = 128
NA_HEADS, NA_WIN_R, NA_WIN_C = 4, 8, 16
SWA_HEADS, SWA_KV_HEADS, SWA_WINDOW = 4, 2, 128
MLA_HEADS, MLA_Q_LORA, MLA_KV_LORA, MLA_NOPE, MLA_ROPE, MLA_V = 4, 384, 128, 128, 64, 128
GQA_HEADS, GQA_KV_HEADS = 4, 2
ROPE_THETA = 10000.0
EPS = 1e-6
NEG = -1e30
DEPTH = 2
DEEPNORM_ALPHA = (2 * DEPTH) ** 0.25
ADAM_LR, ADAM_B1, ADAM_B2, ADAM_EPS, ADAM_WD, ADAM_STEP = 0.001, 0.9, 0.999, 1e-08, 0.01, 10

LANE = 128
V7X_VMEM_BYTES = 64 * 1024 * 1024
VMEM_LIMIT = 56 * 1024 * 1024
MM_VMEM_BUDGET = 40 * 1024 * 1024
EW_VMEM_BUDGET = 28 * 1024 * 1024
BQ = 128

CB_NA_Q, CB_NA_K, CB_NA_V = 0, 4, 8
CB_SWA_Q, CB_SWA_K, CB_SWA_V = 12, 16, 18
CB_CQ, CB_CKV = 20, 23
CB_GQA_Q, CB_GQA_K, CB_GQA_V = 24, 28, 30
CB_KPE = 32
PCOLS = 33 * LANE
IN_COLS = 4160


def _cparams(sem=None, **kw):
    return pltpu.CompilerParams(dimension_semantics=sem, vmem_limit_bytes=VMEM_LIMIT, **kw)


def _pick(n, target, mult=LANE):
    best = None
    for d in range(mult, min(n, target) + 1, mult):
        if n % d == 0:
            best = d
    return n if best is None else best


def _mm(a, b, *, mode="nn", out_dtype=F32, a_off=0, a_k=None, tm=1408, tn=1408, tk=2816, exact=False, add=None,
        stack=None, split4=False, rows=None, rides=(), b_layer=None, name):
    b_shape = b.shape if b_layer is None else b.shape[1:]
    if mode == "tn":
        K, M = a.shape
        K2, N = b_shape
    elif mode == "nn":
        M, K = a.shape
        K2, N = b_shape
    else:
        M, K = a.shape
        N, K2 = b_shape
    if a_k is not None:
        K = a_k
    if rows is not None:
        if mode == "tn":
            assert rows <= min(K, K2)
            K = K2 = rows
        else:
            assert rows <= M
            M = rows
    assert K == K2, (a.shape, b.shape, mode)
    m_mult = LANE if mode == "tn" else 16
    n_cols = N // 4 if split4 else N
    bm, bn, bk = _pick(M, tm, m_mult), _pick(n_cols, tn), _pick(K, tk)
    sa, sb, so = a.dtype.itemsize, b.dtype.itemsize, jnp.dtype(out_dtype).itemsize

    def vmem_estimate():
        acc = bm * bn * 4 if K // bk > 1 else 0
        return 2 * (bm * bk * sa + bk * bn * sb) + acc + 2 * bm * bn * so + (2 * bm * bn * 4 if add is not None else 0)

    while vmem_estimate() > MM_VMEM_BUDGET:
        if bm >= bn and _pick(M, bm - 1, m_mult) < bm:
            bm = _pick(M, bm - 1, m_mult)
        elif _pick(n_cols, bn - 1) < bn:
            bn = _pick(n_cols, bn - 1)
        else:
            assert _pick(K, bk - 1) < bk, "no tiling fits VMEM"
            bk = _pick(K, bk - 1)
    assert a_off % bk == 0
    koff = a_off // bk
    nk = K // bk
    if mode == "tn":
        a_spec = pl.BlockSpec((bk, bm), lambda i, j, k: (k, i))
        b_spec = pl.BlockSpec((bk, bn), lambda i, j, k: (k, j))
        dims = (((0,), (0,)), ((), ()))
    elif mode == "nn":
        a_spec = pl.BlockSpec((bm, bk), lambda i, j, k: (i, k + koff))
        b_spec = pl.BlockSpec((bk, bn), lambda i, j, k: (k, j))
        dims = (((1,), (0,)), ((), ()))
    else:
        a_spec = pl.BlockSpec((bm, bk), lambda i, j, k: (i, k + koff))
        b_spec = pl.BlockSpec((bn, bk), lambda i, j, k: (j, k))
        dims = (((1,), (1,)), ((), ()))
    if b_layer is not None:
        b_block, b_index = b_spec.block_shape, b_spec.index_map
        b_spec = pl.BlockSpec((None,) + tuple(b_block), lambda i, j, k: (b_layer,) + tuple(b_index(i, j, k)))

    operands = [a, b]
    in_specs = [a_spec, b_spec]
    if add is not None:
        operands.append(add)
        in_specs.append(pl.BlockSpec((bm, bn), lambda i, j, k: (i, j)))
    aliases = {}
    if stack is None:
        out_spec = pl.BlockSpec((bm, bn), lambda i, j, k: (i, j))
        out_shape = jax.ShapeDtypeStruct((M, N), out_dtype)
    else:
        n_layers, layer, buf = stack
        if split4:
            nb = N // 4 // bn
            assert N % (4 * bn) == 0
            out_spec = pl.BlockSpec((None, None, bm, bn), lambda i, j, k: (layer, j // nb, i, j % nb))
            out_shape = jax.ShapeDtypeStruct((n_layers, 4, M, N // 4), out_dtype)
        else:
            out_spec = pl.BlockSpec((None, bm, bn), lambda i, j, k: (layer, i, j))
            out_shape = jax.ShapeDtypeStruct((n_layers, M, N), out_dtype)
        if buf is not None:
            aliases = {len(operands): 0}
            operands.append(buf)
            in_specs.append(pl.BlockSpec(memory_space=pl.ANY))
    has_add, has_buf = add is not None, bool(aliases)
    n_ride = sum(len(r.arrays) for r in rides)
    aliases.update(_ride_aliases(rides, len(operands), 1))
    grid = (M // bm, N // bn, nk)

    def body(*refs):
        a_ref, b_ref = refs[:2]
        add_ref = refs[2] if has_add else None
        base = 2 + has_add + has_buf
        o_ref = refs[base + n_ride]
        scratch = refs[base + 2 * n_ride + 1:]
        if rides:
            ride_groups = _ride_split(rides, refs[base:base + n_ride], refs[base + n_ride + 1:base + 2 * n_ride + 1],
                                      scratch[1 if nk > 1 else 0:])
            steps = [pl.program_id(d) for d in range(3)]
            pl.when((steps[0] == 0) & (steps[1] == 0) & (steps[2] == 0))(lambda: _ride_start(rides, ride_groups))
        compute(a_ref, b_ref, add_ref, o_ref, scratch[0] if nk > 1 else None)
        if rides:
            pl.when((steps[0] == grid[0] - 1) & (steps[1] == grid[1] - 1) & (steps[2] == grid[2] - 1))(
                lambda: _ride_wait(rides, ride_groups))

    def compute(a_ref, b_ref, add_ref, o_ref, acc_ref):
        if exact:
            prod = lax.dot_general(a_ref[...].astype(F32), b_ref[...].astype(F32), dims,
                                   precision=lax.Precision.HIGHEST, preferred_element_type=F32)
        else:
            prod = lax.dot_general(a_ref[...].astype(BF16), b_ref[...].astype(BF16), dims, preferred_element_type=F32)

        def finish(res):
            if has_add:
                res = res + add_ref[...].astype(F32)
            o_ref[...] = res.astype(o_ref.dtype)

        if nk == 1:
            finish(prod)
            return
        k = pl.program_id(2)

        @pl.when(k == 0)
        def _():
            acc_ref[...] = prod

        @pl.when((k > 0) & (k < nk - 1))
        def _():
            acc_ref[...] += prod

        @pl.when(k == nk - 1)
        def _():
            finish(acc_ref[...] + prod)

    outs = pl.pallas_call(
        body, name=name, grid=grid,
        in_specs=in_specs + [_ANY] * n_ride, out_specs=[out_spec] + [_ANY] * n_ride,
        out_shape=[out_shape] + [s for r in rides for s in r.out_shapes()],
        scratch_shapes=([pltpu.VMEM((bm, bn), F32)] if nk > 1 else []) + _ride_scratch(rides),
        input_output_aliases=aliases,
        compiler_params=_cparams(("arbitrary",) * 3 if rides else ("parallel", "parallel", "arbitrary")),
    )(*operands, *[a for r in rides for a in r.arrays])
    if not rides:
        return outs[0]
    return outs[0], _ride_outputs(rides, outs[1:])


def _row_block(T, S):
    return _pick(math.gcd(T, S), 256, 16)


def _ln_stats(x):
    mu = jnp.mean(x, axis=-1, keepdims=True)
    xc = x - mu
    var = jnp.mean(xc * xc, axis=-1, keepdims=True)
    rstd = lax.rsqrt(var + EPS)
    return xc * rstd, rstd


def _ln_bwd(dxhat, xhat, rstd):
    m1 = jnp.mean(dxhat, axis=-1, keepdims=True)
    m2 = jnp.mean(dxhat * xhat, axis=-1, keepdims=True)
    return rstd * (dxhat - m1 - xhat * m2)


def _sel(ref, is_ctx):
    return jnp.where(is_ctx, ref[1:2, :], ref[0:1, :])


def _mod_fwd(x, shift, scale, S, *, rows=None, name):
    T, D = (x.shape[0] if rows is None else rows), x.shape[1]
    bt = _row_block(T, S)

    def body(x_ref, sh_ref, sc_ref, o_ref):
        is_ctx = pl.program_id(0) * bt >= S
        xhat, _ = _ln_stats(x_ref[...])
        o_ref[...] = (xhat * (1.0 + _sel(sc_ref, is_ctx)) + _sel(sh_ref, is_ctx)).astype(o_ref.dtype)

    return pl.pallas_call(
        body, name=name, grid=(T // bt,),
        in_specs=[pl.BlockSpec((bt, D), lambda i: (i, 0)), pl.BlockSpec((2, D), lambda i: (0, 0)),
                  pl.BlockSpec((2, D), lambda i: (0, 0))],
        out_specs=pl.BlockSpec((bt, D), lambda i: (i, 0)),
        out_shape=jax.ShapeDtypeStruct((T, D), BF16),
        compiler_params=_cparams(("parallel",)),
    )(x, shift, scale)


def _acc_groups(acc_ref, row, val, is_ctx):
    f = jnp.where(is_ctx, 1.0, 0.0).astype(F32)
    acc_ref[row:row + 1, :] += val * (1.0 - f)
    acc_ref[row + 1:row + 2, :] += val * f


def _mod_bwd(x, dh, scale, dx_in, S, *, name):
    T, D = dh.shape
    bt = _row_block(T, S)
    in_blocks = dx_in.shape[0] // bt

    def body(x_ref, dh_ref, sc_ref, dxi_ref, dx_ref, acc_ref):
        i = pl.program_id(0)
        is_ctx = i * bt >= S

        @pl.when(i == 0)
        def _():
            acc_ref[...] = jnp.zeros_like(acc_ref)

        xhat, rstd = _ln_stats(x_ref[...])
        dh = dh_ref[...].astype(F32)
        dxhat = dh * (1.0 + _sel(sc_ref, is_ctx))
        dxi = dxi_ref[...] if in_blocks * bt == T else jnp.where(i < in_blocks, dxi_ref[...], 0.0)
        dx_ref[...] = dxi + _ln_bwd(dxhat, xhat, rstd)
        _acc_groups(acc_ref, 0, jnp.sum(dh, axis=0, keepdims=True), is_ctx)
        _acc_groups(acc_ref, 2, jnp.sum(dh * xhat, axis=0, keepdims=True), is_ctx)

    return pl.pallas_call(
        body, name=name, grid=(T // bt,),
        in_specs=[pl.BlockSpec((bt, D), lambda i: (i, 0)), pl.BlockSpec((bt, D), lambda i: (i, 0)),
                  pl.BlockSpec((2, D), lambda i: (0, 0)),
                  pl.BlockSpec((bt, D), lambda i: (jnp.minimum(i, in_blocks - 1), 0))],
        out_specs=[pl.BlockSpec((bt, D), lambda i: (i, 0)), pl.BlockSpec((8, D), lambda i: (0, 0))],
        out_shape=[jax.ShapeDtypeStruct((T, D), F32), jax.ShapeDtypeStruct((8, D), F32)],
        compiler_params=_cparams(("arbitrary",)),
    )(x, dh, scale, dx_in)


def _res_fwd(x, z, gate, lg, lb, S, *, name):
    T, D = z.shape
    bt = _row_block(T, S)

    def body(x_ref, z_ref, g_ref, lg_ref, lb_ref, o_ref):
        is_ctx = pl.program_id(0) * bt >= S
        u = DEEPNORM_ALPHA * x_ref[...] + _sel(g_ref, is_ctx) * z_ref[...]
        uhat, _ = _ln_stats(u)
        o_ref[...] = uhat * lg_ref[...] + lb_ref[...]

    row = pl.BlockSpec((bt, D), lambda i: (i, 0))
    return pl.pallas_call(
        body, name=name, grid=(T // bt,),
        in_specs=[row, row, pl.BlockSpec((2, D), lambda i: (0, 0)), pl.BlockSpec((1, D), lambda i: (0, 0)),
                  pl.BlockSpec((1, D), lambda i: (0, 0))],
        out_specs=row,
        out_shape=jax.ShapeDtypeStruct((T, D), F32),
        compiler_params=_cparams(("parallel",)),
    )(x, z, gate, lg, lb)


def _res_bwd(x, z, gate, lg, dy, S, *, name):
    T, D = z.shape
    bt = _row_block(T, S)

    def body(x_ref, z_ref, g_ref, lg_ref, dy_ref, dx_ref, dz_ref, acc_ref):
        i = pl.program_id(0)
        is_ctx = i * bt >= S

        @pl.when(i == 0)
        def _():
            acc_ref[...] = jnp.zeros_like(acc_ref)

        gate_v = _sel(g_ref, is_ctx)
        zv = z_ref[...]
        u = DEEPNORM_ALPHA * x_ref[...] + gate_v * zv
        uhat, rstd = _ln_stats(u)
        dyv = dy_ref[...]
        du = _ln_bwd(dyv * lg_ref[...], uhat, rstd)
        dx_ref[...] = DEEPNORM_ALPHA * du
        dz_ref[...] = (gate_v * du).astype(dz_ref.dtype)
        _acc_groups(acc_ref, 0, jnp.sum(du * zv, axis=0, keepdims=True), is_ctx)
        acc_ref[2:3, :] += jnp.sum(dyv * uhat, axis=0, keepdims=True)
        acc_ref[3:4, :] += jnp.sum(dyv, axis=0, keepdims=True)

    row = pl.BlockSpec((bt, D), lambda i: (i, 0))
    return pl.pallas_call(
        body, name=name, grid=(T // bt,),
        in_specs=[row, row, pl.BlockSpec((2, D), lambda i: (0, 0)), pl.BlockSpec((1, D), lambda i: (0, 0)), row],
        out_specs=[row, row, pl.BlockSpec((8, D), lambda i: (0, 0))],
        out_shape=[jax.ShapeDtypeStruct((T, D), F32), jax.ShapeDtypeStruct((T, D), BF16),
                   jax.ShapeDtypeStruct((8, D), F32)],
        compiler_params=_cparams(("arbitrary",)),
    )(x, z, gate, lg, dy)


def _loss_fwd_bwd(y, target, S, *, name):
    T, D = y.shape
    bt = _row_block(T, S)
    n_lat = S // bt

    def body(y_ref, t_ref, dy_ref, l_ref):
        i = pl.program_id(0)

        @pl.when(i == 0)
        def _():
            l_ref[...] = jnp.zeros_like(l_ref)

        keep = jnp.where(i * bt >= S, 0.0, 1.0).astype(F32)
        err = (y_ref[...] - t_ref[...]) * keep
        dy_ref[...] = err * (1.0 / D)
        l_ref[...] += jnp.sum(err * err) * (0.5 / D)

    return pl.pallas_call(
        body, name=name, grid=(T // bt,),
        in_specs=[pl.BlockSpec((bt, D), lambda i: (i, 0)),
                  pl.BlockSpec((bt, D), lambda i: (jnp.minimum(i, n_lat - 1), 0))],
        out_specs=[pl.BlockSpec((bt, D), lambda i: (i, 0)), pl.BlockSpec((8, LANE), lambda i: (0, 0))],
        out_shape=[jax.ShapeDtypeStruct((T, D), F32), jax.ShapeDtypeStruct((8, LANE), F32)],
        compiler_params=_cparams(("arbitrary",)),
    )(y, target)


def _rope_tables(S, C, dim):
    half = dim // 4
    t = jnp.arange(S)
    row = (t // GRID_W).astype(F32)
    col = (t % GRID_W).astype(F32)
    inv = ROPE_THETA ** (-jnp.arange(half, dtype=F32) / half)
    ar, ac = row[:, None] * inv[None, :], col[:, None] * inv[None, :]
    cos = jnp.concatenate([jnp.cos(ar), jnp.cos(ar), jnp.cos(ac), jnp.cos(ac)], axis=1)
    ss = jnp.concatenate([-jnp.sin(ar), jnp.sin(ar), -jnp.sin(ac), jnp.sin(ac)], axis=1)
    cos = jnp.pad(cos, ((0, C), (0, LANE - dim)), constant_values=1.0)
    ss = jnp.pad(ss, ((0, C), (0, LANE - dim)))
    return cos, ss


def _rope(x, cos, ss, half):
    lane = lax.broadcasted_iota(jnp.int32, x.shape, 1)
    first = (lane % (2 * half)) < half
    partner = jnp.where(first, pltpu.roll(x, LANE - half, 1), pltpu.roll(x, half, 1))
    return x * cos + partner * ss


def _rms(x):
    r = lax.rsqrt(jnp.mean(x * x, axis=-1, keepdims=True) + EPS)
    return x * r, r


_CAST_BLOCKS = tuple(range(0, 12)) + (18, 19, 30, 31)
_ROPE_BLOCKS = tuple(range(12, 18))
_GQA_Q_BLOCKS = tuple(range(24, 28))
_GQA_K_BLOCKS = (28, 29)


def _prep_fwd(p, tabs, gq, gk, mq, mkv, S, *, name):
    T = p.shape[0]
    bt = _row_block(T, S)
    cA, sA, cP, sP = tabs

    def body(p_ref, cA_ref, sA_ref, cP_ref, sP_ref, gq_ref, gk_ref, mq_ref, mkv_ref, o_ref):
        def blk(b):
            return p_ref[:, b * LANE:(b + 1) * LANE]

        def put(b, val):
            o_ref[:, b * LANE:(b + 1) * LANE] = val.astype(o_ref.dtype)

        cA_v, sA_v = cA_ref[...], sA_ref[...]
        for b in _CAST_BLOCKS:
            put(b, blk(b))
        for b in _ROPE_BLOCKS:
            put(b, _rope(blk(b), cA_v, sA_v, 32))
        for b in _GQA_Q_BLOCKS:
            put(b, _rope(_rms(blk(b))[0] * gq_ref[...], cA_v, sA_v, 32))
        for b in _GQA_K_BLOCKS:
            put(b, _rope(_rms(blk(b))[0] * gk_ref[...], cA_v, sA_v, 32))
        put(CB_KPE, _rope(blk(CB_KPE), cP_ref[...], sP_ref[...], 16))
        cq = p_ref[:, CB_CQ * LANE:CB_CKV * LANE]
        o_ref[:, CB_CQ * LANE:CB_CKV * LANE] = (_rms(cq)[0] * mq_ref[...]).astype(o_ref.dtype)
        put(CB_CKV, _rms(blk(CB_CKV))[0] * mkv_ref[...])

    row128 = pl.BlockSpec((bt, LANE), lambda i: (i, 0))
    vec = lambda n: pl.BlockSpec((1, n), lambda i: (0, 0))
    return pl.pallas_call(
        body, name=name, grid=(T // bt,),
        in_specs=[pl.BlockSpec((bt, PCOLS), lambda i: (i, 0)), row128, row128, row128, row128,
                  vec(LANE), vec(LANE), vec(MLA_Q_LORA), vec(LANE)],
        out_specs=pl.BlockSpec((bt, PCOLS), lambda i: (i, 0)),
        out_shape=jax.ShapeDtypeStruct((T, PCOLS), BF16),
        compiler_params=_cparams(("parallel",)),
    )(p, cA, sA, cP, sP, gq, gk, mq, mkv)


def _prep_bwd(p, grads, tabs, gq, gk, mq, mkv, S, *, name):
    T = p.shape[0]
    bt = _row_block(T, S)
    cA, sA, cP, sP = tabs
    arrays = []
    where = {}
    for key, (arr, cb) in grads.items():
        idx = next((n for n, a in enumerate(arrays) if a is arr), None)
        if idx is None:
            arrays.append(arr)
            idx = len(arrays) - 1
        where[key] = (idx, cb)
    ng = len(arrays)

    def body(*refs):
        p_ref, cA_ref, sA_ref, cP_ref, sP_ref, gq_ref, gk_ref, mq_ref, mkv_ref = refs[:9]
        g_refs = refs[9:9 + ng]
        o_ref, acc_ref = refs[9 + ng:]
        i = pl.program_id(0)

        @pl.when(i == 0)
        def _():
            acc_ref[...] = jnp.zeros_like(acc_ref)

        def blk(b):
            return p_ref[:, b * LANE:(b + 1) * LANE]

        def grad(b, width=LANE):
            idx, cb = where[b]
            return g_refs[idx][:, cb * LANE:cb * LANE + width].astype(F32)

        def put(b, val):
            o_ref[:, b * LANE:(b + 1) * LANE] = val.astype(o_ref.dtype)

        def rms_bwd(x, dy, g, row, width):
            n, r = _rms(x)
            acc_ref[row:row + 1, 0:width] += jnp.sum(dy * n, axis=0, keepdims=True)
            dn = dy * g
            return r * (dn - n * jnp.mean(dn * n, axis=-1, keepdims=True))

        cA_v, sA_v = cA_ref[...], sA_ref[...]
        for b in _CAST_BLOCKS:
            put(b, grad(b))
        for b in _ROPE_BLOCKS:
            put(b, _rope(grad(b), cA_v, -sA_v, 32))
        for b in _GQA_Q_BLOCKS:
            put(b, rms_bwd(blk(b), _rope(grad(b), cA_v, -sA_v, 32), gq_ref[...], 0, LANE))
        for b in _GQA_K_BLOCKS:
            put(b, rms_bwd(blk(b), _rope(grad(b), cA_v, -sA_v, 32), gk_ref[...], 1, LANE))
        put(CB_KPE, _rope(grad(CB_KPE), cP_ref[...], -sP_ref[...], 16))
        dcq = rms_bwd(p_ref[:, CB_CQ * LANE:CB_CKV * LANE], grad(CB_CQ, MLA_Q_LORA), mq_ref[...], 2, MLA_Q_LORA)
        o_ref[:, CB_CQ * LANE:CB_CKV * LANE] = dcq.astype(o_ref.dtype)
        put(CB_CKV, rms_bwd(blk(CB_CKV), grad(CB_CKV), mkv_ref[...], 3, LANE))

    row128 = pl.BlockSpec((bt, LANE), lambda i: (i, 0))
    vec = lambda n: pl.BlockSpec((1, n), lambda i: (0, 0))
    g_specs = [pl.BlockSpec((bt, a.shape[1]), lambda i: (i, 0)) for a in arrays]
    return pl.pallas_call(
        body, name=name, grid=(T // bt,),
        in_specs=[pl.BlockSpec((bt, PCOLS), lambda i: (i, 0)), row128, row128, row128, row128,
                  vec(LANE), vec(LANE), vec(MLA_Q_LORA), vec(LANE)] + g_specs,
        out_specs=[pl.BlockSpec((bt, PCOLS), lambda i: (i, 0)), pl.BlockSpec((8, MLA_Q_LORA), lambda i: (0, 0))],
        out_shape=[jax.ShapeDtypeStruct((T, PCOLS), BF16), jax.ShapeDtypeStruct((8, MLA_Q_LORA), F32)],
        compiler_params=_cparams(("arbitrary",)),
    )(p, cA, sA, cP, sP, gq, gk, mq, mkv, *arrays)


def _pe_rope(qm, tabs, S, sign, out_dtype, *, name):
    T, N = qm.shape
    bt = _row_block(T, S)
    cP, sP = tabs[2], tabs[3]

    def body(x_ref, c_ref, s_ref, o_ref):
        for b in range(MLA_HEADS):
            o_ref[:, b * LANE:(b + 1) * LANE] = x_ref[:, b * LANE:(b + 1) * LANE].astype(o_ref.dtype)
        for b in range(MLA_HEADS, 2 * MLA_HEADS):
            x = x_ref[:, b * LANE:(b + 1) * LANE].astype(F32)
            o_ref[:, b * LANE:(b + 1) * LANE] = _rope(x, c_ref[...], sign * s_ref[...], 16).astype(o_ref.dtype)

    row128 = pl.BlockSpec((bt, LANE), lambda i: (i, 0))
    return pl.pallas_call(
        body, name=name, grid=(T // bt,),
        in_specs=[pl.BlockSpec((bt, N), lambda i: (i, 0)), row128, row128],
        out_specs=pl.BlockSpec((bt, N), lambda i: (i, 0)),
        out_shape=jax.ShapeDtypeStruct((T, N), out_dtype),
        compiler_params=_cparams(("parallel",)),
    )(qm, cP, sP)


def _dot_nt(a, b):
    return lax.dot_general(a, b, (((1,), (1,)), ((), ())), preferred_element_type=F32)


def _dot_tn(a, b):
    return lax.dot_general(a, b, (((0,), (0,)), ((), ())), preferred_element_type=F32)


def _dot(a, b):
    return jnp.dot(a, b, preferred_element_type=F32)


def _window_fns(band, S, n_var):
    n_lat = S // BQ
    if band is None:
        return None
    reach, span = band

    def fns(j):
        start = jnp.clip(j - reach, 0, n_lat - span)
        return start, jnp.clip(j - start, 0, n_var - 1)

    return fns


class _AttnCfg:
    def __init__(self, *, Hkv, G, S, C, band, scale, n_var=0, bias_per_head=False, has_sink=False, two=False, bq=BQ,
                 ctx_queries=True):
        self.Hkv, self.G, self.S, self.C, self.band, self.scale = Hkv, G, S, C, band, scale
        self.n_var, self.bias_per_head, self.has_sink, self.two = n_var, bias_per_head, has_sink, two
        self.W = S if band is None else band[1] * BQ
        self.T = S + C
        self.bq, self.ctx_queries = bq, ctx_queries
        assert band is None or bq == BQ
        assert S % bq == 0 and C % bq == 0


def _attn_probs(cfg, j, q_ref, k_ref, q2_ref, k2_ref, bias_ref, sink_ref):
    G, S, C, W = cfg.G, cfg.S, cfg.C, cfg.W
    is_ctx = j * cfg.bq >= S
    if cfg.band is None:
        off, var = 0, 0
    else:
        start, var = _window_fns(cfg.band, S, cfg.n_var)(j)
        off = pl.multiple_of(start * BQ, BQ)
    qt = q_ref[...]
    qs = jnp.concatenate([qt[:, g * LANE:(g + 1) * LANE] for g in range(G)], axis=0) if G > 1 else qt
    kw = k_ref[pl.ds(off, W), :]
    kc = k_ref[pl.ds(S, C), :]
    s_w = _dot_nt(qs, kw)
    s_c = _dot_nt(qs, kc)
    q2s = k2w = k2c = None
    if cfg.two:
        q2s = q2_ref[...]
        k2w = k2_ref[pl.ds(off, W), :]
        k2c = k2_ref[pl.ds(S, C), :]
        s_w = s_w + _dot_nt(q2s, k2w)
        s_c = s_c + _dot_nt(q2s, k2c)
    operands = (off, var, qs, kw, kc, q2s, k2w, k2c)
    if not cfg.n_var and not cfg.has_sink:
        if cfg.ctx_queries:
            s_w = jnp.where(is_ctx, NEG, s_w)
        m = jnp.maximum(jnp.max(s_w, axis=-1, keepdims=True), jnp.max(s_c, axis=-1, keepdims=True))
        c2 = cfg.scale * math.log2(math.e)
        e_w = jnp.exp2((s_w - m) * c2)
        e_c = jnp.exp2((s_c - m) * c2)
        inv = 1.0 / (jnp.sum(e_w, axis=-1, keepdims=True) + jnp.sum(e_c, axis=-1, keepdims=True))
        return e_w * inv, e_c * inv, None, operands
    s_w = s_w * cfg.scale
    s_c = s_c * cfg.scale
    if cfg.n_var:
        b = bias_ref[0, pl.ds(var, 1)][0]
        s_w = s_w + (jnp.concatenate([b] * G, axis=0) if G > 1 else b)
    if cfg.ctx_queries:
        s_w = jnp.where(is_ctx, NEG, s_w)
    m = jnp.maximum(jnp.max(s_w, axis=-1, keepdims=True), jnp.max(s_c, axis=-1, keepdims=True))
    if cfg.has_sink:
        sink = sink_ref[0][:, 0:1]
        m = jnp.maximum(m, sink)
    e_w = jnp.exp(s_w - m)
    e_c = jnp.exp(s_c - m)
    l = jnp.sum(e_w, axis=-1, keepdims=True) + jnp.sum(e_c, axis=-1, keepdims=True)
    p_s = None
    if cfg.has_sink:
        e_s = jnp.exp(sink - m)
        l = l + e_s
    inv = 1.0 / l
    if cfg.has_sink:
        p_s = e_s * inv
    return e_w * inv, e_c * inv, p_s, operands


def _attn_specs(cfg, q_cb, k_cb, v_cb, q2_cb, k2_cb):
    G, T, bq = cfg.G, cfg.T, cfg.bq
    specs = [pl.BlockSpec((bq, G * LANE), lambda h, j: (j, q_cb // G + h)),
             pl.BlockSpec((T, LANE), lambda h, j: (0, k_cb + h)),
             pl.BlockSpec((T, LANE), lambda h, j: (0, v_cb + h))]
    if cfg.two:
        specs += [pl.BlockSpec((bq, LANE), lambda h, j: (j, q2_cb + h)),
                  pl.BlockSpec((T, LANE), lambda h, j: (0, k2_cb))]
    if cfg.n_var:
        if cfg.bias_per_head:
            specs.append(pl.BlockSpec((1, cfg.n_var, BQ, cfg.W), lambda h, j: (h, 0, 0, 0)))
        else:
            specs.append(pl.BlockSpec((1, cfg.n_var, BQ, cfg.W), lambda h, j: (0, 0, 0, 0)))
    if cfg.has_sink:
        specs.append(pl.BlockSpec((1, G * BQ, LANE), lambda h, j: (h, 0, 0)))
    return specs


def _attn_unpack(cfg, refs):
    refs = list(refs)
    q_ref, k_ref, v_ref = refs[:3]
    n = 3
    q2_ref = k2_ref = bias_ref = sink_ref = None
    if cfg.two:
        q2_ref, k2_ref = refs[n:n + 2]
        n += 2
    if cfg.n_var:
        bias_ref = refs[n]
        n += 1
    if cfg.has_sink:
        sink_ref = refs[n]
        n += 1
    return (q_ref, k_ref, v_ref, q2_ref, k2_ref, bias_ref, sink_ref), refs[n:]


def _attn_fwd(cfg, q, q_cb, k, k_cb, v, v_cb, *, q2=None, q2_cb=0, k2=None, k2_cb=0, bias=None, sink=None, rides=(),
              into=None, name):
    G, T, S, C, W = cfg.G, cfg.T, cfg.S, cfg.C, cfg.W
    assert q_cb % G == 0
    operands = [q, k, v] + ([q2, k2] if cfg.two else []) + ([bias] if cfg.n_var else []) + ([sink] if cfg.has_sink else [])

    bq = cfg.bq
    n_ride = sum(len(r.arrays) for r in rides)
    n_q = T // bq
    has_buf = into is not None and into[0] is not None
    col0 = 0 if into is None else into[1]
    width = cfg.Hkv * G * LANE if into is None else into[2] * LANE
    assert col0 % G == 0

    def body(*refs):
        (q_ref, k_ref, v_ref, q2_ref, k2_ref, bias_ref, sink_ref), rest = _attn_unpack(cfg, refs)
        rest = rest[:n_ride] + rest[n_ride + has_buf:]
        o_ref = rest[n_ride]
        ride_groups = _ride_split(rides, rest[:n_ride], rest[n_ride + 1:2 * n_ride + 1], rest[2 * n_ride + 1:])
        h = pl.program_id(0)
        j = pl.program_id(1)
        if rides:
            pl.when((h == 0) & (j == 0))(lambda: _ride_start(rides, ride_groups))

        def block():
            p_w, p_c, _, (off, _, _, _, _, _, _, _) = _attn_probs(cfg, j, q_ref, k_ref, q2_ref, k2_ref, bias_ref, sink_ref)
            o = _dot(p_w.astype(BF16), v_ref[pl.ds(off, W), :]) + _dot(p_c.astype(BF16), v_ref[pl.ds(S, C), :])
            for g in range(G):
                o_ref[:, g * LANE:(g + 1) * LANE] = o[g * bq:(g + 1) * bq].astype(o_ref.dtype)

        if cfg.ctx_queries:
            block()
        else:
            pl.when(j * bq < S)(block)

            @pl.when(j * bq >= S)
            def _():
                o_ref[...] = jnp.zeros_like(o_ref)

        if rides:
            pl.when((h == cfg.Hkv - 1) & (j == n_q - 1))(lambda: _ride_wait(rides, ride_groups))

    aliases = _ride_aliases(rides, len(operands), 1)
    if has_buf:
        aliases[len(operands) + n_ride] = 0
    outs = pl.pallas_call(
        body, name=name, grid=(cfg.Hkv, n_q),
        in_specs=_attn_specs(cfg, q_cb, k_cb, v_cb, q2_cb, k2_cb) + [_ANY] * (n_ride + has_buf),
        out_specs=[pl.BlockSpec((bq, G * LANE), lambda h, j: (j, col0 // G + h))] + [_ANY] * n_ride,
        out_shape=[jax.ShapeDtypeStruct((T, width), BF16)] + [s for r in rides for s in r.out_shapes()],
        scratch_shapes=_ride_scratch(rides),
        input_output_aliases=aliases,
        compiler_params=_cparams(("arbitrary", "arbitrary") if rides else ("parallel", "parallel")),
    )(*operands, *[a for r in rides for a in r.arrays], *([into[0]] if has_buf else []))
    if not rides:
        return outs[0]
    return outs[0], _ride_outputs(rides, outs[1:])


def _attn_bwd(cfg, q, q_cb, k, k_cb, v, v_cb, do, do_cb, *, q2=None, q2_cb=0, k2=None, k2_cb=0, bias=None, sink=None,
              want_dbias=False, dq_dtype=F32, rides=(), name):
    G, T, S, C, W, Hkv = cfg.G, cfg.T, cfg.S, cfg.C, cfg.W, cfg.Hkv
    assert q_cb % G == 0 and do_cb % G == 0 and not (want_dbias and G > 1)
    operands = [q, k, v] + ([q2, k2] if cfg.two else []) + ([bias] if cfg.n_var else []) + ([sink] if cfg.has_sink else [])
    operands.append(do)
    in_specs = _attn_specs(cfg, q_cb, k_cb, v_cb, q2_cb, k2_cb)
    bq = cfg.bq
    do_blocks = do.shape[0] // bq
    assert do.shape[0] == T or (do.shape[0] == S and not cfg.ctx_queries)
    in_specs.append(pl.BlockSpec((bq, G * LANE), lambda h, j: (jnp.minimum(j, do_blocks - 1), do_cb // G + h)))

    out_specs = [pl.BlockSpec((bq, G * LANE), lambda h, j: (j, h)),
                 pl.BlockSpec((T, LANE), lambda h, j: (0, h)),
                 pl.BlockSpec((T, LANE), lambda h, j: (0, h))]
    out_shape = [jax.ShapeDtypeStruct((T, Hkv * G * LANE), dq_dtype),
                 jax.ShapeDtypeStruct((T, Hkv * LANE), F32),
                 jax.ShapeDtypeStruct((T, Hkv * LANE), F32)]
    if cfg.two:
        out_specs += [pl.BlockSpec((bq, LANE), lambda h, j: (j, h)), pl.BlockSpec((T, LANE), lambda h, j: (0, 0))]
        out_shape += [jax.ShapeDtypeStruct((T, Hkv * LANE), dq_dtype), jax.ShapeDtypeStruct((T, LANE), F32)]
    if want_dbias:
        out_specs.append(pl.BlockSpec((1, cfg.n_var, BQ, W), lambda h, j: (h, 0, 0, 0)))
        out_shape.append(jax.ShapeDtypeStruct((Hkv, cfg.n_var, BQ, W), F32))
    if cfg.has_sink:
        out_specs.append(pl.BlockSpec((1, G * BQ, LANE), lambda h, j: (h, 0, 0)))
        out_shape.append(jax.ShapeDtypeStruct((Hkv, G * BQ, LANE), F32))

    n_ride = sum(len(r.arrays) for r in rides)
    operands += [a for r in rides for a in r.arrays]
    in_specs += [_ANY] * n_ride
    out_specs += [_ANY] * n_ride
    out_shape += [s for r in rides for s in r.out_shapes()]
    n_q = T // bq

    def body(*refs):
        (q_ref, k_ref, v_ref, q2_ref, k2_ref, bias_ref, sink_ref), rest = _attn_unpack(cfg, refs)
        do_ref, ride_in = rest[0], rest[1:1 + n_ride]
        dq_ref, dk_ref, dv_ref = rest[1 + n_ride:4 + n_ride]
        rest = rest[4 + n_ride:]
        dq2_ref = dk2_ref = dbias_ref = dsink_ref = None
        if cfg.two:
            dq2_ref, dk2_ref = rest[:2]
            rest = rest[2:]
        if want_dbias:
            dbias_ref = rest[0]
            rest = rest[1:]
        if cfg.has_sink:
            dsink_ref = rest[0]
            rest = rest[1:]
        ride_groups = _ride_split(rides, ride_in, rest[:n_ride], rest[n_ride:])
        h = pl.program_id(0)
        j = pl.program_id(1)
        if rides:
            pl.when((h == 0) & (j == 0))(lambda: _ride_start(rides, ride_groups))

        @pl.when(j == 0)
        def _():
            dk_ref[...] = jnp.zeros_like(dk_ref)
            dv_ref[...] = jnp.zeros_like(dv_ref)
            if want_dbias:
                dbias_ref[...] = jnp.zeros_like(dbias_ref)
            if cfg.has_sink:
                dsink_ref[...] = jnp.zeros_like(dsink_ref)

        if cfg.two:
            @pl.when((j == 0) & (h == 0))
            def _():
                dk2_ref[...] = jnp.zeros_like(dk2_ref)

        def block():
            p_w, p_c, p_s, (off, var, qs, kw, kc, q2s, k2w, k2c) = _attn_probs(
                cfg, j, q_ref, k_ref, q2_ref, k2_ref, bias_ref, sink_ref)
            dot_ = do_ref[...]
            dos = jnp.concatenate([dot_[:, g * LANE:(g + 1) * LANE] for g in range(G)], axis=0) if G > 1 else dot_
            dos = dos.astype(BF16)
            vw = v_ref[pl.ds(off, W), :]
            vc = v_ref[pl.ds(S, C), :]
            dp_w = _dot_nt(dos, vw)
            dp_c = _dot_nt(dos, vc)
            delta = jnp.sum(p_w * dp_w, axis=-1, keepdims=True) + jnp.sum(p_c * dp_c, axis=-1, keepdims=True)
            ds_w = p_w * (dp_w - delta)
            ds_c = p_c * (dp_c - delta)
            if want_dbias:
                dbias_ref[0, pl.ds(var, 1)] += ds_w[None]
            if cfg.has_sink:
                dsink_ref[0] += jnp.broadcast_to(-(p_s * delta), (G * bq, LANE))
            dsw = (ds_w * cfg.scale).astype(BF16)
            dsc = (ds_c * cfg.scale).astype(BF16)
            dq = _dot(dsw, kw) + _dot(dsc, kc)
            for g in range(G):
                dq_ref[:, g * LANE:(g + 1) * LANE] = dq[g * bq:(g + 1) * bq].astype(dq_ref.dtype)
            dk_ref[pl.ds(off, W), :] += _dot_tn(dsw, qs)
            dk_ref[pl.ds(S, C), :] += _dot_tn(dsc, qs)
            dv_ref[pl.ds(off, W), :] += _dot_tn(p_w.astype(BF16), dos)
            dv_ref[pl.ds(S, C), :] += _dot_tn(p_c.astype(BF16), dos)
            if cfg.two:
                dq2_ref[...] = (_dot(dsw, k2w) + _dot(dsc, k2c)).astype(dq2_ref.dtype)
                dk2_ref[pl.ds(off, W), :] += _dot_tn(dsw, q2s)
                dk2_ref[pl.ds(S, C), :] += _dot_tn(dsc, q2s)

        if cfg.ctx_queries:
            block()
        else:
            pl.when(j * bq < S)(block)

            @pl.when(j * bq >= S)
            def _():
                dq_ref[...] = jnp.zeros_like(dq_ref)
                if cfg.two:
                    dq2_ref[...] = jnp.zeros_like(dq2_ref)

        if rides:
            pl.when((h == Hkv - 1) & (j == n_q - 1))(lambda: _ride_wait(rides, ride_groups))

    outs = pl.pallas_call(
        body, name=name, grid=(Hkv, n_q),
        in_specs=in_specs, out_specs=out_specs, out_shape=out_shape,
        scratch_shapes=_ride_scratch(rides),
        compiler_params=_cparams(("arbitrary", "arbitrary")),
    )(*operands)
    if not rides:
        return outs
    return list(outs[:len(outs) - n_ride]) + [_ride_outputs(rides, outs[len(outs) - n_ride:])]


def _na_bias(rpb, S):
    H = rpb.shape[0]
    rows = S // GRID_W
    pad_l = GRID_W - 1 - (NA_WIN_C - 1)
    ext = jnp.concatenate([jnp.broadcast_to(rpb[:, :, :1], (H, 2 * NA_WIN_R - 1, pad_l)), rpb,
                           jnp.broadcast_to(rpb[:, :, -1:], (H, 2 * NA_WIN_R - 1, pad_l))], axis=2)
    by_col = jnp.stack([ext[:, :, GRID_W - 1 - qc:2 * GRID_W - 1 - qc] for qc in range(GRID_W)], axis=2)
    cq = np.arange(GRID_W)
    c0 = np.clip(cq - NA_WIN_C // 2, 0, GRID_W - NA_WIN_C)
    col_in = (cq[None, :] >= c0[:, None]) & (cq[None, :] < c0[:, None] + NA_WIN_C)
    n_lat = S // BQ
    neg_tile = jnp.full((H, GRID_W, GRID_W), NEG, F32)
    variants = []
    for v in range(5):
        j = {0: 0, 1: 1, 2: 2, 3: n_lat - 2, 4: n_lat - 1}[v]
        start = int(np.clip(j - 2, 0, n_lat - 5))
        assert j - start == v
        q_rows = []
        for qr in range(2):
            r = 2 * j + qr
            r0 = int(np.clip(r - NA_WIN_R // 2, 0, rows - NA_WIN_R))
            k_tiles = []
            for kr in range(10):
                krow = 2 * start + kr
                if r0 <= krow < r0 + NA_WIN_R:
                    k_tiles.append(jnp.where(col_in[None], by_col[:, krow - r + NA_WIN_R - 1], NEG))
                else:
                    k_tiles.append(neg_tile)
            q_rows.append(jnp.concatenate(k_tiles, axis=2))
        variants.append(jnp.concatenate(q_rows, axis=1))
    return jnp.stack(variants, axis=1)


def _swa_mask(S):
    qq = np.arange(BQ)[:, None]
    kk = np.arange(3 * BQ)[None, :]
    tiles = [np.where(np.abs(kk - v * BQ - qq) <= SWA_WINDOW, 0.0, NEG) for v in range(3)]
    return jnp.asarray(np.stack(tiles)[None], F32)


def _ffn_tiles(T, S, F):
    return _row_block(T, S), _pick(F, 1408)


def _halo_rows(dtype):
    return 8 * 4 // jnp.dtype(dtype).itemsize


def _halo_specs(T, bt, bf, dtype):
    hr = _halo_rows(dtype)
    nh = bt // hr
    return [pl.BlockSpec((bt, bf), lambda f, i: (i, f)),
            pl.BlockSpec((hr, bf), lambda f, i: (jnp.maximum(i * nh - 1, 0), f)),
            pl.BlockSpec((hr, bf), lambda f, i: (jnp.minimum((i + 1) * nh, T // hr - 1), f))]


def _neighbours(x, prev, nxt, i, bt, S, T):
    r = lax.broadcasted_iota(jnp.int32, x.shape, 0)
    g0 = i * bt
    first_open = jnp.logical_or(g0 == 0, g0 == S)
    last_open = jnp.logical_or(g0 + bt == S, g0 + bt == T)
    hr = prev.shape[0]
    before = jnp.where(r == 0, jnp.where(first_open, 0.0, prev[hr - 1:hr, :].astype(F32)), pltpu.roll(x, 1, 0))
    after = jnp.where(r == bt - 1, jnp.where(last_open, 0.0, nxt[0:1, :].astype(F32)), pltpu.roll(x, bt - 1, 0))
    return before, after


def _sigmoid(a):
    return 1.0 / (1.0 + jnp.exp(-a))


def _ffn_fwd(gp, u, cw, cb, S, *, name):
    T, F = gp.shape
    bt, bf = _ffn_tiles(T, S, F)

    def body(g_ref, gp_ref, gn_ref, u_ref, w_ref, b_ref, o_ref):
        i = pl.program_id(1)
        g = g_ref[...].astype(F32)
        before, after = _neighbours(g, gp_ref[...], gn_ref[...], i, bt, S, T)
        a = before * w_ref[0:1, :] + g * w_ref[1:2, :] + after * w_ref[2:3, :] + b_ref[...]
        o_ref[...] = (a * _sigmoid(a) * u_ref[...].astype(F32)).astype(o_ref.dtype)

    return pl.pallas_call(
        body, name=name, grid=(F // bf, T // bt),
        in_specs=_halo_specs(T, bt, bf, gp.dtype) + [pl.BlockSpec((bt, bf), lambda f, i: (i, f)),
                                              pl.BlockSpec((3, bf), lambda f, i: (0, f)),
                                              pl.BlockSpec((1, bf), lambda f, i: (0, f))],
        out_specs=pl.BlockSpec((bt, bf), lambda f, i: (i, f)),
        out_shape=jax.ShapeDtypeStruct((T, F), BF16),
        compiler_params=_cparams(("parallel", "parallel")),
    )(gp, gp, gp, u, cw, cb)


def _ffn_bwd_act(gp, u, da_out, cw, cb, S, *, name):
    T, F = gp.shape
    bt, bf = _ffn_tiles(T, S, F)

    def body(g_ref, gp_ref, gn_ref, u_ref, d_ref, w_ref, b_ref, da_ref, du_ref, acc_ref):
        i = pl.program_id(1)

        @pl.when(i == 0)
        def _():
            acc_ref[...] = jnp.zeros_like(acc_ref)

        g = g_ref[...].astype(F32)
        before, after = _neighbours(g, gp_ref[...], gn_ref[...], i, bt, S, T)
        a = before * w_ref[0:1, :] + g * w_ref[1:2, :] + after * w_ref[2:3, :] + b_ref[...]
        sig = _sigmoid(a)
        d = d_ref[...].astype(F32)
        du_ref[...] = (d * (a * sig)).astype(du_ref.dtype)
        da = d * u_ref[...].astype(F32) * (sig * (1.0 + a * (1.0 - sig)))
        da_ref[...] = da
        acc_ref[0:1, :] += jnp.sum(da * before, axis=0, keepdims=True)
        acc_ref[1:2, :] += jnp.sum(da * g, axis=0, keepdims=True)
        acc_ref[2:3, :] += jnp.sum(da * after, axis=0, keepdims=True)
        acc_ref[3:4, :] += jnp.sum(da, axis=0, keepdims=True)

    blk = pl.BlockSpec((bt, bf), lambda f, i: (i, f))
    return pl.pallas_call(
        body, name=name, grid=(F // bf, T // bt),
        in_specs=_halo_specs(T, bt, bf, gp.dtype) + [blk, blk,
                                              pl.BlockSpec((3, bf), lambda f, i: (0, f)),
                                              pl.BlockSpec((1, bf), lambda f, i: (0, f))],
        out_specs=[blk, blk, pl.BlockSpec((8, bf), lambda f, i: (0, f))],
        out_shape=[jax.ShapeDtypeStruct((T, F), F32), jax.ShapeDtypeStruct((T, F), BF16),
                   jax.ShapeDtypeStruct((8, F), F32)],
        compiler_params=_cparams(("parallel", "arbitrary")),
    )(gp, gp, gp, u, da_out, cw, cb)


def _ffn_bwd_conv(da, cw, S, *, name):
    T, F = da.shape
    bt, bf = _ffn_tiles(T, S, F)

    def body(d_ref, dp_ref, dn_ref, w_ref, o_ref):
        i = pl.program_id(1)
        d = d_ref[...]
        before, after = _neighbours(d, dp_ref[...], dn_ref[...], i, bt, S, T)
        o_ref[...] = (after * w_ref[0:1, :] + d * w_ref[1:2, :] + before * w_ref[2:3, :]).astype(o_ref.dtype)

    return pl.pallas_call(
        body, name=name, grid=(F // bf, T // bt),
        in_specs=_halo_specs(T, bt, bf, da.dtype) + [pl.BlockSpec((3, bf), lambda f, i: (0, f))],
        out_specs=pl.BlockSpec((bt, bf), lambda f, i: (i, f)),
        out_shape=jax.ShapeDtypeStruct((T, F), BF16),
        compiler_params=_cparams(("parallel", "parallel")),
    )(da, da, da, cw)


def _ew_rows(R, N, n_arrays):
    return _pick(R, max(16, EW_VMEM_BUDGET // (8 * n_arrays * N)), 16)


def _adam(w, g, m, v, *, name):
    lead = w.shape[:-2]
    R, N = w.shape[-2:]
    br = _ew_rows(R, N, 7)
    bc1 = 1.0 - ADAM_B1 ** ADAM_STEP
    bc2 = 1.0 - ADAM_B2 ** ADAM_STEP

    def body(w_ref, g_ref, m_ref, v_ref, d_ref, mo_ref, vo_ref):
        gv = g_ref[...]
        mn = ADAM_B1 * m_ref[...] + (1.0 - ADAM_B1) * gv
        vn = ADAM_B2 * v_ref[...] + (1.0 - ADAM_B2) * (gv * gv)
        mo_ref[...] = mn
        vo_ref[...] = vn
        d_ref[...] = -ADAM_LR * ((mn / bc1) / (jnp.sqrt(vn / bc2) + ADAM_EPS) + ADAM_WD * w_ref[...])

    if lead:
        blk = pl.BlockSpec((None, br, N), lambda l, i: (l, i, 0))
    else:
        blk = pl.BlockSpec((br, N), lambda i: (i, 0))
    shp = jax.ShapeDtypeStruct(w.shape, F32)
    return pl.pallas_call(
        body, name=name, grid=lead + (R // br,),
        in_specs=[blk, blk, blk, blk], out_specs=[blk, blk, blk], out_shape=[shp, shp, shp],
        compiler_params=_cparams(("parallel",) * (len(lead) + 1)),
    )(w, g, m, v)


def _sum_lead(x, out_dtype, *, name):
    n, R, N = x.shape
    br = _ew_rows(R, N, n + 1)

    def body(x_ref, o_ref):
        acc = x_ref[0].astype(F32)
        for k in range(1, n):
            acc = acc + x_ref[k].astype(F32)
        o_ref[...] = acc.astype(o_ref.dtype)

    return pl.pallas_call(
        body, name=name, grid=(R // br,),
        in_specs=[pl.BlockSpec((n, br, N), lambda i: (0, i, 0))],
        out_specs=pl.BlockSpec((br, N), lambda i: (i, 0)),
        out_shape=jax.ShapeDtypeStruct((R, N), out_dtype),
        compiler_params=_cparams(("parallel",)),
    )(x)


def _sum_parts(parts, landed, chip, core, stack, *, name):
    _, R, N = parts.shape
    n_layers, layer, buf = stack
    br = _ew_rows(R, N, 5)

    def body(pos_ref, own_ref, landed_ref, *rest):
        o_ref = rest[-1]
        acc = own_ref[...].astype(F32)
        for k in range(3):
            acc = acc + landed_ref[k].astype(F32)
        o_ref[...] = acc

    operands = [jnp.stack([chip, core]).astype(jnp.int32), parts, landed]
    in_specs = [pl.BlockSpec((None, br, N), lambda i, pos: (pos[0], i, 0)),
                pl.BlockSpec((3, br, N), lambda i, pos: (0, i, 0))]
    aliases = {}
    if buf is not None:
        aliases = {3: 0}
        operands.append(buf)
        in_specs.append(pl.BlockSpec(memory_space=pl.ANY))
    return pl.pallas_call(
        body, name=name,
        grid_spec=pltpu.PrefetchScalarGridSpec(
            num_scalar_prefetch=1, grid=(R // br,), in_specs=in_specs,
            out_specs=pl.BlockSpec((None, None, br, N), lambda i, pos: (layer, pos[1], i, 0))),
        out_shape=jax.ShapeDtypeStruct((n_layers, 2, R, N), F32),
        input_output_aliases=aliases,
        compiler_params=_cparams(("parallel",)),
    )(*operands)


def _place_own(shards, layer, core, slot, *, name):
    _, R, N = shards.shape
    br = _ew_rows(R // 2, N, 2)
    nb = R // 2 // br

    def body(pos_ref, x_ref, o_ref):
        o_ref[...] = x_ref[...].astype(o_ref.dtype)

    return pl.pallas_call(
        body, name=name,
        grid_spec=pltpu.PrefetchScalarGridSpec(
            num_scalar_prefetch=1, grid=(nb,),
            in_specs=[pl.BlockSpec((None, br, N), lambda i, pos: (layer, pos[0] * nb + i, 0))],
            out_specs=pl.BlockSpec((None, br, N), lambda i, pos: (pos[1], i, 0))),
        out_shape=jax.ShapeDtypeStruct((8, R // 2, N), BF16),
        compiler_params=_cparams(("parallel",)),
    )(jnp.stack([core, slot]).astype(jnp.int32), shards)


def _add_half(g, r, core, *, name):
    Q, _, R, N = g.shape
    br = _ew_rows(R, N, 3)

    def body(c_ref, g_ref, r_ref, o_ref):
        o_ref[...] = (g_ref[...] + r_ref[...]).astype(o_ref.dtype)

    return pl.pallas_call(
        body, name=name,
        grid_spec=pltpu.PrefetchScalarGridSpec(
            num_scalar_prefetch=1, grid=(Q, R // br),
            in_specs=[pl.BlockSpec((None, None, br, N), lambda q, i, c_ref: (q, c_ref[0], i, 0)),
                      pl.BlockSpec((None, br, N), lambda q, i, c_ref: (q, i, 0))],
            out_specs=pl.BlockSpec((None, br, N), lambda q, i, c_ref: (q, i, 0))),
        out_shape=jax.ShapeDtypeStruct((Q, R, N), BF16),
        compiler_params=_cparams(("parallel", "parallel")),
    )(core.reshape(1).astype(jnp.int32), g, r)


_ANY = pl.BlockSpec(memory_space=pl.ANY)


def _place():
    return lax.axis_index("x"), lax.axis_index("y"), lax.axis_index("c")


def _allgather8(blocks, *, name):
    n = len(blocks)

    def body(*refs):
        xs, outs = refs[:n], refs[n:2 * n]
        send_sems, recv_sems, local_sems = refs[2 * n:]
        x, y, c = _place()
        me, sibling = (x, y, c), (x, y, 1 - c)
        chips = [(1 - x, y), (x, 1 - y), (1 - x, 1 - y)]

        def slot(a, px, py, pc):
            return outs[a].at[4 * px + 2 * py + pc]

        def copy(a, k, block, to, src=None):
            return pltpu.make_async_remote_copy(
                src_ref=slot(a, *block) if src is None else src, dst_ref=slot(a, *block),
                send_sem=send_sems.at[a, k], recv_sem=recv_sems.at[a, k], device_id=to, device_id_type=MESH)

        mine = [pltpu.make_async_copy(xs[a], slot(a, *me), local_sems.at[a]) for a in range(n)]
        for cp in mine:
            cp.start()
        first = []
        for a in range(n):
            first.append(copy(a, 0, me, sibling, src=xs[a]))
            first += [copy(a, 1 + j, me, (*chip, c), src=xs[a]) for j, chip in enumerate(chips)]
        for cp in first:
            cp.start()
        passed = []
        for j, chip in enumerate(chips):
            for a in range(n):
                copy(a, 1 + j, (*chip, c), me).wait_recv()
                fwd = copy(a, 4 + j, (*chip, c), sibling)
                fwd.start()
                passed.append(fwd)
        for a in range(n):
            copy(a, 0, sibling, me).wait_recv()
            for j, chip in enumerate(chips):
                copy(a, 4 + j, (*chip, 1 - c), me).wait_recv()
        for cp in first + passed:
            cp.wait_send()
        for cp in mine:
            cp.wait()

    return pl.pallas_call(
        body, name=name,
        in_specs=[_ANY] * n, out_specs=[_ANY] * n,
        out_shape=[jax.ShapeDtypeStruct((8,) + b.shape, b.dtype) for b in blocks],
        scratch_shapes=[pltpu.SemaphoreType.DMA((n, 7)), pltpu.SemaphoreType.DMA((n, 7)), pltpu.SemaphoreType.DMA((n,))],
    )(*blocks)


class _Exchange:
    n_sems = 1

    def __init__(self, arrays):
        self.arrays = list(arrays)

    def out_shapes(self):
        return [jax.ShapeDtypeStruct(g.shape[:1] + g.shape[2:], g.dtype) for g in self.arrays]

    def copy(self, k, src, dst, sems, landing):
        x, y, c = _place()
        return pltpu.make_async_remote_copy(src_ref=src.at[:, 1 - c], dst_ref=dst, send_sem=sems[0], recv_sem=sems[1],
                                            device_id=(x, y, 1 - c), device_id_type=MESH)

    def copies(self, group, landing):
        xs, outs, send_sems, recv_sems = group
        return [self.copy(k, xs[a], outs[a], (send_sems.at[a, k], recv_sems.at[a, k]), landing)
                for a in range(len(xs)) for k in range(self.n_sems)]


class _Scatter(_Exchange):
    n_sems = 3

    def out_shapes(self):
        return [jax.ShapeDtypeStruct((3,) + p.shape[1:], p.dtype) for p in self.arrays]

    def copy(self, k, src, dst, sems, landing):
        x, y, c = _place()
        px, py = [(1 - x, y), (x, 1 - y), (1 - x, 1 - y)][k]
        return pltpu.make_async_remote_copy(src_ref=src.at[2 * px + py], dst_ref=dst.at[k], send_sem=sems[0], recv_sem=sems[1],
                                            device_id=(px, py, c), device_id_type=MESH)


class _GatherChips(_Exchange):
    n_sems = 3
    in_place = True

    def out_shapes(self):
        return [jax.ShapeDtypeStruct(b.shape, b.dtype) for b in self.arrays]

    def copy(self, k, src, dst, sems, landing):
        x, y, c = _place()
        px, py = [(1 - x, y), (x, 1 - y), (1 - x, 1 - y)][k]
        slot = 4 * px + 2 * py + c if landing else 4 * x + 2 * y + c
        return pltpu.make_async_remote_copy(src_ref=src.at[4 * x + 2 * y + c], dst_ref=dst.at[slot], send_sem=sems[0],
                                            recv_sem=sems[1], device_id=(px, py, c), device_id_type=MESH)


class _GatherCores(_GatherChips):
    n_sems = 4

    def copy(self, k, src, dst, sems, landing):
        x, y, c = _place()
        slot = 2 * k + 1 - c if landing else 2 * k + c
        return pltpu.make_async_remote_copy(src_ref=src.at[2 * k + c], dst_ref=dst.at[slot], send_sem=sems[0],
                                            recv_sem=sems[1], device_id=(x, y, 1 - c), device_id_type=MESH)


def _ride_aliases(rides, first_in, first_out):
    aliases, i = {}, 0
    for r in rides:
        for a in range(len(r.arrays)):
            if getattr(r, "in_place", False):
                aliases[first_in + i + a] = first_out + i + a
        i += len(r.arrays)
    return aliases


def _ride_scratch(rides):
    shapes = []
    for r in rides:
        shapes += [pltpu.SemaphoreType.DMA((len(r.arrays), r.n_sems)), pltpu.SemaphoreType.DMA((len(r.arrays), r.n_sems))]
    return shapes


def _ride_split(rides, in_refs, out_refs, sem_refs):
    groups, i, o = [], 0, 0
    for k, r in enumerate(rides):
        n = len(r.arrays)
        groups.append((in_refs[i:i + n], out_refs[o:o + n], sem_refs[2 * k], sem_refs[2 * k + 1]))
        i, o = i + n, o + n
    return groups


def _ride_start(rides, groups):
    for r, g in zip(rides, groups):
        for cp in r.copies(g, False):
            cp.start()


def _ride_wait(rides, groups):
    for r, g in zip(rides, groups):
        for cp in r.copies(g, True):
            cp.wait_recv()
        for cp in r.copies(g, False):
            cp.wait_send()


def _run_rides(rides, *, name):
    n_in = sum(len(r.arrays) for r in rides)

    def body(*refs):
        groups = _ride_split(rides, refs[:n_in], refs[n_in:2 * n_in], refs[2 * n_in:])
        _ride_start(rides, groups)
        _ride_wait(rides, groups)

    outs = pl.pallas_call(
        body, name=name,
        in_specs=[_ANY] * n_in, out_specs=[_ANY] * n_in,
        out_shape=[s for r in rides for s in r.out_shapes()],
        scratch_shapes=_ride_scratch(rides),
    )(*[a for r in rides for a in r.arrays])
    return _ride_outputs(rides, outs)


def _ride_outputs(rides, outs):
    res, o = [], 0
    for r in rides:
        res.append(list(outs[o:o + len(r.arrays)]))
        o += len(r.arrays)
    return res


def _join_halves(bufs, *, name):
    n = len(bufs)

    def body(*refs):
        xs, outs = refs[:n], refs[n:2 * n]
        send_sems, recv_sems = refs[2 * n:]
        x, y, c = _place()
        sends = [pltpu.make_async_remote_copy(
            src_ref=xs[a].at[:, c], dst_ref=outs[a].at[:, c], send_sem=send_sems.at[a], recv_sem=recv_sems.at[a],
            device_id=(x, y, 1 - c), device_id_type=MESH) for a in range(n)]
        recvs = [pltpu.make_async_remote_copy(
            src_ref=xs[a].at[:, c], dst_ref=outs[a].at[:, 1 - c], send_sem=send_sems.at[a], recv_sem=recv_sems.at[a],
            device_id=(x, y, 1 - c), device_id_type=MESH) for a in range(n)]
        for cp in sends:
            cp.start()
        for cp in recvs:
            cp.wait_recv()
        for cp in sends:
            cp.wait_send()

    return pl.pallas_call(
        body, name=name,
        in_specs=[_ANY] * n, out_specs=[_ANY] * n,
        out_shape=[jax.ShapeDtypeStruct(b.shape, b.dtype) for b in bufs],
        input_output_aliases={a: a for a in range(n)},
        scratch_shapes=[pltpu.SemaphoreType.DMA((n,)), pltpu.SemaphoreType.DMA((n,))],
    )(*bufs)


def _perm_w_in(w):
    pad = jnp.zeros((w.shape[0], PCOLS - IN_COLS), w.dtype)
    return jnp.concatenate([w[:, :3072], w[:, 3136:IN_COLS], w[:, 3072:3136], pad], axis=1)


def _unperm_w_in(g):
    return jnp.concatenate([g[:, :3072], g[:, 4096:IN_COLS], g[:, 3072:4096]], axis=1)


def _perm_w_uq(w):
    w4 = w.reshape(MLA_Q_LORA, MLA_HEADS, MLA_NOPE + MLA_ROPE)
    nope = w4[:, :, :MLA_NOPE].reshape(MLA_Q_LORA, MLA_HEADS * LANE)
    pe = jnp.pad(w4[:, :, MLA_NOPE:], ((0, 0), (0, 0), (0, LANE - MLA_ROPE))).reshape(MLA_Q_LORA, MLA_HEADS * LANE)
    return jnp.concatenate([nope, pe], axis=1)


def _unperm_w_uq(g):
    nope = g[:, :MLA_HEADS * LANE].reshape(MLA_Q_LORA, MLA_HEADS, LANE)
    pe = g[:, MLA_HEADS * LANE:].reshape(MLA_Q_LORA, MLA_HEADS, LANE)[:, :, :MLA_ROPE]
    return jnp.concatenate([nope, pe], axis=2).reshape(MLA_Q_LORA, MLA_HEADS * (MLA_NOPE + MLA_ROPE))


def _perm_w_ukv(w):
    w4 = w.reshape(MLA_KV_LORA, MLA_HEADS, MLA_NOPE + MLA_V)
    return jnp.concatenate([w4[:, :, :MLA_NOPE].reshape(MLA_KV_LORA, -1), w4[:, :, MLA_NOPE:].reshape(MLA_KV_LORA, -1)], axis=1)


def _unperm_w_ukv(g):
    kn = g[:, :MLA_HEADS * LANE].reshape(MLA_KV_LORA, MLA_HEADS, LANE)
    vv = g[:, MLA_HEADS * LANE:].reshape(MLA_KV_LORA, MLA_HEADS, LANE)
    return jnp.concatenate([kn, vv], axis=2).reshape(MLA_KV_LORA, -1)


def _silu(v):
    return v * jax.nn.sigmoid(v)


def _silu_grad(v):
    s = jax.nn.sigmoid(v)
    return s * (1.0 + v * (1.0 - s))


_WEIGHTS = ("c_ctx", "w_ada", "b_ada", "w_in", "na_rpb", "swa_sink", "mla_q_norm", "mla_kv_norm", "mla_w_uq", "mla_w_ukv",
            "gqa_q_norm", "gqa_k_norm", "w_out", "ln1_g", "ln1_b", "ffn_w_gate", "ffn_w_up", "ffn_conv_w", "ffn_conv_b",
            "ffn_w_down", "ln2_g", "ln2_b")
_COL_SHARDED = ("w_in", "mla_w_uq", "mla_w_ukv", "ffn_w_gate", "ffn_w_up")
_ROW_SHARDED = ("w_out", "ffn_w_down")
_BIG = _COL_SHARDED + _ROW_SHARDED
_SMALL = ("c_ctx", "b_ada", "na_rpb", "swa_sink", "mla_q_norm", "mla_kv_norm", "gqa_q_norm", "gqa_k_norm", "ln1_g", "ln1_b",
          "ffn_conv_w", "ffn_conv_b", "ln2_g", "ln2_b")


def _pack(arrays):
    flat = jnp.concatenate([a.reshape(-1) for a in arrays])
    n = flat.shape[0]
    rows = -(-n // (8 * LANE)) * 8
    return jnp.pad(flat, (0, rows * LANE - n)).reshape(rows, LANE)


def _unpack(packed, like):
    flat = packed.reshape(-1)
    out, o = [], 0
    for a in like:
        out.append(flat[o:o + a.size].reshape(a.shape))
        o += a.size
    return out


def _train_step(x, c, ctx, loss_target, w, m_in, v_in):
    L = DEPTH
    S, D = x.shape[1], x.shape[2]
    C = ctx.shape[1]
    T = S + C
    F = w["ffn_conv_b"].shape[1]
    ax, ay, ac = _place()
    chip = 2 * ax + ay
    dev = 2 * chip + ac
    n_ada = w["w_ada"].shape[2]

    gather_groups = {"A": ("w_in", "mla_w_uq", "mla_w_ukv"), "B": ("w_out",), "C": ("ffn_w_gate", "ffn_w_up"), "D": ("ffn_w_down",)}
    full = {n: [None] * L for n in _BIG}
    w_in_p, w_uq_p, w_ukv_p = [None] * L, [None] * L, [None] * L
    half_done = {}

    def chips_step(group, l):
        return _GatherChips([_place_own(w[n], l, ac, dev, name="gather_place") for n in gather_groups[group]])

    def cores_step(group, l):
        return _GatherCores(half_done.pop((group, l)))

    def finish_group(group, l, bufs):
        for n, b in zip(gather_groups[group], bufs):
            r, cols = b.shape[1:]
            if n in _COL_SHARDED:
                full[n][l] = b.reshape(4, 2, r, cols).transpose(1, 2, 0, 3).reshape(2 * r, 4 * cols)
            else:
                full[n][l] = b.reshape(8 * r, cols)
        if group == "A":
            w_in_p[l], w_uq_p[l] = _perm_w_in(full["w_in"][l]), _perm_w_uq(full["mla_w_uq"][l])
            w_ukv_p[l] = _perm_w_ukv(full["mla_w_ukv"][l])

    def with_rides(result, rides):
        return result if rides else (result, [])

    def my_half(a):
        r = a.shape[0] // 2
        return lax.dynamic_slice_in_dim(a, ac * r, r, axis=0).astype(BF16)

    gathered = _allgather8([my_half(w[n][0]) for n in gather_groups["A"]] + [w["ffn_conv_w"]], name="gather_weights")
    finish_group("A", 0, gathered[:-1])
    conv_w = gathered[-1][::2].transpose(1, 2, 0, 3).reshape(L, 3, F)

    (c_all,) = _allgather8([c], name="gather_c")
    c16 = jnp.concatenate([c_all.reshape(8, D), jnp.broadcast_to(w["c_ctx"][None], (8, D))], axis=0)
    row_keep = (jnp.arange(16) <= 8).astype(F32)[:, None]
    sc = _silu(c16) * row_keep
    b_loc = lax.dynamic_slice_in_dim(w["b_ada"], chip * n_ada, n_ada, axis=1)
    mod_loc = jnp.stack([_mm(sc, w["w_ada"], b_layer=l, name="mod_mm") + b_loc[l][None] for l in range(L)])
    (mod_g,) = _allgather8([mod_loc], name="gather_mod")
    mod_all = mod_g[::2].transpose(1, 2, 0, 3).reshape(L, 16, 4 * n_ada)
    mod_x = lax.dynamic_index_in_dim(mod_all, dev, axis=1, keepdims=False)
    mod_c = mod_all[:, 8]
    mods = [jnp.stack([mod_x[l].reshape(6, D), mod_c[l].reshape(6, D)], axis=1) for l in range(L)]

    tabs = _rope_tables(S, C, HEAD_DIM) + _rope_tables(S, C, MLA_ROPE)
    swa_mask = _swa_mask(S)
    scale = HEAD_DIM ** -0.5
    def attn_cfgs(l):
        cq = l < L - 1
        return (_AttnCfg(Hkv=NA_HEADS, G=1, S=S, C=C, band=(2, 5), scale=scale, n_var=5, bias_per_head=True, ctx_queries=cq),
                _AttnCfg(Hkv=SWA_KV_HEADS, G=SWA_HEADS // SWA_KV_HEADS, S=S, C=C, band=(1, 3), scale=scale, n_var=3,
                         has_sink=True, ctx_queries=cq),
                _AttnCfg(Hkv=MLA_HEADS, G=1, S=S, C=C, band=None, scale=(MLA_NOPE + MLA_ROPE) ** -0.5, two=True,
                         bq=2 * BQ, ctx_queries=cq),
                _AttnCfg(Hkv=GQA_KV_HEADS, G=GQA_HEADS // GQA_KV_HEADS, S=S, C=C, band=None, scale=scale, ctx_queries=cq))

    row = lambda a: a[None, :]

    xt = jnp.concatenate([x[0], ctx[0]], axis=0)
    saved = []
    for l in range(L):
        md = mods[l]
        gq, gk, mq, mkv = row(w["gqa_q_norm"][l]), row(w["gqa_k_norm"][l]), row(w["mla_q_norm"][l]), row(w["mla_kv_norm"][l])
        h1 = _mod_fwd(xt, md[0], md[1], S, name="mod_fwd")
        p = _mm(h1, w_in_p[l], name="in_proj")
        qkv = _prep_fwd(p, tabs, gq, gk, mq, mkv, S, name="prep_fwd")
        qm = _mm(qkv, w_uq_p[l], a_off=CB_CQ * LANE, a_k=MLA_Q_LORA, tk=LANE, name="mla_uq")
        qmb = _pe_rope(qm, tabs, S, 1.0, BF16, name="mla_q_rope")
        kvm = _mm(qkv, w_ukv_p[l], a_off=CB_CKV * LANE, a_k=MLA_KV_LORA, tk=LANE, out_dtype=BF16, name="mla_ukv")
        bias_na = _na_bias(w["na_rpb"][l], S)
        sink = jnp.broadcast_to(jnp.repeat(w["swa_sink"][l].reshape(SWA_KV_HEADS, -1), BQ, axis=1)[:, :, None],
                                (SWA_KV_HEADS, SWA_HEADS // SWA_KV_HEADS * BQ, LANE))
        cfg_na, cfg_swa, cfg_mla, cfg_gqa = attn_cfgs(l)
        rows = T if l < L - 1 else S
        first, more = l == 0, l + 1 < L
        rides = [chips_step("B", l)] if first else []
        mix_blocks = NA_HEADS + SWA_HEADS + MLA_HEADS + GQA_HEADS
        mix, got = with_rides(_attn_fwd(cfg_na, qkv, CB_NA_Q, qkv, CB_NA_K, qkv, CB_NA_V, bias=bias_na, rides=rides,
                                        into=(None, 0, mix_blocks), name="na_fwd"), rides)
        if first:
            half_done[("B", l)] = got[0]
        rides = [cores_step("B", l)] if first else []
        mix, got = with_rides(_attn_fwd(cfg_swa, qkv, CB_SWA_Q, qkv, CB_SWA_K, qkv, CB_SWA_V, bias=swa_mask, sink=sink,
                                        rides=rides, into=(mix, NA_HEADS, mix_blocks), name="swa_fwd"), rides)
        if first:
            finish_group("B", l, got[0])
        mix, got = _attn_fwd(cfg_mla, qmb, 0, kvm, 0, kvm, MLA_HEADS, q2=qmb, q2_cb=MLA_HEADS, k2=qkv, k2_cb=CB_KPE,
                             rides=[chips_step("C", l)], into=(mix, NA_HEADS + SWA_HEADS, mix_blocks), name="mla_fwd")
        half_done[("C", l)] = got[0]
        mix, got = _attn_fwd(cfg_gqa, qkv, CB_GQA_Q, qkv, CB_GQA_K, qkv, CB_GQA_V,
                             rides=[cores_step("C", l), chips_step("D", l)],
                             into=(mix, NA_HEADS + SWA_HEADS + MLA_HEADS, mix_blocks), name="gqa_fwd")
        finish_group("C", l, got[0])
        half_done[("D", l)] = got[1]
        z1 = _mm(mix, full["w_out"][l], rows=rows, name="out_proj")
        x1 = _res_fwd(xt, z1, md[2], row(w["ln1_g"][l]), row(w["ln1_b"][l]), S, name="res_fwd")
        h2 = _mod_fwd(x1, md[3], md[4], S, name="mod_fwd")
        gp, got = _mm(h2, full["ffn_w_gate"][l], rides=[cores_step("D", l)] + ([chips_step("A", l + 1)] if more else []),
                      out_dtype=BF16, name="ffn_in")
        finish_group("D", l, got[0])
        if more:
            half_done[("A", l + 1)] = got[1]
        rides = [cores_step("A", l + 1), chips_step("B", l + 1)] if more else []
        up, got = with_rides(_mm(h2, full["ffn_w_up"][l], rides=rides, out_dtype=BF16, name="ffn_in"), rides)
        if more:
            finish_group("A", l + 1, got[0])
            half_done[("B", l + 1)] = got[1]
        act = _ffn_fwd(gp, up, conv_w[l], row(w["ffn_conv_b"][l]), S, name="ffn_mid")
        rides = [cores_step("B", l + 1)] if more else []
        z2, got = with_rides(_mm(act, full["ffn_w_down"][l], rides=rides, name="ffn_out"), rides)
        if more:
            finish_group("B", l + 1, got[0])
        x2 = _res_fwd(x1, z2, md[5], row(w["ln2_g"][l]), row(w["ln2_b"][l]), S, name="res_fwd")
        saved.append(dict(x=xt, h1=h1, p=p, qkv=qkv, qmb=qmb, kvm=kvm, bias_na=bias_na, sink=sink, mix=mix, z1=z1, x1=x1,
                          h2=h2, gp=gp, up=up, act=act, z2=z2, cfgs=(cfg_na, cfg_swa, cfg_mla, cfg_gqa)))
        xt = x2

    dx, loss_part = _loss_fwd_bwd(xt, loss_target[0], S, name="loss")
    loss = lax.psum(loss_part[0, 0], ("x", "y", "c"))

    groups = {"ffn": ("ffn_w_gate", "ffn_w_up", "ffn_w_down"), "rest": ("w_in", "mla_w_uq", "mla_w_ukv", "w_out")}
    wgrad = [dict() for _ in range(L)]
    parts, landed = {}, {}

    def halves_of(group, l):
        return [wgrad[l][n].reshape(4, 2, wgrad[l][n].shape[1] // 2, wgrad[l][n].shape[2]) for n in groups[group]]

    def add_halves(group, l, received):
        parts[(group, l)] = [_add_half(h, r, ac, name="rs_core_add") for h, r in zip(halves_of(group, l), received)]

    small = {n: [None] * L for n in ("na_rpb", "swa_sink", "mla_q_norm", "mla_kv_norm", "gqa_q_norm", "gqa_k_norm",
                                     "ln1_g", "ln1_b", "ffn_conv_w", "ffn_conv_b", "ln2_g", "ln2_b")}
    dmod = [None] * L
    for l in reversed(range(L)):
        sv, md = saved[l], mods[l]
        gq, gk, mq, mkv = row(w["gqa_q_norm"][l]), row(w["gqa_k_norm"][l]), row(w["mla_q_norm"][l]), row(w["mla_kv_norm"][l])
        cb_row = row(w["ffn_conv_b"][l])
        dx1, dz2, acc_r2 = _res_bwd(sv["x1"], sv["z2"], md[5], row(w["ln2_g"][l]), dx, S, name="res_bwd")
        dact = _mm(dz2, full["ffn_w_down"][l], mode="nt", out_dtype=BF16, name="ffn_out_dx")
        wgrad[l]["ffn_w_down"] = _mm(sv["act"], dz2, mode="tn", name="ffn_out_dw").reshape(4, F // 4, D)
        da, du, acc_f = _ffn_bwd_act(sv["gp"], sv["up"], dact, conv_w[l], cb_row, S, name="ffn_mid_bwd")
        dg = _ffn_bwd_conv(da, conv_w[l], S, name="ffn_conv_bwd")
        dh2 = _mm(dg, full["ffn_w_gate"][l], mode="nt", name="ffn_in_dx")
        dh2 = _mm(du, full["ffn_w_up"][l], mode="nt", add=dh2, name="ffn_in_dx_add")
        wgrad[l]["ffn_w_gate"] = _mm(sv["h2"], dg, mode="tn", stack=(1, 0, None), split4=True,
                                     name="ffn_in_dw").reshape(4, D, F // 4)
        wgrad[l]["ffn_w_up"] = _mm(sv["h2"], du, mode="tn", stack=(1, 0, None), split4=True,
                                   name="ffn_in_dw").reshape(4, D, F // 4)
        dx1, acc_m2 = _mod_bwd(sv["x1"], dh2, md[4], dx1, S, name="mod_bwd")
        dxa, dz1, acc_r1 = _res_bwd(sv["x"], sv["z1"], md[2], row(w["ln1_g"][l]), dx1, S, name="res_bwd")
        dmix = _mm(dz1, full["w_out"][l], mode="nt", out_dtype=BF16, name="out_proj_dx")
        wgrad[l]["w_out"] = _mm(sv["mix"], dz1, mode="tn", rows=dz1.shape[0], name="out_proj_dw").reshape(4, -1, D)

        qkv, qmb, kvm = sv["qkv"], sv["qmb"], sv["kvm"]
        cfg_na, cfg_swa, cfg_mla, cfg_gqa = sv["cfgs"]
        rest_above = l + 1 < L
        rides = [_Exchange(halves_of("ffn", l))] + ([_Exchange(halves_of("rest", l + 1))] if rest_above else [])
        dq_a, dk_a, dv_a, dbias, received = _attn_bwd(cfg_na, qkv, CB_NA_Q, qkv, CB_NA_K, qkv, CB_NA_V, dmix, 0,
                                                      bias=sv["bias_na"], want_dbias=True, rides=rides, name="na_bwd")
        add_halves("ffn", l, received[0])
        if rest_above:
            add_halves("rest", l + 1, received[1])
        dq_b, dk_b, dv_b, dsink = _attn_bwd(cfg_swa, qkv, CB_SWA_Q, qkv, CB_SWA_K, qkv, CB_SWA_V, dmix, NA_HEADS,
                                            bias=swa_mask, sink=sv["sink"], name="swa_bwd")
        dq_c, dk_c, dv_c, dq2_c, dk2_c, got = _attn_bwd(
            cfg_mla, qmb, 0, kvm, 0, kvm, MLA_HEADS, dmix, NA_HEADS + SWA_HEADS, q2=qmb, q2_cb=MLA_HEADS, k2=qkv,
            k2_cb=CB_KPE, rides=[_Scatter(parts[("ffn", l)])], name="mla_bwd")
        landed[("ffn", l)] = got[0]
        rides = [_Scatter(parts[("rest", l + 1)])] if rest_above else []
        gqa_out = _attn_bwd(cfg_gqa, qkv, CB_GQA_Q, qkv, CB_GQA_K, qkv, CB_GQA_V, dmix,
                            NA_HEADS + SWA_HEADS + MLA_HEADS, rides=rides, name="gqa_bwd")
        dq_d, dk_d, dv_d = gqa_out[:3]
        if rest_above:
            landed[("rest", l + 1)] = gqa_out[3][0]
        dqm = _pe_rope(jnp.concatenate([dq_c, dq2_c], axis=1), tabs, S, -1.0, BF16, name="mla_q_rope_bwd")
        dkvm = jnp.concatenate([dk_c, dv_c], axis=1).astype(BF16)
        dcq = _mm(dqm, w_uq_p[l], mode="nt", name="mla_uq_dx")
        dckv = _mm(dkvm, w_ukv_p[l], mode="nt", name="mla_ukv_dx")
        cqn = qkv[:, CB_CQ * LANE:CB_CKV * LANE]
        ckvn = qkv[:, CB_CKV * LANE:(CB_CKV + 1) * LANE]
        d_uq = _unperm_w_uq(_mm(cqn, dqm, mode="tn", name="mla_uq_dw"))
        d_ukv = _unperm_w_ukv(_mm(ckvn, dkvm, mode="tn", name="mla_ukv_dw"))
        grads = {}
        for h in range(NA_HEADS):
            grads[CB_NA_Q + h], grads[CB_NA_K + h], grads[CB_NA_V + h] = (dq_a, h), (dk_a, h), (dv_a, h)
        for h in range(SWA_HEADS):
            grads[CB_SWA_Q + h] = (dq_b, h)
        for h in range(SWA_KV_HEADS):
            grads[CB_SWA_K + h], grads[CB_SWA_V + h] = (dk_b, h), (dv_b, h)
        for h in range(GQA_HEADS):
            grads[CB_GQA_Q + h] = (dq_d, h)
        for h in range(GQA_KV_HEADS):
            grads[CB_GQA_K + h], grads[CB_GQA_V + h] = (dk_d, h), (dv_d, h)
        grads[CB_KPE], grads[CB_CQ], grads[CB_CKV] = (dk2_c, 0), (dcq, 0), (dckv, 0)
        dp, acc_p = _prep_bwd(sv["p"], grads, tabs, gq, gk, mq, mkv, S, name="prep_bwd")
        dh1 = _mm(dp, w_in_p[l], mode="nt", name="in_proj_dx")
        d_in = _unperm_w_in(_mm(sv["h1"], dp, mode="tn", name="in_proj_dw"))
        dx, acc_m1 = _mod_bwd(sv["x"], dh1, md[1], dxa, S, name="mod_bwd")

        to4 = lambda g: g.reshape(g.shape[0], 4, g.shape[1] // 4).transpose(1, 0, 2)
        wgrad[l]["w_in"], wgrad[l]["mla_w_uq"], wgrad[l]["mla_w_ukv"] = to4(d_in), to4(d_uq), to4(d_ukv)
        dmod[l] = jnp.stack([acc_m1[0:2], acc_m1[2:4], acc_r1[0:2], acc_m2[0:2], acc_m2[2:4], acc_r2[0:2]])
        rpb_vjp = jax.vjp(lambda r: _na_bias(r, S), w["na_rpb"][l])[1]
        small["na_rpb"][l] = rpb_vjp(dbias)[0]
        small["swa_sink"][l] = dsink[:, :, 0].reshape(SWA_KV_HEADS, -1, BQ).sum(axis=-1).reshape(-1)
        small["gqa_q_norm"][l], small["gqa_k_norm"][l] = acc_p[0, :LANE], acc_p[1, :LANE]
        small["mla_q_norm"][l], small["mla_kv_norm"][l] = acc_p[2], acc_p[3, :LANE]
        small["ln1_g"][l], small["ln1_b"][l] = acc_r1[2], acc_r1[3]
        small["ln2_g"][l], small["ln2_b"][l] = acc_r2[2], acc_r2[3]
        small["ffn_conv_w"][l], small["ffn_conv_b"][l] = acc_f[0:3], acc_f[3]
    grad_x = dx[:S][None]

    dmod_x = jnp.stack([dmod[l][:, 0].reshape(-1) for l in range(L)])
    dmod_c = jnp.stack([dmod[l][:, 1].reshape(-1) for l in range(L)])
    small_names = tuple(small)
    bucket = [dmod_x, dmod_c] + [jnp.stack(small[n]) for n in small_names]
    (b8,) = _allgather8([_pack(bucket)], name="gather_small")
    tot = _unpack(_sum_lead(b8, F32, name="sum_small"), bucket)
    dmod_x_all = b8.reshape(8, -1)[:, :dmod_x.size].reshape(8, L, 6 * D)
    dmod_c_tot = tot[1]
    g_small = dict(zip(small_names, tot[2:]))
    g_small["b_ada"] = tot[0] + dmod_c_tot
    g_small["ffn_conv_w"] = lax.dynamic_slice_in_dim(g_small["ffn_conv_w"], chip * (F // 4), F // 4, axis=2)

    dmod16 = jnp.concatenate([dmod_x_all, jnp.broadcast_to(dmod_c_tot[None], (8, L, 6 * D))], axis=0) * row_keep[:, :, None]
    dmod16 = lax.dynamic_slice_in_dim(dmod16, chip * n_ada, n_ada, axis=2)
    g_ada, dsc = None, None
    for l in range(L):
        g_ada = _mm(sc, dmod16[:, l], mode="tn", exact=True, stack=(L, l, g_ada), name="ada_dw")
        dsc = _mm(dmod16[:, l], w["w_ada"], b_layer=l, mode="nt", add=dsc, name="ada_dx" if dsc is None else "ada_dx_add")
    (dsc8,) = _allgather8([dsc[8:16]], name="gather_dsc")
    dsc4 = dsc8[::2, 0]
    g_small["c_ctx"] = (((dsc4[0] + dsc4[1]) + dsc4[2]) + dsc4[3]) * _silu_grad(w["c_ctx"])

    add_halves("rest", 0, _run_rides([_Exchange(halves_of("rest", 0))], name="rs_core_exchange")[0])
    landed[("rest", 0)] = _run_rides([_Scatter(parts[("rest", 0)])], name="rs_chip_scatter")[0]
    sums = {}
    for l in range(L):
        for group, names in groups.items():
            for n, p, got in zip(names, parts[(group, l)], landed[(group, l)]):
                sums[n] = _sum_parts(p, got, chip, ac, (L, l, sums.get(n)), name="rs_chip_sum")
    joined = _join_halves([sums[n] for n in _BIG], name="rs_join")
    g_big = {n: j.reshape(L, 2 * j.shape[2], j.shape[3]) for n, j in zip(_BIG, joined)}
    g_big["w_ada"] = g_ada

    grad, delta, new_m, new_v = {}, {}, {}, {}
    for n in _BIG + ("w_ada",):
        grad[n] = g_big[n]
        delta[n], new_m[n], new_v[n] = _adam(w[n], g_big[n], m_in[n], v_in[n], name="adam")
    like = [w[n] for n in _SMALL]
    packed = [_pack([src[n].reshape(w[n].shape) for n in _SMALL]) for src in (w, g_small, m_in, v_in)]
    d_s, m_s, v_s = _adam(*packed, name="adam_small")
    for n, g_, d_, m_, v_ in zip(_SMALL, _unpack(packed[1], like), _unpack(d_s, like), _unpack(m_s, like), _unpack(v_s, like)):
        grad[n], delta[n], new_m[n], new_v[n] = g_, d_, m_, v_

    return (loss, grad_x, *[grad[n] for n in _WEIGHTS], *[delta[n] for n in _WEIGHTS],
            *[new_m[n] for n in _WEIGHTS], *[new_v[n] for n in _WEIGHTS])


def kernel(x, c, ctx, c_ctx, w_ada, b_ada, w_in, na_rpb, swa_sink, mla_q_norm, mla_kv_norm, mla_w_uq, mla_w_ukv, gqa_q_norm, gqa_k_norm, w_out, ln1_g, ln1_b, ffn_w_gate, ffn_w_up, ffn_conv_w, ffn_conv_b, ffn_w_down, ln2_g, ln2_b, loss_target, m_c_ctx, m_w_ada, m_b_ada, m_w_in, m_na_rpb, m_swa_sink, m_mla_q_norm, m_mla_kv_norm, m_mla_w_uq, m_mla_w_ukv, m_gqa_q_norm, m_gqa_k_norm, m_w_out, m_ln1_g, m_ln1_b, m_ffn_w_gate, m_ffn_w_up, m_ffn_conv_w, m_ffn_conv_b, m_ffn_w_down, m_ln2_g, m_ln2_b, v_c_ctx, v_w_ada, v_b_ada, v_w_in, v_na_rpb, v_swa_sink, v_mla_q_norm, v_mla_kv_norm, v_mla_w_uq, v_mla_w_ukv, v_gqa_q_norm, v_gqa_k_norm, v_w_out, v_ln1_g, v_ln1_b, v_ffn_w_gate, v_ffn_w_up, v_ffn_conv_w, v_ffn_conv_b, v_ffn_w_down, v_ln2_g, v_ln2_b):
    args = locals()
    w = {n: args[n] for n in _WEIGHTS}
    m_in = {n: args["m_" + n] for n in _WEIGHTS}
    v_in = {n: args["v_" + n] for n in _WEIGHTS}
    return _train_step(x, c, ctx, loss_target, w, m_in, v_in)
```

```python
import functools
import math

import numpy as np
import jax
import jax.numpy as jnp
from jax import lax
from jax.experimental import pallas as pl
from jax.experimental.pallas import tpu as pltpu

F32 = jnp.float32
BF16 = jnp.bfloat16
MESH = pl.DeviceIdType.MESH

GRID_W = 64
HEAD_DIM = 128
NA_HEADS, NA_WIN_R, NA_WIN_C = 4, 8, 16
SWA_HEADS, SWA_KV_HEADS, SWA_WINDOW = 4, 2, 128
MLA_HEADS, MLA_Q_LORA, MLA_KV_LORA, MLA_NOPE, MLA_ROPE, MLA_V = 4, 384, 128, 128, 64, 128
GQA_HEADS, GQA_KV_HEADS = 4, 2
ROPE_THETA = 10000.0
EPS = 1e-6
NEG = -1e30
DEPTH = 2
DEEPNORM_ALPHA = (2 * DEPTH) ** 0.25
ADAM_LR, ADAM_B1, ADAM_B2, ADAM_EPS, ADAM_WD, ADAM_STEP = 0.001, 0.9, 0.999, 1e-08, 0.01, 10

LANE = 128
V7X_VMEM_BYTES = 64 * 1024 * 1024
VMEM_LIMIT = 56 * 1024 * 1024
MM_VMEM_BUDGET = 40 * 1024 * 1024
EW_VMEM_BUDGET = 28 * 1024 * 1024
BQ = 128

CB_NA_Q, CB_NA_K, CB_NA_V = 0, 4, 8
CB_SWA_Q, CB_SWA_K, CB_SWA_V = 12, 16, 18
CB_CQ, CB_CKV = 20, 23
CB_GQA_Q, CB_GQA_K, CB_GQA_V = 24, 28, 30
CB_KPE = 32
PCOLS = 33 * LANE
IN_COLS = 4160


def _cparams(sem=None, **kw):
    return pltpu.CompilerParams(dimension_semantics=sem, vmem_limit_bytes=VMEM_LIMIT, **kw)


def _pick(n, target, mult=LANE):
    best = None
    for d in range(mult, min(n, target) + 1, mult):
        if n % d == 0:
            best = d
    return n if best is None else best


def _mm(a, b, *, mode="nn", out_dtype=F32, a_off=0, a_k=None, tm=1408, tn=1408, tk=2816, exact=False, add=None,
        stack=None, split4=False, rows=None, rides=(), b_layer=None, name):
    b_shape = b.shape if b_layer is None else b.shape[1:]
    if mode == "tn":
        K, M = a.shape
        K2, N = b_shape
    elif mode == "nn":
        M, K = a.shape
        K2, N = b_shape
    else:
        M, K = a.shape
        N, K2 = b_shape
    if a_k is not None:
        K = a_k
    if rows is not None:
        if mode == "tn":
            assert rows <= min(K, K2)
            K = K2 = rows
        else:
            assert rows <= M
            M = rows
    assert K == K2, (a.shape, b.shape, mode)
    m_mult = LANE if mode == "tn" else 16
    n_cols = N // 4 if split4 else N
    bm, bn, bk = _pick(M, tm, m_mult), _pick(n_cols, tn), _pick(K, tk)
    sa, sb, so = a.dtype.itemsize, b.dtype.itemsize, jnp.dtype(out_dtype).itemsize

    def vmem_estimate():
        acc = bm * bn * 4 if K // bk > 1 else 0
        return 2 * (bm * bk * sa + bk * bn * sb) + acc + 2 * bm * bn * so + (2 * bm * bn * 4 if add is not None else 0)

    while vmem_estimate() > MM_VMEM_BUDGET:
        if bm >= bn and _pick(M, bm - 1, m_mult) < bm:
            bm = _pick(M, bm - 1, m_mult)
        elif _pick(n_cols, bn - 1) < bn:
            bn = _pick(n_cols, bn - 1)
        else:
            assert _pick(K, bk - 1) < bk, "no tiling fits VMEM"
            bk = _pick(K, bk - 1)
    assert a_off % bk == 0
    koff = a_off // bk
    nk = K // bk
    if mode == "tn":
        a_spec = pl.BlockSpec((bk, bm), lambda i, j, k: (k, i))
        b_spec = pl.BlockSpec((bk, bn), lambda i, j, k: (k, j))
        dims = (((0,), (0,)), ((), ()))
    elif mode == "nn":
        a_spec = pl.BlockSpec((bm, bk), lambda i, j, k: (i, k + koff))
        b_spec = pl.BlockSpec((bk, bn), lambda i, j, k: (k, j))
        dims = (((1,), (0,)), ((), ()))
    else:
        a_spec = pl.BlockSpec((bm, bk), lambda i, j, k: (i, k + koff))
        b_spec = pl.BlockSpec((bn, bk), lambda i, j, k: (j, k))
        dims = (((1,), (1,)), ((), ()))
    if b_layer is not None:
        b_block, b_index = b_spec.block_shape, b_spec.index_map
        b_spec = pl.BlockSpec((None,) + tuple(b_block), lambda i, j, k: (b_layer,) + tuple(b_index(i, j, k)))

    operands = [a, b]
    in_specs = [a_spec, b_spec]
    if add is not None:
        operands.append(add)
        in_specs.append(pl.BlockSpec((bm, bn), lambda i, j, k: (i, j)))
    aliases = {}
    if stack is None:
        out_spec = pl.BlockSpec((bm, bn), lambda i, j, k: (i, j))
        out_shape = jax.ShapeDtypeStruct((M, N), out_dtype)
    else:
        n_layers, layer, buf = stack
        if split4:
            nb = N // 4 // bn
            assert N % (4 * bn) == 0
            out_spec = pl.BlockSpec((None, None, bm, bn), lambda i, j, k: (layer, j // nb, i, j % nb))
            out_shape = jax.ShapeDtypeStruct((n_layers, 4, M, N // 4), out_dtype)
        else:
            out_spec = pl.BlockSpec((None, bm, bn), lambda i, j, k: (layer, i, j))
            out_shape = jax.ShapeDtypeStruct((n_layers, M, N), out_dtype)
        if buf is not None:
            aliases = {len(operands): 0}
            operands.append(buf)
            in_specs.append(pl.BlockSpec(memory_space=pl.ANY))
    has_add, has_buf = add is not None, bool(aliases)
    n_ride = sum(len(r.arrays) for r in rides)
    aliases.update(_ride_aliases(rides, len(operands), 1))
    grid = (M // bm, N // bn, nk)

    def body(*refs):
        a_ref, b_ref = refs[:2]
        add_ref = refs[2] if has_add else None
        base = 2 + has_add + has_buf
        o_ref = refs[base + n_ride]
        scratch = refs[base + 2 * n_ride + 1:]
        if rides:
            ride_groups = _ride_split(rides, refs[base:base + n_ride], refs[base + n_ride + 1:base + 2 * n_ride + 1],
                                      scratch[1 if nk > 1 else 0:])
            steps = [pl.program_id(d) for d in range(3)]
            pl.when((steps[0] == 0) & (steps[1] == 0) & (steps[2] == 0))(lambda: _ride_start(rides, ride_groups))
        compute(a_ref, b_ref, add_ref, o_ref, scratch[0] if nk > 1 else None)
        if rides:
            pl.when((steps[0] == grid[0] - 1) & (steps[1] == grid[1] - 1) & (steps[2] == grid[2] - 1))(
                lambda: _ride_wait(rides, ride_groups))

    def compute(a_ref, b_ref, add_ref, o_ref, acc_ref):
        if exact:
            prod = lax.dot_general(a_ref[...].astype(F32), b_ref[...].astype(F32), dims,
                                   precision=lax.Precision.HIGHEST, preferred_element_type=F32)
        else:
            prod = lax.dot_general(a_ref[...].astype(BF16), b_ref[...].astype(BF16), dims, preferred_element_type=F32)

        def finish(res):
            if has_add:
                res = res + add_ref[...].astype(F32)
            o_ref[...] = res.astype(o_ref.dtype)

        if nk == 1:
            finish(prod)
            return
        k = pl.program_id(2)

        @pl.when(k == 0)
        def _():
            acc_ref[...] = prod

        @pl.when((k > 0) & (k < nk - 1))
        def _():
            acc_ref[...] += prod

        @pl.when(k == nk - 1)
        def _():
            finish(acc_ref[...] + prod)

    outs = pl.pallas_call(
        body, name=name, grid=grid,
        in_specs=in_specs + [_ANY] * n_ride, out_specs=[out_spec] + [_ANY] * n_ride,
        out_shape=[out_shape] + [s for r in rides for s in r.out_shapes()],
        scratch_shapes=([pltpu.VMEM((bm, bn), F32)] if nk > 1 else []) + _ride_scratch(rides),
        input_output_aliases=aliases,
        compiler_params=_cparams(("arbitrary",) * 3 if rides else ("parallel", "parallel", "arbitrary")),
    )(*operands, *[a for r in rides for a in r.arrays])
    if not rides:
        return outs[0]
    return outs[0], _ride_outputs(rides, outs[1:])


def _row_block(T, S):
    return _pick(math.gcd(T, S), 256, 16)


def _ln_stats(x):
    mu = jnp.mean(x, axis=-1, keepdims=True)
    xc = x - mu
    var = jnp.mean(xc * xc, axis=-1, keepdims=True)
    rstd = lax.rsqrt(var + EPS)
    return xc * rstd, rstd


def _ln_bwd(dxhat, xhat, rstd):
    m1 = jnp.mean(dxhat, axis=-1, keepdims=True)
    m2 = jnp.mean(dxhat * xhat, axis=-1, keepdims=True)
    return rstd * (dxhat - m1 - xhat * m2)


def _sel(ref, is_ctx):
    return jnp.where(is_ctx, ref[1:2, :], ref[0:1, :])


def _mod_fwd(x, shift, scale, S, *, rows=None, name):
    T, D = (x.shape[0] if rows is None else rows), x.shape[1]
    bt = _row_block(T, S)

    def body(x_ref, sh_ref, sc_ref, o_ref):
        is_ctx = pl.program_id(0) * bt >= S
        xhat, _ = _ln_stats(x_ref[...])
        o_ref[...] = (xhat * (1.0 + _sel(sc_ref, is_ctx)) + _sel(sh_ref, is_ctx)).astype(o_ref.dtype)

    return pl.pallas_call(
        body, name=name, grid=(T // bt,),
        in_specs=[pl.BlockSpec((bt, D), lambda i: (i, 0)), pl.BlockSpec((2, D), lambda i: (0, 0)),
                  pl.BlockSpec((2, D), lambda i: (0, 0))],
        out_specs=pl.BlockSpec((bt, D), lambda i: (i, 0)),
        out_shape=jax.ShapeDtypeStruct((T, D), BF16),
        compiler_params=_cparams(("parallel",)),
    )(x, shift, scale)


def _acc_groups(acc_ref, row, val, is_ctx):
    f = jnp.where(is_ctx, 1.0, 0.0).astype(F32)
    acc_ref[row:row + 1, :] += val * (1.0 - f)
    acc_ref[row + 1:row + 2, :] += val * f


def _mod_bwd(x, dh, scale, dx_in, S, *, name):
    T, D = dh.shape
    bt = _row_block(T, S)
    in_blocks = dx_in.shape[0] // bt

    def body(x_ref, dh_ref, sc_ref, dxi_ref, dx_ref, acc_ref):
        i = pl.program_id(0)
        is_ctx = i * bt >= S

        @pl.when(i == 0)
        def _():
            acc_ref[...] = jnp.zeros_like(acc_ref)

        xhat, rstd = _ln_stats(x_ref[...])
        dh = dh_ref[...].astype(F32)
        dxhat = dh * (1.0 + _sel(sc_ref, is_ctx))
        dxi = dxi_ref[...] if in_blocks * bt == T else jnp.where(i < in_blocks, dxi_ref[...], 0.0)
        dx_ref[...] = dxi + _ln_bwd(dxhat, xhat, rstd)
        _acc_groups(acc_ref, 0, jnp.sum(dh, axis=0, keepdims=True), is_ctx)
        _acc_groups(acc_ref, 2, jnp.sum(dh * xhat, axis=0, keepdims=True), is_ctx)

    return pl.pallas_call(
        body, name=name, grid=(T // bt,),
        in_specs=[pl.BlockSpec((bt, D), lambda i: (i, 0)), pl.BlockSpec((bt, D), lambda i: (i, 0)),
                  pl.BlockSpec((2, D), lambda i: (0, 0)),
                  pl.BlockSpec((bt, D), lambda i: (jnp.minimum(i, in_blocks - 1), 0))],
        out_specs=[pl.BlockSpec((bt, D), lambda i: (i, 0)), pl.BlockSpec((8, D), lambda i: (0, 0))],
        out_shape=[jax.ShapeDtypeStruct((T, D), F32), jax.ShapeDtypeStruct((8, D), F32)],
        compiler_params=_cparams(("arbitrary",)),
    )(x, dh, scale, dx_in)


def _res_fwd(x, z, gate, lg, lb, S, *, name):
    T, D = z.shape
    bt = _row_block(T, S)

    def body(x_ref, z_ref, g_ref, lg_ref, lb_ref, o_ref):
        is_ctx = pl.program_id(0) * bt >= S
        u = DEEPNORM_ALPHA * x_ref[...] + _sel(g_ref, is_ctx) * z_ref[...]
        uhat, _ = _ln_stats(u)
        o_ref[...] = uhat * lg_ref[...] + lb_ref[...]

    row = pl.BlockSpec((bt, D), lambda i: (i, 0))
    return pl.pallas_call(
        body, name=name, grid=(T // bt,),
        in_specs=[row, row, pl.BlockSpec((2, D), lambda i: (0, 0)), pl.BlockSpec((1, D), lambda i: (0, 0)),
                  pl.BlockSpec((1, D), lambda i: (0, 0))],
        out_specs=row,
        out_shape=jax.ShapeDtypeStruct((T, D), F32),
        compiler_params=_cparams(("parallel",)),
    )(x, z, gate, lg, lb)


def _res_bwd(x, z, gate, lg, dy, S, *, name):
    T, D = z.shape
    bt = _row_block(T, S)

    def body(x_ref, z_ref, g_ref, lg_ref, dy_ref, dx_ref, dz_ref, acc_ref):
        i = pl.program_id(0)
        is_ctx = i * bt >= S

        @pl.when(i == 0)
        def _():
            acc_ref[...] = jnp.zeros_like(acc_ref)

        gate_v = _sel(g_ref, is_ctx)
        zv = z_ref[...]
        u = DEEPNORM_ALPHA * x_ref[...] + gate_v * zv
        uhat, rstd = _ln_stats(u)
        dyv = dy_ref[...]
        du = _ln_bwd(dyv * lg_ref[...], uhat, rstd)
        dx_ref[...] = DEEPNORM_ALPHA * du
        dz_ref[...] = (gate_v * du).astype(dz_ref.dtype)
        _acc_groups(acc_ref, 0, jnp.sum(du * zv, axis=0, keepdims=True), is_ctx)
        acc_ref[2:3, :] += jnp.sum(dyv * uhat, axis=0, keepdims=True)
        acc_ref[3:4, :] += jnp.sum(dyv, axis=0, keepdims=True)

    row = pl.BlockSpec((bt, D), lambda i: (i, 0))
    return pl.pallas_call(
        body, name=name, grid=(T // bt,),
        in_specs=[row, row, pl.BlockSpec((2, D), lambda i: (0, 0)), pl.BlockSpec((1, D), lambda i: (0, 0)), row],
        out_specs=[row, row, pl.BlockSpec((8, D), lambda i: (0, 0))],
        out_shape=[jax.ShapeDtypeStruct((T, D), F32), jax.ShapeDtypeStruct((T, D), BF16),
                   jax.ShapeDtypeStruct((8, D), F32)],
        compiler_params=_cparams(("arbitrary",)),
    )(x, z, gate, lg, dy)


def _loss_fwd_bwd(y, target, S, *, name):
    T, D = y.shape
    bt = _row_block(T, S)
    n_lat = S // bt

    def body(y_ref, t_ref, dy_ref, l_ref):
        i = pl.program_id(0)

        @pl.when(i == 0)
        def _():
            l_ref[...] = jnp.zeros_like(l_ref)

        keep = jnp.where(i * bt >= S, 0.0, 1.0).astype(F32)
        err = (y_ref[...] - t_ref[...]) * keep
        dy_ref[...] = err * (1.0 / D)
        l_ref[...] += jnp.sum(err * err) * (0.5 / D)

    return pl.pallas_call(
        body, name=name, grid=(T // bt,),
        in_specs=[pl.BlockSpec((bt, D), lambda i: (i, 0)),
                  pl.BlockSpec((bt, D), lambda i: (jnp.minimum(i, n_lat - 1), 0))],
        out_specs=[pl.BlockSpec((bt, D), lambda i: (i, 0)), pl.BlockSpec((8, LANE), lambda i: (0, 0))],
        out_shape=[jax.ShapeDtypeStruct((T, D), F32), jax.ShapeDtypeStruct((8, LANE), F32)],
        compiler_params=_cparams(("arbitrary",)),
    )(y, target)


def _rope_tables(S, C, dim):
    half = dim // 4
    t = jnp.arange(S)
    row = (t // GRID_W).astype(F32)
    col = (t % GRID_W).astype(F32)
    inv = ROPE_THETA ** (-jnp.arange(half, dtype=F32) / half)
    ar, ac = row[:, None] * inv[None, :], col[:, None] * inv[None, :]
    cos = jnp.concatenate([jnp.cos(ar), jnp.cos(ar), jnp.cos(ac), jnp.cos(ac)], axis=1)
    ss = jnp.concatenate([-jnp.sin(ar), jnp.sin(ar), -jnp.sin(ac), jnp.sin(ac)], axis=1)
    cos = jnp.pad(cos, ((0, C), (0, LANE - dim)), constant_values=1.0)
    ss = jnp.pad(ss, ((0, C), (0, LANE - dim)))
    return cos, ss


def _rope(x, cos, ss, half):
    lane = lax.broadcasted_iota(jnp.int32, x.shape, 1)
    first = (lane % (2 * half)) < half
    partner = jnp.where(first, pltpu.roll(x, LANE - half, 1), pltpu.roll(x, half, 1))
    return x * cos + partner * ss


def _rms(x):
    r = lax.rsqrt(jnp.mean(x * x, axis=-1, keepdims=True) + EPS)
    return x * r, r


_CAST_BLOCKS = tuple(range(0, 12)) + (18, 19, 30, 31)
_ROPE_BLOCKS = tuple(range(12, 18))
_GQA_Q_BLOCKS = tuple(range(24, 28))
_GQA_K_BLOCKS = (28, 29)


def _prep_fwd(p, tabs, gq, gk, mq, mkv, S, *, name):
    T = p.shape[0]
    bt = _row_block(T, S)
    cA, sA, cP, sP = tabs

    def body(p_ref, cA_ref, sA_ref, cP_ref, sP_ref, gq_ref, gk_ref, mq_ref, mkv_ref, o_ref):
        def blk(b):
            return p_ref[:, b * LANE:(b + 1) * LANE]

        def put(b, val):
            o_ref[:, b * LANE:(b + 1) * LANE] = val.astype(o_ref.dtype)

        cA_v, sA_v = cA_ref[...], sA_ref[...]
        for b in _CAST_BLOCKS:
            put(b, blk(b))
        for b in _ROPE_BLOCKS:
            put(b, _rope(blk(b), cA_v, sA_v, 32))
        for b in _GQA_Q_BLOCKS:
            put(b, _rope(_rms(blk(b))[0] * gq_ref[...], cA_v, sA_v, 32))
        for b in _GQA_K_BLOCKS:
            put(b, _rope(_rms(blk(b))[0] * gk_ref[...], cA_v, sA_v, 32))
        put(CB_KPE, _rope(blk(CB_KPE), cP_ref[...], sP_ref[...], 16))
        cq = p_ref[:, CB_CQ * LANE:CB_CKV * LANE]
        o_ref[:, CB_CQ * LANE:CB_CKV * LANE] = (_rms(cq)[0] * mq_ref[...]).astype(o_ref.dtype)
        put(CB_CKV, _rms(blk(CB_CKV))[0] * mkv_ref[...])

    row128 = pl.BlockSpec((bt, LANE), lambda i: (i, 0))
    vec = lambda n: pl.BlockSpec((1, n), lambda i: (0, 0))
    return pl.pallas_call(
        body, name=name, grid=(T // bt,),
        in_specs=[pl.BlockSpec((bt, PCOLS), lambda i: (i, 0)), row128, row128, row128, row128,
                  vec(LANE), vec(LANE), vec(MLA_Q_LORA), vec(LANE)],
        out_specs=pl.BlockSpec((bt, PCOLS), lambda i: (i, 0)),
        out_shape=jax.ShapeDtypeStruct((T, PCOLS), BF16),
        compiler_params=_cparams(("parallel",)),
    )(p, cA, sA, cP, sP, gq, gk, mq, mkv)


def _prep_bwd(p, grads, tabs, gq, gk, mq, mkv, S, *, name):
    T = p.shape[0]
    bt = _row_block(T, S)
    cA, sA, cP, sP = tabs
    arrays = []
    where = {}
    for key, (arr, cb) in grads.items():
        idx = next((n for n, a in enumerate(arrays) if a is arr), None)
        if idx is None:
            arrays.append(arr)
            idx = len(arrays) - 1
        where[key] = (idx, cb)
    ng = len(arrays)

    def body(*refs):
        p_ref, cA_ref, sA_ref, cP_ref, sP_ref, gq_ref, gk_ref, mq_ref, mkv_ref = refs[:9]
        g_refs = refs[9:9 + ng]
        o_ref, acc_ref = refs[9 + ng:]
        i = pl.program_id(0)

        @pl.when(i == 0)
        def _():
            acc_ref[...] = jnp.zeros_like(acc_ref)

        def blk(b):
            return p_ref[:, b * LANE:(b + 1) * LANE]

        def grad(b, width=LANE):
            idx, cb = where[b]
            return g_refs[idx][:, cb * LANE:cb * LANE + width].astype(F32)

        def put(b, val):
            o_ref[:, b * LANE:(b + 1) * LANE] = val.astype(o_ref.dtype)

        def rms_bwd(x, dy, g, row, width):
            n, r = _rms(x)
            acc_ref[row:row + 1, 0:width] += jnp.sum(dy * n, axis=0, keepdims=True)
            dn = dy * g
            return r * (dn - n * jnp.mean(dn * n, axis=-1, keepdims=True))

        cA_v, sA_v = cA_ref[...], sA_ref[...]
        for b in _CAST_BLOCKS:
            put(b, grad(b))
        for b in _ROPE_BLOCKS:
            put(b, _rope(grad(b), cA_v, -sA_v, 32))
        for b in _GQA_Q_BLOCKS:
            put(b, rms_bwd(blk(b), _rope(grad(b), cA_v, -sA_v, 32), gq_ref[...], 0, LANE))
        for b in _GQA_K_BLOCKS:
            put(b, rms_bwd(blk(b), _rope(grad(b), cA_v, -sA_v, 32), gk_ref[...], 1, LANE))
        put(CB_KPE, _rope(grad(CB_KPE), cP_ref[...], -sP_ref[...], 16))
        dcq = rms_bwd(p_ref[:, CB_CQ * LANE:CB_CKV * LANE], grad(CB_CQ, MLA_Q_LORA), mq_ref[...], 2, MLA_Q_LORA)
        o_ref[:, CB_CQ * LANE:CB_CKV * LANE] = dcq.astype(o_ref.dtype)
        put(CB_CKV, rms_bwd(blk(CB_CKV), grad(CB_CKV), mkv_ref[...], 3, LANE))

    row128 = pl.BlockSpec((bt, LANE), lambda i: (i, 0))
    vec = lambda n: pl.BlockSpec((1, n), lambda i: (0, 0))
    g_specs = [pl.BlockSpec((bt, a.shape[1]), lambda i: (i, 0)) for a in arrays]
    return pl.pallas_call(
        body, name=name, grid=(T // bt,),
        in_specs=[pl.BlockSpec((bt, PCOLS), lambda i: (i, 0)), row128, row128, row128, row128,
                  vec(LANE), vec(LANE), vec(MLA_Q_LORA), vec(LANE)] + g_specs,
        out_specs=[pl.BlockSpec((bt, PCOLS), lambda i: (i, 0)), pl.BlockSpec((8, MLA_Q_LORA), lambda i: (0, 0))],
        out_shape=[jax.ShapeDtypeStruct((T, PCOLS), BF16), jax.ShapeDtypeStruct((8, MLA_Q_LORA), F32)],
        compiler_params=_cparams(("arbitrary",)),
    )(p, cA, sA, cP, sP, gq, gk, mq, mkv, *arrays)


def _pe_rope(qm, tabs, S, sign, out_dtype, *, name):
    T, N = qm.shape
    bt = _row_block(T, S)
    cP, sP = tabs[2], tabs[3]

    def body(x_ref, c_ref, s_ref, o_ref):
        for b in range(MLA_HEADS):
            o_ref[:, b * LANE:(b + 1) * LANE] = x_ref[:, b * LANE:(b + 1) * LANE].astype(o_ref.dtype)
        for b in range(MLA_HEADS, 2 * MLA_HEADS):
            x = x_ref[:, b * LANE:(b + 1) * LANE].astype(F32)
            o_ref[:, b * LANE:(b + 1) * LANE] = _rope(x, c_ref[...], sign * s_ref[...], 16).astype(o_ref.dtype)

    row128 = pl.BlockSpec((bt, LANE), lambda i: (i, 0))
    return pl.pallas_call(
        body, name=name, grid=(T // bt,),
        in_specs=[pl.BlockSpec((bt, N), lambda i: (i, 0)), row128, row128],
        out_specs=pl.BlockSpec((bt, N), lambda i: (i, 0)),
        out_shape=jax.ShapeDtypeStruct((T, N), out_dtype),
        compiler_params=_cparams(("parallel",)),
    )(qm, cP, sP)


def _dot_nt(a, b):
    return lax.dot_general(a, b, (((1,), (1,)), ((), ())), preferred_element_type=F32)


def _dot_tn(a, b):
    return lax.dot_general(a, b, (((0,), (0,)), ((), ())), preferred_element_type=F32)


def _dot(a, b):
    return jnp.dot(a, b, preferred_element_type=F32)


def _window_fns(band, S, n_var):
    n_lat = S // BQ
    if band is None:
        return None
    reach, span = band

    def fns(j):
        start = jnp.clip(j - reach, 0, n_lat - span)
        return start, jnp.clip(j - start, 0, n_var - 1)

    return fns


class _AttnCfg:
    def __init__(self, *, Hkv, G, S, C, band, scale, n_var=0, bias_per_head=False, has_sink=False, two=False, bq=BQ,
                 ctx_queries=True):
        self.Hkv, self.G, self.S, self.C, self.band, self.scale = Hkv, G, S, C, band, scale
        self.n_var, self.bias_per_head, self.has_sink, self.two = n_var, bias_per_head, has_sink, two
        self.W = S if band is None else band[1] * BQ
        self.T = S + C
        self.bq, self.ctx_queries = bq, ctx_queries
        assert band is None or bq == BQ
        assert S % bq == 0 and C % bq == 0


def _attn_probs(cfg, j, q_ref, k_ref, q2_ref, k2_ref, bias_ref, sink_ref):
    G, S, C, W = cfg.G, cfg.S, cfg.C, cfg.W
    is_ctx = j * cfg.bq >= S
    if cfg.band is None:
        off, var = 0, 0
    else:
        start, var = _window_fns(cfg.band, S, cfg.n_var)(j)
        off = pl.multiple_of(start * BQ, BQ)
    qt = q_ref[...]
    qs = jnp.concatenate([qt[:, g * LANE:(g + 1) * LANE] for g in range(G)], axis=0) if G > 1 else qt
    kw = k_ref[pl.ds(off, W), :]
    kc = k_ref[pl.ds(S, C), :]
    s_w = _dot_nt(qs, kw)
    s_c = _dot_nt(qs, kc)
    q2s = k2w = k2c = None
    if cfg.two:
        q2s = q2_ref[...]
        k2w = k2_ref[pl.ds(off, W), :]
        k2c = k2_ref[pl.ds(S, C), :]
        s_w = s_w + _dot_nt(q2s, k2w)
        s_c = s_c + _dot_nt(q2s, k2c)
    operands = (off, var, qs, kw, kc, q2s, k2w, k2c)
    if not cfg.n_var and not cfg.has_sink:
        if cfg.ctx_queries:
            s_w = jnp.where(is_ctx, NEG, s_w)
        m = jnp.maximum(jnp.max(s_w, axis=-1, keepdims=True), jnp.max(s_c, axis=-1, keepdims=True))
        c2 = cfg.scale * math.log2(math.e)
        e_w = jnp.exp2((s_w - m) * c2)
        e_c = jnp.exp2((s_c - m) * c2)
        inv = 1.0 / (jnp.sum(e_w, axis=-1, keepdims=True) + jnp.sum(e_c, axis=-1, keepdims=True))
        return e_w * inv, e_c * inv, None, operands
    s_w = s_w * cfg.scale
    s_c = s_c * cfg.scale
    if cfg.n_var:
        b = bias_ref[0, pl.ds(var, 1)][0]
        s_w = s_w + (jnp.concatenate([b] * G, axis=0) if G > 1 else b)
    if cfg.ctx_queries:
        s_w = jnp.where(is_ctx, NEG, s_w)
    m = jnp.maximum(jnp.max(s_w, axis=-1, keepdims=True), jnp.max(s_c, axis=-1, keepdims=True))
    if cfg.has_sink:
        sink = sink_ref[0][:, 0:1]
        m = jnp.maximum(m, sink)
    e_w = jnp.exp(s_w - m)
    e_c = jnp.exp(s_c - m)
    l = jnp.sum(e_w, axis=-1, keepdims=True) + jnp.sum(e_c, axis=-1, keepdims=True)
    p_s = None
    if cfg.has_sink:
        e_s = jnp.exp(sink - m)
        l = l + e_s
    inv = 1.0 / l
    if cfg.has_sink:
        p_s = e_s * inv
    return e_w * inv, e_c * inv, p_s, operands


def _attn_specs(cfg, q_cb, k_cb, v_cb, q2_cb, k2_cb):
    G, T, bq = cfg.G, cfg.T, cfg.bq
    specs = [pl.BlockSpec((bq, G * LANE), lambda h, j: (j, q_cb // G + h)),
             pl.BlockSpec((T, LANE), lambda h, j: (0, k_cb + h)),
             pl.BlockSpec((T, LANE), lambda h, j: (0, v_cb + h))]
    if cfg.two:
        specs += [pl.BlockSpec((bq, LANE), lambda h, j: (j, q2_cb + h)),
                  pl.BlockSpec((T, LANE), lambda h, j: (0, k2_cb))]
    if cfg.n_var:
        if cfg.bias_per_head:
            specs.append(pl.BlockSpec((1, cfg.n_var, BQ, cfg.W), lambda h, j: (h, 0, 0, 0)))
        else:
            specs.append(pl.BlockSpec((1, cfg.n_var, BQ, cfg.W), lambda h, j: (0, 0, 0, 0)))
    if cfg.has_sink:
        specs.append(pl.BlockSpec((1, G * BQ, LANE), lambda h, j: (h, 0, 0)))
    return specs


def _attn_unpack(cfg, refs):
    refs = list(refs)
    q_ref, k_ref, v_ref = refs[:3]
    n = 3
    q2_ref = k2_ref = bias_ref = sink_ref = None
    if cfg.two:
        q2_ref, k2_ref = refs[n:n + 2]
        n += 2
    if cfg.n_var:
        bias_ref = refs[n]
        n += 1
    if cfg.has_sink:
        sink_ref = refs[n]
        n += 1
    return (q_ref, k_ref, v_ref, q2_ref, k2_ref, bias_ref, sink_ref), refs[n:]


def _attn_fwd(cfg, q, q_cb, k, k_cb, v, v_cb, *, q2=None, q2_cb=0, k2=None, k2_cb=0, bias=None, sink=None, rides=(),
              into=None, name):
    G, T, S, C, W = cfg.G, cfg.T, cfg.S, cfg.C, cfg.W
    assert q_cb % G == 0
    operands = [q, k, v] + ([q2, k2] if cfg.two else []) + ([bias] if cfg.n_var else []) + ([sink] if cfg.has_sink else [])

    bq = cfg.bq
    n_ride = sum(len(r.arrays) for r in rides)
    n_q = T // bq
    has_buf = into is not None and into[0] is not None
    col0 = 0 if into is None else into[1]
    width = cfg.Hkv * G * LANE if into is None else into[2] * LANE
    assert col0 % G == 0

    def body(*refs):
        (q_ref, k_ref, v_ref, q2_ref, k2_ref, bias_ref, sink_ref), rest = _attn_unpack(cfg, refs)
        rest = rest[:n_ride] + rest[n_ride + has_buf:]
        o_ref = rest[n_ride]
        ride_groups = _ride_split(rides, rest[:n_ride], rest[n_ride + 1:2 * n_ride + 1], rest[2 * n_ride + 1:])
        h = pl.program_id(0)
        j = pl.program_id(1)
        if rides:
            pl.when((h == 0) & (j == 0))(lambda: _ride_start(rides, ride_groups))

        def block():
            p_w, p_c, _, (off, _, _, _, _, _, _, _) = _attn_probs(cfg, j, q_ref, k_ref, q2_ref, k2_ref, bias_ref, sink_ref)
            o = _dot(p_w.astype(BF16), v_ref[pl.ds(off, W), :]) + _dot(p_c.astype(BF16), v_ref[pl.ds(S, C), :])
            for g in range(G):
                o_ref[:, g * LANE:(g + 1) * LANE] = o[g * bq:(g + 1) * bq].astype(o_ref.dtype)

        if cfg.ctx_queries:
            block()
        else:
            pl.when(j * bq < S)(block)

            @pl.when(j * bq >= S)
            def _():
                o_ref[...] = jnp.zeros_like(o_ref)

        if rides:
            pl.when((h == cfg.Hkv - 1) & (j == n_q - 1))(lambda: _ride_wait(rides, ride_groups))

    aliases = _ride_aliases(rides, len(operands), 1)
    if has_buf:
        aliases[len(operands) + n_ride] = 0
    outs = pl.pallas_call(
        body, name=name, grid=(cfg.Hkv, n_q),
        in_specs=_attn_specs(cfg, q_cb, k_cb, v_cb, q2_cb, k2_cb) + [_ANY] * (n_ride + has_buf),
        out_specs=[pl.BlockSpec((bq, G * LANE), lambda h, j: (j, col0 // G + h))] + [_ANY] * n_ride,
        out_shape=[jax.ShapeDtypeStruct((T, width), BF16)] + [s for r in rides for s in r.out_shapes()],
        scratch_shapes=_ride_scratch(rides),
        input_output_aliases=aliases,
        compiler_params=_cparams(("arbitrary", "arbitrary") if rides else ("parallel", "parallel")),
    )(*operands, *[a for r in rides for a in r.arrays], *([into[0]] if has_buf else []))
    if not rides:
        return outs[0]
    return outs[0], _ride_outputs(rides, outs[1:])


def _attn_bwd(cfg, q, q_cb, k, k_cb, v, v_cb, do, do_cb, *, q2=None, q2_cb=0, k2=None, k2_cb=0, bias=None, sink=None,
              want_dbias=False, dq_dtype=F32, rides=(), name):
    G, T, S, C, W, Hkv = cfg.G, cfg.T, cfg.S, cfg.C, cfg.W, cfg.Hkv
    assert q_cb % G == 0 and do_cb % G == 0 and not (want_dbias and G > 1)
    operands = [q, k, v] + ([q2, k2] if cfg.two else []) + ([bias] if cfg.n_var else []) + ([sink] if cfg.has_sink else [])
    operands.append(do)
    in_specs = _attn_specs(cfg, q_cb, k_cb, v_cb, q2_cb, k2_cb)
    bq = cfg.bq
    do_blocks = do.shape[0] // bq
    assert do.shape[0] == T or (do.shape[0] == S and not cfg.ctx_queries)
    in_specs.append(pl.BlockSpec((bq, G * LANE), lambda h, j: (jnp.minimum(j, do_blocks - 1), do_cb // G + h)))

    out_specs = [pl.BlockSpec((bq, G * LANE), lambda h, j: (j, h)),
                 pl.BlockSpec((T, LANE), lambda h, j: (0, h)),
                 pl.BlockSpec((T, LANE), lambda h, j: (0, h))]
    out_shape = [jax.ShapeDtypeStruct((T, Hkv * G * LANE), dq_dtype),
                 jax.ShapeDtypeStruct((T, Hkv * LANE), F32),
                 jax.ShapeDtypeStruct((T, Hkv * LANE), F32)]
    if cfg.two:
        out_specs += [pl.BlockSpec((bq, LANE), lambda h, j: (j, h)), pl.BlockSpec((T, LANE), lambda h, j: (0, 0))]
        out_shape += [jax.ShapeDtypeStruct((T, Hkv * LANE), dq_dtype), jax.ShapeDtypeStruct((T, LANE), F32)]
    if want_dbias:
        out_specs.append(pl.BlockSpec((1, cfg.n_var, BQ, W), lambda h, j: (h, 0, 0, 0)))
        out_shape.append(jax.ShapeDtypeStruct((Hkv, cfg.n_var, BQ, W), F32))
    if cfg.has_sink:
        out_specs.append(pl.BlockSpec((1, G * BQ, LANE), lambda h, j: (h, 0, 0)))
        out_shape.append(jax.ShapeDtypeStruct((Hkv, G * BQ, LANE), F32))

    n_ride = sum(len(r.arrays) for r in rides)
    operands += [a for r in rides for a in r.arrays]
    in_specs += [_ANY] * n_ride
    out_specs += [_ANY] * n_ride
    out_shape += [s for r in rides for s in r.out_shapes()]
    n_q = T // bq

    def body(*refs):
        (q_ref, k_ref, v_ref, q2_ref, k2_ref, bias_ref, sink_ref), rest = _attn_unpack(cfg, refs)
        do_ref, ride_in = rest[0], rest[1:1 + n_ride]
        dq_ref, dk_ref, dv_ref = rest[1 + n_ride:4 + n_ride]
        rest = rest[4 + n_ride:]
        dq2_ref = dk2_ref = dbias_ref = dsink_ref = None
        if cfg.two:
            dq2_ref, dk2_ref = rest[:2]
            rest = rest[2:]
        if want_dbias:
            dbias_ref = rest[0]
            rest = rest[1:]
        if cfg.has_sink:
            dsink_ref = rest[0]
            rest = rest[1:]
        ride_groups = _ride_split(rides, ride_in, rest[:n_ride], rest[n_ride:])
        h = pl.program_id(0)
        j = pl.program_id(1)
        if rides:
            pl.when((h == 0) & (j == 0))(lambda: _ride_start(rides, ride_groups))

        @pl.when(j == 0)
        def _():
            dk_ref[...] = jnp.zeros_like(dk_ref)
            dv_ref[...] = jnp.zeros_like(dv_ref)
            if want_dbias:
                dbias_ref[...] = jnp.zeros_like(dbias_ref)
            if cfg.has_sink:
                dsink_ref[...] = jnp.zeros_like(dsink_ref)

        if cfg.two:
            @pl.when((j == 0) & (h == 0))
            def _():
                dk2_ref[...] = jnp.zeros_like(dk2_ref)

        def block():
            p_w, p_c, p_s, (off, var, qs, kw, kc, q2s, k2w, k2c) = _attn_probs(
                cfg, j, q_ref, k_ref, q2_ref, k2_ref, bias_ref, sink_ref)
            dot_ = do_ref[...]
            dos = jnp.concatenate([dot_[:, g * LANE:(g + 1) * LANE] for g in range(G)], axis=0) if G > 1 else dot_
            dos = dos.astype(BF16)
            vw = v_ref[pl.ds(off, W), :]
            vc = v_ref[pl.ds(S, C), :]
            dp_w = _dot_nt(dos, vw)
            dp_c = _dot_nt(dos, vc)
            delta = jnp.sum(p_w * dp_w, axis=-1, keepdims=True) + jnp.sum(p_c * dp_c, axis=-1, keepdims=True)
            ds_w = p_w * (dp_w - delta)
            ds_c = p_c * (dp_c - delta)
            if want_dbias:
                dbias_ref[0, pl.ds(var, 1)] += ds_w[None]
            if cfg.has_sink:
                dsink_ref[0] += jnp.broadcast_to(-(p_s * delta), (G * bq, LANE))
            dsw = (ds_w * cfg.scale).astype(BF16)
            dsc = (ds_c * cfg.scale).astype(BF16)
            dq = _dot(dsw, kw) + _dot(dsc, kc)
            for g in range(G):
                dq_ref[:, g * LANE:(g + 1) * LANE] = dq[g * bq:(g + 1) * bq].astype(dq_ref.dtype)
            dk_ref[pl.ds(off, W), :] += _dot_tn(dsw, qs)
            dk_ref[pl.ds(S, C), :] += _dot_tn(dsc, qs)
            dv_ref[pl.ds(off, W), :] += _dot_tn(p_w.astype(BF16), dos)
            dv_ref[pl.ds(S, C), :] += _dot_tn(p_c.astype(BF16), dos)
            if cfg.two:
                dq2_ref[...] = (_dot(dsw, k2w) + _dot(dsc, k2c)).astype(dq2_ref.dtype)
                dk2_ref[pl.ds(off, W), :] += _dot_tn(dsw, q2s)
                dk2_ref[pl.ds(S, C), :] += _dot_tn(dsc, q2s)

        if cfg.ctx_queries:
            block()
        else:
            pl.when(j * bq < S)(block)

            @pl.when(j * bq >= S)
            def _():
                dq_ref[...] = jnp.zeros_like(dq_ref)
                if cfg.two:
                    dq2_ref[...] = jnp.zeros_like(dq2_ref)

        if rides:
            pl.when((h == Hkv - 1) & (j == n_q - 1))(lambda: _ride_wait(rides, ride_groups))

    outs = pl.pallas_call(
        body, name=name, grid=(Hkv, n_q),
        in_specs=in_specs, out_specs=out_specs, out_shape=out_shape,
        scratch_shapes=_ride_scratch(rides),
        compiler_params=_cparams(("arbitrary", "arbitrary")),
    )(*operands)
    if not rides:
        return outs
    return list(outs[:len(outs) - n_ride]) + [_ride_outputs(rides, outs[len(outs) - n_ride:])]


def _na_bias(rpb, S):
    H = rpb.shape[0]
    rows = S // GRID_W
    pad_l = GRID_W - 1 - (NA_WIN_C - 1)
    ext = jnp.concatenate([jnp.broadcast_to(rpb[:, :, :1], (H, 2 * NA_WIN_R - 1, pad_l)), rpb,
                           jnp.broadcast_to(rpb[:, :, -1:], (H, 2 * NA_WIN_R - 1, pad_l))], axis=2)
    by_col = jnp.stack([ext[:, :, GRID_W - 1 - qc:2 * GRID_W - 1 - qc] for qc in range(GRID_W)], axis=2)
    cq = np.arange(GRID_W)
    c0 = np.clip(cq - NA_WIN_C // 2, 0, GRID_W - NA_WIN_C)
    col_in = (cq[None, :] >= c0[:, None]) & (cq[None, :] < c0[:, None] + NA_WIN_C)
    n_lat = S // BQ
    neg_tile = jnp.full((H, GRID_W, GRID_W), NEG, F32)
    variants = []
    for v in range(5):
        j = {0: 0, 1: 1, 2: 2, 3: n_lat - 2, 4: n_lat - 1}[v]
        start = int(np.clip(j - 2, 0, n_lat - 5))
        assert j - start == v
        q_rows = []
        for qr in range(2):
            r = 2 * j + qr
            r0 = int(np.clip(r - NA_WIN_R // 2, 0, rows - NA_WIN_R))
            k_tiles = []
            for kr in range(10):
                krow = 2 * start + kr
                if r0 <= krow < r0 + NA_WIN_R:
                    k_tiles.append(jnp.where(col_in[None], by_col[:, krow - r + NA_WIN_R - 1], NEG))
                else:
                    k_tiles.append(neg_tile)
            q_rows.append(jnp.concatenate(k_tiles, axis=2))
        variants.append(jnp.concatenate(q_rows, axis=1))
    return jnp.stack(variants, axis=1)


def _swa_mask(S):
    qq = np.arange(BQ)[:, None]
    kk = np.arange(3 * BQ)[None, :]
    tiles = [np.where(np.abs(kk - v * BQ - qq) <= SWA_WINDOW, 0.0, NEG) for v in range(3)]
    return jnp.asarray(np.stack(tiles)[None], F32)


def _ffn_tiles(T, S, F):
    return _row_block(T, S), _pick(F, 1408)


def _halo_rows(dtype):
    return 8 * 4 // jnp.dtype(dtype).itemsize


def _halo_specs(T, bt, bf, dtype):
    hr = _halo_rows(dtype)
    nh = bt // hr
    return [pl.BlockSpec((bt, bf), lambda f, i: (i, f)),
            pl.BlockSpec((hr, bf), lambda f, i: (jnp.maximum(i * nh - 1, 0), f)),
            pl.BlockSpec((hr, bf), lambda f, i: (jnp.minimum((i + 1) * nh, T // hr - 1), f))]


def _neighbours(x, prev, nxt, i, bt, S, T):
    r = lax.broadcasted_iota(jnp.int32, x.shape, 0)
    g0 = i * bt
    first_open = jnp.logical_or(g0 == 0, g0 == S)
    last_open = jnp.logical_or(g0 + bt == S, g0 + bt == T)
    hr = prev.shape[0]
    before = jnp.where(r == 0, jnp.where(first_open, 0.0, prev[hr - 1:hr, :].astype(F32)), pltpu.roll(x, 1, 0))
    after = jnp.where(r == bt - 1, jnp.where(last_open, 0.0, nxt[0:1, :].astype(F32)), pltpu.roll(x, bt - 1, 0))
    return before, after


def _sigmoid(a):
    return 1.0 / (1.0 + jnp.exp(-a))


def _ffn_fwd(gp, u, cw, cb, S, *, name):
    T, F = gp.shape
    bt, bf = _ffn_tiles(T, S, F)

    def body(g_ref, gp_ref, gn_ref, u_ref, w_ref, b_ref, o_ref):
        i = pl.program_id(1)
        g = g_ref[...].astype(F32)
        before, after = _neighbours(g, gp_ref[...], gn_ref[...], i, bt, S, T)
        a = before * w_ref[0:1, :] + g * w_ref[1:2, :] + after * w_ref[2:3, :] + b_ref[...]
        o_ref[...] = (a * _sigmoid(a) * u_ref[...].astype(F32)).astype(o_ref.dtype)

    return pl.pallas_call(
        body, name=name, grid=(F // bf, T // bt),
        in_specs=_halo_specs(T, bt, bf, gp.dtype) + [pl.BlockSpec((bt, bf), lambda f, i: (i, f)),
                                              pl.BlockSpec((3, bf), lambda f, i: (0, f)),
                                              pl.BlockSpec((1, bf), lambda f, i: (0, f))],
        out_specs=pl.BlockSpec((bt, bf), lambda f, i: (i, f)),
        out_shape=jax.ShapeDtypeStruct((T, F), BF16),
        compiler_params=_cparams(("parallel", "parallel")),
    )(gp, gp, gp, u, cw, cb)


def _ffn_bwd_act(gp, u, da_out, cw, cb, S, *, name):
    T, F = gp.shape
    bt, bf = _ffn_tiles(T, S, F)

    def body(g_ref, gp_ref, gn_ref, u_ref, d_ref, w_ref, b_ref, da_ref, du_ref, acc_ref):
        i = pl.program_id(1)

        @pl.when(i == 0)
        def _():
            acc_ref[...] = jnp.zeros_like(acc_ref)

        g = g_ref[...].astype(F32)
        before, after = _neighbours(g, gp_ref[...], gn_ref[...], i, bt, S, T)
        a = before * w_ref[0:1, :] + g * w_ref[1:2, :] + after * w_ref[2:3, :] + b_ref[...]
        sig = _sigmoid(a)
        d = d_ref[...].astype(F32)
        du_ref[...] = (d * (a * sig)).astype(du_ref.dtype)
        da = d * u_ref[...].astype(F32) * (sig * (1.0 + a * (1.0 - sig)))
        da_ref[...] = da
        acc_ref[0:1, :] += jnp.sum(da * before, axis=0, keepdims=True)
        acc_ref[1:2, :] += jnp.sum(da * g, axis=0, keepdims=True)
        acc_ref[2:3, :] += jnp.sum(da * after, axis=0, keepdims=True)
        acc_ref[3:4, :] += jnp.sum(da, axis=0, keepdims=True)

    blk = pl.BlockSpec((bt, bf), lambda f, i: (i, f))
    return pl.pallas_call(
        body, name=name, grid=(F // bf, T // bt),
        in_specs=_halo_specs(T, bt, bf, gp.dtype) + [blk, blk,
                                              pl.BlockSpec((3, bf), lambda f, i: (0, f)),
                                              pl.BlockSpec((1, bf), lambda f, i: (0, f))],
        out_specs=[blk, blk, pl.BlockSpec((8, bf), lambda f, i: (0, f))],
        out_shape=[jax.ShapeDtypeStruct((T, F), F32), jax.ShapeDtypeStruct((T, F), BF16),
                   jax.ShapeDtypeStruct((8, F), F32)],
        compiler_params=_cparams(("parallel", "arbitrary")),
    )(gp, gp, gp, u, da_out, cw, cb)


def _ffn_bwd_conv(da, cw, S, *, name):
    T, F = da.shape
    bt, bf = _ffn_tiles(T, S, F)

    def body(d_ref, dp_ref, dn_ref, w_ref, o_ref):
        i = pl.program_id(1)
        d = d_ref[...]
        before, after = _neighbours(d, dp_ref[...], dn_ref[...], i, bt, S, T)
        o_ref[...] = (after * w_ref[0:1, :] + d * w_ref[1:2, :] + before * w_ref[2:3, :]).astype(o_ref.dtype)

    return pl.pallas_call(
        body, name=name, grid=(F // bf, T // bt),
        in_specs=_halo_specs(T, bt, bf, da.dtype) + [pl.BlockSpec((3, bf), lambda f, i: (0, f))],
        out_specs=pl.BlockSpec((bt, bf), lambda f, i: (i, f)),
        out_shape=jax.ShapeDtypeStruct((T, F), BF16),
        compiler_params=_cparams(("parallel", "parallel")),
    )(da, da, da, cw)


def _ew_rows(R, N, n_arrays):
    return _pick(R, max(16, EW_VMEM_BUDGET // (8 * n_arrays * N)), 16)


def _adam(w, g, m, v, *, name):
    lead = w.shape[:-2]
    R, N = w.shape[-2:]
    br = _ew_rows(R, N, 7)
    bc1 = 1.0 - ADAM_B1 ** ADAM_STEP
    bc2 = 1.0 - ADAM_B2 ** ADAM_STEP

    def body(w_ref, g_ref, m_ref, v_ref, d_ref, mo_ref, vo_ref):
        gv = g_ref[...]
        mn = ADAM_B1 * m_ref[...] + (1.0 - ADAM_B1) * gv
        vn = ADAM_B2 * v_ref[...] + (1.0 - ADAM_B2) * (gv * gv)
        mo_ref[...] = mn
        vo_ref[...] = vn
        d_ref[...] = -ADAM_LR * ((mn / bc1) / (jnp.sqrt(vn / bc2) + ADAM_EPS) + ADAM_WD * w_ref[...])

    if lead:
        blk = pl.BlockSpec((None, br, N), lambda l, i: (l, i, 0))
    else:
        blk = pl.BlockSpec((br, N), lambda i: (i, 0))
    shp = jax.ShapeDtypeStruct(w.shape, F32)
    return pl.pallas_call(
        body, name=name, grid=lead + (R // br,),
        in_specs=[blk, blk, blk, blk], out_specs=[blk, blk, blk], out_shape=[shp, shp, shp],
        compiler_params=_cparams(("parallel",) * (len(lead) + 1)),
    )(w, g, m, v)


def _sum_lead(x, out_dtype, *, name):
    n, R, N = x.shape
    br = _ew_rows(R, N, n + 1)

    def body(x_ref, o_ref):
        acc = x_ref[0].astype(F32)
        for k in range(1, n):
            acc = acc + x_ref[k].astype(F32)
        o_ref[...] = acc.astype(o_ref.dtype)

    return pl.pallas_call(
        body, name=name, grid=(R // br,),
        in_specs=[pl.BlockSpec((n, br, N), lambda i: (0, i, 0))],
        out_specs=pl.BlockSpec((br, N), lambda i: (i, 0)),
        out_shape=jax.ShapeDtypeStruct((R, N), out_dtype),
        compiler_params=_cparams(("parallel",)),
    )(x)


def _sum_parts(parts, landed, chip, core, stack, *, name):
    _, R, N = parts.shape
    n_layers, layer, buf = stack
    br = _ew_rows(R, N, 5)

    def body(pos_ref, own_ref, landed_ref, *rest):
        o_ref = rest[-1]
        acc = own_ref[...].astype(F32)
        for k in range(3):
            acc = acc + landed_ref[k].astype(F32)
        o_ref[...] = acc

    operands = [jnp.stack([chip, core]).astype(jnp.int32), parts, landed]
    in_specs = [pl.BlockSpec((None, br, N), lambda i, pos: (pos[0], i, 0)),
                pl.BlockSpec((3, br, N), lambda i, pos: (0, i, 0))]
    aliases = {}
    if buf is not None:
        aliases = {3: 0}
        operands.append(buf)
        in_specs.append(pl.BlockSpec(memory_space=pl.ANY))
    return pl.pallas_call(
        body, name=name,
        grid_spec=pltpu.PrefetchScalarGridSpec(
            num_scalar_prefetch=1, grid=(R // br,), in_specs=in_specs,
            out_specs=pl.BlockSpec((None, None, br, N), lambda i, pos: (layer, pos[1], i, 0))),
        out_shape=jax.ShapeDtypeStruct((n_layers, 2, R, N), F32),
        input_output_aliases=aliases,
        compiler_params=_cparams(("parallel",)),
    )(*operands)


def _place_own(shards, layer, core, slot, *, name):
    _, R, N = shards.shape
    br = _ew_rows(R // 2, N, 2)
    nb = R // 2 // br

    def body(pos_ref, x_ref, o_ref):
        o_ref[...] = x_ref[...].astype(o_ref.dtype)

    return pl.pallas_call(
        body, name=name,
        grid_spec=pltpu.PrefetchScalarGridSpec(
            num_scalar_prefetch=1, grid=(nb,),
            in_specs=[pl.BlockSpec((None, br, N), lambda i, pos: (layer, pos[0] * nb + i, 0))],
            out_specs=pl.BlockSpec((None, br, N), lambda i, pos: (pos[1], i, 0))),
        out_shape=jax.ShapeDtypeStruct((8, R // 2, N), BF16),
        compiler_params=_cparams(("parallel",)),
    )(jnp.stack([core, slot]).astype(jnp.int32), shards)


def _add_half(g, r, core, *, name):
    Q, _, R, N = g.shape
    br = _ew_rows(R, N, 3)

    def body(c_ref, g_ref, r_ref, o_ref):
        o_ref[...] = (g_ref[...] + r_ref[...]).astype(o_ref.dtype)

    return pl.pallas_call(
        body, name=name,
        grid_spec=pltpu.PrefetchScalarGridSpec(
            num_scalar_prefetch=1, grid=(Q, R // br),
            in_specs=[pl.BlockSpec((None, None, br, N), lambda q, i, c_ref: (q, c_ref[0], i, 0)),
                      pl.BlockSpec((None, br, N), lambda q, i, c_ref: (q, i, 0))],
            out_specs=pl.BlockSpec((None, br, N), lambda q, i, c_ref: (q, i, 0))),
        out_shape=jax.ShapeDtypeStruct((Q, R, N), BF16),
        compiler_params=_cparams(("parallel", "parallel")),
    )(core.reshape(1).astype(jnp.int32), g, r)


_ANY = pl.BlockSpec(memory_space=pl.ANY)


def _place():
    return lax.axis_index("x"), lax.axis_index("y"), lax.axis_index("c")


def _allgather8(blocks, *, name):
    n = len(blocks)

    def body(*refs):
        xs, outs = refs[:n], refs[n:2 * n]
        send_sems, recv_sems, local_sems = refs[2 * n:]
        x, y, c = _place()
        me, sibling = (x, y, c), (x, y, 1 - c)
        chips = [(1 - x, y), (x, 1 - y), (1 - x, 1 - y)]

        def slot(a, px, py, pc):
            return outs[a].at[4 * px + 2 * py + pc]

        def copy(a, k, block, to, src=None):
            return pltpu.make_async_remote_copy(
                src_ref=slot(a, *block) if src is None else src, dst_ref=slot(a, *block),
                send_sem=send_sems.at[a, k], recv_sem=recv_sems.at[a, k], device_id=to, device_id_type=MESH)

        mine = [pltpu.make_async_copy(xs[a], slot(a, *me), local_sems.at[a]) for a in range(n)]
        for cp in mine:
            cp.start()
        first = []
        for a in range(n):
            first.append(copy(a, 0, me, sibling, src=xs[a]))
            first += [copy(a, 1 + j, me, (*chip, c), src=xs[a]) for j, chip in enumerate(chips)]
        for cp in first:
            cp.start()
        passed = []
        for j, chip in enumerate(chips):
            for a in range(n):
                copy(a, 1 + j, (*chip, c), me).wait_recv()
                fwd = copy(a, 4 + j, (*chip, c), sibling)
                fwd.start()
                passed.append(fwd)
        for a in range(n):
            copy(a, 0, sibling, me).wait_recv()
            for j, chip in enumerate(chips):
                copy(a, 4 + j, (*chip, 1 - c), me).wait_recv()
        for cp in first + passed:
            cp.wait_send()
        for cp in mine:
            cp.wait()

    return pl.pallas_call(
        body, name=name,
        in_specs=[_ANY] * n, out_specs=[_ANY] * n,
        out_shape=[jax.ShapeDtypeStruct((8,) + b.shape, b.dtype) for b in blocks],
        scratch_shapes=[pltpu.SemaphoreType.DMA((n, 7)), pltpu.SemaphoreType.DMA((n, 7)), pltpu.SemaphoreType.DMA((n,))],
    )(*blocks)


class _Exchange:
    n_sems = 1

    def __init__(self, arrays):
        self.arrays = list(arrays)

    def out_shapes(self):
        return [jax.ShapeDtypeStruct(g.shape[:1] + g.shape[2:], g.dtype) for g in self.arrays]

    def copy(self, k, src, dst, sems, landing):
        x, y, c = _place()
        return pltpu.make_async_remote_copy(src_ref=src.at[:, 1 - c], dst_ref=dst, send_sem=sems[0], recv_sem=sems[1],
                                            device_id=(x, y, 1 - c), device_id_type=MESH)

    def copies(self, group, landing):
        xs, outs, send_sems, recv_sems = group
        return [self.copy(k, xs[a], outs[a], (send_sems.at[a, k], recv_sems.at[a, k]), landing)
                for a in range(len(xs)) for k in range(self.n_sems)]


class _Scatter(_Exchange):
    n_sems = 3

    def out_shapes(self):
        return [jax.ShapeDtypeStruct((3,) + p.shape[1:], p.dtype) for p in self.arrays]

    def copy(self, k, src, dst, sems, landing):
        x, y, c = _place()
        px, py = [(1 - x, y), (x, 1 - y), (1 - x, 1 - y)][k]
        return pltpu.make_async_remote_copy(src_ref=src.at[2 * px + py], dst_ref=dst.at[k], send_sem=sems[0], recv_sem=sems[1],
                                            device_id=(px, py, c), device_id_type=MESH)


class _GatherChips(_Exchange):
    n_sems = 3
    in_place = True

    def out_shapes(self):
        return [jax.ShapeDtypeStruct(b.shape, b.dtype) for b in self.arrays]

    def copy(self, k, src, dst, sems, landing):
        x, y, c = _place()
        px, py = [(1 - x, y), (x, 1 - y), (1 - x, 1 - y)][k]
        slot = 4 * px + 2 * py + c if landing else 4 * x + 2 * y + c
        return pltpu.make_async_remote_copy(src_ref=src.at[4 * x + 2 * y + c], dst_ref=dst.at[slot], send_sem=sems[0],
                                            recv_sem=sems[1], device_id=(px, py, c), device_id_type=MESH)


class _GatherCores(_GatherChips):
    n_sems = 4

    def copy(self, k, src, dst, sems, landing):
        x, y, c = _place()
        slot = 2 * k + 1 - c if landing else 2 * k + c
        return pltpu.make_async_remote_copy(src_ref=src.at[2 * k + c], dst_ref=dst.at[slot], send_sem=sems[0],
                                            recv_sem=sems[1], device_id=(x, y, 1 - c), device_id_type=MESH)


def _ride_aliases(rides, first_in, first_out):
    aliases, i = {}, 0
    for r in rides:
        for a in range(len(r.arrays)):
            if getattr(r, "in_place", False):
                aliases[first_in + i + a] = first_out + i + a
        i += len(r.arrays)
    return aliases


def _ride_scratch(rides):
    shapes = []
    for r in rides:
        shapes += [pltpu.SemaphoreType.DMA((len(r.arrays), r.n_sems)), pltpu.SemaphoreType.DMA((len(r.arrays), r.n_sems))]
    return shapes


def _ride_split(rides, in_refs, out_refs, sem_refs):
    groups, i, o = [], 0, 0
    for k, r in enumerate(rides):
        n = len(r.arrays)
        groups.append((in_refs[i:i + n], out_refs[o:o + n], sem_refs[2 * k], sem_refs[2 * k + 1]))
        i, o = i + n, o + n
    return groups


def _ride_start(rides, groups):
    for r, g in zip(rides, groups):
        for cp in r.copies(g, False):
            cp.start()


def _ride_wait(rides, groups):
    for r, g in zip(rides, groups):
        for cp in r.copies(g, True):
            cp.wait_recv()
        for cp in r.copies(g, False):
            cp.wait_send()


def _run_rides(rides, *, name):
    n_in = sum(len(r.arrays) for r in rides)

    def body(*refs):
        groups = _ride_split(rides, refs[:n_in], refs[n_in:2 * n_in], refs[2 * n_in:])
        _ride_start(rides, groups)
        _ride_wait(rides, groups)

    outs = pl.pallas_call(
        body, name=name,
        in_specs=[_ANY] * n_in, out_specs=[_ANY] * n_in,
        out_shape=[s for r in rides for s in r.out_shapes()],
        scratch_shapes=_ride_scratch(rides),
    )(*[a for r in rides for a in r.arrays])
    return _ride_outputs(rides, outs)


def _ride_outputs(rides, outs):
    res, o = [], 0
    for r in rides:
        res.append(list(outs[o:o + len(r.arrays)]))
        o += len(r.arrays)
    return res


def _join_halves(bufs, *, name):
    n = len(bufs)

    def body(*refs):
        xs, outs = refs[:n], refs[n:2 * n]
        send_sems, recv_sems = refs[2 * n:]
        x, y, c = _place()
        sends = [pltpu.make_async_remote_copy(
            src_ref=xs[a].at[:, c], dst_ref=outs[a].at[:, c], send_sem=send_sems.at[a], recv_sem=recv_sems.at[a],
            device_id=(x, y, 1 - c), device_id_type=MESH) for a in range(n)]
        recvs = [pltpu.make_async_remote_copy(
            src_ref=xs[a].at[:, c], dst_ref=outs[a].at[:, 1 - c], send_sem=send_sems.at[a], recv_sem=recv_sems.at[a],
            device_id=(x, y, 1 - c), device_id_type=MESH) for a in range(n)]
        for cp in sends:
            cp.start()
        for cp in recvs:
            cp.wait_recv()
        for cp in sends:
            cp.wait_send()

    return pl.pallas_call(
        body, name=name,
        in_specs=[_ANY] * n, out_specs=[_ANY] * n,
        out_shape=[jax.ShapeDtypeStruct(b.shape, b.dtype) for b in bufs],
        input_output_aliases={a: a for a in range(n)},
        scratch_shapes=[pltpu.SemaphoreType.DMA((n,)), pltpu.SemaphoreType.DMA((n,))],
    )(*bufs)


def _perm_w_in(wt):
    pad = jnp.zeros((PCOLS - IN_COLS, wt.shape[1]), wt.dtype)
    return jnp.concatenate([wt[:3072], wt[3136:IN_COLS], wt[3072:3136], pad], axis=0)


def _unperm_w_in(gt):
    return jnp.concatenate([gt[:3072], gt[4096:IN_COLS], gt[3072:4096]], axis=0)


def _perm_w_uq(w):
    w4 = w.reshape(MLA_Q_LORA, MLA_HEADS, MLA_NOPE + MLA_ROPE)
    nope = w4[:, :, :MLA_NOPE].reshape(MLA_Q_LORA, MLA_HEADS * LANE)
    pe = jnp.pad(w4[:, :, MLA_NOPE:], ((0, 0), (0, 0), (0, LANE - MLA_ROPE))).reshape(MLA_Q_LORA, MLA_HEADS * LANE)
    return jnp.concatenate([nope, pe], axis=1)


def _unperm_w_uq(g):
    nope = g[:, :MLA_HEADS * LANE].reshape(MLA_Q_LORA, MLA_HEADS, LANE)
    pe = g[:, MLA_HEADS * LANE:].reshape(MLA_Q_LORA, MLA_HEADS, LANE)[:, :, :MLA_ROPE]
    return jnp.concatenate([nope, pe], axis=2).reshape(MLA_Q_LORA, MLA_HEADS * (MLA_NOPE + MLA_ROPE))


def _perm_w_ukv(w):
    w4 = w.reshape(MLA_KV_LORA, MLA_HEADS, MLA_NOPE + MLA_V)
    return jnp.concatenate([w4[:, :, :MLA_NOPE].reshape(MLA_KV_LORA, -1), w4[:, :, MLA_NOPE:].reshape(MLA_KV_LORA, -1)], axis=1)


def _unperm_w_ukv(g):
    kn = g[:, :MLA_HEADS * LANE].reshape(MLA_KV_LORA, MLA_HEADS, LANE)
    vv = g[:, MLA_HEADS * LANE:].reshape(MLA_KV_LORA, MLA_HEADS, LANE)
    return jnp.concatenate([kn, vv], axis=2).reshape(MLA_KV_LORA, -1)


def _silu(v):
    return v * jax.nn.sigmoid(v)


def _silu_grad(v):
    s = jax.nn.sigmoid(v)
    return s * (1.0 + v * (1.0 - s))


_WEIGHTS = ("c_ctx", "w_ada", "b_ada", "w_in", "na_rpb", "swa_sink", "mla_q_norm", "mla_kv_norm", "mla_w_uq", "mla_w_ukv",
            "gqa_q_norm", "gqa_k_norm", "w_out", "ln1_g", "ln1_b", "ffn_w_gate", "ffn_w_up", "ffn_conv_w", "ffn_conv_b",
            "ffn_w_down", "ln2_g", "ln2_b")
_COL_SHARDED = ("mla_w_uq", "mla_w_ukv", "ffn_w_gate", "ffn_w_up")
_ROW_SHARDED = ("w_out", "ffn_w_down")
_BIG = ("w_in",) + _COL_SHARDED + _ROW_SHARDED
_SMALL = ("c_ctx", "b_ada", "na_rpb", "swa_sink", "mla_q_norm", "mla_kv_norm", "gqa_q_norm", "gqa_k_norm", "ln1_g", "ln1_b",
          "ffn_conv_w", "ffn_conv_b", "ln2_g", "ln2_b")


def _pack(arrays):
    flat = jnp.concatenate([a.reshape(-1) for a in arrays])
    n = flat.shape[0]
    rows = -(-n // (8 * LANE)) * 8
    return jnp.pad(flat, (0, rows * LANE - n)).reshape(rows, LANE)


def _unpack(packed, like):
    flat = packed.reshape(-1)
    out, o = [], 0
    for a in like:
        out.append(flat[o:o + a.size].reshape(a.shape))
        o += a.size
    return out


def _train_step(x, c, ctx, loss_target, w, m_in, v_in):
    L = DEPTH
    S, D = x.shape[1], x.shape[2]
    C = ctx.shape[1]
    T = S + C
    F = w["ffn_conv_b"].shape[1]
    ax, ay, ac = _place()
    chip = 2 * ax + ay
    dev = 2 * chip + ac
    n_ada = w["w_ada"].shape[2]
    w, m_in, v_in = dict(w), dict(m_in), dict(v_in)
    for d in (w, m_in, v_in):
        d["w_in"] = jnp.swapaxes(d["w_in"], 1, 2)

    gather_groups = {"A": ("w_in", "mla_w_uq", "mla_w_ukv"), "B": ("w_out",), "C": ("ffn_w_gate", "ffn_w_up"), "D": ("ffn_w_down",)}
    full = {n: [None] * L for n in _BIG}
    w_in_p, w_uq_p, w_ukv_p = [None] * L, [None] * L, [None] * L
    half_done = {}

    def chips_step(group, l):
        return _GatherChips([_place_own(w[n], l, ac, dev, name="gather_place") for n in gather_groups[group]])

    def cores_step(group, l):
        return _GatherCores(half_done.pop((group, l)))

    def finish_group(group, l, bufs):
        for n, b in zip(gather_groups[group], bufs):
            r, cols = b.shape[1:]
            if n in _COL_SHARDED:
                full[n][l] = b.reshape(4, 2, r, cols).transpose(1, 2, 0, 3).reshape(2 * r, 4 * cols)
            else:
                full[n][l] = b.reshape(8 * r, cols)
        if group == "A":
            w_in_p[l], w_uq_p[l] = _perm_w_in(full["w_in"][l]), _perm_w_uq(full["mla_w_uq"][l])
            w_ukv_p[l] = _perm_w_ukv(full["mla_w_ukv"][l])

    def with_rides(result, rides):
        return result if rides else (result, [])

    def my_half(a):
        r = a.shape[0] // 2
        return lax.dynamic_slice_in_dim(a, ac * r, r, axis=0).astype(BF16)

    gathered = _allgather8([my_half(w[n][0]) for n in gather_groups["A"]] + [w["ffn_conv_w"]], name="gather_weights")
    finish_group("A", 0, gathered[:-1])
    conv_w = gathered[-1][::2].transpose(1, 2, 0, 3).reshape(L, 3, F)

    (c_all,) = _allgather8([c], name="gather_c")
    c16 = jnp.concatenate([c_all.reshape(8, D), jnp.broadcast_to(w["c_ctx"][None], (8, D))], axis=0)
    row_keep = (jnp.arange(16) <= 8).astype(F32)[:, None]
    sc = _silu(c16) * row_keep
    b_loc = lax.dynamic_slice_in_dim(w["b_ada"], chip * n_ada, n_ada, axis=1)
    mod_loc = jnp.stack([_mm(sc, w["w_ada"], b_layer=l, name="mod_mm") + b_loc[l][None] for l in range(L)])
    (mod_g,) = _allgather8([mod_loc], name="gather_mod")
    mod_all = mod_g[::2].transpose(1, 2, 0, 3).reshape(L, 16, 4 * n_ada)
    mod_x = lax.dynamic_index_in_dim(mod_all, dev, axis=1, keepdims=False)
    mod_c = mod_all[:, 8]
    mods = [jnp.stack([mod_x[l].reshape(6, D), mod_c[l].reshape(6, D)], axis=1) for l in range(L)]

    tabs = _rope_tables(S, C, HEAD_DIM) + _rope_tables(S, C, MLA_ROPE)
    swa_mask = _swa_mask(S)
    scale = HEAD_DIM ** -0.5
    def attn_cfgs(l):
        cq = l < L - 1
        return (_AttnCfg(Hkv=NA_HEADS, G=1, S=S, C=C, band=(2, 5), scale=scale, n_var=5, bias_per_head=True, ctx_queries=cq),
                _AttnCfg(Hkv=SWA_KV_HEADS, G=SWA_HEADS // SWA_KV_HEADS, S=S, C=C, band=(1, 3), scale=scale, n_var=3,
                         has_sink=True, ctx_queries=cq),
                _AttnCfg(Hkv=MLA_HEADS, G=1, S=S, C=C, band=None, scale=(MLA_NOPE + MLA_ROPE) ** -0.5, two=True,
                         bq=2 * BQ, ctx_queries=cq),
                _AttnCfg(Hkv=GQA_KV_HEADS, G=GQA_HEADS // GQA_KV_HEADS, S=S, C=C, band=None, scale=scale, ctx_queries=cq))

    row = lambda a: a[None, :]

    xt = jnp.concatenate([x[0], ctx[0]], axis=0)
    saved = []
    for l in range(L):
        md = mods[l]
        gq, gk, mq, mkv = row(w["gqa_q_norm"][l]), row(w["gqa_k_norm"][l]), row(w["mla_q_norm"][l]), row(w["mla_kv_norm"][l])
        h1 = _mod_fwd(xt, md[0], md[1], S, name="mod_fwd")
        p = _mm(h1, w_in_p[l], mode="nt", name="in_proj")
        qkv = _prep_fwd(p, tabs, gq, gk, mq, mkv, S, name="prep_fwd")
        qm = _mm(qkv, w_uq_p[l], a_off=CB_CQ * LANE, a_k=MLA_Q_LORA, tk=LANE, name="mla_uq")
        qmb = _pe_rope(qm, tabs, S, 1.0, BF16, name="mla_q_rope")
        kvm = _mm(qkv, w_ukv_p[l], a_off=CB_CKV * LANE, a_k=MLA_KV_LORA, tk=LANE, out_dtype=BF16, name="mla_ukv")
        bias_na = _na_bias(w["na_rpb"][l], S)
        sink = jnp.broadcast_to(jnp.repeat(w["swa_sink"][l].reshape(SWA_KV_HEADS, -1), BQ, axis=1)[:, :, None],
                                (SWA_KV_HEADS, SWA_HEADS // SWA_KV_HEADS * BQ, LANE))
        cfg_na, cfg_swa, cfg_mla, cfg_gqa = attn_cfgs(l)
        rows = T if l < L - 1 else S
        first, more = l == 0, l + 1 < L
        rides = [chips_step("B", l)] if first else []
        mix_blocks = NA_HEADS + SWA_HEADS + MLA_HEADS + GQA_HEADS
        mix, got = with_rides(_attn_fwd(cfg_na, qkv, CB_NA_Q, qkv, CB_NA_K, qkv, CB_NA_V, bias=bias_na, rides=rides,
                                        into=(None, 0, mix_blocks), name="na_fwd"), rides)
        if first:
            half_done[("B", l)] = got[0]
        rides = [cores_step("B", l)] if first else []
        mix, got = with_rides(_attn_fwd(cfg_swa, qkv, CB_SWA_Q, qkv, CB_SWA_K, qkv, CB_SWA_V, bias=swa_mask, sink=sink,
                                        rides=rides, into=(mix, NA_HEADS, mix_blocks), name="swa_fwd"), rides)
        if first:
            finish_group("B", l, got[0])
        mix, got = _attn_fwd(cfg_mla, qmb, 0, kvm, 0, kvm, MLA_HEADS, q2=qmb, q2_cb=MLA_HEADS, k2=qkv, k2_cb=CB_KPE,
                             rides=[chips_step("C", l)], into=(mix, NA_HEADS + SWA_HEADS, mix_blocks), name="mla_fwd")
        half_done[("C", l)] = got[0]
        mix, got = _attn_fwd(cfg_gqa, qkv, CB_GQA_Q, qkv, CB_GQA_K, qkv, CB_GQA_V,
                             rides=[cores_step("C", l), chips_step("D", l)],
                             into=(mix, NA_HEADS + SWA_HEADS + MLA_HEADS, mix_blocks), name="gqa_fwd")
        finish_group("C", l, got[0])
        half_done[("D", l)] = got[1]
        z1 = _mm(mix, full["w_out"][l], rows=rows, name="out_proj")
        x1 = _res_fwd(xt, z1, md[2], row(w["ln1_g"][l]), row(w["ln1_b"][l]), S, name="res_fwd")
        h2 = _mod_fwd(x1, md[3], md[4], S, name="mod_fwd")
        gp, got = _mm(h2, full["ffn_w_gate"][l], rides=[cores_step("D", l)] + ([chips_step("A", l + 1)] if more else []),
                      out_dtype=BF16, name="ffn_in")
        finish_group("D", l, got[0])
        if more:
            half_done[("A", l + 1)] = got[1]
        rides = [cores_step("A", l + 1), chips_step("B", l + 1)] if more else []
        up, got = with_rides(_mm(h2, full["ffn_w_up"][l], rides=rides, out_dtype=BF16, name="ffn_in"), rides)
        if more:
            finish_group("A", l + 1, got[0])
            half_done[("B", l + 1)] = got[1]
        act = _ffn_fwd(gp, up, conv_w[l], row(w["ffn_conv_b"][l]), S, name="ffn_mid")
        rides = [cores_step("B", l + 1)] if more else []
        z2, got = with_rides(_mm(act, full["ffn_w_down"][l], rides=rides, name="ffn_out"), rides)
        if more:
            finish_group("B", l + 1, got[0])
        x2 = _res_fwd(x1, z2, md[5], row(w["ln2_g"][l]), row(w["ln2_b"][l]), S, name="res_fwd")
        saved.append(dict(x=xt, h1=h1, p=p, qkv=qkv, qmb=qmb, kvm=kvm, bias_na=bias_na, sink=sink, mix=mix, z1=z1, x1=x1,
                          h2=h2, gp=gp, up=up, act=act, z2=z2, cfgs=(cfg_na, cfg_swa, cfg_mla, cfg_gqa)))
        xt = x2

    dx, loss_part = _loss_fwd_bwd(xt, loss_target[0], S, name="loss")
    loss = lax.psum(loss_part[0, 0], ("x", "y", "c"))

    groups = {"ffn": ("ffn_w_gate", "ffn_w_up", "ffn_w_down"), "rest": ("w_in", "mla_w_uq", "mla_w_ukv", "w_out")}
    wgrad = [dict() for _ in range(L)]
    parts, landed = {}, {}

    def halves_of(group, l):
        return [wgrad[l][n].reshape(4, 2, wgrad[l][n].shape[1] // 2, wgrad[l][n].shape[2]) for n in groups[group]]

    def add_halves(group, l, received):
        parts[(group, l)] = [_add_half(h, r, ac, name="rs_core_add") for h, r in zip(halves_of(group, l), received)]

    small = {n: [None] * L for n in ("na_rpb", "swa_sink", "mla_q_norm", "mla_kv_norm", "gqa_q_norm", "gqa_k_norm",
                                     "ln1_g", "ln1_b", "ffn_conv_w", "ffn_conv_b", "ln2_g", "ln2_b")}
    dmod = [None] * L
    for l in reversed(range(L)):
        sv, md = saved[l], mods[l]
        gq, gk, mq, mkv = row(w["gqa_q_norm"][l]), row(w["gqa_k_norm"][l]), row(w["mla_q_norm"][l]), row(w["mla_kv_norm"][l])
        cb_row = row(w["ffn_conv_b"][l])
        dx1, dz2, acc_r2 = _res_bwd(sv["x1"], sv["z2"], md[5], row(w["ln2_g"][l]), dx, S, name="res_bwd")
        dact = _mm(dz2, full["ffn_w_down"][l], mode="nt", out_dtype=BF16, name="ffn_out_dx")
        wgrad[l]["ffn_w_down"] = _mm(sv["act"], dz2, mode="tn", name="ffn_out_dw").reshape(4, F // 4, D)
        da, du, acc_f = _ffn_bwd_act(sv["gp"], sv["up"], dact, conv_w[l], cb_row, S, name="ffn_mid_bwd")
        dg = _ffn_bwd_conv(da, conv_w[l], S, name="ffn_conv_bwd")
        dh2 = _mm(dg, full["ffn_w_gate"][l], mode="nt", name="ffn_in_dx")
        dh2 = _mm(du, full["ffn_w_up"][l], mode="nt", add=dh2, name="ffn_in_dx_add")
        wgrad[l]["ffn_w_gate"] = _mm(sv["h2"], dg, mode="tn", stack=(1, 0, None), split4=True,
                                     name="ffn_in_dw").reshape(4, D, F // 4)
        wgrad[l]["ffn_w_up"] = _mm(sv["h2"], du, mode="tn", stack=(1, 0, None), split4=True,
                                   name="ffn_in_dw").reshape(4, D, F // 4)
        dx1, acc_m2 = _mod_bwd(sv["x1"], dh2, md[4], dx1, S, name="mod_bwd")
        dxa, dz1, acc_r1 = _res_bwd(sv["x"], sv["z1"], md[2], row(w["ln1_g"][l]), dx1, S, name="res_bwd")
        dmix = _mm(dz1, full["w_out"][l], mode="nt", out_dtype=BF16, name="out_proj_dx")
        wgrad[l]["w_out"] = _mm(sv["mix"], dz1, mode="tn", rows=dz1.shape[0], name="out_proj_dw").reshape(4, -1, D)

        qkv, qmb, kvm = sv["qkv"], sv["qmb"], sv["kvm"]
        cfg_na, cfg_swa, cfg_mla, cfg_gqa = sv["cfgs"]
        rest_above = l + 1 < L
        rides = [_Exchange(halves_of("ffn", l))] + ([_Exchange(halves_of("rest", l + 1))] if rest_above else [])
        dq_a, dk_a, dv_a, dbias, received = _attn_bwd(cfg_na, qkv, CB_NA_Q, qkv, CB_NA_K, qkv, CB_NA_V, dmix, 0,
                                                      bias=sv["bias_na"], want_dbias=True, rides=rides, name="na_bwd")
        add_halves("ffn", l, received[0])
        if rest_above:
            add_halves("rest", l + 1, received[1])
        dq_b, dk_b, dv_b, dsink = _attn_bwd(cfg_swa, qkv, CB_SWA_Q, qkv, CB_SWA_K, qkv, CB_SWA_V, dmix, NA_HEADS,
                                            bias=swa_mask, sink=sv["sink"], name="swa_bwd")
        dq_c, dk_c, dv_c, dq2_c, dk2_c, got = _attn_bwd(
            cfg_mla, qmb, 0, kvm, 0, kvm, MLA_HEADS, dmix, NA_HEADS + SWA_HEADS, q2=qmb, q2_cb=MLA_HEADS, k2=qkv,
            k2_cb=CB_KPE, rides=[_Scatter(parts[("ffn", l)])], name="mla_bwd")
        landed[("ffn", l)] = got[0]
        rides = [_Scatter(parts[("rest", l + 1)])] if rest_above else []
        gqa_out = _attn_bwd(cfg_gqa, qkv, CB_GQA_Q, qkv, CB_GQA_K, qkv, CB_GQA_V, dmix,
                            NA_HEADS + SWA_HEADS + MLA_HEADS, rides=rides, name="gqa_bwd")
        dq_d, dk_d, dv_d = gqa_out[:3]
        if rest_above:
            landed[("rest", l + 1)] = gqa_out[3][0]
        dqm = _pe_rope(jnp.concatenate([dq_c, dq2_c], axis=1), tabs, S, -1.0, BF16, name="mla_q_rope_bwd")
        dkvm = jnp.concatenate([dk_c, dv_c], axis=1).astype(BF16)
        dcq = _mm(dqm, w_uq_p[l], mode="nt", name="mla_uq_dx")
        dckv = _mm(dkvm, w_ukv_p[l], mode="nt", name="mla_ukv_dx")
        cqn = qkv[:, CB_CQ * LANE:CB_CKV * LANE]
        ckvn = qkv[:, CB_CKV * LANE:(CB_CKV + 1) * LANE]
        d_uq = _unperm_w_uq(_mm(cqn, dqm, mode="tn", name="mla_uq_dw"))
        d_ukv = _unperm_w_ukv(_mm(ckvn, dkvm, mode="tn", name="mla_ukv_dw"))
        grads = {}
        for h in range(NA_HEADS):
            grads[CB_NA_Q + h], grads[CB_NA_K + h], grads[CB_NA_V + h] = (dq_a, h), (dk_a, h), (dv_a, h)
        for h in range(SWA_HEADS):
            grads[CB_SWA_Q + h] = (dq_b, h)
        for h in range(SWA_KV_HEADS):
            grads[CB_SWA_K + h], grads[CB_SWA_V + h] = (dk_b, h), (dv_b, h)
        for h in range(GQA_HEADS):
            grads[CB_GQA_Q + h] = (dq_d, h)
        for h in range(GQA_KV_HEADS):
            grads[CB_GQA_K + h], grads[CB_GQA_V + h] = (dk_d, h), (dv_d, h)
        grads[CB_KPE], grads[CB_CQ], grads[CB_CKV] = (dk2_c, 0), (dcq, 0), (dckv, 0)
        dp, acc_p = _prep_bwd(sv["p"], grads, tabs, gq, gk, mq, mkv, S, name="prep_bwd")
        dh1 = _mm(dp, w_in_p[l], name="in_proj_dx")
        d_in = _unperm_w_in(_mm(dp, sv["h1"], mode="tn", name="in_proj_dw")).reshape(4, IN_COLS // 4, D)
        dx, acc_m1 = _mod_bwd(sv["x"], dh1, md[1], dxa, S, name="mod_bwd")

        to4 = lambda g: g.reshape(g.shape[0], 4, g.shape[1] // 4).transpose(1, 0, 2)
        wgrad[l]["w_in"], wgrad[l]["mla_w_uq"], wgrad[l]["mla_w_ukv"] = d_in, to4(d_uq), to4(d_ukv)
        dmod[l] = jnp.stack([acc_m1[0:2], acc_m1[2:4], acc_r1[0:2], acc_m2[0:2], acc_m2[2:4], acc_r2[0:2]])
        rpb_vjp = jax.vjp(lambda r: _na_bias(r, S), w["na_rpb"][l])[1]
        small["na_rpb"][l] = rpb_vjp(dbias)[0]
        small["swa_sink"][l] = dsink[:, :, 0].reshape(SWA_KV_HEADS, -1, BQ).sum(axis=-1).reshape(-1)
        small["gqa_q_norm"][l], small["gqa_k_norm"][l] = acc_p[0, :LANE], acc_p[1, :LANE]
        small["mla_q_norm"][l], small["mla_kv_norm"][l] = acc_p[2], acc_p[3, :LANE]
        small["ln1_g"][l], small["ln1_b"][l] = acc_r1[2], acc_r1[3]
        small["ln2_g"][l], small["ln2_b"][l] = acc_r2[2], acc_r2[3]
        small["ffn_conv_w"][l], small["ffn_conv_b"][l] = acc_f[0:3], acc_f[3]
    grad_x = dx[:S][None]

    dmod_x = jnp.stack([dmod[l][:, 0].reshape(-1) for l in range(L)])
    dmod_c = jnp.stack([dmod[l][:, 1].reshape(-1) for l in range(L)])
    small_names = tuple(small)
    bucket = [dmod_x, dmod_c] + [jnp.stack(small[n]) for n in small_names]
    (b8,) = _allgather8([_pack(bucket)], name="gather_small")
    tot = _unpack(_sum_lead(b8, F32, name="sum_small"), bucket)
    dmod_x_all = b8.reshape(8, -1)[:, :dmod_x.size].reshape(8, L, 6 * D)
    dmod_c_tot = tot[1]
    g_small = dict(zip(small_names, tot[2:]))
    g_small["b_ada"] = tot[0] + dmod_c_tot
    g_small["ffn_conv_w"] = lax.dynamic_slice_in_dim(g_small["ffn_conv_w"], chip * (F // 4), F // 4, axis=2)

    dmod16 = jnp.concatenate([dmod_x_all, jnp.broadcast_to(dmod_c_tot[None], (8, L, 6 * D))], axis=0) * row_keep[:, :, None]
    dmod16 = lax.dynamic_slice_in_dim(dmod16, chip * n_ada, n_ada, axis=2)
    g_ada, dsc = None, None
    for l in range(L):
        g_ada = _mm(sc, dmod16[:, l], mode="tn", exact=True, stack=(L, l, g_ada), name="ada_dw")
        dsc = _mm(dmod16[:, l], w["w_ada"], b_layer=l, mode="nt", add=dsc, name="ada_dx" if dsc is None else "ada_dx_add")
    (dsc8,) = _allgather8([dsc[8:16]], name="gather_dsc")
    dsc4 = dsc8[::2, 0]
    g_small["c_ctx"] = (((dsc4[0] + dsc4[1]) + dsc4[2]) + dsc4[3]) * _silu_grad(w["c_ctx"])

    add_halves("rest", 0, _run_rides([_Exchange(halves_of("rest", 0))], name="rs_core_exchange")[0])
    landed[("rest", 0)] = _run_rides([_Scatter(parts[("rest", 0)])], name="rs_chip_scatter")[0]
    sums = {}
    for l in range(L):
        for group, names in groups.items():
            for n, p, got in zip(names, parts[(group, l)], landed[(group, l)]):
                sums[n] = _sum_parts(p, got, chip, ac, (L, l, sums.get(n)), name="rs_chip_sum")
    joined = _join_halves([sums[n] for n in _BIG], name="rs_join")
    g_big = {n: j.reshape(L, 2 * j.shape[2], j.shape[3]) for n, j in zip(_BIG, joined)}
    g_big["w_ada"] = g_ada

    grad, delta, new_m, new_v = {}, {}, {}, {}
    for n in _BIG + ("w_ada",):
        grad[n] = g_big[n]
        delta[n], new_m[n], new_v[n] = _adam(w[n], g_big[n], m_in[n], v_in[n], name="adam")
    like = [w[n] for n in _SMALL]
    packed = [_pack([src[n].reshape(w[n].shape) for n in _SMALL]) for src in (w, g_small, m_in, v_in)]
    d_s, m_s, v_s = _adam(*packed, name="adam_small")
    for n, g_, d_, m_, v_ in zip(_SMALL, _unpack(packed[1], like), _unpack(d_s, like), _unpack(m_s, like), _unpack(v_s, like)):
        grad[n], delta[n], new_m[n], new_v[n] = g_, d_, m_, v_

    for d in (grad, delta, new_m, new_v):
        d["w_in"] = jnp.swapaxes(d["w_in"], 1, 2)
    return (loss, grad_x, *[grad[n] for n in _WEIGHTS], *[delta[n] for n in _WEIGHTS],
            *[new_m[n] for n in _WEIGHTS], *[new_v[n] for n in _WEIGHTS])


def kernel(x, c, ctx, c_ctx, w_ada, b_ada, w_in, na_rpb, swa_sink, mla_q_norm, mla_kv_norm, mla_w_uq, mla_w_ukv, gqa_q_norm, gqa_k_norm, w_out, ln1_g, ln1_b, ffn_w_gate, ffn_w_up, ffn_conv_w, ffn_conv_b, ffn_w_down, ln2_g, ln2_b, loss_target, m_c_ctx, m_w_ada, m_b_ada, m_w_in, m_na_rpb, m_swa_sink, m_mla_q_norm, m_mla_kv_norm, m_mla_w_uq, m_mla_w_ukv, m_gqa_q_norm, m_gqa_k_norm, m_w_out, m_ln1_g, m_ln1_b, m_ffn_w_gate, m_ffn_w_up, m_ffn_conv_w, m_ffn_conv_b, m_ffn_w_down, m_ln2_g, m_ln2_b, v_c_ctx, v_w_ada, v_b_ada, v_w_in, v_na_rpb, v_swa_sink, v_mla_q_norm, v_mla_kv_norm, v_mla_w_uq, v_mla_w_ukv, v_gqa_q_norm, v_gqa_k_norm, v_w_out, v_ln1_g, v_ln1_b, v_ffn_w_gate, v_ffn_w_up, v_ffn_conv_w, v_ffn_conv_b, v_ffn_w_down, v_ln2_g, v_ln2_b):
    args = locals()
    w = {n: args[n] for n in _WEIGHTS}
    m_in = {n: args["m_" + n] for n in _WEIGHTS}
    v_in = {n: args["v_" + n] for n in _WEIGHTS}
    return _train_step(x, c, ctx, loss_target, w, m_in, v_in)
```

```python
import functools
import math

import numpy as np
import jax
import jax.numpy as jnp
from jax import lax
from jax.experimental import pallas as pl
from jax.experimental.pallas import tpu as pltpu

F32 = jnp.float32
BF16 = jnp.bfloat16
MESH = pl.DeviceIdType.MESH

GRID_W = 64
HEAD_DIM = 128
NA_HEADS, NA_WIN_R, NA_WIN_C = 4, 8, 16
SWA_HEADS, SWA_KV_HEADS, SWA_WINDOW = 4, 2, 128
MLA_HEADS, MLA_Q_LORA, MLA_KV_LORA, MLA_NOPE, MLA_ROPE, MLA_V = 4, 384, 128, 128, 64, 128
GQA_HEADS, GQA_KV_HEADS = 4, 2
ROPE_THETA = 10000.0
EPS = 1e-6
NEG = -1e30
DEPTH = 2
DEEPNORM_ALPHA = (2 * DEPTH) ** 0.25
ADAM_LR, ADAM_B1, ADAM_B2, ADAM_EPS, ADAM_WD, ADAM_STEP = 0.001, 0.9, 0.999, 1e-08, 0.01, 10

LANE = 128
V7X_VMEM_BYTES = 64 * 1024 * 1024
VMEM_LIMIT = 56 * 1024 * 1024
MM_VMEM_BUDGET = 40 * 1024 * 1024
EW_VMEM_BUDGET = 28 * 1024 * 1024
BQ = 128

CB_NA_Q, CB_NA_K, CB_NA_V = 0, 4, 8
CB_SWA_Q, CB_SWA_K, CB_SWA_V = 12, 16, 18
CB_CQ, CB_CKV = 20, 23
CB_GQA_Q, CB_GQA_K, CB_GQA_V = 24, 28, 30
CB_KPE = 32
PCOLS = 33 * LANE
IN_COLS = 4160


def _cparams(sem=None, **kw):
    return pltpu.CompilerParams(dimension_semantics=sem, vmem_limit_bytes=VMEM_LIMIT, **kw)


def _pick(n, target, mult=LANE):
    best = None
    for d in range(mult, min(n, target) + 1, mult):
        if n % d == 0:
            best = d
    return n if best is None else best


def _mm(a, b, *, mode="nn", out_dtype=F32, a_off=0, a_k=None, tm=1408, tn=1408, tk=2816, exact=False, add=None,
        stack=None, split4=False, rows=None, rides=(), b_layer=None, name):
    b_shape = b.shape if b_layer is None else b.shape[1:]
    if mode == "tn":
        K, M = a.shape
        K2, N = b_shape
    elif mode == "nn":
        M, K = a.shape
        K2, N = b_shape
    else:
        M, K = a.shape
        N, K2 = b_shape
    if a_k is not None:
        K = a_k
    if rows is not None:
        if mode == "tn":
            assert rows <= min(K, K2)
            K = K2 = rows
        else:
            assert rows <= M
            M = rows
    assert K == K2, (a.shape, b.shape, mode)
    m_mult = LANE if mode == "tn" else 16
    n_cols = N // 4 if split4 else N
    bm, bn, bk = _pick(M, tm, m_mult), _pick(n_cols, tn), _pick(K, tk)
    sa, sb, so = a.dtype.itemsize, b.dtype.itemsize, jnp.dtype(out_dtype).itemsize

    def vmem_estimate():
        acc = bm * bn * 4 if K // bk > 1 else 0
        return 2 * (bm * bk * sa + bk * bn * sb) + acc + 2 * bm * bn * so + (2 * bm * bn * 4 if add is not None else 0)

    while vmem_estimate() > MM_VMEM_BUDGET:
        if bm >= bn and _pick(M, bm - 1, m_mult) < bm:
            bm = _pick(M, bm - 1, m_mult)
        elif _pick(n_cols, bn - 1) < bn:
            bn = _pick(n_cols, bn - 1)
        else:
            assert _pick(K, bk - 1) < bk, "no tiling fits VMEM"
            bk = _pick(K, bk - 1)
    assert a_off % bk == 0
    koff = a_off // bk
    nk = K // bk
    if mode == "tn":
        a_spec = pl.BlockSpec((bk, bm), lambda i, j, k: (k, i))
        b_spec = pl.BlockSpec((bk, bn), lambda i, j, k: (k, j))
        dims = (((0,), (0,)), ((), ()))
    elif mode == "nn":
        a_spec = pl.BlockSpec((bm, bk), lambda i, j, k: (i, k + koff))
        b_spec = pl.BlockSpec((bk, bn), lambda i, j, k: (k, j))
        dims = (((1,), (0,)), ((), ()))
    else:
        a_spec = pl.BlockSpec((bm, bk), lambda i, j, k: (i, k + koff))
        b_spec = pl.BlockSpec((bn, bk), lambda i, j, k: (j, k))
        dims = (((1,), (1,)), ((), ()))
    if b_layer is not None:
        b_block, b_index = b_spec.block_shape, b_spec.index_map
        b_spec = pl.BlockSpec((None,) + tuple(b_block), lambda i, j, k: (b_layer,) + tuple(b_index(i, j, k)))

    operands = [a, b]
    in_specs = [a_spec, b_spec]
    if add is not None:
        operands.append(add)
        in_specs.append(pl.BlockSpec((bm, bn), lambda i, j, k: (i, j)))
    aliases = {}
    if stack is None:
        out_spec = pl.BlockSpec((bm, bn), lambda i, j, k: (i, j))
        out_shape = jax.ShapeDtypeStruct((M, N), out_dtype)
    else:
        n_layers, layer, buf = stack
        if split4:
            nb = N // 4 // bn
            assert N % (4 * bn) == 0
            out_spec = pl.BlockSpec((None, None, bm, bn), lambda i, j, k: (layer, j // nb, i, j % nb))
            out_shape = jax.ShapeDtypeStruct((n_layers, 4, M, N // 4), out_dtype)
        else:
            out_spec = pl.BlockSpec((None, bm, bn), lambda i, j, k: (layer, i, j))
            out_shape = jax.ShapeDtypeStruct((n_layers, M, N), out_dtype)
        if buf is not None:
            aliases = {len(operands): 0}
            operands.append(buf)
            in_specs.append(pl.BlockSpec(memory_space=pl.ANY))
    has_add, has_buf = add is not None, bool(aliases)
    n_ride = sum(len(r.arrays) for r in rides)
    aliases.update(_ride_aliases(rides, len(operands), 1))
    grid = (M // bm, N // bn, nk)

    def body(*refs):
        a_ref, b_ref = refs[:2]
        add_ref = refs[2] if has_add else None
        base = 2 + has_add + has_buf
        o_ref = refs[base + n_ride]
        scratch = refs[base + 2 * n_ride + 1:]
        if rides:
            ride_groups = _ride_split(rides, refs[base:base + n_ride], refs[base + n_ride + 1:base + 2 * n_ride + 1],
                                      scratch[1 if nk > 1 else 0:])
            steps = [pl.program_id(d) for d in range(3)]
            pl.when((steps[0] == 0) & (steps[1] == 0) & (steps[2] == 0))(lambda: _ride_start(rides, ride_groups))
        compute(a_ref, b_ref, add_ref, o_ref, scratch[0] if nk > 1 else None)
        if rides:
            pl.when((steps[0] == grid[0] - 1) & (steps[1] == grid[1] - 1) & (steps[2] == grid[2] - 1))(
                lambda: _ride_wait(rides, ride_groups))

    def compute(a_ref, b_ref, add_ref, o_ref, acc_ref):
        if exact:
            prod = lax.dot_general(a_ref[...].astype(F32), b_ref[...].astype(F32), dims,
                                   precision=lax.Precision.HIGHEST, preferred_element_type=F32)
        else:
            prod = lax.dot_general(a_ref[...].astype(BF16), b_ref[...].astype(BF16), dims, preferred_element_type=F32)

        def finish(res):
            if has_add:
                res = res + add_ref[...].astype(F32)
            o_ref[...] = res.astype(o_ref.dtype)

        if nk == 1:
            finish(prod)
            return
        k = pl.program_id(2)

        @pl.when(k == 0)
        def _():
            acc_ref[...] = prod

        @pl.when((k > 0) & (k < nk - 1))
        def _():
            acc_ref[...] += prod

        @pl.when(k == nk - 1)
        def _():
            finish(acc_ref[...] + prod)

    outs = pl.pallas_call(
        body, name=name, grid=grid,
        in_specs=in_specs + [_ANY] * n_ride, out_specs=[out_spec] + [_ANY] * n_ride,
        out_shape=[out_shape] + [s for r in rides for s in r.out_shapes()],
        scratch_shapes=([pltpu.VMEM((bm, bn), F32)] if nk > 1 else []) + _ride_scratch(rides),
        input_output_aliases=aliases,
        compiler_params=_cparams(("arbitrary",) * 3 if rides else ("parallel", "parallel", "arbitrary")),
    )(*operands, *[a for r in rides for a in r.arrays])
    if not rides:
        return outs[0]
    return outs[0], _ride_outputs(rides, outs[1:])


def _row_block(T, S):
    return _pick(math.gcd(T, S), 256, 16)


def _ln_stats(x):
    mu = jnp.mean(x, axis=-1, keepdims=True)
    xc = x - mu
    var = jnp.mean(xc * xc, axis=-1, keepdims=True)
    rstd = lax.rsqrt(var + EPS)
    return xc * rstd, rstd


def _ln_bwd(dxhat, xhat, rstd):
    m1 = jnp.mean(dxhat, axis=-1, keepdims=True)
    m2 = jnp.mean(dxhat * xhat, axis=-1, keepdims=True)
    return rstd * (dxhat - m1 - xhat * m2)


def _sel(ref, is_ctx):
    return jnp.where(is_ctx, ref[1:2, :], ref[0:1, :])


def _mod_fwd(x, shift, scale, S, *, rows=None, name):
    T, D = (x.shape[0] if rows is None else rows), x.shape[1]
    bt = _row_block(T, S)

    def body(x_ref, sh_ref, sc_ref, o_ref):
        is_ctx = pl.program_id(0) * bt >= S
        xhat, _ = _ln_stats(x_ref[...])
        o_ref[...] = (xhat * (1.0 + _sel(sc_ref, is_ctx)) + _sel(sh_ref, is_ctx)).astype(o_ref.dtype)

    return pl.pallas_call(
        body, name=name, grid=(T // bt,),
        in_specs=[pl.BlockSpec((bt, D), lambda i: (i, 0)), pl.BlockSpec((2, D), lambda i: (0, 0)),
                  pl.BlockSpec((2, D), lambda i: (0, 0))],
        out_specs=pl.BlockSpec((bt, D), lambda i: (i, 0)),
        out_shape=jax.ShapeDtypeStruct((T, D), BF16),
        compiler_params=_cparams(("parallel",)),
    )(x, shift, scale)


def _acc_groups(acc_ref, row, val, is_ctx):
    f = jnp.where(is_ctx, 1.0, 0.0).astype(F32)
    acc_ref[row:row + 1, :] += val * (1.0 - f)
    acc_ref[row + 1:row + 2, :] += val * f


def _mod_bwd(x, dh, scale, dx_in, S, *, name):
    T, D = dh.shape
    bt = _row_block(T, S)
    in_blocks = dx_in.shape[0] // bt

    def body(x_ref, dh_ref, sc_ref, dxi_ref, dx_ref, acc_ref):
        i = pl.program_id(0)
        is_ctx = i * bt >= S

        @pl.when(i == 0)
        def _():
            acc_ref[...] = jnp.zeros_like(acc_ref)

        xhat, rstd = _ln_stats(x_ref[...])
        dh = dh_ref[...].astype(F32)
        dxhat = dh * (1.0 + _sel(sc_ref, is_ctx))
        dxi = dxi_ref[...] if in_blocks * bt == T else jnp.where(i < in_blocks, dxi_ref[...], 0.0)
        dx_ref[...] = dxi + _ln_bwd(dxhat, xhat, rstd)
        _acc_groups(acc_ref, 0, jnp.sum(dh, axis=0, keepdims=True), is_ctx)
        _acc_groups(acc_ref, 2, jnp.sum(dh * xhat, axis=0, keepdims=True), is_ctx)

    return pl.pallas_call(
        body, name=name, grid=(T // bt,),
        in_specs=[pl.BlockSpec((bt, D), lambda i: (i, 0)), pl.BlockSpec((bt, D), lambda i: (i, 0)),
                  pl.BlockSpec((2, D), lambda i: (0, 0)),
                  pl.BlockSpec((bt, D), lambda i: (jnp.minimum(i, in_blocks - 1), 0))],
        out_specs=[pl.BlockSpec((bt, D), lambda i: (i, 0)), pl.BlockSpec((8, D), lambda i: (0, 0))],
        out_shape=[jax.ShapeDtypeStruct((T, D), F32), jax.ShapeDtypeStruct((8, D), F32)],
        compiler_params=_cparams(("arbitrary",)),
    )(x, dh, scale, dx_in)


def _res_fwd(x, z, gate, lg, lb, S, *, name):
    T, D = z.shape
    bt = _row_block(T, S)

    def body(x_ref, z_ref, g_ref, lg_ref, lb_ref, o_ref):
        is_ctx = pl.program_id(0) * bt >= S
        u = DEEPNORM_ALPHA * x_ref[...] + _sel(g_ref, is_ctx) * z_ref[...]
        uhat, _ = _ln_stats(u)
        o_ref[...] = uhat * lg_ref[...] + lb_ref[...]

    row = pl.BlockSpec((bt, D), lambda i: (i, 0))
    return pl.pallas_call(
        body, name=name, grid=(T // bt,),
        in_specs=[row, row, pl.BlockSpec((2, D), lambda i: (0, 0)), pl.BlockSpec((1, D), lambda i: (0, 0)),
                  pl.BlockSpec((1, D), lambda i: (0, 0))],
        out_specs=row,
        out_shape=jax.ShapeDtypeStruct((T, D), F32),
        compiler_params=_cparams(("parallel",)),
    )(x, z, gate, lg, lb)


def _res_bwd(x, z, gate, lg, dy, S, *, name):
    T, D = z.shape
    bt = _row_block(T, S)

    def body(x_ref, z_ref, g_ref, lg_ref, dy_ref, dx_ref, dz_ref, acc_ref):
        i = pl.program_id(0)
        is_ctx = i * bt >= S

        @pl.when(i == 0)
        def _():
            acc_ref[...] = jnp.zeros_like(acc_ref)

        gate_v = _sel(g_ref, is_ctx)
        zv = z_ref[...]
        u = DEEPNORM_ALPHA * x_ref[...] + gate_v * zv
        uhat, rstd = _ln_stats(u)
        dyv = dy_ref[...]
        du = _ln_bwd(dyv * lg_ref[...], uhat, rstd)
        dx_ref[...] = DEEPNORM_ALPHA * du
        dz_ref[...] = (gate_v * du).astype(dz_ref.dtype)
        _acc_groups(acc_ref, 0, jnp.sum(du * zv, axis=0, keepdims=True), is_ctx)
        acc_ref[2:3, :] += jnp.sum(dyv * uhat, axis=0, keepdims=True)
        acc_ref[3:4, :] += jnp.sum(dyv, axis=0, keepdims=True)

    row = pl.BlockSpec((bt, D), lambda i: (i, 0))
    return pl.pallas_call(
        body, name=name, grid=(T // bt,),
        in_specs=[row, row, pl.BlockSpec((2, D), lambda i: (0, 0)), pl.BlockSpec((1, D), lambda i: (0, 0)), row],
        out_specs=[row, row, pl.BlockSpec((8, D), lambda i: (0, 0))],
        out_shape=[jax.ShapeDtypeStruct((T, D), F32), jax.ShapeDtypeStruct((T, D), BF16),
                   jax.ShapeDtypeStruct((8, D), F32)],
        compiler_params=_cparams(("arbitrary",)),
    )(x, z, gate, lg, dy)


def _loss_fwd_bwd(y, target, S, *, name):
    T, D = y.shape
    bt = _row_block(T, S)
    n_lat = S // bt

    def body(y_ref, t_ref, dy_ref, l_ref):
        i = pl.program_id(0)

        @pl.when(i == 0)
        def _():
            l_ref[...] = jnp.zeros_like(l_ref)

        keep = jnp.where(i * bt >= S, 0.0, 1.0).astype(F32)
        err = (y_ref[...] - t_ref[...]) * keep
        dy_ref[...] = err * (1.0 / D)
        l_ref[...] += jnp.sum(err * err) * (0.5 / D)

    return pl.pallas_call(
        body, name=name, grid=(T // bt,),
        in_specs=[pl.BlockSpec((bt, D), lambda i: (i, 0)),
                  pl.BlockSpec((bt, D), lambda i: (jnp.minimum(i, n_lat - 1), 0))],
        out_specs=[pl.BlockSpec((bt, D), lambda i: (i, 0)), pl.BlockSpec((8, LANE), lambda i: (0, 0))],
        out_shape=[jax.ShapeDtypeStruct((T, D), F32), jax.ShapeDtypeStruct((8, LANE), F32)],
        compiler_params=_cparams(("arbitrary",)),
    )(y, target)


def _rope_tables(S, C, dim):
    half = dim // 4
    t = jnp.arange(S)
    row = (t // GRID_W).astype(F32)
    col = (t % GRID_W).astype(F32)
    inv = ROPE_THETA ** (-jnp.arange(half, dtype=F32) / half)
    ar, ac = row[:, None] * inv[None, :], col[:, None] * inv[None, :]
    cos = jnp.concatenate([jnp.cos(ar), jnp.cos(ar), jnp.cos(ac), jnp.cos(ac)], axis=1)
    ss = jnp.concatenate([-jnp.sin(ar), jnp.sin(ar), -jnp.sin(ac), jnp.sin(ac)], axis=1)
    cos = jnp.pad(cos, ((0, C), (0, LANE - dim)), constant_values=1.0)
    ss = jnp.pad(ss, ((0, C), (0, LANE - dim)))
    return cos, ss


def _rope(x, cos, ss, half):
    lane = lax.broadcasted_iota(jnp.int32, x.shape, 1)
    first = (lane % (2 * half)) < half
    partner = jnp.where(first, pltpu.roll(x, LANE - half, 1), pltpu.roll(x, half, 1))
    return x * cos + partner * ss


def _rms(x):
    r = lax.rsqrt(jnp.mean(x * x, axis=-1, keepdims=True) + EPS)
    return x * r, r


_CAST_BLOCKS = tuple(range(0, 12)) + (18, 19, 30, 31)
_ROPE_BLOCKS = tuple(range(12, 18))
_GQA_Q_BLOCKS = tuple(range(24, 28))
_GQA_K_BLOCKS = (28, 29)


def _prep_fwd(p, tabs, gq, gk, mq, mkv, S, *, name):
    T = p.shape[0]
    bt = _row_block(T, S)
    cA, sA, cP, sP = tabs

    def body(p_ref, cA_ref, sA_ref, cP_ref, sP_ref, gq_ref, gk_ref, mq_ref, mkv_ref, o_ref):
        def blk(b):
            return p_ref[:, b * LANE:(b + 1) * LANE].astype(F32)

        def put(b, val):
            o_ref[:, b * LANE:(b + 1) * LANE] = val.astype(o_ref.dtype)

        cA_v, sA_v = cA_ref[...], sA_ref[...]
        for b in _CAST_BLOCKS:
            put(b, blk(b))
        for b in _ROPE_BLOCKS:
            put(b, _rope(blk(b), cA_v, sA_v, 32))
        for b in _GQA_Q_BLOCKS:
            put(b, _rope(_rms(blk(b))[0] * gq_ref[...], cA_v, sA_v, 32))
        for b in _GQA_K_BLOCKS:
            put(b, _rope(_rms(blk(b))[0] * gk_ref[...], cA_v, sA_v, 32))
        put(CB_KPE, _rope(blk(CB_KPE), cP_ref[...], sP_ref[...], 16))
        cq = p_ref[:, CB_CQ * LANE:CB_CKV * LANE].astype(F32)
        o_ref[:, CB_CQ * LANE:CB_CKV * LANE] = (_rms(cq)[0] * mq_ref[...]).astype(o_ref.dtype)
        put(CB_CKV, _rms(blk(CB_CKV))[0] * mkv_ref[...])

    row128 = pl.BlockSpec((bt, LANE), lambda i: (i, 0))
    vec = lambda n: pl.BlockSpec((1, n), lambda i: (0, 0))
    return pl.pallas_call(
        body, name=name, grid=(T // bt,),
        in_specs=[pl.BlockSpec((bt, PCOLS), lambda i: (i, 0)), row128, row128, row128, row128,
                  vec(LANE), vec(LANE), vec(MLA_Q_LORA), vec(LANE)],
        out_specs=pl.BlockSpec((bt, PCOLS), lambda i: (i, 0)),
        out_shape=jax.ShapeDtypeStruct((T, PCOLS), BF16),
        compiler_params=_cparams(("parallel",)),
    )(p, cA, sA, cP, sP, gq, gk, mq, mkv)


def _prep_bwd(p, grads, tabs, gq, gk, mq, mkv, S, *, name):
    T = p.shape[0]
    bt = _row_block(T, S)
    cA, sA, cP, sP = tabs
    arrays = []
    where = {}
    for key, (arr, cb) in grads.items():
        idx = next((n for n, a in enumerate(arrays) if a is arr), None)
        if idx is None:
            arrays.append(arr)
            idx = len(arrays) - 1
        where[key] = (idx, cb)
    ng = len(arrays)

    def body(*refs):
        p_ref, cA_ref, sA_ref, cP_ref, sP_ref, gq_ref, gk_ref, mq_ref, mkv_ref = refs[:9]
        g_refs = refs[9:9 + ng]
        o_ref, acc_ref = refs[9 + ng:]
        i = pl.program_id(0)

        @pl.when(i == 0)
        def _():
            acc_ref[...] = jnp.zeros_like(acc_ref)

        def blk(b):
            return p_ref[:, b * LANE:(b + 1) * LANE].astype(F32)

        def grad(b, width=LANE):
            idx, cb = where[b]
            return g_refs[idx][:, cb * LANE:cb * LANE + width].astype(F32)

        def put(b, val):
            o_ref[:, b * LANE:(b + 1) * LANE] = val.astype(o_ref.dtype)

        def rms_bwd(x, dy, g, row, width):
            n, r = _rms(x)
            acc_ref[row:row + 1, 0:width] += jnp.sum(dy * n, axis=0, keepdims=True)
            dn = dy * g
            return r * (dn - n * jnp.mean(dn * n, axis=-1, keepdims=True))

        cA_v, sA_v = cA_ref[...], sA_ref[...]
        for b in _CAST_BLOCKS:
            put(b, grad(b))
        for b in _ROPE_BLOCKS:
            put(b, _rope(grad(b), cA_v, -sA_v, 32))
        for b in _GQA_Q_BLOCKS:
            put(b, rms_bwd(blk(b), _rope(grad(b), cA_v, -sA_v, 32), gq_ref[...], 0, LANE))
        for b in _GQA_K_BLOCKS:
            put(b, rms_bwd(blk(b), _rope(grad(b), cA_v, -sA_v, 32), gk_ref[...], 1, LANE))
        put(CB_KPE, _rope(grad(CB_KPE), cP_ref[...], -sP_ref[...], 16))
        dcq = rms_bwd(p_ref[:, CB_CQ * LANE:CB_CKV * LANE].astype(F32), grad(CB_CQ, MLA_Q_LORA), mq_ref[...], 2, MLA_Q_LORA)
        o_ref[:, CB_CQ * LANE:CB_CKV * LANE] = dcq.astype(o_ref.dtype)
        put(CB_CKV, rms_bwd(blk(CB_CKV), grad(CB_CKV), mkv_ref[...], 3, LANE))

    row128 = pl.BlockSpec((bt, LANE), lambda i: (i, 0))
    vec = lambda n: pl.BlockSpec((1, n), lambda i: (0, 0))
    g_specs = [pl.BlockSpec((bt, a.shape[1]), lambda i: (i, 0)) for a in arrays]
    return pl.pallas_call(
        body, name=name, grid=(T // bt,),
        in_specs=[pl.BlockSpec((bt, PCOLS), lambda i: (i, 0)), row128, row128, row128, row128,
                  vec(LANE), vec(LANE), vec(MLA_Q_LORA), vec(LANE)] + g_specs,
        out_specs=[pl.BlockSpec((bt, PCOLS), lambda i: (i, 0)), pl.BlockSpec((8, MLA_Q_LORA), lambda i: (0, 0))],
        out_shape=[jax.ShapeDtypeStruct((T, PCOLS), BF16), jax.ShapeDtypeStruct((8, MLA_Q_LORA), F32)],
        compiler_params=_cparams(("arbitrary",)),
    )(p, cA, sA, cP, sP, gq, gk, mq, mkv, *arrays)


def _pe_rope(qm, tabs, S, sign, out_dtype, *, name):
    T, N = qm.shape
    bt = _row_block(T, S)
    cP, sP = tabs[2], tabs[3]

    def body(x_ref, c_ref, s_ref, o_ref):
        for b in range(MLA_HEADS):
            o_ref[:, b * LANE:(b + 1) * LANE] = x_ref[:, b * LANE:(b + 1) * LANE].astype(o_ref.dtype)
        for b in range(MLA_HEADS, 2 * MLA_HEADS):
            x = x_ref[:, b * LANE:(b + 1) * LANE].astype(F32)
            o_ref[:, b * LANE:(b + 1) * LANE] = _rope(x, c_ref[...], sign * s_ref[...], 16).astype(o_ref.dtype)

    row128 = pl.BlockSpec((bt, LANE), lambda i: (i, 0))
    return pl.pallas_call(
        body, name=name, grid=(T // bt,),
        in_specs=[pl.BlockSpec((bt, N), lambda i: (i, 0)), row128, row128],
        out_specs=pl.BlockSpec((bt, N), lambda i: (i, 0)),
        out_shape=jax.ShapeDtypeStruct((T, N), out_dtype),
        compiler_params=_cparams(("parallel",)),
    )(qm, cP, sP)


def _dot_nt(a, b):
    return lax.dot_general(a, b, (((1,), (1,)), ((), ())), preferred_element_type=F32)


def _dot_tn(a, b):
    return lax.dot_general(a, b, (((0,), (0,)), ((), ())), preferred_element_type=F32)


def _dot(a, b):
    return jnp.dot(a, b, preferred_element_type=F32)


def _window_fns(band, S, n_var):
    n_lat = S // BQ
    if band is None:
        return None
    reach, span = band

    def fns(j):
        start = jnp.clip(j - reach, 0, n_lat - span)
        return start, jnp.clip(j - start, 0, n_var - 1)

    return fns


class _AttnCfg:
    def __init__(self, *, Hkv, G, S, C, band, scale, n_var=0, bias_per_head=False, has_sink=False, two=False, bq=BQ,
                 ctx_queries=True):
        self.Hkv, self.G, self.S, self.C, self.band, self.scale = Hkv, G, S, C, band, scale
        self.n_var, self.bias_per_head, self.has_sink, self.two = n_var, bias_per_head, has_sink, two
        self.W = S if band is None else band[1] * BQ
        self.T = S + C
        self.bq, self.ctx_queries = bq, ctx_queries
        assert band is None or bq == BQ
        assert S % bq == 0 and C % bq == 0


def _attn_probs(cfg, j, q_ref, k_ref, q2_ref, k2_ref, bias_ref, sink_ref):
    G, S, C, W = cfg.G, cfg.S, cfg.C, cfg.W
    is_ctx = j * cfg.bq >= S
    if cfg.band is None:
        off, var = 0, 0
    else:
        start, var = _window_fns(cfg.band, S, cfg.n_var)(j)
        off = pl.multiple_of(start * BQ, BQ)
    qt = q_ref[...]
    qs = jnp.concatenate([qt[:, g * LANE:(g + 1) * LANE] for g in range(G)], axis=0) if G > 1 else qt
    kw = k_ref[pl.ds(off, W), :]
    kc = k_ref[pl.ds(S, C), :]
    s_w = _dot_nt(qs, kw)
    s_c = _dot_nt(qs, kc)
    q2s = k2w = k2c = None
    if cfg.two:
        q2s = q2_ref[...]
        k2w = k2_ref[pl.ds(off, W), :]
        k2c = k2_ref[pl.ds(S, C), :]
        s_w = s_w + _dot_nt(q2s, k2w)
        s_c = s_c + _dot_nt(q2s, k2c)
    operands = (off, var, qs, kw, kc, q2s, k2w, k2c)
    if not cfg.n_var and not cfg.has_sink:
        if cfg.ctx_queries:
            s_w = jnp.where(is_ctx, NEG, s_w)
        m = jnp.maximum(jnp.max(s_w, axis=-1, keepdims=True), jnp.max(s_c, axis=-1, keepdims=True))
        c2 = cfg.scale * math.log2(math.e)
        e_w = jnp.exp2((s_w - m) * c2)
        e_c = jnp.exp2((s_c - m) * c2)
        inv = 1.0 / (jnp.sum(e_w, axis=-1, keepdims=True) + jnp.sum(e_c, axis=-1, keepdims=True))
        return e_w * inv, e_c * inv, None, operands
    s_w = s_w * cfg.scale
    s_c = s_c * cfg.scale
    if cfg.n_var:
        b = bias_ref[0, pl.ds(var, 1)][0]
        s_w = s_w + (jnp.concatenate([b] * G, axis=0) if G > 1 else b)
    if cfg.ctx_queries:
        s_w = jnp.where(is_ctx, NEG, s_w)
    m = jnp.maximum(jnp.max(s_w, axis=-1, keepdims=True), jnp.max(s_c, axis=-1, keepdims=True))
    if cfg.has_sink:
        sink = sink_ref[0][:, 0:1]
        m = jnp.maximum(m, sink)
    e_w = jnp.exp(s_w - m)
    e_c = jnp.exp(s_c - m)
    l = jnp.sum(e_w, axis=-1, keepdims=True) + jnp.sum(e_c, axis=-1, keepdims=True)
    p_s = None
    if cfg.has_sink:
        e_s = jnp.exp(sink - m)
        l = l + e_s
    inv = 1.0 / l
    if cfg.has_sink:
        p_s = e_s * inv
    return e_w * inv, e_c * inv, p_s, operands


def _attn_specs(cfg, q_cb, k_cb, v_cb, q2_cb, k2_cb):
    G, T, bq = cfg.G, cfg.T, cfg.bq
    specs = [pl.BlockSpec((bq, G * LANE), lambda h, j: (j, q_cb // G + h)),
             pl.BlockSpec((T, LANE), lambda h, j: (0, k_cb + h)),
             pl.BlockSpec((T, LANE), lambda h, j: (0, v_cb + h))]
    if cfg.two:
        specs += [pl.BlockSpec((bq, LANE), lambda h, j: (j, q2_cb + h)),
                  pl.BlockSpec((T, LANE), lambda h, j: (0, k2_cb))]
    if cfg.n_var:
        if cfg.bias_per_head:
            specs.append(pl.BlockSpec((1, cfg.n_var, BQ, cfg.W), lambda h, j: (h, 0, 0, 0)))
        else:
            specs.append(pl.BlockSpec((1, cfg.n_var, BQ, cfg.W), lambda h, j: (0, 0, 0, 0)))
    if cfg.has_sink:
        specs.append(pl.BlockSpec((1, G * BQ, LANE), lambda h, j: (h, 0, 0)))
    return specs


def _attn_unpack(cfg, refs):
    refs = list(refs)
    q_ref, k_ref, v_ref = refs[:3]
    n = 3
    q2_ref = k2_ref = bias_ref = sink_ref = None
    if cfg.two:
        q2_ref, k2_ref = refs[n:n + 2]
        n += 2
    if cfg.n_var:
        bias_ref = refs[n]
        n += 1
    if cfg.has_sink:
        sink_ref = refs[n]
        n += 1
    return (q_ref, k_ref, v_ref, q2_ref, k2_ref, bias_ref, sink_ref), refs[n:]


def _attn_fwd(cfg, q, q_cb, k, k_cb, v, v_cb, *, q2=None, q2_cb=0, k2=None, k2_cb=0, bias=None, sink=None, rides=(),
              into=None, name):
    G, T, S, C, W = cfg.G, cfg.T, cfg.S, cfg.C, cfg.W
    assert q_cb % G == 0
    operands = [q, k, v] + ([q2, k2] if cfg.two else []) + ([bias] if cfg.n_var else []) + ([sink] if cfg.has_sink else [])

    bq = cfg.bq
    n_ride = sum(len(r.arrays) for r in rides)
    n_q = T // bq
    has_buf = into is not None and into[0] is not None
    col0 = 0 if into is None else into[1]
    width = cfg.Hkv * G * LANE if into is None else into[2] * LANE
    assert col0 % G == 0

    def body(*refs):
        (q_ref, k_ref, v_ref, q2_ref, k2_ref, bias_ref, sink_ref), rest = _attn_unpack(cfg, refs)
        rest = rest[:n_ride] + rest[n_ride + has_buf:]
        o_ref = rest[n_ride]
        ride_groups = _ride_split(rides, rest[:n_ride], rest[n_ride + 1:2 * n_ride + 1], rest[2 * n_ride + 1:])
        h = pl.program_id(0)
        j = pl.program_id(1)
        if rides:
            pl.when((h == 0) & (j == 0))(lambda: _ride_start(rides, ride_groups))

        def block():
            p_w, p_c, _, (off, _, _, _, _, _, _, _) = _attn_probs(cfg, j, q_ref, k_ref, q2_ref, k2_ref, bias_ref, sink_ref)
            o = _dot(p_w.astype(BF16), v_ref[pl.ds(off, W), :]) + _dot(p_c.astype(BF16), v_ref[pl.ds(S, C), :])
            for g in range(G):
                o_ref[:, g * LANE:(g + 1) * LANE] = o[g * bq:(g + 1) * bq].astype(o_ref.dtype)

        if cfg.ctx_queries:
            block()
        else:
            pl.when(j * bq < S)(block)

            @pl.when(j * bq >= S)
            def _():
                o_ref[...] = jnp.zeros_like(o_ref)

        if rides:
            pl.when((h == cfg.Hkv - 1) & (j == n_q - 1))(lambda: _ride_wait(rides, ride_groups))

    aliases = _ride_aliases(rides, len(operands), 1)
    if has_buf:
        aliases[len(operands) + n_ride] = 0
    outs = pl.pallas_call(
        body, name=name, grid=(cfg.Hkv, n_q),
        in_specs=_attn_specs(cfg, q_cb, k_cb, v_cb, q2_cb, k2_cb) + [_ANY] * (n_ride + has_buf),
        out_specs=[pl.BlockSpec((bq, G * LANE), lambda h, j: (j, col0 // G + h))] + [_ANY] * n_ride,
        out_shape=[jax.ShapeDtypeStruct((T, width), BF16)] + [s for r in rides for s in r.out_shapes()],
        scratch_shapes=_ride_scratch(rides),
        input_output_aliases=aliases,
        compiler_params=_cparams(("arbitrary", "arbitrary") if rides else ("parallel", "parallel")),
    )(*operands, *[a for r in rides for a in r.arrays], *([into[0]] if has_buf else []))
    if not rides:
        return outs[0]
    return outs[0], _ride_outputs(rides, outs[1:])


def _attn_bwd(cfg, q, q_cb, k, k_cb, v, v_cb, do, do_cb, *, q2=None, q2_cb=0, k2=None, k2_cb=0, bias=None, sink=None,
              want_dbias=False, dq_dtype=F32, rides=(), name):
    G, T, S, C, W, Hkv = cfg.G, cfg.T, cfg.S, cfg.C, cfg.W, cfg.Hkv
    assert q_cb % G == 0 and do_cb % G == 0 and not (want_dbias and G > 1)
    operands = [q, k, v] + ([q2, k2] if cfg.two else []) + ([bias] if cfg.n_var else []) + ([sink] if cfg.has_sink else [])
    operands.append(do)
    in_specs = _attn_specs(cfg, q_cb, k_cb, v_cb, q2_cb, k2_cb)
    bq = cfg.bq
    do_blocks = do.shape[0] // bq
    assert do.shape[0] == T or (do.shape[0] == S and not cfg.ctx_queries)
    in_specs.append(pl.BlockSpec((bq, G * LANE), lambda h, j: (jnp.minimum(j, do_blocks - 1), do_cb // G + h)))

    out_specs = [pl.BlockSpec((bq, G * LANE), lambda h, j: (j, h)),
                 pl.BlockSpec((T, LANE), lambda h, j: (0, h)),
                 pl.BlockSpec((T, LANE), lambda h, j: (0, h))]
    out_shape = [jax.ShapeDtypeStruct((T, Hkv * G * LANE), dq_dtype),
                 jax.ShapeDtypeStruct((T, Hkv * LANE), F32),
                 jax.ShapeDtypeStruct((T, Hkv * LANE), F32)]
    if cfg.two:
        out_specs += [pl.BlockSpec((bq, LANE), lambda h, j: (j, h)), pl.BlockSpec((T, LANE), lambda h, j: (0, 0))]
        out_shape += [jax.ShapeDtypeStruct((T, Hkv * LANE), dq_dtype), jax.ShapeDtypeStruct((T, LANE), F32)]
    if want_dbias:
        out_specs.append(pl.BlockSpec((1, cfg.n_var, BQ, W), lambda h, j: (h, 0, 0, 0)))
        out_shape.append(jax.ShapeDtypeStruct((Hkv, cfg.n_var, BQ, W), F32))
    if cfg.has_sink:
        out_specs.append(pl.BlockSpec((1, G * BQ, LANE), lambda h, j: (h, 0, 0)))
        out_shape.append(jax.ShapeDtypeStruct((Hkv, G * BQ, LANE), F32))

    n_ride = sum(len(r.arrays) for r in rides)
    operands += [a for r in rides for a in r.arrays]
    in_specs += [_ANY] * n_ride
    out_specs += [_ANY] * n_ride
    out_shape += [s for r in rides for s in r.out_shapes()]
    n_q = T // bq

    def body(*refs):
        (q_ref, k_ref, v_ref, q2_ref, k2_ref, bias_ref, sink_ref), rest = _attn_unpack(cfg, refs)
        do_ref, ride_in = rest[0], rest[1:1 + n_ride]
        dq_ref, dk_ref, dv_ref = rest[1 + n_ride:4 + n_ride]
        rest = rest[4 + n_ride:]
        dq2_ref = dk2_ref = dbias_ref = dsink_ref = None
        if cfg.two:
            dq2_ref, dk2_ref = rest[:2]
            rest = rest[2:]
        if want_dbias:
            dbias_ref = rest[0]
            rest = rest[1:]
        if cfg.has_sink:
            dsink_ref = rest[0]
            rest = rest[1:]
        ride_groups = _ride_split(rides, ride_in, rest[:n_ride], rest[n_ride:])
        h = pl.program_id(0)
        j = pl.program_id(1)
        if rides:
            pl.when((h == 0) & (j == 0))(lambda: _ride_start(rides, ride_groups))

        @pl.when(j == 0)
        def _():
            dk_ref[...] = jnp.zeros_like(dk_ref)
            dv_ref[...] = jnp.zeros_like(dv_ref)
            if want_dbias:
                dbias_ref[...] = jnp.zeros_like(dbias_ref)
            if cfg.has_sink:
                dsink_ref[...] = jnp.zeros_like(dsink_ref)

        if cfg.two:
            @pl.when((j == 0) & (h == 0))
            def _():
                dk2_ref[...] = jnp.zeros_like(dk2_ref)

        def block():
            p_w, p_c, p_s, (off, var, qs, kw, kc, q2s, k2w, k2c) = _attn_probs(
                cfg, j, q_ref, k_ref, q2_ref, k2_ref, bias_ref, sink_ref)
            dot_ = do_ref[...]
            dos = jnp.concatenate([dot_[:, g * LANE:(g + 1) * LANE] for g in range(G)], axis=0) if G > 1 else dot_
            dos = dos.astype(BF16)
            vw = v_ref[pl.ds(off, W), :]
            vc = v_ref[pl.ds(S, C), :]
            dp_w = _dot_nt(dos, vw)
            dp_c = _dot_nt(dos, vc)
            delta = jnp.sum(p_w * dp_w, axis=-1, keepdims=True) + jnp.sum(p_c * dp_c, axis=-1, keepdims=True)
            ds_w = p_w * (dp_w - delta)
            ds_c = p_c * (dp_c - delta)
            if want_dbias:
                dbias_ref[0, pl.ds(var, 1)] += ds_w[None]
            if cfg.has_sink:
                dsink_ref[0] += jnp.broadcast_to(-(p_s * delta), (G * bq, LANE))
            dsw = (ds_w * cfg.scale).astype(BF16)
            dsc = (ds_c * cfg.scale).astype(BF16)
            dq = _dot(dsw, kw) + _dot(dsc, kc)
            for g in range(G):
                dq_ref[:, g * LANE:(g + 1) * LANE] = dq[g * bq:(g + 1) * bq].astype(dq_ref.dtype)
            dk_ref[pl.ds(off, W), :] += _dot_tn(dsw, qs)
            dk_ref[pl.ds(S, C), :] += _dot_tn(dsc, qs)
            dv_ref[pl.ds(off, W), :] += _dot_tn(p_w.astype(BF16), dos)
            dv_ref[pl.ds(S, C), :] += _dot_tn(p_c.astype(BF16), dos)
            if cfg.two:
                dq2_ref[...] = (_dot(dsw, k2w) + _dot(dsc, k2c)).astype(dq2_ref.dtype)
                dk2_ref[pl.ds(off, W), :] += _dot_tn(dsw, q2s)
                dk2_ref[pl.ds(S, C), :] += _dot_tn(dsc, q2s)

        if cfg.ctx_queries:
            block()
        else:
            pl.when(j * bq < S)(block)

            @pl.when(j * bq >= S)
            def _():
                dq_ref[...] = jnp.zeros_like(dq_ref)
                if cfg.two:
                    dq2_ref[...] = jnp.zeros_like(dq2_ref)

        if rides:
            pl.when((h == Hkv - 1) & (j == n_q - 1))(lambda: _ride_wait(rides, ride_groups))

    outs = pl.pallas_call(
        body, name=name, grid=(Hkv, n_q),
        in_specs=in_specs, out_specs=out_specs, out_shape=out_shape,
        scratch_shapes=_ride_scratch(rides),
        compiler_params=_cparams(("arbitrary", "arbitrary")),
    )(*operands)
    if not rides:
        return outs
    return list(outs[:len(outs) - n_ride]) + [_ride_outputs(rides, outs[len(outs) - n_ride:])]


def _na_bias(rpb, S):
    H = rpb.shape[0]
    rows = S // GRID_W
    pad_l = GRID_W - 1 - (NA_WIN_C - 1)
    ext = jnp.concatenate([jnp.broadcast_to(rpb[:, :, :1], (H, 2 * NA_WIN_R - 1, pad_l)), rpb,
                           jnp.broadcast_to(rpb[:, :, -1:], (H, 2 * NA_WIN_R - 1, pad_l))], axis=2)
    by_col = jnp.stack([ext[:, :, GRID_W - 1 - qc:2 * GRID_W - 1 - qc] for qc in range(GRID_W)], axis=2)
    cq = np.arange(GRID_W)
    c0 = np.clip(cq - NA_WIN_C // 2, 0, GRID_W - NA_WIN_C)
    col_in = (cq[None, :] >= c0[:, None]) & (cq[None, :] < c0[:, None] + NA_WIN_C)
    n_lat = S // BQ
    neg_tile = jnp.full((H, GRID_W, GRID_W), NEG, F32)
    variants = []
    for v in range(5):
        j = {0: 0, 1: 1, 2: 2, 3: n_lat - 2, 4: n_lat - 1}[v]
        start = int(np.clip(j - 2, 0, n_lat - 5))
        assert j - start == v
        q_rows = []
        for qr in range(2):
            r = 2 * j + qr
            r0 = int(np.clip(r - NA_WIN_R // 2, 0, rows - NA_WIN_R))
            k_tiles = []
            for kr in range(10):
                krow = 2 * start + kr
                if r0 <= krow < r0 + NA_WIN_R:
                    k_tiles.append(jnp.where(col_in[None], by_col[:, krow - r + NA_WIN_R - 1], NEG))
                else:
                    k_tiles.append(neg_tile)
            q_rows.append(jnp.concatenate(k_tiles, axis=2))
        variants.append(jnp.concatenate(q_rows, axis=1))
    return jnp.stack(variants, axis=1)


def _swa_mask(S):
    qq = np.arange(BQ)[:, None]
    kk = np.arange(3 * BQ)[None, :]
    tiles = [np.where(np.abs(kk - v * BQ - qq) <= SWA_WINDOW, 0.0, NEG) for v in range(3)]
    return jnp.asarray(np.stack(tiles)[None], F32)


def _ffn_tiles(T, S, F):
    return _row_block(T, S), _pick(F, 1408)


def _halo_rows(dtype):
    return 8 * 4 // jnp.dtype(dtype).itemsize


def _halo_specs(T, bt, bf, dtype):
    hr = _halo_rows(dtype)
    nh = bt // hr
    return [pl.BlockSpec((bt, bf), lambda f, i: (i, f)),
            pl.BlockSpec((hr, bf), lambda f, i: (jnp.maximum(i * nh - 1, 0), f)),
            pl.BlockSpec((hr, bf), lambda f, i: (jnp.minimum((i + 1) * nh, T // hr - 1), f))]


def _neighbours(x, prev, nxt, i, bt, S, T):
    r = lax.broadcasted_iota(jnp.int32, x.shape, 0)
    g0 = i * bt
    first_open = jnp.logical_or(g0 == 0, g0 == S)
    last_open = jnp.logical_or(g0 + bt == S, g0 + bt == T)
    hr = prev.shape[0]
    before = jnp.where(r == 0, jnp.where(first_open, 0.0, prev[hr - 1:hr, :].astype(F32)), pltpu.roll(x, 1, 0))
    after = jnp.where(r == bt - 1, jnp.where(last_open, 0.0, nxt[0:1, :].astype(F32)), pltpu.roll(x, bt - 1, 0))
    return before, after


def _sigmoid(a):
    return 1.0 / (1.0 + jnp.exp(-a))


def _ffn_fwd(gp, u, cw, cb, S, *, name):
    T, F = gp.shape
    bt, bf = _ffn_tiles(T, S, F)

    def body(g_ref, gp_ref, gn_ref, u_ref, w_ref, b_ref, o_ref):
        i = pl.program_id(1)
        g = g_ref[...].astype(F32)
        before, after = _neighbours(g, gp_ref[...], gn_ref[...], i, bt, S, T)
        a = before * w_ref[0:1, :] + g * w_ref[1:2, :] + after * w_ref[2:3, :] + b_ref[...]
        o_ref[...] = (a * _sigmoid(a) * u_ref[...].astype(F32)).astype(o_ref.dtype)

    return pl.pallas_call(
        body, name=name, grid=(F // bf, T // bt),
        in_specs=_halo_specs(T, bt, bf, gp.dtype) + [pl.BlockSpec((bt, bf), lambda f, i: (i, f)),
                                              pl.BlockSpec((3, bf), lambda f, i: (0, f)),
                                              pl.BlockSpec((1, bf), lambda f, i: (0, f))],
        out_specs=pl.BlockSpec((bt, bf), lambda f, i: (i, f)),
        out_shape=jax.ShapeDtypeStruct((T, F), BF16),
        compiler_params=_cparams(("parallel", "parallel")),
    )(gp, gp, gp, u, cw, cb)


def _ffn_bwd_act(gp, u, da_out, cw, cb, S, *, name):
    T, F = gp.shape
    bt, bf = _ffn_tiles(T, S, F)

    def body(g_ref, gp_ref, gn_ref, u_ref, d_ref, w_ref, b_ref, da_ref, du_ref, acc_ref):
        i = pl.program_id(1)

        @pl.when(i == 0)
        def _():
            acc_ref[...] = jnp.zeros_like(acc_ref)

        g = g_ref[...].astype(F32)
        before, after = _neighbours(g, gp_ref[...], gn_ref[...], i, bt, S, T)
        a = before * w_ref[0:1, :] + g * w_ref[1:2, :] + after * w_ref[2:3, :] + b_ref[...]
        sig = _sigmoid(a)
        d = d_ref[...].astype(F32)
        du_ref[...] = (d * (a * sig)).astype(du_ref.dtype)
        da = d * u_ref[...].astype(F32) * (sig * (1.0 + a * (1.0 - sig)))
        da_ref[...] = da
        acc_ref[0:1, :] += jnp.sum(da * before, axis=0, keepdims=True)
        acc_ref[1:2, :] += jnp.sum(da * g, axis=0, keepdims=True)
        acc_ref[2:3, :] += jnp.sum(da * after, axis=0, keepdims=True)
        acc_ref[3:4, :] += jnp.sum(da, axis=0, keepdims=True)

    blk = pl.BlockSpec((bt, bf), lambda f, i: (i, f))
    return pl.pallas_call(
        body, name=name, grid=(F // bf, T // bt),
        in_specs=_halo_specs(T, bt, bf, gp.dtype) + [blk, blk,
                                              pl.BlockSpec((3, bf), lambda f, i: (0, f)),
                                              pl.BlockSpec((1, bf), lambda f, i: (0, f))],
        out_specs=[blk, blk, pl.BlockSpec((8, bf), lambda f, i: (0, f))],
        out_shape=[jax.ShapeDtypeStruct((T, F), F32), jax.ShapeDtypeStruct((T, F), BF16),
                   jax.ShapeDtypeStruct((8, F), F32)],
        compiler_params=_cparams(("parallel", "arbitrary")),
    )(gp, gp, gp, u, da_out, cw, cb)


def _ffn_bwd_conv(da, cw, S, *, name):
    T, F = da.shape
    bt, bf = _ffn_tiles(T, S, F)

    def body(d_ref, dp_ref, dn_ref, w_ref, o_ref):
        i = pl.program_id(1)
        d = d_ref[...]
        before, after = _neighbours(d, dp_ref[...], dn_ref[...], i, bt, S, T)
        o_ref[...] = (after * w_ref[0:1, :] + d * w_ref[1:2, :] + before * w_ref[2:3, :]).astype(o_ref.dtype)

    return pl.pallas_call(
        body, name=name, grid=(F // bf, T // bt),
        in_specs=_halo_specs(T, bt, bf, da.dtype) + [pl.BlockSpec((3, bf), lambda f, i: (0, f))],
        out_specs=pl.BlockSpec((bt, bf), lambda f, i: (i, f)),
        out_shape=jax.ShapeDtypeStruct((T, F), BF16),
        compiler_params=_cparams(("parallel", "parallel")),
    )(da, da, da, cw)


def _ew_rows(R, N, n_arrays):
    return _pick(R, max(16, EW_VMEM_BUDGET // (8 * n_arrays * N)), 16)


def _adam(w, g, m, v, *, rides=(), name):
    lead = w.shape[:-2]
    R, N = w.shape[-2:]
    br = _ew_rows(R, N, 7)
    bc1 = 1.0 - ADAM_B1 ** ADAM_STEP
    bc2 = 1.0 - ADAM_B2 ** ADAM_STEP
    grid = lead + (R // br,)
    n_ride = sum(len(r.arrays) for r in rides)

    def body(*refs):
        w_ref, g_ref, m_ref, v_ref = refs[:4]
        d_ref, mo_ref, vo_ref = refs[4 + n_ride:7 + n_ride]
        if rides:
            ride_groups = _ride_split(rides, refs[4:4 + n_ride], refs[7 + n_ride:7 + 2 * n_ride], refs[7 + 2 * n_ride:])
            steps = [pl.program_id(d) for d in range(len(grid))]
            pl.when(functools.reduce(jnp.logical_and, [s == 0 for s in steps]))(lambda: _ride_start(rides, ride_groups))
        gv = g_ref[...]
        mn = ADAM_B1 * m_ref[...] + (1.0 - ADAM_B1) * gv
        vn = ADAM_B2 * v_ref[...] + (1.0 - ADAM_B2) * (gv * gv)
        mo_ref[...] = mn
        vo_ref[...] = vn
        d_ref[...] = -ADAM_LR * ((mn / bc1) / (jnp.sqrt(vn / bc2) + ADAM_EPS) + ADAM_WD * w_ref[...])
        if rides:
            pl.when(functools.reduce(jnp.logical_and, [s == n - 1 for s, n in zip(steps, grid)]))(
                lambda: _ride_wait(rides, ride_groups))

    if lead:
        blk = pl.BlockSpec((None, br, N), lambda l, i: (l, i, 0))
    else:
        blk = pl.BlockSpec((br, N), lambda i: (i, 0))
    shp = jax.ShapeDtypeStruct(w.shape, F32)
    outs = pl.pallas_call(
        body, name=name, grid=grid,
        in_specs=[blk, blk, blk, blk] + [_ANY] * n_ride, out_specs=[blk, blk, blk] + [_ANY] * n_ride,
        out_shape=[shp, shp, shp] + [s for r in rides for s in r.out_shapes()],
        scratch_shapes=_ride_scratch(rides),
        input_output_aliases=_ride_aliases(rides, 4, 3),
        compiler_params=_cparams(("arbitrary" if rides else "parallel",) * len(grid)),
    )(w, g, m, v, *[a for r in rides for a in r.arrays])
    if not rides:
        return outs
    return outs[:3], _ride_outputs(rides, outs[3:])


def _sum_lead(x, out_dtype, *, name):
    n, R, N = x.shape
    br = _ew_rows(R, N, n + 1)

    def body(x_ref, o_ref):
        acc = x_ref[0].astype(F32)
        for k in range(1, n):
            acc = acc + x_ref[k].astype(F32)
        o_ref[...] = acc.astype(o_ref.dtype)

    return pl.pallas_call(
        body, name=name, grid=(R // br,),
        in_specs=[pl.BlockSpec((n, br, N), lambda i: (0, i, 0))],
        out_specs=pl.BlockSpec((br, N), lambda i: (i, 0)),
        out_shape=jax.ShapeDtypeStruct((R, N), out_dtype),
        compiler_params=_cparams(("parallel",)),
    )(x)


def _sum_parts(parts, landed, chip, core, stack, *, name):
    _, R, N = parts.shape
    n_layers, layer, buf = stack
    br = _ew_rows(R, N, 5)

    def body(pos_ref, own_ref, landed_ref, *rest):
        o_ref = rest[-1]
        acc = own_ref[...].astype(F32)
        for k in range(3):
            acc = acc + landed_ref[k].astype(F32)
        o_ref[...] = acc

    operands = [jnp.stack([chip, core]).astype(jnp.int32), parts, landed]
    in_specs = [pl.BlockSpec((None, br, N), lambda i, pos: (pos[0], i, 0)),
                pl.BlockSpec((3, br, N), lambda i, pos: (0, i, 0))]
    aliases = {}
    if buf is not None:
        aliases = {3: 0}
        operands.append(buf)
        in_specs.append(pl.BlockSpec(memory_space=pl.ANY))
    return pl.pallas_call(
        body, name=name,
        grid_spec=pltpu.PrefetchScalarGridSpec(
            num_scalar_prefetch=1, grid=(R // br,), in_specs=in_specs,
            out_specs=pl.BlockSpec((None, None, br, N), lambda i, pos: (layer, pos[1], i, 0))),
        out_shape=jax.ShapeDtypeStruct((n_layers, 2, R, N), F32),
        input_output_aliases=aliases,
        compiler_params=_cparams(("parallel",)),
    )(*operands)


def _place_own(shards, layer, core, slot, *, name):
    _, R, N = shards.shape
    br = _ew_rows(R // 2, N, 2)
    nb = R // 2 // br

    def body(pos_ref, x_ref, o_ref):
        o_ref[...] = x_ref[...].astype(o_ref.dtype)

    return pl.pallas_call(
        body, name=name,
        grid_spec=pltpu.PrefetchScalarGridSpec(
            num_scalar_prefetch=1, grid=(nb,),
            in_specs=[pl.BlockSpec((None, br, N), lambda i, pos: (layer, pos[0] * nb + i, 0))],
            out_specs=pl.BlockSpec((None, br, N), lambda i, pos: (pos[1], i, 0))),
        out_shape=jax.ShapeDtypeStruct((8, R // 2, N), BF16),
        compiler_params=_cparams(("parallel",)),
    )(jnp.stack([core, slot]).astype(jnp.int32), shards)


def _add_half(g, r, core, *, name):
    Q, _, R, N = g.shape
    br = _ew_rows(R, N, 3)

    def body(c_ref, g_ref, r_ref, o_ref):
        o_ref[...] = (g_ref[...] + r_ref[...]).astype(o_ref.dtype)

    return pl.pallas_call(
        body, name=name,
        grid_spec=pltpu.PrefetchScalarGridSpec(
            num_scalar_prefetch=1, grid=(Q, R // br),
            in_specs=[pl.BlockSpec((None, None, br, N), lambda q, i, c_ref: (q, c_ref[0], i, 0)),
                      pl.BlockSpec((None, br, N), lambda q, i, c_ref: (q, i, 0))],
            out_specs=pl.BlockSpec((None, br, N), lambda q, i, c_ref: (q, i, 0))),
        out_shape=jax.ShapeDtypeStruct((Q, R, N), BF16),
        compiler_params=_cparams(("parallel", "parallel")),
    )(core.reshape(1).astype(jnp.int32), g, r)


_ANY = pl.BlockSpec(memory_space=pl.ANY)


def _place():
    return lax.axis_index("x"), lax.axis_index("y"), lax.axis_index("c")


def _allgather8(blocks, *, name):
    n = len(blocks)

    def body(*refs):
        xs, outs = refs[:n], refs[n:2 * n]
        send_sems, recv_sems, local_sems = refs[2 * n:]
        x, y, c = _place()
        me, sibling = (x, y, c), (x, y, 1 - c)
        chips = [(1 - x, y), (x, 1 - y), (1 - x, 1 - y)]

        def slot(a, px, py, pc):
            return outs[a].at[4 * px + 2 * py + pc]

        def copy(a, k, block, to, src=None):
            return pltpu.make_async_remote_copy(
                src_ref=slot(a, *block) if src is None else src, dst_ref=slot(a, *block),
                send_sem=send_sems.at[a, k], recv_sem=recv_sems.at[a, k], device_id=to, device_id_type=MESH)

        mine = [pltpu.make_async_copy(xs[a], slot(a, *me), local_sems.at[a]) for a in range(n)]
        for cp in mine:
            cp.start()
        first = []
        for a in range(n):
            first.append(copy(a, 0, me, sibling, src=xs[a]))
            first += [copy(a, 1 + j, me, (*chip, c), src=xs[a]) for j, chip in enumerate(chips)]
        for cp in first:
            cp.start()
        passed = []
        for j, chip in enumerate(chips):
            for a in range(n):
                copy(a, 1 + j, (*chip, c), me).wait_recv()
                fwd = copy(a, 4 + j, (*chip, c), sibling)
                fwd.start()
                passed.append(fwd)
        for a in range(n):
            copy(a, 0, sibling, me).wait_recv()
            for j, chip in enumerate(chips):
                copy(a, 4 + j, (*chip, 1 - c), me).wait_recv()
        for cp in first + passed:
            cp.wait_send()
        for cp in mine:
            cp.wait()

    return pl.pallas_call(
        body, name=name,
        in_specs=[_ANY] * n, out_specs=[_ANY] * n,
        out_shape=[jax.ShapeDtypeStruct((8,) + b.shape, b.dtype) for b in blocks],
        scratch_shapes=[pltpu.SemaphoreType.DMA((n, 7)), pltpu.SemaphoreType.DMA((n, 7)), pltpu.SemaphoreType.DMA((n,))],
    )(*blocks)


class _Exchange:
    n_sems = 1

    def __init__(self, arrays):
        self.arrays = list(arrays)

    def out_shapes(self):
        return [jax.ShapeDtypeStruct(g.shape[:1] + g.shape[2:], g.dtype) for g in self.arrays]

    def copy(self, k, src, dst, sems, landing):
        x, y, c = _place()
        return pltpu.make_async_remote_copy(src_ref=src.at[:, 1 - c], dst_ref=dst, send_sem=sems[0], recv_sem=sems[1],
                                            device_id=(x, y, 1 - c), device_id_type=MESH)

    def copies(self, group, landing):
        xs, outs, send_sems, recv_sems = group
        return [self.copy(k, xs[a], outs[a], (send_sems.at[a, k], recv_sems.at[a, k]), landing)
                for a in range(len(xs)) for k in range(self.n_sems)]


class _Scatter(_Exchange):
    n_sems = 3

    def out_shapes(self):
        return [jax.ShapeDtypeStruct((3,) + p.shape[1:], p.dtype) for p in self.arrays]

    def copy(self, k, src, dst, sems, landing):
        x, y, c = _place()
        px, py = [(1 - x, y), (x, 1 - y), (1 - x, 1 - y)][k]
        return pltpu.make_async_remote_copy(src_ref=src.at[2 * px + py], dst_ref=dst.at[k], send_sem=sems[0], recv_sem=sems[1],
                                            device_id=(px, py, c), device_id_type=MESH)


class _GatherChips(_Exchange):
    n_sems = 3
    in_place = True

    def out_shapes(self):
        return [jax.ShapeDtypeStruct(b.shape, b.dtype) for b in self.arrays]

    def copy(self, k, src, dst, sems, landing):
        x, y, c = _place()
        px, py = [(1 - x, y), (x, 1 - y), (1 - x, 1 - y)][k]
        slot = 4 * px + 2 * py + c if landing else 4 * x + 2 * y + c
        return pltpu.make_async_remote_copy(src_ref=src.at[4 * x + 2 * y + c], dst_ref=dst.at[slot], send_sem=sems[0],
                                            recv_sem=sems[1], device_id=(px, py, c), device_id_type=MESH)


class _GatherCores(_GatherChips):
    n_sems = 4

    def copy(self, k, src, dst, sems, landing):
        x, y, c = _place()
        slot = 2 * k + 1 - c if landing else 2 * k + c
        return pltpu.make_async_remote_copy(src_ref=src.at[2 * k + c], dst_ref=dst.at[slot], send_sem=sems[0],
                                            recv_sem=sems[1], device_id=(x, y, 1 - c), device_id_type=MESH)


class _Join(_GatherChips):
    n_sems = 1

    def copy(self, k, src, dst, sems, landing):
        x, y, c = _place()
        return pltpu.make_async_remote_copy(src_ref=src.at[:, c], dst_ref=dst.at[:, 1 - c if landing else c], send_sem=sems[0],
                                            recv_sem=sems[1], device_id=(x, y, 1 - c), device_id_type=MESH)


def _ride_aliases(rides, first_in, first_out):
    aliases, i = {}, 0
    for r in rides:
        for a in range(len(r.arrays)):
            if getattr(r, "in_place", False):
                aliases[first_in + i + a] = first_out + i + a
        i += len(r.arrays)
    return aliases


def _ride_scratch(rides):
    shapes = []
    for r in rides:
        shapes += [pltpu.SemaphoreType.DMA((len(r.arrays), r.n_sems)), pltpu.SemaphoreType.DMA((len(r.arrays), r.n_sems))]
    return shapes


def _ride_split(rides, in_refs, out_refs, sem_refs):
    groups, i, o = [], 0, 0
    for k, r in enumerate(rides):
        n = len(r.arrays)
        groups.append((in_refs[i:i + n], out_refs[o:o + n], sem_refs[2 * k], sem_refs[2 * k + 1]))
        i, o = i + n, o + n
    return groups


def _ride_start(rides, groups):
    for r, g in zip(rides, groups):
        for cp in r.copies(g, False):
            cp.start()


def _ride_wait(rides, groups):
    for r, g in zip(rides, groups):
        for cp in r.copies(g, True):
            cp.wait_recv()
        for cp in r.copies(g, False):
            cp.wait_send()


def _run_rides(rides, *, name):
    n_in = sum(len(r.arrays) for r in rides)

    def body(*refs):
        groups = _ride_split(rides, refs[:n_in], refs[n_in:2 * n_in], refs[2 * n_in:])
        _ride_start(rides, groups)
        _ride_wait(rides, groups)

    outs = pl.pallas_call(
        body, name=name,
        in_specs=[_ANY] * n_in, out_specs=[_ANY] * n_in,
        out_shape=[s for r in rides for s in r.out_shapes()],
        scratch_shapes=_ride_scratch(rides),
        input_output_aliases=_ride_aliases(rides, 0, 0),
    )(*[a for r in rides for a in r.arrays])
    return _ride_outputs(rides, outs)


def _ride_outputs(rides, outs):
    res, o = [], 0
    for r in rides:
        res.append(list(outs[o:o + len(r.arrays)]))
        o += len(r.arrays)
    return res


def _perm_w_in(wt):
    pad = jnp.zeros((PCOLS - IN_COLS, wt.shape[1]), wt.dtype)
    return jnp.concatenate([wt[:3072], wt[3136:IN_COLS], wt[3072:3136], pad], axis=0)


def _unperm_w_in(gt):
    return jnp.concatenate([gt[:3072], gt[4096:IN_COLS], gt[3072:4096]], axis=0)


def _perm_w_uq(w):
    w4 = w.reshape(MLA_Q_LORA, MLA_HEADS, MLA_NOPE + MLA_ROPE)
    nope = w4[:, :, :MLA_NOPE].reshape(MLA_Q_LORA, MLA_HEADS * LANE)
    pe = jnp.pad(w4[:, :, MLA_NOPE:], ((0, 0), (0, 0), (0, LANE - MLA_ROPE))).reshape(MLA_Q_LORA, MLA_HEADS * LANE)
    return jnp.concatenate([nope, pe], axis=1)


def _unperm_w_uq(g):
    nope = g[:, :MLA_HEADS * LANE].reshape(MLA_Q_LORA, MLA_HEADS, LANE)
    pe = g[:, MLA_HEADS * LANE:].reshape(MLA_Q_LORA, MLA_HEADS, LANE)[:, :, :MLA_ROPE]
    return jnp.concatenate([nope, pe], axis=2).reshape(MLA_Q_LORA, MLA_HEADS * (MLA_NOPE + MLA_ROPE))


def _perm_w_ukv(w):
    w4 = w.reshape(MLA_KV_LORA, MLA_HEADS, MLA_NOPE + MLA_V)
    return jnp.concatenate([w4[:, :, :MLA_NOPE].reshape(MLA_KV_LORA, -1), w4[:, :, MLA_NOPE:].reshape(MLA_KV_LORA, -1)], axis=1)


def _unperm_w_ukv(g):
    kn = g[:, :MLA_HEADS * LANE].reshape(MLA_KV_LORA, MLA_HEADS, LANE)
    vv = g[:, MLA_HEADS * LANE:].reshape(MLA_KV_LORA, MLA_HEADS, LANE)
    return jnp.concatenate([kn, vv], axis=2).reshape(MLA_KV_LORA, -1)


def _silu(v):
    return v * jax.nn.sigmoid(v)


def _silu_grad(v):
    s = jax.nn.sigmoid(v)
    return s * (1.0 + v * (1.0 - s))


_WEIGHTS = ("c_ctx", "w_ada", "b_ada", "w_in", "na_rpb", "swa_sink", "mla_q_norm", "mla_kv_norm", "mla_w_uq", "mla_w_ukv",
            "gqa_q_norm", "gqa_k_norm", "w_out", "ln1_g", "ln1_b", "ffn_w_gate", "ffn_w_up", "ffn_conv_w", "ffn_conv_b",
            "ffn_w_down", "ln2_g", "ln2_b")
_COL_SHARDED = ("mla_w_uq", "mla_w_ukv", "ffn_w_gate", "ffn_w_up")
_ROW_SHARDED = ("w_out", "ffn_w_down")
_BIG = ("w_in",) + _COL_SHARDED + _ROW_SHARDED
_SMALL = ("c_ctx", "b_ada", "na_rpb", "swa_sink", "mla_q_norm", "mla_kv_norm", "gqa_q_norm", "gqa_k_norm", "ln1_g", "ln1_b",
          "ffn_conv_w", "ffn_conv_b", "ln2_g", "ln2_b")


def _pack(arrays):
    flat = jnp.concatenate([a.reshape(-1) for a in arrays])
    n = flat.shape[0]
    rows = -(-n // (8 * LANE)) * 8
    return jnp.pad(flat, (0, rows * LANE - n)).reshape(rows, LANE)


def _unpack(packed, like):
    flat = packed.reshape(-1)
    out, o = [], 0
    for a in like:
        out.append(flat[o:o + a.size].reshape(a.shape))
        o += a.size
    return out


def _train_step(x, c, ctx, loss_target, w, m_in, v_in):
    L = DEPTH
    S, D = x.shape[1], x.shape[2]
    C = ctx.shape[1]
    T = S + C
    F = w["ffn_conv_b"].shape[1]
    ax, ay, ac = _place()
    chip = 2 * ax + ay
    dev = 2 * chip + ac
    n_ada = w["w_ada"].shape[2]
    w, m_in, v_in = dict(w), dict(m_in), dict(v_in)
    for d in (w, m_in, v_in):
        d["w_in"] = jnp.swapaxes(d["w_in"], 1, 2)

    gather_groups = {"A": ("w_in", "mla_w_uq", "mla_w_ukv"), "B": ("w_out",), "C": ("ffn_w_gate", "ffn_w_up"), "D": ("ffn_w_down",)}
    full = {n: [None] * L for n in _BIG}
    w_in_p, w_uq_p, w_ukv_p = [None] * L, [None] * L, [None] * L
    half_done = {}

    def chips_step(group, l):
        return _GatherChips([_place_own(w[n], l, ac, dev, name="gather_place") for n in gather_groups[group]])

    def cores_step(group, l):
        return _GatherCores(half_done.pop((group, l)))

    def finish_group(group, l, bufs):
        for n, b in zip(gather_groups[group], bufs):
            r, cols = b.shape[1:]
            if n in _COL_SHARDED:
                full[n][l] = b.reshape(4, 2, r, cols).transpose(1, 2, 0, 3).reshape(2 * r, 4 * cols)
            else:
                full[n][l] = b.reshape(8 * r, cols)
        if group == "A":
            w_in_p[l], w_uq_p[l] = _perm_w_in(full["w_in"][l]), _perm_w_uq(full["mla_w_uq"][l])
            w_ukv_p[l] = _perm_w_ukv(full["mla_w_ukv"][l])

    def with_rides(result, rides):
        return result if rides else (result, [])

    def my_half(a):
        r = a.shape[0] // 2
        return lax.dynamic_slice_in_dim(a, ac * r, r, axis=0).astype(BF16)

    gathered = _allgather8([my_half(w[n][0]) for n in gather_groups["A"]] + [w["ffn_conv_w"]], name="gather_weights")
    finish_group("A", 0, gathered[:-1])
    conv_w = gathered[-1][::2].transpose(1, 2, 0, 3).reshape(L, 3, F)

    (c_all,) = _allgather8([c], name="gather_c")
    c16 = jnp.concatenate([c_all.reshape(8, D), jnp.broadcast_to(w["c_ctx"][None], (8, D))], axis=0)
    row_keep = (jnp.arange(16) <= 8).astype(F32)[:, None]
    sc = _silu(c16) * row_keep
    b_loc = lax.dynamic_slice_in_dim(w["b_ada"], chip * n_ada, n_ada, axis=1)
    mod_loc = jnp.stack([_mm(sc, w["w_ada"], b_layer=l, name="mod_mm") + b_loc[l][None] for l in range(L)])
    (mod_g,) = _allgather8([mod_loc], name="gather_mod")
    mod_all = mod_g[::2].transpose(1, 2, 0, 3).reshape(L, 16, 4 * n_ada)
    mod_x = lax.dynamic_index_in_dim(mod_all, dev, axis=1, keepdims=False)
    mod_c = mod_all[:, 8]
    mods = [jnp.stack([mod_x[l].reshape(6, D), mod_c[l].reshape(6, D)], axis=1) for l in range(L)]

    tabs = _rope_tables(S, C, HEAD_DIM) + _rope_tables(S, C, MLA_ROPE)
    swa_mask = _swa_mask(S)
    scale = HEAD_DIM ** -0.5
    def attn_cfgs(l):
        cq = l < L - 1
        return (_AttnCfg(Hkv=NA_HEADS, G=1, S=S, C=C, band=(2, 5), scale=scale, n_var=5, bias_per_head=True, ctx_queries=cq),
                _AttnCfg(Hkv=SWA_KV_HEADS, G=SWA_HEADS // SWA_KV_HEADS, S=S, C=C, band=(1, 3), scale=scale, n_var=3,
                         has_sink=True, ctx_queries=cq),
                _AttnCfg(Hkv=MLA_HEADS, G=1, S=S, C=C, band=None, scale=(MLA_NOPE + MLA_ROPE) ** -0.5, two=True,
                         bq=2 * BQ, ctx_queries=cq),
                _AttnCfg(Hkv=GQA_KV_HEADS, G=GQA_HEADS // GQA_KV_HEADS, S=S, C=C, band=None, scale=scale, ctx_queries=cq))

    row = lambda a: a[None, :]

    xt = jnp.concatenate([x[0], ctx[0]], axis=0)
    saved = []
    for l in range(L):
        md = mods[l]
        gq, gk, mq, mkv = row(w["gqa_q_norm"][l]), row(w["gqa_k_norm"][l]), row(w["mla_q_norm"][l]), row(w["mla_kv_norm"][l])
        h1 = _mod_fwd(xt, md[0], md[1], S, name="mod_fwd")
        p = _mm(h1, w_in_p[l], mode="nt", out_dtype=BF16, name="in_proj")
        qkv = _prep_fwd(p, tabs, gq, gk, mq, mkv, S, name="prep_fwd")
        qm = _mm(qkv, w_uq_p[l], a_off=CB_CQ * LANE, a_k=MLA_Q_LORA, tk=LANE, name="mla_uq")
        qmb = _pe_rope(qm, tabs, S, 1.0, BF16, name="mla_q_rope")
        kvm = _mm(qkv, w_ukv_p[l], a_off=CB_CKV * LANE, a_k=MLA_KV_LORA, tk=LANE, out_dtype=BF16, name="mla_ukv")
        bias_na = _na_bias(w["na_rpb"][l], S)
        sink = jnp.broadcast_to(jnp.repeat(w["swa_sink"][l].reshape(SWA_KV_HEADS, -1), BQ, axis=1)[:, :, None],
                                (SWA_KV_HEADS, SWA_HEADS // SWA_KV_HEADS * BQ, LANE))
        cfg_na, cfg_swa, cfg_mla, cfg_gqa = attn_cfgs(l)
        rows = T if l < L - 1 else S
        first, more = l == 0, l + 1 < L
        rides = [chips_step("B", l)] if first else []
        mix_blocks = NA_HEADS + SWA_HEADS + MLA_HEADS + GQA_HEADS
        mix, got = with_rides(_attn_fwd(cfg_na, qkv, CB_NA_Q, qkv, CB_NA_K, qkv, CB_NA_V, bias=bias_na, rides=rides,
                                        into=(None, 0, mix_blocks), name="na_fwd"), rides)
        if first:
            half_done[("B", l)] = got[0]
        rides = [cores_step("B", l)] if first else []
        mix, got = with_rides(_attn_fwd(cfg_swa, qkv, CB_SWA_Q, qkv, CB_SWA_K, qkv, CB_SWA_V, bias=swa_mask, sink=sink,
                                        rides=rides, into=(mix, NA_HEADS, mix_blocks), name="swa_fwd"), rides)
        if first:
            finish_group("B", l, got[0])
        mix, got = _attn_fwd(cfg_mla, qmb, 0, kvm, 0, kvm, MLA_HEADS, q2=qmb, q2_cb=MLA_HEADS, k2=qkv, k2_cb=CB_KPE,
                             rides=[chips_step("C", l)], into=(mix, NA_HEADS + SWA_HEADS, mix_blocks), name="mla_fwd")
        half_done[("C", l)] = got[0]
        mix, got = _attn_fwd(cfg_gqa, qkv, CB_GQA_Q, qkv, CB_GQA_K, qkv, CB_GQA_V,
                             rides=[cores_step("C", l), chips_step("D", l)],
                             into=(mix, NA_HEADS + SWA_HEADS + MLA_HEADS, mix_blocks), name="gqa_fwd")
        finish_group("C", l, got[0])
        half_done[("D", l)] = got[1]
        z1 = _mm(mix, full["w_out"][l], rows=rows, name="out_proj")
        x1 = _res_fwd(xt, z1, md[2], row(w["ln1_g"][l]), row(w["ln1_b"][l]), S, name="res_fwd")
        h2 = _mod_fwd(x1, md[3], md[4], S, name="mod_fwd")
        gp, got = _mm(h2, full["ffn_w_gate"][l], rides=[cores_step("D", l)] + ([chips_step("A", l + 1)] if more else []),
                      out_dtype=BF16, name="ffn_in")
        finish_group("D", l, got[0])
        if more:
            half_done[("A", l + 1)] = got[1]
        rides = [cores_step("A", l + 1), chips_step("B", l + 1)] if more else []
        up, got = with_rides(_mm(h2, full["ffn_w_up"][l], rides=rides, out_dtype=BF16, name="ffn_in"), rides)
        if more:
            finish_group("A", l + 1, got[0])
            half_done[("B", l + 1)] = got[1]
        act = _ffn_fwd(gp, up, conv_w[l], row(w["ffn_conv_b"][l]), S, name="ffn_mid")
        rides = [cores_step("B", l + 1)] if more else []
        z2, got = with_rides(_mm(act, full["ffn_w_down"][l], rides=rides, name="ffn_out"), rides)
        if more:
            finish_group("B", l + 1, got[0])
        x2 = _res_fwd(x1, z2, md[5], row(w["ln2_g"][l]), row(w["ln2_b"][l]), S, name="res_fwd")
        saved.append(dict(x=xt, h1=h1, p=p, qkv=qkv, qmb=qmb, kvm=kvm, bias_na=bias_na, sink=sink, mix=mix, z1=z1, x1=x1,
                          h2=h2, gp=gp, up=up, act=act, z2=z2, cfgs=(cfg_na, cfg_swa, cfg_mla, cfg_gqa)))
        xt = x2

    dx, loss_part = _loss_fwd_bwd(xt, loss_target[0], S, name="loss")
    loss = lax.psum(loss_part[0, 0], ("x", "y", "c"))

    groups = {"ffn": ("ffn_w_gate", "ffn_w_up", "ffn_w_down", "w_out"), "rest": ("w_in", "mla_w_uq", "mla_w_ukv")}
    wgrad = [dict() for _ in range(L)]
    parts, landed = {}, {}

    def halves_of(group, l):
        return [wgrad[l][n].reshape(4, 2, wgrad[l][n].shape[1] // 2, wgrad[l][n].shape[2]) for n in groups[group]]

    def add_halves(group, l, received):
        parts[(group, l)] = [_add_half(h, r, ac, name="rs_core_add") for h, r in zip(halves_of(group, l), received)]

    small = {n: [None] * L for n in ("na_rpb", "swa_sink", "mla_q_norm", "mla_kv_norm", "gqa_q_norm", "gqa_k_norm",
                                     "ln1_g", "ln1_b", "ffn_conv_w", "ffn_conv_b", "ln2_g", "ln2_b")}
    dmod = [None] * L
    for l in reversed(range(L)):
        sv, md = saved[l], mods[l]
        gq, gk, mq, mkv = row(w["gqa_q_norm"][l]), row(w["gqa_k_norm"][l]), row(w["mla_q_norm"][l]), row(w["mla_kv_norm"][l])
        cb_row = row(w["ffn_conv_b"][l])
        dx1, dz2, acc_r2 = _res_bwd(sv["x1"], sv["z2"], md[5], row(w["ln2_g"][l]), dx, S, name="res_bwd")
        dact = _mm(dz2, full["ffn_w_down"][l], mode="nt", out_dtype=BF16, name="ffn_out_dx")
        wgrad[l]["ffn_w_down"] = _mm(sv["act"], dz2, mode="tn", name="ffn_out_dw").reshape(4, F // 4, D)
        da, du, acc_f = _ffn_bwd_act(sv["gp"], sv["up"], dact, conv_w[l], cb_row, S, name="ffn_mid_bwd")
        dg = _ffn_bwd_conv(da, conv_w[l], S, name="ffn_conv_bwd")
        dh2 = _mm(dg, full["ffn_w_gate"][l], mode="nt", name="ffn_in_dx")
        dh2 = _mm(du, full["ffn_w_up"][l], mode="nt", add=dh2, name="ffn_in_dx_add")
        wgrad[l]["ffn_w_gate"] = _mm(sv["h2"], dg, mode="tn", stack=(1, 0, None), split4=True,
                                     name="ffn_in_dw").reshape(4, D, F // 4)
        wgrad[l]["ffn_w_up"] = _mm(sv["h2"], du, mode="tn", stack=(1, 0, None), split4=True,
                                   name="ffn_in_dw").reshape(4, D, F // 4)
        dx1, acc_m2 = _mod_bwd(sv["x1"], dh2, md[4], dx1, S, name="mod_bwd")
        dxa, dz1, acc_r1 = _res_bwd(sv["x"], sv["z1"], md[2], row(w["ln1_g"][l]), dx1, S, name="res_bwd")
        dmix = _mm(dz1, full["w_out"][l], mode="nt", out_dtype=BF16, name="out_proj_dx")
        wgrad[l]["w_out"] = _mm(sv["mix"], dz1, mode="tn", rows=dz1.shape[0], name="out_proj_dw").reshape(4, -1, D)

        qkv, qmb, kvm = sv["qkv"], sv["qmb"], sv["kvm"]
        cfg_na, cfg_swa, cfg_mla, cfg_gqa = sv["cfgs"]
        rest_above = l + 1 < L
        rides = [_Exchange(halves_of("ffn", l))] + ([_Exchange(halves_of("rest", l + 1))] if rest_above else [])
        dq_a, dk_a, dv_a, dbias, received = _attn_bwd(cfg_na, qkv, CB_NA_Q, qkv, CB_NA_K, qkv, CB_NA_V, dmix, 0,
                                                      bias=sv["bias_na"], want_dbias=True, rides=rides, name="na_bwd")
        add_halves("ffn", l, received[0])
        if rest_above:
            add_halves("rest", l + 1, received[1])
        dq_b, dk_b, dv_b, dsink = _attn_bwd(cfg_swa, qkv, CB_SWA_Q, qkv, CB_SWA_K, qkv, CB_SWA_V, dmix, NA_HEADS,
                                            bias=swa_mask, sink=sv["sink"], name="swa_bwd")
        dq_c, dk_c, dv_c, dq2_c, dk2_c, got = _attn_bwd(
            cfg_mla, qmb, 0, kvm, 0, kvm, MLA_HEADS, dmix, NA_HEADS + SWA_HEADS, q2=qmb, q2_cb=MLA_HEADS, k2=qkv,
            k2_cb=CB_KPE, rides=[_Scatter(parts[("ffn", l)])], name="mla_bwd")
        landed[("ffn", l)] = got[0]
        rides = [_Scatter(parts[("rest", l + 1)])] if rest_above else []
        gqa_out = _attn_bwd(cfg_gqa, qkv, CB_GQA_Q, qkv, CB_GQA_K, qkv, CB_GQA_V, dmix,
                            NA_HEADS + SWA_HEADS + MLA_HEADS, rides=rides, name="gqa_bwd")
        dq_d, dk_d, dv_d = gqa_out[:3]
        if rest_above:
            landed[("rest", l + 1)] = gqa_out[3][0]
        dqm = _pe_rope(jnp.concatenate([dq_c, dq2_c], axis=1), tabs, S, -1.0, BF16, name="mla_q_rope_bwd")
        dkvm = jnp.concatenate([dk_c, dv_c], axis=1).astype(BF16)
        dcq = _mm(dqm, w_uq_p[l], mode="nt", name="mla_uq_dx")
        dckv = _mm(dkvm, w_ukv_p[l], mode="nt", name="mla_ukv_dx")
        cqn = qkv[:, CB_CQ * LANE:CB_CKV * LANE]
        ckvn = qkv[:, CB_CKV * LANE:(CB_CKV + 1) * LANE]
        d_uq = _unperm_w_uq(_mm(cqn, dqm, mode="tn", name="mla_uq_dw"))
        d_ukv = _unperm_w_ukv(_mm(ckvn, dkvm, mode="tn", name="mla_ukv_dw"))
        grads = {}
        for h in range(NA_HEADS):
            grads[CB_NA_Q + h], grads[CB_NA_K + h], grads[CB_NA_V + h] = (dq_a, h), (dk_a, h), (dv_a, h)
        for h in range(SWA_HEADS):
            grads[CB_SWA_Q + h] = (dq_b, h)
        for h in range(SWA_KV_HEADS):
            grads[CB_SWA_K + h], grads[CB_SWA_V + h] = (dk_b, h), (dv_b, h)
        for h in range(GQA_HEADS):
            grads[CB_GQA_Q + h] = (dq_d, h)
        for h in range(GQA_KV_HEADS):
            grads[CB_GQA_K + h], grads[CB_GQA_V + h] = (dk_d, h), (dv_d, h)
        grads[CB_KPE], grads[CB_CQ], grads[CB_CKV] = (dk2_c, 0), (dcq, 0), (dckv, 0)
        dp, acc_p = _prep_bwd(sv["p"], grads, tabs, gq, gk, mq, mkv, S, name="prep_bwd")
        dh1 = _mm(dp, w_in_p[l], name="in_proj_dx")
        d_in = _unperm_w_in(_mm(dp, sv["h1"], mode="tn", name="in_proj_dw")).reshape(4, IN_COLS // 4, D)
        dx, acc_m1 = _mod_bwd(sv["x"], dh1, md[1], dxa, S, name="mod_bwd")

        to4 = lambda g: g.reshape(g.shape[0], 4, g.shape[1] // 4).transpose(1, 0, 2)
        wgrad[l]["w_in"], wgrad[l]["mla_w_uq"], wgrad[l]["mla_w_ukv"] = d_in, to4(d_uq), to4(d_ukv)
        dmod[l] = jnp.stack([acc_m1[0:2], acc_m1[2:4], acc_r1[0:2], acc_m2[0:2], acc_m2[2:4], acc_r2[0:2]])
        rpb_vjp = jax.vjp(lambda r: _na_bias(r, S), w["na_rpb"][l])[1]
        small["na_rpb"][l] = rpb_vjp(dbias)[0]
        small["swa_sink"][l] = dsink[:, :, 0].reshape(SWA_KV_HEADS, -1, BQ).sum(axis=-1).reshape(-1)
        small["gqa_q_norm"][l], small["gqa_k_norm"][l] = acc_p[0, :LANE], acc_p[1, :LANE]
        small["mla_q_norm"][l], small["mla_kv_norm"][l] = acc_p[2], acc_p[3, :LANE]
        small["ln1_g"][l], small["ln1_b"][l] = acc_r1[2], acc_r1[3]
        small["ln2_g"][l], small["ln2_b"][l] = acc_r2[2], acc_r2[3]
        small["ffn_conv_w"][l], small["ffn_conv_b"][l] = acc_f[0:3], acc_f[3]
    grad_x = dx[:S][None]

    dmod_x = jnp.stack([dmod[l][:, 0].reshape(-1) for l in range(L)])
    dmod_c = jnp.stack([dmod[l][:, 1].reshape(-1) for l in range(L)])
    small_names = tuple(small)
    bucket = [dmod_x, dmod_c] + [jnp.stack(small[n]) for n in small_names]
    (b8,) = _allgather8([_pack(bucket)], name="gather_small")
    tot = _unpack(_sum_lead(b8, F32, name="sum_small"), bucket)
    dmod_x_all = b8.reshape(8, -1)[:, :dmod_x.size].reshape(8, L, 6 * D)
    dmod_c_tot = tot[1]
    g_small = dict(zip(small_names, tot[2:]))
    g_small["b_ada"] = tot[0] + dmod_c_tot
    g_small["ffn_conv_w"] = lax.dynamic_slice_in_dim(g_small["ffn_conv_w"], chip * (F // 4), F // 4, axis=2)

    dmod16 = jnp.concatenate([dmod_x_all, jnp.broadcast_to(dmod_c_tot[None], (8, L, 6 * D))], axis=0) * row_keep[:, :, None]
    dmod16 = lax.dynamic_slice_in_dim(dmod16, chip * n_ada, n_ada, axis=2)
    g_ada, dsc = None, None
    for l in range(L):
        g_ada = _mm(sc, dmod16[:, l], mode="tn", exact=True, stack=(L, l, g_ada), name="ada_dw")
        dsc = _mm(dmod16[:, l], w["w_ada"], b_layer=l, mode="nt", add=dsc, name="ada_dx" if dsc is None else "ada_dx_add")
    (dsc8,) = _allgather8([dsc[8:16]], name="gather_dsc")
    dsc4 = dsc8[::2, 0]
    g_small["c_ctx"] = (((dsc4[0] + dsc4[1]) + dsc4[2]) + dsc4[3]) * _silu_grad(w["c_ctx"])

    sums = {}

    def sum_group(group, l):
        for n, p, got in zip(groups[group], parts[(group, l)], landed[(group, l)]):
            sums[n] = _sum_parts(p, got, chip, ac, (L, l, sums.get(n)), name="rs_chip_sum")

    add_halves("rest", 0, _run_rides([_Exchange(halves_of("rest", 0))], name="rs_core_exchange")[0])
    for l in range(L):
        sum_group("ffn", l)
        if l > 0:
            sum_group("rest", l)
    grad, delta, new_m, new_v = {}, {}, {}, {}
    (delta["w_ada"], new_m["w_ada"], new_v["w_ada"]), got = _adam(
        w["w_ada"], g_ada, m_in["w_ada"], v_in["w_ada"],
        rides=[_Scatter(parts[("rest", 0)]), _Join([sums[n] for n in groups["ffn"]])], name="adam")
    landed[("rest", 0)] = got[0]
    joined = dict(zip(groups["ffn"], got[1]))
    sum_group("rest", 0)
    joined.update(zip(groups["rest"], _run_rides([_Join([sums[n] for n in groups["rest"]])], name="rs_join")[0]))
    g_big = {n: j.reshape(L, 2 * j.shape[2], j.shape[3]) for n, j in joined.items()}
    g_big["w_ada"] = g_ada

    grad["w_ada"] = g_ada
    for n in _BIG:
        grad[n] = g_big[n]
        delta[n], new_m[n], new_v[n] = _adam(w[n], g_big[n], m_in[n], v_in[n], name="adam")
    like = [w[n] for n in _SMALL]
    packed = [_pack([src[n].reshape(w[n].shape) for n in _SMALL]) for src in (w, g_small, m_in, v_in)]
    d_s, m_s, v_s = _adam(*packed, name="adam_small")
    for n, g_, d_, m_, v_ in zip(_SMALL, _unpack(packed[1], like), _unpack(d_s, like), _unpack(m_s, like), _unpack(v_s, like)):
        grad[n], delta[n], new_m[n], new_v[n] = g_, d_, m_, v_

    for d in (grad, delta, new_m, new_v):
        d["w_in"] = jnp.swapaxes(d["w_in"], 1, 2)
    return (loss, grad_x, *[grad[n] for n in _WEIGHTS], *[delta[n] for n in _WEIGHTS],
            *[new_m[n] for n in _WEIGHTS], *[new_v[n] for n in _WEIGHTS])


def kernel(x, c, ctx, c_ctx, w_ada, b_ada, w_in, na_rpb, swa_sink, mla_q_norm, mla_kv_norm, mla_w_uq, mla_w_ukv, gqa_q_norm, gqa_k_norm, w_out, ln1_g, ln1_b, ffn_w_gate, ffn_w_up, ffn_conv_w, ffn_conv_b, ffn_w_down, ln2_g, ln2_b, loss_target, m_c_ctx, m_w_ada, m_b_ada, m_w_in, m_na_rpb, m_swa_sink, m_mla_q_norm, m_mla_kv_norm, m_mla_w_uq, m_mla_w_ukv, m_gqa_q_norm, m_gqa_k_norm, m_w_out, m_ln1_g, m_ln1_b, m_ffn_w_gate, m_ffn_w_up, m_ffn_conv_w, m_ffn_conv_b, m_ffn_w_down, m_ln2_g, m_ln2_b, v_c_ctx, v_w_ada, v_b_ada, v_w_in, v_na_rpb, v_swa_sink, v_mla_q_norm, v_mla_kv_norm, v_mla_w_uq, v_mla_w_ukv, v_gqa_q_norm, v_gqa_k_norm, v_w_out, v_ln1_g, v_ln1_b, v_ffn_w_gate, v_ffn_w_up, v_ffn_conv_w, v_ffn_conv_b, v_ffn_w_down, v_ln2_g, v_ln2_b):
    args = locals()
    w = {n: args[n] for n in _WEIGHTS}
    m_in = {n: args["m_" + n] for n in _WEIGHTS}
    v_in = {n: args["v_" + n] for n in _WEIGHTS}
    return _train_step(x, c, ctx, loss_target, w, m_in, v_in)
```

```python
import functools
import math

import numpy as np
import jax
import jax.numpy as jnp
from jax import lax
from jax.experimental import pallas as pl
from jax.experimental.pallas import tpu as pltpu

F32 = jnp.float32
BF16 = jnp.bfloat16
MESH = pl.DeviceIdType.MESH

GRID_W = 64
HEAD_DIM = 128
NA_HEADS, NA_WIN_R, NA_WIN_C = 4, 8, 16
SWA_HEADS, SWA_KV_HEADS, SWA_WINDOW = 4, 2, 128
MLA_HEADS, MLA_Q_LORA, MLA_KV_LORA, MLA_NOPE, MLA_ROPE, MLA_V = 4, 384, 128, 128, 64, 128
GQA_HEADS, GQA_KV_HEADS = 4, 2
ROPE_THETA = 10000.0
EPS = 1e-6
NEG = -1e30
DEPTH = 2
DEEPNORM_ALPHA = (2 * DEPTH) ** 0.25
ADAM_LR, ADAM_B1, ADAM_B2, ADAM_EPS, ADAM_WD, ADAM_STEP = 0.001, 0.9, 0.999, 1e-08, 0.01, 10

LANE = 128
V7X_VMEM_BYTES = 64 * 1024 * 1024
VMEM_LIMIT = 56 * 1024 * 1024
MM_VMEM_BUDGET = 40 * 1024 * 1024
EW_VMEM_BUDGET = 28 * 1024 * 1024
BQ = 128

CB_NA_Q, CB_NA_K, CB_NA_V = 0, 4, 8
CB_SWA_Q, CB_SWA_K, CB_SWA_V = 12, 16, 18
CB_CQ, CB_CKV = 20, 23
CB_GQA_Q, CB_GQA_K, CB_GQA_V = 24, 28, 30
CB_KPE = 32
PCOLS = 33 * LANE
IN_COLS = 4160


def _cparams(sem=None, **kw):
    return pltpu.CompilerParams(dimension_semantics=sem, vmem_limit_bytes=VMEM_LIMIT, **kw)


def _pick(n, target, mult=LANE):
    best = None
    for d in range(mult, min(n, target) + 1, mult):
        if n % d == 0:
            best = d
    return n if best is None else best


def _mm(a, b, *, mode="nn", out_dtype=F32, a_off=0, a_k=None, tm=1408, tn=1408, tk=2816, exact=False, add=None,
        stack=None, split4=False, rows=None, rides=(), b_layer=None, name):
    b_shape = b.shape if b_layer is None else b.shape[1:]
    if mode == "tn":
        K, M = a.shape
        K2, N = b_shape
    elif mode == "nn":
        M, K = a.shape
        K2, N = b_shape
    else:
        M, K = a.shape
        N, K2 = b_shape
    if a_k is not None:
        K = a_k
    if rows is not None:
        if mode == "tn":
            assert rows <= min(K, K2)
            K = K2 = rows
        else:
            assert rows <= M
            M = rows
    assert K == K2, (a.shape, b.shape, mode)
    m_mult = LANE if mode == "tn" else 16
    n_cols = N // 4 if split4 else N
    bm, bn, bk = _pick(M, tm, m_mult), _pick(n_cols, tn), _pick(K, tk)
    sa, sb, so = a.dtype.itemsize, b.dtype.itemsize, jnp.dtype(out_dtype).itemsize

    def vmem_estimate():
        acc = bm * bn * 4 if K // bk > 1 else 0
        return 2 * (bm * bk * sa + bk * bn * sb) + acc + 2 * bm * bn * so + (2 * bm * bn * 4 if add is not None else 0)

    while vmem_estimate() > MM_VMEM_BUDGET:
        if bm >= bn and _pick(M, bm - 1, m_mult) < bm:
            bm = _pick(M, bm - 1, m_mult)
        elif _pick(n_cols, bn - 1) < bn:
            bn = _pick(n_cols, bn - 1)
        else:
            assert _pick(K, bk - 1) < bk, "no tiling fits VMEM"
            bk = _pick(K, bk - 1)
    assert a_off % bk == 0
    koff = a_off // bk
    nk = K // bk
    if mode == "tn":
        a_spec = pl.BlockSpec((bk, bm), lambda i, j, k: (k, i))
        b_spec = pl.BlockSpec((bk, bn), lambda i, j, k: (k, j))
        dims = (((0,), (0,)), ((), ()))
    elif mode == "nn":
        a_spec = pl.BlockSpec((bm, bk), lambda i, j, k: (i, k + koff))
        b_spec = pl.BlockSpec((bk, bn), lambda i, j, k: (k, j))
        dims = (((1,), (0,)), ((), ()))
    else:
        a_spec = pl.BlockSpec((bm, bk), lambda i, j, k: (i, k + koff))
        b_spec = pl.BlockSpec((bn, bk), lambda i, j, k: (j, k))
        dims = (((1,), (1,)), ((), ()))
    if b_layer is not None:
        b_block, b_index = b_spec.block_shape, b_spec.index_map
        b_spec = pl.BlockSpec((None,) + tuple(b_block), lambda i, j, k: (b_layer,) + tuple(b_index(i, j, k)))

    operands = [a, b]
    in_specs = [a_spec, b_spec]
    if add is not None:
        operands.append(add)
        in_specs.append(pl.BlockSpec((bm, bn), lambda i, j, k: (i, j)))
    aliases = {}
    if stack is None:
        out_spec = pl.BlockSpec((bm, bn), lambda i, j, k: (i, j))
        out_shape = jax.ShapeDtypeStruct((M, N), out_dtype)
    else:
        n_layers, layer, buf = stack
        if split4:
            nb = N // 4 // bn
            assert N % (4 * bn) == 0
            out_spec = pl.BlockSpec((None, None, bm, bn), lambda i, j, k: (layer, j // nb, i, j % nb))
            out_shape = jax.ShapeDtypeStruct((n_layers, 4, M, N // 4), out_dtype)
        else:
            out_spec = pl.BlockSpec((None, bm, bn), lambda i, j, k: (layer, i, j))
            out_shape = jax.ShapeDtypeStruct((n_layers, M, N), out_dtype)
        if buf is not None:
            aliases = {len(operands): 0}
            operands.append(buf)
            in_specs.append(pl.BlockSpec(memory_space=pl.ANY))
    has_add, has_buf = add is not None, bool(aliases)
    n_ride = sum(len(r.arrays) for r in rides)
    aliases.update(_ride_aliases(rides, len(operands), 1))
    grid = (M // bm, N // bn, nk)

    def body(*refs):
        a_ref, b_ref = refs[:2]
        add_ref = refs[2] if has_add else None
        base = 2 + has_add + has_buf
        o_ref = refs[base + n_ride]
        scratch = refs[base + 2 * n_ride + 1:]
        if rides:
            ride_groups = _ride_split(rides, refs[base:base + n_ride], refs[base + n_ride + 1:base + 2 * n_ride + 1],
                                      scratch[1 if nk > 1 else 0:])
            steps = [pl.program_id(d) for d in range(3)]
            pl.when((steps[0] == 0) & (steps[1] == 0) & (steps[2] == 0))(lambda: _ride_start(rides, ride_groups))
        compute(a_ref, b_ref, add_ref, o_ref, scratch[0] if nk > 1 else None)
        if rides:
            pl.when((steps[0] == grid[0] - 1) & (steps[1] == grid[1] - 1) & (steps[2] == grid[2] - 1))(
                lambda: _ride_wait(rides, ride_groups))

    def compute(a_ref, b_ref, add_ref, o_ref, acc_ref):
        if exact:
            prod = lax.dot_general(a_ref[...].astype(F32), b_ref[...].astype(F32), dims,
                                   precision=lax.Precision.HIGHEST, preferred_element_type=F32)
        else:
            prod = lax.dot_general(a_ref[...].astype(BF16), b_ref[...].astype(BF16), dims, preferred_element_type=F32)

        def finish(res):
            if has_add:
                res = res + add_ref[...].astype(F32)
            o_ref[...] = res.astype(o_ref.dtype)

        if nk == 1:
            finish(prod)
            return
        k = pl.program_id(2)

        @pl.when(k == 0)
        def _():
            acc_ref[...] = prod

        @pl.when((k > 0) & (k < nk - 1))
        def _():
            acc_ref[...] += prod

        @pl.when(k == nk - 1)
        def _():
            finish(acc_ref[...] + prod)

    outs = pl.pallas_call(
        body, name=name, grid=grid,
        in_specs=in_specs + [_ANY] * n_ride, out_specs=[out_spec] + [_ANY] * n_ride,
        out_shape=[out_shape] + [s for r in rides for s in r.out_shapes()],
        scratch_shapes=([pltpu.VMEM((bm, bn), F32)] if nk > 1 else []) + _ride_scratch(rides),
        input_output_aliases=aliases,
        compiler_params=_cparams(("arbitrary",) * 3 if rides else ("parallel", "parallel", "arbitrary")),
    )(*operands, *[a for r in rides for a in r.arrays])
    if not rides:
        return outs[0]
    return outs[0], _ride_outputs(rides, outs[1:])


def _row_block(T, S):
    return _pick(math.gcd(T, S), 256, 16)


def _ln_stats(x):
    mu = jnp.mean(x, axis=-1, keepdims=True)
    xc = x - mu
    var = jnp.mean(xc * xc, axis=-1, keepdims=True)
    rstd = lax.rsqrt(var + EPS)
    return xc * rstd, rstd


def _ln_bwd(dxhat, xhat, rstd):
    m1 = jnp.mean(dxhat, axis=-1, keepdims=True)
    m2 = jnp.mean(dxhat * xhat, axis=-1, keepdims=True)
    return rstd * (dxhat - m1 - xhat * m2)


def _sel(ref, is_ctx):
    return jnp.where(is_ctx, ref[1:2, :], ref[0:1, :])


def _mod_fwd(x, shift, scale, S, *, rows=None, name):
    T, D = (x.shape[0] if rows is None else rows), x.shape[1]
    bt = _row_block(T, S)

    def body(x_ref, sh_ref, sc_ref, o_ref):
        is_ctx = pl.program_id(0) * bt >= S
        xhat, _ = _ln_stats(x_ref[...])
        o_ref[...] = (xhat * (1.0 + _sel(sc_ref, is_ctx)) + _sel(sh_ref, is_ctx)).astype(o_ref.dtype)

    return pl.pallas_call(
        body, name=name, grid=(T // bt,),
        in_specs=[pl.BlockSpec((bt, D), lambda i: (i, 0)), pl.BlockSpec((2, D), lambda i: (0, 0)),
                  pl.BlockSpec((2, D), lambda i: (0, 0))],
        out_specs=pl.BlockSpec((bt, D), lambda i: (i, 0)),
        out_shape=jax.ShapeDtypeStruct((T, D), BF16),
        compiler_params=_cparams(("parallel",)),
    )(x, shift, scale)


def _acc_groups(acc_ref, row, val, is_ctx):
    f = jnp.where(is_ctx, 1.0, 0.0).astype(F32)
    acc_ref[row:row + 1, :] += val * (1.0 - f)
    acc_ref[row + 1:row + 2, :] += val * f


def _mod_bwd(x, dh, scale, dx_in, S, *, name):
    T, D = dh.shape
    bt = _row_block(T, S)
    in_blocks = dx_in.shape[0] // bt

    def body(x_ref, dh_ref, sc_ref, dxi_ref, dx_ref, acc_ref):
        i = pl.program_id(0)
        is_ctx = i * bt >= S

        @pl.when(i == 0)
        def _():
            acc_ref[...] = jnp.zeros_like(acc_ref)

        xhat, rstd = _ln_stats(x_ref[...])
        dh = dh_ref[...].astype(F32)
        dxhat = dh * (1.0 + _sel(sc_ref, is_ctx))
        dxi = dxi_ref[...] if in_blocks * bt == T else jnp.where(i < in_blocks, dxi_ref[...], 0.0)
        dx_ref[...] = dxi + _ln_bwd(dxhat, xhat, rstd)
        _acc_groups(acc_ref, 0, jnp.sum(dh, axis=0, keepdims=True), is_ctx)
        _acc_groups(acc_ref, 2, jnp.sum(dh * xhat, axis=0, keepdims=True), is_ctx)

    return pl.pallas_call(
        body, name=name, grid=(T // bt,),
        in_specs=[pl.BlockSpec((bt, D), lambda i: (i, 0)), pl.BlockSpec((bt, D), lambda i: (i, 0)),
                  pl.BlockSpec((2, D), lambda i: (0, 0)),
                  pl.BlockSpec((bt, D), lambda i: (jnp.minimum(i, in_blocks - 1), 0))],
        out_specs=[pl.BlockSpec((bt, D), lambda i: (i, 0)), pl.BlockSpec((8, D), lambda i: (0, 0))],
        out_shape=[jax.ShapeDtypeStruct((T, D), F32), jax.ShapeDtypeStruct((8, D), F32)],
        compiler_params=_cparams(("arbitrary",)),
    )(x, dh, scale, dx_in)


def _res_fwd(x, z, gate, lg, lb, S, *, name):
    T, D = z.shape
    bt = _row_block(T, S)

    def body(x_ref, z_ref, g_ref, lg_ref, lb_ref, o_ref):
        is_ctx = pl.program_id(0) * bt >= S
        u = DEEPNORM_ALPHA * x_ref[...] + _sel(g_ref, is_ctx) * z_ref[...]
        uhat, _ = _ln_stats(u)
        o_ref[...] = uhat * lg_ref[...] + lb_ref[...]

    row = pl.BlockSpec((bt, D), lambda i: (i, 0))
    return pl.pallas_call(
        body, name=name, grid=(T // bt,),
        in_specs=[row, row, pl.BlockSpec((2, D), lambda i: (0, 0)), pl.BlockSpec((1, D), lambda i: (0, 0)),
                  pl.BlockSpec((1, D), lambda i: (0, 0))],
        out_specs=row,
        out_shape=jax.ShapeDtypeStruct((T, D), F32),
        compiler_params=_cparams(("parallel",)),
    )(x, z, gate, lg, lb)


def _res_bwd(x, z, gate, lg, dy, S, *, name):
    T, D = z.shape
    bt = _row_block(T, S)

    def body(x_ref, z_ref, g_ref, lg_ref, dy_ref, dx_ref, dz_ref, acc_ref):
        i = pl.program_id(0)
        is_ctx = i * bt >= S

        @pl.when(i == 0)
        def _():
            acc_ref[...] = jnp.zeros_like(acc_ref)

        gate_v = _sel(g_ref, is_ctx)
        zv = z_ref[...]
        u = DEEPNORM_ALPHA * x_ref[...] + gate_v * zv
        uhat, rstd = _ln_stats(u)
        dyv = dy_ref[...]
        du = _ln_bwd(dyv * lg_ref[...], uhat, rstd)
        dx_ref[...] = DEEPNORM_ALPHA * du
        dz_ref[...] = (gate_v * du).astype(dz_ref.dtype)
        _acc_groups(acc_ref, 0, jnp.sum(du * zv, axis=0, keepdims=True), is_ctx)
        acc_ref[2:3, :] += jnp.sum(dyv * uhat, axis=0, keepdims=True)
        acc_ref[3:4, :] += jnp.sum(dyv, axis=0, keepdims=True)

    row = pl.BlockSpec((bt, D), lambda i: (i, 0))
    return pl.pallas_call(
        body, name=name, grid=(T // bt,),
        in_specs=[row, row, pl.BlockSpec((2, D), lambda i: (0, 0)), pl.BlockSpec((1, D), lambda i: (0, 0)), row],
        out_specs=[row, row, pl.BlockSpec((8, D), lambda i: (0, 0))],
        out_shape=[jax.ShapeDtypeStruct((T, D), F32), jax.ShapeDtypeStruct((T, D), BF16),
                   jax.ShapeDtypeStruct((8, D), F32)],
        compiler_params=_cparams(("arbitrary",)),
    )(x, z, gate, lg, dy)


def _loss_fwd_bwd(y, target, S, *, name):
    T, D = y.shape
    bt = _row_block(T, S)
    n_lat = S // bt

    def body(y_ref, t_ref, dy_ref, l_ref):
        i = pl.program_id(0)

        @pl.when(i == 0)
        def _():
            l_ref[...] = jnp.zeros_like(l_ref)

        keep = jnp.where(i * bt >= S, 0.0, 1.0).astype(F32)
        err = (y_ref[...] - t_ref[...]) * keep
        dy_ref[...] = err * (1.0 / D)
        l_ref[...] += jnp.sum(err * err) * (0.5 / D)

    return pl.pallas_call(
        body, name=name, grid=(T // bt,),
        in_specs=[pl.BlockSpec((bt, D), lambda i: (i, 0)),
                  pl.BlockSpec((bt, D), lambda i: (jnp.minimum(i, n_lat - 1), 0))],
        out_specs=[pl.BlockSpec((bt, D), lambda i: (i, 0)), pl.BlockSpec((8, LANE), lambda i: (0, 0))],
        out_shape=[jax.ShapeDtypeStruct((T, D), F32), jax.ShapeDtypeStruct((8, LANE), F32)],
        compiler_params=_cparams(("arbitrary",)),
    )(y, target)


def _rope_tables(S, C, dim):
    half = dim // 4
    t = jnp.arange(S)
    row = (t // GRID_W).astype(F32)
    col = (t % GRID_W).astype(F32)
    inv = ROPE_THETA ** (-jnp.arange(half, dtype=F32) / half)
    ar, ac = row[:, None] * inv[None, :], col[:, None] * inv[None, :]
    cos = jnp.concatenate([jnp.cos(ar), jnp.cos(ar), jnp.cos(ac), jnp.cos(ac)], axis=1)
    ss = jnp.concatenate([-jnp.sin(ar), jnp.sin(ar), -jnp.sin(ac), jnp.sin(ac)], axis=1)
    cos = jnp.pad(cos, ((0, C), (0, LANE - dim)), constant_values=1.0)
    ss = jnp.pad(ss, ((0, C), (0, LANE - dim)))
    return cos, ss


def _rope(x, cos, ss, half):
    lane = lax.broadcasted_iota(jnp.int32, x.shape, 1)
    first = (lane % (2 * half)) < half
    partner = jnp.where(first, pltpu.roll(x, LANE - half, 1), pltpu.roll(x, half, 1))
    return x * cos + partner * ss


def _rms(x):
    r = lax.rsqrt(jnp.mean(x * x, axis=-1, keepdims=True) + EPS)
    return x * r, r


_CAST_BLOCKS = tuple(range(0, 12)) + (18, 19, 30, 31)
_ROPE_BLOCKS = tuple(range(12, 18))
_GQA_Q_BLOCKS = tuple(range(24, 28))
_GQA_K_BLOCKS = (28, 29)


def _prep_fwd(p, tabs, gq, gk, mq, mkv, S, *, name):
    T = p.shape[0]
    bt = _row_block(T, S)
    cA, sA, cP, sP = tabs

    def body(p_ref, cA_ref, sA_ref, cP_ref, sP_ref, gq_ref, gk_ref, mq_ref, mkv_ref, o_ref):
        def blk(b):
            return p_ref[:, b * LANE:(b + 1) * LANE].astype(F32)

        def put(b, val):
            o_ref[:, b * LANE:(b + 1) * LANE] = val.astype(o_ref.dtype)

        cA_v, sA_v = cA_ref[...], sA_ref[...]
        for b in _CAST_BLOCKS:
            put(b, blk(b))
        for b in _ROPE_BLOCKS:
            put(b, _rope(blk(b), cA_v, sA_v, 32))
        for b in _GQA_Q_BLOCKS:
            put(b, _rope(_rms(blk(b))[0] * gq_ref[...], cA_v, sA_v, 32))
        for b in _GQA_K_BLOCKS:
            put(b, _rope(_rms(blk(b))[0] * gk_ref[...], cA_v, sA_v, 32))
        put(CB_KPE, _rope(blk(CB_KPE), cP_ref[...], sP_ref[...], 16))
        cq = p_ref[:, CB_CQ * LANE:CB_CKV * LANE].astype(F32)
        o_ref[:, CB_CQ * LANE:CB_CKV * LANE] = (_rms(cq)[0] * mq_ref[...]).astype(o_ref.dtype)
        put(CB_CKV, _rms(blk(CB_CKV))[0] * mkv_ref[...])

    row128 = pl.BlockSpec((bt, LANE), lambda i: (i, 0))
    vec = lambda n: pl.BlockSpec((1, n), lambda i: (0, 0))
    return pl.pallas_call(
        body, name=name, grid=(T // bt,),
        in_specs=[pl.BlockSpec((bt, PCOLS), lambda i: (i, 0)), row128, row128, row128, row128,
                  vec(LANE), vec(LANE), vec(MLA_Q_LORA), vec(LANE)],
        out_specs=pl.BlockSpec((bt, PCOLS), lambda i: (i, 0)),
        out_shape=jax.ShapeDtypeStruct((T, PCOLS), BF16),
        compiler_params=_cparams(("parallel",)),
    )(p, cA, sA, cP, sP, gq, gk, mq, mkv)


def _prep_bwd(p, grads, tabs, gq, gk, mq, mkv, S, *, name):
    T = p.shape[0]
    bt = _row_block(T, S)
    cA, sA, cP, sP = tabs
    arrays = []
    where = {}
    for key, (arr, cb) in grads.items():
        idx = next((n for n, a in enumerate(arrays) if a is arr), None)
        if idx is None:
            arrays.append(arr)
            idx = len(arrays) - 1
        where[key] = (idx, cb)
    ng = len(arrays)

    def body(*refs):
        p_ref, cA_ref, sA_ref, cP_ref, sP_ref, gq_ref, gk_ref, mq_ref, mkv_ref = refs[:9]
        g_refs = refs[9:9 + ng]
        o_ref, acc_ref = refs[9 + ng:]
        i = pl.program_id(0)

        @pl.when(i == 0)
        def _():
            acc_ref[...] = jnp.zeros_like(acc_ref)

        def blk(b):
            return p_ref[:, b * LANE:(b + 1) * LANE].astype(F32)

        def grad(b, width=LANE):
            idx, cb = where[b]
            return g_refs[idx][:, cb * LANE:cb * LANE + width].astype(F32)

        def put(b, val):
            o_ref[:, b * LANE:(b + 1) * LANE] = val.astype(o_ref.dtype)

        def rms_bwd(x, dy, g, row, width):
            n, r = _rms(x)
            acc_ref[row:row + 1, 0:width] += jnp.sum(dy * n, axis=0, keepdims=True)
            dn = dy * g
            return r * (dn - n * jnp.mean(dn * n, axis=-1, keepdims=True))

        cA_v, sA_v = cA_ref[...], sA_ref[...]
        for b in _CAST_BLOCKS:
            put(b, grad(b))
        for b in _ROPE_BLOCKS:
            put(b, _rope(grad(b), cA_v, -sA_v, 32))
        for b in _GQA_Q_BLOCKS:
            put(b, rms_bwd(blk(b), _rope(grad(b), cA_v, -sA_v, 32), gq_ref[...], 0, LANE))
        for b in _GQA_K_BLOCKS:
            put(b, rms_bwd(blk(b), _rope(grad(b), cA_v, -sA_v, 32), gk_ref[...], 1, LANE))
        put(CB_KPE, _rope(grad(CB_KPE), cP_ref[...], -sP_ref[...], 16))
        dcq = rms_bwd(p_ref[:, CB_CQ * LANE:CB_CKV * LANE].astype(F32), grad(CB_CQ, MLA_Q_LORA), mq_ref[...], 2, MLA_Q_LORA)
        o_ref[:, CB_CQ * LANE:CB_CKV * LANE] = dcq.astype(o_ref.dtype)
        put(CB_CKV, rms_bwd(blk(CB_CKV), grad(CB_CKV), mkv_ref[...], 3, LANE))

    row128 = pl.BlockSpec((bt, LANE), lambda i: (i, 0))
    vec = lambda n: pl.BlockSpec((1, n), lambda i: (0, 0))
    g_specs = [pl.BlockSpec((bt, a.shape[1]), lambda i: (i, 0)) for a in arrays]
    return pl.pallas_call(
        body, name=name, grid=(T // bt,),
        in_specs=[pl.BlockSpec((bt, PCOLS), lambda i: (i, 0)), row128, row128, row128, row128,
                  vec(LANE), vec(LANE), vec(MLA_Q_LORA), vec(LANE)] + g_specs,
        out_specs=[pl.BlockSpec((bt, PCOLS), lambda i: (i, 0)), pl.BlockSpec((8, MLA_Q_LORA), lambda i: (0, 0))],
        out_shape=[jax.ShapeDtypeStruct((T, PCOLS), BF16), jax.ShapeDtypeStruct((8, MLA_Q_LORA), F32)],
        compiler_params=_cparams(("arbitrary",)),
    )(p, cA, sA, cP, sP, gq, gk, mq, mkv, *arrays)


def _pe_rope(qm, tabs, S, sign, out_dtype, *, name):
    T, N = qm.shape
    bt = _row_block(T, S)
    cP, sP = tabs[2], tabs[3]

    def body(x_ref, c_ref, s_ref, o_ref):
        for b in range(MLA_HEADS):
            o_ref[:, b * LANE:(b + 1) * LANE] = x_ref[:, b * LANE:(b + 1) * LANE].astype(o_ref.dtype)
        for b in range(MLA_HEADS, 2 * MLA_HEADS):
            x = x_ref[:, b * LANE:(b + 1) * LANE].astype(F32)
            o_ref[:, b * LANE:(b + 1) * LANE] = _rope(x, c_ref[...], sign * s_ref[...], 16).astype(o_ref.dtype)

    row128 = pl.BlockSpec((bt, LANE), lambda i: (i, 0))
    return pl.pallas_call(
        body, name=name, grid=(T // bt,),
        in_specs=[pl.BlockSpec((bt, N), lambda i: (i, 0)), row128, row128],
        out_specs=pl.BlockSpec((bt, N), lambda i: (i, 0)),
        out_shape=jax.ShapeDtypeStruct((T, N), out_dtype),
        compiler_params=_cparams(("parallel",)),
    )(qm, cP, sP)


def _dot_nt(a, b):
    return lax.dot_general(a, b, (((1,), (1,)), ((), ())), preferred_element_type=F32)


def _dot_tn(a, b):
    return lax.dot_general(a, b, (((0,), (0,)), ((), ())), preferred_element_type=F32)


def _dot(a, b):
    return jnp.dot(a, b, preferred_element_type=F32)


def _window_fns(band, S, n_var):
    n_lat = S // BQ
    if band is None:
        return None
    reach, span = band

    def fns(j):
        start = jnp.clip(j - reach, 0, n_lat - span)
        return start, jnp.clip(j - start, 0, n_var - 1)

    return fns


class _AttnCfg:
    def __init__(self, *, Hkv, G, S, C, band, scale, n_var=0, bias_per_head=False, has_sink=False, two=False, bq=BQ,
                 ctx_queries=True):
        self.Hkv, self.G, self.S, self.C, self.band, self.scale = Hkv, G, S, C, band, scale
        self.n_var, self.bias_per_head, self.has_sink, self.two = n_var, bias_per_head, has_sink, two
        self.W = S if band is None else band[1] * BQ
        self.T = S + C
        self.bq, self.ctx_queries = bq, ctx_queries
        assert band is None or bq == BQ
        assert S % bq == 0 and C % bq == 0


def _attn_probs(cfg, j, q_ref, k_ref, q2_ref, k2_ref, bias_ref, sink_ref):
    G, S, C, W = cfg.G, cfg.S, cfg.C, cfg.W
    is_ctx = j * cfg.bq >= S
    if cfg.band is None:
        off, var = 0, 0
    else:
        start, var = _window_fns(cfg.band, S, cfg.n_var)(j)
        off = pl.multiple_of(start * BQ, BQ)
    qt = q_ref[...]
    qs = jnp.concatenate([qt[:, g * LANE:(g + 1) * LANE] for g in range(G)], axis=0) if G > 1 else qt
    kw = k_ref[pl.ds(off, W), :]
    kc = k_ref[pl.ds(S, C), :]
    s_w = _dot_nt(qs, kw)
    s_c = _dot_nt(qs, kc)
    q2s = k2w = k2c = None
    if cfg.two:
        q2s = q2_ref[...]
        k2w = k2_ref[pl.ds(off, W), :]
        k2c = k2_ref[pl.ds(S, C), :]
        s_w = s_w + _dot_nt(q2s, k2w)
        s_c = s_c + _dot_nt(q2s, k2c)
    operands = (off, var, qs, kw, kc, q2s, k2w, k2c)
    if not cfg.n_var and not cfg.has_sink:
        if cfg.ctx_queries:
            s_w = jnp.where(is_ctx, NEG, s_w)
        m = jnp.maximum(jnp.max(s_w, axis=-1, keepdims=True), jnp.max(s_c, axis=-1, keepdims=True))
        c2 = cfg.scale * math.log2(math.e)
        e_w = jnp.exp2((s_w - m) * c2)
        e_c = jnp.exp2((s_c - m) * c2)
        inv = 1.0 / (jnp.sum(e_w, axis=-1, keepdims=True) + jnp.sum(e_c, axis=-1, keepdims=True))
        return e_w * inv, e_c * inv, None, operands
    s_w = s_w * cfg.scale
    s_c = s_c * cfg.scale
    if cfg.n_var:
        b = bias_ref[0, pl.ds(var, 1)][0]
        s_w = s_w + (jnp.concatenate([b] * G, axis=0) if G > 1 else b)
    if cfg.ctx_queries:
        s_w = jnp.where(is_ctx, NEG, s_w)
    m = jnp.maximum(jnp.max(s_w, axis=-1, keepdims=True), jnp.max(s_c, axis=-1, keepdims=True))
    if cfg.has_sink:
        sink = sink_ref[0][:, 0:1]
        m = jnp.maximum(m, sink)
    e_w = jnp.exp(s_w - m)
    e_c = jnp.exp(s_c - m)
    l = jnp.sum(e_w, axis=-1, keepdims=True) + jnp.sum(e_c, axis=-1, keepdims=True)
    p_s = None
    if cfg.has_sink:
        e_s = jnp.exp(sink - m)
        l = l + e_s
    inv = 1.0 / l
    if cfg.has_sink:
        p_s = e_s * inv
    return e_w * inv, e_c * inv, p_s, operands


def _attn_specs(cfg, q_cb, k_cb, v_cb, q2_cb, k2_cb):
    G, T, bq = cfg.G, cfg.T, cfg.bq
    specs = [pl.BlockSpec((bq, G * LANE), lambda h, j: (j, q_cb // G + h)),
             pl.BlockSpec((T, LANE), lambda h, j: (0, k_cb + h)),
             pl.BlockSpec((T, LANE), lambda h, j: (0, v_cb + h))]
    if cfg.two:
        specs += [pl.BlockSpec((bq, LANE), lambda h, j: (j, q2_cb + h)),
                  pl.BlockSpec((T, LANE), lambda h, j: (0, k2_cb))]
    if cfg.n_var:
        if cfg.bias_per_head:
            specs.append(pl.BlockSpec((1, cfg.n_var, BQ, cfg.W), lambda h, j: (h, 0, 0, 0)))
        else:
            specs.append(pl.BlockSpec((1, cfg.n_var, BQ, cfg.W), lambda h, j: (0, 0, 0, 0)))
    if cfg.has_sink:
        specs.append(pl.BlockSpec((1, G * BQ, LANE), lambda h, j: (h, 0, 0)))
    return specs


def _attn_unpack(cfg, refs):
    refs = list(refs)
    q_ref, k_ref, v_ref = refs[:3]
    n = 3
    q2_ref = k2_ref = bias_ref = sink_ref = None
    if cfg.two:
        q2_ref, k2_ref = refs[n:n + 2]
        n += 2
    if cfg.n_var:
        bias_ref = refs[n]
        n += 1
    if cfg.has_sink:
        sink_ref = refs[n]
        n += 1
    return (q_ref, k_ref, v_ref, q2_ref, k2_ref, bias_ref, sink_ref), refs[n:]


def _attn_fwd(cfg, q, q_cb, k, k_cb, v, v_cb, *, q2=None, q2_cb=0, k2=None, k2_cb=0, bias=None, sink=None, rides=(),
              into=None, name):
    G, T, S, C, W = cfg.G, cfg.T, cfg.S, cfg.C, cfg.W
    assert q_cb % G == 0
    operands = [q, k, v] + ([q2, k2] if cfg.two else []) + ([bias] if cfg.n_var else []) + ([sink] if cfg.has_sink else [])

    bq = cfg.bq
    n_ride = sum(len(r.arrays) for r in rides)
    n_q = T // bq
    has_buf = into is not None and into[0] is not None
    col0 = 0 if into is None else into[1]
    width = cfg.Hkv * G * LANE if into is None else into[2] * LANE
    assert col0 % G == 0

    def body(*refs):
        (q_ref, k_ref, v_ref, q2_ref, k2_ref, bias_ref, sink_ref), rest = _attn_unpack(cfg, refs)
        rest = rest[:n_ride] + rest[n_ride + has_buf:]
        o_ref = rest[n_ride]
        ride_groups = _ride_split(rides, rest[:n_ride], rest[n_ride + 1:2 * n_ride + 1], rest[2 * n_ride + 1:])
        h = pl.program_id(0)
        j = pl.program_id(1)
        if rides:
            pl.when((h == 0) & (j == 0))(lambda: _ride_start(rides, ride_groups))

        def block():
            p_w, p_c, _, (off, _, _, _, _, _, _, _) = _attn_probs(cfg, j, q_ref, k_ref, q2_ref, k2_ref, bias_ref, sink_ref)
            o = _dot(p_w.astype(BF16), v_ref[pl.ds(off, W), :]) + _dot(p_c.astype(BF16), v_ref[pl.ds(S, C), :])
            for g in range(G):
                o_ref[:, g * LANE:(g + 1) * LANE] = o[g * bq:(g + 1) * bq].astype(o_ref.dtype)

        if cfg.ctx_queries:
            block()
        else:
            pl.when(j * bq < S)(block)

            @pl.when(j * bq >= S)
            def _():
                o_ref[...] = jnp.zeros_like(o_ref)

        if rides:
            pl.when((h == cfg.Hkv - 1) & (j == n_q - 1))(lambda: _ride_wait(rides, ride_groups))

    aliases = _ride_aliases(rides, len(operands), 1)
    if has_buf:
        aliases[len(operands) + n_ride] = 0
    outs = pl.pallas_call(
        body, name=name, grid=(cfg.Hkv, n_q),
        in_specs=_attn_specs(cfg, q_cb, k_cb, v_cb, q2_cb, k2_cb) + [_ANY] * (n_ride + has_buf),
        out_specs=[pl.BlockSpec((bq, G * LANE), lambda h, j: (j, col0 // G + h))] + [_ANY] * n_ride,
        out_shape=[jax.ShapeDtypeStruct((T, width), BF16)] + [s for r in rides for s in r.out_shapes()],
        scratch_shapes=_ride_scratch(rides),
        input_output_aliases=aliases,
        compiler_params=_cparams(("arbitrary", "arbitrary") if rides else ("parallel", "parallel")),
    )(*operands, *[a for r in rides for a in r.arrays], *([into[0]] if has_buf else []))
    if not rides:
        return outs[0]
    return outs[0], _ride_outputs(rides, outs[1:])


def _attn_bwd(cfg, q, q_cb, k, k_cb, v, v_cb, do, do_cb, *, q2=None, q2_cb=0, k2=None, k2_cb=0, bias=None, sink=None,
              want_dbias=False, dq_dtype=F32, rides=(), name):
    G, T, S, C, W, Hkv = cfg.G, cfg.T, cfg.S, cfg.C, cfg.W, cfg.Hkv
    assert q_cb % G == 0 and do_cb % G == 0 and not (want_dbias and G > 1)
    operands = [q, k, v] + ([q2, k2] if cfg.two else []) + ([bias] if cfg.n_var else []) + ([sink] if cfg.has_sink else [])
    operands.append(do)
    in_specs = _attn_specs(cfg, q_cb, k_cb, v_cb, q2_cb, k2_cb)
    bq = cfg.bq
    do_blocks = do.shape[0] // bq
    assert do.shape[0] == T or (do.shape[0] == S and not cfg.ctx_queries)
    in_specs.append(pl.BlockSpec((bq, G * LANE), lambda h, j: (jnp.minimum(j, do_blocks - 1), do_cb // G + h)))

    out_specs = [pl.BlockSpec((bq, G * LANE), lambda h, j: (j, h)),
                 pl.BlockSpec((T, LANE), lambda h, j: (0, h)),
                 pl.BlockSpec((T, LANE), lambda h, j: (0, h))]
    out_shape = [jax.ShapeDtypeStruct((T, Hkv * G * LANE), dq_dtype),
                 jax.ShapeDtypeStruct((T, Hkv * LANE), F32),
                 jax.ShapeDtypeStruct((T, Hkv * LANE), F32)]
    if cfg.two:
        out_specs += [pl.BlockSpec((bq, LANE), lambda h, j: (j, h)), pl.BlockSpec((T, LANE), lambda h, j: (0, 0))]
        out_shape += [jax.ShapeDtypeStruct((T, Hkv * LANE), dq_dtype), jax.ShapeDtypeStruct((T, LANE), F32)]
    if want_dbias:
        out_specs.append(pl.BlockSpec((1, cfg.n_var, BQ, W), lambda h, j: (h, 0, 0, 0)))
        out_shape.append(jax.ShapeDtypeStruct((Hkv, cfg.n_var, BQ, W), F32))
    if cfg.has_sink:
        out_specs.append(pl.BlockSpec((1, G * BQ, LANE), lambda h, j: (h, 0, 0)))
        out_shape.append(jax.ShapeDtypeStruct((Hkv, G * BQ, LANE), F32))

    n_ride = sum(len(r.arrays) for r in rides)
    operands += [a for r in rides for a in r.arrays]
    in_specs += [_ANY] * n_ride
    out_specs += [_ANY] * n_ride
    out_shape += [s for r in rides for s in r.out_shapes()]
    n_q = T // bq

    def body(*refs):
        (q_ref, k_ref, v_ref, q2_ref, k2_ref, bias_ref, sink_ref), rest = _attn_unpack(cfg, refs)
        do_ref, ride_in = rest[0], rest[1:1 + n_ride]
        dq_ref, dk_ref, dv_ref = rest[1 + n_ride:4 + n_ride]
        rest = rest[4 + n_ride:]
        dq2_ref = dk2_ref = dbias_ref = dsink_ref = None
        if cfg.two:
            dq2_ref, dk2_ref = rest[:2]
            rest = rest[2:]
        if want_dbias:
            dbias_ref = rest[0]
            rest = rest[1:]
        if cfg.has_sink:
            dsink_ref = rest[0]
            rest = rest[1:]
        ride_groups = _ride_split(rides, ride_in, rest[:n_ride], rest[n_ride:])
        h = pl.program_id(0)
        j = pl.program_id(1)
        if rides:
            pl.when((h == 0) & (j == 0))(lambda: _ride_start(rides, ride_groups))

        @pl.when(j == 0)
        def _():
            dk_ref[...] = jnp.zeros_like(dk_ref)
            dv_ref[...] = jnp.zeros_like(dv_ref)
            if want_dbias:
                dbias_ref[...] = jnp.zeros_like(dbias_ref)
            if cfg.has_sink:
                dsink_ref[...] = jnp.zeros_like(dsink_ref)

        if cfg.two:
            @pl.when((j == 0) & (h == 0))
            def _():
                dk2_ref[...] = jnp.zeros_like(dk2_ref)

        def block():
            p_w, p_c, p_s, (off, var, qs, kw, kc, q2s, k2w, k2c) = _attn_probs(
                cfg, j, q_ref, k_ref, q2_ref, k2_ref, bias_ref, sink_ref)
            dot_ = do_ref[...]
            dos = jnp.concatenate([dot_[:, g * LANE:(g + 1) * LANE] for g in range(G)], axis=0) if G > 1 else dot_
            dos = dos.astype(BF16)
            vw = v_ref[pl.ds(off, W), :]
            vc = v_ref[pl.ds(S, C), :]
            dp_w = _dot_nt(dos, vw)
            dp_c = _dot_nt(dos, vc)
            delta = jnp.sum(p_w * dp_w, axis=-1, keepdims=True) + jnp.sum(p_c * dp_c, axis=-1, keepdims=True)
            ds_w = p_w * (dp_w - delta)
            ds_c = p_c * (dp_c - delta)
            if want_dbias:
                dbias_ref[0, pl.ds(var, 1)] += ds_w[None]
            if cfg.has_sink:
                dsink_ref[0] += jnp.broadcast_to(-(p_s * delta), (G * bq, LANE))
            dsw = (ds_w * cfg.scale).astype(BF16)
            dsc = (ds_c * cfg.scale).astype(BF16)
            dq = _dot(dsw, kw) + _dot(dsc, kc)
            for g in range(G):
                dq_ref[:, g * LANE:(g + 1) * LANE] = dq[g * bq:(g + 1) * bq].astype(dq_ref.dtype)
            dk_ref[pl.ds(off, W), :] += _dot_tn(dsw, qs)
            dk_ref[pl.ds(S, C), :] += _dot_tn(dsc, qs)
            dv_ref[pl.ds(off, W), :] += _dot_tn(p_w.astype(BF16), dos)
            dv_ref[pl.ds(S, C), :] += _dot_tn(p_c.astype(BF16), dos)
            if cfg.two:
                dq2_ref[...] = (_dot(dsw, k2w) + _dot(dsc, k2c)).astype(dq2_ref.dtype)
                dk2_ref[pl.ds(off, W), :] += _dot_tn(dsw, q2s)
                dk2_ref[pl.ds(S, C), :] += _dot_tn(dsc, q2s)

        if cfg.ctx_queries:
            block()
        else:
            pl.when(j * bq < S)(block)

            @pl.when(j * bq >= S)
            def _():
                dq_ref[...] = jnp.zeros_like(dq_ref)
                if cfg.two:
                    dq2_ref[...] = jnp.zeros_like(dq2_ref)

        if rides:
            pl.when((h == Hkv - 1) & (j == n_q - 1))(lambda: _ride_wait(rides, ride_groups))

    outs = pl.pallas_call(
        body, name=name, grid=(Hkv, n_q),
        in_specs=in_specs, out_specs=out_specs, out_shape=out_shape,
        scratch_shapes=_ride_scratch(rides),
        compiler_params=_cparams(("arbitrary", "arbitrary")),
    )(*operands)
    if not rides:
        return outs
    return list(outs[:len(outs) - n_ride]) + [_ride_outputs(rides, outs[len(outs) - n_ride:])]


def _na_bias(rpb, S):
    H = rpb.shape[0]
    rows = S // GRID_W
    pad_l = GRID_W - 1 - (NA_WIN_C - 1)
    ext = jnp.concatenate([jnp.broadcast_to(rpb[:, :, :1], (H, 2 * NA_WIN_R - 1, pad_l)), rpb,
                           jnp.broadcast_to(rpb[:, :, -1:], (H, 2 * NA_WIN_R - 1, pad_l))], axis=2)
    by_col = jnp.stack([ext[:, :, GRID_W - 1 - qc:2 * GRID_W - 1 - qc] for qc in range(GRID_W)], axis=2)
    cq = np.arange(GRID_W)
    c0 = np.clip(cq - NA_WIN_C // 2, 0, GRID_W - NA_WIN_C)
    col_in = (cq[None, :] >= c0[:, None]) & (cq[None, :] < c0[:, None] + NA_WIN_C)
    n_lat = S // BQ
    neg_tile = jnp.full((H, GRID_W, GRID_W), NEG, F32)
    variants = []
    for v in range(5):
        j = {0: 0, 1: 1, 2: 2, 3: n_lat - 2, 4: n_lat - 1}[v]
        start = int(np.clip(j - 2, 0, n_lat - 5))
        assert j - start == v
        q_rows = []
        for qr in range(2):
            r = 2 * j + qr
            r0 = int(np.clip(r - NA_WIN_R // 2, 0, rows - NA_WIN_R))
            k_tiles = []
            for kr in range(10):
                krow = 2 * start + kr
                if r0 <= krow < r0 + NA_WIN_R:
                    k_tiles.append(jnp.where(col_in[None], by_col[:, krow - r + NA_WIN_R - 1], NEG))
                else:
                    k_tiles.append(neg_tile)
            q_rows.append(jnp.concatenate(k_tiles, axis=2))
        variants.append(jnp.concatenate(q_rows, axis=1))
    return jnp.stack(variants, axis=1)


def _swa_mask(S):
    qq = np.arange(BQ)[:, None]
    kk = np.arange(3 * BQ)[None, :]
    tiles = [np.where(np.abs(kk - v * BQ - qq) <= SWA_WINDOW, 0.0, NEG) for v in range(3)]
    return jnp.asarray(np.stack(tiles)[None], F32)


def _ffn_tiles(T, S, F):
    return _row_block(T, S), _pick(F, 1408)


def _halo_rows(dtype):
    return 8 * 4 // jnp.dtype(dtype).itemsize


def _halo_specs(T, bt, bf, dtype):
    hr = _halo_rows(dtype)
    nh = bt // hr
    return [pl.BlockSpec((bt, bf), lambda f, i: (i, f)),
            pl.BlockSpec((hr, bf), lambda f, i: (jnp.maximum(i * nh - 1, 0), f)),
            pl.BlockSpec((hr, bf), lambda f, i: (jnp.minimum((i + 1) * nh, T // hr - 1), f))]


def _neighbours(x, prev, nxt, i, bt, S, T):
    r = lax.broadcasted_iota(jnp.int32, x.shape, 0)
    g0 = i * bt
    first_open = jnp.logical_or(g0 == 0, g0 == S)
    last_open = jnp.logical_or(g0 + bt == S, g0 + bt == T)
    hr = prev.shape[0]
    before = jnp.where(r == 0, jnp.where(first_open, 0.0, prev[hr - 1:hr, :].astype(F32)), pltpu.roll(x, 1, 0))
    after = jnp.where(r == bt - 1, jnp.where(last_open, 0.0, nxt[0:1, :].astype(F32)), pltpu.roll(x, bt - 1, 0))
    return before, after


def _sigmoid(a):
    return 1.0 / (1.0 + jnp.exp(-a))


def _ffn_fwd(gp, u, cw, cb, S, *, name):
    T, F = gp.shape
    bt, bf = _ffn_tiles(T, S, F)

    def body(g_ref, gp_ref, gn_ref, u_ref, w_ref, b_ref, o_ref):
        i = pl.program_id(1)
        g = g_ref[...].astype(F32)
        before, after = _neighbours(g, gp_ref[...], gn_ref[...], i, bt, S, T)
        a = before * w_ref[0:1, :] + g * w_ref[1:2, :] + after * w_ref[2:3, :] + b_ref[...]
        o_ref[...] = (a * _sigmoid(a) * u_ref[...].astype(F32)).astype(o_ref.dtype)

    return pl.pallas_call(
        body, name=name, grid=(F // bf, T // bt),
        in_specs=_halo_specs(T, bt, bf, gp.dtype) + [pl.BlockSpec((bt, bf), lambda f, i: (i, f)),
                                              pl.BlockSpec((3, bf), lambda f, i: (0, f)),
                                              pl.BlockSpec((1, bf), lambda f, i: (0, f))],
        out_specs=pl.BlockSpec((bt, bf), lambda f, i: (i, f)),
        out_shape=jax.ShapeDtypeStruct((T, F), BF16),
        compiler_params=_cparams(("parallel", "parallel")),
    )(gp, gp, gp, u, cw, cb)


def _ffn_bwd_act(gp, u, da_out, cw, cb, S, *, name):
    T, F = gp.shape
    bt, bf = _ffn_tiles(T, S, F)

    def body(g_ref, gp_ref, gn_ref, u_ref, d_ref, w_ref, b_ref, da_ref, du_ref, acc_ref):
        i = pl.program_id(1)

        @pl.when(i == 0)
        def _():
            acc_ref[...] = jnp.zeros_like(acc_ref)

        g = g_ref[...].astype(F32)
        before, after = _neighbours(g, gp_ref[...], gn_ref[...], i, bt, S, T)
        a = before * w_ref[0:1, :] + g * w_ref[1:2, :] + after * w_ref[2:3, :] + b_ref[...]
        sig = _sigmoid(a)
        d = d_ref[...].astype(F32)
        du_ref[...] = (d * (a * sig)).astype(du_ref.dtype)
        da = d * u_ref[...].astype(F32) * (sig * (1.0 + a * (1.0 - sig)))
        da_ref[...] = da
        acc_ref[0:1, :] += jnp.sum(da * before, axis=0, keepdims=True)
        acc_ref[1:2, :] += jnp.sum(da * g, axis=0, keepdims=True)
        acc_ref[2:3, :] += jnp.sum(da * after, axis=0, keepdims=True)
        acc_ref[3:4, :] += jnp.sum(da, axis=0, keepdims=True)

    blk = pl.BlockSpec((bt, bf), lambda f, i: (i, f))
    return pl.pallas_call(
        body, name=name, grid=(F // bf, T // bt),
        in_specs=_halo_specs(T, bt, bf, gp.dtype) + [blk, blk,
                                              pl.BlockSpec((3, bf), lambda f, i: (0, f)),
                                              pl.BlockSpec((1, bf), lambda f, i: (0, f))],
        out_specs=[blk, blk, pl.BlockSpec((8, bf), lambda f, i: (0, f))],
        out_shape=[jax.ShapeDtypeStruct((T, F), F32), jax.ShapeDtypeStruct((T, F), BF16),
                   jax.ShapeDtypeStruct((8, F), F32)],
        compiler_params=_cparams(("parallel", "arbitrary")),
    )(gp, gp, gp, u, da_out, cw, cb)


def _ffn_bwd_conv(da, cw, S, *, name):
    T, F = da.shape
    bt, bf = _ffn_tiles(T, S, F)

    def body(d_ref, dp_ref, dn_ref, w_ref, o_ref):
        i = pl.program_id(1)
        d = d_ref[...]
        before, after = _neighbours(d, dp_ref[...], dn_ref[...], i, bt, S, T)
        o_ref[...] = (after * w_ref[0:1, :] + d * w_ref[1:2, :] + before * w_ref[2:3, :]).astype(o_ref.dtype)

    return pl.pallas_call(
        body, name=name, grid=(F // bf, T // bt),
        in_specs=_halo_specs(T, bt, bf, da.dtype) + [pl.BlockSpec((3, bf), lambda f, i: (0, f))],
        out_specs=pl.BlockSpec((bt, bf), lambda f, i: (i, f)),
        out_shape=jax.ShapeDtypeStruct((T, F), BF16),
        compiler_params=_cparams(("parallel", "parallel")),
    )(da, da, da, cw)


def _ew_rows(R, N, n_arrays):
    return _pick(R, max(16, EW_VMEM_BUDGET // (8 * n_arrays * N)), 16)


def _adam(w, g, m, v, *, rides=(), emit_grad=False, name):
    lead = w.shape[:-2]
    R, N = w.shape[-2:]
    br = _ew_rows(R, N, 7)
    bc1 = 1.0 - ADAM_B1 ** ADAM_STEP
    bc2 = 1.0 - ADAM_B2 ** ADAM_STEP
    grid = lead + (R // br,)
    n_ride = sum(len(r.arrays) for r in rides)

    n_out = 4 if emit_grad else 3

    def body(*refs):
        w_ref, g_ref, m_ref, v_ref = refs[:4]
        d_ref, mo_ref, vo_ref = refs[4 + n_ride:7 + n_ride]
        if emit_grad:
            refs[7 + n_ride][...] = g_ref[...]
        if rides:
            ride_groups = _ride_split(rides, refs[4:4 + n_ride], refs[4 + n_out + n_ride:4 + n_out + 2 * n_ride],
                                      refs[4 + n_out + 2 * n_ride:])
            steps = [pl.program_id(d) for d in range(len(grid))]
            pl.when(functools.reduce(jnp.logical_and, [s == 0 for s in steps]))(lambda: _ride_start(rides, ride_groups))
        gv = g_ref[...]
        mn = ADAM_B1 * m_ref[...] + (1.0 - ADAM_B1) * gv
        vn = ADAM_B2 * v_ref[...] + (1.0 - ADAM_B2) * (gv * gv)
        mo_ref[...] = mn
        vo_ref[...] = vn
        d_ref[...] = -ADAM_LR * ((mn / bc1) / (jnp.sqrt(vn / bc2) + ADAM_EPS) + ADAM_WD * w_ref[...])
        if rides:
            pl.when(functools.reduce(jnp.logical_and, [s == n - 1 for s, n in zip(steps, grid)]))(
                lambda: _ride_wait(rides, ride_groups))

    if lead:
        blk = pl.BlockSpec((None, br, N), lambda l, i: (l, i, 0))
    else:
        blk = pl.BlockSpec((br, N), lambda i: (i, 0))
    shp = jax.ShapeDtypeStruct(w.shape, F32)
    outs = pl.pallas_call(
        body, name=name, grid=grid,
        in_specs=[blk, blk, blk, blk] + [_ANY] * n_ride, out_specs=[blk] * n_out + [_ANY] * n_ride,
        out_shape=[shp] * n_out + [s for r in rides for s in r.out_shapes()],
        scratch_shapes=_ride_scratch(rides),
        input_output_aliases=_ride_aliases(rides, 4, n_out),
        compiler_params=_cparams(("arbitrary" if rides else "parallel",) * len(grid)),
    )(w, g, m, v, *[a for r in rides for a in r.arrays])
    if not rides:
        return outs
    return outs[:n_out], _ride_outputs(rides, outs[n_out:])


def _sum_lead(x, out_dtype, *, name):
    n, R, N = x.shape
    br = _ew_rows(R, N, n + 1)

    def body(x_ref, o_ref):
        acc = x_ref[0].astype(F32)
        for k in range(1, n):
            acc = acc + x_ref[k].astype(F32)
        o_ref[...] = acc.astype(o_ref.dtype)

    return pl.pallas_call(
        body, name=name, grid=(R // br,),
        in_specs=[pl.BlockSpec((n, br, N), lambda i: (0, i, 0))],
        out_specs=pl.BlockSpec((br, N), lambda i: (i, 0)),
        out_shape=jax.ShapeDtypeStruct((R, N), out_dtype),
        compiler_params=_cparams(("parallel",)),
    )(x)


def _sum_parts(parts, landed, chip, core, stack, *, name):
    _, R, N = parts.shape
    n_layers, layer, buf = stack
    br = _ew_rows(R, N, 5)

    def body(pos_ref, own_ref, landed_ref, *rest):
        o_ref = rest[-1]
        acc = own_ref[...].astype(F32)
        for k in range(3):
            acc = acc + landed_ref[k].astype(F32)
        o_ref[...] = acc

    operands = [jnp.stack([chip, core]).astype(jnp.int32), parts, landed]
    in_specs = [pl.BlockSpec((None, br, N), lambda i, pos: (pos[0], i, 0)),
                pl.BlockSpec((3, br, N), lambda i, pos: (0, i, 0))]
    aliases = {}
    if buf is not None:
        aliases = {3: 0}
        operands.append(buf)
        in_specs.append(pl.BlockSpec(memory_space=pl.ANY))
    return pl.pallas_call(
        body, name=name,
        grid_spec=pltpu.PrefetchScalarGridSpec(
            num_scalar_prefetch=1, grid=(R // br,), in_specs=in_specs,
            out_specs=pl.BlockSpec((None, None, br, N), lambda i, pos: (layer, pos[1], i, 0))),
        out_shape=jax.ShapeDtypeStruct((n_layers, 2, R, N), F32),
        input_output_aliases=aliases,
        compiler_params=_cparams(("parallel",)),
    )(*operands)


def _place_own(shards, layer, core, slot, *, name):
    _, R, N = shards.shape
    br = _ew_rows(R // 2, N, 2)
    nb = R // 2 // br

    def body(pos_ref, x_ref, o_ref):
        o_ref[...] = x_ref[...].astype(o_ref.dtype)

    return pl.pallas_call(
        body, name=name,
        grid_spec=pltpu.PrefetchScalarGridSpec(
            num_scalar_prefetch=1, grid=(nb,),
            in_specs=[pl.BlockSpec((None, br, N), lambda i, pos: (layer, pos[0] * nb + i, 0))],
            out_specs=pl.BlockSpec((None, br, N), lambda i, pos: (pos[1], i, 0))),
        out_shape=jax.ShapeDtypeStruct((8, R // 2, N), BF16),
        compiler_params=_cparams(("parallel",)),
    )(jnp.stack([core, slot]).astype(jnp.int32), shards)


def _add_half(g, r, core, *, name):
    Q, _, R, N = g.shape
    br = _ew_rows(R, N, 3)

    def body(c_ref, g_ref, r_ref, o_ref):
        o_ref[...] = (g_ref[...] + r_ref[...]).astype(o_ref.dtype)

    return pl.pallas_call(
        body, name=name,
        grid_spec=pltpu.PrefetchScalarGridSpec(
            num_scalar_prefetch=1, grid=(Q, R // br),
            in_specs=[pl.BlockSpec((None, None, br, N), lambda q, i, c_ref: (q, c_ref[0], i, 0)),
                      pl.BlockSpec((None, br, N), lambda q, i, c_ref: (q, i, 0))],
            out_specs=pl.BlockSpec((None, br, N), lambda q, i, c_ref: (q, i, 0))),
        out_shape=jax.ShapeDtypeStruct((Q, R, N), BF16),
        compiler_params=_cparams(("parallel", "parallel")),
    )(core.reshape(1).astype(jnp.int32), g, r)


_ANY = pl.BlockSpec(memory_space=pl.ANY)


def _place():
    return lax.axis_index("x"), lax.axis_index("y"), lax.axis_index("c")


def _allgather8(blocks, *, name):
    n = len(blocks)

    def body(*refs):
        xs, outs = refs[:n], refs[n:2 * n]
        send_sems, recv_sems, local_sems = refs[2 * n:]
        x, y, c = _place()
        me, sibling = (x, y, c), (x, y, 1 - c)
        chips = [(1 - x, y), (x, 1 - y), (1 - x, 1 - y)]

        def slot(a, px, py, pc):
            return outs[a].at[4 * px + 2 * py + pc]

        def copy(a, k, block, to, src=None):
            return pltpu.make_async_remote_copy(
                src_ref=slot(a, *block) if src is None else src, dst_ref=slot(a, *block),
                send_sem=send_sems.at[a, k], recv_sem=recv_sems.at[a, k], device_id=to, device_id_type=MESH)

        mine = [pltpu.make_async_copy(xs[a], slot(a, *me), local_sems.at[a]) for a in range(n)]
        for cp in mine:
            cp.start()
        first = []
        for a in range(n):
            first.append(copy(a, 0, me, sibling, src=xs[a]))
            first += [copy(a, 1 + j, me, (*chip, c), src=xs[a]) for j, chip in enumerate(chips)]
        for cp in first:
            cp.start()
        passed = []
        for j, chip in enumerate(chips):
            for a in range(n):
                copy(a, 1 + j, (*chip, c), me).wait_recv()
                fwd = copy(a, 4 + j, (*chip, c), sibling)
                fwd.start()
                passed.append(fwd)
        for a in range(n):
            copy(a, 0, sibling, me).wait_recv()
            for j, chip in enumerate(chips):
                copy(a, 4 + j, (*chip, 1 - c), me).wait_recv()
        for cp in first + passed:
            cp.wait_send()
        for cp in mine:
            cp.wait()

    return pl.pallas_call(
        body, name=name,
        in_specs=[_ANY] * n, out_specs=[_ANY] * n,
        out_shape=[jax.ShapeDtypeStruct((8,) + b.shape, b.dtype) for b in blocks],
        scratch_shapes=[pltpu.SemaphoreType.DMA((n, 7)), pltpu.SemaphoreType.DMA((n, 7)), pltpu.SemaphoreType.DMA((n,))],
    )(*blocks)


class _Exchange:
    n_sems = 1

    def __init__(self, arrays):
        self.arrays = list(arrays)

    def out_shapes(self):
        return [jax.ShapeDtypeStruct(g.shape[:1] + g.shape[2:], g.dtype) for g in self.arrays]

    def copy(self, k, src, dst, sems, landing):
        x, y, c = _place()
        return pltpu.make_async_remote_copy(src_ref=src.at[:, 1 - c], dst_ref=dst, send_sem=sems[0], recv_sem=sems[1],
                                            device_id=(x, y, 1 - c), device_id_type=MESH)

    def copies(self, group, landing):
        xs, outs, send_sems, recv_sems = group
        return [self.copy(k, xs[a], outs[a], (send_sems.at[a, k], recv_sems.at[a, k]), landing)
                for a in range(len(xs)) for k in range(self.n_sems)]


class _Scatter(_Exchange):
    n_sems = 3

    def out_shapes(self):
        return [jax.ShapeDtypeStruct((3,) + p.shape[1:], p.dtype) for p in self.arrays]

    def copy(self, k, src, dst, sems, landing):
        x, y, c = _place()
        px, py = [(1 - x, y), (x, 1 - y), (1 - x, 1 - y)][k]
        return pltpu.make_async_remote_copy(src_ref=src.at[2 * px + py], dst_ref=dst.at[k], send_sem=sems[0], recv_sem=sems[1],
                                            device_id=(px, py, c), device_id_type=MESH)


class _GatherChips(_Exchange):
    n_sems = 3
    in_place = True

    def out_shapes(self):
        return [jax.ShapeDtypeStruct(b.shape, b.dtype) for b in self.arrays]

    def copy(self, k, src, dst, sems, landing):
        x, y, c = _place()
        px, py = [(1 - x, y), (x, 1 - y), (1 - x, 1 - y)][k]
        slot = 4 * px + 2 * py + c if landing else 4 * x + 2 * y + c
        return pltpu.make_async_remote_copy(src_ref=src.at[4 * x + 2 * y + c], dst_ref=dst.at[slot], send_sem=sems[0],
                                            recv_sem=sems[1], device_id=(px, py, c), device_id_type=MESH)


class _GatherCores(_GatherChips):
    n_sems = 4

    def copy(self, k, src, dst, sems, landing):
        x, y, c = _place()
        slot = 2 * k + 1 - c if landing else 2 * k + c
        return pltpu.make_async_remote_copy(src_ref=src.at[2 * k + c], dst_ref=dst.at[slot], send_sem=sems[0],
                                            recv_sem=sems[1], device_id=(x, y, 1 - c), device_id_type=MESH)


class _Join(_GatherChips):
    n_sems = 1

    def copy(self, k, src, dst, sems, landing):
        x, y, c = _place()
        return pltpu.make_async_remote_copy(src_ref=src.at[:, c], dst_ref=dst.at[:, 1 - c if landing else c], send_sem=sems[0],
                                            recv_sem=sems[1], device_id=(x, y, 1 - c), device_id_type=MESH)


def _ride_aliases(rides, first_in, first_out):
    aliases, i = {}, 0
    for r in rides:
        for a in range(len(r.arrays)):
            if getattr(r, "in_place", False):
                aliases[first_in + i + a] = first_out + i + a
        i += len(r.arrays)
    return aliases


def _ride_scratch(rides):
    shapes = []
    for r in rides:
        shapes += [pltpu.SemaphoreType.DMA((len(r.arrays), r.n_sems)), pltpu.SemaphoreType.DMA((len(r.arrays), r.n_sems))]
    return shapes


def _ride_split(rides, in_refs, out_refs, sem_refs):
    groups, i, o = [], 0, 0
    for k, r in enumerate(rides):
        n = len(r.arrays)
        groups.append((in_refs[i:i + n], out_refs[o:o + n], sem_refs[2 * k], sem_refs[2 * k + 1]))
        i, o = i + n, o + n
    return groups


def _ride_start(rides, groups):
    for r, g in zip(rides, groups):
        for cp in r.copies(g, False):
            cp.start()


def _ride_wait(rides, groups):
    for r, g in zip(rides, groups):
        for cp in r.copies(g, True):
            cp.wait_recv()
        for cp in r.copies(g, False):
            cp.wait_send()


def _run_rides(rides, *, name):
    n_in = sum(len(r.arrays) for r in rides)

    def body(*refs):
        groups = _ride_split(rides, refs[:n_in], refs[n_in:2 * n_in], refs[2 * n_in:])
        _ride_start(rides, groups)
        _ride_wait(rides, groups)

    outs = pl.pallas_call(
        body, name=name,
        in_specs=[_ANY] * n_in, out_specs=[_ANY] * n_in,
        out_shape=[s for r in rides for s in r.out_shapes()],
        scratch_shapes=_ride_scratch(rides),
        input_output_aliases=_ride_aliases(rides, 0, 0),
    )(*[a for r in rides for a in r.arrays])
    return _ride_outputs(rides, outs)


def _ride_outputs(rides, outs):
    res, o = [], 0
    for r in rides:
        res.append(list(outs[o:o + len(r.arrays)]))
        o += len(r.arrays)
    return res


def _perm_w_in(wt):
    pad = jnp.zeros((PCOLS - IN_COLS, wt.shape[1]), wt.dtype)
    return jnp.concatenate([wt[:3072], wt[3136:IN_COLS], wt[3072:3136], pad], axis=0)


def _unperm_w_in(gt):
    return jnp.concatenate([gt[:3072], gt[4096:IN_COLS], gt[3072:4096]], axis=0)


def _perm_w_uq(w):
    w4 = w.reshape(MLA_Q_LORA, MLA_HEADS, MLA_NOPE + MLA_ROPE)
    nope = w4[:, :, :MLA_NOPE].reshape(MLA_Q_LORA, MLA_HEADS * LANE)
    pe = jnp.pad(w4[:, :, MLA_NOPE:], ((0, 0), (0, 0), (0, LANE - MLA_ROPE))).reshape(MLA_Q_LORA, MLA_HEADS * LANE)
    return jnp.concatenate([nope, pe], axis=1)


def _unperm_w_uq(g):
    nope = g[:, :MLA_HEADS * LANE].reshape(MLA_Q_LORA, MLA_HEADS, LANE)
    pe = g[:, MLA_HEADS * LANE:].reshape(MLA_Q_LORA, MLA_HEADS, LANE)[:, :, :MLA_ROPE]
    return jnp.concatenate([nope, pe], axis=2).reshape(MLA_Q_LORA, MLA_HEADS * (MLA_NOPE + MLA_ROPE))


def _perm_w_ukv(w):
    w4 = w.reshape(MLA_KV_LORA, MLA_HEADS, MLA_NOPE + MLA_V)
    return jnp.concatenate([w4[:, :, :MLA_NOPE].reshape(MLA_KV_LORA, -1), w4[:, :, MLA_NOPE:].reshape(MLA_KV_LORA, -1)], axis=1)


def _unperm_w_ukv(g):
    kn = g[:, :MLA_HEADS * LANE].reshape(MLA_KV_LORA, MLA_HEADS, LANE)
    vv = g[:, MLA_HEADS * LANE:].reshape(MLA_KV_LORA, MLA_HEADS, LANE)
    return jnp.concatenate([kn, vv], axis=2).reshape(MLA_KV_LORA, -1)


def _silu(v):
    return v * jax.nn.sigmoid(v)


def _silu_grad(v):
    s = jax.nn.sigmoid(v)
    return s * (1.0 + v * (1.0 - s))


_WEIGHTS = ("c_ctx", "w_ada", "b_ada", "w_in", "na_rpb", "swa_sink", "mla_q_norm", "mla_kv_norm", "mla_w_uq", "mla_w_ukv",
            "gqa_q_norm", "gqa_k_norm", "w_out", "ln1_g", "ln1_b", "ffn_w_gate", "ffn_w_up", "ffn_conv_w", "ffn_conv_b",
            "ffn_w_down", "ln2_g", "ln2_b")
_COL_SHARDED = ("mla_w_uq", "mla_w_ukv", "ffn_w_gate", "ffn_w_up")
_ROW_SHARDED = ("w_out", "ffn_w_down")
_BIG = ("w_in",) + _COL_SHARDED + _ROW_SHARDED
_SMALL = ("c_ctx", "b_ada", "na_rpb", "swa_sink", "mla_q_norm", "mla_kv_norm", "gqa_q_norm", "gqa_k_norm", "ln1_g", "ln1_b",
          "ffn_conv_w", "ffn_conv_b", "ln2_g", "ln2_b")


def _pack(arrays):
    flat = jnp.concatenate([a.reshape(-1) for a in arrays])
    n = flat.shape[0]
    rows = -(-n // (8 * LANE)) * 8
    return jnp.pad(flat, (0, rows * LANE - n)).reshape(rows, LANE)


def _unpack(packed, like):
    flat = packed.reshape(-1)
    out, o = [], 0
    for a in like:
        out.append(flat[o:o + a.size].reshape(a.shape))
        o += a.size
    return out


def _train_step(x, c, ctx, loss_target, w, m_in, v_in):
    L = DEPTH
    S, D = x.shape[1], x.shape[2]
    C = ctx.shape[1]
    T = S + C
    F = w["ffn_conv_b"].shape[1]
    ax, ay, ac = _place()
    chip = 2 * ax + ay
    dev = 2 * chip + ac
    n_ada = w["w_ada"].shape[2]
    w, m_in, v_in = dict(w), dict(m_in), dict(v_in)
    for d in (w, m_in, v_in):
        d["w_in"] = jnp.swapaxes(d["w_in"], 1, 2)

    gather_groups = {"A": ("w_in", "mla_w_uq", "mla_w_ukv"), "B": ("w_out",), "Cg": ("ffn_w_gate",), "Cu": ("ffn_w_up",),
                     "D": ("ffn_w_down",)}
    full = {n: [None] * L for n in _BIG}
    w_in_p, w_uq_p, w_ukv_p = [None] * L, [None] * L, [None] * L
    half_done = {}

    def chips_step(group, l):
        return _GatherChips([_place_own(w[n], l, ac, dev, name="gather_place") for n in gather_groups[group]])

    def cores_step(group, l):
        return _GatherCores(half_done.pop((group, l)))

    def finish_group(group, l, bufs):
        for n, b in zip(gather_groups[group], bufs):
            r, cols = b.shape[1:]
            if n in _COL_SHARDED:
                full[n][l] = b.reshape(4, 2, r, cols).transpose(1, 2, 0, 3).reshape(2 * r, 4 * cols)
            else:
                full[n][l] = b.reshape(8 * r, cols)
        if group == "A":
            w_in_p[l], w_uq_p[l] = _perm_w_in(full["w_in"][l]), _perm_w_uq(full["mla_w_uq"][l])
            w_ukv_p[l] = _perm_w_ukv(full["mla_w_ukv"][l])

    def with_rides(result, rides):
        return result if rides else (result, [])

    def my_half(a):
        r = a.shape[0] // 2
        return lax.dynamic_slice_in_dim(a, ac * r, r, axis=0).astype(BF16)

    gathered = _allgather8([my_half(w[n][0]) for n in gather_groups["A"]] + [w["ffn_conv_w"]], name="gather_weights")
    finish_group("A", 0, gathered[:-1])
    conv_w = gathered[-1][::2].transpose(1, 2, 0, 3).reshape(L, 3, F)

    (c_all,) = _allgather8([c], name="gather_c")
    c16 = jnp.concatenate([c_all.reshape(8, D), jnp.broadcast_to(w["c_ctx"][None], (8, D))], axis=0)
    row_keep = (jnp.arange(16) <= 8).astype(F32)[:, None]
    sc = _silu(c16) * row_keep
    b_loc = lax.dynamic_slice_in_dim(w["b_ada"], chip * n_ada, n_ada, axis=1)
    mod_loc = jnp.stack([_mm(sc, w["w_ada"], b_layer=l, name="mod_mm") + b_loc[l][None] for l in range(L)])
    (mod_g,) = _allgather8([mod_loc], name="gather_mod")
    mod_all = mod_g[::2].transpose(1, 2, 0, 3).reshape(L, 16, 4 * n_ada)
    mod_x = lax.dynamic_index_in_dim(mod_all, dev, axis=1, keepdims=False)
    mod_c = mod_all[:, 8]
    mods = [jnp.stack([mod_x[l].reshape(6, D), mod_c[l].reshape(6, D)], axis=1) for l in range(L)]

    tabs = _rope_tables(S, C, HEAD_DIM) + _rope_tables(S, C, MLA_ROPE)
    swa_mask = _swa_mask(S)
    scale = HEAD_DIM ** -0.5
    def attn_cfgs(l):
        cq = l < L - 1
        return (_AttnCfg(Hkv=NA_HEADS, G=1, S=S, C=C, band=(2, 5), scale=scale, n_var=5, bias_per_head=True, ctx_queries=cq),
                _AttnCfg(Hkv=SWA_KV_HEADS, G=SWA_HEADS // SWA_KV_HEADS, S=S, C=C, band=(1, 3), scale=scale, n_var=3,
                         has_sink=True, ctx_queries=cq),
                _AttnCfg(Hkv=MLA_HEADS, G=1, S=S, C=C, band=None, scale=(MLA_NOPE + MLA_ROPE) ** -0.5, two=True,
                         bq=2 * BQ, ctx_queries=cq),
                _AttnCfg(Hkv=GQA_KV_HEADS, G=GQA_HEADS // GQA_KV_HEADS, S=S, C=C, band=None, scale=scale, ctx_queries=cq))

    row = lambda a: a[None, :]

    xt = jnp.concatenate([x[0], ctx[0]], axis=0)
    saved = []
    for l in range(L):
        md = mods[l]
        gq, gk, mq, mkv = row(w["gqa_q_norm"][l]), row(w["gqa_k_norm"][l]), row(w["mla_q_norm"][l]), row(w["mla_kv_norm"][l])
        h1 = _mod_fwd(xt, md[0], md[1], S, name="mod_fwd")
        p = _mm(h1, w_in_p[l], mode="nt", out_dtype=BF16, name="in_proj")
        qkv = _prep_fwd(p, tabs, gq, gk, mq, mkv, S, name="prep_fwd")
        qm = _mm(qkv, w_uq_p[l], a_off=CB_CQ * LANE, a_k=MLA_Q_LORA, tk=LANE, name="mla_uq")
        qmb = _pe_rope(qm, tabs, S, 1.0, BF16, name="mla_q_rope")
        kvm = _mm(qkv, w_ukv_p[l], a_off=CB_CKV * LANE, a_k=MLA_KV_LORA, tk=LANE, out_dtype=BF16, name="mla_ukv")
        bias_na = _na_bias(w["na_rpb"][l], S)
        sink = jnp.broadcast_to(jnp.repeat(w["swa_sink"][l].reshape(SWA_KV_HEADS, -1), BQ, axis=1)[:, :, None],
                                (SWA_KV_HEADS, SWA_HEADS // SWA_KV_HEADS * BQ, LANE))
        cfg_na, cfg_swa, cfg_mla, cfg_gqa = attn_cfgs(l)
        rows = T if l < L - 1 else S
        first, more = l == 0, l + 1 < L
        rides = [chips_step("B", l)] if first else [cores_step("B", l)]
        mix_blocks = NA_HEADS + SWA_HEADS + MLA_HEADS + GQA_HEADS
        mix, got = _attn_fwd(cfg_na, qkv, CB_NA_Q, qkv, CB_NA_K, qkv, CB_NA_V, bias=bias_na, rides=rides,
                             into=(None, 0, mix_blocks), name="na_fwd")
        if first:
            half_done[("B", l)] = got[0]
        else:
            finish_group("B", l, got[0])
        rides = [cores_step("B", l)] if first else []
        mix, got = with_rides(_attn_fwd(cfg_swa, qkv, CB_SWA_Q, qkv, CB_SWA_K, qkv, CB_SWA_V, bias=swa_mask, sink=sink,
                                        rides=rides, into=(mix, NA_HEADS, mix_blocks), name="swa_fwd"), rides)
        if first:
            finish_group("B", l, got[0])
        mix, got = _attn_fwd(cfg_mla, qmb, 0, kvm, 0, kvm, MLA_HEADS, q2=qmb, q2_cb=MLA_HEADS, k2=qkv, k2_cb=CB_KPE,
                             rides=[chips_step("Cg", l)], into=(mix, NA_HEADS + SWA_HEADS, mix_blocks), name="mla_fwd")
        half_done[("Cg", l)] = got[0]
        mix, got = _attn_fwd(cfg_gqa, qkv, CB_GQA_Q, qkv, CB_GQA_K, qkv, CB_GQA_V,
                             rides=[cores_step("Cg", l), chips_step("Cu", l)],
                             into=(mix, NA_HEADS + SWA_HEADS + MLA_HEADS, mix_blocks), name="gqa_fwd")
        finish_group("Cg", l, got[0])
        half_done[("Cu", l)] = got[1]
        z1, got = _mm(mix, full["w_out"][l], rows=rows, rides=[cores_step("Cu", l)], name="out_proj")
        finish_group("Cu", l, got[0])
        x1 = _res_fwd(xt, z1, md[2], row(w["ln1_g"][l]), row(w["ln1_b"][l]), S, name="res_fwd")
        h2 = _mod_fwd(x1, md[3], md[4], S, name="mod_fwd")
        gp, got = _mm(h2, full["ffn_w_gate"][l], rides=[chips_step("D", l)], out_dtype=BF16, name="ffn_in")
        half_done[("D", l)] = got[0]
        up, got = _mm(h2, full["ffn_w_up"][l], rides=[cores_step("D", l)] + ([chips_step("A", l + 1)] if more else []),
                      out_dtype=BF16, name="ffn_in")
        finish_group("D", l, got[0])
        if more:
            half_done[("A", l + 1)] = got[1]
        act = _ffn_fwd(gp, up, conv_w[l], row(w["ffn_conv_b"][l]), S, name="ffn_mid")
        rides = [cores_step("A", l + 1), chips_step("B", l + 1)] if more else []
        z2, got = with_rides(_mm(act, full["ffn_w_down"][l], rides=rides, name="ffn_out"), rides)
        if more:
            finish_group("A", l + 1, got[0])
            half_done[("B", l + 1)] = got[1]
        x2 = _res_fwd(x1, z2, md[5], row(w["ln2_g"][l]), row(w["ln2_b"][l]), S, name="res_fwd")
        saved.append(dict(x=xt, h1=h1, p=p, qkv=qkv, qmb=qmb, kvm=kvm, bias_na=bias_na, sink=sink, mix=mix, z1=z1, x1=x1,
                          h2=h2, gp=gp, up=up, act=act, z2=z2, cfgs=(cfg_na, cfg_swa, cfg_mla, cfg_gqa)))
        xt = x2

    dx, loss_part = _loss_fwd_bwd(xt, loss_target[0], S, name="loss")
    loss = lax.psum(loss_part[0, 0], ("x", "y", "c"))

    groups = {"ffn": ("ffn_w_gate", "ffn_w_up", "ffn_w_down", "w_out"), "rest": ("w_in", "mla_w_uq", "mla_w_ukv")}
    wgrad = [dict() for _ in range(L)]
    parts, landed = {}, {}

    def halves_of(group, l):
        return [wgrad[l][n].reshape(4, 2, wgrad[l][n].shape[1] // 2, wgrad[l][n].shape[2]) for n in groups[group]]

    def add_halves(group, l, received):
        parts[(group, l)] = [_add_half(h, r, ac, name="rs_core_add") for h, r in zip(halves_of(group, l), received)]

    small = {n: [None] * L for n in ("na_rpb", "swa_sink", "mla_q_norm", "mla_kv_norm", "gqa_q_norm", "gqa_k_norm",
                                     "ln1_g", "ln1_b", "ffn_conv_w", "ffn_conv_b", "ln2_g", "ln2_b")}
    dmod = [None] * L
    for l in reversed(range(L)):
        sv, md = saved[l], mods[l]
        gq, gk, mq, mkv = row(w["gqa_q_norm"][l]), row(w["gqa_k_norm"][l]), row(w["mla_q_norm"][l]), row(w["mla_kv_norm"][l])
        cb_row = row(w["ffn_conv_b"][l])
        dx1, dz2, acc_r2 = _res_bwd(sv["x1"], sv["z2"], md[5], row(w["ln2_g"][l]), dx, S, name="res_bwd")
        dact = _mm(dz2, full["ffn_w_down"][l], mode="nt", out_dtype=BF16, name="ffn_out_dx")
        wgrad[l]["ffn_w_down"] = _mm(sv["act"], dz2, mode="tn", name="ffn_out_dw").reshape(4, F // 4, D)
        da, du, acc_f = _ffn_bwd_act(sv["gp"], sv["up"], dact, conv_w[l], cb_row, S, name="ffn_mid_bwd")
        dg = _ffn_bwd_conv(da, conv_w[l], S, name="ffn_conv_bwd")
        dh2 = _mm(dg, full["ffn_w_gate"][l], mode="nt", name="ffn_in_dx")
        dh2 = _mm(du, full["ffn_w_up"][l], mode="nt", add=dh2, name="ffn_in_dx_add")
        wgrad[l]["ffn_w_gate"] = _mm(sv["h2"], dg, mode="tn", stack=(1, 0, None), split4=True,
                                     name="ffn_in_dw").reshape(4, D, F // 4)
        wgrad[l]["ffn_w_up"] = _mm(sv["h2"], du, mode="tn", stack=(1, 0, None), split4=True,
                                   name="ffn_in_dw").reshape(4, D, F // 4)
        dx1, acc_m2 = _mod_bwd(sv["x1"], dh2, md[4], dx1, S, name="mod_bwd")
        dxa, dz1, acc_r1 = _res_bwd(sv["x"], sv["z1"], md[2], row(w["ln1_g"][l]), dx1, S, name="res_bwd")
        dmix = _mm(dz1, full["w_out"][l], mode="nt", out_dtype=BF16, name="out_proj_dx")
        wgrad[l]["w_out"] = _mm(sv["mix"], dz1, mode="tn", rows=dz1.shape[0], name="out_proj_dw").reshape(4, -1, D)

        qkv, qmb, kvm = sv["qkv"], sv["qmb"], sv["kvm"]
        cfg_na, cfg_swa, cfg_mla, cfg_gqa = sv["cfgs"]
        rest_above = l + 1 < L
        rides = [_Exchange(halves_of("ffn", l))] + ([_Exchange(halves_of("rest", l + 1))] if rest_above else [])
        dq_a, dk_a, dv_a, dbias, received = _attn_bwd(cfg_na, qkv, CB_NA_Q, qkv, CB_NA_K, qkv, CB_NA_V, dmix, 0,
                                                      bias=sv["bias_na"], want_dbias=True, rides=rides, name="na_bwd")
        add_halves("ffn", l, received[0])
        if rest_above:
            add_halves("rest", l + 1, received[1])
        dq_b, dk_b, dv_b, dsink = _attn_bwd(cfg_swa, qkv, CB_SWA_Q, qkv, CB_SWA_K, qkv, CB_SWA_V, dmix, NA_HEADS,
                                            bias=swa_mask, sink=sv["sink"], name="swa_bwd")
        dq_c, dk_c, dv_c, dq2_c, dk2_c, got = _attn_bwd(
            cfg_mla, qmb, 0, kvm, 0, kvm, MLA_HEADS, dmix, NA_HEADS + SWA_HEADS, q2=qmb, q2_cb=MLA_HEADS, k2=qkv,
            k2_cb=CB_KPE, rides=[_Scatter(parts[("ffn", l)])], name="mla_bwd")
        landed[("ffn", l)] = got[0]
        rides = [_Scatter(parts[("rest", l + 1)])] if rest_above else []
        gqa_out = _attn_bwd(cfg_gqa, qkv, CB_GQA_Q, qkv, CB_GQA_K, qkv, CB_GQA_V, dmix,
                            NA_HEADS + SWA_HEADS + MLA_HEADS, rides=rides, name="gqa_bwd")
        dq_d, dk_d, dv_d = gqa_out[:3]
        if rest_above:
            landed[("rest", l + 1)] = gqa_out[3][0]
        dqm = _pe_rope(jnp.concatenate([dq_c, dq2_c], axis=1), tabs, S, -1.0, BF16, name="mla_q_rope_bwd")
        dkvm = jnp.concatenate([dk_c, dv_c], axis=1).astype(BF16)
        dcq = _mm(dqm, w_uq_p[l], mode="nt", name="mla_uq_dx")
        dckv = _mm(dkvm, w_ukv_p[l], mode="nt", name="mla_ukv_dx")
        cqn = qkv[:, CB_CQ * LANE:CB_CKV * LANE]
        ckvn = qkv[:, CB_CKV * LANE:(CB_CKV + 1) * LANE]
        d_uq = _unperm_w_uq(_mm(cqn, dqm, mode="tn", name="mla_uq_dw"))
        d_ukv = _unperm_w_ukv(_mm(ckvn, dkvm, mode="tn", name="mla_ukv_dw"))
        grads = {}
        for h in range(NA_HEADS):
            grads[CB_NA_Q + h], grads[CB_NA_K + h], grads[CB_NA_V + h] = (dq_a, h), (dk_a, h), (dv_a, h)
        for h in range(SWA_HEADS):
            grads[CB_SWA_Q + h] = (dq_b, h)
        for h in range(SWA_KV_HEADS):
            grads[CB_SWA_K + h], grads[CB_SWA_V + h] = (dk_b, h), (dv_b, h)
        for h in range(GQA_HEADS):
            grads[CB_GQA_Q + h] = (dq_d, h)
        for h in range(GQA_KV_HEADS):
            grads[CB_GQA_K + h], grads[CB_GQA_V + h] = (dk_d, h), (dv_d, h)
        grads[CB_KPE], grads[CB_CQ], grads[CB_CKV] = (dk2_c, 0), (dcq, 0), (dckv, 0)
        dp, acc_p = _prep_bwd(sv["p"], grads, tabs, gq, gk, mq, mkv, S, name="prep_bwd")
        dh1 = _mm(dp, w_in_p[l], name="in_proj_dx")
        d_in = _unperm_w_in(_mm(dp, sv["h1"], mode="tn", name="in_proj_dw")).reshape(4, IN_COLS // 4, D)
        dx, acc_m1 = _mod_bwd(sv["x"], dh1, md[1], dxa, S, name="mod_bwd")

        to4 = lambda g: g.reshape(g.shape[0], 4, g.shape[1] // 4).transpose(1, 0, 2)
        wgrad[l]["w_in"], wgrad[l]["mla_w_uq"], wgrad[l]["mla_w_ukv"] = d_in, to4(d_uq), to4(d_ukv)
        dmod[l] = jnp.stack([acc_m1[0:2], acc_m1[2:4], acc_r1[0:2], acc_m2[0:2], acc_m2[2:4], acc_r2[0:2]])
        rpb_vjp = jax.vjp(lambda r: _na_bias(r, S), w["na_rpb"][l])[1]
        small["na_rpb"][l] = rpb_vjp(dbias)[0]
        small["swa_sink"][l] = dsink[:, :, 0].reshape(SWA_KV_HEADS, -1, BQ).sum(axis=-1).reshape(-1)
        small["gqa_q_norm"][l], small["gqa_k_norm"][l] = acc_p[0, :LANE], acc_p[1, :LANE]
        small["mla_q_norm"][l], small["mla_kv_norm"][l] = acc_p[2], acc_p[3, :LANE]
        small["ln1_g"][l], small["ln1_b"][l] = acc_r1[2], acc_r1[3]
        small["ln2_g"][l], small["ln2_b"][l] = acc_r2[2], acc_r2[3]
        small["ffn_conv_w"][l], small["ffn_conv_b"][l] = acc_f[0:3], acc_f[3]
    grad_x = dx[:S][None]

    dmod_x = jnp.stack([dmod[l][:, 0].reshape(-1) for l in range(L)])
    dmod_c = jnp.stack([dmod[l][:, 1].reshape(-1) for l in range(L)])
    small_names = tuple(small)
    bucket = [dmod_x, dmod_c] + [jnp.stack(small[n]) for n in small_names]
    (b8,) = _allgather8([_pack(bucket)], name="gather_small")
    tot = _unpack(_sum_lead(b8, F32, name="sum_small"), bucket)
    dmod_x_all = b8.reshape(8, -1)[:, :dmod_x.size].reshape(8, L, 6 * D)
    dmod_c_tot = tot[1]
    g_small = dict(zip(small_names, tot[2:]))
    g_small["b_ada"] = tot[0] + dmod_c_tot
    g_small["ffn_conv_w"] = lax.dynamic_slice_in_dim(g_small["ffn_conv_w"], chip * (F // 4), F // 4, axis=2)

    dmod16 = jnp.concatenate([dmod_x_all, jnp.broadcast_to(dmod_c_tot[None], (8, L, 6 * D))], axis=0) * row_keep[:, :, None]
    dmod16 = lax.dynamic_slice_in_dim(dmod16, chip * n_ada, n_ada, axis=2)
    g_ada, dsc = None, None
    for l in range(L):
        g_ada = _mm(sc, dmod16[:, l], mode="tn", exact=True, stack=(L, l, g_ada), name="ada_dw")
        dsc = _mm(dmod16[:, l], w["w_ada"], b_layer=l, mode="nt", add=dsc, name="ada_dx" if dsc is None else "ada_dx_add")
    (dsc8,) = _allgather8([dsc[8:16]], name="gather_dsc")
    dsc4 = dsc8[::2, 0]
    g_small["c_ctx"] = (((dsc4[0] + dsc4[1]) + dsc4[2]) + dsc4[3]) * _silu_grad(w["c_ctx"])

    sums = {}

    def sum_group(group, l):
        for n, p, got in zip(groups[group], parts[(group, l)], landed[(group, l)]):
            sums[n] = _sum_parts(p, got, chip, ac, (L, l, sums.get(n)), name="rs_chip_sum")

    add_halves("rest", 0, _run_rides([_Exchange(halves_of("rest", 0))], name="rs_core_exchange")[0])
    for l in range(L):
        sum_group("ffn", l)
        if l > 0:
            sum_group("rest", l)
    grad, delta, new_m, new_v = {}, {}, {}, {}
    (delta["w_ada"], new_m["w_ada"], new_v["w_ada"]), got = _adam(
        w["w_ada"], g_ada, m_in["w_ada"], v_in["w_ada"],
        rides=[_Scatter(parts[("rest", 0)]), _Join([sums[n] for n in groups["ffn"]])], name="adam")
    landed[("rest", 0)] = got[0]
    joined = dict(zip(groups["ffn"], got[1]))
    sum_group("rest", 0)
    joined.update(zip(groups["rest"], _run_rides([_Join([sums[n] for n in groups["rest"]])], name="rs_join")[0]))
    g_big = {n: j.reshape(L, 2 * j.shape[2], j.shape[3]) for n, j in joined.items()}
    g_big["w_ada"] = g_ada

    grad["w_ada"] = g_ada
    for n in _BIG:
        delta[n], new_m[n], new_v[n], grad[n] = _adam(w[n], g_big[n], m_in[n], v_in[n], emit_grad=True, name="adam")
    like = [w[n] for n in _SMALL]
    packed = [_pack([src[n].reshape(w[n].shape) for n in _SMALL]) for src in (w, g_small, m_in, v_in)]
    d_s, m_s, v_s = _adam(*packed, name="adam_small")
    for n, g_, d_, m_, v_ in zip(_SMALL, _unpack(packed[1], like), _unpack(d_s, like), _unpack(m_s, like), _unpack(v_s, like)):
        grad[n], delta[n], new_m[n], new_v[n] = g_, d_, m_, v_

    for d in (grad, delta, new_m, new_v):
        d["w_in"] = jnp.swapaxes(d["w_in"], 1, 2)
    return (loss, grad_x, *[grad[n] for n in _WEIGHTS], *[delta[n] for n in _WEIGHTS],
            *[new_m[n] for n in _WEIGHTS], *[new_v[n] for n in _WEIGHTS])


def kernel(x, c, ctx, c_ctx, w_ada, b_ada, w_in, na_rpb, swa_sink, mla_q_norm, mla_kv_norm, mla_w_uq, mla_w_ukv, gqa_q_norm, gqa_k_norm, w_out, ln1_g, ln1_b, ffn_w_gate, ffn_w_up, ffn_conv_w, ffn_conv_b, ffn_w_down, ln2_g, ln2_b, loss_target, m_c_ctx, m_w_ada, m_b_ada, m_w_in, m_na_rpb, m_swa_sink, m_mla_q_norm, m_mla_kv_norm, m_mla_w_uq, m_mla_w_ukv, m_gqa_q_norm, m_gqa_k_norm, m_w_out, m_ln1_g, m_ln1_b, m_ffn_w_gate, m_ffn_w_up, m_ffn_conv_w, m_ffn_conv_b, m_ffn_w_down, m_ln2_g, m_ln2_b, v_c_ctx, v_w_ada, v_b_ada, v_w_in, v_na_rpb, v_swa_sink, v_mla_q_norm, v_mla_kv_norm, v_mla_w_uq, v_mla_w_ukv, v_gqa_q_norm, v_gqa_k_norm, v_w_out, v_ln1_g, v_ln1_b, v_ffn_w_gate, v_ffn_w_up, v_ffn_conv_w, v_ffn_conv_b, v_ffn_w_down, v_ln2_g, v_ln2_b):
    args = locals()
    w = {n: args[n] for n in _WEIGHTS}
    m_in = {n: args["m_" + n] for n in _WEIGHTS}
    v_in = {n: args["v_" + n] for n in _WEIGHTS}
    return _train_step(x, c, ctx, loss_target, w, m_in, v_in)
```

```python
import functools
import math

import numpy as np
import jax
import jax.numpy as jnp
from jax import lax
from jax.experimental import pallas as pl
from jax.experimental.pallas import tpu as pltpu

F32 = jnp.float32
BF16 = jnp.bfloat16
MESH = pl.DeviceIdType.MESH

GRID_W = 64
HEAD_DIM = 128
NA_HEADS, NA_WIN_R, NA_WIN_C = 4, 8, 16
SWA_HEADS, SWA_KV_HEADS, SWA_WINDOW = 4, 2, 128
MLA_HEADS, MLA_Q_LORA, MLA_KV_LORA, MLA_NOPE, MLA_ROPE, MLA_V = 4, 384, 128, 128, 64, 128
GQA_HEADS, GQA_KV_HEADS = 4, 2
ROPE_THETA = 10000.0
EPS = 1e-6
NEG = -1e30
DEPTH = 2
DEEPNORM_ALPHA = (2 * DEPTH) ** 0.25
ADAM_LR, ADAM_B1, ADAM_B2, ADAM_EPS, ADAM_WD, ADAM_STEP = 0.001, 0.9, 0.999, 1e-08, 0.01, 10

LANE = 128
V7X_VMEM_BYTES = 64 * 1024 * 1024
VMEM_LIMIT = 56 * 1024 * 1024
MM_VMEM_BUDGET = 40 * 1024 * 1024
EW_VMEM_BUDGET = 28 * 1024 * 1024
BQ = 128

CB_NA_Q, CB_NA_K, CB_NA_V = 0, 4, 8
CB_SWA_Q, CB_SWA_K, CB_SWA_V = 12, 16, 18
CB_CQ, CB_CKV = 20, 23
CB_GQA_Q, CB_GQA_K, CB_GQA_V = 24, 28, 30
CB_KPE = 32
PCOLS = 33 * LANE
IN_COLS = 4160


def _cparams(sem=None, **kw):
    return pltpu.CompilerParams(dimension_semantics=sem, vmem_limit_bytes=VMEM_LIMIT, **kw)


def _pick(n, target, mult=LANE):
    best = None
    for d in range(mult, min(n, target) + 1, mult):
        if n % d == 0:
            best = d
    return n if best is None else best


def _mm(a, b, *, mode="nn", out_dtype=F32, a_off=0, a_k=None, tm=1408, tn=1408, tk=2816, exact=False, add=None,
        stack=None, split4=False, rows=None, rides=(), b_layer=None, name):
    b_shape = b.shape if b_layer is None else b.shape[1:]
    if mode == "tn":
        K, M = a.shape
        K2, N = b_shape
    elif mode == "nn":
        M, K = a.shape
        K2, N = b_shape
    else:
        M, K = a.shape
        N, K2 = b_shape
    if a_k is not None:
        K = a_k
    if rows is not None:
        if mode == "tn":
            assert rows <= min(K, K2)
            K = K2 = rows
        else:
            assert rows <= M
            M = rows
    assert K == K2, (a.shape, b.shape, mode)
    m_mult = LANE if mode == "tn" else 16
    n_cols = N // 4 if split4 else N
    bm, bn, bk = _pick(M, tm, m_mult), _pick(n_cols, tn), _pick(K, tk)
    sa, sb, so = a.dtype.itemsize, b.dtype.itemsize, jnp.dtype(out_dtype).itemsize

    def vmem_estimate():
        acc = bm * bn * 4 if K // bk > 1 else 0
        return 2 * (bm * bk * sa + bk * bn * sb) + acc + 2 * bm * bn * so + (2 * bm * bn * 4 if add is not None else 0)

    while vmem_estimate() > MM_VMEM_BUDGET:
        if bm >= bn and _pick(M, bm - 1, m_mult) < bm:
            bm = _pick(M, bm - 1, m_mult)
        elif _pick(n_cols, bn - 1) < bn:
            bn = _pick(n_cols, bn - 1)
        else:
            assert _pick(K, bk - 1) < bk, "no tiling fits VMEM"
            bk = _pick(K, bk - 1)
    assert a_off % bk == 0
    koff = a_off // bk
    nk = K // bk
    if mode == "tn":
        a_spec = pl.BlockSpec((bk, bm), lambda i, j, k: (k, i))
        b_spec = pl.BlockSpec((bk, bn), lambda i, j, k: (k, j))
        dims = (((0,), (0,)), ((), ()))
    elif mode == "nn":
        a_spec = pl.BlockSpec((bm, bk), lambda i, j, k: (i, k + koff))
        b_spec = pl.BlockSpec((bk, bn), lambda i, j, k: (k, j))
        dims = (((1,), (0,)), ((), ()))
    else:
        a_spec = pl.BlockSpec((bm, bk), lambda i, j, k: (i, k + koff))
        b_spec = pl.BlockSpec((bn, bk), lambda i, j, k: (j, k))
        dims = (((1,), (1,)), ((), ()))
    if b_layer is not None:
        b_block, b_index = b_spec.block_shape, b_spec.index_map
        b_spec = pl.BlockSpec((None,) + tuple(b_block), lambda i, j, k: (b_layer,) + tuple(b_index(i, j, k)))

    operands = [a, b]
    in_specs = [a_spec, b_spec]
    if add is not None:
        operands.append(add)
        in_specs.append(pl.BlockSpec((bm, bn), lambda i, j, k: (i, j)))
    aliases = {}
    if stack is None:
        out_spec = pl.BlockSpec((bm, bn), lambda i, j, k: (i, j))
        out_shape = jax.ShapeDtypeStruct((M, N), out_dtype)
    else:
        n_layers, layer, buf = stack
        if split4:
            nb = N // 4 // bn
            assert N % (4 * bn) == 0
            out_spec = pl.BlockSpec((None, None, bm, bn), lambda i, j, k: (layer, j // nb, i, j % nb))
            out_shape = jax.ShapeDtypeStruct((n_layers, 4, M, N // 4), out_dtype)
        else:
            out_spec = pl.BlockSpec((None, bm, bn), lambda i, j, k: (layer, i, j))
            out_shape = jax.ShapeDtypeStruct((n_layers, M, N), out_dtype)
        if buf is not None:
            aliases = {len(operands): 0}
            operands.append(buf)
            in_specs.append(pl.BlockSpec(memory_space=pl.ANY))
    has_add, has_buf = add is not None, bool(aliases)
    n_ride = sum(len(r.arrays) for r in rides)
    aliases.update(_ride_aliases(rides, len(operands), 1))
    grid = (M // bm, N // bn, nk)

    def body(*refs):
        a_ref, b_ref = refs[:2]
        add_ref = refs[2] if has_add else None
        base = 2 + has_add + has_buf
        o_ref = refs[base + n_ride]
        scratch = refs[base + 2 * n_ride + 1:]
        if rides:
            ride_groups = _ride_split(rides, refs[base:base + n_ride], refs[base + n_ride + 1:base + 2 * n_ride + 1],
                                      scratch[1 if nk > 1 else 0:])
            steps = [pl.program_id(d) for d in range(3)]
            pl.when((steps[0] == 0) & (steps[1] == 0) & (steps[2] == 0))(lambda: _ride_start(rides, ride_groups))
        compute(a_ref, b_ref, add_ref, o_ref, scratch[0] if nk > 1 else None)
        if rides:
            pl.when((steps[0] == grid[0] - 1) & (steps[1] == grid[1] - 1) & (steps[2] == grid[2] - 1))(
                lambda: _ride_wait(rides, ride_groups))

    def compute(a_ref, b_ref, add_ref, o_ref, acc_ref):
        if exact:
            prod = lax.dot_general(a_ref[...].astype(F32), b_ref[...].astype(F32), dims,
                                   precision=lax.Precision.HIGHEST, preferred_element_type=F32)
        else:
            prod = lax.dot_general(a_ref[...].astype(BF16), b_ref[...].astype(BF16), dims, preferred_element_type=F32)

        def finish(res):
            if has_add:
                res = res + add_ref[...].astype(F32)
            o_ref[...] = res.astype(o_ref.dtype)

        if nk == 1:
            finish(prod)
            return
        k = pl.program_id(2)

        @pl.when(k == 0)
        def _():
            acc_ref[...] = prod

        @pl.when((k > 0) & (k < nk - 1))
        def _():
            acc_ref[...] += prod

        @pl.when(k == nk - 1)
        def _():
            finish(acc_ref[...] + prod)

    outs = pl.pallas_call(
        body, name=name, grid=grid,
        in_specs=in_specs + [_ANY] * n_ride, out_specs=[out_spec] + [_ANY] * n_ride,
        out_shape=[out_shape] + [s for r in rides for s in r.out_shapes()],
        scratch_shapes=([pltpu.VMEM((bm, bn), F32)] if nk > 1 else []) + _ride_scratch(rides),
        input_output_aliases=aliases,
        compiler_params=_cparams(("arbitrary",) * 3 if rides else ("parallel", "parallel", "arbitrary")),
    )(*operands, *[a for r in rides for a in r.arrays])
    if not rides:
        return outs[0]
    return outs[0], _ride_outputs(rides, outs[1:])


def _row_block(T, S):
    return _pick(math.gcd(T, S), 256, 16)


def _ln_stats(x):
    mu = jnp.mean(x, axis=-1, keepdims=True)
    xc = x - mu
    var = jnp.mean(xc * xc, axis=-1, keepdims=True)
    rstd = lax.rsqrt(var + EPS)
    return xc * rstd, rstd


def _ln_bwd(dxhat, xhat, rstd):
    m1 = jnp.mean(dxhat, axis=-1, keepdims=True)
    m2 = jnp.mean(dxhat * xhat, axis=-1, keepdims=True)
    return rstd * (dxhat - m1 - xhat * m2)


def _sel(ref, is_ctx):
    return jnp.where(is_ctx, ref[1:2, :], ref[0:1, :])


def _mod_fwd(x, shift, scale, S, *, rows=None, name):
    T, D = (x.shape[0] if rows is None else rows), x.shape[1]
    bt = _row_block(T, S)

    def body(x_ref, sh_ref, sc_ref, o_ref):
        is_ctx = pl.program_id(0) * bt >= S
        xhat, _ = _ln_stats(x_ref[...])
        o_ref[...] = (xhat * (1.0 + _sel(sc_ref, is_ctx)) + _sel(sh_ref, is_ctx)).astype(o_ref.dtype)

    return pl.pallas_call(
        body, name=name, grid=(T // bt,),
        in_specs=[pl.BlockSpec((bt, D), lambda i: (i, 0)), pl.BlockSpec((2, D), lambda i: (0, 0)),
                  pl.BlockSpec((2, D), lambda i: (0, 0))],
        out_specs=pl.BlockSpec((bt, D), lambda i: (i, 0)),
        out_shape=jax.ShapeDtypeStruct((T, D), BF16),
        compiler_params=_cparams(("parallel",)),
    )(x, shift, scale)


def _acc_groups(acc_ref, row, val, is_ctx):
    f = jnp.where(is_ctx, 1.0, 0.0).astype(F32)
    acc_ref[row:row + 1, :] += val * (1.0 - f)
    acc_ref[row + 1:row + 2, :] += val * f


def _mod_bwd(x, dh, scale, dx_in, S, *, name):
    T, D = dh.shape
    bt = _row_block(T, S)
    in_blocks = dx_in.shape[0] // bt

    def body(x_ref, dh_ref, sc_ref, dxi_ref, dx_ref, acc_ref):
        i = pl.program_id(0)
        is_ctx = i * bt >= S

        @pl.when(i == 0)
        def _():
            acc_ref[...] = jnp.zeros_like(acc_ref)

        xhat, rstd = _ln_stats(x_ref[...])
        dh = dh_ref[...].astype(F32)
        dxhat = dh * (1.0 + _sel(sc_ref, is_ctx))
        dxi = dxi_ref[...] if in_blocks * bt == T else jnp.where(i < in_blocks, dxi_ref[...], 0.0)
        dx_ref[...] = dxi + _ln_bwd(dxhat, xhat, rstd)
        _acc_groups(acc_ref, 0, jnp.sum(dh, axis=0, keepdims=True), is_ctx)
        _acc_groups(acc_ref, 2, jnp.sum(dh * xhat, axis=0, keepdims=True), is_ctx)

    return pl.pallas_call(
        body, name=name, grid=(T // bt,),
        in_specs=[pl.BlockSpec((bt, D), lambda i: (i, 0)), pl.BlockSpec((bt, D), lambda i: (i, 0)),
                  pl.BlockSpec((2, D), lambda i: (0, 0)),
                  pl.BlockSpec((bt, D), lambda i: (jnp.minimum(i, in_blocks - 1), 0))],
        out_specs=[pl.BlockSpec((bt, D), lambda i: (i, 0)), pl.BlockSpec((8, D), lambda i: (0, 0))],
        out_shape=[jax.ShapeDtypeStruct((T, D), F32), jax.ShapeDtypeStruct((8, D), F32)],
        compiler_params=_cparams(("arbitrary",)),
    )(x, dh, scale, dx_in)


def _res_fwd(x, z, gate, lg, lb, S, *, name):
    T, D = z.shape
    bt = _row_block(T, S)

    def body(x_ref, z_ref, g_ref, lg_ref, lb_ref, o_ref):
        is_ctx = pl.program_id(0) * bt >= S
        u = DEEPNORM_ALPHA * x_ref[...] + _sel(g_ref, is_ctx) * z_ref[...]
        uhat, _ = _ln_stats(u)
        o_ref[...] = uhat * lg_ref[...] + lb_ref[...]

    row = pl.BlockSpec((bt, D), lambda i: (i, 0))
    return pl.pallas_call(
        body, name=name, grid=(T // bt,),
        in_specs=[row, row, pl.BlockSpec((2, D), lambda i: (0, 0)), pl.BlockSpec((1, D), lambda i: (0, 0)),
                  pl.BlockSpec((1, D), lambda i: (0, 0))],
        out_specs=row,
        out_shape=jax.ShapeDtypeStruct((T, D), F32),
        compiler_params=_cparams(("parallel",)),
    )(x, z, gate, lg, lb)


def _res_bwd(x, z, gate, lg, dy, S, *, name):
    T, D = z.shape
    bt = _row_block(T, S)

    def body(x_ref, z_ref, g_ref, lg_ref, dy_ref, dx_ref, dz_ref, acc_ref):
        i = pl.program_id(0)
        is_ctx = i * bt >= S

        @pl.when(i == 0)
        def _():
            acc_ref[...] = jnp.zeros_like(acc_ref)

        gate_v = _sel(g_ref, is_ctx)
        zv = z_ref[...]
        u = DEEPNORM_ALPHA * x_ref[...] + gate_v * zv
        uhat, rstd = _ln_stats(u)
        dyv = dy_ref[...]
        du = _ln_bwd(dyv * lg_ref[...], uhat, rstd)
        dx_ref[...] = DEEPNORM_ALPHA * du
        dz_ref[...] = (gate_v * du).astype(dz_ref.dtype)
        _acc_groups(acc_ref, 0, jnp.sum(du * zv, axis=0, keepdims=True), is_ctx)
        acc_ref[2:3, :] += jnp.sum(dyv * uhat, axis=0, keepdims=True)
        acc_ref[3:4, :] += jnp.sum(dyv, axis=0, keepdims=True)

    row = pl.BlockSpec((bt, D), lambda i: (i, 0))
    return pl.pallas_call(
        body, name=name, grid=(T // bt,),
        in_specs=[row, row, pl.BlockSpec((2, D), lambda i: (0, 0)), pl.BlockSpec((1, D), lambda i: (0, 0)), row],
        out_specs=[row, row, pl.BlockSpec((8, D), lambda i: (0, 0))],
        out_shape=[jax.ShapeDtypeStruct((T, D), F32), jax.ShapeDtypeStruct((T, D), BF16),
                   jax.ShapeDtypeStruct((8, D), F32)],
        compiler_params=_cparams(("arbitrary",)),
    )(x, z, gate, lg, dy)


def _loss_fwd_bwd(y, target, S, *, name):
    T, D = y.shape
    bt = _row_block(T, S)
    n_lat = S // bt

    def body(y_ref, t_ref, dy_ref, l_ref):
        i = pl.program_id(0)

        @pl.when(i == 0)
        def _():
            l_ref[...] = jnp.zeros_like(l_ref)

        keep = jnp.where(i * bt >= S, 0.0, 1.0).astype(F32)
        err = (y_ref[...] - t_ref[...]) * keep
        dy_ref[...] = err * (1.0 / D)
        l_ref[...] += jnp.sum(err * err) * (0.5 / D)

    return pl.pallas_call(
        body, name=name, grid=(T // bt,),
        in_specs=[pl.BlockSpec((bt, D), lambda i: (i, 0)),
                  pl.BlockSpec((bt, D), lambda i: (jnp.minimum(i, n_lat - 1), 0))],
        out_specs=[pl.BlockSpec((bt, D), lambda i: (i, 0)), pl.BlockSpec((8, LANE), lambda i: (0, 0))],
        out_shape=[jax.ShapeDtypeStruct((T, D), F32), jax.ShapeDtypeStruct((8, LANE), F32)],
        compiler_params=_cparams(("arbitrary",)),
    )(y, target)


def _rope_tables(S, C, dim):
    half = dim // 4
    t = jnp.arange(S)
    row = (t // GRID_W).astype(F32)
    col = (t % GRID_W).astype(F32)
    inv = ROPE_THETA ** (-jnp.arange(half, dtype=F32) / half)
    ar, ac = row[:, None] * inv[None, :], col[:, None] * inv[None, :]
    cos = jnp.concatenate([jnp.cos(ar), jnp.cos(ar), jnp.cos(ac), jnp.cos(ac)], axis=1)
    ss = jnp.concatenate([-jnp.sin(ar), jnp.sin(ar), -jnp.sin(ac), jnp.sin(ac)], axis=1)
    cos = jnp.pad(cos, ((0, C), (0, LANE - dim)), constant_values=1.0)
    ss = jnp.pad(ss, ((0, C), (0, LANE - dim)))
    return cos, ss


def _rope(x, cos, ss, half):
    lane = lax.broadcasted_iota(jnp.int32, x.shape, 1)
    first = (lane % (2 * half)) < half
    partner = jnp.where(first, pltpu.roll(x, LANE - half, 1), pltpu.roll(x, half, 1))
    return x * cos + partner * ss


def _rms(x):
    r = lax.rsqrt(jnp.mean(x * x, axis=-1, keepdims=True) + EPS)
    return x * r, r


_CAST_BLOCKS = tuple(range(0, 12)) + (18, 19, 30, 31)
_ROPE_BLOCKS = tuple(range(12, 18))
_GQA_Q_BLOCKS = tuple(range(24, 28))
_GQA_K_BLOCKS = (28, 29)


def _prep_fwd(p, tabs, gq, gk, mq, mkv, S, *, name):
    T = p.shape[0]
    bt = _row_block(T, S)
    cA, sA, cP, sP = tabs

    def body(p_ref, cA_ref, sA_ref, cP_ref, sP_ref, gq_ref, gk_ref, mq_ref, mkv_ref, o_ref):
        def blk(b):
            return p_ref[:, b * LANE:(b + 1) * LANE].astype(F32)

        def put(b, val):
            o_ref[:, b * LANE:(b + 1) * LANE] = val.astype(o_ref.dtype)

        cA_v, sA_v = cA_ref[...], sA_ref[...]
        for b in _CAST_BLOCKS:
            put(b, blk(b))
        for b in _ROPE_BLOCKS:
            put(b, _rope(blk(b), cA_v, sA_v, 32))
        for b in _GQA_Q_BLOCKS:
            put(b, _rope(_rms(blk(b))[0] * gq_ref[...], cA_v, sA_v, 32))
        for b in _GQA_K_BLOCKS:
            put(b, _rope(_rms(blk(b))[0] * gk_ref[...], cA_v, sA_v, 32))
        put(CB_KPE, _rope(blk(CB_KPE), cP_ref[...], sP_ref[...], 16))
        cq = p_ref[:, CB_CQ * LANE:CB_CKV * LANE].astype(F32)
        o_ref[:, CB_CQ * LANE:CB_CKV * LANE] = (_rms(cq)[0] * mq_ref[...]).astype(o_ref.dtype)
        put(CB_CKV, _rms(blk(CB_CKV))[0] * mkv_ref[...])

    row128 = pl.BlockSpec((bt, LANE), lambda i: (i, 0))
    vec = lambda n: pl.BlockSpec((1, n), lambda i: (0, 0))
    return pl.pallas_call(
        body, name=name, grid=(T // bt,),
        in_specs=[pl.BlockSpec((bt, PCOLS), lambda i: (i, 0)), row128, row128, row128, row128,
                  vec(LANE), vec(LANE), vec(MLA_Q_LORA), vec(LANE)],
        out_specs=pl.BlockSpec((bt, PCOLS), lambda i: (i, 0)),
        out_shape=jax.ShapeDtypeStruct((T, PCOLS), BF16),
        compiler_params=_cparams(("parallel",)),
    )(p, cA, sA, cP, sP, gq, gk, mq, mkv)


def _prep_bwd(p, grads, tabs, gq, gk, mq, mkv, S, *, name):
    T = p.shape[0]
    bt = _row_block(T, S)
    cA, sA, cP, sP = tabs
    arrays = []
    where = {}
    for key, (arr, cb) in grads.items():
        idx = next((n for n, a in enumerate(arrays) if a is arr), None)
        if idx is None:
            arrays.append(arr)
            idx = len(arrays) - 1
        where[key] = (idx, cb)
    ng = len(arrays)

    def body(*refs):
        p_ref, cA_ref, sA_ref, cP_ref, sP_ref, gq_ref, gk_ref, mq_ref, mkv_ref = refs[:9]
        g_refs = refs[9:9 + ng]
        o_ref, acc_ref = refs[9 + ng:]
        i = pl.program_id(0)

        @pl.when(i == 0)
        def _():
            acc_ref[...] = jnp.zeros_like(acc_ref)

        def blk(b):
            return p_ref[:, b * LANE:(b + 1) * LANE].astype(F32)

        def grad(b, width=LANE):
            idx, cb = where[b]
            return g_refs[idx][:, cb * LANE:cb * LANE + width].astype(F32)

        def put(b, val):
            o_ref[:, b * LANE:(b + 1) * LANE] = val.astype(o_ref.dtype)

        def rms_bwd(x, dy, g, row, width):
            n, r = _rms(x)
            acc_ref[row:row + 1, 0:width] += jnp.sum(dy * n, axis=0, keepdims=True)
            dn = dy * g
            return r * (dn - n * jnp.mean(dn * n, axis=-1, keepdims=True))

        cA_v, sA_v = cA_ref[...], sA_ref[...]
        for b in _CAST_BLOCKS:
            put(b, grad(b))
        for b in _ROPE_BLOCKS:
            put(b, _rope(grad(b), cA_v, -sA_v, 32))
        for b in _GQA_Q_BLOCKS:
            put(b, rms_bwd(blk(b), _rope(grad(b), cA_v, -sA_v, 32), gq_ref[...], 0, LANE))
        for b in _GQA_K_BLOCKS:
            put(b, rms_bwd(blk(b), _rope(grad(b), cA_v, -sA_v, 32), gk_ref[...], 1, LANE))
        put(CB_KPE, _rope(grad(CB_KPE), cP_ref[...], -sP_ref[...], 16))
        dcq = rms_bwd(p_ref[:, CB_CQ * LANE:CB_CKV * LANE].astype(F32), grad(CB_CQ, MLA_Q_LORA), mq_ref[...], 2, MLA_Q_LORA)
        o_ref[:, CB_CQ * LANE:CB_CKV * LANE] = dcq.astype(o_ref.dtype)
        put(CB_CKV, rms_bwd(blk(CB_CKV), grad(CB_CKV), mkv_ref[...], 3, LANE))

    row128 = pl.BlockSpec((bt, LANE), lambda i: (i, 0))
    vec = lambda n: pl.BlockSpec((1, n), lambda i: (0, 0))
    g_specs = [pl.BlockSpec((bt, a.shape[1]), lambda i: (i, 0)) for a in arrays]
    return pl.pallas_call(
        body, name=name, grid=(T // bt,),
        in_specs=[pl.BlockSpec((bt, PCOLS), lambda i: (i, 0)), row128, row128, row128, row128,
                  vec(LANE), vec(LANE), vec(MLA_Q_LORA), vec(LANE)] + g_specs,
        out_specs=[pl.BlockSpec((bt, PCOLS), lambda i: (i, 0)), pl.BlockSpec((8, MLA_Q_LORA), lambda i: (0, 0))],
        out_shape=[jax.ShapeDtypeStruct((T, PCOLS), BF16), jax.ShapeDtypeStruct((8, MLA_Q_LORA), F32)],
        compiler_params=_cparams(("arbitrary",)),
    )(p, cA, sA, cP, sP, gq, gk, mq, mkv, *arrays)


def _pe_rope(qm, tabs, S, sign, out_dtype, *, name):
    T, N = qm.shape
    bt = _row_block(T, S)
    cP, sP = tabs[2], tabs[3]

    def body(x_ref, c_ref, s_ref, o_ref):
        for b in range(MLA_HEADS):
            o_ref[:, b * LANE:(b + 1) * LANE] = x_ref[:, b * LANE:(b + 1) * LANE].astype(o_ref.dtype)
        for b in range(MLA_HEADS, 2 * MLA_HEADS):
            x = x_ref[:, b * LANE:(b + 1) * LANE].astype(F32)
            o_ref[:, b * LANE:(b + 1) * LANE] = _rope(x, c_ref[...], sign * s_ref[...], 16).astype(o_ref.dtype)

    row128 = pl.BlockSpec((bt, LANE), lambda i: (i, 0))
    return pl.pallas_call(
        body, name=name, grid=(T // bt,),
        in_specs=[pl.BlockSpec((bt, N), lambda i: (i, 0)), row128, row128],
        out_specs=pl.BlockSpec((bt, N), lambda i: (i, 0)),
        out_shape=jax.ShapeDtypeStruct((T, N), out_dtype),
        compiler_params=_cparams(("parallel",)),
    )(qm, cP, sP)


def _dot_nt(a, b):
    return lax.dot_general(a, b, (((1,), (1,)), ((), ())), preferred_element_type=F32)


def _dot_tn(a, b):
    return lax.dot_general(a, b, (((0,), (0,)), ((), ())), preferred_element_type=F32)


def _dot(a, b):
    return jnp.dot(a, b, preferred_element_type=F32)


def _window_fns(band, S, n_var):
    n_lat = S // BQ
    if band is None:
        return None
    reach, span = band

    def fns(j):
        start = jnp.clip(j - reach, 0, n_lat - span)
        return start, jnp.clip(j - start, 0, n_var - 1)

    return fns


class _AttnCfg:
    def __init__(self, *, Hkv, G, S, C, band, scale, n_var=0, bias_per_head=False, has_sink=False, two=False, bq=BQ,
                 ctx_queries=True):
        self.Hkv, self.G, self.S, self.C, self.band, self.scale = Hkv, G, S, C, band, scale
        self.n_var, self.bias_per_head, self.has_sink, self.two = n_var, bias_per_head, has_sink, two
        self.W = S if band is None else band[1] * BQ
        self.T = S + C
        self.bq, self.ctx_queries = bq, ctx_queries
        assert band is None or bq == BQ
        assert S % bq == 0 and C % bq == 0


def _attn_probs(cfg, j, q_ref, k_ref, q2_ref, k2_ref, bias_ref, sink_ref):
    G, S, C, W = cfg.G, cfg.S, cfg.C, cfg.W
    is_ctx = j * cfg.bq >= S
    if cfg.band is None:
        off, var = 0, 0
    else:
        start, var = _window_fns(cfg.band, S, cfg.n_var)(j)
        off = pl.multiple_of(start * BQ, BQ)
    qt = q_ref[...]
    qs = jnp.concatenate([qt[:, g * LANE:(g + 1) * LANE] for g in range(G)], axis=0) if G > 1 else qt
    kw = k_ref[pl.ds(off, W), :]
    kc = k_ref[pl.ds(S, C), :]
    s_w = _dot_nt(qs, kw)
    s_c = _dot_nt(qs, kc)
    q2s = k2w = k2c = None
    if cfg.two:
        q2s = q2_ref[...]
        k2w = k2_ref[pl.ds(off, W), :]
        k2c = k2_ref[pl.ds(S, C), :]
        s_w = s_w + _dot_nt(q2s, k2w)
        s_c = s_c + _dot_nt(q2s, k2c)
    operands = (off, var, qs, kw, kc, q2s, k2w, k2c)
    if not cfg.n_var and not cfg.has_sink:
        if cfg.ctx_queries:
            s_w = jnp.where(is_ctx, NEG, s_w)
        m = jnp.maximum(jnp.max(s_w, axis=-1, keepdims=True), jnp.max(s_c, axis=-1, keepdims=True))
        c2 = cfg.scale * math.log2(math.e)
        e_w = jnp.exp2((s_w - m) * c2)
        e_c = jnp.exp2((s_c - m) * c2)
        inv = 1.0 / (jnp.sum(e_w, axis=-1, keepdims=True) + jnp.sum(e_c, axis=-1, keepdims=True))
        return e_w * inv, e_c * inv, None, operands
    s_w = s_w * cfg.scale
    s_c = s_c * cfg.scale
    if cfg.n_var:
        b = bias_ref[0, pl.ds(var, 1)][0]
        s_w = s_w + (jnp.concatenate([b] * G, axis=0) if G > 1 else b)
    if cfg.ctx_queries:
        s_w = jnp.where(is_ctx, NEG, s_w)
    m = jnp.maximum(jnp.max(s_w, axis=-1, keepdims=True), jnp.max(s_c, axis=-1, keepdims=True))
    if cfg.has_sink:
        sink = sink_ref[0][:, 0:1]
        m = jnp.maximum(m, sink)
    e_w = jnp.exp(s_w - m)
    e_c = jnp.exp(s_c - m)
    l = jnp.sum(e_w, axis=-1, keepdims=True) + jnp.sum(e_c, axis=-1, keepdims=True)
    p_s = None
    if cfg.has_sink:
        e_s = jnp.exp(sink - m)
        l = l + e_s
    inv = 1.0 / l
    if cfg.has_sink:
        p_s = e_s * inv
    return e_w * inv, e_c * inv, p_s, operands


def _attn_specs(cfg, q_cb, k_cb, v_cb, q2_cb, k2_cb):
    G, T, bq = cfg.G, cfg.T, cfg.bq
    specs = [pl.BlockSpec((bq, G * LANE), lambda h, j: (j, q_cb // G + h)),
             pl.BlockSpec((T, LANE), lambda h, j: (0, k_cb + h)),
             pl.BlockSpec((T, LANE), lambda h, j: (0, v_cb + h))]
    if cfg.two:
        specs += [pl.BlockSpec((bq, LANE), lambda h, j: (j, q2_cb + h)),
                  pl.BlockSpec((T, LANE), lambda h, j: (0, k2_cb))]
    if cfg.n_var:
        if cfg.bias_per_head:
            specs.append(pl.BlockSpec((1, cfg.n_var, BQ, cfg.W), lambda h, j: (h, 0, 0, 0)))
        else:
            specs.append(pl.BlockSpec((1, cfg.n_var, BQ, cfg.W), lambda h, j: (0, 0, 0, 0)))
    if cfg.has_sink:
        specs.append(pl.BlockSpec((1, G * BQ, LANE), lambda h, j: (h, 0, 0)))
    return specs


def _attn_unpack(cfg, refs):
    refs = list(refs)
    q_ref, k_ref, v_ref = refs[:3]
    n = 3
    q2_ref = k2_ref = bias_ref = sink_ref = None
    if cfg.two:
        q2_ref, k2_ref = refs[n:n + 2]
        n += 2
    if cfg.n_var:
        bias_ref = refs[n]
        n += 1
    if cfg.has_sink:
        sink_ref = refs[n]
        n += 1
    return (q_ref, k_ref, v_ref, q2_ref, k2_ref, bias_ref, sink_ref), refs[n:]


def _attn_fwd(cfg, q, q_cb, k, k_cb, v, v_cb, *, q2=None, q2_cb=0, k2=None, k2_cb=0, bias=None, sink=None, rides=(),
              into=None, name):
    G, T, S, C, W = cfg.G, cfg.T, cfg.S, cfg.C, cfg.W
    assert q_cb % G == 0
    operands = [q, k, v] + ([q2, k2] if cfg.two else []) + ([bias] if cfg.n_var else []) + ([sink] if cfg.has_sink else [])

    bq = cfg.bq
    n_ride = sum(len(r.arrays) for r in rides)
    n_q = T // bq
    has_buf = into is not None and into[0] is not None
    col0 = 0 if into is None else into[1]
    width = cfg.Hkv * G * LANE if into is None else into[2] * LANE
    assert col0 % G == 0

    def body(*refs):
        (q_ref, k_ref, v_ref, q2_ref, k2_ref, bias_ref, sink_ref), rest = _attn_unpack(cfg, refs)
        rest = rest[:n_ride] + rest[n_ride + has_buf:]
        o_ref = rest[n_ride]
        ride_groups = _ride_split(rides, rest[:n_ride], rest[n_ride + 1:2 * n_ride + 1], rest[2 * n_ride + 1:])
        h = pl.program_id(0)
        j = pl.program_id(1)
        if rides:
            pl.when((h == 0) & (j == 0))(lambda: _ride_start(rides, ride_groups))

        def block():
            p_w, p_c, _, (off, _, _, _, _, _, _, _) = _attn_probs(cfg, j, q_ref, k_ref, q2_ref, k2_ref, bias_ref, sink_ref)
            o = _dot(p_w.astype(BF16), v_ref[pl.ds(off, W), :]) + _dot(p_c.astype(BF16), v_ref[pl.ds(S, C), :])
            for g in range(G):
                o_ref[:, g * LANE:(g + 1) * LANE] = o[g * bq:(g + 1) * bq].astype(o_ref.dtype)

        if cfg.ctx_queries:
            block()
        else:
            pl.when(j * bq < S)(block)

            @pl.when(j * bq >= S)
            def _():
                o_ref[...] = jnp.zeros_like(o_ref)

        if rides:
            pl.when((h == cfg.Hkv - 1) & (j == n_q - 1))(lambda: _ride_wait(rides, ride_groups))

    aliases = _ride_aliases(rides, len(operands), 1)
    if has_buf:
        aliases[len(operands) + n_ride] = 0
    outs = pl.pallas_call(
        body, name=name, grid=(cfg.Hkv, n_q),
        in_specs=_attn_specs(cfg, q_cb, k_cb, v_cb, q2_cb, k2_cb) + [_ANY] * (n_ride + has_buf),
        out_specs=[pl.BlockSpec((bq, G * LANE), lambda h, j: (j, col0 // G + h))] + [_ANY] * n_ride,
        out_shape=[jax.ShapeDtypeStruct((T, width), BF16)] + [s for r in rides for s in r.out_shapes()],
        scratch_shapes=_ride_scratch(rides),
        input_output_aliases=aliases,
        compiler_params=_cparams(("arbitrary", "arbitrary") if rides else ("parallel", "parallel")),
    )(*operands, *[a for r in rides for a in r.arrays], *([into[0]] if has_buf else []))
    if not rides:
        return outs[0]
    return outs[0], _ride_outputs(rides, outs[1:])


def _attn_bwd(cfg, q, q_cb, k, k_cb, v, v_cb, do, do_cb, *, q2=None, q2_cb=0, k2=None, k2_cb=0, bias=None, sink=None,
              want_dbias=False, dq_dtype=F32, rides=(), name):
    G, T, S, C, W, Hkv = cfg.G, cfg.T, cfg.S, cfg.C, cfg.W, cfg.Hkv
    assert q_cb % G == 0 and do_cb % G == 0 and not (want_dbias and G > 1)
    operands = [q, k, v] + ([q2, k2] if cfg.two else []) + ([bias] if cfg.n_var else []) + ([sink] if cfg.has_sink else [])
    operands.append(do)
    in_specs = _attn_specs(cfg, q_cb, k_cb, v_cb, q2_cb, k2_cb)
    bq = cfg.bq
    do_blocks = do.shape[0] // bq
    assert do.shape[0] == T or (do.shape[0] == S and not cfg.ctx_queries)
    in_specs.append(pl.BlockSpec((bq, G * LANE), lambda h, j: (jnp.minimum(j, do_blocks - 1), do_cb // G + h)))

    out_specs = [pl.BlockSpec((bq, G * LANE), lambda h, j: (j, h)),
                 pl.BlockSpec((T, LANE), lambda h, j: (0, h)),
                 pl.BlockSpec((T, LANE), lambda h, j: (0, h))]
    out_shape = [jax.ShapeDtypeStruct((T, Hkv * G * LANE), dq_dtype),
                 jax.ShapeDtypeStruct((T, Hkv * LANE), F32),
                 jax.ShapeDtypeStruct((T, Hkv * LANE), F32)]
    if cfg.two:
        out_specs += [pl.BlockSpec((bq, LANE), lambda h, j: (j, h)), pl.BlockSpec((T, LANE), lambda h, j: (0, 0))]
        out_shape += [jax.ShapeDtypeStruct((T, Hkv * LANE), dq_dtype), jax.ShapeDtypeStruct((T, LANE), F32)]
    if want_dbias:
        out_specs.append(pl.BlockSpec((1, cfg.n_var, BQ, W), lambda h, j: (h, 0, 0, 0)))
        out_shape.append(jax.ShapeDtypeStruct((Hkv, cfg.n_var, BQ, W), F32))
    if cfg.has_sink:
        out_specs.append(pl.BlockSpec((1, G * BQ, LANE), lambda h, j: (h, 0, 0)))
        out_shape.append(jax.ShapeDtypeStruct((Hkv, G * BQ, LANE), F32))

    n_ride = sum(len(r.arrays) for r in rides)
    operands += [a for r in rides for a in r.arrays]
    in_specs += [_ANY] * n_ride
    out_specs += [_ANY] * n_ride
    out_shape += [s for r in rides for s in r.out_shapes()]
    n_q = T // bq

    def body(*refs):
        (q_ref, k_ref, v_ref, q2_ref, k2_ref, bias_ref, sink_ref), rest = _attn_unpack(cfg, refs)
        do_ref, ride_in = rest[0], rest[1:1 + n_ride]
        dq_ref, dk_ref, dv_ref = rest[1 + n_ride:4 + n_ride]
        rest = rest[4 + n_ride:]
        dq2_ref = dk2_ref = dbias_ref = dsink_ref = None
        if cfg.two:
            dq2_ref, dk2_ref = rest[:2]
            rest = rest[2:]
        if want_dbias:
            dbias_ref = rest[0]
            rest = rest[1:]
        if cfg.has_sink:
            dsink_ref = rest[0]
            rest = rest[1:]
        ride_groups = _ride_split(rides, ride_in, rest[:n_ride], rest[n_ride:])
        h = pl.program_id(0)
        j = pl.program_id(1)
        if rides:
            pl.when((h == 0) & (j == 0))(lambda: _ride_start(rides, ride_groups))

        @pl.when(j == 0)
        def _():
            dk_ref[...] = jnp.zeros_like(dk_ref)
            dv_ref[...] = jnp.zeros_like(dv_ref)
            if want_dbias:
                dbias_ref[...] = jnp.zeros_like(dbias_ref)
            if cfg.has_sink:
                dsink_ref[...] = jnp.zeros_like(dsink_ref)

        if cfg.two:
            @pl.when((j == 0) & (h == 0))
            def _():
                dk2_ref[...] = jnp.zeros_like(dk2_ref)

        def block():
            p_w, p_c, p_s, (off, var, qs, kw, kc, q2s, k2w, k2c) = _attn_probs(
                cfg, j, q_ref, k_ref, q2_ref, k2_ref, bias_ref, sink_ref)
            dot_ = do_ref[...]
            dos = jnp.concatenate([dot_[:, g * LANE:(g + 1) * LANE] for g in range(G)], axis=0) if G > 1 else dot_
            dos = dos.astype(BF16)
            vw = v_ref[pl.ds(off, W), :]
            vc = v_ref[pl.ds(S, C), :]
            dp_w = _dot_nt(dos, vw)
            dp_c = _dot_nt(dos, vc)
            delta = jnp.sum(p_w * dp_w, axis=-1, keepdims=True) + jnp.sum(p_c * dp_c, axis=-1, keepdims=True)
            ds_w = p_w * (dp_w - delta)
            ds_c = p_c * (dp_c - delta)
            if want_dbias:
                dbias_ref[0, pl.ds(var, 1)] += ds_w[None]
            if cfg.has_sink:
                dsink_ref[0] += jnp.broadcast_to(-(p_s * delta), (G * bq, LANE))
            dsw = (ds_w * cfg.scale).astype(BF16)
            dsc = (ds_c * cfg.scale).astype(BF16)
            dq = _dot(dsw, kw) + _dot(dsc, kc)
            for g in range(G):
                dq_ref[:, g * LANE:(g + 1) * LANE] = dq[g * bq:(g + 1) * bq].astype(dq_ref.dtype)
            dk_ref[pl.ds(off, W), :] += _dot_tn(dsw, qs)
            dk_ref[pl.ds(S, C), :] += _dot_tn(dsc, qs)
            dv_ref[pl.ds(off, W), :] += _dot_tn(p_w.astype(BF16), dos)
            dv_ref[pl.ds(S, C), :] += _dot_tn(p_c.astype(BF16), dos)
            if cfg.two:
                dq2_ref[...] = (_dot(dsw, k2w) + _dot(dsc, k2c)).astype(dq2_ref.dtype)
                dk2_ref[pl.ds(off, W), :] += _dot_tn(dsw, q2s)
                dk2_ref[pl.ds(S, C), :] += _dot_tn(dsc, q2s)

        if cfg.ctx_queries:
            block()
        else:
            pl.when(j * bq < S)(block)

            @pl.when(j * bq >= S)
            def _():
                dq_ref[...] = jnp.zeros_like(dq_ref)
                if cfg.two:
                    dq2_ref[...] = jnp.zeros_like(dq2_ref)

        if rides:
            pl.when((h == Hkv - 1) & (j == n_q - 1))(lambda: _ride_wait(rides, ride_groups))

    outs = pl.pallas_call(
        body, name=name, grid=(Hkv, n_q),
        in_specs=in_specs, out_specs=out_specs, out_shape=out_shape,
        scratch_shapes=_ride_scratch(rides),
        compiler_params=_cparams(("arbitrary", "arbitrary")),
    )(*operands)
    if not rides:
        return outs
    return list(outs[:len(outs) - n_ride]) + [_ride_outputs(rides, outs[len(outs) - n_ride:])]


def _na_bias(rpb, S):
    H = rpb.shape[0]
    rows = S // GRID_W
    pad_l = GRID_W - 1 - (NA_WIN_C - 1)
    ext = jnp.concatenate([jnp.broadcast_to(rpb[:, :, :1], (H, 2 * NA_WIN_R - 1, pad_l)), rpb,
                           jnp.broadcast_to(rpb[:, :, -1:], (H, 2 * NA_WIN_R - 1, pad_l))], axis=2)
    by_col = jnp.stack([ext[:, :, GRID_W - 1 - qc:2 * GRID_W - 1 - qc] for qc in range(GRID_W)], axis=2)
    cq = np.arange(GRID_W)
    c0 = np.clip(cq - NA_WIN_C // 2, 0, GRID_W - NA_WIN_C)
    col_in = (cq[None, :] >= c0[:, None]) & (cq[None, :] < c0[:, None] + NA_WIN_C)
    n_lat = S // BQ
    neg_tile = jnp.full((H, GRID_W, GRID_W), NEG, F32)
    variants = []
    for v in range(5):
        j = {0: 0, 1: 1, 2: 2, 3: n_lat - 2, 4: n_lat - 1}[v]
        start = int(np.clip(j - 2, 0, n_lat - 5))
        assert j - start == v
        q_rows = []
        for qr in range(2):
            r = 2 * j + qr
            r0 = int(np.clip(r - NA_WIN_R // 2, 0, rows - NA_WIN_R))
            k_tiles = []
            for kr in range(10):
                krow = 2 * start + kr
                if r0 <= krow < r0 + NA_WIN_R:
                    k_tiles.append(jnp.where(col_in[None], by_col[:, krow - r + NA_WIN_R - 1], NEG))
                else:
                    k_tiles.append(neg_tile)
            q_rows.append(jnp.concatenate(k_tiles, axis=2))
        variants.append(jnp.concatenate(q_rows, axis=1))
    return jnp.stack(variants, axis=1)


def _swa_mask(S):
    qq = np.arange(BQ)[:, None]
    kk = np.arange(3 * BQ)[None, :]
    tiles = [np.where(np.abs(kk - v * BQ - qq) <= SWA_WINDOW, 0.0, NEG) for v in range(3)]
    return jnp.asarray(np.stack(tiles)[None], F32)


def _ffn_tiles(T, S, F):
    return _row_block(T, S), _pick(F, 1408)


def _halo_rows(dtype):
    return 8 * 4 // jnp.dtype(dtype).itemsize


def _halo_specs(T, bt, bf, dtype):
    hr = _halo_rows(dtype)
    nh = bt // hr
    return [pl.BlockSpec((bt, bf), lambda f, i: (i, f)),
            pl.BlockSpec((hr, bf), lambda f, i: (jnp.maximum(i * nh - 1, 0), f)),
            pl.BlockSpec((hr, bf), lambda f, i: (jnp.minimum((i + 1) * nh, T // hr - 1), f))]


def _neighbours(x, prev, nxt, i, bt, S, T):
    r = lax.broadcasted_iota(jnp.int32, x.shape, 0)
    g0 = i * bt
    first_open = jnp.logical_or(g0 == 0, g0 == S)
    last_open = jnp.logical_or(g0 + bt == S, g0 + bt == T)
    hr = prev.shape[0]
    before = jnp.where(r == 0, jnp.where(first_open, 0.0, prev[hr - 1:hr, :].astype(F32)), pltpu.roll(x, 1, 0))
    after = jnp.where(r == bt - 1, jnp.where(last_open, 0.0, nxt[0:1, :].astype(F32)), pltpu.roll(x, bt - 1, 0))
    return before, after


def _sigmoid(a):
    return 1.0 / (1.0 + jnp.exp(-a))


def _ffn_fwd(gp, u, cw, cb, S, *, name):
    T, F = gp.shape
    bt, bf = _ffn_tiles(T, S, F)

    def body(g_ref, gp_ref, gn_ref, u_ref, w_ref, b_ref, o_ref):
        i = pl.program_id(1)
        g = g_ref[...].astype(F32)
        before, after = _neighbours(g, gp_ref[...], gn_ref[...], i, bt, S, T)
        a = before * w_ref[0:1, :] + g * w_ref[1:2, :] + after * w_ref[2:3, :] + b_ref[...]
        o_ref[...] = (a * _sigmoid(a) * u_ref[...].astype(F32)).astype(o_ref.dtype)

    return pl.pallas_call(
        body, name=name, grid=(F // bf, T // bt),
        in_specs=_halo_specs(T, bt, bf, gp.dtype) + [pl.BlockSpec((bt, bf), lambda f, i: (i, f)),
                                              pl.BlockSpec((3, bf), lambda f, i: (0, f)),
                                              pl.BlockSpec((1, bf), lambda f, i: (0, f))],
        out_specs=pl.BlockSpec((bt, bf), lambda f, i: (i, f)),
        out_shape=jax.ShapeDtypeStruct((T, F), BF16),
        compiler_params=_cparams(("parallel", "parallel")),
    )(gp, gp, gp, u, cw, cb)


def _ffn_bwd_act(gp, u, da_out, cw, cb, S, *, name):
    T, F = gp.shape
    bt, bf = _ffn_tiles(T, S, F)

    def body(g_ref, gp_ref, gn_ref, u_ref, d_ref, w_ref, b_ref, da_ref, du_ref, acc_ref):
        i = pl.program_id(1)

        @pl.when(i == 0)
        def _():
            acc_ref[...] = jnp.zeros_like(acc_ref)

        g = g_ref[...].astype(F32)
        before, after = _neighbours(g, gp_ref[...], gn_ref[...], i, bt, S, T)
        a = before * w_ref[0:1, :] + g * w_ref[1:2, :] + after * w_ref[2:3, :] + b_ref[...]
        sig = _sigmoid(a)
        d = d_ref[...].astype(F32)
        du_ref[...] = (d * (a * sig)).astype(du_ref.dtype)
        da = d * u_ref[...].astype(F32) * (sig * (1.0 + a * (1.0 - sig)))
        da_ref[...] = da
        acc_ref[0:1, :] += jnp.sum(da * before, axis=0, keepdims=True)
        acc_ref[1:2, :] += jnp.sum(da * g, axis=0, keepdims=True)
        acc_ref[2:3, :] += jnp.sum(da * after, axis=0, keepdims=True)
        acc_ref[3:4, :] += jnp.sum(da, axis=0, keepdims=True)

    blk = pl.BlockSpec((bt, bf), lambda f, i: (i, f))
    return pl.pallas_call(
        body, name=name, grid=(F // bf, T // bt),
        in_specs=_halo_specs(T, bt, bf, gp.dtype) + [blk, blk,
                                              pl.BlockSpec((3, bf), lambda f, i: (0, f)),
                                              pl.BlockSpec((1, bf), lambda f, i: (0, f))],
        out_specs=[blk, blk, pl.BlockSpec((8, bf), lambda f, i: (0, f))],
        out_shape=[jax.ShapeDtypeStruct((T, F), F32), jax.ShapeDtypeStruct((T, F), BF16),
                   jax.ShapeDtypeStruct((8, F), F32)],
        compiler_params=_cparams(("parallel", "arbitrary")),
    )(gp, gp, gp, u, da_out, cw, cb)


def _ffn_bwd_conv(da, cw, S, *, name):
    T, F = da.shape
    bt, bf = _ffn_tiles(T, S, F)

    def body(d_ref, dp_ref, dn_ref, w_ref, o_ref):
        i = pl.program_id(1)
        d = d_ref[...]
        before, after = _neighbours(d, dp_ref[...], dn_ref[...], i, bt, S, T)
        o_ref[...] = (after * w_ref[0:1, :] + d * w_ref[1:2, :] + before * w_ref[2:3, :]).astype(o_ref.dtype)

    return pl.pallas_call(
        body, name=name, grid=(F // bf, T // bt),
        in_specs=_halo_specs(T, bt, bf, da.dtype) + [pl.BlockSpec((3, bf), lambda f, i: (0, f))],
        out_specs=pl.BlockSpec((bt, bf), lambda f, i: (i, f)),
        out_shape=jax.ShapeDtypeStruct((T, F), BF16),
        compiler_params=_cparams(("parallel", "parallel")),
    )(da, da, da, cw)


def _ew_rows(R, N, n_arrays):
    return _pick(R, max(16, EW_VMEM_BUDGET // (8 * n_arrays * N)), 16)


def _adam(w, g, m, v, *, rides=(), emit_grad=False, name):
    lead = w.shape[:-2]
    R, N = w.shape[-2:]
    br = _ew_rows(R, N, 7)
    bc1 = 1.0 - ADAM_B1 ** ADAM_STEP
    bc2 = 1.0 - ADAM_B2 ** ADAM_STEP
    grid = lead + (R // br,)
    n_ride = sum(len(r.arrays) for r in rides)

    n_out = 4 if emit_grad else 3

    def body(*refs):
        w_ref, g_ref, m_ref, v_ref = refs[:4]
        d_ref, mo_ref, vo_ref = refs[4 + n_ride:7 + n_ride]
        if emit_grad:
            refs[7 + n_ride][...] = g_ref[...]
        if rides:
            ride_groups = _ride_split(rides, refs[4:4 + n_ride], refs[4 + n_out + n_ride:4 + n_out + 2 * n_ride],
                                      refs[4 + n_out + 2 * n_ride:])
            steps = [pl.program_id(d) for d in range(len(grid))]
            pl.when(functools.reduce(jnp.logical_and, [s == 0 for s in steps]))(lambda: _ride_start(rides, ride_groups))
        gv = g_ref[...]
        mn = ADAM_B1 * m_ref[...] + (1.0 - ADAM_B1) * gv
        vn = ADAM_B2 * v_ref[...] + (1.0 - ADAM_B2) * (gv * gv)
        mo_ref[...] = mn
        vo_ref[...] = vn
        d_ref[...] = -ADAM_LR * ((mn / bc1) / (jnp.sqrt(vn / bc2) + ADAM_EPS) + ADAM_WD * w_ref[...])
        if rides:
            pl.when(functools.reduce(jnp.logical_and, [s == n - 1 for s, n in zip(steps, grid)]))(
                lambda: _ride_wait(rides, ride_groups))

    if lead:
        blk = pl.BlockSpec((None, br, N), lambda l, i: (l, i, 0))
    else:
        blk = pl.BlockSpec((br, N), lambda i: (i, 0))
    shp = jax.ShapeDtypeStruct(w.shape, F32)
    outs = pl.pallas_call(
        body, name=name, grid=grid,
        in_specs=[blk, blk, blk, blk] + [_ANY] * n_ride, out_specs=[blk] * n_out + [_ANY] * n_ride,
        out_shape=[shp] * n_out + [s for r in rides for s in r.out_shapes()],
        scratch_shapes=_ride_scratch(rides),
        input_output_aliases=_ride_aliases(rides, 4, n_out),
        compiler_params=_cparams(("arbitrary" if rides else "parallel",) * len(grid)),
    )(w, g, m, v, *[a for r in rides for a in r.arrays])
    if not rides:
        return outs
    return outs[:n_out], _ride_outputs(rides, outs[n_out:])


def _sum_lead(x, out_dtype, *, name):
    n, R, N = x.shape
    br = _ew_rows(R, N, n + 1)

    def body(x_ref, o_ref):
        acc = x_ref[0].astype(F32)
        for k in range(1, n):
            acc = acc + x_ref[k].astype(F32)
        o_ref[...] = acc.astype(o_ref.dtype)

    return pl.pallas_call(
        body, name=name, grid=(R // br,),
        in_specs=[pl.BlockSpec((n, br, N), lambda i: (0, i, 0))],
        out_specs=pl.BlockSpec((br, N), lambda i: (i, 0)),
        out_shape=jax.ShapeDtypeStruct((R, N), out_dtype),
        compiler_params=_cparams(("parallel",)),
    )(x)


def _sum_parts(parts, landed, chip, core, stack, *, name):
    _, R, N = parts.shape
    n_layers, layer, buf = stack
    br = _ew_rows(R, N, 5)

    def body(pos_ref, own_ref, landed_ref, *rest):
        o_ref = rest[-1]
        acc = own_ref[...].astype(F32)
        for k in range(3):
            acc = acc + landed_ref[k].astype(F32)
        o_ref[...] = acc

    operands = [jnp.stack([chip, core]).astype(jnp.int32), parts, landed]
    in_specs = [pl.BlockSpec((None, br, N), lambda i, pos: (pos[0], i, 0)),
                pl.BlockSpec((3, br, N), lambda i, pos: (0, i, 0))]
    aliases = {}
    if buf is not None:
        aliases = {3: 0}
        operands.append(buf)
        in_specs.append(pl.BlockSpec(memory_space=pl.ANY))
    return pl.pallas_call(
        body, name=name,
        grid_spec=pltpu.PrefetchScalarGridSpec(
            num_scalar_prefetch=1, grid=(R // br,), in_specs=in_specs,
            out_specs=pl.BlockSpec((None, None, br, N), lambda i, pos: (layer, pos[1], i, 0))),
        out_shape=jax.ShapeDtypeStruct((n_layers, 2, R, N), F32),
        input_output_aliases=aliases,
        compiler_params=_cparams(("parallel",)),
    )(*operands)


def _place_own(shards, layer, core, slot, *, name):
    _, R, N = shards.shape
    br = _ew_rows(R // 2, N, 2)
    nb = R // 2 // br

    def body(pos_ref, x_ref, o_ref):
        o_ref[...] = x_ref[...].astype(o_ref.dtype)

    return pl.pallas_call(
        body, name=name,
        grid_spec=pltpu.PrefetchScalarGridSpec(
            num_scalar_prefetch=1, grid=(nb,),
            in_specs=[pl.BlockSpec((None, br, N), lambda i, pos: (layer, pos[0] * nb + i, 0))],
            out_specs=pl.BlockSpec((None, br, N), lambda i, pos: (pos[1], i, 0))),
        out_shape=jax.ShapeDtypeStruct((8, R // 2, N), BF16),
        compiler_params=_cparams(("parallel",)),
    )(jnp.stack([core, slot]).astype(jnp.int32), shards)


def _add_half(g, r, core, *, name):
    Q, _, R, N = g.shape
    br = _ew_rows(R, N, 3)

    def body(c_ref, g_ref, r_ref, o_ref):
        o_ref[...] = (g_ref[...] + r_ref[...]).astype(o_ref.dtype)

    return pl.pallas_call(
        body, name=name,
        grid_spec=pltpu.PrefetchScalarGridSpec(
            num_scalar_prefetch=1, grid=(Q, R // br),
            in_specs=[pl.BlockSpec((None, None, br, N), lambda q, i, c_ref: (q, c_ref[0], i, 0)),
                      pl.BlockSpec((None, br, N), lambda q, i, c_ref: (q, i, 0))],
            out_specs=pl.BlockSpec((None, br, N), lambda q, i, c_ref: (q, i, 0))),
        out_shape=jax.ShapeDtypeStruct((Q, R, N), BF16),
        compiler_params=_cparams(("parallel", "parallel")),
    )(core.reshape(1).astype(jnp.int32), g, r)


_ANY = pl.BlockSpec(memory_space=pl.ANY)


def _place():
    return lax.axis_index("x"), lax.axis_index("y"), lax.axis_index("c")


def _allgather8(blocks, *, name):
    n = len(blocks)

    def body(*refs):
        xs, outs = refs[:n], refs[n:2 * n]
        send_sems, recv_sems, local_sems = refs[2 * n:]
        x, y, c = _place()
        me, sibling = (x, y, c), (x, y, 1 - c)
        chips = [(1 - x, y), (x, 1 - y), (1 - x, 1 - y)]

        def slot(a, px, py, pc):
            return outs[a].at[4 * px + 2 * py + pc]

        def copy(a, k, block, to, src=None):
            return pltpu.make_async_remote_copy(
                src_ref=slot(a, *block) if src is None else src, dst_ref=slot(a, *block),
                send_sem=send_sems.at[a, k], recv_sem=recv_sems.at[a, k], device_id=to, device_id_type=MESH)

        mine = [pltpu.make_async_copy(xs[a], slot(a, *me), local_sems.at[a]) for a in range(n)]
        for cp in mine:
            cp.start()
        first = []
        for a in range(n):
            first.append(copy(a, 0, me, sibling, src=xs[a]))
            first += [copy(a, 1 + j, me, (*chip, c), src=xs[a]) for j, chip in enumerate(chips)]
        for cp in first:
            cp.start()
        passed = []
        for j, chip in enumerate(chips):
            for a in range(n):
                copy(a, 1 + j, (*chip, c), me).wait_recv()
                fwd = copy(a, 4 + j, (*chip, c), sibling)
                fwd.start()
                passed.append(fwd)
        for a in range(n):
            copy(a, 0, sibling, me).wait_recv()
            for j, chip in enumerate(chips):
                copy(a, 4 + j, (*chip, 1 - c), me).wait_recv()
        for cp in first + passed:
            cp.wait_send()
        for cp in mine:
            cp.wait()

    return pl.pallas_call(
        body, name=name,
        in_specs=[_ANY] * n, out_specs=[_ANY] * n,
        out_shape=[jax.ShapeDtypeStruct((8,) + b.shape, b.dtype) for b in blocks],
        scratch_shapes=[pltpu.SemaphoreType.DMA((n, 7)), pltpu.SemaphoreType.DMA((n, 7)), pltpu.SemaphoreType.DMA((n,))],
    )(*blocks)


class _Exchange:
    n_sems = 1

    def __init__(self, arrays):
        self.arrays = list(arrays)

    def out_shapes(self):
        return [jax.ShapeDtypeStruct(g.shape[:1] + g.shape[2:], g.dtype) for g in self.arrays]

    def copy(self, k, src, dst, sems, landing):
        x, y, c = _place()
        return pltpu.make_async_remote_copy(src_ref=src.at[:, 1 - c], dst_ref=dst, send_sem=sems[0], recv_sem=sems[1],
                                            device_id=(x, y, 1 - c), device_id_type=MESH)

    def copies(self, group, landing):
        xs, outs, send_sems, recv_sems = group
        return [self.copy(k, xs[a], outs[a], (send_sems.at[a, k], recv_sems.at[a, k]), landing)
                for a in range(len(xs)) for k in range(self.n_sems)]


class _Scatter(_Exchange):
    n_sems = 3

    def out_shapes(self):
        return [jax.ShapeDtypeStruct((3,) + p.shape[1:], p.dtype) for p in self.arrays]

    def copy(self, k, src, dst, sems, landing):
        x, y, c = _place()
        px, py = [(1 - x, y), (x, 1 - y), (1 - x, 1 - y)][k]
        return pltpu.make_async_remote_copy(src_ref=src.at[2 * px + py], dst_ref=dst.at[k], send_sem=sems[0], recv_sem=sems[1],
                                            device_id=(px, py, c), device_id_type=MESH)


class _GatherChips(_Exchange):
    n_sems = 3
    in_place = True

    def out_shapes(self):
        return [jax.ShapeDtypeStruct(b.shape, b.dtype) for b in self.arrays]

    def copy(self, k, src, dst, sems, landing):
        x, y, c = _place()
        px, py = [(1 - x, y), (x, 1 - y), (1 - x, 1 - y)][k]
        slot = 4 * px + 2 * py + c if landing else 4 * x + 2 * y + c
        return pltpu.make_async_remote_copy(src_ref=src.at[4 * x + 2 * y + c], dst_ref=dst.at[slot], send_sem=sems[0],
                                            recv_sem=sems[1], device_id=(px, py, c), device_id_type=MESH)


class _GatherCores(_GatherChips):
    n_sems = 4

    def copy(self, k, src, dst, sems, landing):
        x, y, c = _place()
        slot = 2 * k + 1 - c if landing else 2 * k + c
        return pltpu.make_async_remote_copy(src_ref=src.at[2 * k + c], dst_ref=dst.at[slot], send_sem=sems[0],
                                            recv_sem=sems[1], device_id=(x, y, 1 - c), device_id_type=MESH)


class _Join(_GatherChips):
    n_sems = 1

    def copy(self, k, src, dst, sems, landing):
        x, y, c = _place()
        return pltpu.make_async_remote_copy(src_ref=src.at[:, c], dst_ref=dst.at[:, 1 - c if landing else c], send_sem=sems[0],
                                            recv_sem=sems[1], device_id=(x, y, 1 - c), device_id_type=MESH)


def _ride_aliases(rides, first_in, first_out):
    aliases, i = {}, 0
    for r in rides:
        for a in range(len(r.arrays)):
            if getattr(r, "in_place", False):
                aliases[first_in + i + a] = first_out + i + a
        i += len(r.arrays)
    return aliases


def _ride_scratch(rides):
    shapes = []
    for r in rides:
        shapes += [pltpu.SemaphoreType.DMA((len(r.arrays), r.n_sems)), pltpu.SemaphoreType.DMA((len(r.arrays), r.n_sems))]
    return shapes


def _ride_split(rides, in_refs, out_refs, sem_refs):
    groups, i, o = [], 0, 0
    for k, r in enumerate(rides):
        n = len(r.arrays)
        groups.append((in_refs[i:i + n], out_refs[o:o + n], sem_refs[2 * k], sem_refs[2 * k + 1]))
        i, o = i + n, o + n
    return groups


def _ride_start(rides, groups):
    for r, g in zip(rides, groups):
        for cp in r.copies(g, False):
            cp.start()


def _ride_wait(rides, groups):
    for r, g in zip(rides, groups):
        for cp in r.copies(g, True):
            cp.wait_recv()
        for cp in r.copies(g, False):
            cp.wait_send()


def _run_rides(rides, *, name):
    n_in = sum(len(r.arrays) for r in rides)

    def body(*refs):
        groups = _ride_split(rides, refs[:n_in], refs[n_in:2 * n_in], refs[2 * n_in:])
        _ride_start(rides, groups)
        _ride_wait(rides, groups)

    outs = pl.pallas_call(
        body, name=name,
        in_specs=[_ANY] * n_in, out_specs=[_ANY] * n_in,
        out_shape=[s for r in rides for s in r.out_shapes()],
        scratch_shapes=_ride_scratch(rides),
        input_output_aliases=_ride_aliases(rides, 0, 0),
    )(*[a for r in rides for a in r.arrays])
    return _ride_outputs(rides, outs)


def _ride_outputs(rides, outs):
    res, o = [], 0
    for r in rides:
        res.append(list(outs[o:o + len(r.arrays)]))
        o += len(r.arrays)
    return res


def _perm_w_in(wt):
    pad = jnp.zeros((PCOLS - IN_COLS, wt.shape[1]), wt.dtype)
    return jnp.concatenate([wt[:3072], wt[3136:IN_COLS], wt[3072:3136], pad], axis=0)


def _unperm_w_in(gt):
    return jnp.concatenate([gt[:3072], gt[4096:IN_COLS], gt[3072:4096]], axis=0)


def _perm_w_uq(w):
    w4 = w.reshape(MLA_Q_LORA, MLA_HEADS, MLA_NOPE + MLA_ROPE)
    nope = w4[:, :, :MLA_NOPE].reshape(MLA_Q_LORA, MLA_HEADS * LANE)
    pe = jnp.pad(w4[:, :, MLA_NOPE:], ((0, 0), (0, 0), (0, LANE - MLA_ROPE))).reshape(MLA_Q_LORA, MLA_HEADS * LANE)
    return jnp.concatenate([nope, pe], axis=1)


def _unperm_w_uq(g):
    nope = g[:, :MLA_HEADS * LANE].reshape(MLA_Q_LORA, MLA_HEADS, LANE)
    pe = g[:, MLA_HEADS * LANE:].reshape(MLA_Q_LORA, MLA_HEADS, LANE)[:, :, :MLA_ROPE]
    return jnp.concatenate([nope, pe], axis=2).reshape(MLA_Q_LORA, MLA_HEADS * (MLA_NOPE + MLA_ROPE))


def _perm_w_ukv(w):
    w4 = w.reshape(MLA_KV_LORA, MLA_HEADS, MLA_NOPE + MLA_V)
    return jnp.concatenate([w4[:, :, :MLA_NOPE].reshape(MLA_KV_LORA, -1), w4[:, :, MLA_NOPE:].reshape(MLA_KV_LORA, -1)], axis=1)


def _unperm_w_ukv(g):
    kn = g[:, :MLA_HEADS * LANE].reshape(MLA_KV_LORA, MLA_HEADS, LANE)
    vv = g[:, MLA_HEADS * LANE:].reshape(MLA_KV_LORA, MLA_HEADS, LANE)
    return jnp.concatenate([kn, vv], axis=2).reshape(MLA_KV_LORA, -1)


def _silu(v):
    return v * jax.nn.sigmoid(v)


def _silu_grad(v):
    s = jax.nn.sigmoid(v)
    return s * (1.0 + v * (1.0 - s))


_WEIGHTS = ("c_ctx", "w_ada", "b_ada", "w_in", "na_rpb", "swa_sink", "mla_q_norm", "mla_kv_norm", "mla_w_uq", "mla_w_ukv",
            "gqa_q_norm", "gqa_k_norm", "w_out", "ln1_g", "ln1_b", "ffn_w_gate", "ffn_w_up", "ffn_conv_w", "ffn_conv_b",
            "ffn_w_down", "ln2_g", "ln2_b")
_COL_SHARDED = ("mla_w_uq", "mla_w_ukv", "ffn_w_gate", "ffn_w_up")
_ROW_SHARDED = ("w_out", "ffn_w_down")
_BIG = ("w_in",) + _COL_SHARDED + _ROW_SHARDED
_SMALL = ("c_ctx", "b_ada", "na_rpb", "swa_sink", "mla_q_norm", "mla_kv_norm", "gqa_q_norm", "gqa_k_norm", "ln1_g", "ln1_b",
          "ffn_conv_w", "ffn_conv_b", "ln2_g", "ln2_b")


def _piece_rows(a):
    return -(-a.size // (8 * LANE)) * 8


def _pack(arrays):
    return jnp.concatenate([jnp.pad(a.reshape(-1), (0, _piece_rows(a) * LANE - a.size)).reshape(_piece_rows(a), LANE)
                            for a in arrays], axis=0)


def _unpack(packed, like):
    out, r = [], 0
    for a in like:
        n = _piece_rows(a)
        out.append(packed[r:r + n].reshape(-1)[:a.size].reshape(a.shape))
        r += n
    return out


def _train_step(x, c, ctx, loss_target, w, m_in, v_in):
    L = DEPTH
    S, D = x.shape[1], x.shape[2]
    C = ctx.shape[1]
    T = S + C
    F = w["ffn_conv_b"].shape[1]
    ax, ay, ac = _place()
    chip = 2 * ax + ay
    dev = 2 * chip + ac
    n_ada = w["w_ada"].shape[2]
    w, m_in, v_in = dict(w), dict(m_in), dict(v_in)
    for d in (w, m_in, v_in):
        d["w_in"] = jnp.swapaxes(d["w_in"], 1, 2)

    gather_groups = {"A": ("w_in", "mla_w_uq", "mla_w_ukv"), "B": ("w_out",), "Cg": ("ffn_w_gate",), "Cu": ("ffn_w_up",),
                     "D": ("ffn_w_down",)}
    full = {n: [None] * L for n in _BIG}
    w_in_p, w_uq_p, w_ukv_p = [None] * L, [None] * L, [None] * L
    half_done = {}

    def chips_step(group, l):
        return _GatherChips([_place_own(w[n], l, ac, dev, name="gather_place") for n in gather_groups[group]])

    def cores_step(group, l):
        return _GatherCores(half_done.pop((group, l)))

    def finish_group(group, l, bufs):
        for n, b in zip(gather_groups[group], bufs):
            r, cols = b.shape[1:]
            if n in _COL_SHARDED:
                full[n][l] = b.reshape(4, 2, r, cols).transpose(1, 2, 0, 3).reshape(2 * r, 4 * cols)
            else:
                full[n][l] = b.reshape(8 * r, cols)
        if group == "A":
            w_in_p[l], w_uq_p[l] = _perm_w_in(full["w_in"][l]), _perm_w_uq(full["mla_w_uq"][l])
            w_ukv_p[l] = _perm_w_ukv(full["mla_w_ukv"][l])

    def with_rides(result, rides):
        return result if rides else (result, [])

    def my_half(a):
        r = a.shape[0] // 2
        return lax.dynamic_slice_in_dim(a, ac * r, r, axis=0).astype(BF16)

    gathered = _allgather8([my_half(w[n][0]) for n in gather_groups["A"]] + [w["ffn_conv_w"]], name="gather_weights")
    finish_group("A", 0, gathered[:-1])
    conv_w = gathered[-1][::2].transpose(1, 2, 0, 3).reshape(L, 3, F)

    (c_all,) = _allgather8([c], name="gather_c")
    c16 = jnp.concatenate([c_all.reshape(8, D), jnp.broadcast_to(w["c_ctx"][None], (8, D))], axis=0)
    row_keep = (jnp.arange(16) <= 8).astype(F32)[:, None]
    sc = _silu(c16) * row_keep
    b_loc = lax.dynamic_slice_in_dim(w["b_ada"], chip * n_ada, n_ada, axis=1)
    mod_loc = jnp.stack([_mm(sc, w["w_ada"], b_layer=l, name="mod_mm") + b_loc[l][None] for l in range(L)])
    (mod_g,) = _allgather8([mod_loc], name="gather_mod")
    mod_all = mod_g[::2].transpose(1, 2, 0, 3).reshape(L, 16, 4 * n_ada)
    mod_x = lax.dynamic_index_in_dim(mod_all, dev, axis=1, keepdims=False)
    mod_c = mod_all[:, 8]
    mods = [jnp.stack([mod_x[l].reshape(6, D), mod_c[l].reshape(6, D)], axis=1) for l in range(L)]

    tabs = _rope_tables(S, C, HEAD_DIM) + _rope_tables(S, C, MLA_ROPE)
    swa_mask = _swa_mask(S)
    scale = HEAD_DIM ** -0.5
    def attn_cfgs(l):
        cq = l < L - 1
        return (_AttnCfg(Hkv=NA_HEADS, G=1, S=S, C=C, band=(2, 5), scale=scale, n_var=5, bias_per_head=True, ctx_queries=cq),
                _AttnCfg(Hkv=SWA_KV_HEADS, G=SWA_HEADS // SWA_KV_HEADS, S=S, C=C, band=(1, 3), scale=scale, n_var=3,
                         has_sink=True, ctx_queries=cq),
                _AttnCfg(Hkv=MLA_HEADS, G=1, S=S, C=C, band=None, scale=(MLA_NOPE + MLA_ROPE) ** -0.5, two=True,
                         bq=2 * BQ, ctx_queries=cq),
                _AttnCfg(Hkv=GQA_KV_HEADS, G=GQA_HEADS // GQA_KV_HEADS, S=S, C=C, band=None, scale=scale, bq=2 * BQ,
                         ctx_queries=cq))

    row = lambda a: a[None, :]

    xt = jnp.concatenate([x[0], ctx[0]], axis=0)
    saved = []
    for l in range(L):
        md = mods[l]
        gq, gk, mq, mkv = row(w["gqa_q_norm"][l]), row(w["gqa_k_norm"][l]), row(w["mla_q_norm"][l]), row(w["mla_kv_norm"][l])
        h1 = _mod_fwd(xt, md[0], md[1], S, name="mod_fwd")
        p = _mm(h1, w_in_p[l], mode="nt", out_dtype=BF16, name="in_proj")
        qkv = _prep_fwd(p, tabs, gq, gk, mq, mkv, S, name="prep_fwd")
        qm = _mm(qkv, w_uq_p[l], a_off=CB_CQ * LANE, a_k=MLA_Q_LORA, tk=LANE, name="mla_uq")
        qmb = _pe_rope(qm, tabs, S, 1.0, BF16, name="mla_q_rope")
        kvm = _mm(qkv, w_ukv_p[l], a_off=CB_CKV * LANE, a_k=MLA_KV_LORA, tk=LANE, out_dtype=BF16, name="mla_ukv")
        bias_na = _na_bias(w["na_rpb"][l], S)
        sink = jnp.broadcast_to(jnp.repeat(w["swa_sink"][l].reshape(SWA_KV_HEADS, -1), BQ, axis=1)[:, :, None],
                                (SWA_KV_HEADS, SWA_HEADS // SWA_KV_HEADS * BQ, LANE))
        cfg_na, cfg_swa, cfg_mla, cfg_gqa = attn_cfgs(l)
        rows = T if l < L - 1 else S
        first, more = l == 0, l + 1 < L
        rides = [chips_step("B", l)] if first else [cores_step("B", l)]
        mix_blocks = NA_HEADS + SWA_HEADS + MLA_HEADS + GQA_HEADS
        mix, got = _attn_fwd(cfg_na, qkv, CB_NA_Q, qkv, CB_NA_K, qkv, CB_NA_V, bias=bias_na, rides=rides,
                             into=(None, 0, mix_blocks), name="na_fwd")
        if first:
            half_done[("B", l)] = got[0]
        else:
            finish_group("B", l, got[0])
        rides = [cores_step("B", l)] if first else []
        mix, got = with_rides(_attn_fwd(cfg_swa, qkv, CB_SWA_Q, qkv, CB_SWA_K, qkv, CB_SWA_V, bias=swa_mask, sink=sink,
                                        rides=rides, into=(mix, NA_HEADS, mix_blocks), name="swa_fwd"), rides)
        if first:
            finish_group("B", l, got[0])
        mix, got = _attn_fwd(cfg_mla, qmb, 0, kvm, 0, kvm, MLA_HEADS, q2=qmb, q2_cb=MLA_HEADS, k2=qkv, k2_cb=CB_KPE,
                             rides=[chips_step("Cg", l)], into=(mix, NA_HEADS + SWA_HEADS, mix_blocks), name="mla_fwd")
        half_done[("Cg", l)] = got[0]
        mix, got = _attn_fwd(cfg_gqa, qkv, CB_GQA_Q, qkv, CB_GQA_K, qkv, CB_GQA_V,
                             rides=[cores_step("Cg", l), chips_step("Cu", l)],
                             into=(mix, NA_HEADS + SWA_HEADS + MLA_HEADS, mix_blocks), name="gqa_fwd")
        finish_group("Cg", l, got[0])
        half_done[("Cu", l)] = got[1]
        z1, got = _mm(mix, full["w_out"][l], rows=rows, rides=[cores_step("Cu", l)], name="out_proj")
        finish_group("Cu", l, got[0])
        x1 = _res_fwd(xt, z1, md[2], row(w["ln1_g"][l]), row(w["ln1_b"][l]), S, name="res_fwd")
        h2 = _mod_fwd(x1, md[3], md[4], S, name="mod_fwd")
        gp, got = _mm(h2, full["ffn_w_gate"][l], rides=[chips_step("D", l)], out_dtype=BF16, name="ffn_in")
        half_done[("D", l)] = got[0]
        up, got = _mm(h2, full["ffn_w_up"][l], rides=[cores_step("D", l)] + ([chips_step("A", l + 1)] if more else []),
                      out_dtype=BF16, name="ffn_in")
        finish_group("D", l, got[0])
        if more:
            half_done[("A", l + 1)] = got[1]
        act = _ffn_fwd(gp, up, conv_w[l], row(w["ffn_conv_b"][l]), S, name="ffn_mid")
        rides = [cores_step("A", l + 1), chips_step("B", l + 1)] if more else []
        z2, got = with_rides(_mm(act, full["ffn_w_down"][l], rides=rides, name="ffn_out"), rides)
        if more:
            finish_group("A", l + 1, got[0])
            half_done[("B", l + 1)] = got[1]
        x2 = _res_fwd(x1, z2, md[5], row(w["ln2_g"][l]), row(w["ln2_b"][l]), S, name="res_fwd")
        saved.append(dict(x=xt, h1=h1, p=p, qkv=qkv, qmb=qmb, kvm=kvm, bias_na=bias_na, sink=sink, mix=mix, z1=z1, x1=x1,
                          h2=h2, gp=gp, up=up, act=act, z2=z2, cfgs=(cfg_na, cfg_swa, cfg_mla, cfg_gqa)))
        xt = x2

    dx, loss_part = _loss_fwd_bwd(xt, loss_target[0], S, name="loss")
    loss = lax.psum(loss_part[0, 0], ("x", "y", "c"))

    groups = {"ffn": ("ffn_w_gate", "ffn_w_up", "ffn_w_down", "w_out"), "rest": ("w_in", "mla_w_uq", "mla_w_ukv")}
    wgrad = [dict() for _ in range(L)]
    parts, landed = {}, {}

    def halves_of(group, l):
        return [wgrad[l][n].reshape(4, 2, wgrad[l][n].shape[1] // 2, wgrad[l][n].shape[2]) for n in groups[group]]

    def add_halves(group, l, received):
        parts[(group, l)] = [_add_half(h, r, ac, name="rs_core_add") for h, r in zip(halves_of(group, l), received)]

    small = {n: [None] * L for n in ("na_rpb", "swa_sink", "mla_q_norm", "mla_kv_norm", "gqa_q_norm", "gqa_k_norm",
                                     "ln1_g", "ln1_b", "ffn_conv_w", "ffn_conv_b", "ln2_g", "ln2_b")}
    dmod = [None] * L
    for l in reversed(range(L)):
        sv, md = saved[l], mods[l]
        gq, gk, mq, mkv = row(w["gqa_q_norm"][l]), row(w["gqa_k_norm"][l]), row(w["mla_q_norm"][l]), row(w["mla_kv_norm"][l])
        cb_row = row(w["ffn_conv_b"][l])
        dx1, dz2, acc_r2 = _res_bwd(sv["x1"], sv["z2"], md[5], row(w["ln2_g"][l]), dx, S, name="res_bwd")
        dact = _mm(dz2, full["ffn_w_down"][l], mode="nt", out_dtype=BF16, name="ffn_out_dx")
        wgrad[l]["ffn_w_down"] = _mm(sv["act"], dz2, mode="tn", name="ffn_out_dw").reshape(4, F // 4, D)
        da, du, acc_f = _ffn_bwd_act(sv["gp"], sv["up"], dact, conv_w[l], cb_row, S, name="ffn_mid_bwd")
        dg = _ffn_bwd_conv(da, conv_w[l], S, name="ffn_conv_bwd")
        dh2 = _mm(dg, full["ffn_w_gate"][l], mode="nt", name="ffn_in_dx")
        dh2 = _mm(du, full["ffn_w_up"][l], mode="nt", add=dh2, name="ffn_in_dx_add")
        wgrad[l]["ffn_w_gate"] = _mm(sv["h2"], dg, mode="tn", stack=(1, 0, None), split4=True,
                                     name="ffn_in_dw").reshape(4, D, F // 4)
        wgrad[l]["ffn_w_up"] = _mm(sv["h2"], du, mode="tn", stack=(1, 0, None), split4=True,
                                   name="ffn_in_dw").reshape(4, D, F // 4)
        dx1, acc_m2 = _mod_bwd(sv["x1"], dh2, md[4], dx1, S, name="mod_bwd")
        dxa, dz1, acc_r1 = _res_bwd(sv["x"], sv["z1"], md[2], row(w["ln1_g"][l]), dx1, S, name="res_bwd")
        dmix = _mm(dz1, full["w_out"][l], mode="nt", out_dtype=BF16, name="out_proj_dx")
        wgrad[l]["w_out"] = _mm(sv["mix"], dz1, mode="tn", rows=dz1.shape[0], name="out_proj_dw").reshape(4, -1, D)

        qkv, qmb, kvm = sv["qkv"], sv["qmb"], sv["kvm"]
        cfg_na, cfg_swa, cfg_mla, cfg_gqa = sv["cfgs"]
        rest_above = l + 1 < L
        rides = [_Exchange(halves_of("ffn", l))] + ([_Exchange(halves_of("rest", l + 1))] if rest_above else [])
        dq_a, dk_a, dv_a, dbias, received = _attn_bwd(cfg_na, qkv, CB_NA_Q, qkv, CB_NA_K, qkv, CB_NA_V, dmix, 0,
                                                      bias=sv["bias_na"], want_dbias=True, rides=rides, name="na_bwd")
        add_halves("ffn", l, received[0])
        if rest_above:
            add_halves("rest", l + 1, received[1])
        dq_b, dk_b, dv_b, dsink = _attn_bwd(cfg_swa, qkv, CB_SWA_Q, qkv, CB_SWA_K, qkv, CB_SWA_V, dmix, NA_HEADS,
                                            bias=swa_mask, sink=sv["sink"], name="swa_bwd")
        dq_c, dk_c, dv_c, dq2_c, dk2_c, got = _attn_bwd(
            cfg_mla, qmb, 0, kvm, 0, kvm, MLA_HEADS, dmix, NA_HEADS + SWA_HEADS, q2=qmb, q2_cb=MLA_HEADS, k2=qkv,
            k2_cb=CB_KPE, rides=[_Scatter(parts[("ffn", l)])], name="mla_bwd")
        landed[("ffn", l)] = got[0]
        rides = [_Scatter(parts[("rest", l + 1)])] if rest_above else []
        gqa_out = _attn_bwd(cfg_gqa, qkv, CB_GQA_Q, qkv, CB_GQA_K, qkv, CB_GQA_V, dmix,
                            NA_HEADS + SWA_HEADS + MLA_HEADS, rides=rides, name="gqa_bwd")
        dq_d, dk_d, dv_d = gqa_out[:3]
        if rest_above:
            landed[("rest", l + 1)] = gqa_out[3][0]
        dqm = _pe_rope(jnp.concatenate([dq_c, dq2_c], axis=1), tabs, S, -1.0, BF16, name="mla_q_rope_bwd")
        dkvm = jnp.concatenate([dk_c, dv_c], axis=1).astype(BF16)
        dcq = _mm(dqm, w_uq_p[l], mode="nt", name="mla_uq_dx")
        dckv = _mm(dkvm, w_ukv_p[l], mode="nt", name="mla_ukv_dx")
        cqn = qkv[:, CB_CQ * LANE:CB_CKV * LANE]
        ckvn = qkv[:, CB_CKV * LANE:(CB_CKV + 1) * LANE]
        d_uq = _unperm_w_uq(_mm(cqn, dqm, mode="tn", name="mla_uq_dw"))
        d_ukv = _unperm_w_ukv(_mm(ckvn, dkvm, mode="tn", name="mla_ukv_dw"))
        grads = {}
        for h in range(NA_HEADS):
            grads[CB_NA_Q + h], grads[CB_NA_K + h], grads[CB_NA_V + h] = (dq_a, h), (dk_a, h), (dv_a, h)
        for h in range(SWA_HEADS):
            grads[CB_SWA_Q + h] = (dq_b, h)
        for h in range(SWA_KV_HEADS):
            grads[CB_SWA_K + h], grads[CB_SWA_V + h] = (dk_b, h), (dv_b, h)
        for h in range(GQA_HEADS):
            grads[CB_GQA_Q + h] = (dq_d, h)
        for h in range(GQA_KV_HEADS):
            grads[CB_GQA_K + h], grads[CB_GQA_V + h] = (dk_d, h), (dv_d, h)
        grads[CB_KPE], grads[CB_CQ], grads[CB_CKV] = (dk2_c, 0), (dcq, 0), (dckv, 0)
        dp, acc_p = _prep_bwd(sv["p"], grads, tabs, gq, gk, mq, mkv, S, name="prep_bwd")
        dh1 = _mm(dp, w_in_p[l], name="in_proj_dx")
        d_in = _unperm_w_in(_mm(dp, sv["h1"], mode="tn", name="in_proj_dw")).reshape(4, IN_COLS // 4, D)
        dx, acc_m1 = _mod_bwd(sv["x"], dh1, md[1], dxa, S, name="mod_bwd")

        to4 = lambda g: g.reshape(g.shape[0], 4, g.shape[1] // 4).transpose(1, 0, 2)
        wgrad[l]["w_in"], wgrad[l]["mla_w_uq"], wgrad[l]["mla_w_ukv"] = d_in, to4(d_uq), to4(d_ukv)
        dmod[l] = jnp.stack([acc_m1[0:2], acc_m1[2:4], acc_r1[0:2], acc_m2[0:2], acc_m2[2:4], acc_r2[0:2]])
        rpb_vjp = jax.vjp(lambda r: _na_bias(r, S), w["na_rpb"][l])[1]
        small["na_rpb"][l] = rpb_vjp(dbias)[0]
        small["swa_sink"][l] = dsink[:, :, 0].reshape(SWA_KV_HEADS, -1, BQ).sum(axis=-1).reshape(-1)
        small["gqa_q_norm"][l], small["gqa_k_norm"][l] = acc_p[0, :LANE], acc_p[1, :LANE]
        small["mla_q_norm"][l], small["mla_kv_norm"][l] = acc_p[2], acc_p[3, :LANE]
        small["ln1_g"][l], small["ln1_b"][l] = acc_r1[2], acc_r1[3]
        small["ln2_g"][l], small["ln2_b"][l] = acc_r2[2], acc_r2[3]
        small["ffn_conv_w"][l], small["ffn_conv_b"][l] = acc_f[0:3], acc_f[3]
    grad_x = dx[:S][None]

    dmod_x = jnp.stack([dmod[l][:, 0].reshape(-1) for l in range(L)])
    dmod_c = jnp.stack([dmod[l][:, 1].reshape(-1) for l in range(L)])
    small_names = tuple(small)
    bucket = [dmod_x, dmod_c] + [jnp.stack(small[n]) for n in small_names]
    (b8,) = _allgather8([_pack(bucket)], name="gather_small")
    tot = _unpack(_sum_lead(b8, F32, name="sum_small"), bucket)
    dmod_x_all = b8.reshape(8, -1)[:, :dmod_x.size].reshape(8, L, 6 * D)
    dmod_c_tot = tot[1]
    g_small = dict(zip(small_names, tot[2:]))
    g_small["b_ada"] = tot[0] + dmod_c_tot
    g_small["ffn_conv_w"] = lax.dynamic_slice_in_dim(g_small["ffn_conv_w"], chip * (F // 4), F // 4, axis=2)

    dmod16 = jnp.concatenate([dmod_x_all, jnp.broadcast_to(dmod_c_tot[None], (8, L, 6 * D))], axis=0) * row_keep[:, :, None]
    dmod16 = lax.dynamic_slice_in_dim(dmod16, chip * n_ada, n_ada, axis=2)
    g_ada, dsc = None, None
    for l in range(L):
        g_ada = _mm(sc, dmod16[:, l], mode="tn", exact=True, stack=(L, l, g_ada), name="ada_dw")
        dsc = _mm(dmod16[:, l], w["w_ada"], b_layer=l, mode="nt", add=dsc, name="ada_dx" if dsc is None else "ada_dx_add")
    (dsc8,) = _allgather8([dsc[8:16]], name="gather_dsc")
    dsc4 = dsc8[::2, 0]
    g_small["c_ctx"] = (((dsc4[0] + dsc4[1]) + dsc4[2]) + dsc4[3]) * _silu_grad(w["c_ctx"])

    sums = {}

    def sum_group(group, l):
        for n, p, got in zip(groups[group], parts[(group, l)], landed[(group, l)]):
            sums[n] = _sum_parts(p, got, chip, ac, (L, l, sums.get(n)), name="rs_chip_sum")

    add_halves("rest", 0, _run_rides([_Exchange(halves_of("rest", 0))], name="rs_core_exchange")[0])
    for l in range(L):
        sum_group("ffn", l)
        if l > 0:
            sum_group("rest", l)
    grad, delta, new_m, new_v = {}, {}, {}, {}
    (delta["w_ada"], new_m["w_ada"], new_v["w_ada"]), got = _adam(
        w["w_ada"], g_ada, m_in["w_ada"], v_in["w_ada"],
        rides=[_Scatter(parts[("rest", 0)]), _Join([sums[n] for n in groups["ffn"]])], name="adam")
    landed[("rest", 0)] = got[0]
    joined = dict(zip(groups["ffn"], got[1]))
    sum_group("rest", 0)
    joined.update(zip(groups["rest"], _run_rides([_Join([sums[n] for n in groups["rest"]])], name="rs_join")[0]))
    g_big = {n: j.reshape(L, 2 * j.shape[2], j.shape[3]) for n, j in joined.items()}
    g_big["w_ada"] = g_ada

    grad["w_ada"] = g_ada
    for n in _BIG:
        delta[n], new_m[n], new_v[n], grad[n] = _adam(w[n], g_big[n], m_in[n], v_in[n], emit_grad=True, name="adam")
    like = [w[n] for n in _SMALL]
    packed = [_pack([src[n].reshape(w[n].shape) for n in _SMALL]) for src in (w, g_small, m_in, v_in)]
    d_s, m_s, v_s = _adam(*packed, name="adam_small")
    for n, g_, d_, m_, v_ in zip(_SMALL, _unpack(packed[1], like), _unpack(d_s, like), _unpack(m_s, like), _unpack(v_s, like)):
        grad[n], delta[n], new_m[n], new_v[n] = g_, d_, m_, v_

    for d in (grad, delta, new_m, new_v):
        d["w_in"] = jnp.swapaxes(d["w_in"], 1, 2)
    return (loss, grad_x, *[grad[n] for n in _WEIGHTS], *[delta[n] for n in _WEIGHTS],
            *[new_m[n] for n in _WEIGHTS], *[new_v[n] for n in _WEIGHTS])


def kernel(x, c, ctx, c_ctx, w_ada, b_ada, w_in, na_rpb, swa_sink, mla_q_norm, mla_kv_norm, mla_w_uq, mla_w_ukv, gqa_q_norm, gqa_k_norm, w_out, ln1_g, ln1_b, ffn_w_gate, ffn_w_up, ffn_conv_w, ffn_conv_b, ffn_w_down, ln2_g, ln2_b, loss_target, m_c_ctx, m_w_ada, m_b_ada, m_w_in, m_na_rpb, m_swa_sink, m_mla_q_norm, m_mla_kv_norm, m_mla_w_uq, m_mla_w_ukv, m_gqa_q_norm, m_gqa_k_norm, m_w_out, m_ln1_g, m_ln1_b, m_ffn_w_gate, m_ffn_w_up, m_ffn_conv_w, m_ffn_conv_b, m_ffn_w_down, m_ln2_g, m_ln2_b, v_c_ctx, v_w_ada, v_b_ada, v_w_in, v_na_rpb, v_swa_sink, v_mla_q_norm, v_mla_kv_norm, v_mla_w_uq, v_mla_w_ukv, v_gqa_q_norm, v_gqa_k_norm, v_w_out, v_ln1_g, v_ln1_b, v_ffn_w_gate, v_ffn_w_up, v_ffn_conv_w, v_ffn_conv_b, v_ffn_w_down, v_ln2_g, v_ln2_b):
    args = locals()
    w = {n: args[n] for n in _WEIGHTS}
    m_in = {n: args["m_" + n] for n in _WEIGHTS}
    v_in = {n: args["v_" + n] for n in _WEIGHTS}
    return _train_step(x, c, ctx, loss_target, w, m_in, v_in)
```

```python
import functools
import math

import numpy as np
import jax
import jax.numpy as jnp
from jax import lax
from jax.experimental import pallas as pl
from jax.experimental.pallas import tpu as pltpu

F32 = jnp.float32
BF16 = jnp.bfloat16
MESH = pl.DeviceIdType.MESH

GRID_W = 64
HEAD_DIM = 128
NA_HEADS, NA_WIN_R, NA_WIN_C = 4, 8, 16
SWA_HEADS, SWA_KV_HEADS, SWA_WINDOW = 4, 2, 128
MLA_HEADS, MLA_Q_LORA, MLA_KV_LORA, MLA_NOPE, MLA_ROPE, MLA_V = 4, 384, 128, 128, 64, 128
GQA_HEADS, GQA_KV_HEADS = 4, 2
ROPE_THETA = 10000.0
EPS = 1e-6
NEG = -1e30
DEPTH = 2
DEEPNORM_ALPHA = (2 * DEPTH) ** 0.25
ADAM_LR, ADAM_B1, ADAM_B2, ADAM_EPS, ADAM_WD, ADAM_STEP = 0.001, 0.9, 0.999, 1e-08, 0.01, 10

LANE = 128
V7X_VMEM_BYTES = 64 * 1024 * 1024
VMEM_LIMIT = 56 * 1024 * 1024
MM_VMEM_BUDGET = 40 * 1024 * 1024
EW_VMEM_BUDGET = 28 * 1024 * 1024
BQ = 128

CB_NA_Q, CB_NA_K, CB_NA_V = 0, 4, 8
CB_SWA_Q, CB_SWA_K, CB_SWA_V = 12, 16, 18
CB_CQ, CB_CKV = 20, 23
CB_GQA_Q, CB_GQA_K, CB_GQA_V = 24, 28, 30
CB_KPE = 32
PCOLS = 33 * LANE
IN_COLS = 4160


def _cparams(sem=None, **kw):
    return pltpu.CompilerParams(dimension_semantics=sem, vmem_limit_bytes=VMEM_LIMIT, **kw)


def _pick(n, target, mult=LANE):
    best = None
    for d in range(mult, min(n, target) + 1, mult):
        if n % d == 0:
            best = d
    return n if best is None else best


def _mm(a, b, *, mode="nn", out_dtype=F32, a_off=0, a_k=None, tm=1408, tn=1408, tk=2816, exact=False, add=None,
        stack=None, split4=False, rows=None, rides=(), b_layer=None, name):
    b_shape = b.shape if b_layer is None else b.shape[1:]
    if mode == "tn":
        K, M = a.shape
        K2, N = b_shape
    elif mode == "nn":
        M, K = a.shape
        K2, N = b_shape
    else:
        M, K = a.shape
        N, K2 = b_shape
    if a_k is not None:
        K = a_k
    if rows is not None:
        if mode == "tn":
            assert rows <= min(K, K2)
            K = K2 = rows
        else:
            assert rows <= M
            M = rows
    assert K == K2, (a.shape, b.shape, mode)
    m_mult = LANE if mode == "tn" else 16
    n_cols = N // 4 if split4 else N
    bm, bn, bk = _pick(M, tm, m_mult), _pick(n_cols, tn), _pick(K, tk)
    sa, sb, so = a.dtype.itemsize, b.dtype.itemsize, jnp.dtype(out_dtype).itemsize

    def vmem_estimate():
        acc = bm * bn * 4 if K // bk > 1 else 0
        return 2 * (bm * bk * sa + bk * bn * sb) + acc + 2 * bm * bn * so + (2 * bm * bn * 4 if add is not None else 0)

    while vmem_estimate() > MM_VMEM_BUDGET:
        if bm >= bn and _pick(M, bm - 1, m_mult) < bm:
            bm = _pick(M, bm - 1, m_mult)
        elif _pick(n_cols, bn - 1) < bn:
            bn = _pick(n_cols, bn - 1)
        else:
            assert _pick(K, bk - 1) < bk, "no tiling fits VMEM"
            bk = _pick(K, bk - 1)
    assert a_off % bk == 0
    koff = a_off // bk
    nk = K // bk
    if mode == "tn":
        a_spec = pl.BlockSpec((bk, bm), lambda i, j, k: (k, i))
        b_spec = pl.BlockSpec((bk, bn), lambda i, j, k: (k, j))
        dims = (((0,), (0,)), ((), ()))
    elif mode == "nn":
        a_spec = pl.BlockSpec((bm, bk), lambda i, j, k: (i, k + koff))
        b_spec = pl.BlockSpec((bk, bn), lambda i, j, k: (k, j))
        dims = (((1,), (0,)), ((), ()))
    else:
        a_spec = pl.BlockSpec((bm, bk), lambda i, j, k: (i, k + koff))
        b_spec = pl.BlockSpec((bn, bk), lambda i, j, k: (j, k))
        dims = (((1,), (1,)), ((), ()))
    if b_layer is not None:
        b_block, b_index = b_spec.block_shape, b_spec.index_map
        b_spec = pl.BlockSpec((None,) + tuple(b_block), lambda i, j, k: (b_layer,) + tuple(b_index(i, j, k)))

    operands = [a, b]
    in_specs = [a_spec, b_spec]
    if add is not None:
        operands.append(add)
        in_specs.append(pl.BlockSpec((bm, bn), lambda i, j, k: (i, j)))
    aliases = {}
    if stack is None:
        out_spec = pl.BlockSpec((bm, bn), lambda i, j, k: (i, j))
        out_shape = jax.ShapeDtypeStruct((M, N), out_dtype)
    else:
        n_layers, layer, buf = stack
        if split4:
            nb = N // 4 // bn
            assert N % (4 * bn) == 0
            out_spec = pl.BlockSpec((None, None, bm, bn), lambda i, j, k: (layer, j // nb, i, j % nb))
            out_shape = jax.ShapeDtypeStruct((n_layers, 4, M, N // 4), out_dtype)
        else:
            out_spec = pl.BlockSpec((None, bm, bn), lambda i, j, k: (layer, i, j))
            out_shape = jax.ShapeDtypeStruct((n_layers, M, N), out_dtype)
        if buf is not None:
            aliases = {len(operands): 0}
            operands.append(buf)
            in_specs.append(pl.BlockSpec(memory_space=pl.ANY))
    has_add, has_buf = add is not None, bool(aliases)
    n_ride = sum(len(r.arrays) for r in rides)
    aliases.update(_ride_aliases(rides, len(operands), 1))
    grid = (M // bm, N // bn, nk)

    def body(*refs):
        a_ref, b_ref = refs[:2]
        add_ref = refs[2] if has_add else None
        base = 2 + has_add + has_buf
        o_ref = refs[base + n_ride]
        scratch = refs[base + 2 * n_ride + 1:]
        if rides:
            ride_groups = _ride_split(rides, refs[base:base + n_ride], refs[base + n_ride + 1:base + 2 * n_ride + 1],
                                      scratch[1 if nk > 1 else 0:])
            steps = [pl.program_id(d) for d in range(3)]
            pl.when((steps[0] == 0) & (steps[1] == 0) & (steps[2] == 0))(lambda: _ride_start(rides, ride_groups))
        compute(a_ref, b_ref, add_ref, o_ref, scratch[0] if nk > 1 else None)
        if rides:
            pl.when((steps[0] == grid[0] - 1) & (steps[1] == grid[1] - 1) & (steps[2] == grid[2] - 1))(
                lambda: _ride_wait(rides, ride_groups))

    def compute(a_ref, b_ref, add_ref, o_ref, acc_ref):
        if exact:
            prod = lax.dot_general(a_ref[...].astype(F32), b_ref[...].astype(F32), dims,
                                   precision=lax.Precision.HIGHEST, preferred_element_type=F32)
        else:
            prod = lax.dot_general(a_ref[...].astype(BF16), b_ref[...].astype(BF16), dims, preferred_element_type=F32)

        def finish(res):
            if has_add:
                res = res + add_ref[...].astype(F32)
            o_ref[...] = res.astype(o_ref.dtype)

        if nk == 1:
            finish(prod)
            return
        k = pl.program_id(2)

        @pl.when(k == 0)
        def _():
            acc_ref[...] = prod

        @pl.when((k > 0) & (k < nk - 1))
        def _():
            acc_ref[...] += prod

        @pl.when(k == nk - 1)
        def _():
            finish(acc_ref[...] + prod)

    outs = pl.pallas_call(
        body, name=name, grid=grid,
        in_specs=in_specs + [_ANY] * n_ride, out_specs=[out_spec] + [_ANY] * n_ride,
        out_shape=[out_shape] + [s for r in rides for s in r.out_shapes()],
        scratch_shapes=([pltpu.VMEM((bm, bn), F32)] if nk > 1 else []) + _ride_scratch(rides),
        input_output_aliases=aliases,
        compiler_params=_cparams(("arbitrary",) * 3 if rides else ("parallel", "parallel", "arbitrary")),
    )(*operands, *[a for r in rides for a in r.arrays])
    if not rides:
        return outs[0]
    return outs[0], _ride_outputs(rides, outs[1:])


def _row_block(T, S):
    return _pick(math.gcd(T, S), 256, 16)


def _ln_stats(x):
    mu = jnp.mean(x, axis=-1, keepdims=True)
    xc = x - mu
    var = jnp.mean(xc * xc, axis=-1, keepdims=True)
    rstd = lax.rsqrt(var + EPS)
    return xc * rstd, rstd


def _ln_bwd(dxhat, xhat, rstd):
    m1 = jnp.mean(dxhat, axis=-1, keepdims=True)
    m2 = jnp.mean(dxhat * xhat, axis=-1, keepdims=True)
    return rstd * (dxhat - m1 - xhat * m2)


def _sel(ref, is_ctx):
    return jnp.where(is_ctx, ref[1:2, :], ref[0:1, :])


def _mod_fwd(x, shift, scale, S, *, rows=None, name):
    T, D = (x.shape[0] if rows is None else rows), x.shape[1]
    bt = _row_block(T, S)

    def body(x_ref, sh_ref, sc_ref, o_ref):
        is_ctx = pl.program_id(0) * bt >= S
        xhat, _ = _ln_stats(x_ref[...])
        o_ref[...] = (xhat * (1.0 + _sel(sc_ref, is_ctx)) + _sel(sh_ref, is_ctx)).astype(o_ref.dtype)

    return pl.pallas_call(
        body, name=name, grid=(T // bt,),
        in_specs=[pl.BlockSpec((bt, D), lambda i: (i, 0)), pl.BlockSpec((2, D), lambda i: (0, 0)),
                  pl.BlockSpec((2, D), lambda i: (0, 0))],
        out_specs=pl.BlockSpec((bt, D), lambda i: (i, 0)),
        out_shape=jax.ShapeDtypeStruct((T, D), BF16),
        compiler_params=_cparams(("parallel",)),
    )(x, shift, scale)


def _acc_groups(acc_ref, row, val, is_ctx):
    f = jnp.where(is_ctx, 1.0, 0.0).astype(F32)
    acc_ref[row:row + 1, :] += val * (1.0 - f)
    acc_ref[row + 1:row + 2, :] += val * f


def _mod_bwd(x, dh, scale, dx_in, S, *, name):
    T, D = dh.shape
    bt = _row_block(T, S)
    in_blocks = dx_in.shape[0] // bt

    def body(x_ref, dh_ref, sc_ref, dxi_ref, dx_ref, acc_ref):
        i = pl.program_id(0)
        is_ctx = i * bt >= S

        @pl.when(i == 0)
        def _():
            acc_ref[...] = jnp.zeros_like(acc_ref)

        xhat, rstd = _ln_stats(x_ref[...])
        dh = dh_ref[...].astype(F32)
        dxhat = dh * (1.0 + _sel(sc_ref, is_ctx))
        dxi = dxi_ref[...] if in_blocks * bt == T else jnp.where(i < in_blocks, dxi_ref[...], 0.0)
        dx_ref[...] = dxi + _ln_bwd(dxhat, xhat, rstd)
        _acc_groups(acc_ref, 0, jnp.sum(dh, axis=0, keepdims=True), is_ctx)
        _acc_groups(acc_ref, 2, jnp.sum(dh * xhat, axis=0, keepdims=True), is_ctx)

    return pl.pallas_call(
        body, name=name, grid=(T // bt,),
        in_specs=[pl.BlockSpec((bt, D), lambda i: (i, 0)), pl.BlockSpec((bt, D), lambda i: (i, 0)),
                  pl.BlockSpec((2, D), lambda i: (0, 0)),
                  pl.BlockSpec((bt, D), lambda i: (jnp.minimum(i, in_blocks - 1), 0))],
        out_specs=[pl.BlockSpec((bt, D), lambda i: (i, 0)), pl.BlockSpec((8, D), lambda i: (0, 0))],
        out_shape=[jax.ShapeDtypeStruct((T, D), F32), jax.ShapeDtypeStruct((8, D), F32)],
        compiler_params=_cparams(("arbitrary",)),
    )(x, dh, scale, dx_in)


def _res_fwd(x, z, gate, lg, lb, S, *, name):
    T, D = z.shape
    bt = _row_block(T, S)

    def body(x_ref, z_ref, g_ref, lg_ref, lb_ref, o_ref):
        is_ctx = pl.program_id(0) * bt >= S
        u = DEEPNORM_ALPHA * x_ref[...] + _sel(g_ref, is_ctx) * z_ref[...]
        uhat, _ = _ln_stats(u)
        o_ref[...] = uhat * lg_ref[...] + lb_ref[...]

    row = pl.BlockSpec((bt, D), lambda i: (i, 0))
    return pl.pallas_call(
        body, name=name, grid=(T // bt,),
        in_specs=[row, row, pl.BlockSpec((2, D), lambda i: (0, 0)), pl.BlockSpec((1, D), lambda i: (0, 0)),
                  pl.BlockSpec((1, D), lambda i: (0, 0))],
        out_specs=row,
        out_shape=jax.ShapeDtypeStruct((T, D), F32),
        compiler_params=_cparams(("parallel",)),
    )(x, z, gate, lg, lb)


def _res_bwd(x, z, gate, lg, dy, S, *, name):
    T, D = z.shape
    bt = _row_block(T, S)

    def body(x_ref, z_ref, g_ref, lg_ref, dy_ref, dx_ref, dz_ref, acc_ref):
        i = pl.program_id(0)
        is_ctx = i * bt >= S

        @pl.when(i == 0)
        def _():
            acc_ref[...] = jnp.zeros_like(acc_ref)

        gate_v = _sel(g_ref, is_ctx)
        zv = z_ref[...]
        u = DEEPNORM_ALPHA * x_ref[...] + gate_v * zv
        uhat, rstd = _ln_stats(u)
        dyv = dy_ref[...]
        du = _ln_bwd(dyv * lg_ref[...], uhat, rstd)
        dx_ref[...] = DEEPNORM_ALPHA * du
        dz_ref[...] = (gate_v * du).astype(dz_ref.dtype)
        _acc_groups(acc_ref, 0, jnp.sum(du * zv, axis=0, keepdims=True), is_ctx)
        acc_ref[2:3, :] += jnp.sum(dyv * uhat, axis=0, keepdims=True)
        acc_ref[3:4, :] += jnp.sum(dyv, axis=0, keepdims=True)

    row = pl.BlockSpec((bt, D), lambda i: (i, 0))
    return pl.pallas_call(
        body, name=name, grid=(T // bt,),
        in_specs=[row, row, pl.BlockSpec((2, D), lambda i: (0, 0)), pl.BlockSpec((1, D), lambda i: (0, 0)), row],
        out_specs=[row, row, pl.BlockSpec((8, D), lambda i: (0, 0))],
        out_shape=[jax.ShapeDtypeStruct((T, D), F32), jax.ShapeDtypeStruct((T, D), BF16),
                   jax.ShapeDtypeStruct((8, D), F32)],
        compiler_params=_cparams(("arbitrary",)),
    )(x, z, gate, lg, dy)


def _loss_fwd_bwd(y, target, S, *, name):
    T, D = y.shape
    bt = _row_block(T, S)
    n_lat = S // bt

    def body(y_ref, t_ref, dy_ref, l_ref):
        i = pl.program_id(0)

        @pl.when(i == 0)
        def _():
            l_ref[...] = jnp.zeros_like(l_ref)

        keep = jnp.where(i * bt >= S, 0.0, 1.0).astype(F32)
        err = (y_ref[...] - t_ref[...]) * keep
        dy_ref[...] = err * (1.0 / D)
        l_ref[...] += jnp.sum(err * err) * (0.5 / D)

    return pl.pallas_call(
        body, name=name, grid=(T // bt,),
        in_specs=[pl.BlockSpec((bt, D), lambda i: (i, 0)),
                  pl.BlockSpec((bt, D), lambda i: (jnp.minimum(i, n_lat - 1), 0))],
        out_specs=[pl.BlockSpec((bt, D), lambda i: (i, 0)), pl.BlockSpec((8, LANE), lambda i: (0, 0))],
        out_shape=[jax.ShapeDtypeStruct((T, D), F32), jax.ShapeDtypeStruct((8, LANE), F32)],
        compiler_params=_cparams(("arbitrary",)),
    )(y, target)


def _rope_tables(S, C, dim):
    half = dim // 4
    t = jnp.arange(S)
    row = (t // GRID_W).astype(F32)
    col = (t % GRID_W).astype(F32)
    inv = ROPE_THETA ** (-jnp.arange(half, dtype=F32) / half)
    ar, ac = row[:, None] * inv[None, :], col[:, None] * inv[None, :]
    cos = jnp.concatenate([jnp.cos(ar), jnp.cos(ar), jnp.cos(ac), jnp.cos(ac)], axis=1)
    ss = jnp.concatenate([-jnp.sin(ar), jnp.sin(ar), -jnp.sin(ac), jnp.sin(ac)], axis=1)
    cos = jnp.pad(cos, ((0, C), (0, LANE - dim)), constant_values=1.0)
    ss = jnp.pad(ss, ((0, C), (0, LANE - dim)))
    return cos, ss


def _rope(x, cos, ss, half):
    lane = lax.broadcasted_iota(jnp.int32, x.shape, 1)
    first = (lane % (2 * half)) < half
    partner = jnp.where(first, pltpu.roll(x, LANE - half, 1), pltpu.roll(x, half, 1))
    return x * cos + partner * ss


def _rms(x):
    r = lax.rsqrt(jnp.mean(x * x, axis=-1, keepdims=True) + EPS)
    return x * r, r


_CAST_BLOCKS = tuple(range(0, 12)) + (18, 19, 30, 31)
_ROPE_BLOCKS = tuple(range(12, 18))
_GQA_Q_BLOCKS = tuple(range(24, 28))
_GQA_K_BLOCKS = (28, 29)


def _prep_fwd(p, tabs, gq, gk, mq, mkv, S, *, name):
    T = p.shape[0]
    bt = _row_block(T, S)
    cA, sA, cP, sP = tabs

    def body(p_ref, cA_ref, sA_ref, cP_ref, sP_ref, gq_ref, gk_ref, mq_ref, mkv_ref, o_ref):
        def blk(b):
            return p_ref[:, b * LANE:(b + 1) * LANE].astype(F32)

        def put(b, val):
            o_ref[:, b * LANE:(b + 1) * LANE] = val.astype(o_ref.dtype)

        cA_v, sA_v = cA_ref[...], sA_ref[...]
        for b in _CAST_BLOCKS:
            put(b, blk(b))
        for b in _ROPE_BLOCKS:
            put(b, _rope(blk(b), cA_v, sA_v, 32))
        for b in _GQA_Q_BLOCKS:
            put(b, _rope(_rms(blk(b))[0] * gq_ref[...], cA_v, sA_v, 32))
        for b in _GQA_K_BLOCKS:
            put(b, _rope(_rms(blk(b))[0] * gk_ref[...], cA_v, sA_v, 32))
        put(CB_KPE, _rope(blk(CB_KPE), cP_ref[...], sP_ref[...], 16))
        cq = p_ref[:, CB_CQ * LANE:CB_CKV * LANE].astype(F32)
        o_ref[:, CB_CQ * LANE:CB_CKV * LANE] = (_rms(cq)[0] * mq_ref[...]).astype(o_ref.dtype)
        put(CB_CKV, _rms(blk(CB_CKV))[0] * mkv_ref[...])

    row128 = pl.BlockSpec((bt, LANE), lambda i: (i, 0))
    vec = lambda n: pl.BlockSpec((1, n), lambda i: (0, 0))
    return pl.pallas_call(
        body, name=name, grid=(T // bt,),
        in_specs=[pl.BlockSpec((bt, PCOLS), lambda i: (i, 0)), row128, row128, row128, row128,
                  vec(LANE), vec(LANE), vec(MLA_Q_LORA), vec(LANE)],
        out_specs=pl.BlockSpec((bt, PCOLS), lambda i: (i, 0)),
        out_shape=jax.ShapeDtypeStruct((T, PCOLS), BF16),
        compiler_params=_cparams(("parallel",)),
    )(p, cA, sA, cP, sP, gq, gk, mq, mkv)


def _prep_bwd(p, grads, tabs, gq, gk, mq, mkv, S, *, name):
    T = p.shape[0]
    bt = _row_block(T, S)
    cA, sA, cP, sP = tabs
    arrays = []
    where = {}
    for key, (arr, cb) in grads.items():
        idx = next((n for n, a in enumerate(arrays) if a is arr), None)
        if idx is None:
            arrays.append(arr)
            idx = len(arrays) - 1
        where[key] = (idx, cb)
    ng = len(arrays)

    def body(*refs):
        p_ref, cA_ref, sA_ref, cP_ref, sP_ref, gq_ref, gk_ref, mq_ref, mkv_ref = refs[:9]
        g_refs = refs[9:9 + ng]
        o_ref, acc_ref = refs[9 + ng:]
        i = pl.program_id(0)

        @pl.when(i == 0)
        def _():
            acc_ref[...] = jnp.zeros_like(acc_ref)

        def blk(b):
            return p_ref[:, b * LANE:(b + 1) * LANE].astype(F32)

        def grad(b, width=LANE):
            idx, cb = where[b]
            return g_refs[idx][:, cb * LANE:cb * LANE + width].astype(F32)

        def put(b, val):
            o_ref[:, b * LANE:(b + 1) * LANE] = val.astype(o_ref.dtype)

        def rms_bwd(x, dy, g, row, width):
            n, r = _rms(x)
            acc_ref[row:row + 1, 0:width] += jnp.sum(dy * n, axis=0, keepdims=True)
            dn = dy * g
            return r * (dn - n * jnp.mean(dn * n, axis=-1, keepdims=True))

        cA_v, sA_v = cA_ref[...], sA_ref[...]
        for b in _CAST_BLOCKS:
            put(b, grad(b))
        for b in _ROPE_BLOCKS:
            put(b, _rope(grad(b), cA_v, -sA_v, 32))
        for b in _GQA_Q_BLOCKS:
            put(b, rms_bwd(blk(b), _rope(grad(b), cA_v, -sA_v, 32), gq_ref[...], 0, LANE))
        for b in _GQA_K_BLOCKS:
            put(b, rms_bwd(blk(b), _rope(grad(b), cA_v, -sA_v, 32), gk_ref[...], 1, LANE))
        put(CB_KPE, _rope(grad(CB_KPE), cP_ref[...], -sP_ref[...], 16))
        dcq = rms_bwd(p_ref[:, CB_CQ * LANE:CB_CKV * LANE].astype(F32), grad(CB_CQ, MLA_Q_LORA), mq_ref[...], 2, MLA_Q_LORA)
        o_ref[:, CB_CQ * LANE:CB_CKV * LANE] = dcq.astype(o_ref.dtype)
        put(CB_CKV, rms_bwd(blk(CB_CKV), grad(CB_CKV), mkv_ref[...], 3, LANE))

    row128 = pl.BlockSpec((bt, LANE), lambda i: (i, 0))
    vec = lambda n: pl.BlockSpec((1, n), lambda i: (0, 0))
    g_specs = [pl.BlockSpec((bt, a.shape[1]), lambda i: (i, 0)) for a in arrays]
    return pl.pallas_call(
        body, name=name, grid=(T // bt,),
        in_specs=[pl.BlockSpec((bt, PCOLS), lambda i: (i, 0)), row128, row128, row128, row128,
                  vec(LANE), vec(LANE), vec(MLA_Q_LORA), vec(LANE)] + g_specs,
        out_specs=[pl.BlockSpec((bt, PCOLS), lambda i: (i, 0)), pl.BlockSpec((8, MLA_Q_LORA), lambda i: (0, 0))],
        out_shape=[jax.ShapeDtypeStruct((T, PCOLS), BF16), jax.ShapeDtypeStruct((8, MLA_Q_LORA), F32)],
        compiler_params=_cparams(("arbitrary",)),
    )(p, cA, sA, cP, sP, gq, gk, mq, mkv, *arrays)


def _pe_rope(qm, tabs, S, sign, out_dtype, *, name):
    T, N = qm.shape
    bt = _row_block(T, S)
    cP, sP = tabs[2], tabs[3]

    def body(x_ref, c_ref, s_ref, o_ref):
        for b in range(MLA_HEADS):
            o_ref[:, b * LANE:(b + 1) * LANE] = x_ref[:, b * LANE:(b + 1) * LANE].astype(o_ref.dtype)
        for b in range(MLA_HEADS, 2 * MLA_HEADS):
            x = x_ref[:, b * LANE:(b + 1) * LANE].astype(F32)
            o_ref[:, b * LANE:(b + 1) * LANE] = _rope(x, c_ref[...], sign * s_ref[...], 16).astype(o_ref.dtype)

    row128 = pl.BlockSpec((bt, LANE), lambda i: (i, 0))
    return pl.pallas_call(
        body, name=name, grid=(T // bt,),
        in_specs=[pl.BlockSpec((bt, N), lambda i: (i, 0)), row128, row128],
        out_specs=pl.BlockSpec((bt, N), lambda i: (i, 0)),
        out_shape=jax.ShapeDtypeStruct((T, N), out_dtype),
        compiler_params=_cparams(("parallel",)),
    )(qm, cP, sP)


def _dot_nt(a, b):
    return lax.dot_general(a, b, (((1,), (1,)), ((), ())), preferred_element_type=F32)


def _dot_tn(a, b):
    return lax.dot_general(a, b, (((0,), (0,)), ((), ())), preferred_element_type=F32)


def _dot(a, b):
    return jnp.dot(a, b, preferred_element_type=F32)


def _window_fns(band, S, n_var):
    n_lat = S // BQ
    if band is None:
        return None
    reach, span = band

    def fns(j):
        start = jnp.clip(j - reach, 0, n_lat - span)
        return start, jnp.clip(j - start, 0, n_var - 1)

    return fns


class _AttnCfg:
    def __init__(self, *, Hkv, G, S, C, band, scale, n_var=0, bias_per_head=False, has_sink=False, two=False, bq=BQ,
                 ctx_queries=True):
        self.Hkv, self.G, self.S, self.C, self.band, self.scale = Hkv, G, S, C, band, scale
        self.n_var, self.bias_per_head, self.has_sink, self.two = n_var, bias_per_head, has_sink, two
        self.W = S if band is None else band[1] * BQ
        self.T = S + C
        self.bq, self.ctx_queries = bq, ctx_queries
        assert band is None or bq == BQ
        assert S % bq == 0 and C % bq == 0


def _attn_probs(cfg, j, q_ref, k_ref, q2_ref, k2_ref, bias_ref, sink_ref):
    G, S, C, W = cfg.G, cfg.S, cfg.C, cfg.W
    is_ctx = j * cfg.bq >= S
    if cfg.band is None:
        off, var = 0, 0
    else:
        start, var = _window_fns(cfg.band, S, cfg.n_var)(j)
        off = pl.multiple_of(start * BQ, BQ)
    qt = q_ref[...]
    qs = jnp.concatenate([qt[:, g * LANE:(g + 1) * LANE] for g in range(G)], axis=0) if G > 1 else qt
    kw = k_ref[pl.ds(off, W), :]
    kc = k_ref[pl.ds(S, C), :]
    s_w = _dot_nt(qs, kw)
    s_c = _dot_nt(qs, kc)
    q2s = k2w = k2c = None
    if cfg.two:
        q2s = q2_ref[...]
        k2w = k2_ref[pl.ds(off, W), :]
        k2c = k2_ref[pl.ds(S, C), :]
        s_w = s_w + _dot_nt(q2s, k2w)
        s_c = s_c + _dot_nt(q2s, k2c)
    operands = (off, var, qs, kw, kc, q2s, k2w, k2c)
    if not cfg.n_var and not cfg.has_sink:
        if cfg.ctx_queries:
            s_w = jnp.where(is_ctx, NEG, s_w)
        m = jnp.maximum(jnp.max(s_w, axis=-1, keepdims=True), jnp.max(s_c, axis=-1, keepdims=True))
        c2 = cfg.scale * math.log2(math.e)
        e_w = jnp.exp2((s_w - m) * c2)
        e_c = jnp.exp2((s_c - m) * c2)
        inv = 1.0 / (jnp.sum(e_w, axis=-1, keepdims=True) + jnp.sum(e_c, axis=-1, keepdims=True))
        return e_w * inv, e_c * inv, None, operands
    s_w = s_w * cfg.scale
    s_c = s_c * cfg.scale
    if cfg.n_var:
        b = bias_ref[0, pl.ds(var, 1)][0]
        s_w = s_w + (jnp.concatenate([b] * G, axis=0) if G > 1 else b)
    if cfg.ctx_queries:
        s_w = jnp.where(is_ctx, NEG, s_w)
    m = jnp.maximum(jnp.max(s_w, axis=-1, keepdims=True), jnp.max(s_c, axis=-1, keepdims=True))
    if cfg.has_sink:
        sink = sink_ref[0][:, 0:1]
        m = jnp.maximum(m, sink)
    e_w = jnp.exp(s_w - m)
    e_c = jnp.exp(s_c - m)
    l = jnp.sum(e_w, axis=-1, keepdims=True) + jnp.sum(e_c, axis=-1, keepdims=True)
    p_s = None
    if cfg.has_sink:
        e_s = jnp.exp(sink - m)
        l = l + e_s
    inv = 1.0 / l
    if cfg.has_sink:
        p_s = e_s * inv
    return e_w * inv, e_c * inv, p_s, operands


def _attn_specs(cfg, q_cb, k_cb, v_cb, q2_cb, k2_cb):
    G, T, bq = cfg.G, cfg.T, cfg.bq
    specs = [pl.BlockSpec((bq, G * LANE), lambda h, j: (j, q_cb // G + h)),
             pl.BlockSpec((T, LANE), lambda h, j: (0, k_cb + h)),
             pl.BlockSpec((T, LANE), lambda h, j: (0, v_cb + h))]
    if cfg.two:
        specs += [pl.BlockSpec((bq, LANE), lambda h, j: (j, q2_cb + h)),
                  pl.BlockSpec((T, LANE), lambda h, j: (0, k2_cb))]
    if cfg.n_var:
        if cfg.bias_per_head:
            specs.append(pl.BlockSpec((1, cfg.n_var, BQ, cfg.W), lambda h, j: (h, 0, 0, 0)))
        else:
            specs.append(pl.BlockSpec((1, cfg.n_var, BQ, cfg.W), lambda h, j: (0, 0, 0, 0)))
    if cfg.has_sink:
        specs.append(pl.BlockSpec((1, G * BQ, LANE), lambda h, j: (h, 0, 0)))
    return specs


def _attn_unpack(cfg, refs):
    refs = list(refs)
    q_ref, k_ref, v_ref = refs[:3]
    n = 3
    q2_ref = k2_ref = bias_ref = sink_ref = None
    if cfg.two:
        q2_ref, k2_ref = refs[n:n + 2]
        n += 2
    if cfg.n_var:
        bias_ref = refs[n]
        n += 1
    if cfg.has_sink:
        sink_ref = refs[n]
        n += 1
    return (q_ref, k_ref, v_ref, q2_ref, k2_ref, bias_ref, sink_ref), refs[n:]


def _attn_fwd(cfg, q, q_cb, k, k_cb, v, v_cb, *, q2=None, q2_cb=0, k2=None, k2_cb=0, bias=None, sink=None, rides=(),
              into=None, name):
    G, T, S, C, W = cfg.G, cfg.T, cfg.S, cfg.C, cfg.W
    assert q_cb % G == 0
    operands = [q, k, v] + ([q2, k2] if cfg.two else []) + ([bias] if cfg.n_var else []) + ([sink] if cfg.has_sink else [])

    bq = cfg.bq
    n_ride = sum(len(r.arrays) for r in rides)
    n_q = T // bq
    has_buf = into is not None and into[0] is not None
    col0 = 0 if into is None else into[1]
    width = cfg.Hkv * G * LANE if into is None else into[2] * LANE
    assert col0 % G == 0

    def body(*refs):
        (q_ref, k_ref, v_ref, q2_ref, k2_ref, bias_ref, sink_ref), rest = _attn_unpack(cfg, refs)
        rest = rest[:n_ride] + rest[n_ride + has_buf:]
        o_ref = rest[n_ride]
        ride_groups = _ride_split(rides, rest[:n_ride], rest[n_ride + 1:2 * n_ride + 1], rest[2 * n_ride + 1:])
        h = pl.program_id(0)
        j = pl.program_id(1)
        if rides:
            pl.when((h == 0) & (j == 0))(lambda: _ride_start(rides, ride_groups))

        def block():
            p_w, p_c, _, (off, _, _, _, _, _, _, _) = _attn_probs(cfg, j, q_ref, k_ref, q2_ref, k2_ref, bias_ref, sink_ref)
            o = _dot(p_w.astype(BF16), v_ref[pl.ds(off, W), :]) + _dot(p_c.astype(BF16), v_ref[pl.ds(S, C), :])
            for g in range(G):
                o_ref[:, g * LANE:(g + 1) * LANE] = o[g * bq:(g + 1) * bq].astype(o_ref.dtype)

        if cfg.ctx_queries:
            block()
        else:
            pl.when(j * bq < S)(block)

            @pl.when(j * bq >= S)
            def _():
                o_ref[...] = jnp.zeros_like(o_ref)

        if rides:
            pl.when((h == cfg.Hkv - 1) & (j == n_q - 1))(lambda: _ride_wait(rides, ride_groups))

    aliases = _ride_aliases(rides, len(operands), 1)
    if has_buf:
        aliases[len(operands) + n_ride] = 0
    outs = pl.pallas_call(
        body, name=name, grid=(cfg.Hkv, n_q),
        in_specs=_attn_specs(cfg, q_cb, k_cb, v_cb, q2_cb, k2_cb) + [_ANY] * (n_ride + has_buf),
        out_specs=[pl.BlockSpec((bq, G * LANE), lambda h, j: (j, col0 // G + h))] + [_ANY] * n_ride,
        out_shape=[jax.ShapeDtypeStruct((T, width), BF16)] + [s for r in rides for s in r.out_shapes()],
        scratch_shapes=_ride_scratch(rides),
        input_output_aliases=aliases,
        compiler_params=_cparams(("arbitrary", "arbitrary") if rides else ("parallel", "parallel")),
    )(*operands, *[a for r in rides for a in r.arrays], *([into[0]] if has_buf else []))
    if not rides:
        return outs[0]
    return outs[0], _ride_outputs(rides, outs[1:])


def _attn_bwd(cfg, q, q_cb, k, k_cb, v, v_cb, do, do_cb, *, q2=None, q2_cb=0, k2=None, k2_cb=0, bias=None, sink=None,
              want_dbias=False, dq_dtype=F32, rides=(), name):
    G, T, S, C, W, Hkv = cfg.G, cfg.T, cfg.S, cfg.C, cfg.W, cfg.Hkv
    assert q_cb % G == 0 and do_cb % G == 0 and not (want_dbias and G > 1)
    operands = [q, k, v] + ([q2, k2] if cfg.two else []) + ([bias] if cfg.n_var else []) + ([sink] if cfg.has_sink else [])
    operands.append(do)
    in_specs = _attn_specs(cfg, q_cb, k_cb, v_cb, q2_cb, k2_cb)
    bq = cfg.bq
    do_blocks = do.shape[0] // bq
    assert do.shape[0] == T or (do.shape[0] == S and not cfg.ctx_queries)
    in_specs.append(pl.BlockSpec((bq, G * LANE), lambda h, j: (jnp.minimum(j, do_blocks - 1), do_cb // G + h)))

    out_specs = [pl.BlockSpec((bq, G * LANE), lambda h, j: (j, h)),
                 pl.BlockSpec((T, LANE), lambda h, j: (0, h)),
                 pl.BlockSpec((T, LANE), lambda h, j: (0, h))]
    out_shape = [jax.ShapeDtypeStruct((T, Hkv * G * LANE), dq_dtype),
                 jax.ShapeDtypeStruct((T, Hkv * LANE), F32),
                 jax.ShapeDtypeStruct((T, Hkv * LANE), F32)]
    if cfg.two:
        out_specs += [pl.BlockSpec((bq, LANE), lambda h, j: (j, h)), pl.BlockSpec((T, LANE), lambda h, j: (0, 0))]
        out_shape += [jax.ShapeDtypeStruct((T, Hkv * LANE), dq_dtype), jax.ShapeDtypeStruct((T, LANE), F32)]
    if want_dbias:
        out_specs.append(pl.BlockSpec((1, cfg.n_var, BQ, W), lambda h, j: (h, 0, 0, 0)))
        out_shape.append(jax.ShapeDtypeStruct((Hkv, cfg.n_var, BQ, W), F32))
    if cfg.has_sink:
        out_specs.append(pl.BlockSpec((1, G * BQ, LANE), lambda h, j: (h, 0, 0)))
        out_shape.append(jax.ShapeDtypeStruct((Hkv, G * BQ, LANE), F32))

    n_ride = sum(len(r.arrays) for r in rides)
    operands += [a for r in rides for a in r.arrays]
    in_specs += [_ANY] * n_ride
    out_specs += [_ANY] * n_ride
    out_shape += [s for r in rides for s in r.out_shapes()]
    n_q = T // bq

    def body(*refs):
        (q_ref, k_ref, v_ref, q2_ref, k2_ref, bias_ref, sink_ref), rest = _attn_unpack(cfg, refs)
        do_ref, ride_in = rest[0], rest[1:1 + n_ride]
        dq_ref, dk_ref, dv_ref = rest[1 + n_ride:4 + n_ride]
        rest = rest[4 + n_ride:]
        dq2_ref = dk2_ref = dbias_ref = dsink_ref = None
        if cfg.two:
            dq2_ref, dk2_ref = rest[:2]
            rest = rest[2:]
        if want_dbias:
            dbias_ref = rest[0]
            rest = rest[1:]
        if cfg.has_sink:
            dsink_ref = rest[0]
            rest = rest[1:]
        ride_groups = _ride_split(rides, ride_in, rest[:n_ride], rest[n_ride:])
        h = pl.program_id(0)
        j = pl.program_id(1)
        if rides:
            pl.when((h == 0) & (j == 0))(lambda: _ride_start(rides, ride_groups))

        @pl.when(j == 0)
        def _():
            dk_ref[...] = jnp.zeros_like(dk_ref)
            dv_ref[...] = jnp.zeros_like(dv_ref)
            if want_dbias:
                dbias_ref[...] = jnp.zeros_like(dbias_ref)
            if cfg.has_sink:
                dsink_ref[...] = jnp.zeros_like(dsink_ref)

        if cfg.two:
            @pl.when((j == 0) & (h == 0))
            def _():
                dk2_ref[...] = jnp.zeros_like(dk2_ref)

        def block():
            p_w, p_c, p_s, (off, var, qs, kw, kc, q2s, k2w, k2c) = _attn_probs(
                cfg, j, q_ref, k_ref, q2_ref, k2_ref, bias_ref, sink_ref)
            dot_ = do_ref[...]
            dos = jnp.concatenate([dot_[:, g * LANE:(g + 1) * LANE] for g in range(G)], axis=0) if G > 1 else dot_
            dos = dos.astype(BF16)
            vw = v_ref[pl.ds(off, W), :]
            vc = v_ref[pl.ds(S, C), :]
            dp_w = _dot_nt(dos, vw)
            dp_c = _dot_nt(dos, vc)
            delta = jnp.sum(p_w * dp_w, axis=-1, keepdims=True) + jnp.sum(p_c * dp_c, axis=-1, keepdims=True)
            ds_w = p_w * (dp_w - delta)
            ds_c = p_c * (dp_c - delta)
            if want_dbias:
                dbias_ref[0, pl.ds(var, 1)] += ds_w[None]
            if cfg.has_sink:
                dsink_ref[0] += jnp.broadcast_to(-(p_s * delta), (G * bq, LANE))
            dsw = (ds_w * cfg.scale).astype(BF16)
            dsc = (ds_c * cfg.scale).astype(BF16)
            dq = _dot(dsw, kw) + _dot(dsc, kc)
            for g in range(G):
                dq_ref[:, g * LANE:(g + 1) * LANE] = dq[g * bq:(g + 1) * bq].astype(dq_ref.dtype)
            dk_ref[pl.ds(off, W), :] += _dot_tn(dsw, qs)
            dk_ref[pl.ds(S, C), :] += _dot_tn(dsc, qs)
            dv_ref[pl.ds(off, W), :] += _dot_tn(p_w.astype(BF16), dos)
            dv_ref[pl.ds(S, C), :] += _dot_tn(p_c.astype(BF16), dos)
            if cfg.two:
                dq2_ref[...] = (_dot(dsw, k2w) + _dot(dsc, k2c)).astype(dq2_ref.dtype)
                dk2_ref[pl.ds(off, W), :] += _dot_tn(dsw, q2s)
                dk2_ref[pl.ds(S, C), :] += _dot_tn(dsc, q2s)

        if cfg.ctx_queries:
            block()
        else:
            pl.when(j * bq < S)(block)

            @pl.when(j * bq >= S)
            def _():
                dq_ref[...] = jnp.zeros_like(dq_ref)
                if cfg.two:
                    dq2_ref[...] = jnp.zeros_like(dq2_ref)

        if rides:
            pl.when((h == Hkv - 1) & (j == n_q - 1))(lambda: _ride_wait(rides, ride_groups))

    outs = pl.pallas_call(
        body, name=name, grid=(Hkv, n_q),
        in_specs=in_specs, out_specs=out_specs, out_shape=out_shape,
        scratch_shapes=_ride_scratch(rides),
        compiler_params=_cparams(("arbitrary", "arbitrary")),
    )(*operands)
    if not rides:
        return outs
    return list(outs[:len(outs) - n_ride]) + [_ride_outputs(rides, outs[len(outs) - n_ride:])]


def _na_bias(rpb, S):
    H = rpb.shape[0]
    rows = S // GRID_W
    pad_l = GRID_W - 1 - (NA_WIN_C - 1)
    ext = jnp.concatenate([jnp.broadcast_to(rpb[:, :, :1], (H, 2 * NA_WIN_R - 1, pad_l)), rpb,
                           jnp.broadcast_to(rpb[:, :, -1:], (H, 2 * NA_WIN_R - 1, pad_l))], axis=2)
    by_col = jnp.stack([ext[:, :, GRID_W - 1 - qc:2 * GRID_W - 1 - qc] for qc in range(GRID_W)], axis=2)
    cq = np.arange(GRID_W)
    c0 = np.clip(cq - NA_WIN_C // 2, 0, GRID_W - NA_WIN_C)
    col_in = (cq[None, :] >= c0[:, None]) & (cq[None, :] < c0[:, None] + NA_WIN_C)
    n_lat = S // BQ
    neg_tile = jnp.full((H, GRID_W, GRID_W), NEG, F32)
    variants = []
    for v in range(5):
        j = {0: 0, 1: 1, 2: 2, 3: n_lat - 2, 4: n_lat - 1}[v]
        start = int(np.clip(j - 2, 0, n_lat - 5))
        assert j - start == v
        q_rows = []
        for qr in range(2):
            r = 2 * j + qr
            r0 = int(np.clip(r - NA_WIN_R // 2, 0, rows - NA_WIN_R))
            k_tiles = []
            for kr in range(10):
                krow = 2 * start + kr
                if r0 <= krow < r0 + NA_WIN_R:
                    k_tiles.append(jnp.where(col_in[None], by_col[:, krow - r + NA_WIN_R - 1], NEG))
                else:
                    k_tiles.append(neg_tile)
            q_rows.append(jnp.concatenate(k_tiles, axis=2))
        variants.append(jnp.concatenate(q_rows, axis=1))
    return jnp.stack(variants, axis=1)


def _swa_mask(S):
    qq = np.arange(BQ)[:, None]
    kk = np.arange(3 * BQ)[None, :]
    tiles = [np.where(np.abs(kk - v * BQ - qq) <= SWA_WINDOW, 0.0, NEG) for v in range(3)]
    return jnp.asarray(np.stack(tiles)[None], F32)


def _ffn_tiles(T, S, F):
    return _row_block(T, S), _pick(F, 1408)


def _halo_rows(dtype):
    return 8 * 4 // jnp.dtype(dtype).itemsize


def _halo_specs(T, bt, bf, dtype):
    hr = _halo_rows(dtype)
    nh = bt // hr
    return [pl.BlockSpec((bt, bf), lambda f, i: (i, f)),
            pl.BlockSpec((hr, bf), lambda f, i: (jnp.maximum(i * nh - 1, 0), f)),
            pl.BlockSpec((hr, bf), lambda f, i: (jnp.minimum((i + 1) * nh, T // hr - 1), f))]


def _neighbours(x, prev, nxt, i, bt, S, T):
    r = lax.broadcasted_iota(jnp.int32, x.shape, 0)
    g0 = i * bt
    first_open = jnp.logical_or(g0 == 0, g0 == S)
    last_open = jnp.logical_or(g0 + bt == S, g0 + bt == T)
    hr = prev.shape[0]
    before = jnp.where(r == 0, jnp.where(first_open, 0.0, prev[hr - 1:hr, :].astype(F32)), pltpu.roll(x, 1, 0))
    after = jnp.where(r == bt - 1, jnp.where(last_open, 0.0, nxt[0:1, :].astype(F32)), pltpu.roll(x, bt - 1, 0))
    return before, after


def _sigmoid(a):
    return 1.0 / (1.0 + jnp.exp(-a))


def _ffn_fwd(gp, u, cw, cb, S, *, name):
    T, F = gp.shape
    bt, bf = _ffn_tiles(T, S, F)

    def body(g_ref, gp_ref, gn_ref, u_ref, w_ref, b_ref, o_ref):
        i = pl.program_id(1)
        g = g_ref[...].astype(F32)
        before, after = _neighbours(g, gp_ref[...], gn_ref[...], i, bt, S, T)
        a = before * w_ref[0:1, :] + g * w_ref[1:2, :] + after * w_ref[2:3, :] + b_ref[...]
        o_ref[...] = (a * _sigmoid(a) * u_ref[...].astype(F32)).astype(o_ref.dtype)

    return pl.pallas_call(
        body, name=name, grid=(F // bf, T // bt),
        in_specs=_halo_specs(T, bt, bf, gp.dtype) + [pl.BlockSpec((bt, bf), lambda f, i: (i, f)),
                                              pl.BlockSpec((3, bf), lambda f, i: (0, f)),
                                              pl.BlockSpec((1, bf), lambda f, i: (0, f))],
        out_specs=pl.BlockSpec((bt, bf), lambda f, i: (i, f)),
        out_shape=jax.ShapeDtypeStruct((T, F), BF16),
        compiler_params=_cparams(("parallel", "parallel")),
    )(gp, gp, gp, u, cw, cb)


def _ffn_bwd_act(gp, u, da_out, cw, cb, S, *, name):
    T, F = gp.shape
    bt, bf = _ffn_tiles(T, S, F)

    def body(g_ref, gp_ref, gn_ref, u_ref, d_ref, w_ref, b_ref, da_ref, du_ref, acc_ref):
        i = pl.program_id(1)

        @pl.when(i == 0)
        def _():
            acc_ref[...] = jnp.zeros_like(acc_ref)

        g = g_ref[...].astype(F32)
        before, after = _neighbours(g, gp_ref[...], gn_ref[...], i, bt, S, T)
        a = before * w_ref[0:1, :] + g * w_ref[1:2, :] + after * w_ref[2:3, :] + b_ref[...]
        sig = _sigmoid(a)
        d = d_ref[...].astype(F32)
        du_ref[...] = (d * (a * sig)).astype(du_ref.dtype)
        da = d * u_ref[...].astype(F32) * (sig * (1.0 + a * (1.0 - sig)))
        da_ref[...] = da
        acc_ref[0:1, :] += jnp.sum(da * before, axis=0, keepdims=True)
        acc_ref[1:2, :] += jnp.sum(da * g, axis=0, keepdims=True)
        acc_ref[2:3, :] += jnp.sum(da * after, axis=0, keepdims=True)
        acc_ref[3:4, :] += jnp.sum(da, axis=0, keepdims=True)

    blk = pl.BlockSpec((bt, bf), lambda f, i: (i, f))
    return pl.pallas_call(
        body, name=name, grid=(F // bf, T // bt),
        in_specs=_halo_specs(T, bt, bf, gp.dtype) + [blk, blk,
                                              pl.BlockSpec((3, bf), lambda f, i: (0, f)),
                                              pl.BlockSpec((1, bf), lambda f, i: (0, f))],
        out_specs=[blk, blk, pl.BlockSpec((8, bf), lambda f, i: (0, f))],
        out_shape=[jax.ShapeDtypeStruct((T, F), F32), jax.ShapeDtypeStruct((T, F), BF16),
                   jax.ShapeDtypeStruct((8, F), F32)],
        compiler_params=_cparams(("parallel", "arbitrary")),
    )(gp, gp, gp, u, da_out, cw, cb)


def _ffn_bwd_conv(da, cw, S, *, name):
    T, F = da.shape
    bt, bf = _ffn_tiles(T, S, F)

    def body(d_ref, dp_ref, dn_ref, w_ref, o_ref):
        i = pl.program_id(1)
        d = d_ref[...]
        before, after = _neighbours(d, dp_ref[...], dn_ref[...], i, bt, S, T)
        o_ref[...] = (after * w_ref[0:1, :] + d * w_ref[1:2, :] + before * w_ref[2:3, :]).astype(o_ref.dtype)

    return pl.pallas_call(
        body, name=name, grid=(F // bf, T // bt),
        in_specs=_halo_specs(T, bt, bf, da.dtype) + [pl.BlockSpec((3, bf), lambda f, i: (0, f))],
        out_specs=pl.BlockSpec((bt, bf), lambda f, i: (i, f)),
        out_shape=jax.ShapeDtypeStruct((T, F), BF16),
        compiler_params=_cparams(("parallel", "parallel")),
    )(da, da, da, cw)


def _ew_rows(R, N, n_arrays):
    return _pick(R, max(16, EW_VMEM_BUDGET // (8 * n_arrays * N)), 16)


def _adam(w, g, m, v, *, rides=(), emit_grad=False, name):
    lead = w.shape[:-2]
    R, N = w.shape[-2:]
    br = _ew_rows(R, N, 7)
    bc1 = 1.0 - ADAM_B1 ** ADAM_STEP
    bc2 = 1.0 - ADAM_B2 ** ADAM_STEP
    grid = lead + (R // br,)
    n_ride = sum(len(r.arrays) for r in rides)

    n_out = 4 if emit_grad else 3

    def body(*refs):
        w_ref, g_ref, m_ref, v_ref = refs[:4]
        d_ref, mo_ref, vo_ref = refs[4 + n_ride:7 + n_ride]
        if emit_grad:
            refs[7 + n_ride][...] = g_ref[...]
        if rides:
            ride_groups = _ride_split(rides, refs[4:4 + n_ride], refs[4 + n_out + n_ride:4 + n_out + 2 * n_ride],
                                      refs[4 + n_out + 2 * n_ride:])
            steps = [pl.program_id(d) for d in range(len(grid))]
            pl.when(functools.reduce(jnp.logical_and, [s == 0 for s in steps]))(lambda: _ride_start(rides, ride_groups))
        gv = g_ref[...]
        mn = ADAM_B1 * m_ref[...] + (1.0 - ADAM_B1) * gv
        vn = ADAM_B2 * v_ref[...] + (1.0 - ADAM_B2) * (gv * gv)
        mo_ref[...] = mn
        vo_ref[...] = vn
        d_ref[...] = -ADAM_LR * ((mn / bc1) / (jnp.sqrt(vn / bc2) + ADAM_EPS) + ADAM_WD * w_ref[...])
        if rides:
            pl.when(functools.reduce(jnp.logical_and, [s == n - 1 for s, n in zip(steps, grid)]))(
                lambda: _ride_wait(rides, ride_groups))

    if lead:
        blk = pl.BlockSpec((None, br, N), lambda l, i: (l, i, 0))
    else:
        blk = pl.BlockSpec((br, N), lambda i: (i, 0))
    shp = jax.ShapeDtypeStruct(w.shape, F32)
    outs = pl.pallas_call(
        body, name=name, grid=grid,
        in_specs=[blk, blk, blk, blk] + [_ANY] * n_ride, out_specs=[blk] * n_out + [_ANY] * n_ride,
        out_shape=[shp] * n_out + [s for r in rides for s in r.out_shapes()],
        scratch_shapes=_ride_scratch(rides),
        input_output_aliases=_ride_aliases(rides, 4, n_out),
        compiler_params=_cparams(("arbitrary" if rides else "parallel",) * len(grid)),
    )(w, g, m, v, *[a for r in rides for a in r.arrays])
    if not rides:
        return outs
    return outs[:n_out], _ride_outputs(rides, outs[n_out:])


def _sum_lead(x, out_dtype, *, name):
    n, R, N = x.shape
    br = _ew_rows(R, N, n + 1)

    def body(x_ref, o_ref):
        acc = x_ref[0].astype(F32)
        for k in range(1, n):
            acc = acc + x_ref[k].astype(F32)
        o_ref[...] = acc.astype(o_ref.dtype)

    return pl.pallas_call(
        body, name=name, grid=(R // br,),
        in_specs=[pl.BlockSpec((n, br, N), lambda i: (0, i, 0))],
        out_specs=pl.BlockSpec((br, N), lambda i: (i, 0)),
        out_shape=jax.ShapeDtypeStruct((R, N), out_dtype),
        compiler_params=_cparams(("parallel",)),
    )(x)


def _sum_parts(parts, landed, chip, core, stack, *, name):
    _, R, N = parts.shape
    n_layers, layer, buf = stack
    br = _ew_rows(R, N, 5)

    def body(pos_ref, own_ref, landed_ref, *rest):
        o_ref = rest[-1]
        acc = own_ref[...].astype(F32)
        for k in range(3):
            acc = acc + landed_ref[k].astype(F32)
        o_ref[...] = acc

    operands = [jnp.stack([chip, core]).astype(jnp.int32), parts, landed]
    in_specs = [pl.BlockSpec((None, br, N), lambda i, pos: (pos[0], i, 0)),
                pl.BlockSpec((3, br, N), lambda i, pos: (0, i, 0))]
    aliases = {}
    if buf is not None:
        aliases = {3: 0}
        operands.append(buf)
        in_specs.append(pl.BlockSpec(memory_space=pl.ANY))
    return pl.pallas_call(
        body, name=name,
        grid_spec=pltpu.PrefetchScalarGridSpec(
            num_scalar_prefetch=1, grid=(R // br,), in_specs=in_specs,
            out_specs=pl.BlockSpec((None, None, br, N), lambda i, pos: (layer, pos[1], i, 0))),
        out_shape=jax.ShapeDtypeStruct((n_layers, 2, R, N), F32),
        input_output_aliases=aliases,
        compiler_params=_cparams(("parallel",)),
    )(*operands)


def _place_own(shards, layer, core, slot, *, name):
    _, R, N = shards.shape
    br = _ew_rows(R // 2, N, 2)
    nb = R // 2 // br

    def body(pos_ref, x_ref, o_ref):
        o_ref[...] = x_ref[...].astype(o_ref.dtype)

    return pl.pallas_call(
        body, name=name,
        grid_spec=pltpu.PrefetchScalarGridSpec(
            num_scalar_prefetch=1, grid=(nb,),
            in_specs=[pl.BlockSpec((None, br, N), lambda i, pos: (layer, pos[0] * nb + i, 0))],
            out_specs=pl.BlockSpec((None, br, N), lambda i, pos: (pos[1], i, 0))),
        out_shape=jax.ShapeDtypeStruct((8, R // 2, N), BF16),
        compiler_params=_cparams(("parallel",)),
    )(jnp.stack([core, slot]).astype(jnp.int32), shards)


def _add_half(g, r, core, *, name):
    Q, _, R, N = g.shape
    br = _ew_rows(R, N, 3)

    def body(c_ref, g_ref, r_ref, o_ref):
        o_ref[...] = (g_ref[...] + r_ref[...]).astype(o_ref.dtype)

    return pl.pallas_call(
        body, name=name,
        grid_spec=pltpu.PrefetchScalarGridSpec(
            num_scalar_prefetch=1, grid=(Q, R // br),
            in_specs=[pl.BlockSpec((None, None, br, N), lambda q, i, c_ref: (q, c_ref[0], i, 0)),
                      pl.BlockSpec((None, br, N), lambda q, i, c_ref: (q, i, 0))],
            out_specs=pl.BlockSpec((None, br, N), lambda q, i, c_ref: (q, i, 0))),
        out_shape=jax.ShapeDtypeStruct((Q, R, N), BF16),
        compiler_params=_cparams(("parallel", "parallel")),
    )(core.reshape(1).astype(jnp.int32), g, r)


_ANY = pl.BlockSpec(memory_space=pl.ANY)


def _place():
    return lax.axis_index("x"), lax.axis_index("y"), lax.axis_index("c")


def _allgather8(blocks, *, name):
    n = len(blocks)

    def body(*refs):
        xs, outs = refs[:n], refs[n:2 * n]
        send_sems, recv_sems, local_sems = refs[2 * n:]
        x, y, c = _place()
        me, sibling = (x, y, c), (x, y, 1 - c)
        chips = [(1 - x, y), (x, 1 - y), (1 - x, 1 - y)]

        def slot(a, px, py, pc):
            return outs[a].at[4 * px + 2 * py + pc]

        def copy(a, k, block, to, src=None):
            return pltpu.make_async_remote_copy(
                src_ref=slot(a, *block) if src is None else src, dst_ref=slot(a, *block),
                send_sem=send_sems.at[a, k], recv_sem=recv_sems.at[a, k], device_id=to, device_id_type=MESH)

        mine = [pltpu.make_async_copy(xs[a], slot(a, *me), local_sems.at[a]) for a in range(n)]
        for cp in mine:
            cp.start()
        first = []
        for a in range(n):
            first.append(copy(a, 0, me, sibling, src=xs[a]))
            first += [copy(a, 1 + j, me, (*chip, c), src=xs[a]) for j, chip in enumerate(chips)]
        for cp in first:
            cp.start()
        passed = []
        for j, chip in enumerate(chips):
            for a in range(n):
                copy(a, 1 + j, (*chip, c), me).wait_recv()
                fwd = copy(a, 4 + j, (*chip, c), sibling)
                fwd.start()
                passed.append(fwd)
        for a in range(n):
            copy(a, 0, sibling, me).wait_recv()
            for j, chip in enumerate(chips):
                copy(a, 4 + j, (*chip, 1 - c), me).wait_recv()
        for cp in first + passed:
            cp.wait_send()
        for cp in mine:
            cp.wait()

    return pl.pallas_call(
        body, name=name,
        in_specs=[_ANY] * n, out_specs=[_ANY] * n,
        out_shape=[jax.ShapeDtypeStruct((8,) + b.shape, b.dtype) for b in blocks],
        scratch_shapes=[pltpu.SemaphoreType.DMA((n, 7)), pltpu.SemaphoreType.DMA((n, 7)), pltpu.SemaphoreType.DMA((n,))],
    )(*blocks)


class _Exchange:
    n_sems = 1

    def __init__(self, arrays):
        self.arrays = list(arrays)

    def out_shapes(self):
        return [jax.ShapeDtypeStruct(g.shape[:1] + g.shape[2:], g.dtype) for g in self.arrays]

    def copy(self, k, src, dst, sems, landing):
        x, y, c = _place()
        return pltpu.make_async_remote_copy(src_ref=src.at[:, 1 - c], dst_ref=dst, send_sem=sems[0], recv_sem=sems[1],
                                            device_id=(x, y, 1 - c), device_id_type=MESH)

    def copies(self, group, landing):
        xs, outs, send_sems, recv_sems = group
        return [self.copy(k, xs[a], outs[a], (send_sems.at[a, k], recv_sems.at[a, k]), landing)
                for a in range(len(xs)) for k in range(self.n_sems)]


class _Scatter(_Exchange):
    n_sems = 3

    def out_shapes(self):
        return [jax.ShapeDtypeStruct((3,) + p.shape[1:], p.dtype) for p in self.arrays]

    def copy(self, k, src, dst, sems, landing):
        x, y, c = _place()
        px, py = [(1 - x, y), (x, 1 - y), (1 - x, 1 - y)][k]
        return pltpu.make_async_remote_copy(src_ref=src.at[2 * px + py], dst_ref=dst.at[k], send_sem=sems[0], recv_sem=sems[1],
                                            device_id=(px, py, c), device_id_type=MESH)


class _GatherChips(_Exchange):
    n_sems = 3
    in_place = True

    def out_shapes(self):
        return [jax.ShapeDtypeStruct(b.shape, b.dtype) for b in self.arrays]

    def copy(self, k, src, dst, sems, landing):
        x, y, c = _place()
        px, py = [(1 - x, y), (x, 1 - y), (1 - x, 1 - y)][k]
        slot = 4 * px + 2 * py + c if landing else 4 * x + 2 * y + c
        return pltpu.make_async_remote_copy(src_ref=src.at[4 * x + 2 * y + c], dst_ref=dst.at[slot], send_sem=sems[0],
                                            recv_sem=sems[1], device_id=(px, py, c), device_id_type=MESH)


class _GatherCores(_GatherChips):
    n_sems = 4

    def copy(self, k, src, dst, sems, landing):
        x, y, c = _place()
        slot = 2 * k + 1 - c if landing else 2 * k + c
        return pltpu.make_async_remote_copy(src_ref=src.at[2 * k + c], dst_ref=dst.at[slot], send_sem=sems[0],
                                            recv_sem=sems[1], device_id=(x, y, 1 - c), device_id_type=MESH)


class _Join(_GatherChips):
    n_sems = 1

    def copy(self, k, src, dst, sems, landing):
        x, y, c = _place()
        return pltpu.make_async_remote_copy(src_ref=src.at[:, c], dst_ref=dst.at[:, 1 - c if landing else c], send_sem=sems[0],
                                            recv_sem=sems[1], device_id=(x, y, 1 - c), device_id_type=MESH)


def _ride_aliases(rides, first_in, first_out):
    aliases, i = {}, 0
    for r in rides:
        for a in range(len(r.arrays)):
            if getattr(r, "in_place", False):
                aliases[first_in + i + a] = first_out + i + a
        i += len(r.arrays)
    return aliases


def _ride_scratch(rides):
    shapes = []
    for r in rides:
        shapes += [pltpu.SemaphoreType.DMA((len(r.arrays), r.n_sems)), pltpu.SemaphoreType.DMA((len(r.arrays), r.n_sems))]
    return shapes


def _ride_split(rides, in_refs, out_refs, sem_refs):
    groups, i, o = [], 0, 0
    for k, r in enumerate(rides):
        n = len(r.arrays)
        groups.append((in_refs[i:i + n], out_refs[o:o + n], sem_refs[2 * k], sem_refs[2 * k + 1]))
        i, o = i + n, o + n
    return groups


def _ride_start(rides, groups):
    for r, g in zip(rides, groups):
        for cp in r.copies(g, False):
            cp.start()


def _ride_wait(rides, groups):
    for r, g in zip(rides, groups):
        for cp in r.copies(g, True):
            cp.wait_recv()
        for cp in r.copies(g, False):
            cp.wait_send()


def _run_rides(rides, *, name):
    n_in = sum(len(r.arrays) for r in rides)

    def body(*refs):
        groups = _ride_split(rides, refs[:n_in], refs[n_in:2 * n_in], refs[2 * n_in:])
        _ride_start(rides, groups)
        _ride_wait(rides, groups)

    outs = pl.pallas_call(
        body, name=name,
        in_specs=[_ANY] * n_in, out_specs=[_ANY] * n_in,
        out_shape=[s for r in rides for s in r.out_shapes()],
        scratch_shapes=_ride_scratch(rides),
        input_output_aliases=_ride_aliases(rides, 0, 0),
    )(*[a for r in rides for a in r.arrays])
    return _ride_outputs(rides, outs)


def _ride_outputs(rides, outs):
    res, o = [], 0
    for r in rides:
        res.append(list(outs[o:o + len(r.arrays)]))
        o += len(r.arrays)
    return res


def _perm_w_in(wt):
    pad = jnp.zeros((PCOLS - IN_COLS, wt.shape[1]), wt.dtype)
    return jnp.concatenate([wt[:3072], wt[3136:IN_COLS], wt[3072:3136], pad], axis=0)


def _unperm_w_in(gt):
    return jnp.concatenate([gt[:3072], gt[4096:IN_COLS], gt[3072:4096]], axis=0)


def _perm_w_uq(w):
    w4 = w.reshape(MLA_Q_LORA, MLA_HEADS, MLA_NOPE + MLA_ROPE)
    nope = w4[:, :, :MLA_NOPE].reshape(MLA_Q_LORA, MLA_HEADS * LANE)
    pe = jnp.pad(w4[:, :, MLA_NOPE:], ((0, 0), (0, 0), (0, LANE - MLA_ROPE))).reshape(MLA_Q_LORA, MLA_HEADS * LANE)
    return jnp.concatenate([nope, pe], axis=1)


def _unperm_w_uq(g):
    nope = g[:, :MLA_HEADS * LANE].reshape(MLA_Q_LORA, MLA_HEADS, LANE)
    pe = g[:, MLA_HEADS * LANE:].reshape(MLA_Q_LORA, MLA_HEADS, LANE)[:, :, :MLA_ROPE]
    return jnp.concatenate([nope, pe], axis=2).reshape(MLA_Q_LORA, MLA_HEADS * (MLA_NOPE + MLA_ROPE))


def _perm_w_ukv(w):
    w4 = w.reshape(MLA_KV_LORA, MLA_HEADS, MLA_NOPE + MLA_V)
    return jnp.concatenate([w4[:, :, :MLA_NOPE].reshape(MLA_KV_LORA, -1), w4[:, :, MLA_NOPE:].reshape(MLA_KV_LORA, -1)], axis=1)


def _unperm_w_ukv(g):
    kn = g[:, :MLA_HEADS * LANE].reshape(MLA_KV_LORA, MLA_HEADS, LANE)
    vv = g[:, MLA_HEADS * LANE:].reshape(MLA_KV_LORA, MLA_HEADS, LANE)
    return jnp.concatenate([kn, vv], axis=2).reshape(MLA_KV_LORA, -1)


def _silu(v):
    return v * jax.nn.sigmoid(v)


def _silu_grad(v):
    s = jax.nn.sigmoid(v)
    return s * (1.0 + v * (1.0 - s))


_WEIGHTS = ("c_ctx", "w_ada", "b_ada", "w_in", "na_rpb", "swa_sink", "mla_q_norm", "mla_kv_norm", "mla_w_uq", "mla_w_ukv",
            "gqa_q_norm", "gqa_k_norm", "w_out", "ln1_g", "ln1_b", "ffn_w_gate", "ffn_w_up", "ffn_conv_w", "ffn_conv_b",
            "ffn_w_down", "ln2_g", "ln2_b")
_COL_SHARDED = ("mla_w_uq", "mla_w_ukv", "ffn_w_gate", "ffn_w_up")
_ROW_SHARDED = ("w_out", "ffn_w_down")
_BIG = ("w_in",) + _COL_SHARDED + _ROW_SHARDED
_SMALL = ("c_ctx", "b_ada", "na_rpb", "swa_sink", "mla_q_norm", "mla_kv_norm", "gqa_q_norm", "gqa_k_norm", "ln1_g", "ln1_b",
          "ffn_conv_w", "ffn_conv_b", "ln2_g", "ln2_b")


def _piece_rows(a):
    return -(-a.size // (8 * LANE)) * 8


def _pack(arrays):
    return jnp.concatenate([jnp.pad(a.reshape(-1), (0, _piece_rows(a) * LANE - a.size)).reshape(_piece_rows(a), LANE)
                            for a in arrays], axis=0)


def _unpack(packed, like):
    out, r = [], 0
    for a in like:
        n = _piece_rows(a)
        out.append(packed[r:r + n].reshape(-1)[:a.size].reshape(a.shape))
        r += n
    return out


def _train_step(x, c, ctx, loss_target, w, m_in, v_in):
    L = DEPTH
    S, D = x.shape[1], x.shape[2]
    C = ctx.shape[1]
    T = S + C
    F = w["ffn_conv_b"].shape[1]
    ax, ay, ac = _place()
    chip = 2 * ax + ay
    dev = 2 * chip + ac
    n_ada = w["w_ada"].shape[2]
    w, m_in, v_in = dict(w), dict(m_in), dict(v_in)
    for d in (w, m_in, v_in):
        d["w_in"] = jnp.swapaxes(d["w_in"], 1, 2)

    gather_groups = {"A": ("w_in", "mla_w_uq", "mla_w_ukv"), "B": ("w_out",), "Cg": ("ffn_w_gate",), "Cu": ("ffn_w_up",),
                     "D": ("ffn_w_down",)}
    full = {n: [None] * L for n in _BIG}
    w_in_p, w_uq_p, w_ukv_p = [None] * L, [None] * L, [None] * L
    half_done = {}

    def chips_step(group, l):
        return _GatherChips([_place_own(w[n], l, ac, dev, name="gather_place") for n in gather_groups[group]])

    def cores_step(group, l):
        return _GatherCores(half_done.pop((group, l)))

    def finish_group(group, l, bufs):
        for n, b in zip(gather_groups[group], bufs):
            r, cols = b.shape[1:]
            if n in _COL_SHARDED:
                full[n][l] = b.reshape(4, 2, r, cols).transpose(1, 2, 0, 3).reshape(2 * r, 4 * cols)
            else:
                full[n][l] = b.reshape(8 * r, cols)
        if group == "A":
            w_in_p[l], w_uq_p[l] = _perm_w_in(full["w_in"][l]), _perm_w_uq(full["mla_w_uq"][l])
            w_ukv_p[l] = _perm_w_ukv(full["mla_w_ukv"][l])

    def with_rides(result, rides):
        return result if rides else (result, [])

    def my_half(a):
        r = a.shape[0] // 2
        return lax.dynamic_slice_in_dim(a, ac * r, r, axis=0).astype(BF16)

    gathered = _allgather8([my_half(w[n][0]) for n in gather_groups["A"]] + [w["ffn_conv_w"]], name="gather_weights")
    finish_group("A", 0, gathered[:-1])
    conv_w = gathered[-1][::2].transpose(1, 2, 0, 3).reshape(L, 3, F)

    (c_all,) = _allgather8([c], name="gather_c")
    c16 = jnp.concatenate([c_all.reshape(8, D), jnp.broadcast_to(w["c_ctx"][None], (8, D))], axis=0)
    row_keep = (jnp.arange(16) <= 8).astype(F32)[:, None]
    sc = _silu(c16) * row_keep
    b_loc = lax.dynamic_slice_in_dim(w["b_ada"], chip * n_ada, n_ada, axis=1)
    mod_loc = jnp.stack([_mm(sc, w["w_ada"], b_layer=l, name="mod_mm") + b_loc[l][None] for l in range(L)])
    (mod_g,) = _allgather8([mod_loc], name="gather_mod")
    mod_all = mod_g[::2].transpose(1, 2, 0, 3).reshape(L, 16, 4 * n_ada)
    mod_x = lax.dynamic_index_in_dim(mod_all, dev, axis=1, keepdims=False)
    mod_c = mod_all[:, 8]
    mods = [jnp.stack([mod_x[l].reshape(6, D), mod_c[l].reshape(6, D)], axis=1) for l in range(L)]

    tabs = _rope_tables(S, C, HEAD_DIM) + _rope_tables(S, C, MLA_ROPE)
    swa_mask = _swa_mask(S)
    scale = HEAD_DIM ** -0.5
    def attn_cfgs(l):
        cq = l < L - 1
        return (_AttnCfg(Hkv=NA_HEADS, G=1, S=S, C=C, band=(2, 5), scale=scale, n_var=5, bias_per_head=True, ctx_queries=cq),
                _AttnCfg(Hkv=SWA_KV_HEADS, G=SWA_HEADS // SWA_KV_HEADS, S=S, C=C, band=(1, 3), scale=scale, n_var=3,
                         has_sink=True, ctx_queries=cq),
                _AttnCfg(Hkv=MLA_HEADS, G=1, S=S, C=C, band=None, scale=(MLA_NOPE + MLA_ROPE) ** -0.5, two=True,
                         bq=2 * BQ, ctx_queries=cq),
                _AttnCfg(Hkv=GQA_KV_HEADS, G=GQA_HEADS // GQA_KV_HEADS, S=S, C=C, band=None, scale=scale, ctx_queries=cq))

    def gqa_bwd_cfg(cfg):
        return _AttnCfg(Hkv=cfg.Hkv, G=cfg.G, S=S, C=C, band=None, scale=cfg.scale, bq=2 * BQ, ctx_queries=cfg.ctx_queries)

    row = lambda a: a[None, :]

    xt = jnp.concatenate([x[0], ctx[0]], axis=0)
    saved = []
    for l in range(L):
        md = mods[l]
        gq, gk, mq, mkv = row(w["gqa_q_norm"][l]), row(w["gqa_k_norm"][l]), row(w["mla_q_norm"][l]), row(w["mla_kv_norm"][l])
        h1 = _mod_fwd(xt, md[0], md[1], S, name="mod_fwd")
        p = _mm(h1, w_in_p[l], mode="nt", out_dtype=BF16, name="in_proj")
        qkv = _prep_fwd(p, tabs, gq, gk, mq, mkv, S, name="prep_fwd")
        qm = _mm(qkv, w_uq_p[l], a_off=CB_CQ * LANE, a_k=MLA_Q_LORA, tk=LANE, name="mla_uq")
        qmb = _pe_rope(qm, tabs, S, 1.0, BF16, name="mla_q_rope")
        kvm = _mm(qkv, w_ukv_p[l], a_off=CB_CKV * LANE, a_k=MLA_KV_LORA, tk=LANE, out_dtype=BF16, name="mla_ukv")
        bias_na = _na_bias(w["na_rpb"][l], S)
        sink = jnp.broadcast_to(jnp.repeat(w["swa_sink"][l].reshape(SWA_KV_HEADS, -1), BQ, axis=1)[:, :, None],
                                (SWA_KV_HEADS, SWA_HEADS // SWA_KV_HEADS * BQ, LANE))
        cfg_na, cfg_swa, cfg_mla, cfg_gqa = attn_cfgs(l)
        rows = T if l < L - 1 else S
        first, more = l == 0, l + 1 < L
        rides = [chips_step("B", l)] if first else [cores_step("B", l)]
        mix_blocks = NA_HEADS + SWA_HEADS + MLA_HEADS + GQA_HEADS
        mix, got = _attn_fwd(cfg_na, qkv, CB_NA_Q, qkv, CB_NA_K, qkv, CB_NA_V, bias=bias_na, rides=rides,
                             into=(None, 0, mix_blocks), name="na_fwd")
        if first:
            half_done[("B", l)] = got[0]
        else:
            finish_group("B", l, got[0])
        rides = [cores_step("B", l)] if first else []
        mix, got = with_rides(_attn_fwd(cfg_swa, qkv, CB_SWA_Q, qkv, CB_SWA_K, qkv, CB_SWA_V, bias=swa_mask, sink=sink,
                                        rides=rides, into=(mix, NA_HEADS, mix_blocks), name="swa_fwd"), rides)
        if first:
            finish_group("B", l, got[0])
        mix, got = _attn_fwd(cfg_mla, qmb, 0, kvm, 0, kvm, MLA_HEADS, q2=qmb, q2_cb=MLA_HEADS, k2=qkv, k2_cb=CB_KPE,
                             rides=[chips_step("Cg", l)], into=(mix, NA_HEADS + SWA_HEADS, mix_blocks), name="mla_fwd")
        half_done[("Cg", l)] = got[0]
        mix, got = _attn_fwd(cfg_gqa, qkv, CB_GQA_Q, qkv, CB_GQA_K, qkv, CB_GQA_V,
                             rides=[cores_step("Cg", l), chips_step("Cu", l)],
                             into=(mix, NA_HEADS + SWA_HEADS + MLA_HEADS, mix_blocks), name="gqa_fwd")
        finish_group("Cg", l, got[0])
        half_done[("Cu", l)] = got[1]
        z1, got = _mm(mix, full["w_out"][l], rows=rows, rides=[cores_step("Cu", l)], name="out_proj")
        finish_group("Cu", l, got[0])
        x1 = _res_fwd(xt, z1, md[2], row(w["ln1_g"][l]), row(w["ln1_b"][l]), S, name="res_fwd")
        h2 = _mod_fwd(x1, md[3], md[4], S, name="mod_fwd")
        gp, got = _mm(h2, full["ffn_w_gate"][l], rides=[chips_step("D", l)], out_dtype=BF16, name="ffn_in")
        half_done[("D", l)] = got[0]
        up, got = _mm(h2, full["ffn_w_up"][l], rides=[cores_step("D", l)] + ([chips_step("A", l + 1)] if more else []),
                      out_dtype=BF16, name="ffn_in")
        finish_group("D", l, got[0])
        if more:
            half_done[("A", l + 1)] = got[1]
        act = _ffn_fwd(gp, up, conv_w[l], row(w["ffn_conv_b"][l]), S, name="ffn_mid")
        rides = [cores_step("A", l + 1), chips_step("B", l + 1)] if more else []
        z2, got = with_rides(_mm(act, full["ffn_w_down"][l], rides=rides, name="ffn_out"), rides)
        if more:
            finish_group("A", l + 1, got[0])
            half_done[("B", l + 1)] = got[1]
        x2 = _res_fwd(x1, z2, md[5], row(w["ln2_g"][l]), row(w["ln2_b"][l]), S, name="res_fwd")
        saved.append(dict(x=xt, h1=h1, p=p, qkv=qkv, qmb=qmb, kvm=kvm, bias_na=bias_na, sink=sink, mix=mix, z1=z1, x1=x1,
                          h2=h2, gp=gp, up=up, act=act, z2=z2, cfgs=(cfg_na, cfg_swa, cfg_mla, cfg_gqa)))
        xt = x2

    dx, loss_part = _loss_fwd_bwd(xt, loss_target[0], S, name="loss")
    loss = lax.psum(loss_part[0, 0], ("x", "y", "c"))

    groups = {"ffn": ("ffn_w_gate", "ffn_w_up", "ffn_w_down", "w_out"), "rest": ("w_in", "mla_w_uq", "mla_w_ukv")}
    wgrad = [dict() for _ in range(L)]
    parts, landed = {}, {}

    def halves_of(group, l):
        return [wgrad[l][n].reshape(4, 2, wgrad[l][n].shape[1] // 2, wgrad[l][n].shape[2]) for n in groups[group]]

    def add_halves(group, l, received):
        parts[(group, l)] = [_add_half(h, r, ac, name="rs_core_add") for h, r in zip(halves_of(group, l), received)]

    small = {n: [None] * L for n in ("na_rpb", "swa_sink", "mla_q_norm", "mla_kv_norm", "gqa_q_norm", "gqa_k_norm",
                                     "ln1_g", "ln1_b", "ffn_conv_w", "ffn_conv_b", "ln2_g", "ln2_b")}
    dmod = [None] * L
    for l in reversed(range(L)):
        sv, md = saved[l], mods[l]
        gq, gk, mq, mkv = row(w["gqa_q_norm"][l]), row(w["gqa_k_norm"][l]), row(w["mla_q_norm"][l]), row(w["mla_kv_norm"][l])
        cb_row = row(w["ffn_conv_b"][l])
        dx1, dz2, acc_r2 = _res_bwd(sv["x1"], sv["z2"], md[5], row(w["ln2_g"][l]), dx, S, name="res_bwd")
        dact = _mm(dz2, full["ffn_w_down"][l], mode="nt", out_dtype=BF16, name="ffn_out_dx")
        wgrad[l]["ffn_w_down"] = _mm(sv["act"], dz2, mode="tn", name="ffn_out_dw").reshape(4, F // 4, D)
        da, du, acc_f = _ffn_bwd_act(sv["gp"], sv["up"], dact, conv_w[l], cb_row, S, name="ffn_mid_bwd")
        dg = _ffn_bwd_conv(da, conv_w[l], S, name="ffn_conv_bwd")
        dh2 = _mm(dg, full["ffn_w_gate"][l], mode="nt", name="ffn_in_dx")
        dh2 = _mm(du, full["ffn_w_up"][l], mode="nt", add=dh2, name="ffn_in_dx_add")
        wgrad[l]["ffn_w_gate"] = _mm(sv["h2"], dg, mode="tn", stack=(1, 0, None), split4=True,
                                     name="ffn_in_dw").reshape(4, D, F // 4)
        wgrad[l]["ffn_w_up"] = _mm(sv["h2"], du, mode="tn", stack=(1, 0, None), split4=True,
                                   name="ffn_in_dw").reshape(4, D, F // 4)
        dx1, acc_m2 = _mod_bwd(sv["x1"], dh2, md[4], dx1, S, name="mod_bwd")
        dxa, dz1, acc_r1 = _res_bwd(sv["x"], sv["z1"], md[2], row(w["ln1_g"][l]), dx1, S, name="res_bwd")
        dmix = _mm(dz1, full["w_out"][l], mode="nt", out_dtype=BF16, name="out_proj_dx")
        wgrad[l]["w_out"] = _mm(sv["mix"], dz1, mode="tn", rows=dz1.shape[0], name="out_proj_dw").reshape(4, -1, D)

        qkv, qmb, kvm = sv["qkv"], sv["qmb"], sv["kvm"]
        cfg_na, cfg_swa, cfg_mla, cfg_gqa = sv["cfgs"]
        rest_above = l + 1 < L
        rides = [_Exchange(halves_of("ffn", l))] + ([_Exchange(halves_of("rest", l + 1))] if rest_above else [])
        dq_a, dk_a, dv_a, dbias, received = _attn_bwd(cfg_na, qkv, CB_NA_Q, qkv, CB_NA_K, qkv, CB_NA_V, dmix, 0,
                                                      bias=sv["bias_na"], want_dbias=True, rides=rides, name="na_bwd")
        add_halves("ffn", l, received[0])
        if rest_above:
            add_halves("rest", l + 1, received[1])
        dq_b, dk_b, dv_b, dsink = _attn_bwd(cfg_swa, qkv, CB_SWA_Q, qkv, CB_SWA_K, qkv, CB_SWA_V, dmix, NA_HEADS,
                                            bias=swa_mask, sink=sv["sink"], name="swa_bwd")
        dq_c, dk_c, dv_c, dq2_c, dk2_c, got = _attn_bwd(
            cfg_mla, qmb, 0, kvm, 0, kvm, MLA_HEADS, dmix, NA_HEADS + SWA_HEADS, q2=qmb, q2_cb=MLA_HEADS, k2=qkv,
            k2_cb=CB_KPE, rides=[_Scatter(parts[("ffn", l)])], name="mla_bwd")
        landed[("ffn", l)] = got[0]
        rides = [_Scatter(parts[("rest", l + 1)])] if rest_above else []
        gqa_out = _attn_bwd(gqa_bwd_cfg(cfg_gqa), qkv, CB_GQA_Q, qkv, CB_GQA_K, qkv, CB_GQA_V, dmix,
                            NA_HEADS + SWA_HEADS + MLA_HEADS, rides=rides, name="gqa_bwd")
        dq_d, dk_d, dv_d = gqa_out[:3]
        if rest_above:
            landed[("rest", l + 1)] = gqa_out[3][0]
        dqm = _pe_rope(jnp.concatenate([dq_c, dq2_c], axis=1), tabs, S, -1.0, BF16, name="mla_q_rope_bwd")
        dkvm = jnp.concatenate([dk_c, dv_c], axis=1).astype(BF16)
        dcq = _mm(dqm, w_uq_p[l], mode="nt", name="mla_uq_dx")
        dckv = _mm(dkvm, w_ukv_p[l], mode="nt", name="mla_ukv_dx")
        cqn = qkv[:, CB_CQ * LANE:CB_CKV * LANE]
        ckvn = qkv[:, CB_CKV * LANE:(CB_CKV + 1) * LANE]
        d_uq = _unperm_w_uq(_mm(cqn, dqm, mode="tn", name="mla_uq_dw"))
        d_ukv = _unperm_w_ukv(_mm(ckvn, dkvm, mode="tn", name="mla_ukv_dw"))
        grads = {}
        for h in range(NA_HEADS):
            grads[CB_NA_Q + h], grads[CB_NA_K + h], grads[CB_NA_V + h] = (dq_a, h), (dk_a, h), (dv_a, h)
        for h in range(SWA_HEADS):
            grads[CB_SWA_Q + h] = (dq_b, h)
        for h in range(SWA_KV_HEADS):
            grads[CB_SWA_K + h], grads[CB_SWA_V + h] = (dk_b, h), (dv_b, h)
        for h in range(GQA_HEADS):
            grads[CB_GQA_Q + h] = (dq_d, h)
        for h in range(GQA_KV_HEADS):
            grads[CB_GQA_K + h], grads[CB_GQA_V + h] = (dk_d, h), (dv_d, h)
        grads[CB_KPE], grads[CB_CQ], grads[CB_CKV] = (dk2_c, 0), (dcq, 0), (dckv, 0)
        dp, acc_p = _prep_bwd(sv["p"], grads, tabs, gq, gk, mq, mkv, S, name="prep_bwd")
        dh1 = _mm(dp, w_in_p[l], name="in_proj_dx")
        d_in = _unperm_w_in(_mm(dp, sv["h1"], mode="tn", name="in_proj_dw")).reshape(4, IN_COLS // 4, D)
        dx, acc_m1 = _mod_bwd(sv["x"], dh1, md[1], dxa, S, name="mod_bwd")

        to4 = lambda g: g.reshape(g.shape[0], 4, g.shape[1] // 4).transpose(1, 0, 2)
        wgrad[l]["w_in"], wgrad[l]["mla_w_uq"], wgrad[l]["mla_w_ukv"] = d_in, to4(d_uq), to4(d_ukv)
        dmod[l] = jnp.stack([acc_m1[0:2], acc_m1[2:4], acc_r1[0:2], acc_m2[0:2], acc_m2[2:4], acc_r2[0:2]])
        rpb_vjp = jax.vjp(lambda r: _na_bias(r, S), w["na_rpb"][l])[1]
        small["na_rpb"][l] = rpb_vjp(dbias)[0]
        small["swa_sink"][l] = dsink[:, :, 0].reshape(SWA_KV_HEADS, -1, BQ).sum(axis=-1).reshape(-1)
        small["gqa_q_norm"][l], small["gqa_k_norm"][l] = acc_p[0, :LANE], acc_p[1, :LANE]
        small["mla_q_norm"][l], small["mla_kv_norm"][l] = acc_p[2], acc_p[3, :LANE]
        small["ln1_g"][l], small["ln1_b"][l] = acc_r1[2], acc_r1[3]
        small["ln2_g"][l], small["ln2_b"][l] = acc_r2[2], acc_r2[3]
        small["ffn_conv_w"][l], small["ffn_conv_b"][l] = acc_f[0:3], acc_f[3]
    grad_x = dx[:S][None]

    dmod_x = jnp.stack([dmod[l][:, 0].reshape(-1) for l in range(L)])
    dmod_c = jnp.stack([dmod[l][:, 1].reshape(-1) for l in range(L)])
    small_names = tuple(small)
    bucket = [dmod_x, dmod_c] + [jnp.stack(small[n]) for n in small_names]
    (b8,) = _allgather8([_pack(bucket)], name="gather_small")
    tot = _unpack(_sum_lead(b8, F32, name="sum_small"), bucket)
    dmod_x_all = b8.reshape(8, -1)[:, :dmod_x.size].reshape(8, L, 6 * D)
    dmod_c_tot = tot[1]
    g_small = dict(zip(small_names, tot[2:]))
    g_small["b_ada"] = tot[0] + dmod_c_tot
    g_small["ffn_conv_w"] = lax.dynamic_slice_in_dim(g_small["ffn_conv_w"], chip * (F // 4), F // 4, axis=2)

    dmod16 = jnp.concatenate([dmod_x_all, jnp.broadcast_to(dmod_c_tot[None], (8, L, 6 * D))], axis=0) * row_keep[:, :, None]
    dmod16 = lax.dynamic_slice_in_dim(dmod16, chip * n_ada, n_ada, axis=2)
    g_ada, dsc = None, None
    for l in range(L):
        g_ada = _mm(sc, dmod16[:, l], mode="tn", exact=True, stack=(L, l, g_ada), name="ada_dw")
        dsc = _mm(dmod16[:, l], w["w_ada"], b_layer=l, mode="nt", add=dsc, name="ada_dx" if dsc is None else "ada_dx_add")
    (dsc8,) = _allgather8([dsc[8:16]], name="gather_dsc")
    dsc4 = dsc8[::2, 0]
    g_small["c_ctx"] = (((dsc4[0] + dsc4[1]) + dsc4[2]) + dsc4[3]) * _silu_grad(w["c_ctx"])

    sums = {}

    def sum_group(group, l):
        for n, p, got in zip(groups[group], parts[(group, l)], landed[(group, l)]):
            sums[n] = _sum_parts(p, got, chip, ac, (L, l, sums.get(n)), name="rs_chip_sum")

    add_halves("rest", 0, _run_rides([_Exchange(halves_of("rest", 0))], name="rs_core_exchange")[0])
    for l in range(L):
        sum_group("ffn", l)
        if l > 0:
            sum_group("rest", l)
    grad, delta, new_m, new_v = {}, {}, {}, {}
    (delta["w_ada"], new_m["w_ada"], new_v["w_ada"]), got = _adam(
        w["w_ada"], g_ada, m_in["w_ada"], v_in["w_ada"],
        rides=[_Scatter(parts[("rest", 0)]), _Join([sums[n] for n in groups["ffn"]])], name="adam")
    landed[("rest", 0)] = got[0]
    joined = dict(zip(groups["ffn"], got[1]))
    sum_group("rest", 0)
    joined.update(zip(groups["rest"], _run_rides([_Join([sums[n] for n in groups["rest"]])], name="rs_join")[0]))
    g_big = {n: j.reshape(L, 2 * j.shape[2], j.shape[3]) for n, j in joined.items()}
    g_big["w_ada"] = g_ada

    grad["w_ada"] = g_ada
    for n in _BIG:
        delta[n], new_m[n], new_v[n], grad[n] = _adam(w[n], g_big[n], m_in[n], v_in[n], emit_grad=True, name="adam")
    like = [w[n] for n in _SMALL]
    packed = [_pack([src[n].reshape(w[n].shape) for n in _SMALL]) for src in (w, g_small, m_in, v_in)]
    d_s, m_s, v_s = _adam(*packed, name="adam_small")
    for n, g_, d_, m_, v_ in zip(_SMALL, _unpack(packed[1], like), _unpack(d_s, like), _unpack(m_s, like), _unpack(v_s, like)):
        grad[n], delta[n], new_m[n], new_v[n] = g_, d_, m_, v_

    for d in (grad, delta, new_m, new_v):
        d["w_in"] = jnp.swapaxes(d["w_in"], 1, 2)
    return (loss, grad_x, *[grad[n] for n in _WEIGHTS], *[delta[n] for n in _WEIGHTS],
            *[new_m[n] for n in _WEIGHTS], *[new_v[n] for n in _WEIGHTS])


def kernel(x, c, ctx, c_ctx, w_ada, b_ada, w_in, na_rpb, swa_sink, mla_q_norm, mla_kv_norm, mla_w_uq, mla_w_ukv, gqa_q_norm, gqa_k_norm, w_out, ln1_g, ln1_b, ffn_w_gate, ffn_w_up, ffn_conv_w, ffn_conv_b, ffn_w_down, ln2_g, ln2_b, loss_target, m_c_ctx, m_w_ada, m_b_ada, m_w_in, m_na_rpb, m_swa_sink, m_mla_q_norm, m_mla_kv_norm, m_mla_w_uq, m_mla_w_ukv, m_gqa_q_norm, m_gqa_k_norm, m_w_out, m_ln1_g, m_ln1_b, m_ffn_w_gate, m_ffn_w_up, m_ffn_conv_w, m_ffn_conv_b, m_ffn_w_down, m_ln2_g, m_ln2_b, v_c_ctx, v_w_ada, v_b_ada, v_w_in, v_na_rpb, v_swa_sink, v_mla_q_norm, v_mla_kv_norm, v_mla_w_uq, v_mla_w_ukv, v_gqa_q_norm, v_gqa_k_norm, v_w_out, v_ln1_g, v_ln1_b, v_ffn_w_gate, v_ffn_w_up, v_ffn_conv_w, v_ffn_conv_b, v_ffn_w_down, v_ln2_g, v_ln2_b):
    args = locals()
    w = {n: args[n] for n in _WEIGHTS}
    m_in = {n: args["m_" + n] for n in _WEIGHTS}
    v_in = {n: args["v_" + n] for n in _WEIGHTS}
    return _train_step(x, c, ctx, loss_target, w, m_in, v_in)
```

```python
import functools
import math

import numpy as np
import jax
import jax.numpy as jnp
from jax import lax
from jax.experimental import pallas as pl
from jax.experimental.pallas import tpu as pltpu

F32 = jnp.float32
BF16 = jnp.bfloat16
MESH = pl.DeviceIdType.MESH

GRID_W = 64
HEAD_DIM = 128
NA_HEADS, NA_WIN_R, NA_WIN_C = 4, 8, 16
SWA_HEADS, SWA_KV_HEADS, SWA_WINDOW = 4, 2, 128
MLA_HEADS, MLA_Q_LORA, MLA_KV_LORA, MLA_NOPE, MLA_ROPE, MLA_V = 4, 384, 128, 128, 64, 128
GQA_HEADS, GQA_KV_HEADS = 4, 2
ROPE_THETA = 10000.0
EPS = 1e-6
NEG = -1e30
DEPTH = 2
DEEPNORM_ALPHA = (2 * DEPTH) ** 0.25
ADAM_LR, ADAM_B1, ADAM_B2, ADAM_EPS, ADAM_WD, ADAM_STEP = 0.001, 0.9, 0.999, 1e-08, 0.01, 10

LANE = 128
V7X_VMEM_BYTES = 64 * 1024 * 1024
VMEM_LIMIT = 56 * 1024 * 1024
MM_VMEM_BUDGET = 40 * 1024 * 1024
EW_VMEM_BUDGET = 28 * 1024 * 1024
BQ = 128

CB_NA_Q, CB_NA_K, CB_NA_V = 0, 4, 8
CB_SWA_Q, CB_SWA_K, CB_SWA_V = 12, 16, 18
CB_CQ, CB_CKV = 20, 23
CB_GQA_Q, CB_GQA_K, CB_GQA_V = 24, 28, 30
CB_KPE = 32
PCOLS = 33 * LANE
IN_COLS = 4160


def _cparams(sem=None, **kw):
    return pltpu.CompilerParams(dimension_semantics=sem, vmem_limit_bytes=VMEM_LIMIT, **kw)


def _pick(n, target, mult=LANE):
    best = None
    for d in range(mult, min(n, target) + 1, mult):
        if n % d == 0:
            best = d
    return n if best is None else best


def _mm(a, b, *, mode="nn", out_dtype=F32, a_off=0, a_k=None, tm=1408, tn=1408, tk=2816, exact=False, add=None,
        stack=None, split4=False, rows=None, rides=(), b_layer=None, name):
    b_shape = b.shape if b_layer is None else b.shape[1:]
    if mode == "tn":
        K, M = a.shape
        K2, N = b_shape
    elif mode == "nn":
        M, K = a.shape
        K2, N = b_shape
    else:
        M, K = a.shape
        N, K2 = b_shape
    if a_k is not None:
        K = a_k
    if rows is not None:
        if mode == "tn":
            assert rows <= min(K, K2)
            K = K2 = rows
        else:
            assert rows <= M
            M = rows
    assert K == K2, (a.shape, b.shape, mode)
    m_mult = LANE if mode == "tn" else 16
    n_cols = N // 4 if split4 else N
    bm, bn, bk = _pick(M, tm, m_mult), _pick(n_cols, tn), _pick(K, tk)
    sa, sb, so = a.dtype.itemsize, b.dtype.itemsize, jnp.dtype(out_dtype).itemsize

    def vmem_estimate():
        acc = bm * bn * 4 if K // bk > 1 else 0
        return 2 * (bm * bk * sa + bk * bn * sb) + acc + 2 * bm * bn * so + (2 * bm * bn * 4 if add is not None else 0)

    while vmem_estimate() > MM_VMEM_BUDGET:
        if bm >= bn and _pick(M, bm - 1, m_mult) < bm:
            bm = _pick(M, bm - 1, m_mult)
        elif _pick(n_cols, bn - 1) < bn:
            bn = _pick(n_cols, bn - 1)
        else:
            assert _pick(K, bk - 1) < bk, "no tiling fits VMEM"
            bk = _pick(K, bk - 1)
    assert a_off % bk == 0
    koff = a_off // bk
    nk = K // bk
    if mode == "tn":
        a_spec = pl.BlockSpec((bk, bm), lambda i, j, k: (k, i))
        b_spec = pl.BlockSpec((bk, bn), lambda i, j, k: (k, j))
        dims = (((0,), (0,)), ((), ()))
    elif mode == "nn":
        a_spec = pl.BlockSpec((bm, bk), lambda i, j, k: (i, k + koff))
        b_spec = pl.BlockSpec((bk, bn), lambda i, j, k: (k, j))
        dims = (((1,), (0,)), ((), ()))
    else:
        a_spec = pl.BlockSpec((bm, bk), lambda i, j, k: (i, k + koff))
        b_spec = pl.BlockSpec((bn, bk), lambda i, j, k: (j, k))
        dims = (((1,), (1,)), ((), ()))
    if b_layer is not None:
        b_block, b_index = b_spec.block_shape, b_spec.index_map
        b_spec = pl.BlockSpec((None,) + tuple(b_block), lambda i, j, k: (b_layer,) + tuple(b_index(i, j, k)))

    operands = [a, b]
    in_specs = [a_spec, b_spec]
    if add is not None:
        operands.append(add)
        in_specs.append(pl.BlockSpec((bm, bn), lambda i, j, k: (i, j)))
    aliases = {}
    if stack is None:
        out_spec = pl.BlockSpec((bm, bn), lambda i, j, k: (i, j))
        out_shape = jax.ShapeDtypeStruct((M, N), out_dtype)
    else:
        n_layers, layer, buf = stack
        if split4:
            nb = N // 4 // bn
            assert N % (4 * bn) == 0
            out_spec = pl.BlockSpec((None, None, bm, bn), lambda i, j, k: (layer, j // nb, i, j % nb))
            out_shape = jax.ShapeDtypeStruct((n_layers, 4, M, N // 4), out_dtype)
        else:
            out_spec = pl.BlockSpec((None, bm, bn), lambda i, j, k: (layer, i, j))
            out_shape = jax.ShapeDtypeStruct((n_layers, M, N), out_dtype)
        if buf is not None:
            aliases = {len(operands): 0}
            operands.append(buf)
            in_specs.append(pl.BlockSpec(memory_space=pl.ANY))
    has_add, has_buf = add is not None, bool(aliases)
    n_ride = sum(len(r.arrays) for r in rides)
    aliases.update(_ride_aliases(rides, len(operands), 1))
    grid = (M // bm, N // bn, nk)

    def body(*refs):
        a_ref, b_ref = refs[:2]
        add_ref = refs[2] if has_add else None
        base = 2 + has_add + has_buf
        o_ref = refs[base + n_ride]
        scratch = refs[base + 2 * n_ride + 1:]
        if rides:
            ride_groups = _ride_split(rides, refs[base:base + n_ride], refs[base + n_ride + 1:base + 2 * n_ride + 1],
                                      scratch[1 if nk > 1 else 0:])
            steps = [pl.program_id(d) for d in range(3)]
            pl.when((steps[0] == 0) & (steps[1] == 0) & (steps[2] == 0))(lambda: _ride_start(rides, ride_groups))
        compute(a_ref, b_ref, add_ref, o_ref, scratch[0] if nk > 1 else None)
        if rides:
            pl.when((steps[0] == grid[0] - 1) & (steps[1] == grid[1] - 1) & (steps[2] == grid[2] - 1))(
                lambda: _ride_wait(rides, ride_groups))

    def compute(a_ref, b_ref, add_ref, o_ref, acc_ref):
        if exact:
            prod = lax.dot_general(a_ref[...].astype(F32), b_ref[...].astype(F32), dims,
                                   precision=lax.Precision.HIGHEST, preferred_element_type=F32)
        else:
            prod = lax.dot_general(a_ref[...].astype(BF16), b_ref[...].astype(BF16), dims, preferred_element_type=F32)

        def finish(res):
            if has_add:
                res = res + add_ref[...].astype(F32)
            o_ref[...] = res.astype(o_ref.dtype)

        if nk == 1:
            finish(prod)
            return
        k = pl.program_id(2)

        @pl.when(k == 0)
        def _():
            acc_ref[...] = prod

        @pl.when((k > 0) & (k < nk - 1))
        def _():
            acc_ref[...] += prod

        @pl.when(k == nk - 1)
        def _():
            finish(acc_ref[...] + prod)

    outs = pl.pallas_call(
        body, name=name, grid=grid,
        in_specs=in_specs + [_ANY] * n_ride, out_specs=[out_spec] + [_ANY] * n_ride,
        out_shape=[out_shape] + [s for r in rides for s in r.out_shapes()],
        scratch_shapes=([pltpu.VMEM((bm, bn), F32)] if nk > 1 else []) + _ride_scratch(rides),
        input_output_aliases=aliases,
        compiler_params=_cparams(("arbitrary",) * 3 if rides else ("parallel", "parallel", "arbitrary")),
    )(*operands, *[a for r in rides for a in r.arrays])
    if not rides:
        return outs[0]
    return outs[0], _ride_outputs(rides, outs[1:])


def _row_block(T, S):
    return _pick(math.gcd(T, S), 256, 16)


def _ln_stats(x):
    mu = jnp.mean(x, axis=-1, keepdims=True)
    xc = x - mu
    var = jnp.mean(xc * xc, axis=-1, keepdims=True)
    rstd = lax.rsqrt(var + EPS)
    return xc * rstd, rstd


def _ln_bwd(dxhat, xhat, rstd):
    m1 = jnp.mean(dxhat, axis=-1, keepdims=True)
    m2 = jnp.mean(dxhat * xhat, axis=-1, keepdims=True)
    return rstd * (dxhat - m1 - xhat * m2)


def _sel(ref, is_ctx):
    return jnp.where(is_ctx, ref[1:2, :], ref[0:1, :])


def _mod_fwd(x, shift, scale, S, *, rows=None, name):
    T, D = (x.shape[0] if rows is None else rows), x.shape[1]
    bt = _row_block(T, S)

    def body(x_ref, sh_ref, sc_ref, o_ref):
        is_ctx = pl.program_id(0) * bt >= S
        xhat, _ = _ln_stats(x_ref[...])
        o_ref[...] = (xhat * (1.0 + _sel(sc_ref, is_ctx)) + _sel(sh_ref, is_ctx)).astype(o_ref.dtype)

    return pl.pallas_call(
        body, name=name, grid=(T // bt,),
        in_specs=[pl.BlockSpec((bt, D), lambda i: (i, 0)), pl.BlockSpec((2, D), lambda i: (0, 0)),
                  pl.BlockSpec((2, D), lambda i: (0, 0))],
        out_specs=pl.BlockSpec((bt, D), lambda i: (i, 0)),
        out_shape=jax.ShapeDtypeStruct((T, D), BF16),
        compiler_params=_cparams(("parallel",)),
    )(x, shift, scale)


def _acc_groups(acc_ref, row, val, is_ctx):
    f = jnp.where(is_ctx, 1.0, 0.0).astype(F32)
    acc_ref[row:row + 1, :] += val * (1.0 - f)
    acc_ref[row + 1:row + 2, :] += val * f


def _mod_bwd(x, dh, scale, dx_in, S, *, name):
    T, D = dh.shape
    bt = _row_block(T, S)
    in_blocks = dx_in.shape[0] // bt

    def body(x_ref, dh_ref, sc_ref, dxi_ref, dx_ref, acc_ref):
        i = pl.program_id(0)
        is_ctx = i * bt >= S

        @pl.when(i == 0)
        def _():
            acc_ref[...] = jnp.zeros_like(acc_ref)

        xhat, rstd = _ln_stats(x_ref[...])
        dh = dh_ref[...].astype(F32)
        dxhat = dh * (1.0 + _sel(sc_ref, is_ctx))
        dxi = dxi_ref[...] if in_blocks * bt == T else jnp.where(i < in_blocks, dxi_ref[...], 0.0)
        dx_ref[...] = dxi + _ln_bwd(dxhat, xhat, rstd)
        _acc_groups(acc_ref, 0, jnp.sum(dh, axis=0, keepdims=True), is_ctx)
        _acc_groups(acc_ref, 2, jnp.sum(dh * xhat, axis=0, keepdims=True), is_ctx)

    return pl.pallas_call(
        body, name=name, grid=(T // bt,),
        in_specs=[pl.BlockSpec((bt, D), lambda i: (i, 0)), pl.BlockSpec((bt, D), lambda i: (i, 0)),
                  pl.BlockSpec((2, D), lambda i: (0, 0)),
                  pl.BlockSpec((bt, D), lambda i: (jnp.minimum(i, in_blocks - 1), 0))],
        out_specs=[pl.BlockSpec((bt, D), lambda i: (i, 0)), pl.BlockSpec((8, D), lambda i: (0, 0))],
        out_shape=[jax.ShapeDtypeStruct((T, D), F32), jax.ShapeDtypeStruct((8, D), F32)],
        compiler_params=_cparams(("arbitrary",)),
    )(x, dh, scale, dx_in)


def _res_fwd(x, z, gate, lg, lb, S, *, name):
    T, D = z.shape
    bt = _row_block(T, S)

    def body(x_ref, z_ref, g_ref, lg_ref, lb_ref, o_ref):
        is_ctx = pl.program_id(0) * bt >= S
        u = DEEPNORM_ALPHA * x_ref[...] + _sel(g_ref, is_ctx) * z_ref[...]
        uhat, _ = _ln_stats(u)
        o_ref[...] = uhat * lg_ref[...] + lb_ref[...]

    row = pl.BlockSpec((bt, D), lambda i: (i, 0))
    return pl.pallas_call(
        body, name=name, grid=(T // bt,),
        in_specs=[row, row, pl.BlockSpec((2, D), lambda i: (0, 0)), pl.BlockSpec((1, D), lambda i: (0, 0)),
                  pl.BlockSpec((1, D), lambda i: (0, 0))],
        out_specs=row,
        out_shape=jax.ShapeDtypeStruct((T, D), F32),
        compiler_params=_cparams(("parallel",)),
    )(x, z, gate, lg, lb)


def _res_bwd(x, z, gate, lg, dy, S, *, name):
    T, D = z.shape
    bt = _row_block(T, S)

    def body(x_ref, z_ref, g_ref, lg_ref, dy_ref, dx_ref, dz_ref, acc_ref):
        i = pl.program_id(0)
        is_ctx = i * bt >= S

        @pl.when(i == 0)
        def _():
            acc_ref[...] = jnp.zeros_like(acc_ref)

        gate_v = _sel(g_ref, is_ctx)
        zv = z_ref[...]
        u = DEEPNORM_ALPHA * x_ref[...] + gate_v * zv
        uhat, rstd = _ln_stats(u)
        dyv = dy_ref[...]
        du = _ln_bwd(dyv * lg_ref[...], uhat, rstd)
        dx_ref[...] = DEEPNORM_ALPHA * du
        dz_ref[...] = (gate_v * du).astype(dz_ref.dtype)
        _acc_groups(acc_ref, 0, jnp.sum(du * zv, axis=0, keepdims=True), is_ctx)
        acc_ref[2:3, :] += jnp.sum(dyv * uhat, axis=0, keepdims=True)
        acc_ref[3:4, :] += jnp.sum(dyv, axis=0, keepdims=True)

    row = pl.BlockSpec((bt, D), lambda i: (i, 0))
    return pl.pallas_call(
        body, name=name, grid=(T // bt,),
        in_specs=[row, row, pl.BlockSpec((2, D), lambda i: (0, 0)), pl.BlockSpec((1, D), lambda i: (0, 0)), row],
        out_specs=[row, row, pl.BlockSpec((8, D), lambda i: (0, 0))],
        out_shape=[jax.ShapeDtypeStruct((T, D), F32), jax.ShapeDtypeStruct((T, D), BF16),
                   jax.ShapeDtypeStruct((8, D), F32)],
        compiler_params=_cparams(("arbitrary",)),
    )(x, z, gate, lg, dy)


def _loss_fwd_bwd(y, target, S, *, name):
    T, D = y.shape
    bt = _row_block(T, S)
    n_lat = S // bt

    def body(y_ref, t_ref, dy_ref, l_ref):
        i = pl.program_id(0)

        @pl.when(i == 0)
        def _():
            l_ref[...] = jnp.zeros_like(l_ref)

        keep = jnp.where(i * bt >= S, 0.0, 1.0).astype(F32)
        err = (y_ref[...] - t_ref[...]) * keep
        dy_ref[...] = err * (1.0 / D)
        l_ref[...] += jnp.sum(err * err) * (0.5 / D)

    return pl.pallas_call(
        body, name=name, grid=(T // bt,),
        in_specs=[pl.BlockSpec((bt, D), lambda i: (i, 0)),
                  pl.BlockSpec((bt, D), lambda i: (jnp.minimum(i, n_lat - 1), 0))],
        out_specs=[pl.BlockSpec((bt, D), lambda i: (i, 0)), pl.BlockSpec((8, LANE), lambda i: (0, 0))],
        out_shape=[jax.ShapeDtypeStruct((T, D), F32), jax.ShapeDtypeStruct((8, LANE), F32)],
        compiler_params=_cparams(("arbitrary",)),
    )(y, target)


def _rope_tables(S, C, dim):
    half = dim // 4
    t = jnp.arange(S)
    row = (t // GRID_W).astype(F32)
    col = (t % GRID_W).astype(F32)
    inv = ROPE_THETA ** (-jnp.arange(half, dtype=F32) / half)
    ar, ac = row[:, None] * inv[None, :], col[:, None] * inv[None, :]
    cos = jnp.concatenate([jnp.cos(ar), jnp.cos(ar), jnp.cos(ac), jnp.cos(ac)], axis=1)
    ss = jnp.concatenate([-jnp.sin(ar), jnp.sin(ar), -jnp.sin(ac), jnp.sin(ac)], axis=1)
    cos = jnp.pad(cos, ((0, C), (0, LANE - dim)), constant_values=1.0)
    ss = jnp.pad(ss, ((0, C), (0, LANE - dim)))
    return cos, ss


def _rope(x, cos, ss, half):
    lane = lax.broadcasted_iota(jnp.int32, x.shape, 1)
    first = (lane % (2 * half)) < half
    partner = jnp.where(first, pltpu.roll(x, LANE - half, 1), pltpu.roll(x, half, 1))
    return x * cos + partner * ss


def _rms(x):
    r = lax.rsqrt(jnp.mean(x * x, axis=-1, keepdims=True) + EPS)
    return x * r, r


_CAST_BLOCKS = tuple(range(0, 12)) + (18, 19, 30, 31)
_ROPE_BLOCKS = tuple(range(12, 18))
_GQA_Q_BLOCKS = tuple(range(24, 28))
_GQA_K_BLOCKS = (28, 29)


def _prep_fwd(p, tabs, gq, gk, mq, mkv, S, *, name):
    T = p.shape[0]
    bt = _row_block(T, S)
    cA, sA, cP, sP = tabs

    def body(p_ref, cA_ref, sA_ref, cP_ref, sP_ref, gq_ref, gk_ref, mq_ref, mkv_ref, o_ref):
        def blk(b):
            return p_ref[:, b * LANE:(b + 1) * LANE].astype(F32)

        def put(b, val):
            o_ref[:, b * LANE:(b + 1) * LANE] = val.astype(o_ref.dtype)

        cA_v, sA_v = cA_ref[...], sA_ref[...]
        for b in _CAST_BLOCKS:
            put(b, blk(b))
        for b in _ROPE_BLOCKS:
            put(b, _rope(blk(b), cA_v, sA_v, 32))
        for b in _GQA_Q_BLOCKS:
            put(b, _rope(_rms(blk(b))[0] * gq_ref[...], cA_v, sA_v, 32))
        for b in _GQA_K_BLOCKS:
            put(b, _rope(_rms(blk(b))[0] * gk_ref[...], cA_v, sA_v, 32))
        put(CB_KPE, _rope(blk(CB_KPE), cP_ref[...], sP_ref[...], 16))
        cq = p_ref[:, CB_CQ * LANE:CB_CKV * LANE].astype(F32)
        o_ref[:, CB_CQ * LANE:CB_CKV * LANE] = (_rms(cq)[0] * mq_ref[...]).astype(o_ref.dtype)
        put(CB_CKV, _rms(blk(CB_CKV))[0] * mkv_ref[...])

    row128 = pl.BlockSpec((bt, LANE), lambda i: (i, 0))
    vec = lambda n: pl.BlockSpec((1, n), lambda i: (0, 0))
    return pl.pallas_call(
        body, name=name, grid=(T // bt,),
        in_specs=[pl.BlockSpec((bt, PCOLS), lambda i: (i, 0)), row128, row128, row128, row128,
                  vec(LANE), vec(LANE), vec(MLA_Q_LORA), vec(LANE)],
        out_specs=pl.BlockSpec((bt, PCOLS), lambda i: (i, 0)),
        out_shape=jax.ShapeDtypeStruct((T, PCOLS), BF16),
        compiler_params=_cparams(("parallel",)),
    )(p, cA, sA, cP, sP, gq, gk, mq, mkv)


def _prep_bwd(p, grads, tabs, gq, gk, mq, mkv, S, *, name):
    T = p.shape[0]
    bt = _row_block(T, S)
    cA, sA, cP, sP = tabs
    arrays = []
    where = {}
    for key, (arr, cb) in grads.items():
        idx = next((n for n, a in enumerate(arrays) if a is arr), None)
        if idx is None:
            arrays.append(arr)
            idx = len(arrays) - 1
        where[key] = (idx, cb)
    ng = len(arrays)

    def body(*refs):
        p_ref, cA_ref, sA_ref, cP_ref, sP_ref, gq_ref, gk_ref, mq_ref, mkv_ref = refs[:9]
        g_refs = refs[9:9 + ng]
        o_ref, acc_ref = refs[9 + ng:]
        i = pl.program_id(0)

        @pl.when(i == 0)
        def _():
            acc_ref[...] = jnp.zeros_like(acc_ref)

        def blk(b):
            return p_ref[:, b * LANE:(b + 1) * LANE].astype(F32)

        def grad(b, width=LANE):
            idx, cb = where[b]
            return g_refs[idx][:, cb * LANE:cb * LANE + width].astype(F32)

        def put(b, val):
            o_ref[:, b * LANE:(b + 1) * LANE] = val.astype(o_ref.dtype)

        def rms_bwd(x, dy, g, row, width):
            n, r = _rms(x)
            acc_ref[row:row + 1, 0:width] += jnp.sum(dy * n, axis=0, keepdims=True)
            dn = dy * g
            return r * (dn - n * jnp.mean(dn * n, axis=-1, keepdims=True))

        cA_v, sA_v = cA_ref[...], sA_ref[...]
        for b in _CAST_BLOCKS:
            put(b, grad(b))
        for b in _ROPE_BLOCKS:
            put(b, _rope(grad(b), cA_v, -sA_v, 32))
        for b in _GQA_Q_BLOCKS:
            put(b, rms_bwd(blk(b), _rope(grad(b), cA_v, -sA_v, 32), gq_ref[...], 0, LANE))
        for b in _GQA_K_BLOCKS:
            put(b, rms_bwd(blk(b), _rope(grad(b), cA_v, -sA_v, 32), gk_ref[...], 1, LANE))
        put(CB_KPE, _rope(grad(CB_KPE), cP_ref[...], -sP_ref[...], 16))
        dcq = rms_bwd(p_ref[:, CB_CQ * LANE:CB_CKV * LANE].astype(F32), grad(CB_CQ, MLA_Q_LORA), mq_ref[...], 2, MLA_Q_LORA)
        o_ref[:, CB_CQ * LANE:CB_CKV * LANE] = dcq.astype(o_ref.dtype)
        put(CB_CKV, rms_bwd(blk(CB_CKV), grad(CB_CKV), mkv_ref[...], 3, LANE))

    row128 = pl.BlockSpec((bt, LANE), lambda i: (i, 0))
    vec = lambda n: pl.BlockSpec((1, n), lambda i: (0, 0))
    g_specs = [pl.BlockSpec((bt, a.shape[1]), lambda i: (i, 0)) for a in arrays]
    return pl.pallas_call(
        body, name=name, grid=(T // bt,),
        in_specs=[pl.BlockSpec((bt, PCOLS), lambda i: (i, 0)), row128, row128, row128, row128,
                  vec(LANE), vec(LANE), vec(MLA_Q_LORA), vec(LANE)] + g_specs,
        out_specs=[pl.BlockSpec((bt, PCOLS), lambda i: (i, 0)), pl.BlockSpec((8, MLA_Q_LORA), lambda i: (0, 0))],
        out_shape=[jax.ShapeDtypeStruct((T, PCOLS), BF16), jax.ShapeDtypeStruct((8, MLA_Q_LORA), F32)],
        compiler_params=_cparams(("arbitrary",)),
    )(p, cA, sA, cP, sP, gq, gk, mq, mkv, *arrays)


def _pe_rope(qm, tabs, S, sign, out_dtype, *, name):
    T, N = qm.shape
    bt = _row_block(T, S)
    cP, sP = tabs[2], tabs[3]

    def body(x_ref, c_ref, s_ref, o_ref):
        for b in range(MLA_HEADS):
            o_ref[:, b * LANE:(b + 1) * LANE] = x_ref[:, b * LANE:(b + 1) * LANE].astype(o_ref.dtype)
        for b in range(MLA_HEADS, 2 * MLA_HEADS):
            x = x_ref[:, b * LANE:(b + 1) * LANE].astype(F32)
            o_ref[:, b * LANE:(b + 1) * LANE] = _rope(x, c_ref[...], sign * s_ref[...], 16).astype(o_ref.dtype)

    row128 = pl.BlockSpec((bt, LANE), lambda i: (i, 0))
    return pl.pallas_call(
        body, name=name, grid=(T // bt,),
        in_specs=[pl.BlockSpec((bt, N), lambda i: (i, 0)), row128, row128],
        out_specs=pl.BlockSpec((bt, N), lambda i: (i, 0)),
        out_shape=jax.ShapeDtypeStruct((T, N), out_dtype),
        compiler_params=_cparams(("parallel",)),
    )(qm, cP, sP)


def _dot_nt(a, b):
    return lax.dot_general(a, b, (((1,), (1,)), ((), ())), preferred_element_type=F32)


def _dot_tn(a, b):
    return lax.dot_general(a, b, (((0,), (0,)), ((), ())), preferred_element_type=F32)


def _dot(a, b):
    return jnp.dot(a, b, preferred_element_type=F32)


def _window_fns(band, S, n_var):
    n_lat = S // BQ
    if band is None:
        return None
    reach, span = band

    def fns(j):
        start = jnp.clip(j - reach, 0, n_lat - span)
        return start, jnp.clip(j - start, 0, n_var - 1)

    return fns


class _AttnCfg:
    def __init__(self, *, Hkv, G, S, C, band, scale, n_var=0, bias_per_head=False, has_sink=False, two=False, bq=BQ,
                 ctx_queries=True):
        self.Hkv, self.G, self.S, self.C, self.band, self.scale = Hkv, G, S, C, band, scale
        self.n_var, self.bias_per_head, self.has_sink, self.two = n_var, bias_per_head, has_sink, two
        self.W = S if band is None else band[1] * BQ
        self.T = S + C
        self.bq, self.ctx_queries = bq, ctx_queries
        assert band is None or bq == BQ
        assert S % bq == 0 and C % bq == 0


def _attn_probs(cfg, j, q_ref, k_ref, q2_ref, k2_ref, bias_ref, sink_ref):
    G, S, C, W = cfg.G, cfg.S, cfg.C, cfg.W
    is_ctx = j * cfg.bq >= S
    if cfg.band is None:
        off, var = 0, 0
    else:
        start, var = _window_fns(cfg.band, S, cfg.n_var)(j)
        off = pl.multiple_of(start * BQ, BQ)
    qt = q_ref[...]
    qs = jnp.concatenate([qt[:, g * LANE:(g + 1) * LANE] for g in range(G)], axis=0) if G > 1 else qt
    kw = k_ref[pl.ds(off, W), :]
    kc = k_ref[pl.ds(S, C), :]
    s_w = _dot_nt(qs, kw)
    s_c = _dot_nt(qs, kc)
    q2s = k2w = k2c = None
    if cfg.two:
        q2s = q2_ref[...]
        k2w = k2_ref[pl.ds(off, W), :]
        k2c = k2_ref[pl.ds(S, C), :]
        s_w = s_w + _dot_nt(q2s, k2w)
        s_c = s_c + _dot_nt(q2s, k2c)
    operands = (off, var, qs, kw, kc, q2s, k2w, k2c)
    if not cfg.n_var and not cfg.has_sink:
        if cfg.ctx_queries:
            s_w = jnp.where(is_ctx, NEG, s_w)
        m = jnp.maximum(jnp.max(s_w, axis=-1, keepdims=True), jnp.max(s_c, axis=-1, keepdims=True))
        c2 = cfg.scale * math.log2(math.e)
        e_w = jnp.exp2((s_w - m) * c2)
        e_c = jnp.exp2((s_c - m) * c2)
        total = jnp.sum(e_w, axis=-1, keepdims=True) + jnp.sum(e_c, axis=-1, keepdims=True)
        inv = 1.0 / total
        return e_w * inv, e_c * inv, m * c2 + jnp.log2(total), operands
    s_w = s_w * cfg.scale
    s_c = s_c * cfg.scale
    if cfg.n_var:
        b = bias_ref[0, pl.ds(var, 1)][0]
        s_w = s_w + (jnp.concatenate([b] * G, axis=0) if G > 1 else b)
    if cfg.ctx_queries:
        s_w = jnp.where(is_ctx, NEG, s_w)
    m = jnp.maximum(jnp.max(s_w, axis=-1, keepdims=True), jnp.max(s_c, axis=-1, keepdims=True))
    if cfg.has_sink:
        sink = sink_ref[0][:, 0:1]
        m = jnp.maximum(m, sink)
    e_w = jnp.exp(s_w - m)
    e_c = jnp.exp(s_c - m)
    l = jnp.sum(e_w, axis=-1, keepdims=True) + jnp.sum(e_c, axis=-1, keepdims=True)
    p_s = None
    if cfg.has_sink:
        e_s = jnp.exp(sink - m)
        l = l + e_s
    inv = 1.0 / l
    if cfg.has_sink:
        p_s = e_s * inv
    return e_w * inv, e_c * inv, p_s, operands


def _attn_specs(cfg, q_cb, k_cb, v_cb, q2_cb, k2_cb):
    G, T, bq = cfg.G, cfg.T, cfg.bq
    specs = [pl.BlockSpec((bq, G * LANE), lambda h, j: (j, q_cb // G + h)),
             pl.BlockSpec((T, LANE), lambda h, j: (0, k_cb + h)),
             pl.BlockSpec((T, LANE), lambda h, j: (0, v_cb + h))]
    if cfg.two:
        specs += [pl.BlockSpec((bq, LANE), lambda h, j: (j, q2_cb + h)),
                  pl.BlockSpec((T, LANE), lambda h, j: (0, k2_cb))]
    if cfg.n_var:
        if cfg.bias_per_head:
            specs.append(pl.BlockSpec((1, cfg.n_var, BQ, cfg.W), lambda h, j: (h, 0, 0, 0)))
        else:
            specs.append(pl.BlockSpec((1, cfg.n_var, BQ, cfg.W), lambda h, j: (0, 0, 0, 0)))
    if cfg.has_sink:
        specs.append(pl.BlockSpec((1, G * BQ, LANE), lambda h, j: (h, 0, 0)))
    return specs


def _attn_unpack(cfg, refs):
    refs = list(refs)
    q_ref, k_ref, v_ref = refs[:3]
    n = 3
    q2_ref = k2_ref = bias_ref = sink_ref = None
    if cfg.two:
        q2_ref, k2_ref = refs[n:n + 2]
        n += 2
    if cfg.n_var:
        bias_ref = refs[n]
        n += 1
    if cfg.has_sink:
        sink_ref = refs[n]
        n += 1
    return (q_ref, k_ref, v_ref, q2_ref, k2_ref, bias_ref, sink_ref), refs[n:]


def _attn_fwd(cfg, q, q_cb, k, k_cb, v, v_cb, *, q2=None, q2_cb=0, k2=None, k2_cb=0, bias=None, sink=None, rides=(),
              into=None, want_lse=False, name):
    G, T, S, C, W = cfg.G, cfg.T, cfg.S, cfg.C, cfg.W
    assert q_cb % G == 0
    operands = [q, k, v] + ([q2, k2] if cfg.two else []) + ([bias] if cfg.n_var else []) + ([sink] if cfg.has_sink else [])

    bq = cfg.bq
    n_ride = sum(len(r.arrays) for r in rides)
    n_q = T // bq
    has_buf = into is not None and into[0] is not None
    col0 = 0 if into is None else into[1]
    width = cfg.Hkv * G * LANE if into is None else into[2] * LANE
    assert col0 % G == 0

    def body(*refs):
        (q_ref, k_ref, v_ref, q2_ref, k2_ref, bias_ref, sink_ref), rest = _attn_unpack(cfg, refs)
        rest = rest[:n_ride] + rest[n_ride + has_buf:]
        o_ref = rest[n_ride]
        lse_ref = rest[n_ride + 1] if want_lse else None
        n_o = 1 + want_lse
        ride_groups = _ride_split(rides, rest[:n_ride], rest[n_ride + n_o:2 * n_ride + n_o], rest[2 * n_ride + n_o:])
        h = pl.program_id(0)
        j = pl.program_id(1)
        if rides:
            pl.when((h == 0) & (j == 0))(lambda: _ride_start(rides, ride_groups))

        def block():
            p_w, p_c, lse2, (off, _, _, _, _, _, _, _) = _attn_probs(cfg, j, q_ref, k_ref, q2_ref, k2_ref, bias_ref, sink_ref)
            o = _dot(p_w.astype(BF16), v_ref[pl.ds(off, W), :]) + _dot(p_c.astype(BF16), v_ref[pl.ds(S, C), :])
            for g in range(G):
                o_ref[:, g * LANE:(g + 1) * LANE] = o[g * bq:(g + 1) * bq].astype(o_ref.dtype)
                if want_lse:
                    lse_ref[:, g * LANE:(g + 1) * LANE] = jnp.broadcast_to(lse2[g * bq:(g + 1) * bq], (bq, LANE))

        if cfg.ctx_queries:
            block()
        else:
            pl.when(j * bq < S)(block)

            @pl.when(j * bq >= S)
            def _():
                o_ref[...] = jnp.zeros_like(o_ref)
                if want_lse:
                    lse_ref[...] = jnp.zeros_like(lse_ref)

        if rides:
            pl.when((h == cfg.Hkv - 1) & (j == n_q - 1))(lambda: _ride_wait(rides, ride_groups))

    n_o = 1 + want_lse
    aliases = _ride_aliases(rides, len(operands), n_o)
    if has_buf:
        aliases[len(operands) + n_ride] = 0
    head_blk = pl.BlockSpec((bq, G * LANE), lambda h, j: (j, h))
    outs = pl.pallas_call(
        body, name=name, grid=(cfg.Hkv, n_q),
        in_specs=_attn_specs(cfg, q_cb, k_cb, v_cb, q2_cb, k2_cb) + [_ANY] * (n_ride + has_buf),
        out_specs=[pl.BlockSpec((bq, G * LANE), lambda h, j: (j, col0 // G + h))] + [head_blk] * want_lse + [_ANY] * n_ride,
        out_shape=[jax.ShapeDtypeStruct((T, width), BF16)] + [jax.ShapeDtypeStruct((T, cfg.Hkv * G * LANE), F32)] * want_lse
        + [s for r in rides for s in r.out_shapes()],
        scratch_shapes=_ride_scratch(rides),
        input_output_aliases=aliases,
        compiler_params=_cparams(("arbitrary", "arbitrary") if rides else ("parallel", "parallel")),
    )(*operands, *[a for r in rides for a in r.arrays], *([into[0]] if has_buf else []))
    first = (outs[0], outs[1]) if want_lse else outs[0]
    if not rides:
        return first
    return first, _ride_outputs(rides, outs[n_o:])


def _attn_bwd(cfg, q, q_cb, k, k_cb, v, v_cb, do, do_cb, *, q2=None, q2_cb=0, k2=None, k2_cb=0, bias=None, sink=None,
              want_dbias=False, dq_dtype=F32, rides=(), o=None, lse=None, key_chunk=1024, name):
    G, T, S, C, W, Hkv = cfg.G, cfg.T, cfg.S, cfg.C, cfg.W, cfg.Hkv
    assert q_cb % G == 0 and do_cb % G == 0 and not (want_dbias and G > 1)
    operands = [q, k, v] + ([q2, k2] if cfg.two else []) + ([bias] if cfg.n_var else []) + ([sink] if cfg.has_sink else [])
    operands.append(do)
    in_specs = _attn_specs(cfg, q_cb, k_cb, v_cb, q2_cb, k2_cb)
    bq = cfg.bq
    do_blocks = do.shape[0] // bq
    assert do.shape[0] == T or (do.shape[0] == S and not cfg.ctx_queries)
    in_specs.append(pl.BlockSpec((bq, G * LANE), lambda h, j: (jnp.minimum(j, do_blocks - 1), do_cb // G + h)))
    by_chunks = o is not None
    if by_chunks:
        assert cfg.band is None and not cfg.n_var and not cfg.has_sink and o.shape[0] == T
        operands += [o, lse]
        in_specs += [pl.BlockSpec((bq, G * LANE), lambda h, j: (j, do_cb // G + h)),
                     pl.BlockSpec((bq, G * LANE), lambda h, j: (j, h))]
    n_x = 2 if by_chunks else 0

    out_specs = [pl.BlockSpec((bq, G * LANE), lambda h, j: (j, h)),
                 pl.BlockSpec((T, LANE), lambda h, j: (0, h)),
                 pl.BlockSpec((T, LANE), lambda h, j: (0, h))]
    out_shape = [jax.ShapeDtypeStruct((T, Hkv * G * LANE), dq_dtype),
                 jax.ShapeDtypeStruct((T, Hkv * LANE), F32),
                 jax.ShapeDtypeStruct((T, Hkv * LANE), F32)]
    if cfg.two:
        out_specs += [pl.BlockSpec((bq, LANE), lambda h, j: (j, h)), pl.BlockSpec((T, LANE), lambda h, j: (0, 0))]
        out_shape += [jax.ShapeDtypeStruct((T, Hkv * LANE), dq_dtype), jax.ShapeDtypeStruct((T, LANE), F32)]
    if want_dbias:
        out_specs.append(pl.BlockSpec((1, cfg.n_var, BQ, W), lambda h, j: (h, 0, 0, 0)))
        out_shape.append(jax.ShapeDtypeStruct((Hkv, cfg.n_var, BQ, W), F32))
    if cfg.has_sink:
        out_specs.append(pl.BlockSpec((1, G * BQ, LANE), lambda h, j: (h, 0, 0)))
        out_shape.append(jax.ShapeDtypeStruct((Hkv, G * BQ, LANE), F32))

    n_ride = sum(len(r.arrays) for r in rides)
    operands += [a for r in rides for a in r.arrays]
    in_specs += [_ANY] * n_ride
    out_specs += [_ANY] * n_ride
    out_shape += [s for r in rides for s in r.out_shapes()]
    n_q = T // bq

    def body(*refs):
        (q_ref, k_ref, v_ref, q2_ref, k2_ref, bias_ref, sink_ref), rest = _attn_unpack(cfg, refs)
        do_ref = rest[0]
        o_ref, lse_ref = (rest[1], rest[2]) if by_chunks else (None, None)
        rest = rest[1 + n_x:]
        ride_in = rest[:n_ride]
        dq_ref, dk_ref, dv_ref = rest[n_ride:3 + n_ride]
        rest = rest[3 + n_ride:]
        dq2_ref = dk2_ref = dbias_ref = dsink_ref = None
        if cfg.two:
            dq2_ref, dk2_ref = rest[:2]
            rest = rest[2:]
        if want_dbias:
            dbias_ref = rest[0]
            rest = rest[1:]
        if cfg.has_sink:
            dsink_ref = rest[0]
            rest = rest[1:]
        ride_groups = _ride_split(rides, ride_in, rest[:n_ride], rest[n_ride:])
        h = pl.program_id(0)
        j = pl.program_id(1)
        if rides:
            pl.when((h == 0) & (j == 0))(lambda: _ride_start(rides, ride_groups))

        @pl.when(j == 0)
        def _():
            dk_ref[...] = jnp.zeros_like(dk_ref)
            dv_ref[...] = jnp.zeros_like(dv_ref)
            if want_dbias:
                dbias_ref[...] = jnp.zeros_like(dbias_ref)
            if cfg.has_sink:
                dsink_ref[...] = jnp.zeros_like(dsink_ref)

        if cfg.two:
            @pl.when((j == 0) & (h == 0))
            def _():
                dk2_ref[...] = jnp.zeros_like(dk2_ref)

        def stacked(tile):
            return jnp.concatenate([tile[:, g * LANE:(g + 1) * LANE] for g in range(G)], axis=0) if G > 1 else tile

        def block_by_key_chunks():
            qs, dos = stacked(q_ref[...]), stacked(do_ref[...]).astype(BF16)
            q2s = q2_ref[...] if cfg.two else None
            delta = jnp.sum(dos.astype(F32) * stacked(o_ref[...]).astype(F32), axis=-1, keepdims=True)
            lse_col = stacked(lse_ref[...])[:, 0:1]
            c2 = cfg.scale * math.log2(math.e)
            lse_window = lse_col + jnp.where(j * bq >= S, -NEG, 0.0) if cfg.ctx_queries else lse_col
            dq = jnp.zeros((G * bq, LANE), F32)
            dq2 = jnp.zeros((bq, LANE), F32)
            step = min(key_chunk, S)
            for r0, n, lse_rows in [(r, step, lse_window) for r in range(0, S, step)] + [(S, C, lse_col)]:
                k_rows, v_rows = k_ref[r0:r0 + n, :], v_ref[r0:r0 + n, :]
                s = _dot_nt(qs, k_rows)
                if cfg.two:
                    k2_rows = k2_ref[r0:r0 + n, :]
                    s = s + _dot_nt(q2s, k2_rows)
                p = jnp.exp2(s * c2 - lse_rows)
                ds = (p * ((_dot_nt(dos, v_rows) - delta) * cfg.scale)).astype(BF16)
                dq = dq + _dot(ds, k_rows)
                dk_ref[r0:r0 + n, :] += _dot_tn(ds, qs)
                dv_ref[r0:r0 + n, :] += _dot_tn(p.astype(BF16), dos)
                if cfg.two:
                    dq2 = dq2 + _dot(ds, k2_rows)
                    dk2_ref[r0:r0 + n, :] += _dot_tn(ds, q2s)
            for g in range(G):
                dq_ref[:, g * LANE:(g + 1) * LANE] = dq[g * bq:(g + 1) * bq].astype(dq_ref.dtype)
            if cfg.two:
                dq2_ref[...] = dq2.astype(dq2_ref.dtype)

        def block():
            if by_chunks:
                return block_by_key_chunks()
            p_w, p_c, p_s, (off, var, qs, kw, kc, q2s, k2w, k2c) = _attn_probs(
                cfg, j, q_ref, k_ref, q2_ref, k2_ref, bias_ref, sink_ref)
            dot_ = do_ref[...]
            dos = jnp.concatenate([dot_[:, g * LANE:(g + 1) * LANE] for g in range(G)], axis=0) if G > 1 else dot_
            dos = dos.astype(BF16)
            vw = v_ref[pl.ds(off, W), :]
            vc = v_ref[pl.ds(S, C), :]
            dp_w = _dot_nt(dos, vw)
            dp_c = _dot_nt(dos, vc)
            delta = jnp.sum(p_w * dp_w, axis=-1, keepdims=True) + jnp.sum(p_c * dp_c, axis=-1, keepdims=True)
            ds_w = p_w * (dp_w - delta)
            ds_c = p_c * (dp_c - delta)
            if want_dbias:
                dbias_ref[0, pl.ds(var, 1)] += ds_w[None]
            if cfg.has_sink:
                dsink_ref[0] += jnp.broadcast_to(-(p_s * delta), (G * bq, LANE))
            dsw = (ds_w * cfg.scale).astype(BF16)
            dsc = (ds_c * cfg.scale).astype(BF16)
            dq = _dot(dsw, kw) + _dot(dsc, kc)
            for g in range(G):
                dq_ref[:, g * LANE:(g + 1) * LANE] = dq[g * bq:(g + 1) * bq].astype(dq_ref.dtype)
            dk_ref[pl.ds(off, W), :] += _dot_tn(dsw, qs)
            dk_ref[pl.ds(S, C), :] += _dot_tn(dsc, qs)
            dv_ref[pl.ds(off, W), :] += _dot_tn(p_w.astype(BF16), dos)
            dv_ref[pl.ds(S, C), :] += _dot_tn(p_c.astype(BF16), dos)
            if cfg.two:
                dq2_ref[...] = (_dot(dsw, k2w) + _dot(dsc, k2c)).astype(dq2_ref.dtype)
                dk2_ref[pl.ds(off, W), :] += _dot_tn(dsw, q2s)
                dk2_ref[pl.ds(S, C), :] += _dot_tn(dsc, q2s)

        if cfg.ctx_queries:
            block()
        else:
            pl.when(j * bq < S)(block)

            @pl.when(j * bq >= S)
            def _():
                dq_ref[...] = jnp.zeros_like(dq_ref)
                if cfg.two:
                    dq2_ref[...] = jnp.zeros_like(dq2_ref)

        if rides:
            pl.when((h == Hkv - 1) & (j == n_q - 1))(lambda: _ride_wait(rides, ride_groups))

    outs = pl.pallas_call(
        body, name=name, grid=(Hkv, n_q),
        in_specs=in_specs, out_specs=out_specs, out_shape=out_shape,
        scratch_shapes=_ride_scratch(rides),
        compiler_params=_cparams(("arbitrary", "arbitrary")),
    )(*operands)
    if not rides:
        return outs
    return list(outs[:len(outs) - n_ride]) + [_ride_outputs(rides, outs[len(outs) - n_ride:])]


def _na_bias(rpb, S):
    H = rpb.shape[0]
    rows = S // GRID_W
    pad_l = GRID_W - 1 - (NA_WIN_C - 1)
    ext = jnp.concatenate([jnp.broadcast_to(rpb[:, :, :1], (H, 2 * NA_WIN_R - 1, pad_l)), rpb,
                           jnp.broadcast_to(rpb[:, :, -1:], (H, 2 * NA_WIN_R - 1, pad_l))], axis=2)
    by_col = jnp.stack([ext[:, :, GRID_W - 1 - qc:2 * GRID_W - 1 - qc] for qc in range(GRID_W)], axis=2)
    cq = np.arange(GRID_W)
    c0 = np.clip(cq - NA_WIN_C // 2, 0, GRID_W - NA_WIN_C)
    col_in = (cq[None, :] >= c0[:, None]) & (cq[None, :] < c0[:, None] + NA_WIN_C)
    n_lat = S // BQ
    neg_tile = jnp.full((H, GRID_W, GRID_W), NEG, F32)
    variants = []
    for v in range(5):
        j = {0: 0, 1: 1, 2: 2, 3: n_lat - 2, 4: n_lat - 1}[v]
        start = int(np.clip(j - 2, 0, n_lat - 5))
        assert j - start == v
        q_rows = []
        for qr in range(2):
            r = 2 * j + qr
            r0 = int(np.clip(r - NA_WIN_R // 2, 0, rows - NA_WIN_R))
            k_tiles = []
            for kr in range(10):
                krow = 2 * start + kr
                if r0 <= krow < r0 + NA_WIN_R:
                    k_tiles.append(jnp.where(col_in[None], by_col[:, krow - r + NA_WIN_R - 1], NEG))
                else:
                    k_tiles.append(neg_tile)
            q_rows.append(jnp.concatenate(k_tiles, axis=2))
        variants.append(jnp.concatenate(q_rows, axis=1))
    return jnp.stack(variants, axis=1)


def _swa_mask(S):
    qq = np.arange(BQ)[:, None]
    kk = np.arange(3 * BQ)[None, :]
    tiles = [np.where(np.abs(kk - v * BQ - qq) <= SWA_WINDOW, 0.0, NEG) for v in range(3)]
    return jnp.asarray(np.stack(tiles)[None], F32)


def _ffn_tiles(T, S, F):
    return _row_block(T, S), _pick(F, 1408)


def _halo_rows(dtype):
    return 8 * 4 // jnp.dtype(dtype).itemsize


def _halo_specs(T, bt, bf, dtype):
    hr = _halo_rows(dtype)
    nh = bt // hr
    return [pl.BlockSpec((bt, bf), lambda f, i: (i, f)),
            pl.BlockSpec((hr, bf), lambda f, i: (jnp.maximum(i * nh - 1, 0), f)),
            pl.BlockSpec((hr, bf), lambda f, i: (jnp.minimum((i + 1) * nh, T // hr - 1), f))]


def _neighbours(x, prev, nxt, i, bt, S, T):
    r = lax.broadcasted_iota(jnp.int32, x.shape, 0)
    g0 = i * bt
    first_open = jnp.logical_or(g0 == 0, g0 == S)
    last_open = jnp.logical_or(g0 + bt == S, g0 + bt == T)
    hr = prev.shape[0]
    before = jnp.where(r == 0, jnp.where(first_open, 0.0, prev[hr - 1:hr, :].astype(F32)), pltpu.roll(x, 1, 0))
    after = jnp.where(r == bt - 1, jnp.where(last_open, 0.0, nxt[0:1, :].astype(F32)), pltpu.roll(x, bt - 1, 0))
    return before, after


def _sigmoid(a):
    return 1.0 / (1.0 + jnp.exp(-a))


def _ffn_fwd(gp, u, cw, cb, S, *, name):
    T, F = gp.shape
    bt, bf = _ffn_tiles(T, S, F)

    def body(g_ref, gp_ref, gn_ref, u_ref, w_ref, b_ref, o_ref):
        i = pl.program_id(1)
        g = g_ref[...].astype(F32)
        before, after = _neighbours(g, gp_ref[...], gn_ref[...], i, bt, S, T)
        a = before * w_ref[0:1, :] + g * w_ref[1:2, :] + after * w_ref[2:3, :] + b_ref[...]
        o_ref[...] = (a * _sigmoid(a) * u_ref[...].astype(F32)).astype(o_ref.dtype)

    return pl.pallas_call(
        body, name=name, grid=(F // bf, T // bt),
        in_specs=_halo_specs(T, bt, bf, gp.dtype) + [pl.BlockSpec((bt, bf), lambda f, i: (i, f)),
                                              pl.BlockSpec((3, bf), lambda f, i: (0, f)),
                                              pl.BlockSpec((1, bf), lambda f, i: (0, f))],
        out_specs=pl.BlockSpec((bt, bf), lambda f, i: (i, f)),
        out_shape=jax.ShapeDtypeStruct((T, F), BF16),
        compiler_params=_cparams(("parallel", "parallel")),
    )(gp, gp, gp, u, cw, cb)


def _ffn_bwd_act(gp, u, da_out, cw, cb, S, *, name):
    T, F = gp.shape
    bt, bf = _ffn_tiles(T, S, F)

    def body(g_ref, gp_ref, gn_ref, u_ref, d_ref, w_ref, b_ref, da_ref, du_ref, acc_ref):
        i = pl.program_id(1)

        @pl.when(i == 0)
        def _():
            acc_ref[...] = jnp.zeros_like(acc_ref)

        g = g_ref[...].astype(F32)
        before, after = _neighbours(g, gp_ref[...], gn_ref[...], i, bt, S, T)
        a = before * w_ref[0:1, :] + g * w_ref[1:2, :] + after * w_ref[2:3, :] + b_ref[...]
        sig = _sigmoid(a)
        d = d_ref[...].astype(F32)
        du_ref[...] = (d * (a * sig)).astype(du_ref.dtype)
        da = d * u_ref[...].astype(F32) * (sig * (1.0 + a * (1.0 - sig)))
        da_ref[...] = da
        acc_ref[0:1, :] += jnp.sum(da * before, axis=0, keepdims=True)
        acc_ref[1:2, :] += jnp.sum(da * g, axis=0, keepdims=True)
        acc_ref[2:3, :] += jnp.sum(da * after, axis=0, keepdims=True)
        acc_ref[3:4, :] += jnp.sum(da, axis=0, keepdims=True)

    blk = pl.BlockSpec((bt, bf), lambda f, i: (i, f))
    return pl.pallas_call(
        body, name=name, grid=(F // bf, T // bt),
        in_specs=_halo_specs(T, bt, bf, gp.dtype) + [blk, blk,
                                              pl.BlockSpec((3, bf), lambda f, i: (0, f)),
                                              pl.BlockSpec((1, bf), lambda f, i: (0, f))],
        out_specs=[blk, blk, pl.BlockSpec((8, bf), lambda f, i: (0, f))],
        out_shape=[jax.ShapeDtypeStruct((T, F), F32), jax.ShapeDtypeStruct((T, F), BF16),
                   jax.ShapeDtypeStruct((8, F), F32)],
        compiler_params=_cparams(("parallel", "arbitrary")),
    )(gp, gp, gp, u, da_out, cw, cb)


def _ffn_bwd_conv(da, cw, S, *, name):
    T, F = da.shape
    bt, bf = _ffn_tiles(T, S, F)

    def body(d_ref, dp_ref, dn_ref, w_ref, o_ref):
        i = pl.program_id(1)
        d = d_ref[...]
        before, after = _neighbours(d, dp_ref[...], dn_ref[...], i, bt, S, T)
        o_ref[...] = (after * w_ref[0:1, :] + d * w_ref[1:2, :] + before * w_ref[2:3, :]).astype(o_ref.dtype)

    return pl.pallas_call(
        body, name=name, grid=(F // bf, T // bt),
        in_specs=_halo_specs(T, bt, bf, da.dtype) + [pl.BlockSpec((3, bf), lambda f, i: (0, f))],
        out_specs=pl.BlockSpec((bt, bf), lambda f, i: (i, f)),
        out_shape=jax.ShapeDtypeStruct((T, F), BF16),
        compiler_params=_cparams(("parallel", "parallel")),
    )(da, da, da, cw)


def _ew_rows(R, N, n_arrays):
    return _pick(R, max(16, EW_VMEM_BUDGET // (8 * n_arrays * N)), 16)


def _adam(w, g, m, v, *, rides=(), emit_grad=False, name):
    lead = w.shape[:-2]
    R, N = w.shape[-2:]
    br = _ew_rows(R, N, 7)
    bc1 = 1.0 - ADAM_B1 ** ADAM_STEP
    bc2 = 1.0 - ADAM_B2 ** ADAM_STEP
    grid = lead + (R // br,)
    n_ride = sum(len(r.arrays) for r in rides)

    n_out = 4 if emit_grad else 3

    def body(*refs):
        w_ref, g_ref, m_ref, v_ref = refs[:4]
        d_ref, mo_ref, vo_ref = refs[4 + n_ride:7 + n_ride]
        if emit_grad:
            refs[7 + n_ride][...] = g_ref[...]
        if rides:
            ride_groups = _ride_split(rides, refs[4:4 + n_ride], refs[4 + n_out + n_ride:4 + n_out + 2 * n_ride],
                                      refs[4 + n_out + 2 * n_ride:])
            steps = [pl.program_id(d) for d in range(len(grid))]
            pl.when(functools.reduce(jnp.logical_and, [s == 0 for s in steps]))(lambda: _ride_start(rides, ride_groups))
        gv = g_ref[...]
        mn = ADAM_B1 * m_ref[...] + (1.0 - ADAM_B1) * gv
        vn = ADAM_B2 * v_ref[...] + (1.0 - ADAM_B2) * (gv * gv)
        mo_ref[...] = mn
        vo_ref[...] = vn
        d_ref[...] = -ADAM_LR * ((mn / bc1) / (jnp.sqrt(vn / bc2) + ADAM_EPS) + ADAM_WD * w_ref[...])
        if rides:
            pl.when(functools.reduce(jnp.logical_and, [s == n - 1 for s, n in zip(steps, grid)]))(
                lambda: _ride_wait(rides, ride_groups))

    if lead:
        blk = pl.BlockSpec((None, br, N), lambda l, i: (l, i, 0))
    else:
        blk = pl.BlockSpec((br, N), lambda i: (i, 0))
    shp = jax.ShapeDtypeStruct(w.shape, F32)
    outs = pl.pallas_call(
        body, name=name, grid=grid,
        in_specs=[blk, blk, blk, blk] + [_ANY] * n_ride, out_specs=[blk] * n_out + [_ANY] * n_ride,
        out_shape=[shp] * n_out + [s for r in rides for s in r.out_shapes()],
        scratch_shapes=_ride_scratch(rides),
        input_output_aliases=_ride_aliases(rides, 4, n_out),
        compiler_params=_cparams(("arbitrary" if rides else "parallel",) * len(grid)),
    )(w, g, m, v, *[a for r in rides for a in r.arrays])
    if not rides:
        return outs
    return outs[:n_out], _ride_outputs(rides, outs[n_out:])


def _sum_lead(x, out_dtype, *, name):
    n, R, N = x.shape
    br = _ew_rows(R, N, n + 1)

    def body(x_ref, o_ref):
        acc = x_ref[0].astype(F32)
        for k in range(1, n):
            acc = acc + x_ref[k].astype(F32)
        o_ref[...] = acc.astype(o_ref.dtype)

    return pl.pallas_call(
        body, name=name, grid=(R // br,),
        in_specs=[pl.BlockSpec((n, br, N), lambda i: (0, i, 0))],
        out_specs=pl.BlockSpec((br, N), lambda i: (i, 0)),
        out_shape=jax.ShapeDtypeStruct((R, N), out_dtype),
        compiler_params=_cparams(("parallel",)),
    )(x)


def _sum_parts(parts, landed, chip, core, stack, *, name):
    _, R, N = parts.shape
    n_layers, layer, buf = stack
    br = _ew_rows(R, N, 5)

    def body(pos_ref, own_ref, landed_ref, *rest):
        o_ref = rest[-1]
        acc = own_ref[...].astype(F32)
        for k in range(3):
            acc = acc + landed_ref[k].astype(F32)
        o_ref[...] = acc

    operands = [jnp.stack([chip, core]).astype(jnp.int32), parts, landed]
    in_specs = [pl.BlockSpec((None, br, N), lambda i, pos: (pos[0], i, 0)),
                pl.BlockSpec((3, br, N), lambda i, pos: (0, i, 0))]
    aliases = {}
    if buf is not None:
        aliases = {3: 0}
        operands.append(buf)
        in_specs.append(pl.BlockSpec(memory_space=pl.ANY))
    return pl.pallas_call(
        body, name=name,
        grid_spec=pltpu.PrefetchScalarGridSpec(
            num_scalar_prefetch=1, grid=(R // br,), in_specs=in_specs,
            out_specs=pl.BlockSpec((None, None, br, N), lambda i, pos: (layer, pos[1], i, 0))),
        out_shape=jax.ShapeDtypeStruct((n_layers, 2, R, N), F32),
        input_output_aliases=aliases,
        compiler_params=_cparams(("parallel",)),
    )(*operands)


def _place_own(shards, layer, core, slot, *, name):
    _, R, N = shards.shape
    br = _ew_rows(R // 2, N, 2)
    nb = R // 2 // br

    def body(pos_ref, x_ref, o_ref):
        o_ref[...] = x_ref[...].astype(o_ref.dtype)

    return pl.pallas_call(
        body, name=name,
        grid_spec=pltpu.PrefetchScalarGridSpec(
            num_scalar_prefetch=1, grid=(nb,),
            in_specs=[pl.BlockSpec((None, br, N), lambda i, pos: (layer, pos[0] * nb + i, 0))],
            out_specs=pl.BlockSpec((None, br, N), lambda i, pos: (pos[1], i, 0))),
        out_shape=jax.ShapeDtypeStruct((8, R // 2, N), BF16),
        compiler_params=_cparams(("parallel",)),
    )(jnp.stack([core, slot]).astype(jnp.int32), shards)


def _add_half(g, r, core, *, name):
    Q, _, R, N = g.shape
    br = _ew_rows(R, N, 3)

    def body(c_ref, g_ref, r_ref, o_ref):
        o_ref[...] = (g_ref[...] + r_ref[...]).astype(o_ref.dtype)

    return pl.pallas_call(
        body, name=name,
        grid_spec=pltpu.PrefetchScalarGridSpec(
            num_scalar_prefetch=1, grid=(Q, R // br),
            in_specs=[pl.BlockSpec((None, None, br, N), lambda q, i, c_ref: (q, c_ref[0], i, 0)),
                      pl.BlockSpec((None, br, N), lambda q, i, c_ref: (q, i, 0))],
            out_specs=pl.BlockSpec((None, br, N), lambda q, i, c_ref: (q, i, 0))),
        out_shape=jax.ShapeDtypeStruct((Q, R, N), BF16),
        compiler_params=_cparams(("parallel", "parallel")),
    )(core.reshape(1).astype(jnp.int32), g, r)


_ANY = pl.BlockSpec(memory_space=pl.ANY)


def _place():
    return lax.axis_index("x"), lax.axis_index("y"), lax.axis_index("c")


def _allgather8(blocks, *, name):
    n = len(blocks)

    def body(*refs):
        xs, outs = refs[:n], refs[n:2 * n]
        send_sems, recv_sems, local_sems = refs[2 * n:]
        x, y, c = _place()
        me, sibling = (x, y, c), (x, y, 1 - c)
        chips = [(1 - x, y), (x, 1 - y), (1 - x, 1 - y)]

        def slot(a, px, py, pc):
            return outs[a].at[4 * px + 2 * py + pc]

        def copy(a, k, block, to, src=None):
            return pltpu.make_async_remote_copy(
                src_ref=slot(a, *block) if src is None else src, dst_ref=slot(a, *block),
                send_sem=send_sems.at[a, k], recv_sem=recv_sems.at[a, k], device_id=to, device_id_type=MESH)

        mine = [pltpu.make_async_copy(xs[a], slot(a, *me), local_sems.at[a]) for a in range(n)]
        for cp in mine:
            cp.start()
        first = []
        for a in range(n):
            first.append(copy(a, 0, me, sibling, src=xs[a]))
            first += [copy(a, 1 + j, me, (*chip, c), src=xs[a]) for j, chip in enumerate(chips)]
        for cp in first:
            cp.start()
        passed = []
        for j, chip in enumerate(chips):
            for a in range(n):
                copy(a, 1 + j, (*chip, c), me).wait_recv()
                fwd = copy(a, 4 + j, (*chip, c), sibling)
                fwd.start()
                passed.append(fwd)
        for a in range(n):
            copy(a, 0, sibling, me).wait_recv()
            for j, chip in enumerate(chips):
                copy(a, 4 + j, (*chip, 1 - c), me).wait_recv()
        for cp in first + passed:
            cp.wait_send()
        for cp in mine:
            cp.wait()

    return pl.pallas_call(
        body, name=name,
        in_specs=[_ANY] * n, out_specs=[_ANY] * n,
        out_shape=[jax.ShapeDtypeStruct((8,) + b.shape, b.dtype) for b in blocks],
        scratch_shapes=[pltpu.SemaphoreType.DMA((n, 7)), pltpu.SemaphoreType.DMA((n, 7)), pltpu.SemaphoreType.DMA((n,))],
    )(*blocks)


class _Exchange:
    n_sems = 1

    def __init__(self, arrays):
        self.arrays = list(arrays)

    def out_shapes(self):
        return [jax.ShapeDtypeStruct(g.shape[:1] + g.shape[2:], g.dtype) for g in self.arrays]

    def copy(self, k, src, dst, sems, landing):
        x, y, c = _place()
        return pltpu.make_async_remote_copy(src_ref=src.at[:, 1 - c], dst_ref=dst, send_sem=sems[0], recv_sem=sems[1],
                                            device_id=(x, y, 1 - c), device_id_type=MESH)

    def copies(self, group, landing):
        xs, outs, send_sems, recv_sems = group
        return [self.copy(k, xs[a], outs[a], (send_sems.at[a, k], recv_sems.at[a, k]), landing)
                for a in range(len(xs)) for k in range(self.n_sems)]


class _Scatter(_Exchange):
    n_sems = 3

    def out_shapes(self):
        return [jax.ShapeDtypeStruct((3,) + p.shape[1:], p.dtype) for p in self.arrays]

    def copy(self, k, src, dst, sems, landing):
        x, y, c = _place()
        px, py = [(1 - x, y), (x, 1 - y), (1 - x, 1 - y)][k]
        return pltpu.make_async_remote_copy(src_ref=src.at[2 * px + py], dst_ref=dst.at[k], send_sem=sems[0], recv_sem=sems[1],
                                            device_id=(px, py, c), device_id_type=MESH)


class _GatherChips(_Exchange):
    n_sems = 3
    in_place = True

    def out_shapes(self):
        return [jax.ShapeDtypeStruct(b.shape, b.dtype) for b in self.arrays]

    def copy(self, k, src, dst, sems, landing):
        x, y, c = _place()
        px, py = [(1 - x, y), (x, 1 - y), (1 - x, 1 - y)][k]
        slot = 4 * px + 2 * py + c if landing else 4 * x + 2 * y + c
        return pltpu.make_async_remote_copy(src_ref=src.at[4 * x + 2 * y + c], dst_ref=dst.at[slot], send_sem=sems[0],
                                            recv_sem=sems[1], device_id=(px, py, c), device_id_type=MESH)


class _GatherCores(_GatherChips):
    n_sems = 4

    def copy(self, k, src, dst, sems, landing):
        x, y, c = _place()
        slot = 2 * k + 1 - c if landing else 2 * k + c
        return pltpu.make_async_remote_copy(src_ref=src.at[2 * k + c], dst_ref=dst.at[slot], send_sem=sems[0],
                                            recv_sem=sems[1], device_id=(x, y, 1 - c), device_id_type=MESH)


class _Join(_GatherChips):
    n_sems = 1

    def copy(self, k, src, dst, sems, landing):
        x, y, c = _place()
        return pltpu.make_async_remote_copy(src_ref=src.at[:, c], dst_ref=dst.at[:, 1 - c if landing else c], send_sem=sems[0],
                                            recv_sem=sems[1], device_id=(x, y, 1 - c), device_id_type=MESH)


def _ride_aliases(rides, first_in, first_out):
    aliases, i = {}, 0
    for r in rides:
        for a in range(len(r.arrays)):
            if getattr(r, "in_place", False):
                aliases[first_in + i + a] = first_out + i + a
        i += len(r.arrays)
    return aliases


def _ride_scratch(rides):
    shapes = []
    for r in rides:
        shapes += [pltpu.SemaphoreType.DMA((len(r.arrays), r.n_sems)), pltpu.SemaphoreType.DMA((len(r.arrays), r.n_sems))]
    return shapes


def _ride_split(rides, in_refs, out_refs, sem_refs):
    groups, i, o = [], 0, 0
    for k, r in enumerate(rides):
        n = len(r.arrays)
        groups.append((in_refs[i:i + n], out_refs[o:o + n], sem_refs[2 * k], sem_refs[2 * k + 1]))
        i, o = i + n, o + n
    return groups


def _ride_start(rides, groups):
    for r, g in zip(rides, groups):
        for cp in r.copies(g, False):
            cp.start()


def _ride_wait(rides, groups):
    for r, g in zip(rides, groups):
        for cp in r.copies(g, True):
            cp.wait_recv()
        for cp in r.copies(g, False):
            cp.wait_send()


def _run_rides(rides, *, name):
    n_in = sum(len(r.arrays) for r in rides)

    def body(*refs):
        groups = _ride_split(rides, refs[:n_in], refs[n_in:2 * n_in], refs[2 * n_in:])
        _ride_start(rides, groups)
        _ride_wait(rides, groups)

    outs = pl.pallas_call(
        body, name=name,
        in_specs=[_ANY] * n_in, out_specs=[_ANY] * n_in,
        out_shape=[s for r in rides for s in r.out_shapes()],
        scratch_shapes=_ride_scratch(rides),
        input_output_aliases=_ride_aliases(rides, 0, 0),
    )(*[a for r in rides for a in r.arrays])
    return _ride_outputs(rides, outs)


def _ride_outputs(rides, outs):
    res, o = [], 0
    for r in rides:
        res.append(list(outs[o:o + len(r.arrays)]))
        o += len(r.arrays)
    return res


def _perm_w_in(wt):
    pad = jnp.zeros((PCOLS - IN_COLS, wt.shape[1]), wt.dtype)
    return jnp.concatenate([wt[:3072], wt[3136:IN_COLS], wt[3072:3136], pad], axis=0)


def _unperm_w_in(gt):
    return jnp.concatenate([gt[:3072], gt[4096:IN_COLS], gt[3072:4096]], axis=0)


def _perm_w_uq(w):
    w4 = w.reshape(MLA_Q_LORA, MLA_HEADS, MLA_NOPE + MLA_ROPE)
    nope = w4[:, :, :MLA_NOPE].reshape(MLA_Q_LORA, MLA_HEADS * LANE)
    pe = jnp.pad(w4[:, :, MLA_NOPE:], ((0, 0), (0, 0), (0, LANE - MLA_ROPE))).reshape(MLA_Q_LORA, MLA_HEADS * LANE)
    return jnp.concatenate([nope, pe], axis=1)


def _unperm_w_uq(g):
    nope = g[:, :MLA_HEADS * LANE].reshape(MLA_Q_LORA, MLA_HEADS, LANE)
    pe = g[:, MLA_HEADS * LANE:].reshape(MLA_Q_LORA, MLA_HEADS, LANE)[:, :, :MLA_ROPE]
    return jnp.concatenate([nope, pe], axis=2).reshape(MLA_Q_LORA, MLA_HEADS * (MLA_NOPE + MLA_ROPE))


def _perm_w_ukv(w):
    w4 = w.reshape(MLA_KV_LORA, MLA_HEADS, MLA_NOPE + MLA_V)
    return jnp.concatenate([w4[:, :, :MLA_NOPE].reshape(MLA_KV_LORA, -1), w4[:, :, MLA_NOPE:].reshape(MLA_KV_LORA, -1)], axis=1)


def _unperm_w_ukv(g):
    kn = g[:, :MLA_HEADS * LANE].reshape(MLA_KV_LORA, MLA_HEADS, LANE)
    vv = g[:, MLA_HEADS * LANE:].reshape(MLA_KV_LORA, MLA_HEADS, LANE)
    return jnp.concatenate([kn, vv], axis=2).reshape(MLA_KV_LORA, -1)


def _silu(v):
    return v * jax.nn.sigmoid(v)


def _silu_grad(v):
    s = jax.nn.sigmoid(v)
    return s * (1.0 + v * (1.0 - s))


_WEIGHTS = ("c_ctx", "w_ada", "b_ada", "w_in", "na_rpb", "swa_sink", "mla_q_norm", "mla_kv_norm", "mla_w_uq", "mla_w_ukv",
            "gqa_q_norm", "gqa_k_norm", "w_out", "ln1_g", "ln1_b", "ffn_w_gate", "ffn_w_up", "ffn_conv_w", "ffn_conv_b",
            "ffn_w_down", "ln2_g", "ln2_b")
_COL_SHARDED = ("mla_w_uq", "mla_w_ukv", "ffn_w_gate", "ffn_w_up")
_ROW_SHARDED = ("w_out", "ffn_w_down")
_BIG = ("w_in",) + _COL_SHARDED + _ROW_SHARDED
_SMALL = ("c_ctx", "b_ada", "na_rpb", "swa_sink", "mla_q_norm", "mla_kv_norm", "gqa_q_norm", "gqa_k_norm", "ln1_g", "ln1_b",
          "ffn_conv_w", "ffn_conv_b", "ln2_g", "ln2_b")


def _piece_rows(a):
    return -(-a.size // (8 * LANE)) * 8


def _pack(arrays):
    return jnp.concatenate([jnp.pad(a.reshape(-1), (0, _piece_rows(a) * LANE - a.size)).reshape(_piece_rows(a), LANE)
                            for a in arrays], axis=0)


def _unpack(packed, like):
    out, r = [], 0
    for a in like:
        n = _piece_rows(a)
        out.append(packed[r:r + n].reshape(-1)[:a.size].reshape(a.shape))
        r += n
    return out


def _train_step(x, c, ctx, loss_target, w, m_in, v_in):
    L = DEPTH
    S, D = x.shape[1], x.shape[2]
    C = ctx.shape[1]
    T = S + C
    F = w["ffn_conv_b"].shape[1]
    ax, ay, ac = _place()
    chip = 2 * ax + ay
    dev = 2 * chip + ac
    n_ada = w["w_ada"].shape[2]
    w, m_in, v_in = dict(w), dict(m_in), dict(v_in)
    for d in (w, m_in, v_in):
        d["w_in"] = jnp.swapaxes(d["w_in"], 1, 2)

    gather_groups = {"A": ("w_in", "mla_w_uq", "mla_w_ukv"), "B": ("w_out",), "Cg": ("ffn_w_gate",), "Cu": ("ffn_w_up",),
                     "D": ("ffn_w_down",)}
    full = {n: [None] * L for n in _BIG}
    w_in_p, w_uq_p, w_ukv_p = [None] * L, [None] * L, [None] * L
    half_done = {}

    def chips_step(group, l):
        return _GatherChips([_place_own(w[n], l, ac, dev, name="gather_place") for n in gather_groups[group]])

    def cores_step(group, l):
        return _GatherCores(half_done.pop((group, l)))

    def finish_group(group, l, bufs):
        for n, b in zip(gather_groups[group], bufs):
            r, cols = b.shape[1:]
            if n in _COL_SHARDED:
                full[n][l] = b.reshape(4, 2, r, cols).transpose(1, 2, 0, 3).reshape(2 * r, 4 * cols)
            else:
                full[n][l] = b.reshape(8 * r, cols)
        if group == "A":
            w_in_p[l], w_uq_p[l] = _perm_w_in(full["w_in"][l]), _perm_w_uq(full["mla_w_uq"][l])
            w_ukv_p[l] = _perm_w_ukv(full["mla_w_ukv"][l])

    def with_rides(result, rides):
        return result if rides else (result, [])

    def my_half(a):
        r = a.shape[0] // 2
        return lax.dynamic_slice_in_dim(a, ac * r, r, axis=0).astype(BF16)

    gathered = _allgather8([my_half(w[n][0]) for n in gather_groups["A"]] + [w["ffn_conv_w"]], name="gather_weights")
    finish_group("A", 0, gathered[:-1])
    conv_w = gathered[-1][::2].transpose(1, 2, 0, 3).reshape(L, 3, F)

    (c_all,) = _allgather8([c], name="gather_c")
    c16 = jnp.concatenate([c_all.reshape(8, D), jnp.broadcast_to(w["c_ctx"][None], (8, D))], axis=0)
    row_keep = (jnp.arange(16) <= 8).astype(F32)[:, None]
    sc = _silu(c16) * row_keep
    b_loc = lax.dynamic_slice_in_dim(w["b_ada"], chip * n_ada, n_ada, axis=1)
    mod_loc = jnp.stack([_mm(sc, w["w_ada"], b_layer=l, name="mod_mm") + b_loc[l][None] for l in range(L)])
    (mod_g,) = _allgather8([mod_loc], name="gather_mod")
    mod_all = mod_g[::2].transpose(1, 2, 0, 3).reshape(L, 16, 4 * n_ada)
    mod_x = lax.dynamic_index_in_dim(mod_all, dev, axis=1, keepdims=False)
    mod_c = mod_all[:, 8]
    mods = [jnp.stack([mod_x[l].reshape(6, D), mod_c[l].reshape(6, D)], axis=1) for l in range(L)]

    tabs = _rope_tables(S, C, HEAD_DIM) + _rope_tables(S, C, MLA_ROPE)
    swa_mask = _swa_mask(S)
    scale = HEAD_DIM ** -0.5
    def attn_cfgs(l):
        cq = l < L - 1
        return (_AttnCfg(Hkv=NA_HEADS, G=1, S=S, C=C, band=(2, 5), scale=scale, n_var=5, bias_per_head=True, ctx_queries=cq),
                _AttnCfg(Hkv=SWA_KV_HEADS, G=SWA_HEADS // SWA_KV_HEADS, S=S, C=C, band=(1, 3), scale=scale, n_var=3,
                         has_sink=True, ctx_queries=cq),
                _AttnCfg(Hkv=MLA_HEADS, G=1, S=S, C=C, band=None, scale=(MLA_NOPE + MLA_ROPE) ** -0.5, two=True,
                         bq=2 * BQ, ctx_queries=cq),
                _AttnCfg(Hkv=GQA_KV_HEADS, G=GQA_HEADS // GQA_KV_HEADS, S=S, C=C, band=None, scale=scale, ctx_queries=cq))

    def gqa_bwd_cfg(cfg):
        return _AttnCfg(Hkv=cfg.Hkv, G=cfg.G, S=S, C=C, band=None, scale=cfg.scale, bq=2 * BQ, ctx_queries=cfg.ctx_queries)

    row = lambda a: a[None, :]

    xt = jnp.concatenate([x[0], ctx[0]], axis=0)
    saved = []
    for l in range(L):
        md = mods[l]
        gq, gk, mq, mkv = row(w["gqa_q_norm"][l]), row(w["gqa_k_norm"][l]), row(w["mla_q_norm"][l]), row(w["mla_kv_norm"][l])
        h1 = _mod_fwd(xt, md[0], md[1], S, name="mod_fwd")
        p = _mm(h1, w_in_p[l], mode="nt", out_dtype=BF16, name="in_proj")
        qkv = _prep_fwd(p, tabs, gq, gk, mq, mkv, S, name="prep_fwd")
        qm = _mm(qkv, w_uq_p[l], a_off=CB_CQ * LANE, a_k=MLA_Q_LORA, tk=LANE, name="mla_uq")
        qmb = _pe_rope(qm, tabs, S, 1.0, BF16, name="mla_q_rope")
        kvm = _mm(qkv, w_ukv_p[l], a_off=CB_CKV * LANE, a_k=MLA_KV_LORA, tk=LANE, out_dtype=BF16, name="mla_ukv")
        bias_na = _na_bias(w["na_rpb"][l], S)
        sink = jnp.broadcast_to(jnp.repeat(w["swa_sink"][l].reshape(SWA_KV_HEADS, -1), BQ, axis=1)[:, :, None],
                                (SWA_KV_HEADS, SWA_HEADS // SWA_KV_HEADS * BQ, LANE))
        cfg_na, cfg_swa, cfg_mla, cfg_gqa = attn_cfgs(l)
        rows = T if l < L - 1 else S
        first, more = l == 0, l + 1 < L
        rides = [chips_step("B", l)] if first else [cores_step("B", l)]
        mix_blocks = NA_HEADS + SWA_HEADS + MLA_HEADS + GQA_HEADS
        mix, got = _attn_fwd(cfg_na, qkv, CB_NA_Q, qkv, CB_NA_K, qkv, CB_NA_V, bias=bias_na, rides=rides,
                             into=(None, 0, mix_blocks), name="na_fwd")
        if first:
            half_done[("B", l)] = got[0]
        else:
            finish_group("B", l, got[0])
        rides = [cores_step("B", l)] if first else []
        mix, got = with_rides(_attn_fwd(cfg_swa, qkv, CB_SWA_Q, qkv, CB_SWA_K, qkv, CB_SWA_V, bias=swa_mask, sink=sink,
                                        rides=rides, into=(mix, NA_HEADS, mix_blocks), name="swa_fwd"), rides)
        if first:
            finish_group("B", l, got[0])
        (mix, lse_mla), got = _attn_fwd(cfg_mla, qmb, 0, kvm, 0, kvm, MLA_HEADS, q2=qmb, q2_cb=MLA_HEADS, k2=qkv, k2_cb=CB_KPE,
                                        rides=[chips_step("Cg", l)], into=(mix, NA_HEADS + SWA_HEADS, mix_blocks),
                                        want_lse=True, name="mla_fwd")
        half_done[("Cg", l)] = got[0]
        (mix, lse_gqa), got = _attn_fwd(cfg_gqa, qkv, CB_GQA_Q, qkv, CB_GQA_K, qkv, CB_GQA_V, want_lse=True,
                             rides=[cores_step("Cg", l), chips_step("Cu", l)],
                             into=(mix, NA_HEADS + SWA_HEADS + MLA_HEADS, mix_blocks), name="gqa_fwd")
        finish_group("Cg", l, got[0])
        half_done[("Cu", l)] = got[1]
        z1, got = _mm(mix, full["w_out"][l], rows=rows, rides=[cores_step("Cu", l)], name="out_proj")
        finish_group("Cu", l, got[0])
        x1 = _res_fwd(xt, z1, md[2], row(w["ln1_g"][l]), row(w["ln1_b"][l]), S, name="res_fwd")
        h2 = _mod_fwd(x1, md[3], md[4], S, name="mod_fwd")
        gp, got = _mm(h2, full["ffn_w_gate"][l], rides=[chips_step("D", l)], out_dtype=BF16, name="ffn_in")
        half_done[("D", l)] = got[0]
        up, got = _mm(h2, full["ffn_w_up"][l], rides=[cores_step("D", l)] + ([chips_step("A", l + 1)] if more else []),
                      out_dtype=BF16, name="ffn_in")
        finish_group("D", l, got[0])
        if more:
            half_done[("A", l + 1)] = got[1]
        act = _ffn_fwd(gp, up, conv_w[l], row(w["ffn_conv_b"][l]), S, name="ffn_mid")
        rides = [cores_step("A", l + 1), chips_step("B", l + 1)] if more else []
        z2, got = with_rides(_mm(act, full["ffn_w_down"][l], rides=rides, name="ffn_out"), rides)
        if more:
            finish_group("A", l + 1, got[0])
            half_done[("B", l + 1)] = got[1]
        x2 = _res_fwd(x1, z2, md[5], row(w["ln2_g"][l]), row(w["ln2_b"][l]), S, name="res_fwd")
        saved.append(dict(x=xt, h1=h1, p=p, qkv=qkv, qmb=qmb, kvm=kvm, bias_na=bias_na, sink=sink, mix=mix, z1=z1, x1=x1,
                          h2=h2, gp=gp, up=up, act=act, z2=z2, cfgs=(cfg_na, cfg_swa, cfg_mla, cfg_gqa),
                          lse_mla=lse_mla, lse_gqa=lse_gqa))
        xt = x2

    dx, loss_part = _loss_fwd_bwd(xt, loss_target[0], S, name="loss")
    loss = lax.psum(loss_part[0, 0], ("x", "y", "c"))

    groups = {"ffn": ("ffn_w_gate", "ffn_w_up", "ffn_w_down", "w_out"), "rest": ("w_in", "mla_w_uq", "mla_w_ukv")}
    wgrad = [dict() for _ in range(L)]
    parts, landed = {}, {}

    def halves_of(group, l):
        return [wgrad[l][n].reshape(4, 2, wgrad[l][n].shape[1] // 2, wgrad[l][n].shape[2]) for n in groups[group]]

    def add_halves(group, l, received):
        parts[(group, l)] = [_add_half(h, r, ac, name="rs_core_add") for h, r in zip(halves_of(group, l), received)]

    small = {n: [None] * L for n in ("na_rpb", "swa_sink", "mla_q_norm", "mla_kv_norm", "gqa_q_norm", "gqa_k_norm",
                                     "ln1_g", "ln1_b", "ffn_conv_w", "ffn_conv_b", "ln2_g", "ln2_b")}
    dmod = [None] * L
    for l in reversed(range(L)):
        sv, md = saved[l], mods[l]
        gq, gk, mq, mkv = row(w["gqa_q_norm"][l]), row(w["gqa_k_norm"][l]), row(w["mla_q_norm"][l]), row(w["mla_kv_norm"][l])
        cb_row = row(w["ffn_conv_b"][l])
        dx1, dz2, acc_r2 = _res_bwd(sv["x1"], sv["z2"], md[5], row(w["ln2_g"][l]), dx, S, name="res_bwd")
        dact = _mm(dz2, full["ffn_w_down"][l], mode="nt", out_dtype=BF16, name="ffn_out_dx")
        wgrad[l]["ffn_w_down"] = _mm(sv["act"], dz2, mode="tn", name="ffn_out_dw").reshape(4, F // 4, D)
        da, du, acc_f = _ffn_bwd_act(sv["gp"], sv["up"], dact, conv_w[l], cb_row, S, name="ffn_mid_bwd")
        dg = _ffn_bwd_conv(da, conv_w[l], S, name="ffn_conv_bwd")
        dh2 = _mm(dg, full["ffn_w_gate"][l], mode="nt", name="ffn_in_dx")
        dh2 = _mm(du, full["ffn_w_up"][l], mode="nt", add=dh2, name="ffn_in_dx_add")
        wgrad[l]["ffn_w_gate"] = _mm(sv["h2"], dg, mode="tn", stack=(1, 0, None), split4=True,
                                     name="ffn_in_dw").reshape(4, D, F // 4)
        wgrad[l]["ffn_w_up"] = _mm(sv["h2"], du, mode="tn", stack=(1, 0, None), split4=True,
                                   name="ffn_in_dw").reshape(4, D, F // 4)
        dx1, acc_m2 = _mod_bwd(sv["x1"], dh2, md[4], dx1, S, name="mod_bwd")
        dxa, dz1, acc_r1 = _res_bwd(sv["x"], sv["z1"], md[2], row(w["ln1_g"][l]), dx1, S, name="res_bwd")
        dmix = _mm(dz1, full["w_out"][l], mode="nt", out_dtype=BF16, name="out_proj_dx")
        wgrad[l]["w_out"] = _mm(sv["mix"], dz1, mode="tn", rows=dz1.shape[0], name="out_proj_dw").reshape(4, -1, D)

        qkv, qmb, kvm = sv["qkv"], sv["qmb"], sv["kvm"]
        cfg_na, cfg_swa, cfg_mla, cfg_gqa = sv["cfgs"]
        rest_above = l + 1 < L
        rides = [_Exchange(halves_of("ffn", l))] + ([_Exchange(halves_of("rest", l + 1))] if rest_above else [])
        dq_a, dk_a, dv_a, dbias, received = _attn_bwd(cfg_na, qkv, CB_NA_Q, qkv, CB_NA_K, qkv, CB_NA_V, dmix, 0,
                                                      bias=sv["bias_na"], want_dbias=True, rides=rides, name="na_bwd")
        add_halves("ffn", l, received[0])
        if rest_above:
            add_halves("rest", l + 1, received[1])
        dq_b, dk_b, dv_b, dsink = _attn_bwd(cfg_swa, qkv, CB_SWA_Q, qkv, CB_SWA_K, qkv, CB_SWA_V, dmix, NA_HEADS,
                                            bias=swa_mask, sink=sv["sink"], name="swa_bwd")
        dq_c, dk_c, dv_c, dq2_c, dk2_c, got = _attn_bwd(
            cfg_mla, qmb, 0, kvm, 0, kvm, MLA_HEADS, dmix, NA_HEADS + SWA_HEADS, q2=qmb, q2_cb=MLA_HEADS, k2=qkv,
            k2_cb=CB_KPE, rides=[_Scatter(parts[("ffn", l)])], o=sv["mix"], lse=sv["lse_mla"], name="mla_bwd")
        landed[("ffn", l)] = got[0]
        rides = [_Scatter(parts[("rest", l + 1)])] if rest_above else []
        gqa_out = _attn_bwd(gqa_bwd_cfg(cfg_gqa), qkv, CB_GQA_Q, qkv, CB_GQA_K, qkv, CB_GQA_V, dmix,
                            NA_HEADS + SWA_HEADS + MLA_HEADS, rides=rides, o=sv["mix"], lse=sv["lse_gqa"], name="gqa_bwd")
        dq_d, dk_d, dv_d = gqa_out[:3]
        if rest_above:
            landed[("rest", l + 1)] = gqa_out[3][0]
        dqm = _pe_rope(jnp.concatenate([dq_c, dq2_c], axis=1), tabs, S, -1.0, BF16, name="mla_q_rope_bwd")
        dkvm = jnp.concatenate([dk_c, dv_c], axis=1).astype(BF16)
        dcq = _mm(dqm, w_uq_p[l], mode="nt", name="mla_uq_dx")
        dckv = _mm(dkvm, w_ukv_p[l], mode="nt", name="mla_ukv_dx")
        cqn = qkv[:, CB_CQ * LANE:CB_CKV * LANE]
        ckvn = qkv[:, CB_CKV * LANE:(CB_CKV + 1) * LANE]
        d_uq = _unperm_w_uq(_mm(cqn, dqm, mode="tn", name="mla_uq_dw"))
        d_ukv = _unperm_w_ukv(_mm(ckvn, dkvm, mode="tn", name="mla_ukv_dw"))
        grads = {}
        for h in range(NA_HEADS):
            grads[CB_NA_Q + h], grads[CB_NA_K + h], grads[CB_NA_V + h] = (dq_a, h), (dk_a, h), (dv_a, h)
        for h in range(SWA_HEADS):
            grads[CB_SWA_Q + h] = (dq_b, h)
        for h in range(SWA_KV_HEADS):
            grads[CB_SWA_K + h], grads[CB_SWA_V + h] = (dk_b, h), (dv_b, h)
        for h in range(GQA_HEADS):
            grads[CB_GQA_Q + h] = (dq_d, h)
        for h in range(GQA_KV_HEADS):
            grads[CB_GQA_K + h], grads[CB_GQA_V + h] = (dk_d, h), (dv_d, h)
        grads[CB_KPE], grads[CB_CQ], grads[CB_CKV] = (dk2_c, 0), (dcq, 0), (dckv, 0)
        dp, acc_p = _prep_bwd(sv["p"], grads, tabs, gq, gk, mq, mkv, S, name="prep_bwd")
        dh1 = _mm(dp, w_in_p[l], name="in_proj_dx")
        d_in = _unperm_w_in(_mm(dp, sv["h1"], mode="tn", name="in_proj_dw")).reshape(4, IN_COLS // 4, D)
        dx, acc_m1 = _mod_bwd(sv["x"], dh1, md[1], dxa, S, name="mod_bwd")

        to4 = lambda g: g.reshape(g.shape[0], 4, g.shape[1] // 4).transpose(1, 0, 2)
        wgrad[l]["w_in"], wgrad[l]["mla_w_uq"], wgrad[l]["mla_w_ukv"] = d_in, to4(d_uq), to4(d_ukv)
        dmod[l] = jnp.stack([acc_m1[0:2], acc_m1[2:4], acc_r1[0:2], acc_m2[0:2], acc_m2[2:4], acc_r2[0:2]])
        rpb_vjp = jax.vjp(lambda r: _na_bias(r, S), w["na_rpb"][l])[1]
        small["na_rpb"][l] = rpb_vjp(dbias)[0]
        small["swa_sink"][l] = dsink[:, :, 0].reshape(SWA_KV_HEADS, -1, BQ).sum(axis=-1).reshape(-1)
        small["gqa_q_norm"][l], small["gqa_k_norm"][l] = acc_p[0, :LANE], acc_p[1, :LANE]
        small["mla_q_norm"][l], small["mla_kv_norm"][l] = acc_p[2], acc_p[3, :LANE]
        small["ln1_g"][l], small["ln1_b"][l] = acc_r1[2], acc_r1[3]
        small["ln2_g"][l], small["ln2_b"][l] = acc_r2[2], acc_r2[3]
        small["ffn_conv_w"][l], small["ffn_conv_b"][l] = acc_f[0:3], acc_f[3]
    grad_x = dx[:S][None]

    dmod_x = jnp.stack([dmod[l][:, 0].reshape(-1) for l in range(L)])
    dmod_c = jnp.stack([dmod[l][:, 1].reshape(-1) for l in range(L)])
    small_names = tuple(small)
    bucket = [dmod_x, dmod_c] + [jnp.stack(small[n]) for n in small_names]
    (b8,) = _allgather8([_pack(bucket)], name="gather_small")
    tot = _unpack(_sum_lead(b8, F32, name="sum_small"), bucket)
    dmod_x_all = b8.reshape(8, -1)[:, :dmod_x.size].reshape(8, L, 6 * D)
    dmod_c_tot = tot[1]
    g_small = dict(zip(small_names, tot[2:]))
    g_small["b_ada"] = tot[0] + dmod_c_tot
    g_small["ffn_conv_w"] = lax.dynamic_slice_in_dim(g_small["ffn_conv_w"], chip * (F // 4), F // 4, axis=2)

    dmod16 = jnp.concatenate([dmod_x_all, jnp.broadcast_to(dmod_c_tot[None], (8, L, 6 * D))], axis=0) * row_keep[:, :, None]
    dmod16 = lax.dynamic_slice_in_dim(dmod16, chip * n_ada, n_ada, axis=2)
    g_ada, dsc = None, None
    for l in range(L):
        g_ada = _mm(sc, dmod16[:, l], mode="tn", exact=True, stack=(L, l, g_ada), name="ada_dw")
        dsc = _mm(dmod16[:, l], w["w_ada"], b_layer=l, mode="nt", add=dsc, name="ada_dx" if dsc is None else "ada_dx_add")
    (dsc8,) = _allgather8([dsc[8:16]], name="gather_dsc")
    dsc4 = dsc8[::2, 0]
    g_small["c_ctx"] = (((dsc4[0] + dsc4[1]) + dsc4[2]) + dsc4[3]) * _silu_grad(w["c_ctx"])

    sums = {}

    def sum_group(group, l):
        for n, p, got in zip(groups[group], parts[(group, l)], landed[(group, l)]):
            sums[n] = _sum_parts(p, got, chip, ac, (L, l, sums.get(n)), name="rs_chip_sum")

    add_halves("rest", 0, _run_rides([_Exchange(halves_of("rest", 0))], name="rs_core_exchange")[0])
    for l in range(L):
        sum_group("ffn", l)
        if l > 0:
            sum_group("rest", l)
    grad, delta, new_m, new_v = {}, {}, {}, {}
    (delta["w_ada"], new_m["w_ada"], new_v["w_ada"]), got = _adam(
        w["w_ada"], g_ada, m_in["w_ada"], v_in["w_ada"],
        rides=[_Scatter(parts[("rest", 0)]), _Join([sums[n] for n in groups["ffn"]])], name="adam")
    landed[("rest", 0)] = got[0]
    joined = dict(zip(groups["ffn"], got[1]))
    sum_group("rest", 0)
    joined.update(zip(groups["rest"], _run_rides([_Join([sums[n] for n in groups["rest"]])], name="rs_join")[0]))
    g_big = {n: j.reshape(L, 2 * j.shape[2], j.shape[3]) for n, j in joined.items()}
    g_big["w_ada"] = g_ada

    grad["w_ada"] = g_ada
    for n in _BIG:
        delta[n], new_m[n], new_v[n], grad[n] = _adam(w[n], g_big[n], m_in[n], v_in[n], emit_grad=True, name="adam")
    like = [w[n] for n in _SMALL]
    packed = [_pack([src[n].reshape(w[n].shape) for n in _SMALL]) for src in (w, g_small, m_in, v_in)]
    d_s, m_s, v_s = _adam(*packed, name="adam_small")
    for n, g_, d_, m_, v_ in zip(_SMALL, _unpack(packed[1], like), _unpack(d_s, like), _unpack(m_s, like), _unpack(v_s, like)):
        grad[n], delta[n], new_m[n], new_v[n] = g_, d_, m_, v_

    for d in (grad, delta, new_m, new_v):
        d["w_in"] = jnp.swapaxes(d["w_in"], 1, 2)
    return (loss, grad_x, *[grad[n] for n in _WEIGHTS], *[delta[n] for n in _WEIGHTS],
            *[new_m[n] for n in _WEIGHTS], *[new_v[n] for n in _WEIGHTS])


def kernel(x, c, ctx, c_ctx, w_ada, b_ada, w_in, na_rpb, swa_sink, mla_q_norm, mla_kv_norm, mla_w_uq, mla_w_ukv, gqa_q_norm, gqa_k_norm, w_out, ln1_g, ln1_b, ffn_w_gate, ffn_w_up, ffn_conv_w, ffn_conv_b, ffn_w_down, ln2_g, ln2_b, loss_target, m_c_ctx, m_w_ada, m_b_ada, m_w_in, m_na_rpb, m_swa_sink, m_mla_q_norm, m_mla_kv_norm, m_mla_w_uq, m_mla_w_ukv, m_gqa_q_norm, m_gqa_k_norm, m_w_out, m_ln1_g, m_ln1_b, m_ffn_w_gate, m_ffn_w_up, m_ffn_conv_w, m_ffn_conv_b, m_ffn_w_down, m_ln2_g, m_ln2_b, v_c_ctx, v_w_ada, v_b_ada, v_w_in, v_na_rpb, v_swa_sink, v_mla_q_norm, v_mla_kv_norm, v_mla_w_uq, v_mla_w_ukv, v_gqa_q_norm, v_gqa_k_norm, v_w_out, v_ln1_g, v_ln1_b, v_ffn_w_gate, v_ffn_w_up, v_ffn_conv_w, v_ffn_conv_b, v_ffn_w_down, v_ln2_g, v_ln2_b):
    args = locals()
    w = {n: args[n] for n in _WEIGHTS}
    m_in = {n: args["m_" + n] for n in _WEIGHTS}
    v_in = {n: args["v_" + n] for n in _WEIGHTS}
    return _train_step(x, c, ctx, loss_target, w, m_in, v_in)
```
